```python
import math
import jax, jax.numpy as jnp
from jax import lax
import numpy as np

D_MODEL = 1024
BATCH = 8
SEQ = 8192
DEPTH = 1

CONV_A_WIDTH = 512
CONV_A_K = 3
DN_HEADS = 8
DN_DK = 128
DN_DV = 128
DN_CONV_K = 5
CHUNK = 64
DN_QK = DN_HEADS * DN_DK
DN_VW = DN_HEADS * DN_DV
N_BRANCH = 2
D_FF = 2816
N_MOD = 9
EPS = 1e-6

SPLIT_SIZES = (
    CONV_A_WIDTH,
    CONV_A_WIDTH,
    CONV_A_WIDTH,
    2 * DN_QK + DN_VW,
    DN_VW,
    4 * DN_HEADS,
    N_BRANCH * D_MODEL,
)
IN_COLS = sum(SPLIT_SIZES)
SPLIT_OFFSETS = tuple(int(v) for v in np.cumsum(SPLIT_SIZES)[:-1])

kernel_name = "bidir_hybrid_conv_gdn_macaron_adaln"


def _rmsnorm(x, w):
    xf = x.astype(jnp.float32)
    y = xf * lax.rsqrt(jnp.mean(xf * xf, axis=-1, keepdims=True) + EPS)
    return (y * w.astype(jnp.float32)).astype(x.dtype)


def _l2norm(x):
    return x * lax.rsqrt(jnp.sum(x * x, axis=-1, keepdims=True) + EPS)


def _modulate(x, shift, scale):
    return x * (1.0 + scale) + shift


def _swiglu(x, w_up, w_down):
    a, b = jnp.split(x @ w_up, 2, axis=-1)
    return (jax.nn.silu(a) * b) @ w_down


def _dwconv(x, w):
    k_taps = w.shape[0]
    pad = k_taps // 2
    s = x.shape[1]
    xp = jnp.pad(x, ((0, 0), (pad, pad), (0, 0)))
    y = xp[:, 0:s] * w[0]
    for i in range(1, k_taps):
        y = y + xp[:, i:i + s] * w[i]
    return y


def _chunk_gated_delta(q, k, v, beta, g):
    bn, s, h, dk = q.shape
    dv = v.shape[-1]
    n = s // CHUNK

    def to_chunks(t):
        t = t.reshape((bn, n, CHUNK, h) + t.shape[3:])
        return jnp.moveaxis(t, 3, 1)

    q, k, v, beta, g = (to_chunks(t) for t in (q, k, v, beta, g))
    g = jnp.cumsum(g, axis=-1)
    idx = jnp.arange(CHUNK)
    incl = idx[:, None] >= idx[None, :]
    strict = idx[:, None] > idx[None, :]
    decay = jnp.exp(jnp.where(incl, g[..., :, None] - g[..., None, :], -jnp.inf))
    k_beta = k * beta[..., None]
    a_mat = jnp.where(strict, jnp.einsum('bhnid,bhnjd->bhnij', k_beta, k) * decay, 0.0)
    a_mat = a_mat + jnp.eye(CHUNK, dtype=q.dtype)
    rhs = jnp.concatenate([v * beta[..., None], k_beta * jnp.exp(g)[..., None]], axis=-1)
    sol = lax.linalg.triangular_solve(a_mat, rhs, left_side=True, lower=True)
    u, w = sol[..., :dv], sol[..., dv:]
    attn = jnp.einsum('bhnid,bhnjd->bhnij', q, k) * decay

    xs = tuple(jnp.moveaxis(t, 2, 0) for t in (q, k, u, w, g, attn))

    def step(state, inp):
        q_c, k_c, u_c, w_c, g_c, attn_c = inp
        v_new = u_c - jnp.einsum('bhcd,bhde->bhce', w_c, state)
        o_c = (jnp.einsum('bhcd,bhde->bhce', q_c * jnp.exp(g_c)[..., None], state)
               + jnp.einsum('bhij,bhje->bhie', attn_c, v_new))
        g_last = g_c[..., -1:]
        state = (state * jnp.exp(g_last)[..., None]
                 + jnp.einsum('bhcd,bhce->bhde', k_c * jnp.exp(g_last - g_c)[..., None], v_new))
        return state, o_c

    state0 = jnp.zeros((bn, h, dk, dv), q.dtype)
    _, o = lax.scan(step, state0, xs)
    return o.transpose(1, 0, 3, 2, 4).reshape(bn, s, h, dv)


def _hybrid_mixer(u, w_in, conv_a, conv_dn, a_log_fwd, dt_bias_fwd, a_log_bwd, dt_bias_bwd,
                  dn_norm, w_a_out, w_b_out, w_out):
    bn, s, _ = u.shape
    f32 = jnp.float32
    proj = u @ w_in
    ca_b, ca_c, ca_v, dn_qkv, dn_z, dn_ba, gate_logits = jnp.split(proj, SPLIT_OFFSETS, axis=-1)

    y_a = ca_b * _dwconv(ca_c * ca_v, conv_a)

    qkv = jax.nn.silu(_dwconv(dn_qkv, conv_dn))
    q, k, v = jnp.split(qkv, [DN_QK, 2 * DN_QK], axis=-1)
    q = (_l2norm(q.reshape(bn, s, DN_HEADS, DN_DK).astype(f32)) * (DN_DK ** -0.5))
    k = _l2norm(k.reshape(bn, s, DN_HEADS, DN_DK).astype(f32))
    v = v.reshape(bn, s, DN_HEADS, DN_DV).astype(f32)
    b_f, b_b, al_f, al_b = jnp.split(dn_ba.astype(f32), 4, axis=-1)
    beta_f, beta_b = jax.nn.sigmoid(b_f), jax.nn.sigmoid(b_b)
    g_f = -jnp.exp(a_log_fwd.astype(f32)) * jax.nn.softplus(al_f + dt_bias_fwd.astype(f32))
    g_b = -jnp.exp(a_log_bwd.astype(f32)) * jax.nn.softplus(al_b + dt_bias_bwd.astype(f32))
    o_fwd = _chunk_gated_delta(q, k, v, beta_f, g_f)
    rev = lambda t: t[:, ::-1]
    o_bwd = rev(_chunk_gated_delta(rev(q), rev(k), rev(v), rev(beta_b), rev(g_b)))
    o = _rmsnorm(o_fwd + o_bwd, dn_norm) * jax.nn.silu(dn_z.reshape(bn, s, DN_HEADS, DN_DV).astype(f32))
    y_b = o.reshape(bn, s, DN_VW).astype(u.dtype)

    gate_a, gate_b = jnp.split(gate_logits, N_BRANCH, axis=-1)
    merged = jax.nn.sigmoid(gate_a) * (y_a @ w_a_out) + jax.nn.sigmoid(gate_b) * (y_b @ w_b_out)
    return merged @ w_out


def _fwd_setup_inputs(seed: int = 0) -> dict:
    key = jax.random.key(seed)
    ks = jax.random.split(key, 24)
    L = DEPTH
    f32 = jnp.float32

    def nrm(k, shape, fan_in):
        return jax.random.normal(k, shape, f32) * (fan_in ** -0.5)

    def gain(k, shape):
        return 1.0 + 0.02 * jax.random.normal(k, shape, f32)

    def a_log(k):
        return jnp.log(jax.random.uniform(k, (L, DN_HEADS), f32, minval=1.0, maxval=16.0))

    def dt_bias(k):
        dt = jnp.exp(jax.random.uniform(k, (L, DN_HEADS), f32,
                                        minval=math.log(1e-3), maxval=math.log(1e-1)))
        return dt + jnp.log(-jnp.expm1(-dt))

    return {
        "x": jax.random.normal(ks[0], (BATCH, SEQ, D_MODEL), f32),
        "c": jax.random.normal(ks[1], (BATCH, D_MODEL), f32),
        "w_ada": nrm(ks[2], (L, D_MODEL, N_MOD * D_MODEL), D_MODEL),
        "b_ada": 0.02 * jax.random.normal(ks[3], (L, N_MOD * D_MODEL), f32),
        "norm_ffn1": gain(ks[4], (L, D_MODEL)),
        "w_ffn1_up": nrm(ks[5], (L, D_MODEL, 2 * D_FF), D_MODEL),
        "w_ffn1_down": nrm(ks[6], (L, D_FF, D_MODEL), D_FF),
        "norm_mix": gain(ks[7], (L, D_MODEL)),
        "w_in": nrm(ks[8], (L, D_MODEL, IN_COLS), D_MODEL),
        "conv_a": nrm(ks[9], (L, CONV_A_K, CONV_A_WIDTH), CONV_A_K),
        "conv_dn": nrm(ks[10], (L, DN_CONV_K, 2 * DN_QK + DN_VW), DN_CONV_K),
        "a_log_fwd": a_log(ks[11]),
        "dt_bias_fwd": dt_bias(ks[12]),
        "a_log_bwd": a_log(ks[13]),
        "dt_bias_bwd": dt_bias(ks[14]),
        "dn_norm": gain(ks[15], (L, DN_DV)),
        "w_a_out": nrm(ks[16], (L, CONV_A_WIDTH, D_MODEL), CONV_A_WIDTH),
        "w_b_out": nrm(ks[17], (L, DN_VW, D_MODEL), DN_VW),
        "w_out": nrm(ks[18], (L, D_MODEL, D_MODEL), D_MODEL),
        "norm_ffn2": gain(ks[19], (L, D_MODEL)),
        "w_ffn2_up": nrm(ks[20], (L, D_MODEL, 2 * D_FF), D_MODEL),
        "w_ffn2_down": nrm(ks[21], (L, D_FF, D_MODEL), D_FF),
        "norm_final": gain(ks[22], (D_MODEL,)),
    }


def _fwd_reference(x, c, w_ada, b_ada, norm_ffn1, w_ffn1_up, w_ffn1_down, norm_mix, w_in, conv_a,
              conv_dn, a_log_fwd, dt_bias_fwd, a_log_bwd, dt_bias_bwd, dn_norm, w_a_out, w_b_out,
              w_out, norm_ffn2, w_ffn2_up, w_ffn2_down, norm_final):
    h = x
    c_act = jax.nn.silu(c)
    for l in range(DEPTH):
        mod = c_act @ w_ada[l] + b_ada[l]
        sh1, sc1, g1, sh2, sc2, g2, sh3, sc3, g3 = [m[:, None, :] for m in jnp.split(mod, N_MOD, axis=-1)]
        u = _modulate(_rmsnorm(h, norm_ffn1[l]), sh1, sc1)
        h = h + 0.5 * g1 * _swiglu(u, w_ffn1_up[l], w_ffn1_down[l])
        u = _modulate(_rmsnorm(h, norm_mix[l]), sh2, sc2)
        h = h + g2 * _hybrid_mixer(u, w_in[l], conv_a[l], conv_dn[l], a_log_fwd[l], dt_bias_fwd[l],
                                   a_log_bwd[l], dt_bias_bwd[l], dn_norm[l], w_a_out[l],
                                   w_b_out[l], w_out[l])
        u = _modulate(_rmsnorm(h, norm_ffn2[l]), sh3, sc3)
        h = h + 0.5 * g3 * _swiglu(u, w_ffn2_up[l], w_ffn2_down[l])
    return _rmsnorm(h, norm_final)


import jax as _jax
import jax.numpy as _jnp

TWIN_FORMAT = 'train_step'
FWD_PARAMS = ['x', 'c', 'w_ada', 'b_ada', 'norm_ffn1', 'w_ffn1_up', 'w_ffn1_down', 'norm_mix', 'w_in', 'conv_a', 'conv_dn', 'a_log_fwd', 'dt_bias_fwd', 'a_log_bwd', 'dt_bias_bwd', 'dn_norm', 'w_a_out', 'w_b_out', 'w_out', 'norm_ffn2', 'w_ffn2_up', 'w_ffn2_down', 'norm_final']
TWIN_WEIGHTS = ['w_ada', 'b_ada', 'norm_ffn1', 'w_ffn1_up', 'w_ffn1_down', 'norm_mix', 'w_in', 'conv_a', 'conv_dn', 'a_log_fwd', 'dt_bias_fwd', 'a_log_bwd', 'dt_bias_bwd', 'dn_norm', 'w_a_out', 'w_b_out', 'w_out', 'norm_ffn2', 'w_ffn2_up', 'w_ffn2_down', 'norm_final']
TWIN_DIFF_INPUT = 'x'
TWIN_INPUTS = ['x', 'c', 'w_ada', 'b_ada', 'norm_ffn1', 'w_ffn1_up', 'w_ffn1_down', 'norm_mix', 'w_in', 'conv_a', 'conv_dn', 'a_log_fwd', 'dt_bias_fwd', 'a_log_bwd', 'dt_bias_bwd', 'dn_norm', 'w_a_out', 'w_b_out', 'w_out', 'norm_ffn2', 'w_ffn2_up', 'w_ffn2_down', 'norm_final', 'loss_target', 'm_w_ada', 'm_b_ada', 'm_norm_ffn1', 'm_w_ffn1_up', 'm_w_ffn1_down', 'm_norm_mix', 'm_w_in', 'm_conv_a', 'm_conv_dn', 'm_a_log_fwd', 'm_dt_bias_fwd', 'm_a_log_bwd', 'm_dt_bias_bwd', 'm_dn_norm', 'm_w_a_out', 'm_w_b_out', 'm_w_out', 'm_norm_ffn2', 'm_w_ffn2_up', 'm_w_ffn2_down', 'm_norm_final', 'v_w_ada', 'v_b_ada', 'v_norm_ffn1', 'v_w_ffn1_up', 'v_w_ffn1_down', 'v_norm_mix', 'v_w_in', 'v_conv_a', 'v_conv_dn', 'v_a_log_fwd', 'v_dt_bias_fwd', 'v_a_log_bwd', 'v_dt_bias_bwd', 'v_dn_norm', 'v_w_a_out', 'v_w_b_out', 'v_w_out', 'v_norm_ffn2', 'v_w_ffn2_up', 'v_w_ffn2_down', 'v_norm_final']
TWIN_OUTPUTS = ['loss', 'grad_x', 'grad_w_ada', 'grad_b_ada', 'grad_norm_ffn1', 'grad_w_ffn1_up', 'grad_w_ffn1_down', 'grad_norm_mix', 'grad_w_in', 'grad_conv_a', 'grad_conv_dn', 'grad_a_log_fwd', 'grad_dt_bias_fwd', 'grad_a_log_bwd', 'grad_dt_bias_bwd', 'grad_dn_norm', 'grad_w_a_out', 'grad_w_b_out', 'grad_w_out', 'grad_norm_ffn2', 'grad_w_ffn2_up', 'grad_w_ffn2_down', 'grad_norm_final', 'delta_w_ada', 'delta_b_ada', 'delta_norm_ffn1', 'delta_w_ffn1_up', 'delta_w_ffn1_down', 'delta_norm_mix', 'delta_w_in', 'delta_conv_a', 'delta_conv_dn', 'delta_a_log_fwd', 'delta_dt_bias_fwd', 'delta_a_log_bwd', 'delta_dt_bias_bwd', 'delta_dn_norm', 'delta_w_a_out', 'delta_w_b_out', 'delta_w_out', 'delta_norm_ffn2', 'delta_w_ffn2_up', 'delta_w_ffn2_down', 'delta_norm_final', 'new_m_w_ada', 'new_m_b_ada', 'new_m_norm_ffn1', 'new_m_w_ffn1_up', 'new_m_w_ffn1_down', 'new_m_norm_mix', 'new_m_w_in', 'new_m_conv_a', 'new_m_conv_dn', 'new_m_a_log_fwd', 'new_m_dt_bias_fwd', 'new_m_a_log_bwd', 'new_m_dt_bias_bwd', 'new_m_dn_norm', 'new_m_w_a_out', 'new_m_w_b_out', 'new_m_w_out', 'new_m_norm_ffn2', 'new_m_w_ffn2_up', 'new_m_w_ffn2_down', 'new_m_norm_final', 'new_v_w_ada', 'new_v_b_ada', 'new_v_norm_ffn1', 'new_v_w_ffn1_up', 'new_v_w_ffn1_down', 'new_v_norm_mix', 'new_v_w_in', 'new_v_conv_a', 'new_v_conv_dn', 'new_v_a_log_fwd', 'new_v_dt_bias_fwd', 'new_v_a_log_bwd', 'new_v_dt_bias_bwd', 'new_v_dn_norm', 'new_v_w_a_out', 'new_v_w_b_out', 'new_v_w_out', 'new_v_norm_ffn2', 'new_v_w_ffn2_up', 'new_v_w_ffn2_down', 'new_v_norm_final']
TWIN_LEAF_KINDS = {'loss': 'loss', 'grad_x': 'grad_x', 'grad_w_ada': 'grad_w', 'grad_b_ada': 'grad_w', 'grad_norm_ffn1': 'grad_w', 'grad_w_ffn1_up': 'grad_w', 'grad_w_ffn1_down': 'grad_w', 'grad_norm_mix': 'grad_w', 'grad_w_in': 'grad_w', 'grad_conv_a': 'grad_w', 'grad_conv_dn': 'grad_w', 'grad_a_log_fwd': 'grad_w', 'grad_dt_bias_fwd': 'grad_w', 'grad_a_log_bwd': 'grad_w', 'grad_dt_bias_bwd': 'grad_w', 'grad_dn_norm': 'grad_w', 'grad_w_a_out': 'grad_w', 'grad_w_b_out': 'grad_w', 'grad_w_out': 'grad_w', 'grad_norm_ffn2': 'grad_w', 'grad_w_ffn2_up': 'grad_w', 'grad_w_ffn2_down': 'grad_w', 'grad_norm_final': 'grad_w', 'delta_w_ada': 'delta_w', 'delta_b_ada': 'delta_w', 'delta_norm_ffn1': 'delta_w', 'delta_w_ffn1_up': 'delta_w', 'delta_w_ffn1_down': 'delta_w', 'delta_norm_mix': 'delta_w', 'delta_w_in': 'delta_w', 'delta_conv_a': 'delta_w', 'delta_conv_dn': 'delta_w', 'delta_a_log_fwd': 'delta_w', 'delta_dt_bias_fwd': 'delta_w', 'delta_a_log_bwd': 'delta_w', 'delta_dt_bias_bwd': 'delta_w', 'delta_dn_norm': 'delta_w', 'delta_w_a_out': 'delta_w', 'delta_w_b_out': 'delta_w', 'delta_w_out': 'delta_w', 'delta_norm_ffn2': 'delta_w', 'delta_w_ffn2_up': 'delta_w', 'delta_w_ffn2_down': 'delta_w', 'delta_norm_final': 'delta_w', 'new_m_w_ada': 'new_m', 'new_m_b_ada': 'new_m', 'new_m_norm_ffn1': 'new_m', 'new_m_w_ffn1_up': 'new_m', 'new_m_w_ffn1_down': 'new_m', 'new_m_norm_mix': 'new_m', 'new_m_w_in': 'new_m', 'new_m_conv_a': 'new_m', 'new_m_conv_dn': 'new_m', 'new_m_a_log_fwd': 'new_m', 'new_m_dt_bias_fwd': 'new_m', 'new_m_a_log_bwd': 'new_m', 'new_m_dt_bias_bwd': 'new_m', 'new_m_dn_norm': 'new_m', 'new_m_w_a_out': 'new_m', 'new_m_w_b_out': 'new_m', 'new_m_w_out': 'new_m', 'new_m_norm_ffn2': 'new_m', 'new_m_w_ffn2_up': 'new_m', 'new_m_w_ffn2_down': 'new_m', 'new_m_norm_final': 'new_m', 'new_v_w_ada': 'new_v', 'new_v_b_ada': 'new_v', 'new_v_norm_ffn1': 'new_v', 'new_v_w_ffn1_up': 'new_v', 'new_v_w_ffn1_down': 'new_v', 'new_v_norm_mix': 'new_v', 'new_v_w_in': 'new_v', 'new_v_conv_a': 'new_v', 'new_v_conv_dn': 'new_v', 'new_v_a_log_fwd': 'new_v', 'new_v_dt_bias_fwd': 'new_v', 'new_v_a_log_bwd': 'new_v', 'new_v_dt_bias_bwd': 'new_v', 'new_v_dn_norm': 'new_v', 'new_v_w_a_out': 'new_v', 'new_v_w_b_out': 'new_v', 'new_v_w_out': 'new_v', 'new_v_norm_ffn2': 'new_v', 'new_v_w_ffn2_up': 'new_v', 'new_v_w_ffn2_down': 'new_v', 'new_v_norm_final': 'new_v'}


def _forward(args):
    return _fwd_reference(*[args[k] for k in FWD_PARAMS])


def _output_shape():
    def fwd():
        inp = _fwd_setup_inputs(0)
        return _fwd_reference(*[inp[k] for k in FWD_PARAMS])
    out = _jax.eval_shape(fwd)
    return out.shape, out.dtype

N_MICROBATCH = 1
ADAM_LR = 0.001
ADAM_B1 = 0.9
ADAM_B2 = 0.999
ADAM_EPS = 1e-08
ADAM_WD = 0.01
ADAM_STEP = 10
PER_EXAMPLE_BATCH_AXIS = {'x': 0, 'c': 0, 'loss_target': 0}
SHARED_INPUTS = []
_WEIGHT_DTYPES = {'w_ada': _jnp.float32, 'b_ada': _jnp.float32, 'norm_ffn1': _jnp.float32, 'w_ffn1_up': _jnp.float32, 'w_ffn1_down': _jnp.float32, 'norm_mix': _jnp.float32, 'w_in': _jnp.float32, 'conv_a': _jnp.float32, 'conv_dn': _jnp.float32, 'a_log_fwd': _jnp.float32, 'dt_bias_fwd': _jnp.float32, 'a_log_bwd': _jnp.float32, 'dt_bias_bwd': _jnp.float32, 'dn_norm': _jnp.float32, 'w_a_out': _jnp.float32, 'w_b_out': _jnp.float32, 'w_out': _jnp.float32, 'norm_ffn2': _jnp.float32, 'w_ffn2_up': _jnp.float32, 'w_ffn2_down': _jnp.float32, 'norm_final': _jnp.float32}
MOMENT_SCALE = {'w_ada': 1.169723e-01, 'b_ada': 2.142657e-01, 'norm_ffn1': 1.379392e-01, 'w_ffn1_up': 5.987660e-02, 'w_ffn1_down': 9.732249e-02, 'norm_mix': 2.863786e-01, 'w_in': 1.044469e-01, 'conv_a': 2.032358e-01, 'conv_dn': 3.507945e-02, 'a_log_fwd': 1.001099e-01, 'dt_bias_fwd': 9.826544e-02, 'a_log_bwd': 8.890838e-02, 'dt_bias_bwd': 8.714396e-02, 'dn_norm': 1.528880e-01, 'w_a_out': 1.532107e-01, 'w_b_out': 5.328551e-02, 'w_out': 1.629713e-01, 'norm_ffn2': 7.614220e-02, 'w_ffn2_up': 3.606321e-02, 'w_ffn2_down': 5.891253e-02, 'norm_final': 6.526517e+01}


def _to_microbatches(a, axis):
    t = _jnp.moveaxis(a, axis, 0)
    t = t.reshape((N_MICROBATCH, t.shape[0] // N_MICROBATCH) + t.shape[1:])
    return _jnp.moveaxis(t, 1, axis + 1)


def setup_inputs(seed: int = 0) -> dict:
    inp = _fwd_setup_inputs(seed)
    key = _jax.random.fold_in(_jax.random.key(seed), 7919)
    shape, _ = _output_shape()
    out = dict(inp)
    out["loss_target"] = _jax.random.normal(_jax.random.fold_in(key, 0), shape, _jnp.float32)
    for i, name in enumerate(TWIN_WEIGHTS):
        w = inp[name].astype(_jnp.float32)
        if MOMENT_SCALE is None:
            s = _jnp.sqrt(_jnp.mean(_jnp.square(w)) + 1e-30)
        else:
            s = MOMENT_SCALE[name]
        km, kv = _jax.random.split(_jax.random.fold_in(key, i + 1))
        out[name] = w
        out["m_" + name] = s * _jax.random.normal(km, w.shape, _jnp.float32)
        out["v_" + name] = (s * s) * _jax.random.uniform(kv, w.shape, _jnp.float32, 0.5, 1.5)
    if N_MICROBATCH > 1:
        for name, axis in PER_EXAMPLE_BATCH_AXIS.items():
            out[name] = _to_microbatches(out[name], axis)
    return {'x': out['x'], 'c': out['c'], 'w_ada': out['w_ada'], 'b_ada': out['b_ada'], 'norm_ffn1': out['norm_ffn1'], 'w_ffn1_up': out['w_ffn1_up'], 'w_ffn1_down': out['w_ffn1_down'], 'norm_mix': out['norm_mix'], 'w_in': out['w_in'], 'conv_a': out['conv_a'], 'conv_dn': out['conv_dn'], 'a_log_fwd': out['a_log_fwd'], 'dt_bias_fwd': out['dt_bias_fwd'], 'a_log_bwd': out['a_log_bwd'], 'dt_bias_bwd': out['dt_bias_bwd'], 'dn_norm': out['dn_norm'], 'w_a_out': out['w_a_out'], 'w_b_out': out['w_b_out'], 'w_out': out['w_out'], 'norm_ffn2': out['norm_ffn2'], 'w_ffn2_up': out['w_ffn2_up'], 'w_ffn2_down': out['w_ffn2_down'], 'norm_final': out['norm_final'], 'loss_target': out['loss_target'], 'm_w_ada': out['m_w_ada'], 'm_b_ada': out['m_b_ada'], 'm_norm_ffn1': out['m_norm_ffn1'], 'm_w_ffn1_up': out['m_w_ffn1_up'], 'm_w_ffn1_down': out['m_w_ffn1_down'], 'm_norm_mix': out['m_norm_mix'], 'm_w_in': out['m_w_in'], 'm_conv_a': out['m_conv_a'], 'm_conv_dn': out['m_conv_dn'], 'm_a_log_fwd': out['m_a_log_fwd'], 'm_dt_bias_fwd': out['m_dt_bias_fwd'], 'm_a_log_bwd': out['m_a_log_bwd'], 'm_dt_bias_bwd': out['m_dt_bias_bwd'], 'm_dn_norm': out['m_dn_norm'], 'm_w_a_out': out['m_w_a_out'], 'm_w_b_out': out['m_w_b_out'], 'm_w_out': out['m_w_out'], 'm_norm_ffn2': out['m_norm_ffn2'], 'm_w_ffn2_up': out['m_w_ffn2_up'], 'm_w_ffn2_down': out['m_w_ffn2_down'], 'm_norm_final': out['m_norm_final'], 'v_w_ada': out['v_w_ada'], 'v_b_ada': out['v_b_ada'], 'v_norm_ffn1': out['v_norm_ffn1'], 'v_w_ffn1_up': out['v_w_ffn1_up'], 'v_w_ffn1_down': out['v_w_ffn1_down'], 'v_norm_mix': out['v_norm_mix'], 'v_w_in': out['v_w_in'], 'v_conv_a': out['v_conv_a'], 'v_conv_dn': out['v_conv_dn'], 'v_a_log_fwd': out['v_a_log_fwd'], 'v_dt_bias_fwd': out['v_dt_bias_fwd'], 'v_a_log_bwd': out['v_a_log_bwd'], 'v_dt_bias_bwd': out['v_dt_bias_bwd'], 'v_dn_norm': out['v_dn_norm'], 'v_w_a_out': out['v_w_a_out'], 'v_w_b_out': out['v_w_b_out'], 'v_w_out': out['v_w_out'], 'v_norm_ffn2': out['v_norm_ffn2'], 'v_w_ffn2_up': out['v_w_ffn2_up'], 'v_w_ffn2_down': out['v_w_ffn2_down'], 'v_norm_final': out['v_norm_final']}


def _loss(weights, diff, rest, loss_target):
    with _jax.named_scope("forward"):
        args = {**rest, TWIN_DIFF_INPUT: diff, **{k: w.astype(_WEIGHT_DTYPES[k]) for k, w in weights.items()}}
        y = _forward(args)
    with _jax.named_scope("loss_head"):
        err = _jnp.square(y.astype(_jnp.float32) - loss_target)
        return 0.5 * _jnp.sum(_jnp.mean(err, axis=-1)) if err.ndim else 0.5 * err


def _adamw(w, g, m, v):
    m = ADAM_B1 * m + (1.0 - ADAM_B1) * g
    v = ADAM_B2 * v + (1.0 - ADAM_B2) * _jnp.square(g)
    m_hat = m / (1.0 - ADAM_B1 ** ADAM_STEP)
    v_hat = v / (1.0 - ADAM_B2 ** ADAM_STEP)
    delta = -ADAM_LR * (m_hat / (_jnp.sqrt(v_hat) + ADAM_EPS) + ADAM_WD * w)
    return delta, m, v


def reference(x, c, w_ada, b_ada, norm_ffn1, w_ffn1_up, w_ffn1_down, norm_mix, w_in, conv_a, conv_dn, a_log_fwd, dt_bias_fwd, a_log_bwd, dt_bias_bwd, dn_norm, w_a_out, w_b_out, w_out, norm_ffn2, w_ffn2_up, w_ffn2_down, norm_final, loss_target, m_w_ada, m_b_ada, m_norm_ffn1, m_w_ffn1_up, m_w_ffn1_down, m_norm_mix, m_w_in, m_conv_a, m_conv_dn, m_a_log_fwd, m_dt_bias_fwd, m_a_log_bwd, m_dt_bias_bwd, m_dn_norm, m_w_a_out, m_w_b_out, m_w_out, m_norm_ffn2, m_w_ffn2_up, m_w_ffn2_down, m_norm_final, v_w_ada, v_b_ada, v_norm_ffn1, v_w_ffn1_up, v_w_ffn1_down, v_norm_mix, v_w_in, v_conv_a, v_conv_dn, v_a_log_fwd, v_dt_bias_fwd, v_a_log_bwd, v_dt_bias_bwd, v_dn_norm, v_w_a_out, v_w_b_out, v_w_out, v_norm_ffn2, v_w_ffn2_up, v_w_ffn2_down, v_norm_final):
    given = dict(x=x, c=c, w_ada=w_ada, b_ada=b_ada, norm_ffn1=norm_ffn1, w_ffn1_up=w_ffn1_up, w_ffn1_down=w_ffn1_down, norm_mix=norm_mix, w_in=w_in, conv_a=conv_a, conv_dn=conv_dn, a_log_fwd=a_log_fwd, dt_bias_fwd=dt_bias_fwd, a_log_bwd=a_log_bwd, dt_bias_bwd=dt_bias_bwd, dn_norm=dn_norm, w_a_out=w_a_out, w_b_out=w_b_out, w_out=w_out, norm_ffn2=norm_ffn2, w_ffn2_up=w_ffn2_up, w_ffn2_down=w_ffn2_down, norm_final=norm_final, loss_target=loss_target, m_w_ada=m_w_ada, m_b_ada=m_b_ada, m_norm_ffn1=m_norm_ffn1, m_w_ffn1_up=m_w_ffn1_up, m_w_ffn1_down=m_w_ffn1_down, m_norm_mix=m_norm_mix, m_w_in=m_w_in, m_conv_a=m_conv_a, m_conv_dn=m_conv_dn, m_a_log_fwd=m_a_log_fwd, m_dt_bias_fwd=m_dt_bias_fwd, m_a_log_bwd=m_a_log_bwd, m_dt_bias_bwd=m_dt_bias_bwd, m_dn_norm=m_dn_norm, m_w_a_out=m_w_a_out, m_w_b_out=m_w_b_out, m_w_out=m_w_out, m_norm_ffn2=m_norm_ffn2, m_w_ffn2_up=m_w_ffn2_up, m_w_ffn2_down=m_w_ffn2_down, m_norm_final=m_norm_final, v_w_ada=v_w_ada, v_b_ada=v_b_ada, v_norm_ffn1=v_norm_ffn1, v_w_ffn1_up=v_w_ffn1_up, v_w_ffn1_down=v_w_ffn1_down, v_norm_mix=v_norm_mix, v_w_in=v_w_in, v_conv_a=v_conv_a, v_conv_dn=v_conv_dn, v_a_log_fwd=v_a_log_fwd, v_dt_bias_fwd=v_dt_bias_fwd, v_a_log_bwd=v_a_log_bwd, v_dt_bias_bwd=v_dt_bias_bwd, v_dn_norm=v_dn_norm, v_w_a_out=v_w_a_out, v_w_b_out=v_w_b_out, v_w_out=v_w_out, v_norm_ffn2=v_norm_ffn2, v_w_ffn2_up=v_w_ffn2_up, v_w_ffn2_down=v_w_ffn2_down, v_norm_final=v_norm_final)
    weights = {n: given[n] for n in TWIN_WEIGHTS}
    shared = {n: given[n] for n in SHARED_INPUTS}
    per_example = {n: given[n] for n in ['x', 'c']}
    grad_fn = _jax.value_and_grad(_loss, argnums=(0, 1))

    def one_microbatch(ex, loss_target):
        ex = dict(ex)
        diff = ex.pop(TWIN_DIFF_INPUT)
        return grad_fn(weights, diff, {**shared, **ex}, loss_target)

    if N_MICROBATCH == 1:
        loss, (grad_w, grad_x) = one_microbatch(per_example, given["loss_target"])
    else:
        def body(carry, xs):
            loss_sum, grad_sum = carry
            l_k, (gw_k, gx_k) = one_microbatch(xs[0], xs[1])
            with _jax.named_scope("update"):
                return (loss_sum + l_k, _jax.tree.map(_jnp.add, grad_sum, gw_k)), gx_k

        init = (_jnp.zeros((), _jnp.float32), _jax.tree.map(_jnp.zeros_like, weights))
        (loss, grad_w), grad_x = _jax.lax.scan(body, init, (per_example, given["loss_target"]))
    with _jax.named_scope("update"):
        delta_w, new_m, new_v = {}, {}, {}
        for n in TWIN_WEIGHTS:
            delta_w[n], new_m[n], new_v[n] = _adamw(weights[n], grad_w[n], given["m_" + n], given["v_" + n])
    return (loss, grad_x, *[grad_w[n] for n in TWIN_WEIGHTS], *[delta_w[n] for n in TWIN_WEIGHTS],
            *[new_m[n] for n in TWIN_WEIGHTS], *[new_v[n] for n in TWIN_WEIGHTS])
```

```python
import functools
import math

import jax
import jax.numpy as jnp
from jax import lax
from jax.experimental import pallas as pl
from jax.experimental.pallas import tpu as pltpu

F32 = jnp.float32
BF16 = jnp.bfloat16
EPS = 1e-6
N_DEV = 8
CHUNK = 64
HEADS = 8
HEAD_DIM = 128
MESH_AXES = ("x", "y", "c")
VMEM_LIMIT_BYTES = 56 * 1024 * 1024

ADAM_LR = 0.001
ADAM_B1 = 0.9
ADAM_B2 = 0.999
ADAM_EPS = 1e-08
ADAM_WD = 0.01
ADAM_STEP = 10


def _params(sem=None):
    return pltpu.CompilerParams(dimension_semantics=sem, vmem_limit_bytes=VMEM_LIMIT_BYTES)


def _row(n):
    return pl.BlockSpec((1, n), lambda *_: (0, 0))


def _resident(shape):
    nd = len(shape)
    return pl.BlockSpec(shape, lambda *_: (0,) * nd, pipeline_mode=pl.Buffered(1))


def _sigmoid(x):
    return 1.0 / (1.0 + jnp.exp(-x))


def _dot(a, b):
    return jnp.dot(a, b, preferred_element_type=F32)


def _dot_nt(a, b):
    return lax.dot_general(a, b, (((1,), (1,)), ((), ())), preferred_element_type=F32)


def _dot_tn(a, b):
    return lax.dot_general(a, b, (((0,), (0,)), ((), ())), preferred_element_type=F32)


def _split_bf16(x):
    hi = x.astype(BF16)
    lo = (x - hi.astype(F32)).astype(BF16)
    return hi, lo


def _dot3(a, b, dot=_dot):
    ah, al = a if isinstance(a, tuple) else _split_bf16(a)
    bh, bl = b if isinstance(b, tuple) else _split_bf16(b)
    return dot(ah, bh) + dot(ah, bl) + dot(al, bh)


def _dot_exact(a, b):
    return jnp.dot(a, b, preferred_element_type=F32, precision=lax.Precision.HIGHEST)


def _norm_mod(x, nw, sc, sh):
    r = lax.rsqrt(jnp.mean(x * x, axis=-1, keepdims=True) + EPS)
    return (x * r * nw) * (1.0 + sc) + sh


def _norm_mod_bwd(x, nw, sc, du):
    r = lax.rsqrt(jnp.mean(x * x, axis=-1, keepdims=True) + EPS)
    xhat = x * r
    n = xhat * nw
    dsh = jnp.sum(du, axis=0, keepdims=True)
    dsc = jnp.sum(du * n, axis=0, keepdims=True)
    dn = du * (1.0 + sc)
    dnw = jnp.sum(dn * xhat, axis=0, keepdims=True)
    dxhat = dn * nw
    dx = r * (dxhat - xhat * jnp.mean(dxhat * xhat, axis=-1, keepdims=True))
    return dx, dnw, dsc, dsh


def _ffn_up_fwd(h, nw, sc, sh, wup, *, name, ts=512, tn=256):
    s, d = h.shape
    f_dim = wup.shape[1] // 2
    nj = f_dim // tn

    def body(h_ref, nw_ref, sc_ref, sh_ref, wa_ref, wb_ref, u_ref, a_ref, b_ref, f_ref):
        @pl.when(pl.program_id(1) == 0)
        def _():
            u_ref[...] = _norm_mod(h_ref[...], nw_ref[...], sc_ref[...], sh_ref[...]).astype(BF16)

        u = u_ref[...]
        a = _dot(u, wa_ref[...])
        b = _dot(u, wb_ref[...])
        a_ref[...] = a.astype(BF16)
        b_ref[...] = b.astype(BF16)
        f_ref[...] = (a * _sigmoid(a) * b).astype(BF16)

    return pl.pallas_call(
        body, name=name, grid=(s // ts, nj),
        in_specs=[pl.BlockSpec((ts, d), lambda i, j: (i, 0)), _row(d), _row(d), _row(d),
                  pl.BlockSpec((d, tn), lambda i, j: (0, j)),
                  pl.BlockSpec((d, tn), lambda i, j: (0, j + nj))],
        out_specs=[pl.BlockSpec((ts, d), lambda i, j: (i, 0)),
                   pl.BlockSpec((ts, tn), lambda i, j: (i, j)),
                   pl.BlockSpec((ts, tn), lambda i, j: (i, j)),
                   pl.BlockSpec((ts, tn), lambda i, j: (i, j))],
        out_shape=[jax.ShapeDtypeStruct((s, d), BF16)] + [jax.ShapeDtypeStruct((s, f_dim), BF16)] * 3,
        compiler_params=_params(("parallel", "arbitrary")),
    )(h, nw, sc, sh, wup, wup)


def _ffn_down_fwd(f, wd, h, g, *, name, ts=512):
    s, f_dim = f.shape
    d = wd.shape[1]

    def body(f_ref, wd_ref, h_ref, g_ref, y_ref, ho_ref):
        y = _dot(f_ref[...], wd_ref[...])
        y_ref[...] = y.astype(BF16)
        ho_ref[...] = h_ref[...] + (0.5 * g_ref[...]) * y

    return pl.pallas_call(
        body, name=name, grid=(s // ts,),
        in_specs=[pl.BlockSpec((ts, f_dim), lambda i: (i, 0)), _resident((f_dim, d)),
                  pl.BlockSpec((ts, d), lambda i: (i, 0)), _row(d)],
        out_specs=[pl.BlockSpec((ts, d), lambda i: (i, 0)), pl.BlockSpec((ts, d), lambda i: (i, 0))],
        out_shape=[jax.ShapeDtypeStruct((s, d), BF16), jax.ShapeDtypeStruct((s, d), F32)],
        compiler_params=_params(("parallel",)),
    )(f, wd, h, g)


def _ffn_bwd_act(dh, g, y, a, b, wd, *, name, ts=256):
    s, d = dh.shape
    f_dim = a.shape[1]

    def body(dh_ref, g_ref, y_ref, a_ref, b_ref, wd_ref, dy_ref, dab_ref, dg_ref):
        dh_v = dh_ref[...]
        dy = ((0.5 * g_ref[...]) * dh_v).astype(BF16)
        dy_ref[...] = dy
        part = jnp.sum(0.5 * dh_v * y_ref[...].astype(F32), axis=0, keepdims=True)

        @pl.when(pl.program_id(0) == 0)
        def _():
            dg_ref[...] = jnp.zeros_like(dg_ref)

        dg_ref[...] += part
        df = _dot_nt(dy, wd_ref[...])
        av = a_ref[...].astype(F32)
        bv = b_ref[...].astype(F32)
        sg = _sigmoid(av)
        dab_ref[:, :f_dim] = (df * bv * (sg * (1.0 + av * (1.0 - sg)))).astype(BF16)
        dab_ref[:, f_dim:] = (df * (av * sg)).astype(BF16)

    return pl.pallas_call(
        body, name=name, grid=(s // ts,),
        in_specs=[pl.BlockSpec((ts, d), lambda i: (i, 0)), _row(d),
                  pl.BlockSpec((ts, d), lambda i: (i, 0)),
                  pl.BlockSpec((ts, f_dim), lambda i: (i, 0)),
                  pl.BlockSpec((ts, f_dim), lambda i: (i, 0)),
                  _resident((f_dim, d))],
        out_specs=[pl.BlockSpec((ts, d), lambda i: (i, 0)),
                   pl.BlockSpec((ts, 2 * f_dim), lambda i: (i, 0)), _row(d)],
        out_shape=[jax.ShapeDtypeStruct((s, d), BF16), jax.ShapeDtypeStruct((s, 2 * f_dim), BF16),
                   jax.ShapeDtypeStruct((1, d), F32)],
        compiler_params=_params(("arbitrary",)),
    )(dh, g, y, a, b, wd)


def _norm_mod_matmul_bwd(pairs, h, nw, sc, dh_in, *, name, ts=256):
    s, d = h.shape
    n_pairs = len(pairs)

    def body(*refs):
        dx_refs = refs[:n_pairs]
        w_refs = refs[n_pairs:2 * n_pairs]
        h_ref, nw_ref, sc_ref, dhi_ref, dho_ref, dnw_ref, dsc_ref, dsh_ref = refs[2 * n_pairs:]
        du = _dot_nt(dx_refs[0][...], w_refs[0][...])
        for k in range(1, n_pairs):
            du = du + _dot_nt(dx_refs[k][...], w_refs[k][...])
        dx, dnw, dsc, dsh = _norm_mod_bwd(h_ref[...], nw_ref[...], sc_ref[...], du)
        dho_ref[...] = dhi_ref[...] + dx

        @pl.when(pl.program_id(0) == 0)
        def _():
            dnw_ref[...] = jnp.zeros_like(dnw_ref)
            dsc_ref[...] = jnp.zeros_like(dsc_ref)
            dsh_ref[...] = jnp.zeros_like(dsh_ref)

        dnw_ref[...] += dnw
        dsc_ref[...] += dsc
        dsh_ref[...] += dsh

    dxs = [p[0] for p in pairs]
    ws = [p[1] for p in pairs]
    tile = pl.BlockSpec((ts, d), lambda i: (i, 0))
    return pl.pallas_call(
        body, name=name, grid=(s // ts,),
        in_specs=([pl.BlockSpec((ts, x.shape[1]), lambda i: (i, 0)) for x in dxs]
                  + [_resident(w.shape) for w in ws] + [tile, _row(d), _row(d), tile]),
        out_specs=[tile, _row(d), _row(d), _row(d)],
        out_shape=[jax.ShapeDtypeStruct((s, d), F32)] + [jax.ShapeDtypeStruct((1, d), F32)] * 3,
        compiler_params=_params(("arbitrary",)),
    )(*dxs, *ws, h, nw, sc, dh_in)


def _matmul_tn(a, b, *, name, tm, tn, tk=512):
    s, m = a.shape
    n = b.shape[1]
    nk = s // tk

    def body(a_ref, b_ref, o_ref, acc_ref):
        k = pl.program_id(2)

        @pl.when(k == 0)
        def _():
            acc_ref[...] = jnp.zeros_like(acc_ref)

        acc_ref[...] += _dot_tn(a_ref[...], b_ref[...])

        @pl.when(k == nk - 1)
        def _():
            o_ref[...] = acc_ref[...]

    return pl.pallas_call(
        body, name=name, grid=(m // tm, n // tn, nk),
        in_specs=[pl.BlockSpec((tk, tm), lambda i, j, k: (k, i)),
                  pl.BlockSpec((tk, tn), lambda i, j, k: (k, j))],
        out_specs=pl.BlockSpec((tm, tn), lambda i, j, k: (i, j)),
        out_shape=jax.ShapeDtypeStruct((m, n), F32),
        scratch_shapes=[pltpu.VMEM((tm, tn), F32)],
        compiler_params=_params(("parallel", "parallel", "arbitrary")),
    )(a, b)


def _in_proj_fwd(h, nw, sc, sh, w_main, w_ba, *, name, ts=512, tn=256):
    s, d = h.shape
    n_main = w_main.shape[1]
    n_ba = w_ba.shape[1]

    def body(h_ref, nw_ref, sc_ref, sh_ref, w_ref, wba_ref, u_ref, p_ref, ba_ref):
        @pl.when(pl.program_id(1) == 0)
        def _():
            u0 = _norm_mod(h_ref[...], nw_ref[...], sc_ref[...], sh_ref[...]).astype(BF16)
            u_ref[...] = u0
            ba_ref[...] = _dot(u0, wba_ref[...])

        p_ref[...] = _dot(u_ref[...], w_ref[...]).astype(BF16)

    return pl.pallas_call(
        body, name=name, grid=(s // ts, n_main // tn),
        in_specs=[pl.BlockSpec((ts, d), lambda i, j: (i, 0)), _row(d), _row(d), _row(d),
                  pl.BlockSpec((d, tn), lambda i, j: (0, j)), _resident((d, n_ba))],
        out_specs=[pl.BlockSpec((ts, d), lambda i, j: (i, 0)),
                   pl.BlockSpec((ts, tn), lambda i, j: (i, j)),
                   pl.BlockSpec((ts, n_ba), lambda i, j: (i, 0))],
        out_shape=[jax.ShapeDtypeStruct((s, d), BF16), jax.ShapeDtypeStruct((s, n_main), BF16),
                   jax.ShapeDtypeStruct((s, n_ba), F32)],
        compiler_params=_params(("parallel", "arbitrary")),
    )(h, nw, sc, sh, w_main, w_ba)


QKV_W = 3 * HEADS * HEAD_DIM
Z_OFF, Z_W = 3072, 1024
GATE_OFF, GATE_W = 4096, 2048
A_OFF, A_W = 6144, 1536
N_MAIN = 7680
CONV_A = 512
BA_W = 128

L_BETA, L_G, L_EG, L_EKD, L_EGC = 0, 8, 16, 24, 32


def _softplus(z):
    e = jnp.exp(-jnp.abs(z))
    small = e * (1.0 - e * (0.5 - e * (1.0 / 3.0)))
    return jnp.maximum(z, 0.0) + jnp.where(e < 1e-3, small, jnp.log(1.0 + e))


def _tri(n, sgn, strict=False):
    i = lax.broadcasted_iota(jnp.int32, (n, n), 0)
    j = lax.broadcasted_iota(jnp.int32, (n, n), 1)
    dlt = (i - j) * sgn
    return (dlt > 0) if strict else (dlt >= 0)


def _scal_fwd(ba, alog, dtb, *, name, ts=512):
    s = ba.shape[0]

    def body(ba_ref, al_ref, dt_ref, o_ref):
        d = pl.program_id(0)
        sgn = 1 - 2 * d
        x = ba_ref[...]
        lane = lax.broadcasted_iota(jnp.int32, x.shape, 1)
        beta = _sigmoid(x)
        g = -jnp.exp(al_ref[0]) * _softplus(x + dt_ref[0])
        g = jnp.where((lane >= L_G) & (lane < L_EGC + 8), g, 0.0)
        ltri = jnp.where(_tri(CHUNK, sgn), 1.0, 0.0).astype(F32)
        for c in range(ts // CHUNK):
            rows = slice(c * CHUNK, (c + 1) * CHUNK)
            gc = _dot_exact(ltri, g[rows])
            g_end = jnp.where(d == 0, gc[CHUNK - 1:CHUNK], gc[0:1])
            ln = lane[rows]
            out = jnp.where(ln < L_G, beta[rows],
                  jnp.where(ln < L_EG, gc,
                  jnp.where(ln < L_EKD, jnp.exp(gc),
                  jnp.where(ln < L_EGC, jnp.exp(g_end - gc),
                  jnp.where(ln < L_EGC + 8, jnp.broadcast_to(jnp.exp(g_end), gc.shape), 0.0)))))
            o_ref[0, rows, :] = out

    return pl.pallas_call(
        body, name=name, grid=(2, s // ts),
        in_specs=[pl.BlockSpec((ts, BA_W), lambda d, i: (i, d)),
                  pl.BlockSpec((1, 1, BA_W), lambda d, i: (d, 0, 0)),
                  pl.BlockSpec((1, 1, BA_W), lambda d, i: (d, 0, 0))],
        out_specs=pl.BlockSpec((1, ts, BA_W), lambda d, i: (d, i, 0)),
        out_shape=jax.ShapeDtypeStruct((2, s, BA_W), F32),
        compiler_params=_params(("parallel", "parallel")),
    )(ba, alog, dtb)


def _scal_bwd(dscal, drow, ba, alog, dtb, *, name, ts=512):
    s = ba.shape[0]

    def body(ds_ref, dr_ref, ba_ref, al_ref, dt_ref, dba_ref, dal_ref, ddt_ref):
        d = pl.program_id(0)
        sgn = 1 - 2 * d
        x = ba_ref[...]
        lane = lax.broadcasted_iota(jnp.int32, x.shape, 1)
        in_g = (lane >= L_G) & (lane < L_G + 8)
        beta = _sigmoid(x)
        z = x + dt_ref[0]
        neg_a = -jnp.exp(al_ref[0])
        g = neg_a * _softplus(z)
        dsv = ds_ref[0]
        dgc = jnp.where(in_g, dsv + dr_ref[0], 0.0)
        utri = jnp.where(_tri(CHUNK, -sgn), 1.0, 0.0).astype(F32)
        dal = jnp.zeros((1, BA_W), F32)
        ddt = jnp.zeros((1, BA_W), F32)
        for c in range(ts // CHUNK):
            rows = slice(c * CHUNK, (c + 1) * CHUNK)
            dg = _dot_exact(utri, dgc[rows])
            dz = dg * neg_a * _sigmoid(z[rows])
            dal = dal + jnp.sum(dg * g[rows], axis=0, keepdims=True)
            ddt = ddt + jnp.sum(dz, axis=0, keepdims=True)
            b = beta[rows]
            out = jnp.where(lane[rows] < L_G, dsv[rows] * b * (1.0 - b), jnp.where(in_g[rows], dz, 0.0))
            dba_ref[rows, :] = out.astype(BF16)

        @pl.when(pl.program_id(1) == 0)
        def _():
            dal_ref[...] = jnp.zeros_like(dal_ref)
            ddt_ref[...] = jnp.zeros_like(ddt_ref)

        dal_ref[0] += dal
        ddt_ref[0] += ddt

    row3 = pl.BlockSpec((1, 1, BA_W), lambda d, i: (d, 0, 0))
    tok3 = pl.BlockSpec((1, ts, BA_W), lambda d, i: (d, i, 0))
    return pl.pallas_call(
        body, name=name, grid=(2, s // ts),
        in_specs=[tok3, tok3, pl.BlockSpec((ts, BA_W), lambda d, i: (i, d)), row3, row3],
        out_specs=[pl.BlockSpec((ts, BA_W), lambda d, i: (i, d)), row3, row3],
        out_shape=[jax.ShapeDtypeStruct((s, 2 * BA_W), BF16), jax.ShapeDtypeStruct((2, 1, BA_W), F32),
                   jax.ShapeDtypeStruct((2, 1, BA_W), F32)],
        compiler_params=_params(("arbitrary", "arbitrary")),
    )(dscal, drow, ba, alog, dtb)


HALO = 16


def _halo_specs(ts, width, col_block, n_rows, rows=HALO):
    r = ts // rows
    last = n_rows // rows - 1
    return [pl.BlockSpec((rows, width), lambda i: (jnp.maximum(i * r - 1, 0), col_block)),
            pl.BlockSpec((ts, width), lambda i: (i, col_block)),
            pl.BlockSpec((rows, width), lambda i: (jnp.minimum((i + 1) * r, last), col_block))]


def _fill_halo(dst_ref, prev_ref, cur_ref, next_ref, first, last, fn=lambda r: r[...].astype(F32)):
    h = prev_ref.shape[0]
    ts = cur_ref.shape[0]
    p = fn(prev_ref)
    n = fn(next_ref)
    dst_ref[0:h, :] = jnp.where(first, 0.0, p)
    dst_ref[h:h + ts, :] = fn(cur_ref)
    dst_ref[h + ts:h + ts + h, :] = jnp.where(last, 0.0, n)


def _dwconv_rows(src_ref, w, start, n_rows, cols):
    acc = w[0:1, :] * src_ref[start:start + n_rows, cols]
    for i in range(1, w.shape[0]):
        acc = acc + w[i:i + 1, :] * src_ref[start + i:start + i + n_rows, cols]
    return acc


def _l2norm_heads(act, scale):
    outs = []
    for hd in range(HEADS):
        seg = act[:, hd * HEAD_DIM:(hd + 1) * HEAD_DIM]
        outs.append(seg * (lax.rsqrt(jnp.sum(seg * seg, axis=-1, keepdims=True) + EPS) * scale))
    return jnp.concatenate(outs, axis=-1)


Q_SCALE = HEAD_DIM ** -0.5


def _conv_fwd(proj, conv_dn, conv_a, *, name, ts=256):
    s = proj.shape[0]
    hd = HEADS * HEAD_DIM
    nt = s // ts

    def body(qp_ref, qc_ref, qn_ref, ap_ref, ac_ref, an_ref, wdn_ref, wa_ref,
             q_ref, k_ref, v_ref, ya_ref, xs_ref, xa_ref):
        i = pl.program_id(0)
        first, last = i == 0, i == nt - 1
        _fill_halo(xs_ref, qp_ref, qc_ref, qn_ref, first, last)
        wdn = wdn_ref[...]
        for part, o_ref in enumerate((q_ref, k_ref, v_ref)):
            cols = slice(part * hd, (part + 1) * hd)
            pre = _dwconv_rows(xs_ref, wdn[:, cols], HALO - 2, ts, cols)
            act = pre * _sigmoid(pre)
            if part == 0:
                act = _l2norm_heads(act, Q_SCALE)
            elif part == 1:
                act = _l2norm_heads(act, 1.0)
            o_ref[...] = act
        cv = lambda r: r[:, CONV_A:2 * CONV_A].astype(F32) * r[:, 2 * CONV_A:].astype(F32)
        _fill_halo(xa_ref, ap_ref, ac_ref, an_ref, first, last, fn=cv)
        conv = _dwconv_rows(xa_ref, wa_ref[...], HALO - 1, ts, slice(0, CONV_A))
        ya_ref[...] = (ac_ref[:, 0:CONV_A].astype(F32) * conv).astype(BF16)

    tile = lambda w: pl.BlockSpec((ts, w), lambda i: (i, 0))
    return pl.pallas_call(
        body, name=name, grid=(nt,),
        in_specs=(_halo_specs(ts, QKV_W, 0, s) + _halo_specs(ts, A_W, A_OFF // A_W, s)
                  + [_resident(conv_dn.shape), _resident(conv_a.shape)]),
        out_specs=[tile(hd), tile(hd), tile(hd), tile(CONV_A)],
        out_shape=[jax.ShapeDtypeStruct((s, hd), F32)] * 3 + [jax.ShapeDtypeStruct((s, CONV_A), BF16)],
        scratch_shapes=[pltpu.VMEM((ts + 2 * HALO, QKV_W), F32), pltpu.VMEM((ts + 2 * HALO, CONV_A), F32)],
        compiler_params=_params(("parallel",)),
    )(proj, proj, proj, proj, proj, proj, conv_dn, conv_a)


def _chunk_of_step(d, c, n):
    return c + d * (n - 1 - 2 * c)


def _head_scalars(scv, grv, hd):
    col = lambda base: scv[:, base + hd:base + hd + 1]
    return (col(L_BETA), col(L_G), col(L_EG), col(L_EKD),
            scv[0:1, L_EGC + hd:L_EGC + hd + 1], grv[hd:hd + 1, :])


def _decay_matrix(gcol, grow, incl):
    return jnp.where(incl, jnp.exp(jnp.minimum(gcol - grow, 0.0)), 0.0)


INV_BASE = 8


def _unit_lower_inverse(a_m):
    n = a_m.shape[0]
    i = lax.broadcasted_iota(jnp.int32, (n, n), 0)
    j = lax.broadcasted_iota(jnp.int32, (n, n), 1)

    def same_block(m):
        sh = int(math.log2(m))
        return jnp.right_shift(i, sh) == jnp.right_shift(j, sh)

    x = jnp.where(same_block(INV_BASE), -a_m, 0.0)
    t = jnp.where(i == j, 1.0, 0.0) + x
    p = x
    for _ in range(int(math.log2(INV_BASE)) - 1):
        p = _dot3(p, p)
        t = t + _dot3(t, p)
    m = INV_BASE
    while m < n:
        join = jnp.where(same_block(2 * m) & jnp.logical_not(same_block(m)), a_m, 0.0)
        t = t - _dot3(_dot3(t, join), t)
        m *= 2
    return t


def _delta_fwd(q, k, v, scal, grow, *, name):
    s = q.shape[0]
    n = s // CHUNK
    hd_all = HEADS * HEAD_DIM

    def body(q_ref, k_ref, v_ref, sc_ref, gr_ref, o_ref, st_ref, t_ref, vn_ref, state):
        d = pl.program_id(0)
        sgn = 1 - 2 * d

        @pl.when(pl.program_id(1) == 0)
        def _():
            state[...] = jnp.zeros_like(state)

        incl = _tri(CHUNK, sgn)
        strict = _tri(CHUNK, sgn, strict=True)
        scv = sc_ref[0]
        grv = gr_ref[0, 0]
        for hd in range(HEADS):
            cols = slice(hd * HEAD_DIM, (hd + 1) * HEAD_DIM)
            qh, kh, vh = q_ref[:, cols], k_ref[:, cols], v_ref[:, cols]
            beta, gcol, eg, ekd, egc, grow_h = _head_scalars(scv, grv, hd)
            dm = _decay_matrix(gcol, grow_h, incl)
            k_b = kh.astype(BF16)
            kk = _dot_nt((kh * beta).astype(BF16), k_b)
            t = _unit_lower_inverse(jnp.where(strict, kk * dm, 0.0))
            p_m = jnp.where(incl, _dot_nt(qh.astype(BF16), k_b) * dm, 0.0)
            sh = state[hd]
            sh_b = sh.astype(BF16)
            st_ref[0, 0, hd] = sh_b
            r = vh - _dot((kh * eg).astype(BF16), sh_b)
            vn = _dot3(t, beta * r)
            vn_b = vn.astype(BF16)
            o_ref[0, :, cols] = _dot((qh * eg).astype(BF16), sh_b) + _dot(p_m.astype(BF16), vn_b)
            state[hd] = egc * sh + _dot_tn((kh * ekd).astype(BF16), vn_b)
            t_ref[0, 0, hd] = t
            vn_ref[0, :, cols] = vn_b

    tok = lambda d, c: (_chunk_of_step(d, c, n), 0)
    dtok = lambda d, c: (d, _chunk_of_step(d, c, n), 0)
    dchunk4 = lambda d, c: (d, _chunk_of_step(d, c, n), 0, 0)
    dchunk5 = lambda d, c: (d, _chunk_of_step(d, c, n), 0, 0, 0)
    return pl.pallas_call(
        body, name=name, grid=(2, n),
        in_specs=[pl.BlockSpec((CHUNK, hd_all), tok)] * 3
                 + [pl.BlockSpec((1, CHUNK, BA_W), dtok), pl.BlockSpec((1, 1, HEADS, CHUNK), dchunk4)],
        out_specs=[pl.BlockSpec((1, CHUNK, hd_all), dtok),
                   pl.BlockSpec((1, 1, HEADS, HEAD_DIM, HEAD_DIM), dchunk5),
                   pl.BlockSpec((1, 1, HEADS, CHUNK, CHUNK), dchunk5),
                   pl.BlockSpec((1, CHUNK, hd_all), dtok)],
        out_shape=[jax.ShapeDtypeStruct((2, s, hd_all), F32),
                   jax.ShapeDtypeStruct((2, n, HEADS, HEAD_DIM, HEAD_DIM), BF16),
                   jax.ShapeDtypeStruct((2, n, HEADS, CHUNK, CHUNK), F32),
                   jax.ShapeDtypeStruct((2, s, hd_all), BF16)],
        scratch_shapes=[pltpu.VMEM((HEADS, HEAD_DIM, HEAD_DIM), F32)],
        compiler_params=_params(("arbitrary", "arbitrary")),
    )(q, k, v, scal, grow)


def _delta_bwd(q, k, v, scal, grow, states, tinv, vn, do, *, name):
    s = q.shape[0]
    n = s // CHUNK
    hd_all = HEADS * HEAD_DIM

    def body(q_ref, k_ref, v_ref, sc_ref, gr_ref, st_ref, t_ref, vn_ref, do_ref,
             dq_ref, dk_ref, dv_ref, dsc_ref, dgr_ref, dstate):
        d = pl.program_id(0)
        sgn = 1 - 2 * d

        @pl.when(pl.program_id(1) == 0)
        def _():
            dstate[...] = jnp.zeros_like(dstate)

        incl = _tri(CHUNK, sgn)
        strict = _tri(CHUNK, sgn, strict=True)
        scv = sc_ref[0]
        grv = gr_ref[0, 0]
        lane = lax.broadcasted_iota(jnp.int32, (CHUNK, BA_W), 1)
        row = lax.broadcasted_iota(jnp.int32, (CHUNK, 1), 0)
        sub = lax.broadcasted_iota(jnp.int32, (HEADS, CHUNK), 0)
        end_row = jnp.where(d == 0, CHUNK - 1, 0)
        dsc_acc = jnp.zeros((CHUNK, BA_W), F32)
        dgr_acc = jnp.zeros((HEADS, CHUNK), F32)
        for hd in range(HEADS):
            cols = slice(hd * HEAD_DIM, (hd + 1) * HEAD_DIM)
            qh, kh, vh = q_ref[:, cols], k_ref[:, cols], v_ref[:, cols]
            beta, gcol, eg, ekd, egc, grow_h = _head_scalars(scv, grv, hd)
            dm = _decay_matrix(gcol, grow_h, incl)
            q_b, k_b = qh.astype(BF16), kh.astype(BF16)
            kb_b = (kh * beta).astype(BF16)
            kk = _dot_nt(kb_b, k_b)
            qk = _dot_nt(q_b, k_b)
            p_m = jnp.where(incl, qk * dm, 0.0)
            t = t_ref[0, 0, hd]
            vn_b = vn_ref[0, :, cols]
            sh_b = st_ref[0, 0, hd]
            dsp = dstate[hd]
            dsp_b = dsp.astype(BF16)
            do_b = do_ref[:, cols].astype(BF16)
            kg, qg, kd = kh * eg, qh * eg, kh * ekd
            kg_b, qg_b, kd_b = kg.astype(BF16), qg.astype(BF16), kd.astype(BF16)
            r = vh - _dot(kg_b, sh_b)
            dvn = _dot_tn(p_m.astype(BF16), do_b) + _dot(kd_b, dsp_b)
            db = _dot3(t, dvn, dot=_dot_tn)
            dr = db * beta
            dbeta = jnp.sum(db * r, axis=-1, keepdims=True)
            dr_b, db_b = dr.astype(BF16), db.astype(BF16)
            dkg = -_dot_nt(dr_b, sh_b)
            dqg = _dot_nt(do_b, sh_b)
            dkd = _dot_nt(vn_b, dsp_b)
            dpm = jnp.where(incl, _dot_nt(do_b, vn_b), 0.0) * dm
            dam = jnp.where(strict, -_dot_nt(db_b, vn_b), 0.0) * dm
            dpm_b, dam_b = dpm.astype(BF16), dam.astype(BF16)
            dkb = _dot(dam_b, k_b)
            dq_ref[0, :, cols] = dqg * eg + _dot(dpm_b, k_b)
            dk_ref[0, :, cols] = (dkg * eg + dkd * ekd + _dot_tn(dpm_b, q_b) + _dot_tn(dam_b, kb_b)
                                  + dkb * beta)
            dv_ref[0, :, cols] = dr
            dbeta = dbeta + jnp.sum(dkb * kh, axis=-1, keepdims=True)
            m = dpm * qk + dam * kk
            kd_term = jnp.sum(dkd * kd, axis=-1, keepdims=True)
            dgcol = (jnp.sum(dqg * qg, axis=-1, keepdims=True) + jnp.sum(dkg * kg, axis=-1, keepdims=True)
                     - kd_term + jnp.sum(m, axis=-1, keepdims=True))
            dg_end = jnp.sum(kd_term) + egc * jnp.sum(dsp * sh_b.astype(F32))
            dgcol = dgcol + jnp.where(row == end_row, dg_end, 0.0)
            dsc_acc = jnp.where(lane == L_BETA + hd, dbeta, dsc_acc)
            dsc_acc = jnp.where(lane == L_G + hd, dgcol, dsc_acc)
            dgr_acc = jnp.where(sub == hd, -jnp.sum(m, axis=0, keepdims=True), dgr_acc)
            dstate[hd] = _dot_tn(qg_b, do_b) + egc * dsp - _dot_tn(kg_b, dr_b)
        dsc_ref[0] = dsc_acc
        dgr_ref[0, 0] = dgr_acc

    step = lambda d, c: n - 1 - _chunk_of_step(d, c, n)
    tok = lambda d, c: (step(d, c), 0)
    dtok = lambda d, c: (d, step(d, c), 0)
    dchunk4 = lambda d, c: (d, step(d, c), 0, 0)
    dchunk5 = lambda d, c: (d, step(d, c), 0, 0, 0)
    tok_spec = pl.BlockSpec((CHUNK, hd_all), tok)
    dtok_spec = pl.BlockSpec((1, CHUNK, hd_all), dtok)
    return pl.pallas_call(
        body, name=name, grid=(2, n),
        in_specs=[tok_spec] * 3
                 + [pl.BlockSpec((1, CHUNK, BA_W), dtok), pl.BlockSpec((1, 1, HEADS, CHUNK), dchunk4),
                    pl.BlockSpec((1, 1, HEADS, HEAD_DIM, HEAD_DIM), dchunk5),
                    pl.BlockSpec((1, 1, HEADS, CHUNK, CHUNK), dchunk5), dtok_spec, tok_spec],
        out_specs=[dtok_spec] * 3
                  + [pl.BlockSpec((1, CHUNK, BA_W), dtok), pl.BlockSpec((1, 1, HEADS, CHUNK), dchunk4)],
        out_shape=[jax.ShapeDtypeStruct((2, s, hd_all), F32)] * 3
                  + [jax.ShapeDtypeStruct((2, s, BA_W), F32), jax.ShapeDtypeStruct((2, n, HEADS, CHUNK), F32)],
        scratch_shapes=[pltpu.VMEM((HEADS, HEAD_DIM, HEAD_DIM), F32)],
        compiler_params=_params(("arbitrary", "arbitrary")),
    )(q, k, v, scal, grow, states, tinv, vn, do)


def _gate_norm_fwd(o2, proj, dnw, *, name, ts=512):
    s = o2.shape[1]
    hd_all = HEADS * HEAD_DIM

    def body(o_ref, z_ref, w_ref, y_ref):
        w = w_ref[...]
        for hd in range(HEADS):
            cols = slice(hd * HEAD_DIM, (hd + 1) * HEAD_DIM)
            seg = o_ref[0, :, cols] + o_ref[1, :, cols]
            r = lax.rsqrt(jnp.mean(seg * seg, axis=-1, keepdims=True) + EPS)
            z = z_ref[:, cols].astype(F32)
            y_ref[:, cols] = ((seg * r * w) * (z * _sigmoid(z))).astype(BF16)

    return pl.pallas_call(
        body, name=name, grid=(s // ts,),
        in_specs=[pl.BlockSpec((2, ts, hd_all), lambda i: (0, i, 0)),
                  pl.BlockSpec((ts, Z_W), lambda i: (i, Z_OFF // Z_W)), _row(HEAD_DIM)],
        out_specs=pl.BlockSpec((ts, hd_all), lambda i: (i, 0)),
        out_shape=jax.ShapeDtypeStruct((s, hd_all), BF16),
        compiler_params=_params(("parallel",)),
    )(o2, proj, dnw)


def _gate_norm_bwd(dyb, o2, proj, dnw, *, name, ts=512):
    s = o2.shape[1]
    hd_all = HEADS * HEAD_DIM

    def body(dy_ref, o_ref, z_ref, w_ref, do_ref, dz_ref, dw_ref):
        w = w_ref[...]
        dw = jnp.zeros((1, HEAD_DIM), F32)
        for hd in range(HEADS):
            cols = slice(hd * HEAD_DIM, (hd + 1) * HEAD_DIM)
            seg = o_ref[0, :, cols] + o_ref[1, :, cols]
            r = lax.rsqrt(jnp.mean(seg * seg, axis=-1, keepdims=True) + EPS)
            xhat = seg * r
            z = z_ref[:, cols].astype(F32)
            sg = _sigmoid(z)
            dy = dy_ref[:, cols]
            dnrm = dy * (z * sg)
            dz_ref[:, cols] = (dy * (xhat * w) * (sg * (1.0 + z * (1.0 - sg)))).astype(BF16)
            dw = dw + jnp.sum(dnrm * xhat, axis=0, keepdims=True)
            dxhat = dnrm * w
            do_ref[:, cols] = r * (dxhat - xhat * jnp.mean(dxhat * xhat, axis=-1, keepdims=True))

        @pl.when(pl.program_id(0) == 0)
        def _():
            dw_ref[...] = jnp.zeros_like(dw_ref)

        dw_ref[...] += dw

    tile = pl.BlockSpec((ts, hd_all), lambda i: (i, 0))
    return pl.pallas_call(
        body, name=name, grid=(s // ts,),
        in_specs=[tile, pl.BlockSpec((2, ts, hd_all), lambda i: (0, i, 0)),
                  pl.BlockSpec((ts, Z_W), lambda i: (i, Z_OFF // Z_W)), _row(HEAD_DIM)],
        out_specs=[tile, tile, _row(HEAD_DIM)],
        out_shape=[jax.ShapeDtypeStruct((s, hd_all), F32), jax.ShapeDtypeStruct((s, hd_all), BF16),
                   jax.ShapeDtypeStruct((1, HEAD_DIM), F32)],
        compiler_params=_params(("arbitrary",)),
    )(dyb, o2, proj, dnw)


def _merge_fwd(ya, yb, proj, wa, wb, wo, h, g, *, name, ts=512):
    s, d = h.shape

    def body(ya_ref, yb_ref, gt_ref, wa_ref, wb_ref, wo_ref, h_ref, g_ref, pa_ref, pb_ref, mix_ref, ho_ref):
        pa = _dot(ya_ref[...], wa_ref[...])
        pb = _dot(yb_ref[...], wb_ref[...])
        pa_ref[...] = pa.astype(BF16)
        pb_ref[...] = pb.astype(BF16)
        merged = (_sigmoid(gt_ref[:, :d].astype(F32)) * pa + _sigmoid(gt_ref[:, d:].astype(F32)) * pb)
        mix = _dot(merged.astype(BF16), wo_ref[...])
        mix_ref[...] = mix.astype(BF16)
        ho_ref[...] = h_ref[...] + g_ref[...] * mix

    tile = pl.BlockSpec((ts, d), lambda i: (i, 0))
    return pl.pallas_call(
        body, name=name, grid=(s // ts,),
        in_specs=[pl.BlockSpec((ts, CONV_A), lambda i: (i, 0)), tile,
                  pl.BlockSpec((ts, GATE_W), lambda i: (i, GATE_OFF // GATE_W)),
                  _resident(wa.shape), _resident(wb.shape), _resident(wo.shape), tile, _row(d)],
        out_specs=[tile, tile, tile, tile],
        out_shape=[jax.ShapeDtypeStruct((s, d), BF16)] * 3 + [jax.ShapeDtypeStruct((s, d), F32)],
        compiler_params=_params(("parallel",)),
    )(ya, yb, proj, wa, wb, wo, h, g)


def _merge_bwd(dh, g, mix, pa, pb, proj, wa, wb, wo, *, name, ts=256):
    s, d = dh.shape

    def body(dh_ref, g_ref, mix_ref, pa_ref, pb_ref, gt_ref, wa_ref, wb_ref, wo_ref,
             dmix_ref, mg_ref, dpa_ref, dpb_ref, dgt_ref, dya_ref, dyb_ref, dg_ref):
        dh_v = dh_ref[...]
        dmix = (g_ref[...] * dh_v).astype(BF16)
        dmix_ref[...] = dmix

        @pl.when(pl.program_id(0) == 0)
        def _():
            dg_ref[...] = jnp.zeros_like(dg_ref)

        dg_ref[...] += jnp.sum(dh_v * mix_ref[...].astype(F32), axis=0, keepdims=True)
        dmerged = _dot_nt(dmix, wo_ref[...])
        pa = pa_ref[...].astype(F32)
        pb = pb_ref[...].astype(F32)
        sa = _sigmoid(gt_ref[:, :d].astype(F32))
        sb = _sigmoid(gt_ref[:, d:].astype(F32))
        mg_ref[...] = (sa * pa + sb * pb).astype(BF16)
        dpa = (dmerged * sa).astype(BF16)
        dpb = (dmerged * sb).astype(BF16)
        dpa_ref[...] = dpa
        dpb_ref[...] = dpb
        dgt_ref[:, :d] = (dmerged * pa * sa * (1.0 - sa)).astype(BF16)
        dgt_ref[:, d:] = (dmerged * pb * sb * (1.0 - sb)).astype(BF16)
        dya_ref[...] = _dot_nt(dpa, wa_ref[...])
        dyb_ref[...] = _dot_nt(dpb, wb_ref[...])

    tile = pl.BlockSpec((ts, d), lambda i: (i, 0))
    return pl.pallas_call(
        body, name=name, grid=(s // ts,),
        in_specs=[tile, _row(d), tile, tile, tile,
                  pl.BlockSpec((ts, GATE_W), lambda i: (i, GATE_OFF // GATE_W)),
                  _resident(wa.shape), _resident(wb.shape), _resident(wo.shape)],
        out_specs=[tile, tile, tile, tile, pl.BlockSpec((ts, GATE_W), lambda i: (i, 0)),
                   pl.BlockSpec((ts, CONV_A), lambda i: (i, 0)), tile, _row(d)],
        out_shape=[jax.ShapeDtypeStruct((s, d), BF16)] * 4
                  + [jax.ShapeDtypeStruct((s, GATE_W), BF16), jax.ShapeDtypeStruct((s, CONV_A), F32),
                     jax.ShapeDtypeStruct((s, d), F32), jax.ShapeDtypeStruct((1, d), F32)],
        compiler_params=_params(("arbitrary",)),
    )(dh, g, mix, pa, pb, proj, wa, wb, wo)


def _final_fwd_bwd(h, nw, target, *, name, ts=512):
    s, d = h.shape

    def body(h_ref, nw_ref, t_ref, loss_ref, dh_ref, dnw_ref):
        x = h_ref[...]
        w = nw_ref[...]
        r = lax.rsqrt(jnp.mean(x * x, axis=-1, keepdims=True) + EPS)
        xhat = x * r
        e = xhat * w - t_ref[...]
        part = 0.5 * jnp.sum(jnp.mean(e * e, axis=-1, keepdims=True))
        dy = e * (1.0 / d)
        dxhat = dy * w
        dh_ref[...] = r * (dxhat - xhat * jnp.mean(dxhat * xhat, axis=-1, keepdims=True))

        @pl.when(pl.program_id(0) == 0)
        def _():
            loss_ref[...] = jnp.zeros_like(loss_ref)
            dnw_ref[...] = jnp.zeros_like(dnw_ref)

        loss_ref[...] += jnp.broadcast_to(part, loss_ref.shape)
        dnw_ref[...] += jnp.sum(dy * xhat, axis=0, keepdims=True)

    tile = pl.BlockSpec((ts, d), lambda i: (i, 0))
    return pl.pallas_call(
        body, name=name, grid=(s // ts,),
        in_specs=[tile, _row(d), tile],
        out_specs=[_row(128), tile, _row(d)],
        out_shape=[jax.ShapeDtypeStruct((1, 128), F32), jax.ShapeDtypeStruct((s, d), F32),
                   jax.ShapeDtypeStruct((1, d), F32)],
        compiler_params=_params(("arbitrary",)),
    )(h, nw, target)


EXT = 8


def _l2norm_heads_bwd(act, dout, scale):
    outs = []
    for hd in range(HEADS):
        cols = slice(hd * HEAD_DIM, (hd + 1) * HEAD_DIM)
        seg = act[:, cols]
        nrm = lax.rsqrt(jnp.sum(seg * seg, axis=-1, keepdims=True) + EPS)
        yhat = seg * nrm
        dsg = dout[:, cols]
        outs.append((scale * nrm) * (dsg - yhat * jnp.sum(yhat * dsg, axis=-1, keepdims=True)))
    return jnp.concatenate(outs, axis=-1)


def _conv_bwd(dq2, dk2, dv2, dya, proj, conv_dn, conv_a, *, name, ts=256):
    s = proj.shape[0]
    hd = HEADS * HEAD_DIM
    nt = s // ts
    te = ts + 2 * EXT
    kdn, ka = conv_dn.shape[0], conv_a.shape[0]

    def body(*refs):
        (qp_ref, qc_ref, qn_ref, ap_ref, ac_ref, an_ref) = refs[0:6]
        d3 = refs[6:15]
        (yp_ref, yc_ref, yn_ref, wdn_ref, wa_ref) = refs[15:20]
        (dqkv_ref, da_ref, dwdn_ref, dwa_ref) = refs[20:24]
        xs_ref, dps_ref, xa_ref, dca_ref = refs[24:28]
        i = pl.program_id(0)
        first, last = i == 0, i == nt - 1

        @pl.when(first)
        def _():
            dwdn_ref[...] = jnp.zeros_like(dwdn_ref)
            dwa_ref[...] = jnp.zeros_like(dwa_ref)

        rowe = lax.broadcasted_iota(jnp.int32, (te, 1), 0)
        inside = ~((first & (rowe < EXT)) | (last & (rowe >= EXT + ts)))
        _fill_halo(xs_ref, qp_ref, qc_ref, qn_ref, first, last)
        wdn = wdn_ref[...]
        for part in range(3):
            cols = slice(part * hd, (part + 1) * hd)
            pre = _dwconv_rows(xs_ref, wdn[:, cols], HALO - EXT - 2, te, cols)
            sg = _sigmoid(pre)
            act = pre * sg
            p_ref, c_ref, n_ref = d3[3 * part:3 * part + 3]
            dout = jnp.concatenate([p_ref[0] + p_ref[1], c_ref[0] + c_ref[1], n_ref[0] + n_ref[1]], axis=0)
            if part == 0:
                dact = _l2norm_heads_bwd(act, dout, Q_SCALE)
            elif part == 1:
                dact = _l2norm_heads_bwd(act, dout, 1.0)
            else:
                dact = dout
            dpre = jnp.where(inside, dact * (sg * (1.0 + pre * (1.0 - sg))), 0.0)
            dps_ref[:, cols] = dpre
            acc = wdn[0:1, cols] * dps_ref[EXT + 2:EXT + 2 + ts, cols]
            for tap in range(1, kdn):
                acc = acc + wdn[tap:tap + 1, cols] * dps_ref[EXT + 2 - tap:EXT + 2 - tap + ts, cols]
            dqkv_ref[:, cols] = acc.astype(BF16)
            dcur = dps_ref[EXT:EXT + ts, cols]
            for tap in range(kdn):
                dwdn_ref[tap:tap + 1, cols] += jnp.sum(
                    dcur * xs_ref[HALO - 2 + tap:HALO - 2 + tap + ts, cols], axis=0, keepdims=True)

        cv = lambda r: r[:, CONV_A:2 * CONV_A].astype(F32) * r[:, 2 * CONV_A:].astype(F32)
        _fill_halo(xa_ref, ap_ref, ac_ref, an_ref, first, last, fn=cv)
        wa = wa_ref[...]
        gate_b = jnp.concatenate([ap_ref[HALO - EXT:, 0:CONV_A], ac_ref[:, 0:CONV_A], an_ref[0:EXT, 0:CONV_A]],
                                 axis=0).astype(F32)
        dya_e = jnp.concatenate([yp_ref[...], yc_ref[...], yn_ref[...]], axis=0)
        dca_ref[...] = jnp.where(inside, dya_e * gate_b, 0.0)
        conv = _dwconv_rows(xa_ref, wa, HALO - 1, ts, slice(0, CONV_A))
        acc = wa[0:1, :] * dca_ref[EXT + 1:EXT + 1 + ts, :]
        for tap in range(1, ka):
            acc = acc + wa[tap:tap + 1, :] * dca_ref[EXT + 1 - tap:EXT + 1 - tap + ts, :]
        gc = ac_ref[:, CONV_A:2 * CONV_A].astype(F32)
        val = ac_ref[:, 2 * CONV_A:].astype(F32)
        da_ref[:, 0:CONV_A] = (yc_ref[...] * conv).astype(BF16)
        da_ref[:, CONV_A:2 * CONV_A] = (acc * val).astype(BF16)
        da_ref[:, 2 * CONV_A:] = (acc * gc).astype(BF16)
        dcur = dca_ref[EXT:EXT + ts, :]
        for tap in range(ka):
            dwa_ref[tap:tap + 1, :] += jnp.sum(
                dcur * xa_ref[HALO - 1 + tap:HALO - 1 + tap + ts, :], axis=0, keepdims=True)

    def halo3(width, rows):
        r = ts // rows
        lastb = s // rows - 1
        return [pl.BlockSpec((2, rows, width), lambda i: (0, jnp.maximum(i * r - 1, 0), 0)),
                pl.BlockSpec((2, ts, width), lambda i: (0, i, 0)),
                pl.BlockSpec((2, rows, width), lambda i: (0, jnp.minimum((i + 1) * r, lastb), 0))]

    return pl.pallas_call(
        body, name=name, grid=(nt,),
        in_specs=(_halo_specs(ts, QKV_W, 0, s) + _halo_specs(ts, A_W, A_OFF // A_W, s)
                  + halo3(hd, EXT) * 3 + _halo_specs(ts, CONV_A, 0, s, rows=EXT)
                  + [_resident(conv_dn.shape), _resident(conv_a.shape)]),
        out_specs=[pl.BlockSpec((ts, QKV_W), lambda i: (i, 0)), pl.BlockSpec((ts, A_W), lambda i: (i, 0)),
                   pl.BlockSpec((8, QKV_W), lambda i: (0, 0)), pl.BlockSpec((8, CONV_A), lambda i: (0, 0))],
        out_shape=[jax.ShapeDtypeStruct((s, QKV_W), BF16), jax.ShapeDtypeStruct((s, A_W), BF16),
                   jax.ShapeDtypeStruct((8, QKV_W), F32), jax.ShapeDtypeStruct((8, CONV_A), F32)],
        scratch_shapes=[pltpu.VMEM((ts + 2 * HALO, QKV_W), F32), pltpu.VMEM((te, QKV_W), F32),
                        pltpu.VMEM((ts + 2 * HALO, CONV_A), F32), pltpu.VMEM((te, CONV_A), F32)],
        compiler_params=_params(("arbitrary",)),
    )(proj, proj, proj, proj, proj, proj, dq2, dq2, dq2, dk2, dk2, dk2, dv2, dv2, dv2,
      dya, dya, dya, conv_dn, conv_a)


IN_A = (0, 1536)
IN_QKV = (1536, 4608)
IN_Z = (4608, 5632)
IN_BA = 5632
IN_GATE = (5664, 7712)
IN_COLS = 7712
G_REPL = 4


def _split_w_in(w_in):
    sl = lambda ab: w_in[:, ab[0]:ab[1]]
    w_main = jnp.concatenate([sl(IN_QKV), sl(IN_Z), sl(IN_GATE), sl(IN_A)], axis=1)
    blocks = []
    for d in range(2):
        beta = w_in[:, IN_BA + 8 * d:IN_BA + 8 * d + 8]
        alpha = w_in[:, IN_BA + 16 + 8 * d:IN_BA + 24 + 8 * d]
        pad = jnp.zeros((w_in.shape[0], BA_W - 8 - 8 * G_REPL), w_in.dtype)
        blocks += [beta] + [alpha] * G_REPL + [pad]
    return w_main, jnp.concatenate(blocks, axis=1)


def _merge_dw_in(dw_qkv, dw_z, dw_gate, dw_a, dw_ba):
    ba = [dw_ba[:, 0:8], dw_ba[:, BA_W:BA_W + 8], dw_ba[:, 8:16], dw_ba[:, BA_W + 8:BA_W + 16]]
    return jnp.concatenate([dw_a, dw_qkv, dw_z] + ba + [dw_gate], axis=1)


def _decay_rows(a_log_fwd, dt_bias_fwd, a_log_bwd, dt_bias_bwd):
    def rows(f, b):
        out = []
        for vec in (f, b):
            vec = vec.reshape(HEADS)
            out.append(jnp.concatenate([jnp.zeros((8,), F32)] + [vec] * G_REPL
                                       + [jnp.zeros((BA_W - 8 - 8 * G_REPL,), F32)])[None])
        return jnp.stack(out)
    return rows(a_log_fwd, a_log_bwd), rows(dt_bias_fwd, dt_bias_bwd)


def _local_step(x, target, mod9, wt):
    s, d = x.shape
    n = s // CHUNK
    sh1, sc1, g1, sh2, sc2, g2, sh3, sc3, g3 = [mod9[i:i + 1] for i in range(9)]
    w_main, w_ba = _split_w_in(wt["w_in"])
    alog, dtb = _decay_rows(wt["a_log_fwd"], wt["dt_bias_fwd"], wt["a_log_bwd"], wt["dt_bias_bwd"])

    u1, a1, b1, f1 = _ffn_up_fwd(x, wt["norm_ffn1"], sc1, sh1, wt["w_ffn1_up"], name="ffn1_up")
    y1, h1 = _ffn_down_fwd(f1, wt["w_ffn1_down"], x, g1, name="ffn1_down")
    u2, proj, ba = _in_proj_fwd(h1, wt["norm_mix"], sc2, sh2, w_main, w_ba, name="in_proj")
    scal = _scal_fwd(ba, alog, dtb, name="scal_fwd")
    grow = scal[:, :, L_G:L_G + 8].reshape(2, n, CHUNK, HEADS).transpose(0, 1, 3, 2)
    q, k, v, ya = _conv_fwd(proj, wt["conv_dn"], wt["conv_a"], name="conv_fwd")
    o2, states, tinv, vn = _delta_fwd(q, k, v, scal, grow, name="delta_fwd")
    yb = _gate_norm_fwd(o2, proj, wt["dn_norm"], name="gate_norm_fwd")
    pa, pb, mix, h2 = _merge_fwd(ya, yb, proj, wt["w_a_out"], wt["w_b_out"], wt["w_out"], h1, g2,
                                 name="merge_fwd")
    u3, a3, b3, f3 = _ffn_up_fwd(h2, wt["norm_ffn2"], sc3, sh3, wt["w_ffn2_up"], name="ffn2_up")
    y3, h3 = _ffn_down_fwd(f3, wt["w_ffn2_down"], h2, g3, name="ffn2_down")
    loss, dh3, dnorm_final = _final_fwd_bwd(h3, wt["norm_final"], target, name="final")

    dy3, dab3, dg3 = _ffn_bwd_act(dh3, g3, y3, a3, b3, wt["w_ffn2_down"], name="ffn2_bwd_act")
    dh2, dn3, dsc3, dsh3 = _norm_mod_matmul_bwd([(dab3, wt["w_ffn2_up"])], h2, wt["norm_ffn2"], sc3, dh3,
                                                name="ffn2_bwd_up")
    gw = {}
    gw["w_ffn2_up"] = _matmul_tn(u3, dab3, name="dw_ffn2_up", tm=1024, tn=1408)
    gw["w_ffn2_down"] = _matmul_tn(f3, dy3, name="dw_ffn2_down", tm=1408, tn=1024)

    dmix, merged, dpa, dpb, dgates, dya, dyb, dg2 = _merge_bwd(
        dh2, g2, mix, pa, pb, proj, wt["w_a_out"], wt["w_b_out"], wt["w_out"], name="merge_bwd")
    gw["w_out"] = _matmul_tn(merged, dmix, name="dw_out", tm=1024, tn=1024)
    gw["w_a_out"] = _matmul_tn(ya, dpa, name="dw_a_out", tm=512, tn=1024)
    gw["w_b_out"] = _matmul_tn(yb, dpb, name="dw_b_out", tm=1024, tn=1024)
    do, dz, ddn = _gate_norm_bwd(dyb, o2, proj, wt["dn_norm"], name="gate_norm_bwd")
    dq2, dk2, dv2, dscal, drow = _delta_bwd(q, k, v, scal, grow, states, tinv, vn, do, name="delta_bwd")
    drow_p = jnp.pad(drow.transpose(0, 1, 3, 2).reshape(2, s, HEADS),
                     ((0, 0), (0, 0), (L_G, BA_W - L_G - HEADS)))
    dba, dalog, ddtb = _scal_bwd(dscal, drow_p, ba, alog, dtb, name="scal_bwd")
    dqkv, dbr_a, dconv_dn, dconv_a = _conv_bwd(dq2, dk2, dv2, dya, proj, wt["conv_dn"], wt["conv_a"],
                                               name="conv_bwd")
    w_qkv = w_main[:, 0:QKV_W]
    w_z = w_main[:, Z_OFF:Z_OFF + Z_W]
    w_gate = w_main[:, GATE_OFF:GATE_OFF + GATE_W]
    w_bra = w_main[:, A_OFF:A_OFF + A_W]
    dh1, dn2, dsc2, dsh2 = _norm_mod_matmul_bwd(
        [(dqkv, w_qkv), (dz, w_z), (dgates, w_gate), (dbr_a, w_bra), (dba, w_ba)],
        h1, wt["norm_mix"], sc2, dh2, name="in_proj_bwd")
    gw["w_in"] = _merge_dw_in(
        _matmul_tn(u2, dqkv, name="dw_in_qkv", tm=1024, tn=1536),
        _matmul_tn(u2, dz, name="dw_in_z", tm=1024, tn=1024),
        _matmul_tn(u2, dgates, name="dw_in_gate", tm=1024, tn=1024),
        _matmul_tn(u2, dbr_a, name="dw_in_a", tm=1024, tn=1536),
        _matmul_tn(u2, dba, name="dw_in_ba", tm=1024, tn=2 * BA_W))

    dy1, dab1, dg1 = _ffn_bwd_act(dh1, g1, y1, a1, b1, wt["w_ffn1_down"], name="ffn1_bwd_act")
    dx, dn1, dsc1, dsh1 = _norm_mod_matmul_bwd([(dab1, wt["w_ffn1_up"])], x, wt["norm_ffn1"], sc1, dh1,
                                               name="ffn1_bwd_up")
    gw["w_ffn1_up"] = _matmul_tn(u1, dab1, name="dw_ffn1_up", tm=1024, tn=1408)
    gw["w_ffn1_down"] = _matmul_tn(f1, dy1, name="dw_ffn1_down", tm=1408, tn=1024)

    small = {
        "mod": jnp.concatenate([dsh1, dsc1, dg1, dsh2, dsc2, dg2, dsh3, dsc3, dg3], axis=1),
        "norm_ffn1": dn1, "norm_mix": dn2, "norm_ffn2": dn3, "norm_final": dnorm_final,
        "a_log_fwd": dalog[0, :, L_G:L_G + 8], "dt_bias_fwd": ddtb[0, :, L_G:L_G + 8],
        "a_log_bwd": dalog[1, :, L_G:L_G + 8], "dt_bias_bwd": ddtb[1, :, L_G:L_G + 8],
        "dn_norm": ddn,
        "conv_a": dconv_a[0:3].reshape(1, -1), "conv_dn": dconv_dn[0:5].reshape(1, -1),
    }
    return loss, dx, gw, small


def _my_position():
    return tuple(lax.axis_index(a) for a in MESH_AXES)


def _peer(pos, kk):
    return tuple((1 - p) if (kk >> (2 - b)) & 1 else p for b, p in enumerate(pos))


def _flat_index(pos):
    return pos[0] * 4 + pos[1] * 2 + pos[2]


_ANY = pl.BlockSpec(memory_space=pl.ANY)


def _exchange(xs, *, name, gather):
    nt = len(xs)

    def body(*refs):
        in_refs, out_refs = refs[:nt], refs[nt:2 * nt]
        send_sems, recv_sems, local_sems = refs[2 * nt:]
        pos = _my_position()
        me = _flat_index(pos)
        copies = []
        for t in range(nt):
            own = in_refs[t] if gather else in_refs[t].at[me]
            loc = pltpu.make_async_copy(own, out_refs[t].at[me], local_sems.at[t])
            loc.start()
            copies.append(loc)
            for kk in range(1, N_DEV):
                peer = _peer(pos, kk)
                src = in_refs[t] if gather else in_refs[t].at[_flat_index(peer)]
                cp = pltpu.make_async_remote_copy(
                    src_ref=src, dst_ref=out_refs[t].at[me],
                    send_sem=send_sems.at[t, kk - 1], recv_sem=recv_sems.at[t, kk - 1],
                    device_id=peer, device_id_type=pl.DeviceIdType.MESH)
                cp.start()
                copies.append(cp)
        for cp in copies:
            cp.wait()

    out_shape = [jax.ShapeDtypeStruct(((N_DEV,) + x.shape) if gather else x.shape, x.dtype) for x in xs]
    return pl.pallas_call(
        body, name=name, in_specs=[_ANY] * nt, out_specs=[_ANY] * nt, out_shape=out_shape,
        scratch_shapes=[pltpu.SemaphoreType.DMA((nt, N_DEV - 1)), pltpu.SemaphoreType.DMA((nt, N_DEV - 1)),
                        pltpu.SemaphoreType.DMA((nt,))],
        compiler_params=pltpu.CompilerParams(has_side_effects=True),
    )(*xs)


def _mod_fwd(c_all, w_ada, *, name):
    def body(c_ref, w_ref, o_ref):
        cv = c_ref[...]
        o_ref[...] = _dot3(cv * _sigmoid(cv), w_ref[...])

    return pl.pallas_call(
        body, name=name, out_shape=jax.ShapeDtypeStruct((c_all.shape[0], w_ada.shape[1]), F32),
        compiler_params=_params(),
    )(c_all, w_ada)


def _adamw_math(w, g, m, v):
    m_new = ADAM_B1 * m + (1.0 - ADAM_B1) * g
    v_new = ADAM_B2 * v + (1.0 - ADAM_B2) * (g * g)
    m_hat = m_new / (1.0 - ADAM_B1 ** ADAM_STEP)
    v_hat = v_new / (1.0 - ADAM_B2 ** ADAM_STEP)
    delta = -ADAM_LR * (m_hat / (jnp.sqrt(v_hat) + ADAM_EPS) + ADAM_WD * w)
    return delta, m_new, v_new


def _reduce_adamw(pieces, w, m, v, *, name, tr):
    r, c = w.shape

    def body(p_ref, w_ref, m_ref, v_ref, g_ref, d_ref, mo_ref, vo_ref):
        g = p_ref[0].astype(F32)
        for src in range(1, N_DEV):
            g = g + p_ref[src].astype(F32)
        g_ref[...] = g
        d_ref[...], mo_ref[...], vo_ref[...] = _adamw_math(w_ref[...], g, m_ref[...], v_ref[...])

    tile = pl.BlockSpec((tr, c), lambda i: (i, 0))
    return pl.pallas_call(
        body, name=name, grid=(r // tr,),
        in_specs=[pl.BlockSpec((N_DEV, tr, c), lambda i: (0, i, 0)), tile, tile, tile],
        out_specs=[tile] * 4, out_shape=[jax.ShapeDtypeStruct((r, c), F32)] * 4,
        compiler_params=_params(("parallel",)),
    )(pieces, w, m, v)


def _ada_grad_adamw(c_all_t, dmod_cols, w, m, v, *, name, tr=256):
    r, c = w.shape

    def body(c_ref, dm_ref, w_ref, m_ref, v_ref, g_ref, d_ref, mo_ref, vo_ref):
        cv = c_ref[...]
        act = cv * _sigmoid(cv)
        dm = dm_ref[...]
        g = act[:, 0:1] * dm[0:1, :]
        for b in range(1, N_DEV):
            g = g + act[:, b:b + 1] * dm[b:b + 1, :]
        g_ref[...] = g
        d_ref[...], mo_ref[...], vo_ref[...] = _adamw_math(w_ref[...], g, m_ref[...], v_ref[...])

    tile = pl.BlockSpec((tr, c), lambda i: (i, 0))
    return pl.pallas_call(
        body, name=name, grid=(r // tr,),
        in_specs=[pl.BlockSpec((tr, N_DEV), lambda i: (i, 0)), pl.BlockSpec((N_DEV, c), lambda i: (0, 0)),
                  tile, tile, tile],
        out_specs=[tile] * 4, out_shape=[jax.ShapeDtypeStruct((r, c), F32)] * 4,
        compiler_params=_params(("parallel",)),
    )(c_all_t, dmod_cols, w, m, v)


def _sum_rows(parts, *, name):
    def body(p_ref, o_ref):
        acc = p_ref[0:1, :]
        for src in range(1, N_DEV):
            acc = acc + p_ref[src:src + 1, :]
        o_ref[...] = acc

    return pl.pallas_call(
        body, name=name, out_shape=jax.ShapeDtypeStruct((1, parts.shape[1]), F32), compiler_params=_params(),
    )(parts)


def _adamw_rows(g, w, m, v, *, name):
    def body(g_ref, w_ref, m_ref, v_ref, d_ref, mo_ref, vo_ref):
        d_ref[...], mo_ref[...], vo_ref[...] = _adamw_math(w_ref[...], g_ref[...], m_ref[...], v_ref[...])

    return pl.pallas_call(
        body, name=name, out_shape=[jax.ShapeDtypeStruct(g.shape, F32)] * 3, compiler_params=_params(),
    )(g, w, m, v)


WEIGHTS = ["w_ada", "b_ada", "norm_ffn1", "w_ffn1_up", "w_ffn1_down", "norm_mix", "w_in", "conv_a", "conv_dn",
           "a_log_fwd", "dt_bias_fwd", "a_log_bwd", "dt_bias_bwd", "dn_norm", "w_a_out", "w_b_out", "w_out",
           "norm_ffn2", "w_ffn2_up", "w_ffn2_down", "norm_final"]
COL_SHARDED = ["w_ffn1_up", "w_in", "w_a_out", "w_ffn2_up"]
ROW_SHARDED = ["w_ffn1_down", "w_b_out", "w_out", "w_ffn2_down"]
CONV_SHARDED = ["conv_a", "conv_dn"]
REPLICATED = ["b_ada", "norm_ffn1", "norm_mix", "a_log_fwd", "dt_bias_fwd", "a_log_bwd", "dt_bias_bwd",
              "dn_norm", "norm_ffn2", "norm_final"]
SMALL_ORDER = ["mod", "norm_ffn1", "norm_mix", "norm_ffn2", "norm_final", "a_log_fwd", "dt_bias_fwd",
               "a_log_bwd", "dt_bias_bwd", "dn_norm", "conv_a", "conv_dn"]
REDUCE_ROWS = {"w_ffn1_up": 256, "w_in": 256, "w_a_out": 256, "w_ffn2_up": 256,
               "w_ffn1_down": 176, "w_b_out": 128, "w_out": 128, "w_ffn2_down": 176}


def _pad_lanes(row):
    pad = (-row.shape[1]) % 128
    return jnp.pad(row, ((0, 0), (0, pad)))


def _unstack_cols(g):
    return g.transpose(1, 0, 2).reshape(g.shape[1], -1)


def _stack_cols(w):
    k = w.shape[0]
    return w.reshape(k, N_DEV, -1).transpose(1, 0, 2)


def kernel(x, c, w_ada, b_ada, norm_ffn1, w_ffn1_up, w_ffn1_down, norm_mix, w_in, conv_a, conv_dn, a_log_fwd, dt_bias_fwd, a_log_bwd, dt_bias_bwd, dn_norm, w_a_out, w_b_out, w_out, norm_ffn2, w_ffn2_up, w_ffn2_down, norm_final, loss_target, m_w_ada, m_b_ada, m_norm_ffn1, m_w_ffn1_up, m_w_ffn1_down, m_norm_mix, m_w_in, m_conv_a, m_conv_dn, m_a_log_fwd, m_dt_bias_fwd, m_a_log_bwd, m_dt_bias_bwd, m_dn_norm, m_w_a_out, m_w_b_out, m_w_out, m_norm_ffn2, m_w_ffn2_up, m_w_ffn2_down, m_norm_final, v_w_ada, v_b_ada, v_norm_ffn1, v_w_ffn1_up, v_w_ffn1_down, v_norm_mix, v_w_in, v_conv_a, v_conv_dn, v_a_log_fwd, v_dt_bias_fwd, v_a_log_bwd, v_dt_bias_bwd, v_dn_norm, v_w_a_out, v_w_b_out, v_w_out, v_norm_ffn2, v_w_ffn2_up, v_w_ffn2_down, v_norm_final):
    args = dict(locals())
    w_loc = {n: args[n] for n in WEIGHTS}
    m_loc = {n: args["m_" + n] for n in WEIGHTS}
    v_loc = {n: args["v_" + n] for n in WEIGHTS}
    me = _flat_index(_my_position())
    d_model = x.shape[-1]

    big = COL_SHARDED + ROW_SHARDED
    gathered = _exchange([c] + [w_loc[n][0].astype(BF16) for n in big] + [w_loc[n][0] for n in CONV_SHARDED],
                         name="gather_weights", gather=True)
    c_all = gathered[0].reshape(N_DEV, d_model)
    wt = {}
    for n, g in zip(big + CONV_SHARDED, gathered[1:]):
        wt[n] = _unstack_cols(g) if n in COL_SHARDED + CONV_SHARDED else g.reshape(-1, g.shape[-1])
    for n in REPLICATED[1:]:
        wt[n] = w_loc[n].reshape(1, -1)

    mod_cols = _mod_fwd(c_all, w_ada[0], name="mod_fwd")
    mod_all = _exchange([mod_cols], name="gather_mod", gather=True)[0]
    mod_mine = lax.dynamic_index_in_dim(mod_all, me, axis=1, keepdims=False).reshape(1, -1) + b_ada
    mod9 = mod_mine.reshape(9, d_model)

    loss_loc, dx, gw, small = _local_step(x[0], loss_target[0], mod9, wt)
    loss = lax.psum(loss_loc[0, 0], MESH_AXES)

    send = [(_stack_cols(gw[n].astype(BF16)) if n in COL_SHARDED
             else gw[n].astype(BF16).reshape((N_DEV, -1, gw[n].shape[-1]))) for n in big]
    recv = _exchange(send, name="scatter_grads", gather=False)
    res = {}
    for n, pieces in zip(big, recv):
        res[n] = _reduce_adamw(pieces, w_loc[n][0], m_loc[n][0], v_loc[n][0], name="adamw_" + n,
                               tr=REDUCE_ROWS[n])

    packed = _pad_lanes(jnp.concatenate([small[n].reshape(1, -1) for n in SMALL_ORDER], axis=1))
    parts = _exchange([packed], name="gather_small", gather=True)[0].reshape(N_DEV, -1)
    total = _sum_rows(parts, name="sum_small")
    off = 0
    gsmall = {}
    for n in SMALL_ORDER:
        size = small[n].size
        gsmall[n] = total[:, off:off + size]
        off += size
    dmod_all = parts[:, 0:9 * d_model]
    ada_cols = w_ada.shape[-1]
    dmod_cols = lax.dynamic_slice_in_dim(dmod_all, me * ada_cols, ada_cols, axis=1)
    res["w_ada"] = _ada_grad_adamw(c_all.T, dmod_cols, w_ada[0], m_w_ada[0], v_w_ada[0], name="adamw_w_ada")
    g_rows = {"b_ada": gsmall["mod"]}
    for n in REPLICATED[1:]:
        g_rows[n] = gsmall[n]
    for n in CONV_SHARDED:
        taps, width = w_loc[n].shape[1], w_loc[n].shape[2]
        full = gsmall[n].reshape(taps, -1)
        g_rows[n] = lax.dynamic_slice_in_dim(full, me * width, width, axis=1).reshape(1, -1)
    row_names = REPLICATED + CONV_SHARDED
    cat = lambda src: _pad_lanes(jnp.concatenate([src[n].reshape(1, -1) for n in row_names], axis=1))
    g_cat = cat(g_rows)
    d_cat, m_cat, v_cat = _adamw_rows(g_cat, cat(w_loc), cat(m_loc), cat(v_loc), name="adamw_small")
    off = 0
    for n in row_names:
        size = w_loc[n].size
        res[n] = tuple(t[:, off:off + size] for t in (g_cat, d_cat, m_cat, v_cat))
        off += size

    outs = [loss, dx[None]]
    for kind in range(4):
        for n in WEIGHTS:
            outs.append(res[n][kind].reshape(w_loc[n].shape))
    return tuple(outs)
```

```python
import functools
import math

import jax
import jax.numpy as jnp
from jax import lax
from jax.experimental import pallas as pl
from jax.experimental.pallas import tpu as pltpu

F32 = jnp.float32
BF16 = jnp.bfloat16
EPS = 1e-6
N_DEV = 8
CHUNK = 64
HEADS = 8
HEAD_DIM = 128
MESH_AXES = ("x", "y", "c")
VMEM_LIMIT_BYTES = 56 * 1024 * 1024

ADAM_LR = 0.001
ADAM_B1 = 0.9
ADAM_B2 = 0.999
ADAM_EPS = 1e-08
ADAM_WD = 0.01
ADAM_STEP = 10


def _params(sem=None):
    return pltpu.CompilerParams(dimension_semantics=sem, vmem_limit_bytes=VMEM_LIMIT_BYTES)


def _row(n):
    return pl.BlockSpec((1, n), lambda *_: (0, 0))


def _resident(shape):
    nd = len(shape)
    return pl.BlockSpec(shape, lambda *_: (0,) * nd, pipeline_mode=pl.Buffered(1))


def _sigmoid(x):
    return 1.0 / (1.0 + jnp.exp(-x))


def _dot(a, b):
    return jnp.dot(a, b, preferred_element_type=F32)


def _dot_nt(a, b):
    return lax.dot_general(a, b, (((1,), (1,)), ((), ())), preferred_element_type=F32)


def _dot_tn(a, b):
    return lax.dot_general(a, b, (((0,), (0,)), ((), ())), preferred_element_type=F32)


def _split_bf16(x):
    hi = x.astype(BF16)
    lo = (x - hi.astype(F32)).astype(BF16)
    return hi, lo


def _dot3(a, b, dot=_dot):
    ah, al = a if isinstance(a, tuple) else _split_bf16(a)
    bh, bl = b if isinstance(b, tuple) else _split_bf16(b)
    return dot(ah, bh) + dot(ah, bl) + dot(al, bh)


def _dot_exact(a, b):
    return jnp.dot(a, b, preferred_element_type=F32, precision=lax.Precision.HIGHEST)


def _my_position():
    return tuple(lax.axis_index(a) for a in MESH_AXES)


def _peer(pos, kk):
    return tuple((1 - p) if (kk >> (2 - b)) & 1 else p for b, p in enumerate(pos))


def _flat_index(pos):
    return pos[0] * 4 + pos[1] * 2 + pos[2]


_ANY = pl.BlockSpec(memory_space=pl.ANY)


def _exchange_copies(in_refs, out_refs, send_sems, recv_sems, local_sems, gather):
    pos = _my_position()
    me = _flat_index(pos)
    copies = []
    for t in range(len(in_refs)):
        own = in_refs[t] if gather else in_refs[t].at[me]
        copies.append(pltpu.make_async_copy(own, out_refs[t].at[me], local_sems.at[t]))
        for kk in range(1, N_DEV):
            peer = _peer(pos, kk)
            src = in_refs[t] if gather else in_refs[t].at[_flat_index(peer)]
            copies.append(pltpu.make_async_remote_copy(
                src_ref=src, dst_ref=out_refs[t].at[me],
                send_sem=send_sems.at[t, kk - 1], recv_sem=recv_sems.at[t, kk - 1],
                device_id=peer, device_id_type=pl.DeviceIdType.MESH))
    return copies


def _exchange_shapes(xs, gather):
    out_shape = [jax.ShapeDtypeStruct(((N_DEV,) + x.shape) if gather else x.shape, x.dtype) for x in xs]
    sems = [pltpu.SemaphoreType.DMA((len(xs), N_DEV - 1)), pltpu.SemaphoreType.DMA((len(xs), N_DEV - 1)),
            pltpu.SemaphoreType.DMA((len(xs),))]
    return out_shape, sems


def _exchange(xs, *, name, gather):
    nt = len(xs)

    def body(*refs):
        copies = _exchange_copies(refs[:nt], refs[nt:2 * nt], *refs[2 * nt:], gather)
        for cp in copies:
            cp.start()
        for cp in copies:
            cp.wait()

    out_shape, sems = _exchange_shapes(xs, gather)
    return pl.pallas_call(body, name=name, in_specs=[_ANY] * nt, out_specs=[_ANY] * nt, out_shape=out_shape,
                          scratch_shapes=sems)(*xs)


def _launch(body, carry, args, *, name, grid, in_specs, out_specs, out_shape, scratch_shapes=(), sem):
    single = not isinstance(out_shape, (list, tuple))
    out_specs = [out_specs] if single else list(out_specs)
    out_shape = [out_shape] if single else list(out_shape)
    if carry is None:
        outs = pl.pallas_call(body, name=name, grid=grid, in_specs=list(in_specs), out_specs=out_specs,
                              out_shape=out_shape, scratch_shapes=list(scratch_shapes),
                              compiler_params=_params(sem))(*args)
        return outs[0] if single else outs
    xs, gather = carry
    nt, n_in, n_out, n_scr = len(xs), len(args), len(out_shape), len(scratch_shapes)
    x_shape, sems = _exchange_shapes(xs, gather)

    def wrapped(*refs):
        c_in, x_in = refs[:n_in], refs[n_in:n_in + nt]
        c_out = refs[n_in + nt:n_in + nt + n_out]
        x_out = refs[n_in + nt + n_out:n_in + 2 * nt + n_out]
        scr = refs[n_in + 2 * nt + n_out:]
        ids = [pl.program_id(a) for a in range(len(grid))]
        first = functools.reduce(jnp.logical_and, [i == 0 for i in ids])
        last = functools.reduce(jnp.logical_and, [i == g - 1 for i, g in zip(ids, grid)])
        copies = lambda: _exchange_copies(x_in, x_out, *scr[n_scr:], gather)

        @pl.when(first)
        def _():
            for cp in copies():
                cp.start()

        body(*c_in, *c_out, *scr[:n_scr])

        @pl.when(last)
        def _():
            for cp in copies():
                cp.wait()

    outs = pl.pallas_call(
        wrapped, name=name, grid=grid, in_specs=list(in_specs) + [_ANY] * nt,
        out_specs=out_specs + [_ANY] * nt, out_shape=out_shape + x_shape,
        scratch_shapes=list(scratch_shapes) + sems,
        compiler_params=_params(("arbitrary",) * len(grid)))(*args, *xs)
    compute = outs[:n_out]
    return (compute[0] if single else compute), outs[n_out:]


def _norm_mod(x, nw, sc, sh):
    r = lax.rsqrt(jnp.mean(x * x, axis=-1, keepdims=True) + EPS)
    return (x * r * nw) * (1.0 + sc) + sh


def _norm_mod_bwd(x, nw, sc, du):
    r = lax.rsqrt(jnp.mean(x * x, axis=-1, keepdims=True) + EPS)
    xhat = x * r
    n = xhat * nw
    dsh = jnp.sum(du, axis=0, keepdims=True)
    dsc = jnp.sum(du * n, axis=0, keepdims=True)
    dn = du * (1.0 + sc)
    dnw = jnp.sum(dn * xhat, axis=0, keepdims=True)
    dxhat = dn * nw
    dx = r * (dxhat - xhat * jnp.mean(dxhat * xhat, axis=-1, keepdims=True))
    return dx, dnw, dsc, dsh


def _ffn_up_fwd(h, nw, sc, sh, wup, *, name, ts=512, tn=1408, carry=None):
    s, d = h.shape
    f_dim = wup.shape[1] // 2
    nj = f_dim // tn

    def body(h_ref, nw_ref, sc_ref, sh_ref, wa_ref, wb_ref, u_ref, a_ref, b_ref, f_ref):
        @pl.when(pl.program_id(1) == 0)
        def _():
            u_ref[...] = _norm_mod(h_ref[...], nw_ref[...], sc_ref[...], sh_ref[...]).astype(BF16)

        u = u_ref[...]
        a = _dot(u, wa_ref[...])
        b = _dot(u, wb_ref[...])
        a_ref[...] = a.astype(BF16)
        b_ref[...] = b.astype(BF16)
        f_ref[...] = (a * _sigmoid(a) * b).astype(BF16)

    return _launch(
        body, carry, (h, nw, sc, sh, wup, wup), name=name, grid=(s // ts, nj),
        in_specs=[pl.BlockSpec((ts, d), lambda i, j: (i, 0)), _row(d), _row(d), _row(d),
                  pl.BlockSpec((d, tn), lambda i, j: (0, j)),
                  pl.BlockSpec((d, tn), lambda i, j: (0, j + nj))],
        out_specs=[pl.BlockSpec((ts, d), lambda i, j: (i, 0)),
                   pl.BlockSpec((ts, tn), lambda i, j: (i, j)),
                   pl.BlockSpec((ts, tn), lambda i, j: (i, j)),
                   pl.BlockSpec((ts, tn), lambda i, j: (i, j))],
        out_shape=[jax.ShapeDtypeStruct((s, d), BF16)] + [jax.ShapeDtypeStruct((s, f_dim), BF16)] * 3,
        sem=("parallel", "arbitrary"))


def _ffn_down_fwd(f, wd, h, g, *, name, ts=512):
    s, f_dim = f.shape
    d = wd.shape[1]

    def body(f_ref, wd_ref, h_ref, g_ref, y_ref, ho_ref):
        y = _dot(f_ref[...], wd_ref[...])
        y_ref[...] = y.astype(BF16)
        ho_ref[...] = h_ref[...] + (0.5 * g_ref[...]) * y

    return pl.pallas_call(
        body, name=name, grid=(s // ts,),
        in_specs=[pl.BlockSpec((ts, f_dim), lambda i: (i, 0)), _resident((f_dim, d)),
                  pl.BlockSpec((ts, d), lambda i: (i, 0)), _row(d)],
        out_specs=[pl.BlockSpec((ts, d), lambda i: (i, 0)), pl.BlockSpec((ts, d), lambda i: (i, 0))],
        out_shape=[jax.ShapeDtypeStruct((s, d), BF16), jax.ShapeDtypeStruct((s, d), F32)],
        compiler_params=_params(("parallel",)),
    )(f, wd, h, g)


def _ffn_bwd_act(dh, g, y, a, b, wd, *, name, ts=256):
    s, d = dh.shape
    f_dim = a.shape[1]

    def body(dh_ref, g_ref, y_ref, a_ref, b_ref, wd_ref, dy_ref, dab_ref, dg_ref):
        dh_v = dh_ref[...]
        dy = ((0.5 * g_ref[...]) * dh_v).astype(BF16)
        dy_ref[...] = dy
        part = jnp.sum(0.5 * dh_v * y_ref[...].astype(F32), axis=0, keepdims=True)

        @pl.when(pl.program_id(0) == 0)
        def _():
            dg_ref[...] = jnp.zeros_like(dg_ref)

        dg_ref[...] += part
        df = _dot_nt(dy, wd_ref[...])
        av = a_ref[...].astype(F32)
        bv = b_ref[...].astype(F32)
        sg = _sigmoid(av)
        dab_ref[:, :f_dim] = (df * bv * (sg * (1.0 + av * (1.0 - sg)))).astype(BF16)
        dab_ref[:, f_dim:] = (df * (av * sg)).astype(BF16)

    return pl.pallas_call(
        body, name=name, grid=(s // ts,),
        in_specs=[pl.BlockSpec((ts, d), lambda i: (i, 0)), _row(d),
                  pl.BlockSpec((ts, d), lambda i: (i, 0)),
                  pl.BlockSpec((ts, f_dim), lambda i: (i, 0)),
                  pl.BlockSpec((ts, f_dim), lambda i: (i, 0)),
                  _resident((f_dim, d))],
        out_specs=[pl.BlockSpec((ts, d), lambda i: (i, 0)),
                   pl.BlockSpec((ts, 2 * f_dim), lambda i: (i, 0)), _row(d)],
        out_shape=[jax.ShapeDtypeStruct((s, d), BF16), jax.ShapeDtypeStruct((s, 2 * f_dim), BF16),
                   jax.ShapeDtypeStruct((1, d), F32)],
        compiler_params=_params(("arbitrary",)),
    )(dh, g, y, a, b, wd)


def _norm_mod_matmul_bwd(pairs, h, nw, sc, dh_in, *, name, ts=256, carry=None):
    s, d = h.shape
    n_pairs = len(pairs)

    def body(*refs):
        dx_refs = refs[:n_pairs]
        w_refs = refs[n_pairs:2 * n_pairs]
        h_ref, nw_ref, sc_ref, dhi_ref, dho_ref, dnw_ref, dsc_ref, dsh_ref = refs[2 * n_pairs:]
        du = _dot_nt(dx_refs[0][...], w_refs[0][...])
        for k in range(1, n_pairs):
            du = du + _dot_nt(dx_refs[k][...], w_refs[k][...])
        dx, dnw, dsc, dsh = _norm_mod_bwd(h_ref[...], nw_ref[...], sc_ref[...], du)
        dho_ref[...] = dhi_ref[...] + dx

        @pl.when(pl.program_id(0) == 0)
        def _():
            dnw_ref[...] = jnp.zeros_like(dnw_ref)
            dsc_ref[...] = jnp.zeros_like(dsc_ref)
            dsh_ref[...] = jnp.zeros_like(dsh_ref)

        dnw_ref[...] += dnw
        dsc_ref[...] += dsc
        dsh_ref[...] += dsh

    dxs = [p[0] for p in pairs]
    ws = [p[1] for p in pairs]
    tile = pl.BlockSpec((ts, d), lambda i: (i, 0))
    return _launch(
        body, carry, (*dxs, *ws, h, nw, sc, dh_in), name=name, grid=(s // ts,),
        in_specs=([pl.BlockSpec((ts, x.shape[1]), lambda i: (i, 0)) for x in dxs]
                  + [_resident(w.shape) for w in ws] + [tile, _row(d), _row(d), tile]),
        out_specs=[tile, _row(d), _row(d), _row(d)],
        out_shape=[jax.ShapeDtypeStruct((s, d), F32)] + [jax.ShapeDtypeStruct((1, d), F32)] * 3,
        sem=("arbitrary",))


def _matmul_tn(a, b, *, name, tm, tn, tk=512, carry=None):
    s, m = a.shape
    n = b.shape[1]
    nk = s // tk

    def body(a_ref, b_ref, o_ref, acc_ref):
        k = pl.program_id(2)

        @pl.when(k == 0)
        def _():
            acc_ref[...] = jnp.zeros_like(acc_ref)

        acc_ref[...] += _dot_tn(a_ref[...], b_ref[...])

        @pl.when(k == nk - 1)
        def _():
            o_ref[...] = acc_ref[...]

    return _launch(
        body, carry, (a, b), name=name, grid=(m // tm, n // tn, nk),
        in_specs=[pl.BlockSpec((tk, tm), lambda i, j, k: (k, i)),
                  pl.BlockSpec((tk, tn), lambda i, j, k: (k, j))],
        out_specs=pl.BlockSpec((tm, tn), lambda i, j, k: (i, j)),
        out_shape=jax.ShapeDtypeStruct((m, n), F32),
        scratch_shapes=[pltpu.VMEM((tm, tn), F32)],
        sem=("parallel", "parallel", "arbitrary"))


def _in_proj_fwd(h, nw, sc, sh, w_main, w_ba, *, name, ts=512, tn=1536, carry=None):
    s, d = h.shape
    n_main = w_main.shape[1]
    n_ba = w_ba.shape[1]

    def body(h_ref, nw_ref, sc_ref, sh_ref, w_ref, wba_ref, u_ref, p_ref, ba_ref):
        @pl.when(pl.program_id(1) == 0)
        def _():
            u0 = _norm_mod(h_ref[...], nw_ref[...], sc_ref[...], sh_ref[...]).astype(BF16)
            u_ref[...] = u0
            ba_ref[...] = _dot(u0, wba_ref[...])

        p_ref[...] = _dot(u_ref[...], w_ref[...]).astype(BF16)

    return _launch(
        body, carry, (h, nw, sc, sh, w_main, w_ba), name=name, grid=(s // ts, n_main // tn),
        in_specs=[pl.BlockSpec((ts, d), lambda i, j: (i, 0)), _row(d), _row(d), _row(d),
                  pl.BlockSpec((d, tn), lambda i, j: (0, j)), _resident((d, n_ba))],
        out_specs=[pl.BlockSpec((ts, d), lambda i, j: (i, 0)),
                   pl.BlockSpec((ts, tn), lambda i, j: (i, j)),
                   pl.BlockSpec((ts, n_ba), lambda i, j: (i, 0))],
        out_shape=[jax.ShapeDtypeStruct((s, d), BF16), jax.ShapeDtypeStruct((s, n_main), BF16),
                   jax.ShapeDtypeStruct((s, n_ba), F32)],
        sem=("parallel", "arbitrary"))


QKV_W = 3 * HEADS * HEAD_DIM
Z_OFF, Z_W = 3072, 1024
GATE_OFF, GATE_W = 4096, 2048
A_OFF, A_W = 6144, 1536
N_MAIN = 7680
CONV_A = 512
BA_W = 128

L_BETA, L_G, L_EG, L_EKD, L_EGC = 0, 8, 16, 24, 32


def _softplus(z):
    e = jnp.exp(-jnp.abs(z))
    small = e * (1.0 - e * (0.5 - e * (1.0 / 3.0)))
    return jnp.maximum(z, 0.0) + jnp.where(e < 1e-3, small, jnp.log(1.0 + e))


def _tri(n, sgn, strict=False):
    i = lax.broadcasted_iota(jnp.int32, (n, n), 0)
    j = lax.broadcasted_iota(jnp.int32, (n, n), 1)
    dlt = (i - j) * sgn
    return (dlt > 0) if strict else (dlt >= 0)


def _scal_fwd(ba, alog, dtb, *, name, ts=512):
    s = ba.shape[0]

    def body(ba_ref, al_ref, dt_ref, o_ref):
        d = pl.program_id(0)
        sgn = 1 - 2 * d
        x = ba_ref[...]
        lane = lax.broadcasted_iota(jnp.int32, x.shape, 1)
        beta = _sigmoid(x)
        g = -jnp.exp(al_ref[0]) * _softplus(x + dt_ref[0])
        g = jnp.where((lane >= L_G) & (lane < L_EGC + 8), g, 0.0)
        ltri = jnp.where(_tri(CHUNK, sgn), 1.0, 0.0).astype(F32)
        for c in range(ts // CHUNK):
            rows = slice(c * CHUNK, (c + 1) * CHUNK)
            gc = _dot_exact(ltri, g[rows])
            g_end = jnp.where(d == 0, gc[CHUNK - 1:CHUNK], gc[0:1])
            ln = lane[rows]
            out = jnp.where(ln < L_G, beta[rows],
                  jnp.where(ln < L_EG, gc,
                  jnp.where(ln < L_EKD, jnp.exp(gc),
                  jnp.where(ln < L_EGC, jnp.exp(g_end - gc),
                  jnp.where(ln < L_EGC + 8, jnp.broadcast_to(jnp.exp(g_end), gc.shape), 0.0)))))
            o_ref[0, rows, :] = out

    return pl.pallas_call(
        body, name=name, grid=(2, s // ts),
        in_specs=[pl.BlockSpec((ts, BA_W), lambda d, i: (i, d)),
                  pl.BlockSpec((1, 1, BA_W), lambda d, i: (d, 0, 0)),
                  pl.BlockSpec((1, 1, BA_W), lambda d, i: (d, 0, 0))],
        out_specs=pl.BlockSpec((1, ts, BA_W), lambda d, i: (d, i, 0)),
        out_shape=jax.ShapeDtypeStruct((2, s, BA_W), F32),
        compiler_params=_params(("parallel", "parallel")),
    )(ba, alog, dtb)


def _scal_bwd(dscal, drow, ba, alog, dtb, *, name, ts=512):
    s = ba.shape[0]

    def body(ds_ref, dr_ref, ba_ref, al_ref, dt_ref, dba_ref, dal_ref, ddt_ref):
        d = pl.program_id(0)
        sgn = 1 - 2 * d
        x = ba_ref[...]
        lane = lax.broadcasted_iota(jnp.int32, x.shape, 1)
        in_g = (lane >= L_G) & (lane < L_G + 8)
        beta = _sigmoid(x)
        z = x + dt_ref[0]
        neg_a = -jnp.exp(al_ref[0])
        g = neg_a * _softplus(z)
        dsv = ds_ref[0]
        dgc = jnp.where(in_g, dsv + dr_ref[0], 0.0)
        utri = jnp.where(_tri(CHUNK, -sgn), 1.0, 0.0).astype(F32)
        dal = jnp.zeros((1, BA_W), F32)
        ddt = jnp.zeros((1, BA_W), F32)
        for c in range(ts // CHUNK):
            rows = slice(c * CHUNK, (c + 1) * CHUNK)
            dg = _dot_exact(utri, dgc[rows])
            dz = dg * neg_a * _sigmoid(z[rows])
            dal = dal + jnp.sum(dg * g[rows], axis=0, keepdims=True)
            ddt = ddt + jnp.sum(dz, axis=0, keepdims=True)
            b = beta[rows]
            out = jnp.where(lane[rows] < L_G, dsv[rows] * b * (1.0 - b), jnp.where(in_g[rows], dz, 0.0))
            dba_ref[rows, :] = out.astype(BF16)

        @pl.when(pl.program_id(1) == 0)
        def _():
            dal_ref[...] = jnp.zeros_like(dal_ref)
            ddt_ref[...] = jnp.zeros_like(ddt_ref)

        dal_ref[0] += dal
        ddt_ref[0] += ddt

    row3 = pl.BlockSpec((1, 1, BA_W), lambda d, i: (d, 0, 0))
    tok3 = pl.BlockSpec((1, ts, BA_W), lambda d, i: (d, i, 0))
    return pl.pallas_call(
        body, name=name, grid=(2, s // ts),
        in_specs=[tok3, tok3, pl.BlockSpec((ts, BA_W), lambda d, i: (i, d)), row3, row3],
        out_specs=[pl.BlockSpec((ts, BA_W), lambda d, i: (i, d)), row3, row3],
        out_shape=[jax.ShapeDtypeStruct((s, 2 * BA_W), BF16), jax.ShapeDtypeStruct((2, 1, BA_W), F32),
                   jax.ShapeDtypeStruct((2, 1, BA_W), F32)],
        compiler_params=_params(("arbitrary", "arbitrary")),
    )(dscal, drow, ba, alog, dtb)


HALO = 16


def _halo_specs(ts, width, col_block, n_rows, rows=HALO):
    r = ts // rows
    last = n_rows // rows - 1
    return [pl.BlockSpec((rows, width), lambda i: (jnp.maximum(i * r - 1, 0), col_block)),
            pl.BlockSpec((ts, width), lambda i: (i, col_block)),
            pl.BlockSpec((rows, width), lambda i: (jnp.minimum((i + 1) * r, last), col_block))]


def _fill_halo(dst_ref, prev_ref, cur_ref, next_ref, first, last, fn=lambda r: r[...].astype(F32)):
    h = prev_ref.shape[0]
    ts = cur_ref.shape[0]
    p = fn(prev_ref)
    n = fn(next_ref)
    dst_ref[0:h, :] = jnp.where(first, 0.0, p)
    dst_ref[h:h + ts, :] = fn(cur_ref)
    dst_ref[h + ts:h + ts + h, :] = jnp.where(last, 0.0, n)


def _dwconv_rows(src_ref, w, start, n_rows, cols):
    acc = w[0:1, :] * src_ref[start:start + n_rows, cols]
    for i in range(1, w.shape[0]):
        acc = acc + w[i:i + 1, :] * src_ref[start + i:start + i + n_rows, cols]
    return acc


def _l2norm_heads(act, scale):
    outs = []
    for hd in range(HEADS):
        seg = act[:, hd * HEAD_DIM:(hd + 1) * HEAD_DIM]
        outs.append(seg * (lax.rsqrt(jnp.sum(seg * seg, axis=-1, keepdims=True) + EPS) * scale))
    return jnp.concatenate(outs, axis=-1)


Q_SCALE = HEAD_DIM ** -0.5


def _conv_fwd(proj, conv_dn, conv_a, *, name, ts=256):
    s = proj.shape[0]
    hd = HEADS * HEAD_DIM
    nt = s // ts

    def body(qp_ref, qc_ref, qn_ref, ap_ref, ac_ref, an_ref, wdn_ref, wa_ref,
             q_ref, k_ref, v_ref, ya_ref, xs_ref, xa_ref):
        i = pl.program_id(0)
        first, last = i == 0, i == nt - 1
        _fill_halo(xs_ref, qp_ref, qc_ref, qn_ref, first, last)
        wdn = wdn_ref[...]
        for part, o_ref in enumerate((q_ref, k_ref, v_ref)):
            cols = slice(part * hd, (part + 1) * hd)
            pre = _dwconv_rows(xs_ref, wdn[:, cols], HALO - 2, ts, cols)
            act = pre * _sigmoid(pre)
            if part == 0:
                act = _l2norm_heads(act, Q_SCALE)
            elif part == 1:
                act = _l2norm_heads(act, 1.0)
            o_ref[...] = act
        cv = lambda r: r[:, CONV_A:2 * CONV_A].astype(F32) * r[:, 2 * CONV_A:].astype(F32)
        _fill_halo(xa_ref, ap_ref, ac_ref, an_ref, first, last, fn=cv)
        conv = _dwconv_rows(xa_ref, wa_ref[...], HALO - 1, ts, slice(0, CONV_A))
        ya_ref[...] = (ac_ref[:, 0:CONV_A].astype(F32) * conv).astype(BF16)

    tile = lambda w: pl.BlockSpec((ts, w), lambda i: (i, 0))
    return pl.pallas_call(
        body, name=name, grid=(nt,),
        in_specs=(_halo_specs(ts, QKV_W, 0, s) + _halo_specs(ts, A_W, A_OFF // A_W, s)
                  + [_resident(conv_dn.shape), _resident(conv_a.shape)]),
        out_specs=[tile(hd), tile(hd), tile(hd), tile(CONV_A)],
        out_shape=[jax.ShapeDtypeStruct((s, hd), F32)] * 3 + [jax.ShapeDtypeStruct((s, CONV_A), BF16)],
        scratch_shapes=[pltpu.VMEM((ts + 2 * HALO, QKV_W), F32), pltpu.VMEM((ts + 2 * HALO, CONV_A), F32)],
        compiler_params=_params(("parallel",)),
    )(proj, proj, proj, proj, proj, proj, conv_dn, conv_a)


def _chunk_of_step(d, c, n):
    return c + d * (n - 1 - 2 * c)


def _head_scalars(scv, grv, hd):
    col = lambda base: scv[:, base + hd:base + hd + 1]
    return (col(L_BETA), col(L_G), col(L_EG), col(L_EKD),
            scv[0:1, L_EGC + hd:L_EGC + hd + 1], grv[hd:hd + 1, :])


def _decay_matrix(gcol, grow, incl):
    return jnp.where(incl, jnp.exp(jnp.minimum(gcol - grow, 0.0)), 0.0)


INV_BASE = 8


def _unit_lower_inverse(a_m, top=None):
    n = a_m.shape[0]
    top = top or n
    i = lax.broadcasted_iota(jnp.int32, (n, n), 0)
    j = lax.broadcasted_iota(jnp.int32, (n, n), 1)

    def same_block(m):
        sh = int(math.log2(m))
        return jnp.right_shift(i, sh) == jnp.right_shift(j, sh)

    x = jnp.where(same_block(INV_BASE), -a_m, 0.0)
    t = jnp.where(i == j, 1.0, 0.0) + x
    p = x
    for _ in range(int(math.log2(INV_BASE)) - 1):
        p_b = p.astype(BF16)
        p = _dot(p_b, p_b)
        t = t + _dot(t.astype(BF16), p.astype(BF16))
    m = INV_BASE
    while m < top:
        join = jnp.where(same_block(2 * m) & jnp.logical_not(same_block(m)), a_m, 0.0)
        t_b = t.astype(BF16)
        t = t - _dot(_dot(t_b, join.astype(BF16)).astype(BF16), t_b)
        m *= 2
    return t


def _delta_fwd_per_head(q, k, v, scal, grow, *, name):
    s = q.shape[0]
    n = s // CHUNK
    hd_all = HEADS * HEAD_DIM

    def body(q_ref, k_ref, v_ref, sc_ref, gr_ref, o_ref, st_ref, t_ref, vn_ref, state):
        d = pl.program_id(0)
        sgn = 1 - 2 * d

        @pl.when(pl.program_id(1) == 0)
        def _():
            state[...] = jnp.zeros_like(state)

        incl = _tri(CHUNK, sgn)
        strict = _tri(CHUNK, sgn, strict=True)
        scv = sc_ref[0]
        grv = gr_ref[0, 0]
        for hd in range(HEADS):
            cols = slice(hd * HEAD_DIM, (hd + 1) * HEAD_DIM)
            qh, kh, vh = q_ref[:, cols], k_ref[:, cols], v_ref[:, cols]
            beta, gcol, eg, ekd, egc, grow_h = _head_scalars(scv, grv, hd)
            dm = _decay_matrix(gcol, grow_h, incl)
            k_b = kh.astype(BF16)
            kk = _dot_nt((kh * beta).astype(BF16), k_b)
            t = _unit_lower_inverse(jnp.where(strict, kk * dm, 0.0))
            p_m = jnp.where(incl, _dot_nt(qh.astype(BF16), k_b) * dm, 0.0)
            sh = state[hd]
            sh_b = sh.astype(BF16)
            st_ref[0, 0, hd] = sh_b
            r = vh - _dot((kh * eg).astype(BF16), sh_b)
            vn = _dot3(t, beta * r)
            vn_b = vn.astype(BF16)
            o_ref[0, :, cols] = _dot((qh * eg).astype(BF16), sh_b) + _dot(p_m.astype(BF16), vn_b)
            state[hd] = egc * sh + _dot_tn((kh * ekd).astype(BF16), vn_b)
            t_ref[0, 0, hd] = t
            vn_ref[0, :, cols] = vn_b

    tok = lambda d, c: (_chunk_of_step(d, c, n), 0)
    dtok = lambda d, c: (d, _chunk_of_step(d, c, n), 0)
    dchunk4 = lambda d, c: (d, _chunk_of_step(d, c, n), 0, 0)
    dchunk5 = lambda d, c: (d, _chunk_of_step(d, c, n), 0, 0, 0)
    return pl.pallas_call(
        body, name=name, grid=(2, n),
        in_specs=[pl.BlockSpec((CHUNK, hd_all), tok)] * 3
                 + [pl.BlockSpec((1, CHUNK, BA_W), dtok), pl.BlockSpec((1, 1, HEADS, CHUNK), dchunk4)],
        out_specs=[pl.BlockSpec((1, CHUNK, hd_all), dtok),
                   pl.BlockSpec((1, 1, HEADS, HEAD_DIM, HEAD_DIM), dchunk5),
                   pl.BlockSpec((1, 1, HEADS, CHUNK, CHUNK), dchunk5),
                   pl.BlockSpec((1, CHUNK, hd_all), dtok)],
        out_shape=[jax.ShapeDtypeStruct((2, s, hd_all), F32),
                   jax.ShapeDtypeStruct((2, n, HEADS, HEAD_DIM, HEAD_DIM), BF16),
                   jax.ShapeDtypeStruct((2, n, HEADS, CHUNK, CHUNK), F32),
                   jax.ShapeDtypeStruct((2, s, hd_all), BF16)],
        scratch_shapes=[pltpu.VMEM((HEADS, HEAD_DIM, HEAD_DIM), F32)],
        compiler_params=_params(("arbitrary", "arbitrary")),
    )(q, k, v, scal, grow)


def _delta_bwd_per_head(q, k, v, scal, grow, states, tinv, vn, do, *, name):
    s = q.shape[0]
    n = s // CHUNK
    hd_all = HEADS * HEAD_DIM

    def body(q_ref, k_ref, v_ref, sc_ref, gr_ref, st_ref, t_ref, vn_ref, do_ref,
             dq_ref, dk_ref, dv_ref, dsc_ref, dgr_ref, dstate):
        d = pl.program_id(0)
        sgn = 1 - 2 * d

        @pl.when(pl.program_id(1) == 0)
        def _():
            dstate[...] = jnp.zeros_like(dstate)

        incl = _tri(CHUNK, sgn)
        strict = _tri(CHUNK, sgn, strict=True)
        scv = sc_ref[0]
        grv = gr_ref[0, 0]
        lane = lax.broadcasted_iota(jnp.int32, (CHUNK, BA_W), 1)
        row = lax.broadcasted_iota(jnp.int32, (CHUNK, 1), 0)
        sub = lax.broadcasted_iota(jnp.int32, (HEADS, CHUNK), 0)
        end_row = jnp.where(d == 0, CHUNK - 1, 0)
        dsc_acc = jnp.zeros((CHUNK, BA_W), F32)
        dgr_acc = jnp.zeros((HEADS, CHUNK), F32)
        for hd in range(HEADS):
            cols = slice(hd * HEAD_DIM, (hd + 1) * HEAD_DIM)
            qh, kh, vh = q_ref[:, cols], k_ref[:, cols], v_ref[:, cols]
            beta, gcol, eg, ekd, egc, grow_h = _head_scalars(scv, grv, hd)
            dm = _decay_matrix(gcol, grow_h, incl)
            q_b, k_b = qh.astype(BF16), kh.astype(BF16)
            kb_b = (kh * beta).astype(BF16)
            kk = _dot_nt(kb_b, k_b)
            qk = _dot_nt(q_b, k_b)
            p_m = jnp.where(incl, qk * dm, 0.0)
            t = t_ref[0, 0, hd]
            vn_b = vn_ref[0, :, cols]
            sh_b = st_ref[0, 0, hd]
            dsp = dstate[hd]
            dsp_b = dsp.astype(BF16)
            do_b = do_ref[:, cols].astype(BF16)
            kg, qg, kd = kh * eg, qh * eg, kh * ekd
            kg_b, qg_b, kd_b = kg.astype(BF16), qg.astype(BF16), kd.astype(BF16)
            r = vh - _dot(kg_b, sh_b)
            dvn = _dot_tn(p_m.astype(BF16), do_b) + _dot(kd_b, dsp_b)
            db = _dot3(t, dvn, dot=_dot_tn)
            dr = db * beta
            dbeta = jnp.sum(db * r, axis=-1, keepdims=True)
            dr_b, db_b = dr.astype(BF16), db.astype(BF16)
            dkg = -_dot_nt(dr_b, sh_b)
            dqg = _dot_nt(do_b, sh_b)
            dkd = _dot_nt(vn_b, dsp_b)
            dpm = jnp.where(incl, _dot_nt(do_b, vn_b), 0.0) * dm
            dam = jnp.where(strict, -_dot_nt(db_b, vn_b), 0.0) * dm
            dpm_b, dam_b = dpm.astype(BF16), dam.astype(BF16)
            dkb = _dot(dam_b, k_b)
            dq_ref[0, :, cols] = dqg * eg + _dot(dpm_b, k_b)
            dk_ref[0, :, cols] = (dkg * eg + dkd * ekd + _dot_tn(dpm_b, q_b) + _dot_tn(dam_b, kb_b)
                                  + dkb * beta)
            dv_ref[0, :, cols] = dr
            dbeta = dbeta + jnp.sum(dkb * kh, axis=-1, keepdims=True)
            m = dpm * qk + dam * kk
            kd_term = jnp.sum(dkd * kd, axis=-1, keepdims=True)
            dgcol = (jnp.sum(dqg * qg, axis=-1, keepdims=True) + jnp.sum(dkg * kg, axis=-1, keepdims=True)
                     - kd_term + jnp.sum(m, axis=-1, keepdims=True))
            dg_end = jnp.sum(kd_term) + egc * jnp.sum(dsp * sh_b.astype(F32))
            dgcol = dgcol + jnp.where(row == end_row, dg_end, 0.0)
            dsc_acc = jnp.where(lane == L_BETA + hd, dbeta, dsc_acc)
            dsc_acc = jnp.where(lane == L_G + hd, dgcol, dsc_acc)
            dgr_acc = jnp.where(sub == hd, -jnp.sum(m, axis=0, keepdims=True), dgr_acc)
            dstate[hd] = _dot_tn(qg_b, do_b) + egc * dsp - _dot_tn(kg_b, dr_b)
        dsc_ref[0] = dsc_acc
        dgr_ref[0, 0] = dgr_acc

    step = lambda d, c: n - 1 - _chunk_of_step(d, c, n)
    tok = lambda d, c: (step(d, c), 0)
    dtok = lambda d, c: (d, step(d, c), 0)
    dchunk4 = lambda d, c: (d, step(d, c), 0, 0)
    dchunk5 = lambda d, c: (d, step(d, c), 0, 0, 0)
    tok_spec = pl.BlockSpec((CHUNK, hd_all), tok)
    dtok_spec = pl.BlockSpec((1, CHUNK, hd_all), dtok)
    return pl.pallas_call(
        body, name=name, grid=(2, n),
        in_specs=[tok_spec] * 3
                 + [pl.BlockSpec((1, CHUNK, BA_W), dtok), pl.BlockSpec((1, 1, HEADS, CHUNK), dchunk4),
                    pl.BlockSpec((1, 1, HEADS, HEAD_DIM, HEAD_DIM), dchunk5),
                    pl.BlockSpec((1, 1, HEADS, CHUNK, CHUNK), dchunk5), dtok_spec, tok_spec],
        out_specs=[dtok_spec] * 3
                  + [pl.BlockSpec((1, CHUNK, BA_W), dtok), pl.BlockSpec((1, 1, HEADS, CHUNK), dchunk4)],
        out_shape=[jax.ShapeDtypeStruct((2, s, hd_all), F32)] * 3
                  + [jax.ShapeDtypeStruct((2, s, BA_W), F32), jax.ShapeDtypeStruct((2, n, HEADS, CHUNK), F32)],
        scratch_shapes=[pltpu.VMEM((HEADS, HEAD_DIM, HEAD_DIM), F32)],
        compiler_params=_params(("arbitrary", "arbitrary")),
    )(q, k, v, scal, grow, states, tinv, vn, do)


GROUP = 4
GROWS = GROUP * CHUNK
N_GROUPS = HEADS // GROUP


def _stack(parts):
    return jnp.concatenate(parts, axis=0)


def _group_masks(sgn):
    i = lax.broadcasted_iota(jnp.int32, (GROWS, GROWS), 0)
    j = lax.broadcasted_iota(jnp.int32, (GROWS, GROWS), 1)
    same = jnp.right_shift(i, 6) == jnp.right_shift(j, 6)
    dlt = (i - j) * sgn
    return same & (dlt >= 0), same & (dlt > 0)


def _group_operands(q_ref, k_ref, v_ref, scv, grp):
    heads = [GROUP * grp + t for t in range(GROUP)]
    tiles = lambda ref: [ref[:, h * HEAD_DIM:(h + 1) * HEAD_DIM] for h in heads]
    col = lambda base: [scv[:, base + h:base + h + 1] for h in heads]
    egc = [scv[0:1, L_EGC + h:L_EGC + h + 1] for h in heads]
    return heads, tiles(q_ref), tiles(k_ref), tiles(v_ref), col(L_BETA), col(L_G), col(L_EG), col(L_EKD), egc


def _delta_fwd(q, k, v, scal, grow, *, name):
    s = q.shape[0]
    n = s // CHUNK
    hd_all = HEADS * HEAD_DIM

    def body(q_ref, k_ref, v_ref, sc_ref, gr_ref, o_ref, st_ref, t_ref, vn_ref, state):
        d = pl.program_id(0)
        sgn = 1 - 2 * d

        @pl.when(pl.program_id(1) == 0)
        def _():
            state[...] = jnp.zeros_like(state)

        incl, strict = _group_masks(sgn)
        scv = sc_ref[0]
        for grp in range(N_GROUPS):
            heads, qs, ks, vs, beta, gcol, eg, ekd, egc = _group_operands(q_ref, k_ref, v_ref, scv, grp)
            dm = _decay_matrix(_stack(gcol), gr_ref[0, 0, grp:grp + 1, :], incl)
            k_b = _stack(ks).astype(BF16)
            kk = _dot_nt(_stack([ks[t] * beta[t] for t in range(GROUP)]).astype(BF16), k_b)
            tinv = _unit_lower_inverse(jnp.where(strict, kk * dm, 0.0), top=CHUNK).astype(BF16)
            t_ref[0, 0, grp] = tinv
            p_m = jnp.where(incl, _dot_nt(_stack(qs).astype(BF16), k_b) * dm, 0.0)
            sh, sh_b, br = [], [], []
            for t, h in enumerate(heads):
                sh.append(state[h])
                sh_b.append(sh[t].astype(BF16))
                st_ref[0, 0, h] = sh_b[t]
                br.append(beta[t] * (vs[t] - _dot((ks[t] * eg[t]).astype(BF16), sh_b[t])))
            vn_b = _dot(tinv, _stack(br).astype(BF16)).astype(BF16)
            o_intra = _dot(p_m.astype(BF16), vn_b)
            for t, h in enumerate(heads):
                rows = slice(t * CHUNK, (t + 1) * CHUNK)
                cols = slice(h * HEAD_DIM, (h + 1) * HEAD_DIM)
                o_ref[0, :, cols] = _dot((qs[t] * eg[t]).astype(BF16), sh_b[t]) + o_intra[rows]
                state[h] = egc[t] * sh[t] + _dot_tn((ks[t] * ekd[t]).astype(BF16), vn_b[rows])
                vn_ref[0, :, cols] = vn_b[rows]

    tok = lambda d, c: (_chunk_of_step(d, c, n), 0)
    dtok = lambda d, c: (d, _chunk_of_step(d, c, n), 0)
    dchunk4 = lambda d, c: (d, _chunk_of_step(d, c, n), 0, 0)
    dchunk5 = lambda d, c: (d, _chunk_of_step(d, c, n), 0, 0, 0)
    return pl.pallas_call(
        body, name=name, grid=(2, n),
        in_specs=[pl.BlockSpec((CHUNK, hd_all), tok)] * 3
                 + [pl.BlockSpec((1, CHUNK, BA_W), dtok), pl.BlockSpec((1, 1, N_GROUPS, GROWS), dchunk4)],
        out_specs=[pl.BlockSpec((1, CHUNK, hd_all), dtok),
                   pl.BlockSpec((1, 1, HEADS, HEAD_DIM, HEAD_DIM), dchunk5),
                   pl.BlockSpec((1, 1, N_GROUPS, GROWS, GROWS), dchunk5),
                   pl.BlockSpec((1, CHUNK, hd_all), dtok)],
        out_shape=[jax.ShapeDtypeStruct((2, s, hd_all), F32),
                   jax.ShapeDtypeStruct((2, n, HEADS, HEAD_DIM, HEAD_DIM), BF16),
                   jax.ShapeDtypeStruct((2, n, N_GROUPS, GROWS, GROWS), BF16),
                   jax.ShapeDtypeStruct((2, s, hd_all), BF16)],
        scratch_shapes=[pltpu.VMEM((HEADS, HEAD_DIM, HEAD_DIM), F32)],
        compiler_params=_params(("arbitrary", "arbitrary")),
    )(q, k, v, scal, grow)


def _delta_bwd(q, k, v, scal, grow, states, tinv, vn, do, *, name, carry=None):
    s = q.shape[0]
    n = s // CHUNK
    hd_all = HEADS * HEAD_DIM

    def body(q_ref, k_ref, v_ref, sc_ref, gr_ref, st_ref, t_ref, vn_ref, do_ref,
             dq_ref, dk_ref, dv_ref, dsc_ref, dgr_ref, dstate):
        d = pl.program_id(0)
        sgn = 1 - 2 * d

        @pl.when(pl.program_id(1) == 0)
        def _():
            dstate[...] = jnp.zeros_like(dstate)

        incl, strict = _group_masks(sgn)
        scv = sc_ref[0]
        lane = lax.broadcasted_iota(jnp.int32, (CHUNK, BA_W), 1)
        row = lax.broadcasted_iota(jnp.int32, (CHUNK, 1), 0)
        end_row = jnp.where(d == 0, CHUNK - 1, 0)
        dsc_acc = jnp.zeros((CHUNK, BA_W), F32)
        for grp in range(N_GROUPS):
            heads, qs, ks, vs, beta, gcol, eg, ekd, egc = _group_operands(q_ref, k_ref, v_ref, scv, grp)
            dm = _decay_matrix(_stack(gcol), gr_ref[0, 0, grp:grp + 1, :], incl)
            beta_st, eg_st, ekd_st = _stack(beta), _stack(eg), _stack(ekd)
            q_st, k_st = _stack(qs), _stack(ks)
            q_b, k_b = q_st.astype(BF16), k_st.astype(BF16)
            kb_b = (k_st * beta_st).astype(BF16)
            kk = _dot_nt(kb_b, k_b)
            qk = _dot_nt(q_b, k_b)
            p_m = jnp.where(incl, qk * dm, 0.0)
            kg_st, qg_st, kd_st = k_st * eg_st, q_st * eg_st, k_st * ekd_st
            kg_b, qg_b, kd_b = kg_st.astype(BF16), qg_st.astype(BF16), kd_st.astype(BF16)
            tok_cols = [slice(h * HEAD_DIM, (h + 1) * HEAD_DIM) for h in heads]
            grp_rows = [slice(t * CHUNK, (t + 1) * CHUNK) for t in range(GROUP)]
            vn_b = _stack([vn_ref[0, :, c] for c in tok_cols])
            do_b = _stack([do_ref[:, c] for c in tok_cols]).astype(BF16)
            sh_b = [st_ref[0, 0, h] for h in heads]
            dsp = [dstate[h] for h in heads]
            dsp_b = [x.astype(BF16) for x in dsp]
            r_st = _stack([vs[t] - _dot(kg_b[grp_rows[t]], sh_b[t]) for t in range(GROUP)])
            dvn = _dot_tn(p_m.astype(BF16), do_b) + _stack(
                [_dot(kd_b[grp_rows[t]], dsp_b[t]) for t in range(GROUP)])
            db = _dot_tn(t_ref[0, 0, grp], dvn.astype(BF16))
            dr = db * beta_st
            dbeta = jnp.sum(db * r_st, axis=-1, keepdims=True)
            dr_b, db_b = dr.astype(BF16), db.astype(BF16)
            dkg = -_stack([_dot_nt(dr_b[grp_rows[t]], sh_b[t]) for t in range(GROUP)])
            dqg = _stack([_dot_nt(do_b[grp_rows[t]], sh_b[t]) for t in range(GROUP)])
            dkd = _stack([_dot_nt(vn_b[grp_rows[t]], dsp_b[t]) for t in range(GROUP)])
            dpm = jnp.where(incl, _dot_nt(do_b, vn_b), 0.0) * dm
            dam = jnp.where(strict, -_dot_nt(db_b, vn_b), 0.0) * dm
            dpm_b, dam_b = dpm.astype(BF16), dam.astype(BF16)
            dkb = _dot(dam_b, k_b)
            dq_st = dqg * eg_st + _dot(dpm_b, k_b)
            dk_st = (dkg * eg_st + dkd * ekd_st + _dot_tn(dpm_b, q_b) + _dot_tn(dam_b, kb_b) + dkb * beta_st)
            dbeta = dbeta + jnp.sum(dkb * k_st, axis=-1, keepdims=True)
            m = dpm * qk + dam * kk
            kd_term = jnp.sum(dkd * kd_st, axis=-1, keepdims=True)
            dgcol = (jnp.sum(dqg * qg_st, axis=-1, keepdims=True) + jnp.sum(dkg * kg_st, axis=-1, keepdims=True)
                     - kd_term + jnp.sum(m, axis=-1, keepdims=True))
            dgr_ref[0, 0, grp:grp + 1, :] = -jnp.sum(m, axis=0, keepdims=True)
            for t, h in enumerate(heads):
                rows, cols = grp_rows[t], tok_cols[t]
                dq_ref[0, :, cols] = dq_st[rows]
                dk_ref[0, :, cols] = dk_st[rows]
                dv_ref[0, :, cols] = dr[rows]
                dg_end = jnp.sum(kd_term[rows]) + egc[t] * jnp.sum(dsp[t] * sh_b[t].astype(F32))
                dgcol_h = dgcol[rows] + jnp.where(row == end_row, dg_end, 0.0)
                dsc_acc = jnp.where(lane == L_BETA + h, dbeta[rows], dsc_acc)
                dsc_acc = jnp.where(lane == L_G + h, dgcol_h, dsc_acc)
                dstate[h] = (_dot_tn(qg_b[rows], do_b[rows]) + egc[t] * dsp[t]
                             - _dot_tn(kg_b[rows], dr_b[rows]))
        dsc_ref[0] = dsc_acc

    step = lambda d, c: n - 1 - _chunk_of_step(d, c, n)
    tok = lambda d, c: (step(d, c), 0)
    dtok = lambda d, c: (d, step(d, c), 0)
    dchunk4 = lambda d, c: (d, step(d, c), 0, 0)
    dchunk5 = lambda d, c: (d, step(d, c), 0, 0, 0)
    tok_spec = pl.BlockSpec((CHUNK, hd_all), tok)
    dtok_spec = pl.BlockSpec((1, CHUNK, hd_all), dtok)
    grow_spec = pl.BlockSpec((1, 1, N_GROUPS, GROWS), dchunk4)
    return _launch(
        body, carry, (q, k, v, scal, grow, states, tinv, vn, do), name=name, grid=(2, n),
        in_specs=[tok_spec] * 3
                 + [pl.BlockSpec((1, CHUNK, BA_W), dtok), grow_spec,
                    pl.BlockSpec((1, 1, HEADS, HEAD_DIM, HEAD_DIM), dchunk5),
                    pl.BlockSpec((1, 1, N_GROUPS, GROWS, GROWS), dchunk5), dtok_spec, tok_spec],
        out_specs=[dtok_spec] * 3 + [pl.BlockSpec((1, CHUNK, BA_W), dtok), grow_spec],
        out_shape=[jax.ShapeDtypeStruct((2, s, hd_all), F32)] * 3
                  + [jax.ShapeDtypeStruct((2, s, BA_W), F32), jax.ShapeDtypeStruct((2, n, N_GROUPS, GROWS), F32)],
        scratch_shapes=[pltpu.VMEM((HEADS, HEAD_DIM, HEAD_DIM), F32)],
        sem=("arbitrary", "arbitrary"))


def _gate_norm_fwd(o2, proj, dnw, *, name, ts=512):
    s = o2.shape[1]
    hd_all = HEADS * HEAD_DIM

    def body(o_ref, z_ref, w_ref, y_ref):
        w = w_ref[...]
        for hd in range(HEADS):
            cols = slice(hd * HEAD_DIM, (hd + 1) * HEAD_DIM)
            seg = o_ref[0, :, cols] + o_ref[1, :, cols]
            r = lax.rsqrt(jnp.mean(seg * seg, axis=-1, keepdims=True) + EPS)
            z = z_ref[:, cols].astype(F32)
            y_ref[:, cols] = ((seg * r * w) * (z * _sigmoid(z))).astype(BF16)

    return pl.pallas_call(
        body, name=name, grid=(s // ts,),
        in_specs=[pl.BlockSpec((2, ts, hd_all), lambda i: (0, i, 0)),
                  pl.BlockSpec((ts, Z_W), lambda i: (i, Z_OFF // Z_W)), _row(HEAD_DIM)],
        out_specs=pl.BlockSpec((ts, hd_all), lambda i: (i, 0)),
        out_shape=jax.ShapeDtypeStruct((s, hd_all), BF16),
        compiler_params=_params(("parallel",)),
    )(o2, proj, dnw)


def _gate_norm_bwd(dyb, o2, proj, dnw, *, name, ts=512):
    s = o2.shape[1]
    hd_all = HEADS * HEAD_DIM

    def body(dy_ref, o_ref, z_ref, w_ref, do_ref, dz_ref, dw_ref):
        w = w_ref[...]
        dw = jnp.zeros((1, HEAD_DIM), F32)
        for hd in range(HEADS):
            cols = slice(hd * HEAD_DIM, (hd + 1) * HEAD_DIM)
            seg = o_ref[0, :, cols] + o_ref[1, :, cols]
            r = lax.rsqrt(jnp.mean(seg * seg, axis=-1, keepdims=True) + EPS)
            xhat = seg * r
            z = z_ref[:, cols].astype(F32)
            sg = _sigmoid(z)
            dy = dy_ref[:, cols]
            dnrm = dy * (z * sg)
            dz_ref[:, cols] = (dy * (xhat * w) * (sg * (1.0 + z * (1.0 - sg)))).astype(BF16)
            dw = dw + jnp.sum(dnrm * xhat, axis=0, keepdims=True)
            dxhat = dnrm * w
            do_ref[:, cols] = r * (dxhat - xhat * jnp.mean(dxhat * xhat, axis=-1, keepdims=True))

        @pl.when(pl.program_id(0) == 0)
        def _():
            dw_ref[...] = jnp.zeros_like(dw_ref)

        dw_ref[...] += dw

    tile = pl.BlockSpec((ts, hd_all), lambda i: (i, 0))
    return pl.pallas_call(
        body, name=name, grid=(s // ts,),
        in_specs=[tile, pl.BlockSpec((2, ts, hd_all), lambda i: (0, i, 0)),
                  pl.BlockSpec((ts, Z_W), lambda i: (i, Z_OFF // Z_W)), _row(HEAD_DIM)],
        out_specs=[tile, tile, _row(HEAD_DIM)],
        out_shape=[jax.ShapeDtypeStruct((s, hd_all), F32), jax.ShapeDtypeStruct((s, hd_all), BF16),
                   jax.ShapeDtypeStruct((1, HEAD_DIM), F32)],
        compiler_params=_params(("arbitrary",)),
    )(dyb, o2, proj, dnw)


def _merge_fwd(ya, yb, proj, wa, wb, wo, h, g, *, name, ts=512):
    s, d = h.shape

    def body(ya_ref, yb_ref, gt_ref, wa_ref, wb_ref, wo_ref, h_ref, g_ref, pa_ref, pb_ref, mix_ref, ho_ref):
        pa = _dot(ya_ref[...], wa_ref[...])
        pb = _dot(yb_ref[...], wb_ref[...])
        pa_ref[...] = pa.astype(BF16)
        pb_ref[...] = pb.astype(BF16)
        merged = (_sigmoid(gt_ref[:, :d].astype(F32)) * pa + _sigmoid(gt_ref[:, d:].astype(F32)) * pb)
        mix = _dot(merged.astype(BF16), wo_ref[...])
        mix_ref[...] = mix.astype(BF16)
        ho_ref[...] = h_ref[...] + g_ref[...] * mix

    tile = pl.BlockSpec((ts, d), lambda i: (i, 0))
    return pl.pallas_call(
        body, name=name, grid=(s // ts,),
        in_specs=[pl.BlockSpec((ts, CONV_A), lambda i: (i, 0)), tile,
                  pl.BlockSpec((ts, GATE_W), lambda i: (i, GATE_OFF // GATE_W)),
                  _resident(wa.shape), _resident(wb.shape), _resident(wo.shape), tile, _row(d)],
        out_specs=[tile, tile, tile, tile],
        out_shape=[jax.ShapeDtypeStruct((s, d), BF16)] * 3 + [jax.ShapeDtypeStruct((s, d), F32)],
        compiler_params=_params(("parallel",)),
    )(ya, yb, proj, wa, wb, wo, h, g)


def _merge_bwd(dh, g, mix, pa, pb, proj, wa, wb, wo, *, name, ts=256):
    s, d = dh.shape

    def body(dh_ref, g_ref, mix_ref, pa_ref, pb_ref, gt_ref, wa_ref, wb_ref, wo_ref,
             dmix_ref, mg_ref, dpa_ref, dpb_ref, dgt_ref, dya_ref, dyb_ref, dg_ref):
        dh_v = dh_ref[...]
        dmix = (g_ref[...] * dh_v).astype(BF16)
        dmix_ref[...] = dmix

        @pl.when(pl.program_id(0) == 0)
        def _():
            dg_ref[...] = jnp.zeros_like(dg_ref)

        dg_ref[...] += jnp.sum(dh_v * mix_ref[...].astype(F32), axis=0, keepdims=True)
        dmerged = _dot_nt(dmix, wo_ref[...])
        pa = pa_ref[...].astype(F32)
        pb = pb_ref[...].astype(F32)
        sa = _sigmoid(gt_ref[:, :d].astype(F32))
        sb = _sigmoid(gt_ref[:, d:].astype(F32))
        mg_ref[...] = (sa * pa + sb * pb).astype(BF16)
        dpa = (dmerged * sa).astype(BF16)
        dpb = (dmerged * sb).astype(BF16)
        dpa_ref[...] = dpa
        dpb_ref[...] = dpb
        dgt_ref[:, :d] = (dmerged * pa * sa * (1.0 - sa)).astype(BF16)
        dgt_ref[:, d:] = (dmerged * pb * sb * (1.0 - sb)).astype(BF16)
        dya_ref[...] = _dot_nt(dpa, wa_ref[...])
        dyb_ref[...] = _dot_nt(dpb, wb_ref[...])

    tile = pl.BlockSpec((ts, d), lambda i: (i, 0))
    return pl.pallas_call(
        body, name=name, grid=(s // ts,),
        in_specs=[tile, _row(d), tile, tile, tile,
                  pl.BlockSpec((ts, GATE_W), lambda i: (i, GATE_OFF // GATE_W)),
                  _resident(wa.shape), _resident(wb.shape), _resident(wo.shape)],
        out_specs=[tile, tile, tile, tile, pl.BlockSpec((ts, GATE_W), lambda i: (i, 0)),
                   pl.BlockSpec((ts, CONV_A), lambda i: (i, 0)), tile, _row(d)],
        out_shape=[jax.ShapeDtypeStruct((s, d), BF16)] * 4
                  + [jax.ShapeDtypeStruct((s, GATE_W), BF16), jax.ShapeDtypeStruct((s, CONV_A), F32),
                     jax.ShapeDtypeStruct((s, d), F32), jax.ShapeDtypeStruct((1, d), F32)],
        compiler_params=_params(("arbitrary",)),
    )(dh, g, mix, pa, pb, proj, wa, wb, wo)


def _final_fwd_bwd(h, nw, target, *, name, ts=512):
    s, d = h.shape

    def body(h_ref, nw_ref, t_ref, loss_ref, dh_ref, dnw_ref):
        x = h_ref[...]
        w = nw_ref[...]
        r = lax.rsqrt(jnp.mean(x * x, axis=-1, keepdims=True) + EPS)
        xhat = x * r
        e = xhat * w - t_ref[...]
        part = 0.5 * jnp.sum(jnp.mean(e * e, axis=-1, keepdims=True))
        dy = e * (1.0 / d)
        dxhat = dy * w
        dh_ref[...] = r * (dxhat - xhat * jnp.mean(dxhat * xhat, axis=-1, keepdims=True))

        @pl.when(pl.program_id(0) == 0)
        def _():
            loss_ref[...] = jnp.zeros_like(loss_ref)
            dnw_ref[...] = jnp.zeros_like(dnw_ref)

        loss_ref[...] += jnp.broadcast_to(part, loss_ref.shape)
        dnw_ref[...] += jnp.sum(dy * xhat, axis=0, keepdims=True)

    tile = pl.BlockSpec((ts, d), lambda i: (i, 0))
    return pl.pallas_call(
        body, name=name, grid=(s // ts,),
        in_specs=[tile, _row(d), tile],
        out_specs=[_row(128), tile, _row(d)],
        out_shape=[jax.ShapeDtypeStruct((1, 128), F32), jax.ShapeDtypeStruct((s, d), F32),
                   jax.ShapeDtypeStruct((1, d), F32)],
        compiler_params=_params(("arbitrary",)),
    )(h, nw, target)


EXT = 8


def _l2norm_heads_bwd(act, dout, scale):
    outs = []
    for hd in range(HEADS):
        cols = slice(hd * HEAD_DIM, (hd + 1) * HEAD_DIM)
        seg = act[:, cols]
        nrm = lax.rsqrt(jnp.sum(seg * seg, axis=-1, keepdims=True) + EPS)
        yhat = seg * nrm
        dsg = dout[:, cols]
        outs.append((scale * nrm) * (dsg - yhat * jnp.sum(yhat * dsg, axis=-1, keepdims=True)))
    return jnp.concatenate(outs, axis=-1)


def _conv_bwd(dq2, dk2, dv2, dya, proj, conv_dn, conv_a, *, name, ts=256, carry=None):
    s = proj.shape[0]
    hd = HEADS * HEAD_DIM
    nt = s // ts
    te = ts + 2 * EXT
    kdn, ka = conv_dn.shape[0], conv_a.shape[0]

    def body(*refs):
        (qp_ref, qc_ref, qn_ref, ap_ref, ac_ref, an_ref) = refs[0:6]
        d3 = refs[6:15]
        (yp_ref, yc_ref, yn_ref, wdn_ref, wa_ref) = refs[15:20]
        (dqkv_ref, da_ref, dwdn_ref, dwa_ref) = refs[20:24]
        xs_ref, dps_ref, xa_ref, dca_ref = refs[24:28]
        i = pl.program_id(0)
        first, last = i == 0, i == nt - 1

        @pl.when(first)
        def _():
            dwdn_ref[...] = jnp.zeros_like(dwdn_ref)
            dwa_ref[...] = jnp.zeros_like(dwa_ref)

        rowe = lax.broadcasted_iota(jnp.int32, (te, 1), 0)
        inside = ~((first & (rowe < EXT)) | (last & (rowe >= EXT + ts)))
        _fill_halo(xs_ref, qp_ref, qc_ref, qn_ref, first, last)
        wdn = wdn_ref[...]
        for part in range(3):
            cols = slice(part * hd, (part + 1) * hd)
            pre = _dwconv_rows(xs_ref, wdn[:, cols], HALO - EXT - 2, te, cols)
            sg = _sigmoid(pre)
            act = pre * sg
            p_ref, c_ref, n_ref = d3[3 * part:3 * part + 3]
            dout = jnp.concatenate([p_ref[0] + p_ref[1], c_ref[0] + c_ref[1], n_ref[0] + n_ref[1]], axis=0)
            if part == 0:
                dact = _l2norm_heads_bwd(act, dout, Q_SCALE)
            elif part == 1:
                dact = _l2norm_heads_bwd(act, dout, 1.0)
            else:
                dact = dout
            dpre = jnp.where(inside, dact * (sg * (1.0 + pre * (1.0 - sg))), 0.0)
            dps_ref[:, cols] = dpre
            acc = wdn[0:1, cols] * dps_ref[EXT + 2:EXT + 2 + ts, cols]
            for tap in range(1, kdn):
                acc = acc + wdn[tap:tap + 1, cols] * dps_ref[EXT + 2 - tap:EXT + 2 - tap + ts, cols]
            dqkv_ref[:, cols] = acc.astype(BF16)
            dcur = dps_ref[EXT:EXT + ts, cols]
            for tap in range(kdn):
                dwdn_ref[tap:tap + 1, cols] += jnp.sum(
                    dcur * xs_ref[HALO - 2 + tap:HALO - 2 + tap + ts, cols], axis=0, keepdims=True)

        cv = lambda r: r[:, CONV_A:2 * CONV_A].astype(F32) * r[:, 2 * CONV_A:].astype(F32)
        _fill_halo(xa_ref, ap_ref, ac_ref, an_ref, first, last, fn=cv)
        wa = wa_ref[...]
        gate_b = jnp.concatenate([ap_ref[HALO - EXT:, 0:CONV_A], ac_ref[:, 0:CONV_A], an_ref[0:EXT, 0:CONV_A]],
                                 axis=0).astype(F32)
        dya_e = jnp.concatenate([yp_ref[...], yc_ref[...], yn_ref[...]], axis=0)
        dca_ref[...] = jnp.where(inside, dya_e * gate_b, 0.0)
        conv = _dwconv_rows(xa_ref, wa, HALO - 1, ts, slice(0, CONV_A))
        acc = wa[0:1, :] * dca_ref[EXT + 1:EXT + 1 + ts, :]
        for tap in range(1, ka):
            acc = acc + wa[tap:tap + 1, :] * dca_ref[EXT + 1 - tap:EXT + 1 - tap + ts, :]
        gc = ac_ref[:, CONV_A:2 * CONV_A].astype(F32)
        val = ac_ref[:, 2 * CONV_A:].astype(F32)
        da_ref[:, 0:CONV_A] = (yc_ref[...] * conv).astype(BF16)
        da_ref[:, CONV_A:2 * CONV_A] = (acc * val).astype(BF16)
        da_ref[:, 2 * CONV_A:] = (acc * gc).astype(BF16)
        dcur = dca_ref[EXT:EXT + ts, :]
        for tap in range(ka):
            dwa_ref[tap:tap + 1, :] += jnp.sum(
                dcur * xa_ref[HALO - 1 + tap:HALO - 1 + tap + ts, :], axis=0, keepdims=True)

    def halo3(width, rows):
        r = ts // rows
        lastb = s // rows - 1
        return [pl.BlockSpec((2, rows, width), lambda i: (0, jnp.maximum(i * r - 1, 0), 0)),
                pl.BlockSpec((2, ts, width), lambda i: (0, i, 0)),
                pl.BlockSpec((2, rows, width), lambda i: (0, jnp.minimum((i + 1) * r, lastb), 0))]

    return _launch(
        body, carry, (proj, proj, proj, proj, proj, proj, dq2, dq2, dq2, dk2, dk2, dk2, dv2, dv2, dv2,
                      dya, dya, dya, conv_dn, conv_a), name=name, grid=(nt,),
        in_specs=(_halo_specs(ts, QKV_W, 0, s) + _halo_specs(ts, A_W, A_OFF // A_W, s)
                  + halo3(hd, EXT) * 3 + _halo_specs(ts, CONV_A, 0, s, rows=EXT)
                  + [_resident(conv_dn.shape), _resident(conv_a.shape)]),
        out_specs=[pl.BlockSpec((ts, QKV_W), lambda i: (i, 0)), pl.BlockSpec((ts, A_W), lambda i: (i, 0)),
                   pl.BlockSpec((8, QKV_W), lambda i: (0, 0)), pl.BlockSpec((8, CONV_A), lambda i: (0, 0))],
        out_shape=[jax.ShapeDtypeStruct((s, QKV_W), BF16), jax.ShapeDtypeStruct((s, A_W), BF16),
                   jax.ShapeDtypeStruct((8, QKV_W), F32), jax.ShapeDtypeStruct((8, CONV_A), F32)],
        scratch_shapes=[pltpu.VMEM((ts + 2 * HALO, QKV_W), F32), pltpu.VMEM((te, QKV_W), F32),
                        pltpu.VMEM((ts + 2 * HALO, CONV_A), F32), pltpu.VMEM((te, CONV_A), F32)],
        sem=("arbitrary",))


IN_A = (0, 1536)
IN_QKV = (1536, 4608)
IN_Z = (4608, 5632)
IN_BA = 5632
IN_GATE = (5664, 7712)
IN_COLS = 7712
G_REPL = 4


def _split_w_in(w_in):
    sl = lambda ab: w_in[:, ab[0]:ab[1]]
    w_main = jnp.concatenate([sl(IN_QKV), sl(IN_Z), sl(IN_GATE), sl(IN_A)], axis=1)
    blocks = []
    for d in range(2):
        beta = w_in[:, IN_BA + 8 * d:IN_BA + 8 * d + 8]
        alpha = w_in[:, IN_BA + 16 + 8 * d:IN_BA + 24 + 8 * d]
        pad = jnp.zeros((w_in.shape[0], BA_W - 8 - 8 * G_REPL), w_in.dtype)
        blocks += [beta] + [alpha] * G_REPL + [pad]
    return w_main, jnp.concatenate(blocks, axis=1)


def _merge_dw_in(dw_qkv, dw_z, dw_gate, dw_a, dw_ba):
    ba = [dw_ba[:, 0:8], dw_ba[:, BA_W:BA_W + 8], dw_ba[:, 8:16], dw_ba[:, BA_W + 8:BA_W + 16]]
    return jnp.concatenate([dw_a, dw_qkv, dw_z] + ba + [dw_gate], axis=1)


def _decay_rows(a_log_fwd, dt_bias_fwd, a_log_bwd, dt_bias_bwd):
    def rows(f, b):
        out = []
        for vec in (f, b):
            vec = vec.reshape(HEADS)
            out.append(jnp.concatenate([jnp.zeros((8,), F32)] + [vec] * G_REPL
                                       + [jnp.zeros((BA_W - 8 - 8 * G_REPL,), F32)])[None])
        return jnp.stack(out)
    return rows(a_log_fwd, a_log_bwd), rows(dt_bias_fwd, dt_bias_bwd)


def _local_step(x, target, mod9, wt, comm):
    s, d = x.shape
    n = s // CHUNK
    wt = dict(wt)
    sh1, sc1, g1, sh2, sc2, g2, sh3, sc3, g3 = [mod9[i:i + 1] for i in range(9)]
    alog, dtb = _decay_rows(wt["a_log_fwd"], wt["dt_bias_fwd"], wt["a_log_bwd"], wt["dt_bias_bwd"])

    (u1, a1, b1, f1), got = comm.gather(
        ["w_ffn1_down", "w_in"],
        lambda c: _ffn_up_fwd(x, wt["norm_ffn1"], sc1, sh1, wt["w_ffn1_up"], name="ffn1_up", carry=c))
    wt.update(got)
    w_main, w_ba = _split_w_in(wt["w_in"])
    y1, h1 = _ffn_down_fwd(f1, wt["w_ffn1_down"], x, g1, name="ffn1_down")
    (u2, proj, ba), got = comm.gather(
        ["w_a_out", "w_b_out", "w_out", "w_ffn2_up", "w_ffn2_down"],
        lambda c: _in_proj_fwd(h1, wt["norm_mix"], sc2, sh2, w_main, w_ba, name="in_proj", carry=c))
    wt.update(got)
    scal = _scal_fwd(ba, alog, dtb, name="scal_fwd")
    grow = scal[:, :, L_G:L_G + 8].reshape(2, n, CHUNK, HEADS).transpose(0, 1, 3, 2).reshape(
        2, n, N_GROUPS, GROWS)
    q, k, v, ya = _conv_fwd(proj, wt["conv_dn"], wt["conv_a"], name="conv_fwd")
    o2, states, tinv, vn = _delta_fwd(q, k, v, scal, grow, name="delta_fwd")
    yb = _gate_norm_fwd(o2, proj, wt["dn_norm"], name="gate_norm_fwd")
    pa, pb, mix, h2 = _merge_fwd(ya, yb, proj, wt["w_a_out"], wt["w_b_out"], wt["w_out"], h1, g2,
                                 name="merge_fwd")
    u3, a3, b3, f3 = _ffn_up_fwd(h2, wt["norm_ffn2"], sc3, sh3, wt["w_ffn2_up"], name="ffn2_up")
    y3, h3 = _ffn_down_fwd(f3, wt["w_ffn2_down"], h2, g3, name="ffn2_down")
    loss, dh3, dnorm_final = _final_fwd_bwd(h3, wt["norm_final"], target, name="final")

    dy3, dab3, dg3 = _ffn_bwd_act(dh3, g3, y3, a3, b3, wt["w_ffn2_down"], name="ffn2_bwd_act")
    dh2, dn3, dsc3, dsh3 = _norm_mod_matmul_bwd([(dab3, wt["w_ffn2_up"])], h2, wt["norm_ffn2"], sc3, dh3,
                                                name="ffn2_bwd_up")
    gw = {}
    gw["w_ffn2_up"] = _matmul_tn(u3, dab3, name="dw_ffn2_up", tm=1024, tn=1408)
    gw["w_ffn2_down"] = _matmul_tn(f3, dy3, name="dw_ffn2_down", tm=1408, tn=1024)

    dmix, merged, dpa, dpb, dgates, dya, dyb, dg2 = _merge_bwd(
        dh2, g2, mix, pa, pb, proj, wt["w_a_out"], wt["w_b_out"], wt["w_out"], name="merge_bwd")
    gw["w_out"] = _matmul_tn(merged, dmix, name="dw_out", tm=1024, tn=1024)
    gw["w_a_out"] = _matmul_tn(ya, dpa, name="dw_a_out", tm=512, tn=1024)
    gw["w_b_out"] = _matmul_tn(yb, dpb, name="dw_b_out", tm=1024, tn=1024)
    do, dz, ddn = _gate_norm_bwd(dyb, o2, proj, wt["dn_norm"], name="gate_norm_bwd")
    recv = {}
    (dq2, dk2, dv2, dscal, drow), got = comm.scatter(
        {nm: gw.pop(nm) for nm in ("w_ffn2_up", "w_ffn2_down")},
        lambda c: _delta_bwd(q, k, v, scal, grow, states, tinv, vn, do, name="delta_bwd", carry=c))
    recv.update(got)
    drow_p = jnp.pad(drow.reshape(2, n, HEADS, CHUNK).transpose(0, 1, 3, 2).reshape(2, s, HEADS),
                     ((0, 0), (0, 0), (L_G, BA_W - L_G - HEADS)))
    dba, dalog, ddtb = _scal_bwd(dscal, drow_p, ba, alog, dtb, name="scal_bwd")
    (dqkv, dbr_a, dconv_dn, dconv_a), got = comm.scatter(
        {nm: gw.pop(nm) for nm in ("w_out", "w_a_out", "w_b_out")},
        lambda c: _conv_bwd(dq2, dk2, dv2, dya, proj, wt["conv_dn"], wt["conv_a"], name="conv_bwd", carry=c))
    recv.update(got)
    w_qkv = w_main[:, 0:QKV_W]
    w_z = w_main[:, Z_OFF:Z_OFF + Z_W]
    w_gate = w_main[:, GATE_OFF:GATE_OFF + GATE_W]
    w_bra = w_main[:, A_OFF:A_OFF + A_W]
    dh1, dn2, dsc2, dsh2 = _norm_mod_matmul_bwd(
        [(dqkv, w_qkv), (dz, w_z), (dgates, w_gate), (dbr_a, w_bra), (dba, w_ba)],
        h1, wt["norm_mix"], sc2, dh2, name="in_proj_bwd")
    gw["w_in"] = _merge_dw_in(
        _matmul_tn(u2, dqkv, name="dw_in_qkv", tm=1024, tn=1536),
        _matmul_tn(u2, dz, name="dw_in_z", tm=1024, tn=1024),
        _matmul_tn(u2, dgates, name="dw_in_gate", tm=1024, tn=1024),
        _matmul_tn(u2, dbr_a, name="dw_in_a", tm=1024, tn=1536),
        _matmul_tn(u2, dba, name="dw_in_ba", tm=1024, tn=2 * BA_W))

    dy1, dab1, dg1 = _ffn_bwd_act(dh1, g1, y1, a1, b1, wt["w_ffn1_down"], name="ffn1_bwd_act")
    dw_down1 = _matmul_tn(f1, dy1, name="dw_ffn1_down", tm=1408, tn=1024)
    (dx, dn1, dsc1, dsh1), got = comm.scatter(
        {"w_in": gw.pop("w_in")},
        lambda c: _norm_mod_matmul_bwd([(dab1, wt["w_ffn1_up"])], x, wt["norm_ffn1"], sc1, dh1,
                                       name="ffn1_bwd_up", carry=c))
    recv.update(got)
    dw_up1, got = comm.scatter(
        {"w_ffn1_down": dw_down1},
        lambda c: _matmul_tn(u1, dab1, name="dw_ffn1_up", tm=1024, tn=1408, carry=c))
    recv.update(got)
    recv.update(comm.scatter({"w_ffn1_up": dw_up1}, None, name="scatter_last")[1])

    small = {
        "mod": jnp.concatenate([dsh1, dsc1, dg1, dsh2, dsc2, dg2, dsh3, dsc3, dg3], axis=1),
        "norm_ffn1": dn1, "norm_mix": dn2, "norm_ffn2": dn3, "norm_final": dnorm_final,
        "a_log_fwd": dalog[0, :, L_G:L_G + 8], "dt_bias_fwd": ddtb[0, :, L_G:L_G + 8],
        "a_log_bwd": dalog[1, :, L_G:L_G + 8], "dt_bias_bwd": ddtb[1, :, L_G:L_G + 8],
        "dn_norm": ddn,
        "conv_a": dconv_a[0:3].reshape(1, -1), "conv_dn": dconv_dn[0:5].reshape(1, -1),
    }
    return loss, dx, recv, small


def _full_weight(name, g):
    if name in COL_SHARDED + CONV_SHARDED:
        return g.transpose(1, 0, 2).reshape(g.shape[1], -1)
    return g.reshape(-1, g.shape[-1])


def _grad_pieces(name, g):
    g = g.astype(BF16)
    if name in COL_SHARDED:
        return g.reshape(g.shape[0], N_DEV, -1).transpose(1, 0, 2)
    return g.reshape(N_DEV, -1, g.shape[-1])


class _MeshComm:
    def __init__(self, shards):
        self.shards = shards

    def _run(self, xs, carrier, name, gather):
        if carrier is None:
            return None, _exchange(xs, name=name, gather=gather)
        return carrier((xs, gather))

    def gather(self, names, carrier=None, name=None):
        outs, got = self._run([self.shards[nm] for nm in names], carrier, name, True)
        return outs, {nm: _full_weight(nm, g) for nm, g in zip(names, got)}

    def scatter(self, grads, carrier=None, name=None):
        names = list(grads)
        outs, got = self._run([_grad_pieces(nm, grads[nm]) for nm in names], carrier, name, False)
        return outs, dict(zip(names, got))


def _mod_fwd(c_all, w_ada, *, name):
    def body(c_ref, w_ref, o_ref):
        cv = c_ref[...]
        o_ref[...] = _dot3(cv * _sigmoid(cv), w_ref[...])

    return pl.pallas_call(
        body, name=name, out_shape=jax.ShapeDtypeStruct((c_all.shape[0], w_ada.shape[1]), F32),
        compiler_params=_params(),
    )(c_all, w_ada)


def _adamw_math(w, g, m, v):
    m_new = ADAM_B1 * m + (1.0 - ADAM_B1) * g
    v_new = ADAM_B2 * v + (1.0 - ADAM_B2) * (g * g)
    m_hat = m_new / (1.0 - ADAM_B1 ** ADAM_STEP)
    v_hat = v_new / (1.0 - ADAM_B2 ** ADAM_STEP)
    delta = -ADAM_LR * (m_hat / (jnp.sqrt(v_hat) + ADAM_EPS) + ADAM_WD * w)
    return delta, m_new, v_new


def _reduce_adamw(pieces, w, m, v, *, name, tr):
    r, c = w.shape

    def body(p_ref, w_ref, m_ref, v_ref, g_ref, d_ref, mo_ref, vo_ref):
        g = p_ref[0].astype(F32)
        for src in range(1, N_DEV):
            g = g + p_ref[src].astype(F32)
        g_ref[...] = g
        d_ref[...], mo_ref[...], vo_ref[...] = _adamw_math(w_ref[...], g, m_ref[...], v_ref[...])

    tile = pl.BlockSpec((tr, c), lambda i: (i, 0))
    return pl.pallas_call(
        body, name=name, grid=(r // tr,),
        in_specs=[pl.BlockSpec((N_DEV, tr, c), lambda i: (0, i, 0)), tile, tile, tile],
        out_specs=[tile] * 4, out_shape=[jax.ShapeDtypeStruct((r, c), F32)] * 4,
        compiler_params=_params(("parallel",)),
    )(pieces, w, m, v)


def _ada_grad_adamw(c_all_t, dmod_cols, w, m, v, *, name, tr=256):
    r, c = w.shape

    def body(c_ref, dm_ref, w_ref, m_ref, v_ref, g_ref, d_ref, mo_ref, vo_ref):
        cv = c_ref[...]
        act = cv * _sigmoid(cv)
        dm = dm_ref[...]
        g = act[:, 0:1] * dm[0:1, :]
        for b in range(1, N_DEV):
            g = g + act[:, b:b + 1] * dm[b:b + 1, :]
        g_ref[...] = g
        d_ref[...], mo_ref[...], vo_ref[...] = _adamw_math(w_ref[...], g, m_ref[...], v_ref[...])

    tile = pl.BlockSpec((tr, c), lambda i: (i, 0))
    return pl.pallas_call(
        body, name=name, grid=(r // tr,),
        in_specs=[pl.BlockSpec((tr, N_DEV), lambda i: (i, 0)), pl.BlockSpec((N_DEV, c), lambda i: (0, 0)),
                  tile, tile, tile],
        out_specs=[tile] * 4, out_shape=[jax.ShapeDtypeStruct((r, c), F32)] * 4,
        compiler_params=_params(("parallel",)),
    )(c_all_t, dmod_cols, w, m, v)


def _sum_rows(parts, *, name):
    def body(p_ref, o_ref):
        acc = p_ref[0:1, :]
        for src in range(1, N_DEV):
            acc = acc + p_ref[src:src + 1, :]
        o_ref[...] = acc

    return pl.pallas_call(
        body, name=name, out_shape=jax.ShapeDtypeStruct((1, parts.shape[1]), F32), compiler_params=_params(),
    )(parts)


def _adamw_rows(g, w, m, v, *, name):
    def body(g_ref, w_ref, m_ref, v_ref, d_ref, mo_ref, vo_ref):
        d_ref[...], mo_ref[...], vo_ref[...] = _adamw_math(w_ref[...], g_ref[...], m_ref[...], v_ref[...])

    return pl.pallas_call(
        body, name=name, out_shape=[jax.ShapeDtypeStruct(g.shape, F32)] * 3, compiler_params=_params(),
    )(g, w, m, v)


WEIGHTS = ["w_ada", "b_ada", "norm_ffn1", "w_ffn1_up", "w_ffn1_down", "norm_mix", "w_in", "conv_a", "conv_dn",
           "a_log_fwd", "dt_bias_fwd", "a_log_bwd", "dt_bias_bwd", "dn_norm", "w_a_out", "w_b_out", "w_out",
           "norm_ffn2", "w_ffn2_up", "w_ffn2_down", "norm_final"]
COL_SHARDED = ["w_ffn1_up", "w_in", "w_a_out", "w_ffn2_up"]
ROW_SHARDED = ["w_ffn1_down", "w_b_out", "w_out", "w_ffn2_down"]
CONV_SHARDED = ["conv_a", "conv_dn"]
REPLICATED = ["b_ada", "norm_ffn1", "norm_mix", "a_log_fwd", "dt_bias_fwd", "a_log_bwd", "dt_bias_bwd",
              "dn_norm", "norm_ffn2", "norm_final"]
SMALL_ORDER = ["mod", "norm_ffn1", "norm_mix", "norm_ffn2", "norm_final", "a_log_fwd", "dt_bias_fwd",
               "a_log_bwd", "dt_bias_bwd", "dn_norm", "conv_a", "conv_dn"]
REDUCE_ROWS = {"w_ffn1_up": 256, "w_in": 256, "w_a_out": 256, "w_ffn2_up": 256,
               "w_ffn1_down": 176, "w_b_out": 128, "w_out": 128, "w_ffn2_down": 176}


def _pad_lanes(row):
    pad = (-row.shape[1]) % 128
    return jnp.pad(row, ((0, 0), (0, pad)))


def _unstack_cols(g):
    return g.transpose(1, 0, 2).reshape(g.shape[1], -1)


def _stack_cols(w):
    k = w.shape[0]
    return w.reshape(k, N_DEV, -1).transpose(1, 0, 2)


def kernel(x, c, w_ada, b_ada, norm_ffn1, w_ffn1_up, w_ffn1_down, norm_mix, w_in, conv_a, conv_dn, a_log_fwd, dt_bias_fwd, a_log_bwd, dt_bias_bwd, dn_norm, w_a_out, w_b_out, w_out, norm_ffn2, w_ffn2_up, w_ffn2_down, norm_final, loss_target, m_w_ada, m_b_ada, m_norm_ffn1, m_w_ffn1_up, m_w_ffn1_down, m_norm_mix, m_w_in, m_conv_a, m_conv_dn, m_a_log_fwd, m_dt_bias_fwd, m_a_log_bwd, m_dt_bias_bwd, m_dn_norm, m_w_a_out, m_w_b_out, m_w_out, m_norm_ffn2, m_w_ffn2_up, m_w_ffn2_down, m_norm_final, v_w_ada, v_b_ada, v_norm_ffn1, v_w_ffn1_up, v_w_ffn1_down, v_norm_mix, v_w_in, v_conv_a, v_conv_dn, v_a_log_fwd, v_dt_bias_fwd, v_a_log_bwd, v_dt_bias_bwd, v_dn_norm, v_w_a_out, v_w_b_out, v_w_out, v_norm_ffn2, v_w_ffn2_up, v_w_ffn2_down, v_norm_final):
    args = dict(locals())
    w_loc = {n: args[n] for n in WEIGHTS}
    m_loc = {n: args["m_" + n] for n in WEIGHTS}
    v_loc = {n: args["v_" + n] for n in WEIGHTS}
    me = _flat_index(_my_position())
    d_model = x.shape[-1]

    big = COL_SHARDED + ROW_SHARDED
    shards = {n: w_loc[n][0].astype(BF16) for n in big}
    shards.update({n: w_loc[n][0] for n in CONV_SHARDED})
    shards["c"] = c
    comm = _MeshComm(shards)
    wt = comm.gather(["c", "conv_a", "conv_dn", "w_ffn1_up"], name="gather_first")[1]
    c_all = wt.pop("c")
    for n in REPLICATED[1:]:
        wt[n] = w_loc[n].reshape(1, -1)

    mod_cols = _mod_fwd(c_all, w_ada[0], name="mod_fwd")
    mod_all = _exchange([mod_cols], name="gather_mod", gather=True)[0]
    mod_mine = lax.dynamic_index_in_dim(mod_all, me, axis=1, keepdims=False).reshape(1, -1) + b_ada
    mod9 = mod_mine.reshape(9, d_model)

    loss_loc, dx, recv, small = _local_step(x[0], loss_target[0], mod9, wt, comm)
    loss = lax.psum(loss_loc[0, 0], MESH_AXES)

    res = {}
    for n in big:
        res[n] = _reduce_adamw(recv[n], w_loc[n][0], m_loc[n][0], v_loc[n][0], name="adamw_" + n,
                               tr=REDUCE_ROWS[n])

    packed = _pad_lanes(jnp.concatenate([small[n].reshape(1, -1) for n in SMALL_ORDER], axis=1))
    parts = _exchange([packed], name="gather_small", gather=True)[0].reshape(N_DEV, -1)
    total = _sum_rows(parts, name="sum_small")
    off = 0
    gsmall = {}
    for n in SMALL_ORDER:
        size = small[n].size
        gsmall[n] = total[:, off:off + size]
        off += size
    dmod_all = parts[:, 0:9 * d_model]
    ada_cols = w_ada.shape[-1]
    dmod_cols = lax.dynamic_slice_in_dim(dmod_all, me * ada_cols, ada_cols, axis=1)
    res["w_ada"] = _ada_grad_adamw(c_all.T, dmod_cols, w_ada[0], m_w_ada[0], v_w_ada[0], name="adamw_w_ada")
    g_rows = {"b_ada": gsmall["mod"]}
    for n in REPLICATED[1:]:
        g_rows[n] = gsmall[n]
    for n in CONV_SHARDED:
        taps, width = w_loc[n].shape[1], w_loc[n].shape[2]
        full = gsmall[n].reshape(taps, -1)
        g_rows[n] = lax.dynamic_slice_in_dim(full, me * width, width, axis=1).reshape(1, -1)
    row_names = REPLICATED + CONV_SHARDED
    cat = lambda src: _pad_lanes(jnp.concatenate([src[n].reshape(1, -1) for n in row_names], axis=1))
    g_cat = cat(g_rows)
    d_cat, m_cat, v_cat = _adamw_rows(g_cat, cat(w_loc), cat(m_loc), cat(v_loc), name="adamw_small")
    off = 0
    for n in row_names:
        size = w_loc[n].size
        res[n] = tuple(t[:, off:off + size] for t in (g_cat, d_cat, m_cat, v_cat))
        off += size

    outs = [loss, dx[None]]
    for kind in range(4):
        for n in WEIGHTS:
            outs.append(res[n][kind].reshape(w_loc[n].shape))
    return tuple(outs)
```

```python
import functools
import math

import jax
import jax.numpy as jnp
from jax import lax
from jax.experimental import pallas as pl
from jax.experimental.pallas import tpu as pltpu

F32 = jnp.float32
BF16 = jnp.bfloat16
EPS = 1e-6
N_DEV = 8
CHUNK = 64
HEADS = 8
HEAD_DIM = 128
MESH_AXES = ("x", "y", "c")
VMEM_LIMIT_BYTES = 56 * 1024 * 1024

ADAM_LR = 0.001
ADAM_B1 = 0.9
ADAM_B2 = 0.999
ADAM_EPS = 1e-08
ADAM_WD = 0.01
ADAM_STEP = 10


def _params(sem=None):
    return pltpu.CompilerParams(dimension_semantics=sem, vmem_limit_bytes=VMEM_LIMIT_BYTES)


def _row(n):
    return pl.BlockSpec((1, n), lambda *_: (0, 0))


def _resident(shape):
    nd = len(shape)
    return pl.BlockSpec(shape, lambda *_: (0,) * nd, pipeline_mode=pl.Buffered(1))


def _sigmoid(x):
    return 1.0 / (1.0 + jnp.exp(-x))


def _dot(a, b):
    return jnp.dot(a, b, preferred_element_type=F32)


def _dot_nt(a, b):
    return lax.dot_general(a, b, (((1,), (1,)), ((), ())), preferred_element_type=F32)


def _dot_tn(a, b):
    return lax.dot_general(a, b, (((0,), (0,)), ((), ())), preferred_element_type=F32)


def _split_bf16(x):
    hi = x.astype(BF16)
    lo = (x - hi.astype(F32)).astype(BF16)
    return hi, lo


def _dot3(a, b, dot=_dot):
    ah, al = a if isinstance(a, tuple) else _split_bf16(a)
    bh, bl = b if isinstance(b, tuple) else _split_bf16(b)
    return dot(ah, bh) + dot(ah, bl) + dot(al, bh)


def _dot_exact(a, b):
    return jnp.dot(a, b, preferred_element_type=F32, precision=lax.Precision.HIGHEST)


def _my_position():
    return tuple(lax.axis_index(a) for a in MESH_AXES)


def _peer(pos, kk):
    return tuple((1 - p) if (kk >> (2 - b)) & 1 else p for b, p in enumerate(pos))


def _flat_index(pos):
    return pos[0] * 4 + pos[1] * 2 + pos[2]


_ANY = pl.BlockSpec(memory_space=pl.ANY)


class _AllToAll:
    def __init__(self, in_refs, out_refs, send_sems, recv_sems, local_sems):
        pos = _my_position()
        me = _flat_index(pos)
        self.copies = []
        for t in range(len(in_refs)):
            self.copies.append(pltpu.make_async_copy(in_refs[t].at[me], out_refs[t].at[me], local_sems.at[t]))
            for kk in range(1, N_DEV):
                peer = _peer(pos, kk)
                self.copies.append(pltpu.make_async_remote_copy(
                    src_ref=in_refs[t].at[_flat_index(peer)], dst_ref=out_refs[t].at[me],
                    send_sem=send_sems.at[t, kk - 1], recv_sem=recv_sems.at[t, kk - 1],
                    device_id=peer, device_id_type=pl.DeviceIdType.MESH))

    def start(self):
        for cp in self.copies:
            cp.start()

    def finish(self):
        for cp in self.copies:
            cp.wait()


class _AllGather:
    def __init__(self, in_refs, out_refs, send_sems, recv_sems, local_sems):
        self.refs = (in_refs, out_refs, send_sems, recv_sems, local_sems)
        x, y, c = _my_position()
        self.me, self.sibling = (x, y, c), (x, y, 1 - c)
        self.chips = [(1 - x, y), (x, 1 - y), (1 - x, 1 - y)]
        self.core = c

    def _copy(self, t, k, block, to, own=False):
        in_refs, out_refs, send_sems, recv_sems, _ = self.refs
        rows = out_refs[t].at[_flat_index(block)]
        return pltpu.make_async_remote_copy(
            src_ref=in_refs[t] if own else rows, dst_ref=rows,
            send_sem=send_sems.at[t, k], recv_sem=recv_sems.at[t, k],
            device_id=to, device_id_type=pl.DeviceIdType.MESH)

    def _local(self, t):
        in_refs, out_refs, _, _, local_sems = self.refs
        return pltpu.make_async_copy(in_refs[t], out_refs[t].at[_flat_index(self.me)], local_sems.at[t])

    def start(self):
        c = self.core
        for t in range(len(self.refs[0])):
            self._local(t).start()
            self._copy(t, 0, self.me, self.sibling, own=True).start()
            for j, chip in enumerate(self.chips):
                self._copy(t, 1 + j, self.me, (*chip, c), own=True).start()

    def finish(self):
        c = self.core
        n_t = len(self.refs[0])
        for t in range(n_t):
            for j, chip in enumerate(self.chips):
                self._copy(t, 1 + j, (*chip, c), self.me).wait_recv()
                self._copy(t, 4 + j, (*chip, c), self.sibling).start()
        for t in range(n_t):
            self._copy(t, 0, self.sibling, self.me).wait_recv()
            for j, chip in enumerate(self.chips):
                self._copy(t, 4 + j, (*chip, 1 - c), self.me).wait_recv()
            self._copy(t, 0, self.me, self.sibling, own=True).wait_send()
            for j, chip in enumerate(self.chips):
                self._copy(t, 1 + j, self.me, (*chip, c), own=True).wait_send()
                self._copy(t, 4 + j, (*chip, c), self.sibling).wait_send()
            self._local(t).wait()


def _exchange_plan(in_refs, out_refs, send_sems, recv_sems, local_sems, gather):
    return (_AllGather if gather else _AllToAll)(in_refs, out_refs, send_sems, recv_sems, local_sems)


def _exchange_shapes(xs, gather):
    out_shape = [jax.ShapeDtypeStruct(((N_DEV,) + x.shape) if gather else x.shape, x.dtype) for x in xs]
    sems = [pltpu.SemaphoreType.DMA((len(xs), N_DEV - 1)), pltpu.SemaphoreType.DMA((len(xs), N_DEV - 1)),
            pltpu.SemaphoreType.DMA((len(xs),))]
    return out_shape, sems


def _exchange(xs, *, name, gather):
    nt = len(xs)

    def body(*refs):
        plan = _exchange_plan(refs[:nt], refs[nt:2 * nt], *refs[2 * nt:], gather)
        plan.start()
        plan.finish()

    out_shape, sems = _exchange_shapes(xs, gather)
    return pl.pallas_call(body, name=name, in_specs=[_ANY] * nt, out_specs=[_ANY] * nt, out_shape=out_shape,
                          scratch_shapes=sems)(*xs)


def _launch(body, carry, args, *, name, grid, in_specs, out_specs, out_shape, scratch_shapes=(), sem):
    single = not isinstance(out_shape, (list, tuple))
    out_specs = [out_specs] if single else list(out_specs)
    out_shape = [out_shape] if single else list(out_shape)
    if carry is None:
        outs = pl.pallas_call(body, name=name, grid=grid, in_specs=list(in_specs), out_specs=out_specs,
                              out_shape=out_shape, scratch_shapes=list(scratch_shapes),
                              compiler_params=_params(sem))(*args)
        return outs[0] if single else outs
    xs, gather = carry
    nt, n_in, n_out, n_scr = len(xs), len(args), len(out_shape), len(scratch_shapes)
    x_shape, sems = _exchange_shapes(xs, gather)

    def wrapped(*refs):
        c_in, x_in = refs[:n_in], refs[n_in:n_in + nt]
        c_out = refs[n_in + nt:n_in + nt + n_out]
        x_out = refs[n_in + nt + n_out:n_in + 2 * nt + n_out]
        scr = refs[n_in + 2 * nt + n_out:]
        ids = [pl.program_id(a) for a in range(len(grid))]
        first = functools.reduce(jnp.logical_and, [i == 0 for i in ids])
        last = functools.reduce(jnp.logical_and, [i == g - 1 for i, g in zip(ids, grid)])
        plan = lambda: _exchange_plan(x_in, x_out, *scr[n_scr:], gather)

        @pl.when(first)
        def _():
            plan().start()

        body(*c_in, *c_out, *scr[:n_scr])

        @pl.when(last)
        def _():
            plan().finish()

    outs = pl.pallas_call(
        wrapped, name=name, grid=grid, in_specs=list(in_specs) + [_ANY] * nt,
        out_specs=out_specs + [_ANY] * nt, out_shape=out_shape + x_shape,
        scratch_shapes=list(scratch_shapes) + sems,
        compiler_params=_params(("arbitrary",) * len(grid)))(*args, *xs)
    compute = outs[:n_out]
    return (compute[0] if single else compute), outs[n_out:]


def _norm_mod(x, nw, sc, sh):
    r = lax.rsqrt(jnp.mean(x * x, axis=-1, keepdims=True) + EPS)
    return (x * r * nw) * (1.0 + sc) + sh


def _norm_mod_bwd(x, nw, sc, du):
    r = lax.rsqrt(jnp.mean(x * x, axis=-1, keepdims=True) + EPS)
    xhat = x * r
    n = xhat * nw
    dsh = jnp.sum(du, axis=0, keepdims=True)
    dsc = jnp.sum(du * n, axis=0, keepdims=True)
    dn = du * (1.0 + sc)
    dnw = jnp.sum(dn * xhat, axis=0, keepdims=True)
    dxhat = dn * nw
    dx = r * (dxhat - xhat * jnp.mean(dxhat * xhat, axis=-1, keepdims=True))
    return dx, dnw, dsc, dsh


def _ffn_up_fwd(h, nw, sc, sh, wup, *, name, ts=512, tn=1408, carry=None):
    s, d = h.shape
    f_dim = wup.shape[1] // 2
    nj = f_dim // tn

    def body(h_ref, nw_ref, sc_ref, sh_ref, wa_ref, wb_ref, u_ref, a_ref, b_ref, f_ref):
        @pl.when(pl.program_id(1) == 0)
        def _():
            u_ref[...] = _norm_mod(h_ref[...], nw_ref[...], sc_ref[...], sh_ref[...]).astype(BF16)

        u = u_ref[...]
        a = _dot(u, wa_ref[...])
        b = _dot(u, wb_ref[...])
        a_ref[...] = a.astype(BF16)
        b_ref[...] = b.astype(BF16)
        f_ref[...] = (a * _sigmoid(a) * b).astype(BF16)

    return _launch(
        body, carry, (h, nw, sc, sh, wup, wup), name=name, grid=(s // ts, nj),
        in_specs=[pl.BlockSpec((ts, d), lambda i, j: (i, 0)), _row(d), _row(d), _row(d),
                  pl.BlockSpec((d, tn), lambda i, j: (0, j)),
                  pl.BlockSpec((d, tn), lambda i, j: (0, j + nj))],
        out_specs=[pl.BlockSpec((ts, d), lambda i, j: (i, 0)),
                   pl.BlockSpec((ts, tn), lambda i, j: (i, j)),
                   pl.BlockSpec((ts, tn), lambda i, j: (i, j)),
                   pl.BlockSpec((ts, tn), lambda i, j: (i, j))],
        out_shape=[jax.ShapeDtypeStruct((s, d), BF16)] + [jax.ShapeDtypeStruct((s, f_dim), BF16)] * 3,
        sem=("parallel", "arbitrary"))


def _ffn_down_fwd(f, wd, h, g, *, name, ts=512):
    s, f_dim = f.shape
    d = wd.shape[1]

    def body(f_ref, wd_ref, h_ref, g_ref, y_ref, ho_ref):
        y = _dot(f_ref[...], wd_ref[...])
        y_ref[...] = y.astype(BF16)
        ho_ref[...] = h_ref[...] + (0.5 * g_ref[...]) * y

    return pl.pallas_call(
        body, name=name, grid=(s // ts,),
        in_specs=[pl.BlockSpec((ts, f_dim), lambda i: (i, 0)), _resident((f_dim, d)),
                  pl.BlockSpec((ts, d), lambda i: (i, 0)), _row(d)],
        out_specs=[pl.BlockSpec((ts, d), lambda i: (i, 0)), pl.BlockSpec((ts, d), lambda i: (i, 0))],
        out_shape=[jax.ShapeDtypeStruct((s, d), BF16), jax.ShapeDtypeStruct((s, d), F32)],
        compiler_params=_params(("parallel",)),
    )(f, wd, h, g)


def _ffn_bwd_act(dh, g, y, a, b, wd, *, name, ts=256, carry=None):
    s, d = dh.shape
    f_dim = a.shape[1]

    def body(dh_ref, g_ref, y_ref, a_ref, b_ref, wd_ref, dy_ref, dab_ref, dg_ref):
        dh_v = dh_ref[...]
        dy = ((0.5 * g_ref[...]) * dh_v).astype(BF16)
        dy_ref[...] = dy
        part = jnp.sum(0.5 * dh_v * y_ref[...].astype(F32), axis=0, keepdims=True)

        @pl.when(pl.program_id(0) == 0)
        def _():
            dg_ref[...] = jnp.zeros_like(dg_ref)

        dg_ref[...] += part
        df = _dot_nt(dy, wd_ref[...])
        av = a_ref[...].astype(F32)
        bv = b_ref[...].astype(F32)
        sg = _sigmoid(av)
        dab_ref[:, :f_dim] = (df * bv * (sg * (1.0 + av * (1.0 - sg)))).astype(BF16)
        dab_ref[:, f_dim:] = (df * (av * sg)).astype(BF16)

    return _launch(
        body, carry, (dh, g, y, a, b, wd), name=name, grid=(s // ts,),
        in_specs=[pl.BlockSpec((ts, d), lambda i: (i, 0)), _row(d),
                  pl.BlockSpec((ts, d), lambda i: (i, 0)),
                  pl.BlockSpec((ts, f_dim), lambda i: (i, 0)),
                  pl.BlockSpec((ts, f_dim), lambda i: (i, 0)),
                  _resident((f_dim, d))],
        out_specs=[pl.BlockSpec((ts, d), lambda i: (i, 0)),
                   pl.BlockSpec((ts, 2 * f_dim), lambda i: (i, 0)), _row(d)],
        out_shape=[jax.ShapeDtypeStruct((s, d), BF16), jax.ShapeDtypeStruct((s, 2 * f_dim), BF16),
                   jax.ShapeDtypeStruct((1, d), F32)],
        sem=("arbitrary",))


def _norm_mod_matmul_bwd(pairs, h, nw, sc, dh_in, *, name, ts=256, carry=None):
    s, d = h.shape
    n_pairs = len(pairs)

    def body(*refs):
        dx_refs = refs[:n_pairs]
        w_refs = refs[n_pairs:2 * n_pairs]
        h_ref, nw_ref, sc_ref, dhi_ref, dho_ref, dnw_ref, dsc_ref, dsh_ref = refs[2 * n_pairs:]
        du = _dot_nt(dx_refs[0][...], w_refs[0][...])
        for k in range(1, n_pairs):
            du = du + _dot_nt(dx_refs[k][...], w_refs[k][...])
        dx, dnw, dsc, dsh = _norm_mod_bwd(h_ref[...], nw_ref[...], sc_ref[...], du)
        dho_ref[...] = dhi_ref[...] + dx

        @pl.when(pl.program_id(0) == 0)
        def _():
            dnw_ref[...] = jnp.zeros_like(dnw_ref)
            dsc_ref[...] = jnp.zeros_like(dsc_ref)
            dsh_ref[...] = jnp.zeros_like(dsh_ref)

        dnw_ref[...] += dnw
        dsc_ref[...] += dsc
        dsh_ref[...] += dsh

    dxs = [p[0] for p in pairs]
    ws = [p[1] for p in pairs]
    tile = pl.BlockSpec((ts, d), lambda i: (i, 0))
    return _launch(
        body, carry, (*dxs, *ws, h, nw, sc, dh_in), name=name, grid=(s // ts,),
        in_specs=([pl.BlockSpec((ts, x.shape[1]), lambda i: (i, 0)) for x in dxs]
                  + [_resident(w.shape) for w in ws] + [tile, _row(d), _row(d), tile]),
        out_specs=[tile, _row(d), _row(d), _row(d)],
        out_shape=[jax.ShapeDtypeStruct((s, d), F32)] + [jax.ShapeDtypeStruct((1, d), F32)] * 3,
        sem=("arbitrary",))


def _matmul_tn(a, b, *, name, tm, tn, tk=512, carry=None):
    s, m = a.shape
    n = b.shape[1]
    nk = s // tk

    def body(a_ref, b_ref, o_ref, acc_ref):
        k = pl.program_id(2)

        @pl.when(k == 0)
        def _():
            acc_ref[...] = jnp.zeros_like(acc_ref)

        acc_ref[...] += _dot_tn(a_ref[...], b_ref[...])

        @pl.when(k == nk - 1)
        def _():
            o_ref[...] = acc_ref[...].astype(o_ref.dtype)

    return _launch(
        body, carry, (a, b), name=name, grid=(m // tm, n // tn, nk),
        in_specs=[pl.BlockSpec((tk, tm), lambda i, j, k: (k, i)),
                  pl.BlockSpec((tk, tn), lambda i, j, k: (k, j))],
        out_specs=pl.BlockSpec((tm, tn), lambda i, j, k: (i, j)),
        out_shape=jax.ShapeDtypeStruct((m, n), BF16),
        scratch_shapes=[pltpu.VMEM((tm, tn), F32)],
        sem=("parallel", "parallel", "arbitrary"))


def _in_proj_fwd(h, nw, sc, sh, w_main, w_ba, *, name, ts=512, tn=1536, carry=None):
    s, d = h.shape
    n_main = w_main.shape[1]
    n_ba = w_ba.shape[1]

    def body(h_ref, nw_ref, sc_ref, sh_ref, w_ref, wba_ref, u_ref, p_ref, ba_ref):
        @pl.when(pl.program_id(1) == 0)
        def _():
            u0 = _norm_mod(h_ref[...], nw_ref[...], sc_ref[...], sh_ref[...]).astype(BF16)
            u_ref[...] = u0
            ba_ref[...] = _dot(u0, wba_ref[...])

        p_ref[...] = _dot(u_ref[...], w_ref[...]).astype(BF16)

    return _launch(
        body, carry, (h, nw, sc, sh, w_main, w_ba), name=name, grid=(s // ts, n_main // tn),
        in_specs=[pl.BlockSpec((ts, d), lambda i, j: (i, 0)), _row(d), _row(d), _row(d),
                  pl.BlockSpec((d, tn), lambda i, j: (0, j)), _resident((d, n_ba))],
        out_specs=[pl.BlockSpec((ts, d), lambda i, j: (i, 0)),
                   pl.BlockSpec((ts, tn), lambda i, j: (i, j)),
                   pl.BlockSpec((ts, n_ba), lambda i, j: (i, 0))],
        out_shape=[jax.ShapeDtypeStruct((s, d), BF16), jax.ShapeDtypeStruct((s, n_main), BF16),
                   jax.ShapeDtypeStruct((s, n_ba), F32)],
        sem=("parallel", "arbitrary"))


QKV_W = 3 * HEADS * HEAD_DIM
Z_OFF, Z_W = 3072, 1024
GATE_OFF, GATE_W = 4096, 2048
A_OFF, A_W = 6144, 1536
N_MAIN = 7680
CONV_A = 512
BA_W = 128

L_BETA, L_G, L_EG, L_EKD, L_EGC = 0, 8, 16, 24, 32


def _softplus(z):
    e = jnp.exp(-jnp.abs(z))
    small = e * (1.0 - e * (0.5 - e * (1.0 / 3.0)))
    return jnp.maximum(z, 0.0) + jnp.where(e < 1e-3, small, jnp.log(1.0 + e))


def _tri(n, sgn, strict=False):
    i = lax.broadcasted_iota(jnp.int32, (n, n), 0)
    j = lax.broadcasted_iota(jnp.int32, (n, n), 1)
    dlt = (i - j) * sgn
    return (dlt > 0) if strict else (dlt >= 0)


def _scal_fwd(ba, alog, dtb, *, name, ts=512):
    s = ba.shape[0]

    def body(ba_ref, al_ref, dt_ref, o_ref):
        d = pl.program_id(0)
        sgn = 1 - 2 * d
        x = ba_ref[...]
        lane = lax.broadcasted_iota(jnp.int32, x.shape, 1)
        beta = _sigmoid(x)
        g = -jnp.exp(al_ref[0]) * _softplus(x + dt_ref[0])
        g = jnp.where((lane >= L_G) & (lane < L_EGC + 8), g, 0.0)
        ltri = jnp.where(_tri(CHUNK, sgn), 1.0, 0.0).astype(F32)
        for c in range(ts // CHUNK):
            rows = slice(c * CHUNK, (c + 1) * CHUNK)
            gc = _dot_exact(ltri, g[rows])
            g_end = jnp.where(d == 0, gc[CHUNK - 1:CHUNK], gc[0:1])
            ln = lane[rows]
            out = jnp.where(ln < L_G, beta[rows],
                  jnp.where(ln < L_EG, gc,
                  jnp.where(ln < L_EKD, jnp.exp(gc),
                  jnp.where(ln < L_EGC, jnp.exp(g_end - gc),
                  jnp.where(ln < L_EGC + 8, jnp.broadcast_to(jnp.exp(g_end), gc.shape), 0.0)))))
            o_ref[0, rows, :] = out

    return pl.pallas_call(
        body, name=name, grid=(2, s // ts),
        in_specs=[pl.BlockSpec((ts, BA_W), lambda d, i: (i, d)),
                  pl.BlockSpec((1, 1, BA_W), lambda d, i: (d, 0, 0)),
                  pl.BlockSpec((1, 1, BA_W), lambda d, i: (d, 0, 0))],
        out_specs=pl.BlockSpec((1, ts, BA_W), lambda d, i: (d, i, 0)),
        out_shape=jax.ShapeDtypeStruct((2, s, BA_W), F32),
        compiler_params=_params(("parallel", "parallel")),
    )(ba, alog, dtb)


def _scal_bwd(dscal, drow, ba, alog, dtb, *, name, ts=512):
    s = ba.shape[0]

    def body(ds_ref, dr_ref, ba_ref, al_ref, dt_ref, dba_ref, dal_ref, ddt_ref):
        d = pl.program_id(0)
        sgn = 1 - 2 * d
        x = ba_ref[...]
        lane = lax.broadcasted_iota(jnp.int32, x.shape, 1)
        in_g = (lane >= L_G) & (lane < L_G + 8)
        beta = _sigmoid(x)
        z = x + dt_ref[0]
        neg_a = -jnp.exp(al_ref[0])
        g = neg_a * _softplus(z)
        dsv = ds_ref[0]
        dgc = jnp.where(in_g, dsv + dr_ref[0], 0.0)
        utri = jnp.where(_tri(CHUNK, -sgn), 1.0, 0.0).astype(F32)
        dal = jnp.zeros((1, BA_W), F32)
        ddt = jnp.zeros((1, BA_W), F32)
        for c in range(ts // CHUNK):
            rows = slice(c * CHUNK, (c + 1) * CHUNK)
            dg = _dot_exact(utri, dgc[rows])
            dz = dg * neg_a * _sigmoid(z[rows])
            dal = dal + jnp.sum(dg * g[rows], axis=0, keepdims=True)
            ddt = ddt + jnp.sum(dz, axis=0, keepdims=True)
            b = beta[rows]
            out = jnp.where(lane[rows] < L_G, dsv[rows] * b * (1.0 - b), jnp.where(in_g[rows], dz, 0.0))
            dba_ref[rows, :] = out.astype(BF16)

        @pl.when(pl.program_id(1) == 0)
        def _():
            dal_ref[...] = jnp.zeros_like(dal_ref)
            ddt_ref[...] = jnp.zeros_like(ddt_ref)

        dal_ref[0] += dal
        ddt_ref[0] += ddt

    row3 = pl.BlockSpec((1, 1, BA_W), lambda d, i: (d, 0, 0))
    tok3 = pl.BlockSpec((1, ts, BA_W), lambda d, i: (d, i, 0))
    return pl.pallas_call(
        body, name=name, grid=(2, s // ts),
        in_specs=[tok3, tok3, pl.BlockSpec((ts, BA_W), lambda d, i: (i, d)), row3, row3],
        out_specs=[pl.BlockSpec((ts, BA_W), lambda d, i: (i, d)), row3, row3],
        out_shape=[jax.ShapeDtypeStruct((s, 2 * BA_W), BF16), jax.ShapeDtypeStruct((2, 1, BA_W), F32),
                   jax.ShapeDtypeStruct((2, 1, BA_W), F32)],
        compiler_params=_params(("arbitrary", "arbitrary")),
    )(dscal, drow, ba, alog, dtb)


HALO = 16


def _halo_specs(ts, width, col_block, n_rows, rows=HALO):
    r = ts // rows
    last = n_rows // rows - 1
    return [pl.BlockSpec((rows, width), lambda i: (jnp.maximum(i * r - 1, 0), col_block)),
            pl.BlockSpec((ts, width), lambda i: (i, col_block)),
            pl.BlockSpec((rows, width), lambda i: (jnp.minimum((i + 1) * r, last), col_block))]


def _fill_halo(dst_ref, prev_ref, cur_ref, next_ref, first, last, fn=lambda r: r[...].astype(F32)):
    h = prev_ref.shape[0]
    ts = cur_ref.shape[0]
    p = fn(prev_ref)
    n = fn(next_ref)
    dst_ref[0:h, :] = jnp.where(first, 0.0, p)
    dst_ref[h:h + ts, :] = fn(cur_ref)
    dst_ref[h + ts:h + ts + h, :] = jnp.where(last, 0.0, n)


def _dwconv_rows(src_ref, w, start, n_rows, cols):
    acc = w[0:1, :] * src_ref[start:start + n_rows, cols]
    for i in range(1, w.shape[0]):
        acc = acc + w[i:i + 1, :] * src_ref[start + i:start + i + n_rows, cols]
    return acc


def _l2norm_heads(act, scale):
    outs = []
    for hd in range(HEADS):
        seg = act[:, hd * HEAD_DIM:(hd + 1) * HEAD_DIM]
        outs.append(seg * (lax.rsqrt(jnp.sum(seg * seg, axis=-1, keepdims=True) + EPS) * scale))
    return jnp.concatenate(outs, axis=-1)


Q_SCALE = HEAD_DIM ** -0.5


def _conv_fwd(proj, conv_dn, conv_a, *, name, ts=256):
    s = proj.shape[0]
    hd = HEADS * HEAD_DIM
    nt = s // ts

    def body(qp_ref, qc_ref, qn_ref, ap_ref, ac_ref, an_ref, wdn_ref, wa_ref,
             q_ref, k_ref, v_ref, ya_ref, xs_ref, xa_ref):
        i = pl.program_id(0)
        first, last = i == 0, i == nt - 1
        _fill_halo(xs_ref, qp_ref, qc_ref, qn_ref, first, last)
        wdn = wdn_ref[...]
        for part, o_ref in enumerate((q_ref, k_ref, v_ref)):
            cols = slice(part * hd, (part + 1) * hd)
            pre = _dwconv_rows(xs_ref, wdn[:, cols], HALO - 2, ts, cols)
            act = pre * _sigmoid(pre)
            if part == 0:
                act = _l2norm_heads(act, Q_SCALE)
            elif part == 1:
                act = _l2norm_heads(act, 1.0)
            o_ref[...] = act
        cv = lambda r: r[:, CONV_A:2 * CONV_A].astype(F32) * r[:, 2 * CONV_A:].astype(F32)
        _fill_halo(xa_ref, ap_ref, ac_ref, an_ref, first, last, fn=cv)
        conv = _dwconv_rows(xa_ref, wa_ref[...], HALO - 1, ts, slice(0, CONV_A))
        ya_ref[...] = (ac_ref[:, 0:CONV_A].astype(F32) * conv).astype(BF16)

    tile = lambda w: pl.BlockSpec((ts, w), lambda i: (i, 0))
    return pl.pallas_call(
        body, name=name, grid=(nt,),
        in_specs=(_halo_specs(ts, QKV_W, 0, s) + _halo_specs(ts, A_W, A_OFF // A_W, s)
                  + [_resident(conv_dn.shape), _resident(conv_a.shape)]),
        out_specs=[tile(hd), tile(hd), tile(hd), tile(CONV_A)],
        out_shape=[jax.ShapeDtypeStruct((s, hd), F32)] * 3 + [jax.ShapeDtypeStruct((s, CONV_A), BF16)],
        scratch_shapes=[pltpu.VMEM((ts + 2 * HALO, QKV_W), F32), pltpu.VMEM((ts + 2 * HALO, CONV_A), F32)],
        compiler_params=_params(("parallel",)),
    )(proj, proj, proj, proj, proj, proj, conv_dn, conv_a)


def _chunk_of_step(d, c, n):
    return c + d * (n - 1 - 2 * c)


def _head_scalars(scv, grv, hd):
    col = lambda base: scv[:, base + hd:base + hd + 1]
    return (col(L_BETA), col(L_G), col(L_EG), col(L_EKD),
            scv[0:1, L_EGC + hd:L_EGC + hd + 1], grv[hd:hd + 1, :])


def _decay_matrix(gcol, grow, incl):
    return jnp.where(incl, jnp.exp(jnp.minimum(gcol - grow, 0.0)), 0.0)


INV_BASE = 8


def _unit_lower_inverse(a_m, top=None):
    n = a_m.shape[0]
    top = top or n
    i = lax.broadcasted_iota(jnp.int32, (n, n), 0)
    j = lax.broadcasted_iota(jnp.int32, (n, n), 1)

    def same_block(m):
        sh = int(math.log2(m))
        return jnp.right_shift(i, sh) == jnp.right_shift(j, sh)

    x = jnp.where(same_block(INV_BASE), -a_m, 0.0)
    t = jnp.where(i == j, 1.0, 0.0) + x
    p = x
    for _ in range(int(math.log2(INV_BASE)) - 1):
        p_b = p.astype(BF16)
        p = _dot(p_b, p_b)
        t = t + _dot(t.astype(BF16), p.astype(BF16))
    m = INV_BASE
    while m < top:
        join = jnp.where(same_block(2 * m) & jnp.logical_not(same_block(m)), a_m, 0.0)
        t_b = t.astype(BF16)
        t = t - _dot(_dot(t_b, join.astype(BF16)).astype(BF16), t_b)
        m *= 2
    return t


def _delta_fwd_per_head(q, k, v, scal, grow, *, name):
    s = q.shape[0]
    n = s // CHUNK
    hd_all = HEADS * HEAD_DIM

    def body(q_ref, k_ref, v_ref, sc_ref, gr_ref, o_ref, st_ref, t_ref, vn_ref, state):
        d = pl.program_id(0)
        sgn = 1 - 2 * d

        @pl.when(pl.program_id(1) == 0)
        def _():
            state[...] = jnp.zeros_like(state)

        incl = _tri(CHUNK, sgn)
        strict = _tri(CHUNK, sgn, strict=True)
        scv = sc_ref[0]
        grv = gr_ref[0, 0]
        for hd in range(HEADS):
            cols = slice(hd * HEAD_DIM, (hd + 1) * HEAD_DIM)
            qh, kh, vh = q_ref[:, cols], k_ref[:, cols], v_ref[:, cols]
            beta, gcol, eg, ekd, egc, grow_h = _head_scalars(scv, grv, hd)
            dm = _decay_matrix(gcol, grow_h, incl)
            k_b = kh.astype(BF16)
            kk = _dot_nt((kh * beta).astype(BF16), k_b)
            t = _unit_lower_inverse(jnp.where(strict, kk * dm, 0.0))
            p_m = jnp.where(incl, _dot_nt(qh.astype(BF16), k_b) * dm, 0.0)
            sh = state[hd]
            sh_b = sh.astype(BF16)
            st_ref[0, 0, hd] = sh_b
            r = vh - _dot((kh * eg).astype(BF16), sh_b)
            vn = _dot3(t, beta * r)
            vn_b = vn.astype(BF16)
            o_ref[0, :, cols] = _dot((qh * eg).astype(BF16), sh_b) + _dot(p_m.astype(BF16), vn_b)
            state[hd] = egc * sh + _dot_tn((kh * ekd).astype(BF16), vn_b)
            t_ref[0, 0, hd] = t
            vn_ref[0, :, cols] = vn_b

    tok = lambda d, c: (_chunk_of_step(d, c, n), 0)
    dtok = lambda d, c: (d, _chunk_of_step(d, c, n), 0)
    dchunk4 = lambda d, c: (d, _chunk_of_step(d, c, n), 0, 0)
    dchunk5 = lambda d, c: (d, _chunk_of_step(d, c, n), 0, 0, 0)
    return pl.pallas_call(
        body, name=name, grid=(2, n),
        in_specs=[pl.BlockSpec((CHUNK, hd_all), tok)] * 3
                 + [pl.BlockSpec((1, CHUNK, BA_W), dtok), pl.BlockSpec((1, 1, HEADS, CHUNK), dchunk4)],
        out_specs=[pl.BlockSpec((1, CHUNK, hd_all), dtok),
                   pl.BlockSpec((1, 1, HEADS, HEAD_DIM, HEAD_DIM), dchunk5),
                   pl.BlockSpec((1, 1, HEADS, CHUNK, CHUNK), dchunk5),
                   pl.BlockSpec((1, CHUNK, hd_all), dtok)],
        out_shape=[jax.ShapeDtypeStruct((2, s, hd_all), F32),
                   jax.ShapeDtypeStruct((2, n, HEADS, HEAD_DIM, HEAD_DIM), BF16),
                   jax.ShapeDtypeStruct((2, n, HEADS, CHUNK, CHUNK), F32),
                   jax.ShapeDtypeStruct((2, s, hd_all), BF16)],
        scratch_shapes=[pltpu.VMEM((HEADS, HEAD_DIM, HEAD_DIM), F32)],
        compiler_params=_params(("arbitrary", "arbitrary")),
    )(q, k, v, scal, grow)


def _delta_bwd_per_head(q, k, v, scal, grow, states, tinv, vn, do, *, name):
    s = q.shape[0]
    n = s // CHUNK
    hd_all = HEADS * HEAD_DIM

    def body(q_ref, k_ref, v_ref, sc_ref, gr_ref, st_ref, t_ref, vn_ref, do_ref,
             dq_ref, dk_ref, dv_ref, dsc_ref, dgr_ref, dstate):
        d = pl.program_id(0)
        sgn = 1 - 2 * d

        @pl.when(pl.program_id(1) == 0)
        def _():
            dstate[...] = jnp.zeros_like(dstate)

        incl = _tri(CHUNK, sgn)
        strict = _tri(CHUNK, sgn, strict=True)
        scv = sc_ref[0]
        grv = gr_ref[0, 0]
        lane = lax.broadcasted_iota(jnp.int32, (CHUNK, BA_W), 1)
        row = lax.broadcasted_iota(jnp.int32, (CHUNK, 1), 0)
        sub = lax.broadcasted_iota(jnp.int32, (HEADS, CHUNK), 0)
        end_row = jnp.where(d == 0, CHUNK - 1, 0)
        dsc_acc = jnp.zeros((CHUNK, BA_W), F32)
        dgr_acc = jnp.zeros((HEADS, CHUNK), F32)
        for hd in range(HEADS):
            cols = slice(hd * HEAD_DIM, (hd + 1) * HEAD_DIM)
            qh, kh, vh = q_ref[:, cols], k_ref[:, cols], v_ref[:, cols]
            beta, gcol, eg, ekd, egc, grow_h = _head_scalars(scv, grv, hd)
            dm = _decay_matrix(gcol, grow_h, incl)
            q_b, k_b = qh.astype(BF16), kh.astype(BF16)
            kb_b = (kh * beta).astype(BF16)
            kk = _dot_nt(kb_b, k_b)
            qk = _dot_nt(q_b, k_b)
            p_m = jnp.where(incl, qk * dm, 0.0)
            t = t_ref[0, 0, hd]
            vn_b = vn_ref[0, :, cols]
            sh_b = st_ref[0, 0, hd]
            dsp = dstate[hd]
            dsp_b = dsp.astype(BF16)
            do_b = do_ref[:, cols].astype(BF16)
            kg, qg, kd = kh * eg, qh * eg, kh * ekd
            kg_b, qg_b, kd_b = kg.astype(BF16), qg.astype(BF16), kd.astype(BF16)
            r = vh - _dot(kg_b, sh_b)
            dvn = _dot_tn(p_m.astype(BF16), do_b) + _dot(kd_b, dsp_b)
            db = _dot3(t, dvn, dot=_dot_tn)
            dr = db * beta
            dbeta = jnp.sum(db * r, axis=-1, keepdims=True)
            dr_b, db_b = dr.astype(BF16), db.astype(BF16)
            dkg = -_dot_nt(dr_b, sh_b)
            dqg = _dot_nt(do_b, sh_b)
            dkd = _dot_nt(vn_b, dsp_b)
            dpm = jnp.where(incl, _dot_nt(do_b, vn_b), 0.0) * dm
            dam = jnp.where(strict, -_dot_nt(db_b, vn_b), 0.0) * dm
            dpm_b, dam_b = dpm.astype(BF16), dam.astype(BF16)
            dkb = _dot(dam_b, k_b)
            dq_ref[0, :, cols] = dqg * eg + _dot(dpm_b, k_b)
            dk_ref[0, :, cols] = (dkg * eg + dkd * ekd + _dot_tn(dpm_b, q_b) + _dot_tn(dam_b, kb_b)
                                  + dkb * beta)
            dv_ref[0, :, cols] = dr
            dbeta = dbeta + jnp.sum(dkb * kh, axis=-1, keepdims=True)
            m = dpm * qk + dam * kk
            kd_term = jnp.sum(dkd * kd, axis=-1, keepdims=True)
            dgcol = (jnp.sum(dqg * qg, axis=-1, keepdims=True) + jnp.sum(dkg * kg, axis=-1, keepdims=True)
                     - kd_term + jnp.sum(m, axis=-1, keepdims=True))
            dg_end = jnp.sum(kd_term) + egc * jnp.sum(dsp * sh_b.astype(F32))
            dgcol = dgcol + jnp.where(row == end_row, dg_end, 0.0)
            dsc_acc = jnp.where(lane == L_BETA + hd, dbeta, dsc_acc)
            dsc_acc = jnp.where(lane == L_G + hd, dgcol, dsc_acc)
            dgr_acc = jnp.where(sub == hd, -jnp.sum(m, axis=0, keepdims=True), dgr_acc)
            dstate[hd] = _dot_tn(qg_b, do_b) + egc * dsp - _dot_tn(kg_b, dr_b)
        dsc_ref[0] = dsc_acc
        dgr_ref[0, 0] = dgr_acc

    step = lambda d, c: n - 1 - _chunk_of_step(d, c, n)
    tok = lambda d, c: (step(d, c), 0)
    dtok = lambda d, c: (d, step(d, c), 0)
    dchunk4 = lambda d, c: (d, step(d, c), 0, 0)
    dchunk5 = lambda d, c: (d, step(d, c), 0, 0, 0)
    tok_spec = pl.BlockSpec((CHUNK, hd_all), tok)
    dtok_spec = pl.BlockSpec((1, CHUNK, hd_all), dtok)
    return pl.pallas_call(
        body, name=name, grid=(2, n),
        in_specs=[tok_spec] * 3
                 + [pl.BlockSpec((1, CHUNK, BA_W), dtok), pl.BlockSpec((1, 1, HEADS, CHUNK), dchunk4),
                    pl.BlockSpec((1, 1, HEADS, HEAD_DIM, HEAD_DIM), dchunk5),
                    pl.BlockSpec((1, 1, HEADS, CHUNK, CHUNK), dchunk5), dtok_spec, tok_spec],
        out_specs=[dtok_spec] * 3
                  + [pl.BlockSpec((1, CHUNK, BA_W), dtok), pl.BlockSpec((1, 1, HEADS, CHUNK), dchunk4)],
        out_shape=[jax.ShapeDtypeStruct((2, s, hd_all), F32)] * 3
                  + [jax.ShapeDtypeStruct((2, s, BA_W), F32), jax.ShapeDtypeStruct((2, n, HEADS, CHUNK), F32)],
        scratch_shapes=[pltpu.VMEM((HEADS, HEAD_DIM, HEAD_DIM), F32)],
        compiler_params=_params(("arbitrary", "arbitrary")),
    )(q, k, v, scal, grow, states, tinv, vn, do)


GROUP = 4
GROWS = GROUP * CHUNK
N_GROUPS = HEADS // GROUP


def _stack(parts):
    return jnp.concatenate(parts, axis=0)


M_INCL, M_STRICT, M_EYE, M_BASE, M_JOIN = 0, 1, 2, 3, 4
JOIN_SIZES = (16, 32, 64)
N_MASKS = M_JOIN + len(JOIN_SIZES)


def _write_group_masks(mask_ref, sgn, n_masks):
    i = lax.broadcasted_iota(jnp.int32, (GROWS, GROWS), 0)
    j = lax.broadcasted_iota(jnp.int32, (GROWS, GROWS), 1)
    same = lambda m: jnp.right_shift(i, int(math.log2(m))) == jnp.right_shift(j, int(math.log2(m)))
    dlt = (i - j) * sgn
    one = lambda cond: jnp.where(cond, 1.0, 0.0).astype(F32)
    mask_ref[M_INCL] = one(same(CHUNK) & (dlt >= 0))
    mask_ref[M_STRICT] = one(same(CHUNK) & (dlt > 0))
    if n_masks > M_EYE:
        mask_ref[M_EYE] = one(i == j)
        mask_ref[M_BASE] = one(same(INV_BASE))
        for lvl, m in enumerate(JOIN_SIZES):
            mask_ref[M_JOIN + lvl] = one(same(m) & jnp.logical_not(same(m // 2)))


def _group_decay(gcol, grow, mask_ref):
    return jnp.exp(jnp.minimum(gcol - grow, 0.0)) * mask_ref[M_INCL]


def _block_inverse(a_m, mask_ref):
    x = -(a_m * mask_ref[M_BASE])
    t = mask_ref[M_EYE] + x
    p = x
    for _ in range(int(math.log2(INV_BASE)) - 1):
        p_b = p.astype(BF16)
        p = _dot(p_b, p_b)
        t = t + _dot(t.astype(BF16), p.astype(BF16))
    for lvl in range(len(JOIN_SIZES)):
        t_b = t.astype(BF16)
        t = t - _dot(_dot(t_b, (a_m * mask_ref[M_JOIN + lvl]).astype(BF16)).astype(BF16), t_b)
    return t


def _group_operands(q_ref, k_ref, v_ref, scv, grp):
    heads = [GROUP * grp + t for t in range(GROUP)]
    tiles = lambda ref: [ref[:, h * HEAD_DIM:(h + 1) * HEAD_DIM] for h in heads]
    col = lambda base: [scv[:, base + h:base + h + 1] for h in heads]
    egc = [scv[0:1, L_EGC + h:L_EGC + h + 1] for h in heads]
    return heads, tiles(q_ref), tiles(k_ref), tiles(v_ref), col(L_BETA), col(L_G), col(L_EG), col(L_EKD), egc


def _delta_fwd(q, k, v, scal, grow, *, name):
    s = q.shape[0]
    n = s // CHUNK
    hd_all = HEADS * HEAD_DIM

    def body(q_ref, k_ref, v_ref, sc_ref, gr_ref, o_ref, st_ref, t_ref, vn_ref, state, mask_ref):
        d = pl.program_id(0)

        @pl.when(pl.program_id(1) == 0)
        def _():
            state[...] = jnp.zeros_like(state)
            _write_group_masks(mask_ref, 1 - 2 * d, N_MASKS)

        scv = sc_ref[0]
        for grp in range(N_GROUPS):
            heads, qs, ks, vs, beta, gcol, eg, ekd, egc = _group_operands(q_ref, k_ref, v_ref, scv, grp)
            dm = _group_decay(_stack(gcol), gr_ref[0, 0, grp:grp + 1, :], mask_ref)
            k_b = _stack(ks).astype(BF16)
            kk = _dot_nt(_stack([ks[t] * beta[t] for t in range(GROUP)]).astype(BF16), k_b)
            tinv = _block_inverse(kk * dm * mask_ref[M_STRICT], mask_ref).astype(BF16)
            t_ref[0, 0, grp] = tinv
            p_m = _dot_nt(_stack(qs).astype(BF16), k_b) * dm
            sh, sh_b, br = [], [], []
            for t, h in enumerate(heads):
                sh.append(state[h])
                sh_b.append(sh[t].astype(BF16))
                st_ref[0, 0, h] = sh_b[t]
                br.append(beta[t] * (vs[t] - _dot((ks[t] * eg[t]).astype(BF16), sh_b[t])))
            vn_b = _dot(tinv, _stack(br).astype(BF16)).astype(BF16)
            o_intra = _dot(p_m.astype(BF16), vn_b)
            for t, h in enumerate(heads):
                rows = slice(t * CHUNK, (t + 1) * CHUNK)
                cols = slice(h * HEAD_DIM, (h + 1) * HEAD_DIM)
                o_ref[0, :, cols] = _dot((qs[t] * eg[t]).astype(BF16), sh_b[t]) + o_intra[rows]
                state[h] = egc[t] * sh[t] + _dot_tn((ks[t] * ekd[t]).astype(BF16), vn_b[rows])
                vn_ref[0, :, cols] = vn_b[rows]

    tok = lambda d, c: (_chunk_of_step(d, c, n), 0)
    dtok = lambda d, c: (d, _chunk_of_step(d, c, n), 0)
    dchunk4 = lambda d, c: (d, _chunk_of_step(d, c, n), 0, 0)
    dchunk5 = lambda d, c: (d, _chunk_of_step(d, c, n), 0, 0, 0)
    return pl.pallas_call(
        body, name=name, grid=(2, n),
        in_specs=[pl.BlockSpec((CHUNK, hd_all), tok)] * 3
                 + [pl.BlockSpec((1, CHUNK, BA_W), dtok), pl.BlockSpec((1, 1, N_GROUPS, GROWS), dchunk4)],
        out_specs=[pl.BlockSpec((1, CHUNK, hd_all), dtok),
                   pl.BlockSpec((1, 1, HEADS, HEAD_DIM, HEAD_DIM), dchunk5),
                   pl.BlockSpec((1, 1, N_GROUPS, GROWS, GROWS), dchunk5),
                   pl.BlockSpec((1, CHUNK, hd_all), dtok)],
        out_shape=[jax.ShapeDtypeStruct((2, s, hd_all), F32),
                   jax.ShapeDtypeStruct((2, n, HEADS, HEAD_DIM, HEAD_DIM), BF16),
                   jax.ShapeDtypeStruct((2, n, N_GROUPS, GROWS, GROWS), BF16),
                   jax.ShapeDtypeStruct((2, s, hd_all), BF16)],
        scratch_shapes=[pltpu.VMEM((HEADS, HEAD_DIM, HEAD_DIM), F32), pltpu.VMEM((N_MASKS, GROWS, GROWS), F32)],
        compiler_params=_params(("arbitrary", "arbitrary")),
    )(q, k, v, scal, grow)


def _delta_bwd(q, k, v, scal, grow, states, tinv, vn, do, *, name, carry=None):
    s = q.shape[0]
    n = s // CHUNK
    hd_all = HEADS * HEAD_DIM

    def body(q_ref, k_ref, v_ref, sc_ref, gr_ref, st_ref, t_ref, vn_ref, do_ref,
             dq_ref, dk_ref, dv_ref, dsc_ref, dgr_ref, dstate, mask_ref):
        d = pl.program_id(0)

        @pl.when(pl.program_id(1) == 0)
        def _():
            dstate[...] = jnp.zeros_like(dstate)
            _write_group_masks(mask_ref, 1 - 2 * d, M_EYE)

        scv = sc_ref[0]
        lane = lax.broadcasted_iota(jnp.int32, (CHUNK, BA_W), 1)
        row = lax.broadcasted_iota(jnp.int32, (CHUNK, 1), 0)
        end_row = jnp.where(d == 0, CHUNK - 1, 0)
        dsc_acc = jnp.zeros((CHUNK, BA_W), F32)
        for grp in range(N_GROUPS):
            heads, qs, ks, vs, beta, gcol, eg, ekd, egc = _group_operands(q_ref, k_ref, v_ref, scv, grp)
            dm = _group_decay(_stack(gcol), gr_ref[0, 0, grp:grp + 1, :], mask_ref)
            dm_strict = dm * mask_ref[M_STRICT]
            beta_st, eg_st, ekd_st = _stack(beta), _stack(eg), _stack(ekd)
            q_st, k_st = _stack(qs), _stack(ks)
            q_b, k_b = q_st.astype(BF16), k_st.astype(BF16)
            kb_b = (k_st * beta_st).astype(BF16)
            kk = _dot_nt(kb_b, k_b)
            qk = _dot_nt(q_b, k_b)
            p_m = qk * dm
            kg_st, qg_st, kd_st = k_st * eg_st, q_st * eg_st, k_st * ekd_st
            kg_b, qg_b, kd_b = kg_st.astype(BF16), qg_st.astype(BF16), kd_st.astype(BF16)
            tok_cols = [slice(h * HEAD_DIM, (h + 1) * HEAD_DIM) for h in heads]
            grp_rows = [slice(t * CHUNK, (t + 1) * CHUNK) for t in range(GROUP)]
            vn_b = _stack([vn_ref[0, :, c] for c in tok_cols])
            do_b = _stack([do_ref[:, c] for c in tok_cols]).astype(BF16)
            sh_b = [st_ref[0, 0, h] for h in heads]
            dsp = [dstate[h] for h in heads]
            dsp_b = [x.astype(BF16) for x in dsp]
            r_st = _stack([vs[t] - _dot(kg_b[grp_rows[t]], sh_b[t]) for t in range(GROUP)])
            dvn = _dot_tn(p_m.astype(BF16), do_b) + _stack(
                [_dot(kd_b[grp_rows[t]], dsp_b[t]) for t in range(GROUP)])
            db = _dot_tn(t_ref[0, 0, grp], dvn.astype(BF16))
            dr = db * beta_st
            dbeta = jnp.sum(db * r_st, axis=-1, keepdims=True)
            dr_b, db_b = dr.astype(BF16), db.astype(BF16)
            dkg = -_stack([_dot_nt(dr_b[grp_rows[t]], sh_b[t]) for t in range(GROUP)])
            dqg = _stack([_dot_nt(do_b[grp_rows[t]], sh_b[t]) for t in range(GROUP)])
            dkd = _stack([_dot_nt(vn_b[grp_rows[t]], dsp_b[t]) for t in range(GROUP)])
            dpm = _dot_nt(do_b, vn_b) * dm
            dam = -_dot_nt(db_b, vn_b) * dm_strict
            dpm_b, dam_b = dpm.astype(BF16), dam.astype(BF16)
            dkb = _dot(dam_b, k_b)
            dq_st = dqg * eg_st + _dot(dpm_b, k_b)
            dk_st = (dkg * eg_st + dkd * ekd_st + _dot_tn(dpm_b, q_b) + _dot_tn(dam_b, kb_b) + dkb * beta_st)
            dbeta = dbeta + jnp.sum(dkb * k_st, axis=-1, keepdims=True)
            m = dpm * qk + dam * kk
            kd_term = jnp.sum(dkd * kd_st, axis=-1, keepdims=True)
            dgcol = (jnp.sum(dqg * qg_st, axis=-1, keepdims=True) + jnp.sum(dkg * kg_st, axis=-1, keepdims=True)
                     - kd_term + jnp.sum(m, axis=-1, keepdims=True))
            dgr_ref[0, 0, grp:grp + 1, :] = -jnp.sum(m, axis=0, keepdims=True)
            for t, h in enumerate(heads):
                rows, cols = grp_rows[t], tok_cols[t]
                dq_ref[0, :, cols] = dq_st[rows]
                dk_ref[0, :, cols] = dk_st[rows]
                dv_ref[0, :, cols] = dr[rows]
                dg_end = jnp.sum(kd_term[rows]) + egc[t] * jnp.sum(dsp[t] * sh_b[t].astype(F32))
                dgcol_h = dgcol[rows] + jnp.where(row == end_row, dg_end, 0.0)
                dsc_acc = jnp.where(lane == L_BETA + h, dbeta[rows], dsc_acc)
                dsc_acc = jnp.where(lane == L_G + h, dgcol_h, dsc_acc)
                dstate[h] = (_dot_tn(qg_b[rows], do_b[rows]) + egc[t] * dsp[t]
                             - _dot_tn(kg_b[rows], dr_b[rows]))
        dsc_ref[0] = dsc_acc

    step = lambda d, c: n - 1 - _chunk_of_step(d, c, n)
    tok = lambda d, c: (step(d, c), 0)
    dtok = lambda d, c: (d, step(d, c), 0)
    dchunk4 = lambda d, c: (d, step(d, c), 0, 0)
    dchunk5 = lambda d, c: (d, step(d, c), 0, 0, 0)
    tok_spec = pl.BlockSpec((CHUNK, hd_all), tok)
    dtok_spec = pl.BlockSpec((1, CHUNK, hd_all), dtok)
    grow_spec = pl.BlockSpec((1, 1, N_GROUPS, GROWS), dchunk4)
    return _launch(
        body, carry, (q, k, v, scal, grow, states, tinv, vn, do), name=name, grid=(2, n),
        in_specs=[tok_spec] * 3
                 + [pl.BlockSpec((1, CHUNK, BA_W), dtok), grow_spec,
                    pl.BlockSpec((1, 1, HEADS, HEAD_DIM, HEAD_DIM), dchunk5),
                    pl.BlockSpec((1, 1, N_GROUPS, GROWS, GROWS), dchunk5), dtok_spec, tok_spec],
        out_specs=[dtok_spec] * 3 + [pl.BlockSpec((1, CHUNK, BA_W), dtok), grow_spec],
        out_shape=[jax.ShapeDtypeStruct((2, s, hd_all), F32)] * 3
                  + [jax.ShapeDtypeStruct((2, s, BA_W), F32), jax.ShapeDtypeStruct((2, n, N_GROUPS, GROWS), F32)],
        scratch_shapes=[pltpu.VMEM((HEADS, HEAD_DIM, HEAD_DIM), F32), pltpu.VMEM((M_EYE, GROWS, GROWS), F32)],
        sem=("arbitrary", "arbitrary"))


def _gate_norm_fwd(o2, proj, dnw, *, name, ts=512):
    s = o2.shape[1]
    hd_all = HEADS * HEAD_DIM

    def body(o_ref, z_ref, w_ref, y_ref):
        w = w_ref[...]
        for hd in range(HEADS):
            cols = slice(hd * HEAD_DIM, (hd + 1) * HEAD_DIM)
            seg = o_ref[0, :, cols] + o_ref[1, :, cols]
            r = lax.rsqrt(jnp.mean(seg * seg, axis=-1, keepdims=True) + EPS)
            z = z_ref[:, cols].astype(F32)
            y_ref[:, cols] = ((seg * r * w) * (z * _sigmoid(z))).astype(BF16)

    return pl.pallas_call(
        body, name=name, grid=(s // ts,),
        in_specs=[pl.BlockSpec((2, ts, hd_all), lambda i: (0, i, 0)),
                  pl.BlockSpec((ts, Z_W), lambda i: (i, Z_OFF // Z_W)), _row(HEAD_DIM)],
        out_specs=pl.BlockSpec((ts, hd_all), lambda i: (i, 0)),
        out_shape=jax.ShapeDtypeStruct((s, hd_all), BF16),
        compiler_params=_params(("parallel",)),
    )(o2, proj, dnw)


def _gate_norm_bwd(dyb, o2, proj, dnw, *, name, ts=512):
    s = o2.shape[1]
    hd_all = HEADS * HEAD_DIM

    def body(dy_ref, o_ref, z_ref, w_ref, do_ref, dz_ref, dw_ref):
        w = w_ref[...]
        dw = jnp.zeros((1, HEAD_DIM), F32)
        for hd in range(HEADS):
            cols = slice(hd * HEAD_DIM, (hd + 1) * HEAD_DIM)
            seg = o_ref[0, :, cols] + o_ref[1, :, cols]
            r = lax.rsqrt(jnp.mean(seg * seg, axis=-1, keepdims=True) + EPS)
            xhat = seg * r
            z = z_ref[:, cols].astype(F32)
            sg = _sigmoid(z)
            dy = dy_ref[:, cols]
            dnrm = dy * (z * sg)
            dz_ref[:, cols] = (dy * (xhat * w) * (sg * (1.0 + z * (1.0 - sg)))).astype(BF16)
            dw = dw + jnp.sum(dnrm * xhat, axis=0, keepdims=True)
            dxhat = dnrm * w
            do_ref[:, cols] = r * (dxhat - xhat * jnp.mean(dxhat * xhat, axis=-1, keepdims=True))

        @pl.when(pl.program_id(0) == 0)
        def _():
            dw_ref[...] = jnp.zeros_like(dw_ref)

        dw_ref[...] += dw

    tile = pl.BlockSpec((ts, hd_all), lambda i: (i, 0))
    return pl.pallas_call(
        body, name=name, grid=(s // ts,),
        in_specs=[tile, pl.BlockSpec((2, ts, hd_all), lambda i: (0, i, 0)),
                  pl.BlockSpec((ts, Z_W), lambda i: (i, Z_OFF // Z_W)), _row(HEAD_DIM)],
        out_specs=[tile, tile, _row(HEAD_DIM)],
        out_shape=[jax.ShapeDtypeStruct((s, hd_all), F32), jax.ShapeDtypeStruct((s, hd_all), BF16),
                   jax.ShapeDtypeStruct((1, HEAD_DIM), F32)],
        compiler_params=_params(("arbitrary",)),
    )(dyb, o2, proj, dnw)


def _merge_fwd(ya, yb, proj, wa, wb, wo, h, g, *, name, ts=512):
    s, d = h.shape

    def body(ya_ref, yb_ref, gt_ref, wa_ref, wb_ref, wo_ref, h_ref, g_ref, pa_ref, pb_ref, mix_ref, ho_ref):
        pa = _dot(ya_ref[...], wa_ref[...])
        pb = _dot(yb_ref[...], wb_ref[...])
        pa_ref[...] = pa.astype(BF16)
        pb_ref[...] = pb.astype(BF16)
        merged = (_sigmoid(gt_ref[:, :d].astype(F32)) * pa + _sigmoid(gt_ref[:, d:].astype(F32)) * pb)
        mix = _dot(merged.astype(BF16), wo_ref[...])
        mix_ref[...] = mix.astype(BF16)
        ho_ref[...] = h_ref[...] + g_ref[...] * mix

    tile = pl.BlockSpec((ts, d), lambda i: (i, 0))
    return pl.pallas_call(
        body, name=name, grid=(s // ts,),
        in_specs=[pl.BlockSpec((ts, CONV_A), lambda i: (i, 0)), tile,
                  pl.BlockSpec((ts, GATE_W), lambda i: (i, GATE_OFF // GATE_W)),
                  _resident(wa.shape), _resident(wb.shape), _resident(wo.shape), tile, _row(d)],
        out_specs=[tile, tile, tile, tile],
        out_shape=[jax.ShapeDtypeStruct((s, d), BF16)] * 3 + [jax.ShapeDtypeStruct((s, d), F32)],
        compiler_params=_params(("parallel",)),
    )(ya, yb, proj, wa, wb, wo, h, g)


def _merge_bwd(dh, g, mix, pa, pb, proj, wa, wb, wo, *, name, ts=256):
    s, d = dh.shape

    def body(dh_ref, g_ref, mix_ref, pa_ref, pb_ref, gt_ref, wa_ref, wb_ref, wo_ref,
             dmix_ref, mg_ref, dpa_ref, dpb_ref, dgt_ref, dya_ref, dyb_ref, dg_ref):
        dh_v = dh_ref[...]
        dmix = (g_ref[...] * dh_v).astype(BF16)
        dmix_ref[...] = dmix

        @pl.when(pl.program_id(0) == 0)
        def _():
            dg_ref[...] = jnp.zeros_like(dg_ref)

        dg_ref[...] += jnp.sum(dh_v * mix_ref[...].astype(F32), axis=0, keepdims=True)
        dmerged = _dot_nt(dmix, wo_ref[...])
        pa = pa_ref[...].astype(F32)
        pb = pb_ref[...].astype(F32)
        sa = _sigmoid(gt_ref[:, :d].astype(F32))
        sb = _sigmoid(gt_ref[:, d:].astype(F32))
        mg_ref[...] = (sa * pa + sb * pb).astype(BF16)
        dpa = (dmerged * sa).astype(BF16)
        dpb = (dmerged * sb).astype(BF16)
        dpa_ref[...] = dpa
        dpb_ref[...] = dpb
        dgt_ref[:, :d] = (dmerged * pa * sa * (1.0 - sa)).astype(BF16)
        dgt_ref[:, d:] = (dmerged * pb * sb * (1.0 - sb)).astype(BF16)
        dya_ref[...] = _dot_nt(dpa, wa_ref[...])
        dyb_ref[...] = _dot_nt(dpb, wb_ref[...])

    tile = pl.BlockSpec((ts, d), lambda i: (i, 0))
    return pl.pallas_call(
        body, name=name, grid=(s // ts,),
        in_specs=[tile, _row(d), tile, tile, tile,
                  pl.BlockSpec((ts, GATE_W), lambda i: (i, GATE_OFF // GATE_W)),
                  _resident(wa.shape), _resident(wb.shape), _resident(wo.shape)],
        out_specs=[tile, tile, tile, tile, pl.BlockSpec((ts, GATE_W), lambda i: (i, 0)),
                   pl.BlockSpec((ts, CONV_A), lambda i: (i, 0)), tile, _row(d)],
        out_shape=[jax.ShapeDtypeStruct((s, d), BF16)] * 4
                  + [jax.ShapeDtypeStruct((s, GATE_W), BF16), jax.ShapeDtypeStruct((s, CONV_A), F32),
                     jax.ShapeDtypeStruct((s, d), F32), jax.ShapeDtypeStruct((1, d), F32)],
        compiler_params=_params(("arbitrary",)),
    )(dh, g, mix, pa, pb, proj, wa, wb, wo)


def _final_fwd_bwd(h, nw, target, *, name, ts=512):
    s, d = h.shape

    def body(h_ref, nw_ref, t_ref, loss_ref, dh_ref, dnw_ref):
        x = h_ref[...]
        w = nw_ref[...]
        r = lax.rsqrt(jnp.mean(x * x, axis=-1, keepdims=True) + EPS)
        xhat = x * r
        e = xhat * w - t_ref[...]
        part = 0.5 * jnp.sum(jnp.mean(e * e, axis=-1, keepdims=True))
        dy = e * (1.0 / d)
        dxhat = dy * w
        dh_ref[...] = r * (dxhat - xhat * jnp.mean(dxhat * xhat, axis=-1, keepdims=True))

        @pl.when(pl.program_id(0) == 0)
        def _():
            loss_ref[...] = jnp.zeros_like(loss_ref)
            dnw_ref[...] = jnp.zeros_like(dnw_ref)

        loss_ref[...] += jnp.broadcast_to(part, loss_ref.shape)
        dnw_ref[...] += jnp.sum(dy * xhat, axis=0, keepdims=True)

    tile = pl.BlockSpec((ts, d), lambda i: (i, 0))
    return pl.pallas_call(
        body, name=name, grid=(s // ts,),
        in_specs=[tile, _row(d), tile],
        out_specs=[_row(128), tile, _row(d)],
        out_shape=[jax.ShapeDtypeStruct((1, 128), F32), jax.ShapeDtypeStruct((s, d), F32),
                   jax.ShapeDtypeStruct((1, d), F32)],
        compiler_params=_params(("arbitrary",)),
    )(h, nw, target)


EXT = 8


def _l2norm_heads_bwd(act, dout, scale):
    outs = []
    for hd in range(HEADS):
        cols = slice(hd * HEAD_DIM, (hd + 1) * HEAD_DIM)
        seg = act[:, cols]
        nrm = lax.rsqrt(jnp.sum(seg * seg, axis=-1, keepdims=True) + EPS)
        yhat = seg * nrm
        dsg = dout[:, cols]
        outs.append((scale * nrm) * (dsg - yhat * jnp.sum(yhat * dsg, axis=-1, keepdims=True)))
    return jnp.concatenate(outs, axis=-1)


def _conv_bwd(dq2, dk2, dv2, dya, proj, conv_dn, conv_a, *, name, ts=256, carry=None):
    s = proj.shape[0]
    hd = HEADS * HEAD_DIM
    nt = s // ts
    te = ts + 2 * EXT
    kdn, ka = conv_dn.shape[0], conv_a.shape[0]

    def body(*refs):
        (qp_ref, qc_ref, qn_ref, ap_ref, ac_ref, an_ref) = refs[0:6]
        d3 = refs[6:15]
        (yp_ref, yc_ref, yn_ref, wdn_ref, wa_ref) = refs[15:20]
        (dqkv_ref, da_ref, dwdn_ref, dwa_ref) = refs[20:24]
        xs_ref, dps_ref, xa_ref, dca_ref = refs[24:28]
        i = pl.program_id(0)
        first, last = i == 0, i == nt - 1

        @pl.when(first)
        def _():
            dwdn_ref[...] = jnp.zeros_like(dwdn_ref)
            dwa_ref[...] = jnp.zeros_like(dwa_ref)

        rowe = lax.broadcasted_iota(jnp.int32, (te, 1), 0)
        inside = ~((first & (rowe < EXT)) | (last & (rowe >= EXT + ts)))
        _fill_halo(xs_ref, qp_ref, qc_ref, qn_ref, first, last)
        wdn = wdn_ref[...]
        for part in range(3):
            cols = slice(part * hd, (part + 1) * hd)
            pre = _dwconv_rows(xs_ref, wdn[:, cols], HALO - EXT - 2, te, cols)
            sg = _sigmoid(pre)
            act = pre * sg
            p_ref, c_ref, n_ref = d3[3 * part:3 * part + 3]
            dout = jnp.concatenate([p_ref[0] + p_ref[1], c_ref[0] + c_ref[1], n_ref[0] + n_ref[1]], axis=0)
            if part == 0:
                dact = _l2norm_heads_bwd(act, dout, Q_SCALE)
            elif part == 1:
                dact = _l2norm_heads_bwd(act, dout, 1.0)
            else:
                dact = dout
            dpre = jnp.where(inside, dact * (sg * (1.0 + pre * (1.0 - sg))), 0.0)
            dps_ref[:, cols] = dpre
            acc = wdn[0:1, cols] * dps_ref[EXT + 2:EXT + 2 + ts, cols]
            for tap in range(1, kdn):
                acc = acc + wdn[tap:tap + 1, cols] * dps_ref[EXT + 2 - tap:EXT + 2 - tap + ts, cols]
            dqkv_ref[:, cols] = acc.astype(BF16)
            dcur = dps_ref[EXT:EXT + ts, cols]
            for tap in range(kdn):
                dwdn_ref[tap:tap + 1, cols] += jnp.sum(
                    dcur * xs_ref[HALO - 2 + tap:HALO - 2 + tap + ts, cols], axis=0, keepdims=True)

        cv = lambda r: r[:, CONV_A:2 * CONV_A].astype(F32) * r[:, 2 * CONV_A:].astype(F32)
        _fill_halo(xa_ref, ap_ref, ac_ref, an_ref, first, last, fn=cv)
        wa = wa_ref[...]
        gate_b = jnp.concatenate([ap_ref[HALO - EXT:, 0:CONV_A], ac_ref[:, 0:CONV_A], an_ref[0:EXT, 0:CONV_A]],
                                 axis=0).astype(F32)
        dya_e = jnp.concatenate([yp_ref[...], yc_ref[...], yn_ref[...]], axis=0)
        dca_ref[...] = jnp.where(inside, dya_e * gate_b, 0.0)
        conv = _dwconv_rows(xa_ref, wa, HALO - 1, ts, slice(0, CONV_A))
        acc = wa[0:1, :] * dca_ref[EXT + 1:EXT + 1 + ts, :]
        for tap in range(1, ka):
            acc = acc + wa[tap:tap + 1, :] * dca_ref[EXT + 1 - tap:EXT + 1 - tap + ts, :]
        gc = ac_ref[:, CONV_A:2 * CONV_A].astype(F32)
        val = ac_ref[:, 2 * CONV_A:].astype(F32)
        da_ref[:, 0:CONV_A] = (yc_ref[...] * conv).astype(BF16)
        da_ref[:, CONV_A:2 * CONV_A] = (acc * val).astype(BF16)
        da_ref[:, 2 * CONV_A:] = (acc * gc).astype(BF16)
        dcur = dca_ref[EXT:EXT + ts, :]
        for tap in range(ka):
            dwa_ref[tap:tap + 1, :] += jnp.sum(
                dcur * xa_ref[HALO - 1 + tap:HALO - 1 + tap + ts, :], axis=0, keepdims=True)

    def halo3(width, rows):
        r = ts // rows
        lastb = s // rows - 1
        return [pl.BlockSpec((2, rows, width), lambda i: (0, jnp.maximum(i * r - 1, 0), 0)),
                pl.BlockSpec((2, ts, width), lambda i: (0, i, 0)),
                pl.BlockSpec((2, rows, width), lambda i: (0, jnp.minimum((i + 1) * r, lastb), 0))]

    return _launch(
        body, carry, (proj, proj, proj, proj, proj, proj, dq2, dq2, dq2, dk2, dk2, dk2, dv2, dv2, dv2,
                      dya, dya, dya, conv_dn, conv_a), name=name, grid=(nt,),
        in_specs=(_halo_specs(ts, QKV_W, 0, s) + _halo_specs(ts, A_W, A_OFF // A_W, s)
                  + halo3(hd, EXT) * 3 + _halo_specs(ts, CONV_A, 0, s, rows=EXT)
                  + [_resident(conv_dn.shape), _resident(conv_a.shape)]),
        out_specs=[pl.BlockSpec((ts, QKV_W), lambda i: (i, 0)), pl.BlockSpec((ts, A_W), lambda i: (i, 0)),
                   pl.BlockSpec((8, QKV_W), lambda i: (0, 0)), pl.BlockSpec((8, CONV_A), lambda i: (0, 0))],
        out_shape=[jax.ShapeDtypeStruct((s, QKV_W), BF16), jax.ShapeDtypeStruct((s, A_W), BF16),
                   jax.ShapeDtypeStruct((8, QKV_W), F32), jax.ShapeDtypeStruct((8, CONV_A), F32)],
        scratch_shapes=[pltpu.VMEM((ts + 2 * HALO, QKV_W), F32), pltpu.VMEM((te, QKV_W), F32),
                        pltpu.VMEM((ts + 2 * HALO, CONV_A), F32), pltpu.VMEM((te, CONV_A), F32)],
        sem=("arbitrary",))


IN_A = (0, 1536)
IN_QKV = (1536, 4608)
IN_Z = (4608, 5632)
IN_BA = 5632
IN_GATE = (5664, 7712)
IN_COLS = 7712
G_REPL = 4


def _split_w_in(w_in):
    sl = lambda ab: w_in[:, ab[0]:ab[1]]
    w_main = jnp.concatenate([sl(IN_QKV), sl(IN_Z), sl(IN_GATE), sl(IN_A)], axis=1)
    blocks = []
    for d in range(2):
        beta = w_in[:, IN_BA + 8 * d:IN_BA + 8 * d + 8]
        alpha = w_in[:, IN_BA + 16 + 8 * d:IN_BA + 24 + 8 * d]
        pad = jnp.zeros((w_in.shape[0], BA_W - 8 - 8 * G_REPL), w_in.dtype)
        blocks += [beta] + [alpha] * G_REPL + [pad]
    return w_main, jnp.concatenate(blocks, axis=1)


def _merge_dw_in(dw_qkv, dw_z, dw_gate, dw_a, dw_ba):
    ba = [dw_ba[:, 0:8], dw_ba[:, BA_W:BA_W + 8], dw_ba[:, 8:16], dw_ba[:, BA_W + 8:BA_W + 16]]
    return jnp.concatenate([dw_a, dw_qkv, dw_z] + ba + [dw_gate], axis=1)


def _decay_rows(a_log_fwd, dt_bias_fwd, a_log_bwd, dt_bias_bwd):
    def rows(f, b):
        out = []
        for vec in (f, b):
            vec = vec.reshape(HEADS)
            out.append(jnp.concatenate([jnp.zeros((8,), F32)] + [vec] * G_REPL
                                       + [jnp.zeros((BA_W - 8 - 8 * G_REPL,), F32)])[None])
        return jnp.stack(out)
    return rows(a_log_fwd, a_log_bwd), rows(dt_bias_fwd, dt_bias_bwd)


def _local_step(x, target, mod9, wt, comm):
    s, d = x.shape
    n = s // CHUNK
    wt = dict(wt)
    sh1, sc1, g1, sh2, sc2, g2, sh3, sc3, g3 = [mod9[i:i + 1] for i in range(9)]
    alog, dtb = _decay_rows(wt["a_log_fwd"], wt["dt_bias_fwd"], wt["a_log_bwd"], wt["dt_bias_bwd"])

    (u1, a1, b1, f1), got = comm.gather(
        ["w_ffn1_down", "w_in"],
        lambda c: _ffn_up_fwd(x, wt["norm_ffn1"], sc1, sh1, wt["w_ffn1_up"], name="ffn1_up", carry=c))
    wt.update(got)
    w_main, w_ba = _split_w_in(wt["w_in"])
    y1, h1 = _ffn_down_fwd(f1, wt["w_ffn1_down"], x, g1, name="ffn1_down")
    (u2, proj, ba), got = comm.gather(
        ["w_a_out", "w_b_out", "w_out", "w_ffn2_up", "w_ffn2_down"],
        lambda c: _in_proj_fwd(h1, wt["norm_mix"], sc2, sh2, w_main, w_ba, name="in_proj", carry=c))
    wt.update(got)
    scal = _scal_fwd(ba, alog, dtb, name="scal_fwd")
    grow = scal[:, :, L_G:L_G + 8].reshape(2, n, CHUNK, HEADS).transpose(0, 1, 3, 2).reshape(
        2, n, N_GROUPS, GROWS)
    q, k, v, ya = _conv_fwd(proj, wt["conv_dn"], wt["conv_a"], name="conv_fwd")
    o2, states, tinv, vn = _delta_fwd(q, k, v, scal, grow, name="delta_fwd")
    yb = _gate_norm_fwd(o2, proj, wt["dn_norm"], name="gate_norm_fwd")
    pa, pb, mix, h2 = _merge_fwd(ya, yb, proj, wt["w_a_out"], wt["w_b_out"], wt["w_out"], h1, g2,
                                 name="merge_fwd")
    u3, a3, b3, f3 = _ffn_up_fwd(h2, wt["norm_ffn2"], sc3, sh3, wt["w_ffn2_up"], name="ffn2_up")
    y3, h3 = _ffn_down_fwd(f3, wt["w_ffn2_down"], h2, g3, name="ffn2_down")
    loss, dh3, dnorm_final = _final_fwd_bwd(h3, wt["norm_final"], target, name="final")

    dy3, dab3, dg3 = _ffn_bwd_act(dh3, g3, y3, a3, b3, wt["w_ffn2_down"], name="ffn2_bwd_act")
    dh2, dn3, dsc3, dsh3 = _norm_mod_matmul_bwd([(dab3, wt["w_ffn2_up"])], h2, wt["norm_ffn2"], sc3, dh3,
                                                name="ffn2_bwd_up")
    gw = {}
    gw["w_ffn2_up"] = _matmul_tn(u3, dab3, name="dw_ffn2_up", tm=1024, tn=1408)
    gw["w_ffn2_down"] = _matmul_tn(f3, dy3, name="dw_ffn2_down", tm=1408, tn=1024)

    dmix, merged, dpa, dpb, dgates, dya, dyb, dg2 = _merge_bwd(
        dh2, g2, mix, pa, pb, proj, wt["w_a_out"], wt["w_b_out"], wt["w_out"], name="merge_bwd")
    gw["w_out"] = _matmul_tn(merged, dmix, name="dw_out", tm=1024, tn=1024)
    gw["w_a_out"] = _matmul_tn(ya, dpa, name="dw_a_out", tm=512, tn=1024)
    gw["w_b_out"] = _matmul_tn(yb, dpb, name="dw_b_out", tm=1024, tn=1024)
    do, dz, ddn = _gate_norm_bwd(dyb, o2, proj, wt["dn_norm"], name="gate_norm_bwd")
    recv = {}
    (dq2, dk2, dv2, dscal, drow), got = comm.scatter(
        {nm: gw.pop(nm) for nm in ("w_ffn2_up", "w_ffn2_down")},
        lambda c: _delta_bwd(q, k, v, scal, grow, states, tinv, vn, do, name="delta_bwd", carry=c))
    recv.update(got)
    drow_p = jnp.pad(drow.reshape(2, n, HEADS, CHUNK).transpose(0, 1, 3, 2).reshape(2, s, HEADS),
                     ((0, 0), (0, 0), (L_G, BA_W - L_G - HEADS)))
    dba, dalog, ddtb = _scal_bwd(dscal, drow_p, ba, alog, dtb, name="scal_bwd")
    (dqkv, dbr_a, dconv_dn, dconv_a), got = comm.scatter(
        {nm: gw.pop(nm) for nm in ("w_out", "w_a_out", "w_b_out")},
        lambda c: _conv_bwd(dq2, dk2, dv2, dya, proj, wt["conv_dn"], wt["conv_a"], name="conv_bwd", carry=c))
    recv.update(got)
    w_qkv = w_main[:, 0:QKV_W]
    w_z = w_main[:, Z_OFF:Z_OFF + Z_W]
    w_gate = w_main[:, GATE_OFF:GATE_OFF + GATE_W]
    w_bra = w_main[:, A_OFF:A_OFF + A_W]
    dh1, dn2, dsc2, dsh2 = _norm_mod_matmul_bwd(
        [(dqkv, w_qkv), (dz, w_z), (dgates, w_gate), (dbr_a, w_bra), (dba, w_ba)],
        h1, wt["norm_mix"], sc2, dh2, name="in_proj_bwd")
    gw["w_in"] = _merge_dw_in(
        _matmul_tn(u2, dqkv, name="dw_in_qkv", tm=1024, tn=1536),
        _matmul_tn(u2, dz, name="dw_in_z", tm=1024, tn=1024),
        _matmul_tn(u2, dgates, name="dw_in_gate", tm=1024, tn=1024),
        _matmul_tn(u2, dbr_a, name="dw_in_a", tm=1024, tn=1536),
        _matmul_tn(u2, dba, name="dw_in_ba", tm=1024, tn=2 * BA_W))

    (dy1, dab1, dg1), got = comm.scatter(
        {"w_in": gw.pop("w_in")},
        lambda c: _ffn_bwd_act(dh1, g1, y1, a1, b1, wt["w_ffn1_down"], name="ffn1_bwd_act", carry=c))
    recv.update(got)
    dw_down1 = _matmul_tn(f1, dy1, name="dw_ffn1_down", tm=1408, tn=1024)
    dx, dn1, dsc1, dsh1 = _norm_mod_matmul_bwd([(dab1, wt["w_ffn1_up"])], x, wt["norm_ffn1"], sc1, dh1,
                                               name="ffn1_bwd_up")
    dw_up1, got = comm.scatter(
        {"w_ffn1_down": dw_down1},
        lambda c: _matmul_tn(u1, dab1, name="dw_ffn1_up", tm=1024, tn=1408, carry=c))
    recv.update(got)
    recv.update(comm.scatter({"w_ffn1_up": dw_up1}, None, name="scatter_last")[1])

    small = {
        "mod": jnp.concatenate([dsh1, dsc1, dg1, dsh2, dsc2, dg2, dsh3, dsc3, dg3], axis=1),
        "norm_ffn1": dn1, "norm_mix": dn2, "norm_ffn2": dn3, "norm_final": dnorm_final,
        "a_log_fwd": dalog[0, :, L_G:L_G + 8], "dt_bias_fwd": ddtb[0, :, L_G:L_G + 8],
        "a_log_bwd": dalog[1, :, L_G:L_G + 8], "dt_bias_bwd": ddtb[1, :, L_G:L_G + 8],
        "dn_norm": ddn,
        "conv_a": dconv_a[0:3].reshape(1, -1), "conv_dn": dconv_dn[0:5].reshape(1, -1),
    }
    return loss, dx, recv, small


def _full_weight(name, g):
    if name in COL_SHARDED + CONV_SHARDED:
        return g.transpose(1, 0, 2).reshape(g.shape[1], -1)
    return g.reshape(-1, g.shape[-1])


def _grad_pieces(name, g):
    g = g.astype(BF16)
    if name in COL_SHARDED:
        return g.reshape(g.shape[0], N_DEV, -1).transpose(1, 0, 2)
    return g.reshape(N_DEV, -1, g.shape[-1])


class _MeshComm:
    def __init__(self, shards):
        self.shards = shards

    def _run(self, xs, carrier, name, gather):
        if carrier is None:
            return None, _exchange(xs, name=name, gather=gather)
        return carrier((xs, gather))

    def gather(self, names, carrier=None, name=None):
        outs, got = self._run([self.shards[nm] for nm in names], carrier, name, True)
        return outs, {nm: _full_weight(nm, g) for nm, g in zip(names, got)}

    def scatter(self, grads, carrier=None, name=None):
        names = list(grads)
        outs, got = self._run([_grad_pieces(nm, grads[nm]) for nm in names], carrier, name, False)
        return outs, dict(zip(names, got))


def _mod_fwd(c_all, w_ada, *, name):
    def body(c_ref, w_ref, o_ref):
        cv = c_ref[...]
        o_ref[...] = _dot3(cv * _sigmoid(cv), w_ref[...])

    return pl.pallas_call(
        body, name=name, out_shape=jax.ShapeDtypeStruct((c_all.shape[0], w_ada.shape[1]), F32),
        compiler_params=_params(),
    )(c_all, w_ada)


def _adamw_math(w, g, m, v):
    m_new = ADAM_B1 * m + (1.0 - ADAM_B1) * g
    v_new = ADAM_B2 * v + (1.0 - ADAM_B2) * (g * g)
    m_hat = m_new / (1.0 - ADAM_B1 ** ADAM_STEP)
    v_hat = v_new / (1.0 - ADAM_B2 ** ADAM_STEP)
    delta = -ADAM_LR * (m_hat / (jnp.sqrt(v_hat) + ADAM_EPS) + ADAM_WD * w)
    return delta, m_new, v_new


def _reduce_adamw(pieces, w, m, v, *, name, tr):
    r, c = w.shape

    def body(p_ref, w_ref, m_ref, v_ref, g_ref, d_ref, mo_ref, vo_ref):
        g = p_ref[0].astype(F32)
        for src in range(1, N_DEV):
            g = g + p_ref[src].astype(F32)
        g_ref[...] = g
        d_ref[...], mo_ref[...], vo_ref[...] = _adamw_math(w_ref[...], g, m_ref[...], v_ref[...])

    tile = pl.BlockSpec((tr, c), lambda i: (i, 0))
    return pl.pallas_call(
        body, name=name, grid=(r // tr,),
        in_specs=[pl.BlockSpec((N_DEV, tr, c), lambda i: (0, i, 0)), tile, tile, tile],
        out_specs=[tile] * 4, out_shape=[jax.ShapeDtypeStruct((r, c), F32)] * 4,
        compiler_params=_params(("parallel",)),
    )(pieces, w, m, v)


def _ada_grad_adamw(c_all_t, dmod_cols, w, m, v, *, name, tr=256):
    r, c = w.shape

    def body(c_ref, dm_ref, w_ref, m_ref, v_ref, g_ref, d_ref, mo_ref, vo_ref):
        cv = c_ref[...]
        act = cv * _sigmoid(cv)
        dm = dm_ref[...]
        g = act[:, 0:1] * dm[0:1, :]
        for b in range(1, N_DEV):
            g = g + act[:, b:b + 1] * dm[b:b + 1, :]
        g_ref[...] = g
        d_ref[...], mo_ref[...], vo_ref[...] = _adamw_math(w_ref[...], g, m_ref[...], v_ref[...])

    tile = pl.BlockSpec((tr, c), lambda i: (i, 0))
    return pl.pallas_call(
        body, name=name, grid=(r // tr,),
        in_specs=[pl.BlockSpec((tr, N_DEV), lambda i: (i, 0)), pl.BlockSpec((N_DEV, c), lambda i: (0, 0)),
                  tile, tile, tile],
        out_specs=[tile] * 4, out_shape=[jax.ShapeDtypeStruct((r, c), F32)] * 4,
        compiler_params=_params(("parallel",)),
    )(c_all_t, dmod_cols, w, m, v)


def _sum_rows(parts, *, name):
    def body(p_ref, o_ref):
        acc = p_ref[0:1, :]
        for src in range(1, N_DEV):
            acc = acc + p_ref[src:src + 1, :]
        o_ref[...] = acc

    return pl.pallas_call(
        body, name=name, out_shape=jax.ShapeDtypeStruct((1, parts.shape[1]), F32), compiler_params=_params(),
    )(parts)


def _adamw_rows(g, w, m, v, *, name):
    def body(g_ref, w_ref, m_ref, v_ref, d_ref, mo_ref, vo_ref):
        d_ref[...], mo_ref[...], vo_ref[...] = _adamw_math(w_ref[...], g_ref[...], m_ref[...], v_ref[...])

    return pl.pallas_call(
        body, name=name, out_shape=[jax.ShapeDtypeStruct(g.shape, F32)] * 3, compiler_params=_params(),
    )(g, w, m, v)


WEIGHTS = ["w_ada", "b_ada", "norm_ffn1", "w_ffn1_up", "w_ffn1_down", "norm_mix", "w_in", "conv_a", "conv_dn",
           "a_log_fwd", "dt_bias_fwd", "a_log_bwd", "dt_bias_bwd", "dn_norm", "w_a_out", "w_b_out", "w_out",
           "norm_ffn2", "w_ffn2_up", "w_ffn2_down", "norm_final"]
COL_SHARDED = ["w_ffn1_up", "w_in", "w_a_out", "w_ffn2_up"]
ROW_SHARDED = ["w_ffn1_down", "w_b_out", "w_out", "w_ffn2_down"]
CONV_SHARDED = ["conv_a", "conv_dn"]
REPLICATED = ["b_ada", "norm_ffn1", "norm_mix", "a_log_fwd", "dt_bias_fwd", "a_log_bwd", "dt_bias_bwd",
              "dn_norm", "norm_ffn2", "norm_final"]
SMALL_ORDER = ["mod", "norm_ffn1", "norm_mix", "norm_ffn2", "norm_final", "a_log_fwd", "dt_bias_fwd",
               "a_log_bwd", "dt_bias_bwd", "dn_norm", "conv_a", "conv_dn"]
REDUCE_ROWS = {"w_ffn1_up": 256, "w_in": 256, "w_a_out": 256, "w_ffn2_up": 256,
               "w_ffn1_down": 176, "w_b_out": 128, "w_out": 128, "w_ffn2_down": 176}


def _pad_lanes(row):
    pad = (-row.shape[1]) % 128
    return jnp.pad(row, ((0, 0), (0, pad)))


def _unstack_cols(g):
    return g.transpose(1, 0, 2).reshape(g.shape[1], -1)


def _stack_cols(w):
    k = w.shape[0]
    return w.reshape(k, N_DEV, -1).transpose(1, 0, 2)


def kernel(x, c, w_ada, b_ada, norm_ffn1, w_ffn1_up, w_ffn1_down, norm_mix, w_in, conv_a, conv_dn, a_log_fwd, dt_bias_fwd, a_log_bwd, dt_bias_bwd, dn_norm, w_a_out, w_b_out, w_out, norm_ffn2, w_ffn2_up, w_ffn2_down, norm_final, loss_target, m_w_ada, m_b_ada, m_norm_ffn1, m_w_ffn1_up, m_w_ffn1_down, m_norm_mix, m_w_in, m_conv_a, m_conv_dn, m_a_log_fwd, m_dt_bias_fwd, m_a_log_bwd, m_dt_bias_bwd, m_dn_norm, m_w_a_out, m_w_b_out, m_w_out, m_norm_ffn2, m_w_ffn2_up, m_w_ffn2_down, m_norm_final, v_w_ada, v_b_ada, v_norm_ffn1, v_w_ffn1_up, v_w_ffn1_down, v_norm_mix, v_w_in, v_conv_a, v_conv_dn, v_a_log_fwd, v_dt_bias_fwd, v_a_log_bwd, v_dt_bias_bwd, v_dn_norm, v_w_a_out, v_w_b_out, v_w_out, v_norm_ffn2, v_w_ffn2_up, v_w_ffn2_down, v_norm_final):
    args = dict(locals())
    w_loc = {n: args[n] for n in WEIGHTS}
    m_loc = {n: args["m_" + n] for n in WEIGHTS}
    v_loc = {n: args["v_" + n] for n in WEIGHTS}
    me = _flat_index(_my_position())
    d_model = x.shape[-1]

    big = COL_SHARDED + ROW_SHARDED
    shards = {n: w_loc[n][0].astype(BF16) for n in big}
    shards.update({n: w_loc[n][0] for n in CONV_SHARDED})
    shards["c"] = c
    comm = _MeshComm(shards)
    wt = comm.gather(["c", "conv_a", "conv_dn", "w_ffn1_up"], name="gather_first")[1]
    c_all = wt.pop("c")
    for n in REPLICATED[1:]:
        wt[n] = w_loc[n].reshape(1, -1)

    mod_cols = _mod_fwd(c_all, w_ada[0], name="mod_fwd")
    mod_all = _exchange([mod_cols], name="gather_mod", gather=True)[0]
    mod_mine = lax.dynamic_index_in_dim(mod_all, me, axis=1, keepdims=False).reshape(1, -1) + b_ada
    mod9 = mod_mine.reshape(9, d_model)

    loss_loc, dx, recv, small = _local_step(x[0], loss_target[0], mod9, wt, comm)
    loss = lax.psum(loss_loc[0, 0], MESH_AXES)

    res = {}
    for n in big:
        res[n] = _reduce_adamw(recv[n], w_loc[n][0], m_loc[n][0], v_loc[n][0], name="adamw_" + n,
                               tr=REDUCE_ROWS[n])

    packed = _pad_lanes(jnp.concatenate([small[n].reshape(1, -1) for n in SMALL_ORDER], axis=1))
    parts = _exchange([packed], name="gather_small", gather=True)[0].reshape(N_DEV, -1)
    total = _sum_rows(parts, name="sum_small")
    off = 0
    gsmall = {}
    for n in SMALL_ORDER:
        size = small[n].size
        gsmall[n] = total[:, off:off + size]
        off += size
    dmod_all = parts[:, 0:9 * d_model]
    ada_cols = w_ada.shape[-1]
    dmod_cols = lax.dynamic_slice_in_dim(dmod_all, me * ada_cols, ada_cols, axis=1)
    res["w_ada"] = _ada_grad_adamw(c_all.T, dmod_cols, w_ada[0], m_w_ada[0], v_w_ada[0], name="adamw_w_ada")
    g_rows = {"b_ada": gsmall["mod"]}
    for n in REPLICATED[1:]:
        g_rows[n] = gsmall[n]
    for n in CONV_SHARDED:
        taps, width = w_loc[n].shape[1], w_loc[n].shape[2]
        full = gsmall[n].reshape(taps, -1)
        g_rows[n] = lax.dynamic_slice_in_dim(full, me * width, width, axis=1).reshape(1, -1)
    row_names = REPLICATED + CONV_SHARDED
    cat = lambda src: _pad_lanes(jnp.concatenate([src[n].reshape(1, -1) for n in row_names], axis=1))
    g_cat = cat(g_rows)
    d_cat, m_cat, v_cat = _adamw_rows(g_cat, cat(w_loc), cat(m_loc), cat(v_loc), name="adamw_small")
    off = 0
    for n in row_names:
        size = w_loc[n].size
        res[n] = tuple(t[:, off:off + size] for t in (g_cat, d_cat, m_cat, v_cat))
        off += size

    outs = [loss, dx[None]]
    for kind in range(4):
        for n in WEIGHTS:
            outs.append(res[n][kind].reshape(w_loc[n].shape))
    return tuple(outs)
```

```python
import functools
import math

import jax
import jax.numpy as jnp
from jax import lax
from jax.experimental import pallas as pl
from jax.experimental.pallas import tpu as pltpu

F32 = jnp.float32
BF16 = jnp.bfloat16
EPS = 1e-6
N_DEV = 8
CHUNK = 64
HEADS = 8
HEAD_DIM = 128
MESH_AXES = ("x", "y", "c")
VMEM_LIMIT_BYTES = 56 * 1024 * 1024

ADAM_LR = 0.001
ADAM_B1 = 0.9
ADAM_B2 = 0.999
ADAM_EPS = 1e-08
ADAM_WD = 0.01
ADAM_STEP = 10


def _params(sem=None):
    return pltpu.CompilerParams(dimension_semantics=sem, vmem_limit_bytes=VMEM_LIMIT_BYTES)


def _row(n):
    return pl.BlockSpec((1, n), lambda *_: (0, 0))


def _resident(shape):
    nd = len(shape)
    return pl.BlockSpec(shape, lambda *_: (0,) * nd, pipeline_mode=pl.Buffered(1))


def _col_window(w, width, col_block):
    return pl.BlockSpec((w.shape[0], width), lambda *_: (0, col_block), pipeline_mode=pl.Buffered(1))


def _sigmoid(x):
    return 1.0 / (1.0 + jnp.exp(-x))


def _dot(a, b):
    return jnp.dot(a, b, preferred_element_type=F32)


def _dot_nt(a, b):
    return lax.dot_general(a, b, (((1,), (1,)), ((), ())), preferred_element_type=F32)


def _dot_tn(a, b):
    return lax.dot_general(a, b, (((0,), (0,)), ((), ())), preferred_element_type=F32)


def _split_bf16(x):
    hi = x.astype(BF16)
    lo = (x - hi.astype(F32)).astype(BF16)
    return hi, lo


def _dot3(a, b, dot=_dot):
    ah, al = a if isinstance(a, tuple) else _split_bf16(a)
    bh, bl = b if isinstance(b, tuple) else _split_bf16(b)
    return dot(ah, bh) + dot(ah, bl) + dot(al, bh)


def _dot_exact(a, b):
    return jnp.dot(a, b, preferred_element_type=F32, precision=lax.Precision.HIGHEST)


def _my_position():
    return tuple(lax.axis_index(a) for a in MESH_AXES)


def _peer(pos, kk):
    return tuple((1 - p) if (kk >> (2 - b)) & 1 else p for b, p in enumerate(pos))


def _flat_index(pos):
    return pos[0] * 4 + pos[1] * 2 + pos[2]


_ANY = pl.BlockSpec(memory_space=pl.ANY)


class _AllToAll:
    def __init__(self, in_refs, out_refs, send_sems, recv_sems, local_sems):
        pos = _my_position()
        me = _flat_index(pos)
        self.copies = []
        for t in range(len(in_refs)):
            self.copies.append(pltpu.make_async_copy(in_refs[t].at[me], out_refs[t].at[me], local_sems.at[t]))
            for kk in range(1, N_DEV):
                peer = _peer(pos, kk)
                self.copies.append(pltpu.make_async_remote_copy(
                    src_ref=in_refs[t].at[_flat_index(peer)], dst_ref=out_refs[t].at[me],
                    send_sem=send_sems.at[t, kk - 1], recv_sem=recv_sems.at[t, kk - 1],
                    device_id=peer, device_id_type=pl.DeviceIdType.MESH))

    def start(self):
        for cp in self.copies:
            cp.start()

    def finish(self):
        for cp in self.copies:
            cp.wait()


class _AllGather:
    def __init__(self, in_refs, out_refs, send_sems, recv_sems, local_sems):
        self.refs = (in_refs, out_refs, send_sems, recv_sems, local_sems)
        x, y, c = _my_position()
        self.me, self.sibling = (x, y, c), (x, y, 1 - c)
        self.chips = [(1 - x, y), (x, 1 - y), (1 - x, 1 - y)]
        self.core = c

    def _copy(self, t, k, block, to, own=False):
        in_refs, out_refs, send_sems, recv_sems, _ = self.refs
        rows = out_refs[t].at[_flat_index(block)]
        return pltpu.make_async_remote_copy(
            src_ref=in_refs[t] if own else rows, dst_ref=rows,
            send_sem=send_sems.at[t, k], recv_sem=recv_sems.at[t, k],
            device_id=to, device_id_type=pl.DeviceIdType.MESH)

    def _local(self, t):
        in_refs, out_refs, _, _, local_sems = self.refs
        return pltpu.make_async_copy(in_refs[t], out_refs[t].at[_flat_index(self.me)], local_sems.at[t])

    def start(self):
        c = self.core
        for t in range(len(self.refs[0])):
            self._local(t).start()
            self._copy(t, 0, self.me, self.sibling, own=True).start()
            for j, chip in enumerate(self.chips):
                self._copy(t, 1 + j, self.me, (*chip, c), own=True).start()

    def finish(self):
        c = self.core
        n_t = len(self.refs[0])
        for t in range(n_t):
            for j, chip in enumerate(self.chips):
                self._copy(t, 1 + j, (*chip, c), self.me).wait_recv()
                self._copy(t, 4 + j, (*chip, c), self.sibling).start()
        for t in range(n_t):
            self._copy(t, 0, self.sibling, self.me).wait_recv()
            for j, chip in enumerate(self.chips):
                self._copy(t, 4 + j, (*chip, 1 - c), self.me).wait_recv()
            self._copy(t, 0, self.me, self.sibling, own=True).wait_send()
            for j, chip in enumerate(self.chips):
                self._copy(t, 1 + j, self.me, (*chip, c), own=True).wait_send()
                self._copy(t, 4 + j, (*chip, c), self.sibling).wait_send()
            self._local(t).wait()


def _exchange_plan(in_refs, out_refs, send_sems, recv_sems, local_sems, gather):
    return (_AllGather if gather else _AllToAll)(in_refs, out_refs, send_sems, recv_sems, local_sems)


def _exchange_shapes(xs, gather):
    out_shape = [jax.ShapeDtypeStruct(((N_DEV,) + x.shape) if gather else x.shape, x.dtype) for x in xs]
    sems = [pltpu.SemaphoreType.DMA((len(xs), N_DEV - 1)), pltpu.SemaphoreType.DMA((len(xs), N_DEV - 1)),
            pltpu.SemaphoreType.DMA((len(xs),))]
    return out_shape, sems


def _exchange(xs, *, name, gather):
    nt = len(xs)

    def body(*refs):
        plan = _exchange_plan(refs[:nt], refs[nt:2 * nt], *refs[2 * nt:], gather)
        plan.start()
        plan.finish()

    out_shape, sems = _exchange_shapes(xs, gather)
    return pl.pallas_call(body, name=name, in_specs=[_ANY] * nt, out_specs=[_ANY] * nt, out_shape=out_shape,
                          scratch_shapes=sems)(*xs)


def _launch(body, carry, args, *, name, grid, in_specs, out_specs, out_shape, scratch_shapes=(), sem):
    single = not isinstance(out_shape, (list, tuple))
    out_specs = [out_specs] if single else list(out_specs)
    out_shape = [out_shape] if single else list(out_shape)
    if carry is None:
        outs = pl.pallas_call(body, name=name, grid=grid, in_specs=list(in_specs), out_specs=out_specs,
                              out_shape=out_shape, scratch_shapes=list(scratch_shapes),
                              compiler_params=_params(sem))(*args)
        return outs[0] if single else outs
    xs, gather = carry
    nt, n_in, n_out, n_scr = len(xs), len(args), len(out_shape), len(scratch_shapes)
    x_shape, sems = _exchange_shapes(xs, gather)

    def wrapped(*refs):
        c_in, x_in = refs[:n_in], refs[n_in:n_in + nt]
        c_out = refs[n_in + nt:n_in + nt + n_out]
        x_out = refs[n_in + nt + n_out:n_in + 2 * nt + n_out]
        scr = refs[n_in + 2 * nt + n_out:]
        ids = [pl.program_id(a) for a in range(len(grid))]
        first = functools.reduce(jnp.logical_and, [i == 0 for i in ids])
        last = functools.reduce(jnp.logical_and, [i == g - 1 for i, g in zip(ids, grid)])
        plan = lambda: _exchange_plan(x_in, x_out, *scr[n_scr:], gather)

        @pl.when(first)
        def _():
            plan().start()

        body(*c_in, *c_out, *scr[:n_scr])

        @pl.when(last)
        def _():
            plan().finish()

    outs = pl.pallas_call(
        wrapped, name=name, grid=grid, in_specs=list(in_specs) + [_ANY] * nt,
        out_specs=out_specs + [_ANY] * nt, out_shape=out_shape + x_shape,
        scratch_shapes=list(scratch_shapes) + sems,
        compiler_params=_params(("arbitrary",) * len(grid)))(*args, *xs)
    compute = outs[:n_out]
    return (compute[0] if single else compute), outs[n_out:]


def _norm_mod(x, nw, sc, sh):
    r = lax.rsqrt(jnp.mean(x * x, axis=-1, keepdims=True) + EPS)
    return (x * r * nw) * (1.0 + sc) + sh


def _norm_mod_bwd(x, nw, sc, du):
    r = lax.rsqrt(jnp.mean(x * x, axis=-1, keepdims=True) + EPS)
    xhat = x * r
    n = xhat * nw
    dsh = jnp.sum(du, axis=0, keepdims=True)
    dsc = jnp.sum(du * n, axis=0, keepdims=True)
    dn = du * (1.0 + sc)
    dnw = jnp.sum(dn * xhat, axis=0, keepdims=True)
    dxhat = dn * nw
    dx = r * (dxhat - xhat * jnp.mean(dxhat * xhat, axis=-1, keepdims=True))
    return dx, dnw, dsc, dsh


def _ffn_up_fwd(h, nw, sc, sh, wup, *, name, ts=512, tn=1408, carry=None):
    s, d = h.shape
    f_dim = wup.shape[1] // 2
    nj = f_dim // tn

    def body(h_ref, nw_ref, sc_ref, sh_ref, wa_ref, wb_ref, u_ref, a_ref, b_ref, f_ref):
        @pl.when(pl.program_id(1) == 0)
        def _():
            u_ref[...] = _norm_mod(h_ref[...], nw_ref[...], sc_ref[...], sh_ref[...]).astype(BF16)

        u = u_ref[...]
        a = _dot(u, wa_ref[...])
        b = _dot(u, wb_ref[...])
        a_ref[...] = a.astype(BF16)
        b_ref[...] = b.astype(BF16)
        f_ref[...] = (a * _sigmoid(a) * b).astype(BF16)

    return _launch(
        body, carry, (h, nw, sc, sh, wup, wup), name=name, grid=(s // ts, nj),
        in_specs=[pl.BlockSpec((ts, d), lambda i, j: (i, 0)), _row(d), _row(d), _row(d),
                  pl.BlockSpec((d, tn), lambda i, j: (0, j)),
                  pl.BlockSpec((d, tn), lambda i, j: (0, j + nj))],
        out_specs=[pl.BlockSpec((ts, d), lambda i, j: (i, 0)),
                   pl.BlockSpec((ts, tn), lambda i, j: (i, j)),
                   pl.BlockSpec((ts, tn), lambda i, j: (i, j)),
                   pl.BlockSpec((ts, tn), lambda i, j: (i, j))],
        out_shape=[jax.ShapeDtypeStruct((s, d), BF16)] + [jax.ShapeDtypeStruct((s, f_dim), BF16)] * 3,
        sem=("parallel", "arbitrary"))


def _ffn_down_fwd(f, wd, h, g, *, name, ts=512):
    s, f_dim = f.shape
    d = wd.shape[1]

    def body(f_ref, wd_ref, h_ref, g_ref, y_ref, ho_ref):
        y = _dot(f_ref[...], wd_ref[...])
        y_ref[...] = y.astype(BF16)
        ho_ref[...] = h_ref[...] + (0.5 * g_ref[...]) * y

    return pl.pallas_call(
        body, name=name, grid=(s // ts,),
        in_specs=[pl.BlockSpec((ts, f_dim), lambda i: (i, 0)), _resident((f_dim, d)),
                  pl.BlockSpec((ts, d), lambda i: (i, 0)), _row(d)],
        out_specs=[pl.BlockSpec((ts, d), lambda i: (i, 0)), pl.BlockSpec((ts, d), lambda i: (i, 0))],
        out_shape=[jax.ShapeDtypeStruct((s, d), BF16), jax.ShapeDtypeStruct((s, d), F32)],
        compiler_params=_params(("parallel",)),
    )(f, wd, h, g)


def _ffn_bwd_act(dh, g, y, a, b, wd, *, name, ts=256, carry=None):
    s, d = dh.shape
    f_dim = a.shape[1]

    def body(dh_ref, g_ref, y_ref, a_ref, b_ref, wd_ref, dy_ref, dab_ref, dg_ref):
        dh_v = dh_ref[...]
        dy = ((0.5 * g_ref[...]) * dh_v).astype(BF16)
        dy_ref[...] = dy
        part = jnp.sum(0.5 * dh_v * y_ref[...].astype(F32), axis=0, keepdims=True)

        @pl.when(pl.program_id(0) == 0)
        def _():
            dg_ref[...] = jnp.zeros_like(dg_ref)

        dg_ref[...] += part
        df = _dot_nt(dy, wd_ref[...])
        av = a_ref[...].astype(F32)
        bv = b_ref[...].astype(F32)
        sg = _sigmoid(av)
        dab_ref[:, :f_dim] = (df * bv * (sg * (1.0 + av * (1.0 - sg)))).astype(BF16)
        dab_ref[:, f_dim:] = (df * (av * sg)).astype(BF16)

    return _launch(
        body, carry, (dh, g, y, a, b, wd), name=name, grid=(s // ts,),
        in_specs=[pl.BlockSpec((ts, d), lambda i: (i, 0)), _row(d),
                  pl.BlockSpec((ts, d), lambda i: (i, 0)),
                  pl.BlockSpec((ts, f_dim), lambda i: (i, 0)),
                  pl.BlockSpec((ts, f_dim), lambda i: (i, 0)),
                  _resident((f_dim, d))],
        out_specs=[pl.BlockSpec((ts, d), lambda i: (i, 0)),
                   pl.BlockSpec((ts, 2 * f_dim), lambda i: (i, 0)), _row(d)],
        out_shape=[jax.ShapeDtypeStruct((s, d), BF16), jax.ShapeDtypeStruct((s, 2 * f_dim), BF16),
                   jax.ShapeDtypeStruct((1, d), F32)],
        sem=("arbitrary",))


def _norm_mod_matmul_bwd(pairs, h, nw, sc, dh_in, *, name, ts=256, carry=None):
    s, d = h.shape
    n_pairs = len(pairs)

    def body(*refs):
        dx_refs = refs[:n_pairs]
        w_refs = refs[n_pairs:2 * n_pairs]
        h_ref, nw_ref, sc_ref, dhi_ref, dho_ref, dnw_ref, dsc_ref, dsh_ref = refs[2 * n_pairs:]
        du = _dot_nt(dx_refs[0][...], w_refs[0][...])
        for k in range(1, n_pairs):
            du = du + _dot_nt(dx_refs[k][...], w_refs[k][...])
        dx, dnw, dsc, dsh = _norm_mod_bwd(h_ref[...], nw_ref[...], sc_ref[...], du)
        dho_ref[...] = dhi_ref[...] + dx

        @pl.when(pl.program_id(0) == 0)
        def _():
            dnw_ref[...] = jnp.zeros_like(dnw_ref)
            dsc_ref[...] = jnp.zeros_like(dsc_ref)
            dsh_ref[...] = jnp.zeros_like(dsh_ref)

        dnw_ref[...] += dnw
        dsc_ref[...] += dsc
        dsh_ref[...] += dsh

    dxs = [p[0] for p in pairs]
    ws = [p[1] for p in pairs]
    tile = pl.BlockSpec((ts, d), lambda i: (i, 0))
    return _launch(
        body, carry, (*dxs, *ws, h, nw, sc, dh_in), name=name, grid=(s // ts,),
        in_specs=([pl.BlockSpec((ts, x.shape[1]), lambda i: (i, 0)) for x in dxs]
                  + [_col_window(w, x.shape[1], p[2] if len(p) > 2 else 0) for p, x, w in zip(pairs, dxs, ws)]
                  + [tile, _row(d), _row(d), tile]),
        out_specs=[tile, _row(d), _row(d), _row(d)],
        out_shape=[jax.ShapeDtypeStruct((s, d), F32)] + [jax.ShapeDtypeStruct((1, d), F32)] * 3,
        sem=("arbitrary",))


def _matmul_tn(a, b, *, name, tm, tn, tk=512, carry=None):
    s, m = a.shape
    n = b.shape[1]
    nk = s // tk

    def body(a_ref, b_ref, o_ref, acc_ref):
        k = pl.program_id(2)

        @pl.when(k == 0)
        def _():
            acc_ref[...] = jnp.zeros_like(acc_ref)

        acc_ref[...] += _dot_tn(a_ref[...], b_ref[...])

        @pl.when(k == nk - 1)
        def _():
            o_ref[...] = acc_ref[...].astype(o_ref.dtype)

    return _launch(
        body, carry, (a, b), name=name, grid=(m // tm, n // tn, nk),
        in_specs=[pl.BlockSpec((tk, tm), lambda i, j, k: (k, i)),
                  pl.BlockSpec((tk, tn), lambda i, j, k: (k, j))],
        out_specs=pl.BlockSpec((tm, tn), lambda i, j, k: (i, j)),
        out_shape=jax.ShapeDtypeStruct((m, n), BF16),
        scratch_shapes=[pltpu.VMEM((tm, tn), F32)],
        sem=("parallel", "parallel", "arbitrary"))


def _in_proj_fwd(h, nw, sc, sh, w_main, w_ba, *, name, ts=512, tn=1536, carry=None):
    s, d = h.shape
    n_main = w_main.shape[1]
    n_ba = w_ba.shape[1]

    def body(h_ref, nw_ref, sc_ref, sh_ref, w_ref, wba_ref, u_ref, p_ref, ba_ref):
        @pl.when(pl.program_id(1) == 0)
        def _():
            u0 = _norm_mod(h_ref[...], nw_ref[...], sc_ref[...], sh_ref[...]).astype(BF16)
            u_ref[...] = u0
            ba_ref[...] = _dot(u0, wba_ref[...])

        p_ref[...] = _dot(u_ref[...], w_ref[...]).astype(BF16)

    return _launch(
        body, carry, (h, nw, sc, sh, w_main, w_ba), name=name, grid=(s // ts, n_main // tn),
        in_specs=[pl.BlockSpec((ts, d), lambda i, j: (i, 0)), _row(d), _row(d), _row(d),
                  pl.BlockSpec((d, tn), lambda i, j: (0, j)), _resident((d, n_ba))],
        out_specs=[pl.BlockSpec((ts, d), lambda i, j: (i, 0)),
                   pl.BlockSpec((ts, tn), lambda i, j: (i, j)),
                   pl.BlockSpec((ts, n_ba), lambda i, j: (i, 0))],
        out_shape=[jax.ShapeDtypeStruct((s, d), BF16), jax.ShapeDtypeStruct((s, n_main), BF16),
                   jax.ShapeDtypeStruct((s, n_ba), F32)],
        sem=("parallel", "arbitrary"))


QKV_W = 3 * HEADS * HEAD_DIM
Z_OFF, Z_W = 3072, 1024
GATE_OFF, GATE_W = 4096, 2048
A_OFF, A_W = 6144, 1536
N_MAIN = 7680
CONV_A = 512
BA_W = 128

L_BETA, L_G, L_EG, L_EKD, L_EGC = 0, 8, 16, 24, 32


def _softplus(z):
    e = jnp.exp(-jnp.abs(z))
    small = e * (1.0 - e * (0.5 - e * (1.0 / 3.0)))
    return jnp.maximum(z, 0.0) + jnp.where(e < 1e-3, small, jnp.log(1.0 + e))


def _tri(n, sgn, strict=False):
    i = lax.broadcasted_iota(jnp.int32, (n, n), 0)
    j = lax.broadcasted_iota(jnp.int32, (n, n), 1)
    dlt = (i - j) * sgn
    return (dlt > 0) if strict else (dlt >= 0)


def _scal_fwd(ba, alog, dtb, *, name, ts=512):
    s = ba.shape[0]

    def body(ba_ref, al_ref, dt_ref, o_ref):
        d = pl.program_id(0)
        sgn = 1 - 2 * d
        x = ba_ref[...]
        lane = lax.broadcasted_iota(jnp.int32, x.shape, 1)
        beta = _sigmoid(x)
        g = -jnp.exp(al_ref[0]) * _softplus(x + dt_ref[0])
        g = jnp.where((lane >= L_G) & (lane < L_EGC + 8), g, 0.0)
        ltri = jnp.where(_tri(CHUNK, sgn), 1.0, 0.0).astype(F32)
        for c in range(ts // CHUNK):
            rows = slice(c * CHUNK, (c + 1) * CHUNK)
            gc = _dot_exact(ltri, g[rows])
            g_end = jnp.where(d == 0, gc[CHUNK - 1:CHUNK], gc[0:1])
            ln = lane[rows]
            out = jnp.where(ln < L_G, beta[rows],
                  jnp.where(ln < L_EG, gc,
                  jnp.where(ln < L_EKD, jnp.exp(gc),
                  jnp.where(ln < L_EGC, jnp.exp(g_end - gc),
                  jnp.where(ln < L_EGC + 8, jnp.broadcast_to(jnp.exp(g_end), gc.shape), 0.0)))))
            o_ref[0, rows, :] = out

    return pl.pallas_call(
        body, name=name, grid=(2, s // ts),
        in_specs=[pl.BlockSpec((ts, BA_W), lambda d, i: (i, d)),
                  pl.BlockSpec((1, 1, BA_W), lambda d, i: (d, 0, 0)),
                  pl.BlockSpec((1, 1, BA_W), lambda d, i: (d, 0, 0))],
        out_specs=pl.BlockSpec((1, ts, BA_W), lambda d, i: (d, i, 0)),
        out_shape=jax.ShapeDtypeStruct((2, s, BA_W), F32),
        compiler_params=_params(("parallel", "parallel")),
    )(ba, alog, dtb)


def _scal_bwd(dscal, drow, ba, alog, dtb, *, name, ts=512):
    s = ba.shape[0]

    def body(ds_ref, dr_ref, ba_ref, al_ref, dt_ref, dba_ref, dal_ref, ddt_ref):
        d = pl.program_id(0)
        sgn = 1 - 2 * d
        x = ba_ref[...]
        lane = lax.broadcasted_iota(jnp.int32, x.shape, 1)
        in_g = (lane >= L_G) & (lane < L_G + 8)
        beta = _sigmoid(x)
        z = x + dt_ref[0]
        neg_a = -jnp.exp(al_ref[0])
        g = neg_a * _softplus(z)
        dsv = ds_ref[0]
        dgc = jnp.where(in_g, dsv + dr_ref[0], 0.0)
        utri = jnp.where(_tri(CHUNK, -sgn), 1.0, 0.0).astype(F32)
        dal = jnp.zeros((1, BA_W), F32)
        ddt = jnp.zeros((1, BA_W), F32)
        for c in range(ts // CHUNK):
            rows = slice(c * CHUNK, (c + 1) * CHUNK)
            dg = _dot_exact(utri, dgc[rows])
            dz = dg * neg_a * _sigmoid(z[rows])
            dal = dal + jnp.sum(dg * g[rows], axis=0, keepdims=True)
            ddt = ddt + jnp.sum(dz, axis=0, keepdims=True)
            b = beta[rows]
            out = jnp.where(lane[rows] < L_G, dsv[rows] * b * (1.0 - b), jnp.where(in_g[rows], dz, 0.0))
            dba_ref[rows, :] = out.astype(BF16)

        @pl.when(pl.program_id(1) == 0)
        def _():
            dal_ref[...] = jnp.zeros_like(dal_ref)
            ddt_ref[...] = jnp.zeros_like(ddt_ref)

        dal_ref[0] += dal
        ddt_ref[0] += ddt

    row3 = pl.BlockSpec((1, 1, BA_W), lambda d, i: (d, 0, 0))
    tok3 = pl.BlockSpec((1, ts, BA_W), lambda d, i: (d, i, 0))
    return pl.pallas_call(
        body, name=name, grid=(2, s // ts),
        in_specs=[tok3, tok3, pl.BlockSpec((ts, BA_W), lambda d, i: (i, d)), row3, row3],
        out_specs=[pl.BlockSpec((ts, BA_W), lambda d, i: (i, d)), row3, row3],
        out_shape=[jax.ShapeDtypeStruct((s, 2 * BA_W), BF16), jax.ShapeDtypeStruct((2, 1, BA_W), F32),
                   jax.ShapeDtypeStruct((2, 1, BA_W), F32)],
        compiler_params=_params(("arbitrary", "arbitrary")),
    )(dscal, drow, ba, alog, dtb)


HALO = 16


def _halo_specs(ts, width, col_block, n_rows, rows=HALO):
    r = ts // rows
    last = n_rows // rows - 1
    return [pl.BlockSpec((rows, width), lambda i: (jnp.maximum(i * r - 1, 0), col_block)),
            pl.BlockSpec((ts, width), lambda i: (i, col_block)),
            pl.BlockSpec((rows, width), lambda i: (jnp.minimum((i + 1) * r, last), col_block))]


def _fill_halo(dst_ref, prev_ref, cur_ref, next_ref, first, last, fn=lambda r: r[...].astype(F32)):
    h = prev_ref.shape[0]
    ts = cur_ref.shape[0]
    p = fn(prev_ref)
    n = fn(next_ref)
    dst_ref[0:h, :] = jnp.where(first, 0.0, p)
    dst_ref[h:h + ts, :] = fn(cur_ref)
    dst_ref[h + ts:h + ts + h, :] = jnp.where(last, 0.0, n)


def _dwconv_rows(src_ref, w, start, n_rows, cols):
    acc = w[0:1, :] * src_ref[start:start + n_rows, cols]
    for i in range(1, w.shape[0]):
        acc = acc + w[i:i + 1, :] * src_ref[start + i:start + i + n_rows, cols]
    return acc


def _l2norm_heads(act, scale):
    outs = []
    for hd in range(HEADS):
        seg = act[:, hd * HEAD_DIM:(hd + 1) * HEAD_DIM]
        outs.append(seg * (lax.rsqrt(jnp.sum(seg * seg, axis=-1, keepdims=True) + EPS) * scale))
    return jnp.concatenate(outs, axis=-1)


Q_SCALE = HEAD_DIM ** -0.5


def _conv_fwd(proj, conv_dn, conv_a, *, name, ts=256):
    s = proj.shape[0]
    hd = HEADS * HEAD_DIM
    nt = s // ts

    def body(qp_ref, qc_ref, qn_ref, ap_ref, ac_ref, an_ref, wdn_ref, wa_ref,
             q_ref, k_ref, v_ref, ya_ref, xs_ref, xa_ref):
        i = pl.program_id(0)
        first, last = i == 0, i == nt - 1
        _fill_halo(xs_ref, qp_ref, qc_ref, qn_ref, first, last)
        wdn = wdn_ref[...]
        for part, o_ref in enumerate((q_ref, k_ref, v_ref)):
            cols = slice(part * hd, (part + 1) * hd)
            pre = _dwconv_rows(xs_ref, wdn[:, cols], HALO - 2, ts, cols)
            act = pre * _sigmoid(pre)
            if part == 0:
                act = _l2norm_heads(act, Q_SCALE)
            elif part == 1:
                act = _l2norm_heads(act, 1.0)
            o_ref[...] = act
        cv = lambda r: r[:, CONV_A:2 * CONV_A].astype(F32) * r[:, 2 * CONV_A:].astype(F32)
        _fill_halo(xa_ref, ap_ref, ac_ref, an_ref, first, last, fn=cv)
        conv = _dwconv_rows(xa_ref, wa_ref[...], HALO - 1, ts, slice(0, CONV_A))
        ya_ref[...] = (ac_ref[:, 0:CONV_A].astype(F32) * conv).astype(BF16)

    tile = lambda w: pl.BlockSpec((ts, w), lambda i: (i, 0))
    return pl.pallas_call(
        body, name=name, grid=(nt,),
        in_specs=(_halo_specs(ts, QKV_W, 0, s) + _halo_specs(ts, A_W, A_OFF // A_W, s)
                  + [_resident(conv_dn.shape), _resident(conv_a.shape)]),
        out_specs=[tile(hd), tile(hd), tile(hd), tile(CONV_A)],
        out_shape=[jax.ShapeDtypeStruct((s, hd), F32)] * 3 + [jax.ShapeDtypeStruct((s, CONV_A), BF16)],
        scratch_shapes=[pltpu.VMEM((ts + 2 * HALO, QKV_W), F32), pltpu.VMEM((ts + 2 * HALO, CONV_A), F32)],
        compiler_params=_params(("parallel",)),
    )(proj, proj, proj, proj, proj, proj, conv_dn, conv_a)


def _chunk_of_step(d, c, n):
    return c + d * (n - 1 - 2 * c)


def _head_scalars(scv, grv, hd):
    col = lambda base: scv[:, base + hd:base + hd + 1]
    return (col(L_BETA), col(L_G), col(L_EG), col(L_EKD),
            scv[0:1, L_EGC + hd:L_EGC + hd + 1], grv[hd:hd + 1, :])


def _decay_matrix(gcol, grow, incl):
    return jnp.where(incl, jnp.exp(jnp.minimum(gcol - grow, 0.0)), 0.0)


INV_BASE = 8


def _unit_lower_inverse(a_m, top=None):
    n = a_m.shape[0]
    top = top or n
    i = lax.broadcasted_iota(jnp.int32, (n, n), 0)
    j = lax.broadcasted_iota(jnp.int32, (n, n), 1)

    def same_block(m):
        sh = int(math.log2(m))
        return jnp.right_shift(i, sh) == jnp.right_shift(j, sh)

    x = jnp.where(same_block(INV_BASE), -a_m, 0.0)
    t = jnp.where(i == j, 1.0, 0.0) + x
    p = x
    for _ in range(int(math.log2(INV_BASE)) - 1):
        p_b = p.astype(BF16)
        p = _dot(p_b, p_b)
        t = t + _dot(t.astype(BF16), p.astype(BF16))
    m = INV_BASE
    while m < top:
        join = jnp.where(same_block(2 * m) & jnp.logical_not(same_block(m)), a_m, 0.0)
        t_b = t.astype(BF16)
        t = t - _dot(_dot(t_b, join.astype(BF16)).astype(BF16), t_b)
        m *= 2
    return t


def _delta_fwd_per_head(q, k, v, scal, grow, *, name):
    s = q.shape[0]
    n = s // CHUNK
    hd_all = HEADS * HEAD_DIM

    def body(q_ref, k_ref, v_ref, sc_ref, gr_ref, o_ref, st_ref, t_ref, vn_ref, state):
        d = pl.program_id(0)
        sgn = 1 - 2 * d

        @pl.when(pl.program_id(1) == 0)
        def _():
            state[...] = jnp.zeros_like(state)

        incl = _tri(CHUNK, sgn)
        strict = _tri(CHUNK, sgn, strict=True)
        scv = sc_ref[0]
        grv = gr_ref[0, 0]
        for hd in range(HEADS):
            cols = slice(hd * HEAD_DIM, (hd + 1) * HEAD_DIM)
            qh, kh, vh = q_ref[:, cols], k_ref[:, cols], v_ref[:, cols]
            beta, gcol, eg, ekd, egc, grow_h = _head_scalars(scv, grv, hd)
            dm = _decay_matrix(gcol, grow_h, incl)
            k_b = kh.astype(BF16)
            kk = _dot_nt((kh * beta).astype(BF16), k_b)
            t = _unit_lower_inverse(jnp.where(strict, kk * dm, 0.0))
            p_m = jnp.where(incl, _dot_nt(qh.astype(BF16), k_b) * dm, 0.0)
            sh = state[hd]
            sh_b = sh.astype(BF16)
            st_ref[0, 0, hd] = sh_b
            r = vh - _dot((kh * eg).astype(BF16), sh_b)
            vn = _dot3(t, beta * r)
            vn_b = vn.astype(BF16)
            o_ref[0, :, cols] = _dot((qh * eg).astype(BF16), sh_b) + _dot(p_m.astype(BF16), vn_b)
            state[hd] = egc * sh + _dot_tn((kh * ekd).astype(BF16), vn_b)
            t_ref[0, 0, hd] = t
            vn_ref[0, :, cols] = vn_b

    tok = lambda d, c: (_chunk_of_step(d, c, n), 0)
    dtok = lambda d, c: (d, _chunk_of_step(d, c, n), 0)
    dchunk4 = lambda d, c: (d, _chunk_of_step(d, c, n), 0, 0)
    dchunk5 = lambda d, c: (d, _chunk_of_step(d, c, n), 0, 0, 0)
    return pl.pallas_call(
        body, name=name, grid=(2, n),
        in_specs=[pl.BlockSpec((CHUNK, hd_all), tok)] * 3
                 + [pl.BlockSpec((1, CHUNK, BA_W), dtok), pl.BlockSpec((1, 1, HEADS, CHUNK), dchunk4)],
        out_specs=[pl.BlockSpec((1, CHUNK, hd_all), dtok),
                   pl.BlockSpec((1, 1, HEADS, HEAD_DIM, HEAD_DIM), dchunk5),
                   pl.BlockSpec((1, 1, HEADS, CHUNK, CHUNK), dchunk5),
                   pl.BlockSpec((1, CHUNK, hd_all), dtok)],
        out_shape=[jax.ShapeDtypeStruct((2, s, hd_all), F32),
                   jax.ShapeDtypeStruct((2, n, HEADS, HEAD_DIM, HEAD_DIM), BF16),
                   jax.ShapeDtypeStruct((2, n, HEADS, CHUNK, CHUNK), F32),
                   jax.ShapeDtypeStruct((2, s, hd_all), BF16)],
        scratch_shapes=[pltpu.VMEM((HEADS, HEAD_DIM, HEAD_DIM), F32)],
        compiler_params=_params(("arbitrary", "arbitrary")),
    )(q, k, v, scal, grow)


def _delta_bwd_per_head(q, k, v, scal, grow, states, tinv, vn, do, *, name):
    s = q.shape[0]
    n = s // CHUNK
    hd_all = HEADS * HEAD_DIM

    def body(q_ref, k_ref, v_ref, sc_ref, gr_ref, st_ref, t_ref, vn_ref, do_ref,
             dq_ref, dk_ref, dv_ref, dsc_ref, dgr_ref, dstate):
        d = pl.program_id(0)
        sgn = 1 - 2 * d

        @pl.when(pl.program_id(1) == 0)
        def _():
            dstate[...] = jnp.zeros_like(dstate)

        incl = _tri(CHUNK, sgn)
        strict = _tri(CHUNK, sgn, strict=True)
        scv = sc_ref[0]
        grv = gr_ref[0, 0]
        lane = lax.broadcasted_iota(jnp.int32, (CHUNK, BA_W), 1)
        row = lax.broadcasted_iota(jnp.int32, (CHUNK, 1), 0)
        sub = lax.broadcasted_iota(jnp.int32, (HEADS, CHUNK), 0)
        end_row = jnp.where(d == 0, CHUNK - 1, 0)
        dsc_acc = jnp.zeros((CHUNK, BA_W), F32)
        dgr_acc = jnp.zeros((HEADS, CHUNK), F32)
        for hd in range(HEADS):
            cols = slice(hd * HEAD_DIM, (hd + 1) * HEAD_DIM)
            qh, kh, vh = q_ref[:, cols], k_ref[:, cols], v_ref[:, cols]
            beta, gcol, eg, ekd, egc, grow_h = _head_scalars(scv, grv, hd)
            dm = _decay_matrix(gcol, grow_h, incl)
            q_b, k_b = qh.astype(BF16), kh.astype(BF16)
            kb_b = (kh * beta).astype(BF16)
            kk = _dot_nt(kb_b, k_b)
            qk = _dot_nt(q_b, k_b)
            p_m = jnp.where(incl, qk * dm, 0.0)
            t = t_ref[0, 0, hd]
            vn_b = vn_ref[0, :, cols]
            sh_b = st_ref[0, 0, hd]
            dsp = dstate[hd]
            dsp_b = dsp.astype(BF16)
            do_b = do_ref[:, cols].astype(BF16)
            kg, qg, kd = kh * eg, qh * eg, kh * ekd
            kg_b, qg_b, kd_b = kg.astype(BF16), qg.astype(BF16), kd.astype(BF16)
            r = vh - _dot(kg_b, sh_b)
            dvn = _dot_tn(p_m.astype(BF16), do_b) + _dot(kd_b, dsp_b)
            db = _dot3(t, dvn, dot=_dot_tn)
            dr = db * beta
            dbeta = jnp.sum(db * r, axis=-1, keepdims=True)
            dr_b, db_b = dr.astype(BF16), db.astype(BF16)
            dkg = -_dot_nt(dr_b, sh_b)
            dqg = _dot_nt(do_b, sh_b)
            dkd = _dot_nt(vn_b, dsp_b)
            dpm = jnp.where(incl, _dot_nt(do_b, vn_b), 0.0) * dm
            dam = jnp.where(strict, -_dot_nt(db_b, vn_b), 0.0) * dm
            dpm_b, dam_b = dpm.astype(BF16), dam.astype(BF16)
            dkb = _dot(dam_b, k_b)
            dq_ref[0, :, cols] = dqg * eg + _dot(dpm_b, k_b)
            dk_ref[0, :, cols] = (dkg * eg + dkd * ekd + _dot_tn(dpm_b, q_b) + _dot_tn(dam_b, kb_b)
                                  + dkb * beta)
            dv_ref[0, :, cols] = dr
            dbeta = dbeta + jnp.sum(dkb * kh, axis=-1, keepdims=True)
            m = dpm * qk + dam * kk
            kd_term = jnp.sum(dkd * kd, axis=-1, keepdims=True)
            dgcol = (jnp.sum(dqg * qg, axis=-1, keepdims=True) + jnp.sum(dkg * kg, axis=-1, keepdims=True)
                     - kd_term + jnp.sum(m, axis=-1, keepdims=True))
            dg_end = jnp.sum(kd_term) + egc * jnp.sum(dsp * sh_b.astype(F32))
            dgcol = dgcol + jnp.where(row == end_row, dg_end, 0.0)
            dsc_acc = jnp.where(lane == L_BETA + hd, dbeta, dsc_acc)
            dsc_acc = jnp.where(lane == L_G + hd, dgcol, dsc_acc)
            dgr_acc = jnp.where(sub == hd, -jnp.sum(m, axis=0, keepdims=True), dgr_acc)
            dstate[hd] = _dot_tn(qg_b, do_b) + egc * dsp - _dot_tn(kg_b, dr_b)
        dsc_ref[0] = dsc_acc
        dgr_ref[0, 0] = dgr_acc

    step = lambda d, c: n - 1 - _chunk_of_step(d, c, n)
    tok = lambda d, c: (step(d, c), 0)
    dtok = lambda d, c: (d, step(d, c), 0)
    dchunk4 = lambda d, c: (d, step(d, c), 0, 0)
    dchunk5 = lambda d, c: (d, step(d, c), 0, 0, 0)
    tok_spec = pl.BlockSpec((CHUNK, hd_all), tok)
    dtok_spec = pl.BlockSpec((1, CHUNK, hd_all), dtok)
    return pl.pallas_call(
        body, name=name, grid=(2, n),
        in_specs=[tok_spec] * 3
                 + [pl.BlockSpec((1, CHUNK, BA_W), dtok), pl.BlockSpec((1, 1, HEADS, CHUNK), dchunk4),
                    pl.BlockSpec((1, 1, HEADS, HEAD_DIM, HEAD_DIM), dchunk5),
                    pl.BlockSpec((1, 1, HEADS, CHUNK, CHUNK), dchunk5), dtok_spec, tok_spec],
        out_specs=[dtok_spec] * 3
                  + [pl.BlockSpec((1, CHUNK, BA_W), dtok), pl.BlockSpec((1, 1, HEADS, CHUNK), dchunk4)],
        out_shape=[jax.ShapeDtypeStruct((2, s, hd_all), F32)] * 3
                  + [jax.ShapeDtypeStruct((2, s, BA_W), F32), jax.ShapeDtypeStruct((2, n, HEADS, CHUNK), F32)],
        scratch_shapes=[pltpu.VMEM((HEADS, HEAD_DIM, HEAD_DIM), F32)],
        compiler_params=_params(("arbitrary", "arbitrary")),
    )(q, k, v, scal, grow, states, tinv, vn, do)


GROUP = 4
GROWS = GROUP * CHUNK
N_GROUPS = HEADS // GROUP


def _stack(parts):
    return jnp.concatenate(parts, axis=0)


M_INCL, M_STRICT, M_EYE, M_BASE, M_JOIN = 0, 1, 2, 3, 4
JOIN_SIZES = (16, 32, 64)
N_MASKS = M_JOIN + len(JOIN_SIZES)


def _write_group_masks(mask_ref, sgn, n_masks):
    i = lax.broadcasted_iota(jnp.int32, (GROWS, GROWS), 0)
    j = lax.broadcasted_iota(jnp.int32, (GROWS, GROWS), 1)
    same = lambda m: jnp.right_shift(i, int(math.log2(m))) == jnp.right_shift(j, int(math.log2(m)))
    dlt = (i - j) * sgn
    one = lambda cond: jnp.where(cond, 1.0, 0.0).astype(F32)
    mask_ref[M_INCL] = one(same(CHUNK) & (dlt >= 0))
    mask_ref[M_STRICT] = one(same(CHUNK) & (dlt > 0))
    if n_masks > M_EYE:
        mask_ref[M_EYE] = one(i == j)
        mask_ref[M_BASE] = one(same(INV_BASE))
        for lvl, m in enumerate(JOIN_SIZES):
            mask_ref[M_JOIN + lvl] = one(same(m) & jnp.logical_not(same(m // 2)))


def _group_decay(gcol, grow, mask_ref):
    return jnp.exp(jnp.minimum(gcol - grow, 0.0)) * mask_ref[M_INCL]


def _block_inverse(a_m, mask_ref):
    x = -(a_m * mask_ref[M_BASE])
    t = mask_ref[M_EYE] + x
    p = x
    for _ in range(int(math.log2(INV_BASE)) - 1):
        p_b = p.astype(BF16)
        p = _dot(p_b, p_b)
        t = t + _dot(t.astype(BF16), p.astype(BF16))
    for lvl in range(len(JOIN_SIZES)):
        t_b = t.astype(BF16)
        t = t - _dot(_dot(t_b, (a_m * mask_ref[M_JOIN + lvl]).astype(BF16)).astype(BF16), t_b)
    return t


def _group_operands(q_ref, k_ref, v_ref, scv, grp):
    heads = [GROUP * grp + t for t in range(GROUP)]
    tiles = lambda ref: [ref[:, h * HEAD_DIM:(h + 1) * HEAD_DIM] for h in heads]
    col = lambda base: [scv[:, base + h:base + h + 1] for h in heads]
    egc = [scv[0:1, L_EGC + h:L_EGC + h + 1] for h in heads]
    return heads, tiles(q_ref), tiles(k_ref), tiles(v_ref), col(L_BETA), col(L_G), col(L_EG), col(L_EKD), egc


def _delta_fwd(q, k, v, scal, grow, *, name):
    s = q.shape[0]
    n = s // CHUNK
    hd_all = HEADS * HEAD_DIM

    def one_direction(q_ref, k_ref, v_ref, sc_ref, gr_ref, o_ref, st_ref, t_ref, vn_ref, state, mask_ref):
        scv = sc_ref[0]
        for grp in range(N_GROUPS):
            heads, qs, ks, vs, beta, gcol, eg, ekd, egc = _group_operands(q_ref, k_ref, v_ref, scv, grp)
            dm = _group_decay(_stack(gcol), gr_ref[0, 0, grp:grp + 1, :], mask_ref)
            k_b = _stack(ks).astype(BF16)
            kk = _dot_nt(_stack([ks[t] * beta[t] for t in range(GROUP)]).astype(BF16), k_b)
            tinv = _block_inverse(kk * dm * mask_ref[M_STRICT], mask_ref).astype(BF16)
            t_ref[0, grp] = tinv
            p_m = _dot_nt(_stack(qs).astype(BF16), k_b) * dm
            sh, sh_b, br = [], [], []
            for t, h in enumerate(heads):
                sh.append(state[h])
                sh_b.append(sh[t].astype(BF16))
                st_ref[0, h] = sh_b[t]
                br.append(beta[t] * (vs[t] - _dot((ks[t] * eg[t]).astype(BF16), sh_b[t])))
            vn_b = _dot(tinv, _stack(br).astype(BF16)).astype(BF16)
            o_intra = _dot(p_m.astype(BF16), vn_b)
            for t, h in enumerate(heads):
                rows = slice(t * CHUNK, (t + 1) * CHUNK)
                cols = slice(h * HEAD_DIM, (h + 1) * HEAD_DIM)
                o_ref[:, cols] = _dot((qs[t] * eg[t]).astype(BF16), sh_b[t]) + o_intra[rows]
                state[h] = egc[t] * sh[t] + _dot_tn((ks[t] * ekd[t]).astype(BF16), vn_b[rows])
                vn_ref[:, cols] = vn_b[rows]

    def body(*refs):
        ins, outs, (state, mask_ref) = refs[:10], refs[10:18], refs[18:]

        @pl.when(pl.program_id(0) == 0)
        def _():
            state[...] = jnp.zeros_like(state)
            for d in range(2):
                _write_group_masks(mask_ref.at[d], 1 - 2 * d, N_MASKS)

        for d in range(2):
            one_direction(*ins[5 * d:5 * d + 5], *outs[4 * d:4 * d + 4], state.at[d], mask_ref.at[d])

    at = [lambda c: c, lambda c: n - 1 - c]
    in_specs, out_specs = [], []
    for d in range(2):
        tok = pl.BlockSpec((CHUNK, hd_all), lambda c, d=d: (at[d](c), 0))
        in_specs += [tok] * 3 + [pl.BlockSpec((1, CHUNK, BA_W), lambda c, d=d: (d, at[d](c), 0)),
                                 pl.BlockSpec((1, 1, N_GROUPS, GROWS), lambda c, d=d: (d, at[d](c), 0, 0))]
        out_specs += [tok, pl.BlockSpec((1, HEADS, HEAD_DIM, HEAD_DIM), lambda c, d=d: (at[d](c), 0, 0, 0)),
                      pl.BlockSpec((1, N_GROUPS, GROWS, GROWS), lambda c, d=d: (at[d](c), 0, 0, 0)), tok]
    per_dir_shape = [jax.ShapeDtypeStruct((s, hd_all), F32),
                     jax.ShapeDtypeStruct((n, HEADS, HEAD_DIM, HEAD_DIM), BF16),
                     jax.ShapeDtypeStruct((n, N_GROUPS, GROWS, GROWS), BF16),
                     jax.ShapeDtypeStruct((s, hd_all), BF16)]
    outs = pl.pallas_call(
        body, name=name, grid=(n,), in_specs=in_specs, out_specs=out_specs, out_shape=per_dir_shape * 2,
        scratch_shapes=[pltpu.VMEM((2, HEADS, HEAD_DIM, HEAD_DIM), F32),
                        pltpu.VMEM((2, N_MASKS, GROWS, GROWS), F32)],
        compiler_params=_params(("arbitrary",)),
    )(*([q, k, v, scal, grow] * 2))
    return tuple((outs[i], outs[4 + i]) for i in range(4))


def _delta_bwd(q, k, v, scal, grow, states, tinv, vn, do, *, name, carry=None):
    s = q.shape[0]
    n = s // CHUNK
    hd_all = HEADS * HEAD_DIM

    def one_direction(d, q_ref, k_ref, v_ref, sc_ref, gr_ref, st_ref, t_ref, vn_ref, do_ref,
                      dq_ref, dk_ref, dv_ref, dsc_ref, dgr_ref, dstate, mask_ref):
        scv = sc_ref[0]
        lane = lax.broadcasted_iota(jnp.int32, (CHUNK, BA_W), 1)
        row = lax.broadcasted_iota(jnp.int32, (CHUNK, 1), 0)
        end_row = CHUNK - 1 if d == 0 else 0
        dsc_acc = jnp.zeros((CHUNK, BA_W), F32)
        for grp in range(N_GROUPS):
            heads, qs, ks, vs, beta, gcol, eg, ekd, egc = _group_operands(q_ref, k_ref, v_ref, scv, grp)
            dm = _group_decay(_stack(gcol), gr_ref[0, 0, grp:grp + 1, :], mask_ref)
            dm_strict = dm * mask_ref[M_STRICT]
            beta_st, eg_st, ekd_st = _stack(beta), _stack(eg), _stack(ekd)
            q_st, k_st = _stack(qs), _stack(ks)
            q_b, k_b = q_st.astype(BF16), k_st.astype(BF16)
            kb_b = (k_st * beta_st).astype(BF16)
            kk = _dot_nt(kb_b, k_b)
            qk = _dot_nt(q_b, k_b)
            p_m = qk * dm
            kg_st, qg_st, kd_st = k_st * eg_st, q_st * eg_st, k_st * ekd_st
            kg_b, qg_b, kd_b = kg_st.astype(BF16), qg_st.astype(BF16), kd_st.astype(BF16)
            tok_cols = [slice(h * HEAD_DIM, (h + 1) * HEAD_DIM) for h in heads]
            grp_rows = [slice(t * CHUNK, (t + 1) * CHUNK) for t in range(GROUP)]
            vn_b = _stack([vn_ref[:, c] for c in tok_cols])
            do_b = _stack([do_ref[:, c] for c in tok_cols]).astype(BF16)
            sh_b = [st_ref[0, h] for h in heads]
            dsp = [dstate[h] for h in heads]
            dsp_b = [x.astype(BF16) for x in dsp]
            r_st = _stack([vs[t] - _dot(kg_b[grp_rows[t]], sh_b[t]) for t in range(GROUP)])
            dvn = _dot_tn(p_m.astype(BF16), do_b) + _stack(
                [_dot(kd_b[grp_rows[t]], dsp_b[t]) for t in range(GROUP)])
            db = _dot_tn(t_ref[0, grp], dvn.astype(BF16))
            dr = db * beta_st
            dbeta = jnp.sum(db * r_st, axis=-1, keepdims=True)
            dr_b, db_b = dr.astype(BF16), db.astype(BF16)
            dkg = -_stack([_dot_nt(dr_b[grp_rows[t]], sh_b[t]) for t in range(GROUP)])
            dqg = _stack([_dot_nt(do_b[grp_rows[t]], sh_b[t]) for t in range(GROUP)])
            dkd = _stack([_dot_nt(vn_b[grp_rows[t]], dsp_b[t]) for t in range(GROUP)])
            dpm = _dot_nt(do_b, vn_b) * dm
            dam = -_dot_nt(db_b, vn_b) * dm_strict
            dpm_b, dam_b = dpm.astype(BF16), dam.astype(BF16)
            dkb = _dot(dam_b, k_b)
            dq_st = dqg * eg_st + _dot(dpm_b, k_b)
            dk_st = (dkg * eg_st + dkd * ekd_st + _dot_tn(dpm_b, q_b) + _dot_tn(dam_b, kb_b) + dkb * beta_st)
            dbeta = dbeta + jnp.sum(dkb * k_st, axis=-1, keepdims=True)
            m = dpm * qk + dam * kk
            kd_term = jnp.sum(dkd * kd_st, axis=-1, keepdims=True)
            dgcol = (jnp.sum(dqg * qg_st, axis=-1, keepdims=True) + jnp.sum(dkg * kg_st, axis=-1, keepdims=True)
                     - kd_term + jnp.sum(m, axis=-1, keepdims=True))
            dgr_ref[0, grp:grp + 1, :] = -jnp.sum(m, axis=0, keepdims=True)
            for t, h in enumerate(heads):
                rows, cols = grp_rows[t], tok_cols[t]
                dq_ref[:, cols] = dq_st[rows]
                dk_ref[:, cols] = dk_st[rows]
                dv_ref[:, cols] = dr[rows]
                dg_end = jnp.sum(kd_term[rows]) + egc[t] * jnp.sum(dsp[t] * sh_b[t].astype(F32))
                dgcol_h = dgcol[rows] + jnp.where(row == end_row, dg_end, 0.0)
                dsc_acc = jnp.where(lane == L_BETA + h, dbeta[rows], dsc_acc)
                dsc_acc = jnp.where(lane == L_G + h, dgcol_h, dsc_acc)
                dstate[h] = (_dot_tn(qg_b[rows], do_b[rows]) + egc[t] * dsp[t]
                             - _dot_tn(kg_b[rows], dr_b[rows]))
        dsc_ref[...] = dsc_acc

    def body(*refs):
        ins, outs, (dstate, mask_ref) = refs[:18], refs[18:28], refs[28:]

        @pl.when(pl.program_id(0) == 0)
        def _():
            dstate[...] = jnp.zeros_like(dstate)
            for d in range(2):
                _write_group_masks(mask_ref.at[d], 1 - 2 * d, M_EYE)

        for d in range(2):
            one_direction(d, *ins[9 * d:9 * d + 9], *outs[5 * d:5 * d + 5], dstate.at[d], mask_ref.at[d])

    at = [lambda c: n - 1 - c, lambda c: c]
    in_specs, out_specs, args = [], [], []
    for d in range(2):
        tok = pl.BlockSpec((CHUNK, hd_all), lambda c, d=d: (at[d](c), 0))
        in_specs += [tok] * 3 + [pl.BlockSpec((1, CHUNK, BA_W), lambda c, d=d: (d, at[d](c), 0)),
                                 pl.BlockSpec((1, 1, N_GROUPS, GROWS), lambda c, d=d: (d, at[d](c), 0, 0)),
                                 pl.BlockSpec((1, HEADS, HEAD_DIM, HEAD_DIM), lambda c, d=d: (at[d](c), 0, 0, 0)),
                                 pl.BlockSpec((1, N_GROUPS, GROWS, GROWS), lambda c, d=d: (at[d](c), 0, 0, 0)),
                                 tok, tok]
        args += [q, k, v, scal, grow, states[d], tinv[d], vn[d], do]
        out_specs += [tok] * 3 + [pl.BlockSpec((CHUNK, BA_W), lambda c, d=d: (at[d](c), 0)),
                                  pl.BlockSpec((1, N_GROUPS, GROWS), lambda c, d=d: (at[d](c), 0, 0))]
    per_dir_shape = ([jax.ShapeDtypeStruct((s, hd_all), F32)] * 3
                     + [jax.ShapeDtypeStruct((s, BA_W), F32), jax.ShapeDtypeStruct((n, N_GROUPS, GROWS), F32)])
    res = _launch(
        body, carry, tuple(args), name=name, grid=(n,), in_specs=in_specs, out_specs=out_specs,
        out_shape=per_dir_shape * 2,
        scratch_shapes=[pltpu.VMEM((2, HEADS, HEAD_DIM, HEAD_DIM), F32), pltpu.VMEM((2, M_EYE, GROWS, GROWS), F32)],
        sem=("arbitrary",))
    outs, got = res if carry is not None else (res, None)
    paired = tuple((outs[i], outs[5 + i]) for i in range(5))
    return paired if carry is None else (paired, got)


def _gate_norm_fwd(o2, proj, dnw, *, name, ts=512):
    s = o2[0].shape[0]
    hd_all = HEADS * HEAD_DIM

    def body(of_ref, ob_ref, z_ref, w_ref, y_ref):
        w = w_ref[...]
        for hd in range(HEADS):
            cols = slice(hd * HEAD_DIM, (hd + 1) * HEAD_DIM)
            seg = of_ref[:, cols] + ob_ref[:, cols]
            r = lax.rsqrt(jnp.mean(seg * seg, axis=-1, keepdims=True) + EPS)
            z = z_ref[:, cols].astype(F32)
            y_ref[:, cols] = ((seg * r * w) * (z * _sigmoid(z))).astype(BF16)

    tile = pl.BlockSpec((ts, hd_all), lambda i: (i, 0))
    return pl.pallas_call(
        body, name=name, grid=(s // ts,),
        in_specs=[tile, tile, pl.BlockSpec((ts, Z_W), lambda i: (i, Z_OFF // Z_W)), _row(HEAD_DIM)],
        out_specs=tile,
        out_shape=jax.ShapeDtypeStruct((s, hd_all), BF16),
        compiler_params=_params(("parallel",)),
    )(o2[0], o2[1], proj, dnw)


def _gate_norm_bwd(dyb, o2, proj, dnw, *, name, ts=512):
    s = o2[0].shape[0]
    hd_all = HEADS * HEAD_DIM

    def body(dy_ref, of_ref, ob_ref, z_ref, w_ref, do_ref, dz_ref, dw_ref):
        w = w_ref[...]
        dw = jnp.zeros((1, HEAD_DIM), F32)
        for hd in range(HEADS):
            cols = slice(hd * HEAD_DIM, (hd + 1) * HEAD_DIM)
            seg = of_ref[:, cols] + ob_ref[:, cols]
            r = lax.rsqrt(jnp.mean(seg * seg, axis=-1, keepdims=True) + EPS)
            xhat = seg * r
            z = z_ref[:, cols].astype(F32)
            sg = _sigmoid(z)
            dy = dy_ref[:, cols]
            dnrm = dy * (z * sg)
            dz_ref[:, cols] = (dy * (xhat * w) * (sg * (1.0 + z * (1.0 - sg)))).astype(BF16)
            dw = dw + jnp.sum(dnrm * xhat, axis=0, keepdims=True)
            dxhat = dnrm * w
            do_ref[:, cols] = r * (dxhat - xhat * jnp.mean(dxhat * xhat, axis=-1, keepdims=True))

        @pl.when(pl.program_id(0) == 0)
        def _():
            dw_ref[...] = jnp.zeros_like(dw_ref)

        dw_ref[...] += dw

    tile = pl.BlockSpec((ts, hd_all), lambda i: (i, 0))
    return pl.pallas_call(
        body, name=name, grid=(s // ts,),
        in_specs=[tile, tile, tile, pl.BlockSpec((ts, Z_W), lambda i: (i, Z_OFF // Z_W)), _row(HEAD_DIM)],
        out_specs=[tile, tile, _row(HEAD_DIM)],
        out_shape=[jax.ShapeDtypeStruct((s, hd_all), F32), jax.ShapeDtypeStruct((s, hd_all), BF16),
                   jax.ShapeDtypeStruct((1, HEAD_DIM), F32)],
        compiler_params=_params(("arbitrary",)),
    )(dyb, o2[0], o2[1], proj, dnw)


def _merge_fwd(ya, yb, proj, wa, wb, wo, h, g, *, name, ts=512):
    s, d = h.shape

    def body(ya_ref, yb_ref, gt_ref, wa_ref, wb_ref, wo_ref, h_ref, g_ref, pa_ref, pb_ref, mix_ref, ho_ref):
        pa = _dot(ya_ref[...], wa_ref[...])
        pb = _dot(yb_ref[...], wb_ref[...])
        pa_ref[...] = pa.astype(BF16)
        pb_ref[...] = pb.astype(BF16)
        merged = (_sigmoid(gt_ref[:, :d].astype(F32)) * pa + _sigmoid(gt_ref[:, d:].astype(F32)) * pb)
        mix = _dot(merged.astype(BF16), wo_ref[...])
        mix_ref[...] = mix.astype(BF16)
        ho_ref[...] = h_ref[...] + g_ref[...] * mix

    tile = pl.BlockSpec((ts, d), lambda i: (i, 0))
    return pl.pallas_call(
        body, name=name, grid=(s // ts,),
        in_specs=[pl.BlockSpec((ts, CONV_A), lambda i: (i, 0)), tile,
                  pl.BlockSpec((ts, GATE_W), lambda i: (i, GATE_OFF // GATE_W)),
                  _resident(wa.shape), _resident(wb.shape), _resident(wo.shape), tile, _row(d)],
        out_specs=[tile, tile, tile, tile],
        out_shape=[jax.ShapeDtypeStruct((s, d), BF16)] * 3 + [jax.ShapeDtypeStruct((s, d), F32)],
        compiler_params=_params(("parallel",)),
    )(ya, yb, proj, wa, wb, wo, h, g)


def _merge_bwd(dh, g, mix, pa, pb, proj, wa, wb, wo, *, name, ts=256):
    s, d = dh.shape

    def body(dh_ref, g_ref, mix_ref, pa_ref, pb_ref, gt_ref, wa_ref, wb_ref, wo_ref,
             dmix_ref, mg_ref, dpa_ref, dpb_ref, dgt_ref, dya_ref, dyb_ref, dg_ref):
        dh_v = dh_ref[...]
        dmix = (g_ref[...] * dh_v).astype(BF16)
        dmix_ref[...] = dmix

        @pl.when(pl.program_id(0) == 0)
        def _():
            dg_ref[...] = jnp.zeros_like(dg_ref)

        dg_ref[...] += jnp.sum(dh_v * mix_ref[...].astype(F32), axis=0, keepdims=True)
        dmerged = _dot_nt(dmix, wo_ref[...])
        pa = pa_ref[...].astype(F32)
        pb = pb_ref[...].astype(F32)
        sa = _sigmoid(gt_ref[:, :d].astype(F32))
        sb = _sigmoid(gt_ref[:, d:].astype(F32))
        mg_ref[...] = (sa * pa + sb * pb).astype(BF16)
        dpa = (dmerged * sa).astype(BF16)
        dpb = (dmerged * sb).astype(BF16)
        dpa_ref[...] = dpa
        dpb_ref[...] = dpb
        dgt_ref[:, :d] = (dmerged * pa * sa * (1.0 - sa)).astype(BF16)
        dgt_ref[:, d:] = (dmerged * pb * sb * (1.0 - sb)).astype(BF16)
        dya_ref[...] = _dot_nt(dpa, wa_ref[...])
        dyb_ref[...] = _dot_nt(dpb, wb_ref[...])

    tile = pl.BlockSpec((ts, d), lambda i: (i, 0))
    return pl.pallas_call(
        body, name=name, grid=(s // ts,),
        in_specs=[tile, _row(d), tile, tile, tile,
                  pl.BlockSpec((ts, GATE_W), lambda i: (i, GATE_OFF // GATE_W)),
                  _resident(wa.shape), _resident(wb.shape), _resident(wo.shape)],
        out_specs=[tile, tile, tile, tile, pl.BlockSpec((ts, GATE_W), lambda i: (i, 0)),
                   pl.BlockSpec((ts, CONV_A), lambda i: (i, 0)), tile, _row(d)],
        out_shape=[jax.ShapeDtypeStruct((s, d), BF16)] * 4
                  + [jax.ShapeDtypeStruct((s, GATE_W), BF16), jax.ShapeDtypeStruct((s, CONV_A), F32),
                     jax.ShapeDtypeStruct((s, d), F32), jax.ShapeDtypeStruct((1, d), F32)],
        compiler_params=_params(("arbitrary",)),
    )(dh, g, mix, pa, pb, proj, wa, wb, wo)


def _final_fwd_bwd(h, nw, target, *, name, ts=512):
    s, d = h.shape

    def body(h_ref, nw_ref, t_ref, loss_ref, dh_ref, dnw_ref):
        x = h_ref[...]
        w = nw_ref[...]
        r = lax.rsqrt(jnp.mean(x * x, axis=-1, keepdims=True) + EPS)
        xhat = x * r
        e = xhat * w - t_ref[...]
        part = 0.5 * jnp.sum(jnp.mean(e * e, axis=-1, keepdims=True))
        dy = e * (1.0 / d)
        dxhat = dy * w
        dh_ref[...] = r * (dxhat - xhat * jnp.mean(dxhat * xhat, axis=-1, keepdims=True))

        @pl.when(pl.program_id(0) == 0)
        def _():
            loss_ref[...] = jnp.zeros_like(loss_ref)
            dnw_ref[...] = jnp.zeros_like(dnw_ref)

        loss_ref[...] += jnp.broadcast_to(part, loss_ref.shape)
        dnw_ref[...] += jnp.sum(dy * xhat, axis=0, keepdims=True)

    tile = pl.BlockSpec((ts, d), lambda i: (i, 0))
    return pl.pallas_call(
        body, name=name, grid=(s // ts,),
        in_specs=[tile, _row(d), tile],
        out_specs=[_row(128), tile, _row(d)],
        out_shape=[jax.ShapeDtypeStruct((1, 128), F32), jax.ShapeDtypeStruct((s, d), F32),
                   jax.ShapeDtypeStruct((1, d), F32)],
        compiler_params=_params(("arbitrary",)),
    )(h, nw, target)


EXT = 8


def _l2norm_heads_bwd(act, dout, scale):
    outs = []
    for hd in range(HEADS):
        cols = slice(hd * HEAD_DIM, (hd + 1) * HEAD_DIM)
        seg = act[:, cols]
        nrm = lax.rsqrt(jnp.sum(seg * seg, axis=-1, keepdims=True) + EPS)
        yhat = seg * nrm
        dsg = dout[:, cols]
        outs.append((scale * nrm) * (dsg - yhat * jnp.sum(yhat * dsg, axis=-1, keepdims=True)))
    return jnp.concatenate(outs, axis=-1)


def _conv_bwd(dq2, dk2, dv2, dya, proj, conv_dn, conv_a, *, name, ts=256, carry=None):
    s = proj.shape[0]
    hd = HEADS * HEAD_DIM
    nt = s // ts
    te = ts + 2 * EXT
    kdn, ka = conv_dn.shape[0], conv_a.shape[0]

    def body(*refs):
        (qp_ref, qc_ref, qn_ref, ap_ref, ac_ref, an_ref) = refs[0:6]
        d3 = refs[6:24]
        (yp_ref, yc_ref, yn_ref, wdn_ref, wa_ref) = refs[24:29]
        (dqkv_ref, da_ref, dwdn_ref, dwa_ref) = refs[29:33]
        xs_ref, dps_ref, xa_ref, dca_ref = refs[33:37]
        i = pl.program_id(0)
        first, last = i == 0, i == nt - 1

        @pl.when(first)
        def _():
            dwdn_ref[...] = jnp.zeros_like(dwdn_ref)
            dwa_ref[...] = jnp.zeros_like(dwa_ref)

        rowe = lax.broadcasted_iota(jnp.int32, (te, 1), 0)
        inside = ~((first & (rowe < EXT)) | (last & (rowe >= EXT + ts)))
        _fill_halo(xs_ref, qp_ref, qc_ref, qn_ref, first, last)
        wdn = wdn_ref[...]
        for part in range(3):
            cols = slice(part * hd, (part + 1) * hd)
            pre = _dwconv_rows(xs_ref, wdn[:, cols], HALO - EXT - 2, te, cols)
            sg = _sigmoid(pre)
            act = pre * sg
            pf, cf, nf, pb, cb, nb = d3[6 * part:6 * part + 6]
            dout = jnp.concatenate([pf[...] + pb[...], cf[...] + cb[...], nf[...] + nb[...]], axis=0)
            if part == 0:
                dact = _l2norm_heads_bwd(act, dout, Q_SCALE)
            elif part == 1:
                dact = _l2norm_heads_bwd(act, dout, 1.0)
            else:
                dact = dout
            dpre = jnp.where(inside, dact * (sg * (1.0 + pre * (1.0 - sg))), 0.0)
            dps_ref[:, cols] = dpre
            acc = wdn[0:1, cols] * dps_ref[EXT + 2:EXT + 2 + ts, cols]
            for tap in range(1, kdn):
                acc = acc + wdn[tap:tap + 1, cols] * dps_ref[EXT + 2 - tap:EXT + 2 - tap + ts, cols]
            dqkv_ref[:, cols] = acc.astype(BF16)
            dcur = dps_ref[EXT:EXT + ts, cols]
            for tap in range(kdn):
                dwdn_ref[tap:tap + 1, cols] += jnp.sum(
                    dcur * xs_ref[HALO - 2 + tap:HALO - 2 + tap + ts, cols], axis=0, keepdims=True)

        cv = lambda r: r[:, CONV_A:2 * CONV_A].astype(F32) * r[:, 2 * CONV_A:].astype(F32)
        _fill_halo(xa_ref, ap_ref, ac_ref, an_ref, first, last, fn=cv)
        wa = wa_ref[...]
        gate_b = jnp.concatenate([ap_ref[HALO - EXT:, 0:CONV_A], ac_ref[:, 0:CONV_A], an_ref[0:EXT, 0:CONV_A]],
                                 axis=0).astype(F32)
        dya_e = jnp.concatenate([yp_ref[...], yc_ref[...], yn_ref[...]], axis=0)
        dca_ref[...] = jnp.where(inside, dya_e * gate_b, 0.0)
        conv = _dwconv_rows(xa_ref, wa, HALO - 1, ts, slice(0, CONV_A))
        acc = wa[0:1, :] * dca_ref[EXT + 1:EXT + 1 + ts, :]
        for tap in range(1, ka):
            acc = acc + wa[tap:tap + 1, :] * dca_ref[EXT + 1 - tap:EXT + 1 - tap + ts, :]
        gc = ac_ref[:, CONV_A:2 * CONV_A].astype(F32)
        val = ac_ref[:, 2 * CONV_A:].astype(F32)
        da_ref[:, 0:CONV_A] = (yc_ref[...] * conv).astype(BF16)
        da_ref[:, CONV_A:2 * CONV_A] = (acc * val).astype(BF16)
        da_ref[:, 2 * CONV_A:] = (acc * gc).astype(BF16)
        dcur = dca_ref[EXT:EXT + ts, :]
        for tap in range(ka):
            dwa_ref[tap:tap + 1, :] += jnp.sum(
                dcur * xa_ref[HALO - 1 + tap:HALO - 1 + tap + ts, :], axis=0, keepdims=True)

    cot = [arr for pair in (dq2, dk2, dv2) for arr in pair for _ in range(3)]
    return _launch(
        body, carry, (proj, proj, proj, proj, proj, proj, *cot, dya, dya, dya, conv_dn, conv_a),
        name=name, grid=(nt,),
        in_specs=(_halo_specs(ts, QKV_W, 0, s) + _halo_specs(ts, A_W, A_OFF // A_W, s)
                  + _halo_specs(ts, hd, 0, s, rows=EXT) * 6 + _halo_specs(ts, CONV_A, 0, s, rows=EXT)
                  + [_resident(conv_dn.shape), _resident(conv_a.shape)]),
        out_specs=[pl.BlockSpec((ts, QKV_W), lambda i: (i, 0)), pl.BlockSpec((ts, A_W), lambda i: (i, 0)),
                   pl.BlockSpec((8, QKV_W), lambda i: (0, 0)), pl.BlockSpec((8, CONV_A), lambda i: (0, 0))],
        out_shape=[jax.ShapeDtypeStruct((s, QKV_W), BF16), jax.ShapeDtypeStruct((s, A_W), BF16),
                   jax.ShapeDtypeStruct((8, QKV_W), F32), jax.ShapeDtypeStruct((8, CONV_A), F32)],
        scratch_shapes=[pltpu.VMEM((ts + 2 * HALO, QKV_W), F32), pltpu.VMEM((te, QKV_W), F32),
                        pltpu.VMEM((ts + 2 * HALO, CONV_A), F32), pltpu.VMEM((te, CONV_A), F32)],
        sem=("arbitrary",))


IN_A = (0, 1536)
IN_QKV = (1536, 4608)
IN_Z = (4608, 5632)
IN_BA = 5632
IN_GATE = (5664, 7712)
IN_COLS = 7712
G_REPL = 4


def _split_w_in(w_in):
    sl = lambda ab: w_in[:, ab[0]:ab[1]]
    w_main = jnp.concatenate([sl(IN_QKV), sl(IN_Z), sl(IN_GATE), sl(IN_A)], axis=1)
    blocks = []
    for d in range(2):
        beta = w_in[:, IN_BA + 8 * d:IN_BA + 8 * d + 8]
        alpha = w_in[:, IN_BA + 16 + 8 * d:IN_BA + 24 + 8 * d]
        pad = jnp.zeros((w_in.shape[0], BA_W - 8 - 8 * G_REPL), w_in.dtype)
        blocks += [beta] + [alpha] * G_REPL + [pad]
    return w_main, jnp.concatenate(blocks, axis=1)


def _merge_dw_in(dw_qkv, dw_z, dw_gate, dw_a, dw_ba):
    ba = [dw_ba[:, 0:8], dw_ba[:, BA_W:BA_W + 8], dw_ba[:, 8:16], dw_ba[:, BA_W + 8:BA_W + 16]]
    return jnp.concatenate([dw_a, dw_qkv, dw_z] + ba + [dw_gate], axis=1)


def _decay_rows(a_log_fwd, dt_bias_fwd, a_log_bwd, dt_bias_bwd):
    def rows(f, b):
        out = []
        for vec in (f, b):
            vec = vec.reshape(HEADS)
            out.append(jnp.concatenate([jnp.zeros((8,), F32)] + [vec] * G_REPL
                                       + [jnp.zeros((BA_W - 8 - 8 * G_REPL,), F32)])[None])
        return jnp.stack(out)
    return rows(a_log_fwd, a_log_bwd), rows(dt_bias_fwd, dt_bias_bwd)


def _local_step(x, target, mod9, wt, comm):
    s, d = x.shape
    n = s // CHUNK
    wt = dict(wt)
    sh1, sc1, g1, sh2, sc2, g2, sh3, sc3, g3 = [mod9[i:i + 1] for i in range(9)]
    alog, dtb = _decay_rows(wt["a_log_fwd"], wt["dt_bias_fwd"], wt["a_log_bwd"], wt["dt_bias_bwd"])

    (u1, a1, b1, f1), got = comm.gather(
        ["w_ffn1_down", "w_in"],
        lambda c: _ffn_up_fwd(x, wt["norm_ffn1"], sc1, sh1, wt["w_ffn1_up"], name="ffn1_up", carry=c))
    wt.update(got)
    w_main, w_ba = _split_w_in(wt["w_in"])
    y1, h1 = _ffn_down_fwd(f1, wt["w_ffn1_down"], x, g1, name="ffn1_down")
    (u2, proj, ba), got = comm.gather(
        ["w_a_out", "w_b_out", "w_out", "w_ffn2_up", "w_ffn2_down"],
        lambda c: _in_proj_fwd(h1, wt["norm_mix"], sc2, sh2, w_main, w_ba, name="in_proj", carry=c))
    wt.update(got)
    scal = _scal_fwd(ba, alog, dtb, name="scal_fwd")
    grow = scal[:, :, L_G:L_G + 8].reshape(2, n, CHUNK, HEADS).transpose(0, 1, 3, 2).reshape(
        2, n, N_GROUPS, GROWS)
    q, k, v, ya = _conv_fwd(proj, wt["conv_dn"], wt["conv_a"], name="conv_fwd")
    o2, states, tinv, vn = _delta_fwd(q, k, v, scal, grow, name="delta_fwd")
    yb = _gate_norm_fwd(o2, proj, wt["dn_norm"], name="gate_norm_fwd")
    pa, pb, mix, h2 = _merge_fwd(ya, yb, proj, wt["w_a_out"], wt["w_b_out"], wt["w_out"], h1, g2,
                                 name="merge_fwd")
    u3, a3, b3, f3 = _ffn_up_fwd(h2, wt["norm_ffn2"], sc3, sh3, wt["w_ffn2_up"], name="ffn2_up")
    y3, h3 = _ffn_down_fwd(f3, wt["w_ffn2_down"], h2, g3, name="ffn2_down")
    loss, dh3, dnorm_final = _final_fwd_bwd(h3, wt["norm_final"], target, name="final")

    dy3, dab3, dg3 = _ffn_bwd_act(dh3, g3, y3, a3, b3, wt["w_ffn2_down"], name="ffn2_bwd_act")
    dh2, dn3, dsc3, dsh3 = _norm_mod_matmul_bwd([(dab3, wt["w_ffn2_up"])], h2, wt["norm_ffn2"], sc3, dh3,
                                                name="ffn2_bwd_up")
    gw = {}
    gw["w_ffn2_up"] = _matmul_tn(u3, dab3, name="dw_ffn2_up", tm=1024, tn=1408)
    gw["w_ffn2_down"] = _matmul_tn(f3, dy3, name="dw_ffn2_down", tm=1408, tn=1024)

    dmix, merged, dpa, dpb, dgates, dya, dyb, dg2 = _merge_bwd(
        dh2, g2, mix, pa, pb, proj, wt["w_a_out"], wt["w_b_out"], wt["w_out"], name="merge_bwd")
    gw["w_out"] = _matmul_tn(merged, dmix, name="dw_out", tm=1024, tn=1024)
    gw["w_a_out"] = _matmul_tn(ya, dpa, name="dw_a_out", tm=512, tn=1024)
    gw["w_b_out"] = _matmul_tn(yb, dpb, name="dw_b_out", tm=1024, tn=1024)
    do, dz, ddn = _gate_norm_bwd(dyb, o2, proj, wt["dn_norm"], name="gate_norm_bwd")
    recv = {}
    (dq2, dk2, dv2, dscal, drow), got = comm.scatter(
        {nm: gw.pop(nm) for nm in ("w_ffn2_up", "w_ffn2_down")},
        lambda c: _delta_bwd(q, k, v, scal, grow, states, tinv, vn, do, name="delta_bwd", carry=c))
    recv.update(got)
    drow_p = jnp.pad(jnp.stack(drow).reshape(2, n, HEADS, CHUNK).transpose(0, 1, 3, 2).reshape(2, s, HEADS),
                     ((0, 0), (0, 0), (L_G, BA_W - L_G - HEADS)))
    dba, dalog, ddtb = _scal_bwd(jnp.stack(dscal), drow_p, ba, alog, dtb, name="scal_bwd")
    (dqkv, dbr_a, dconv_dn, dconv_a), got = comm.scatter(
        {nm: gw.pop(nm) for nm in ("w_out", "w_a_out", "w_b_out")},
        lambda c: _conv_bwd(dq2, dk2, dv2, dya, proj, wt["conv_dn"], wt["conv_a"], name="conv_bwd", carry=c))
    recv.update(got)
    dh1, dn2, dsc2, dsh2 = _norm_mod_matmul_bwd(
        [(dqkv, w_main, 0), (dz, w_main, Z_OFF // Z_W), (dgates, w_main, GATE_OFF // GATE_W),
         (dbr_a, w_main, A_OFF // A_W), (dba, w_ba)],
        h1, wt["norm_mix"], sc2, dh2, name="in_proj_bwd")
    gw["w_in"] = _merge_dw_in(
        _matmul_tn(u2, dqkv, name="dw_in_qkv", tm=1024, tn=1536),
        _matmul_tn(u2, dz, name="dw_in_z", tm=1024, tn=1024),
        _matmul_tn(u2, dgates, name="dw_in_gate", tm=1024, tn=1024),
        _matmul_tn(u2, dbr_a, name="dw_in_a", tm=1024, tn=1536),
        _matmul_tn(u2, dba, name="dw_in_ba", tm=1024, tn=2 * BA_W))

    (dy1, dab1, dg1), got = comm.scatter(
        {"w_in": gw.pop("w_in")},
        lambda c: _ffn_bwd_act(dh1, g1, y1, a1, b1, wt["w_ffn1_down"], name="ffn1_bwd_act", carry=c))
    recv.update(got)
    dw_down1 = _matmul_tn(f1, dy1, name="dw_ffn1_down", tm=1408, tn=1024)
    dx, dn1, dsc1, dsh1 = _norm_mod_matmul_bwd([(dab1, wt["w_ffn1_up"])], x, wt["norm_ffn1"], sc1, dh1,
                                               name="ffn1_bwd_up")
    dw_up1, got = comm.scatter(
        {"w_ffn1_down": dw_down1},
        lambda c: _matmul_tn(u1, dab1, name="dw_ffn1_up", tm=1024, tn=1408, carry=c))
    recv.update(got)
    recv.update(comm.scatter({"w_ffn1_up": dw_up1}, None, name="scatter_last")[1])

    small = {
        "mod": jnp.concatenate([dsh1, dsc1, dg1, dsh2, dsc2, dg2, dsh3, dsc3, dg3], axis=1),
        "norm_ffn1": dn1, "norm_mix": dn2, "norm_ffn2": dn3, "norm_final": dnorm_final,
        "a_log_fwd": dalog[0, :, L_G:L_G + 8], "dt_bias_fwd": ddtb[0, :, L_G:L_G + 8],
        "a_log_bwd": dalog[1, :, L_G:L_G + 8], "dt_bias_bwd": ddtb[1, :, L_G:L_G + 8],
        "dn_norm": ddn,
        "conv_a": dconv_a[0:3].reshape(1, -1), "conv_dn": dconv_dn[0:5].reshape(1, -1),
    }
    return loss, dx, recv, small


def _full_weight(name, g):
    if name in COL_SHARDED + CONV_SHARDED:
        return g.transpose(1, 0, 2).reshape(g.shape[1], -1)
    return g.reshape(-1, g.shape[-1])


def _grad_pieces(name, g):
    g = g.astype(BF16)
    if name in COL_SHARDED:
        return g.reshape(g.shape[0], N_DEV, -1).transpose(1, 0, 2)
    return g.reshape(N_DEV, -1, g.shape[-1])


class _MeshComm:
    def __init__(self, shards):
        self.shards = shards

    def _run(self, xs, carrier, name, gather):
        if carrier is None:
            return None, _exchange(xs, name=name, gather=gather)
        return carrier((xs, gather))

    def gather(self, names, carrier=None, name=None):
        outs, got = self._run([self.shards[nm] for nm in names], carrier, name, True)
        return outs, {nm: _full_weight(nm, g) for nm, g in zip(names, got)}

    def scatter(self, grads, carrier=None, name=None):
        names = list(grads)
        outs, got = self._run([_grad_pieces(nm, grads[nm]) for nm in names], carrier, name, False)
        return outs, dict(zip(names, got))


def _mod_fwd(c_all, w_ada, *, name):
    def body(c_ref, w_ref, o_ref):
        cv = c_ref[...]
        o_ref[...] = _dot3(cv * _sigmoid(cv), w_ref[...])

    return pl.pallas_call(
        body, name=name, out_shape=jax.ShapeDtypeStruct((c_all.shape[0], w_ada.shape[1]), F32),
        compiler_params=_params(),
    )(c_all, w_ada)


def _adamw_math(w, g, m, v):
    m_new = ADAM_B1 * m + (1.0 - ADAM_B1) * g
    v_new = ADAM_B2 * v + (1.0 - ADAM_B2) * (g * g)
    m_hat = m_new / (1.0 - ADAM_B1 ** ADAM_STEP)
    v_hat = v_new / (1.0 - ADAM_B2 ** ADAM_STEP)
    delta = -ADAM_LR * (m_hat / (jnp.sqrt(v_hat) + ADAM_EPS) + ADAM_WD * w)
    return delta, m_new, v_new


def _reduce_adamw(pieces, w, m, v, *, name, tr):
    r, c = w.shape

    def body(p_ref, w_ref, m_ref, v_ref, g_ref, d_ref, mo_ref, vo_ref):
        g = p_ref[0].astype(F32)
        for src in range(1, N_DEV):
            g = g + p_ref[src].astype(F32)
        g_ref[...] = g
        d_ref[...], mo_ref[...], vo_ref[...] = _adamw_math(w_ref[...], g, m_ref[...], v_ref[...])

    tile = pl.BlockSpec((tr, c), lambda i: (i, 0))
    return pl.pallas_call(
        body, name=name, grid=(r // tr,),
        in_specs=[pl.BlockSpec((N_DEV, tr, c), lambda i: (0, i, 0)), tile, tile, tile],
        out_specs=[tile] * 4, out_shape=[jax.ShapeDtypeStruct((r, c), F32)] * 4,
        compiler_params=_params(("parallel",)),
    )(pieces, w, m, v)


def _ada_grad_adamw(c_all_t, dmod_cols, w, m, v, *, name, tr=256):
    r, c = w.shape

    def body(c_ref, dm_ref, w_ref, m_ref, v_ref, g_ref, d_ref, mo_ref, vo_ref):
        cv = c_ref[...]
        act = cv * _sigmoid(cv)
        dm = dm_ref[...]
        g = act[:, 0:1] * dm[0:1, :]
        for b in range(1, N_DEV):
            g = g + act[:, b:b + 1] * dm[b:b + 1, :]
        g_ref[...] = g
        d_ref[...], mo_ref[...], vo_ref[...] = _adamw_math(w_ref[...], g, m_ref[...], v_ref[...])

    tile = pl.BlockSpec((tr, c), lambda i: (i, 0))
    return pl.pallas_call(
        body, name=name, grid=(r // tr,),
        in_specs=[pl.BlockSpec((tr, N_DEV), lambda i: (i, 0)), pl.BlockSpec((N_DEV, c), lambda i: (0, 0)),
                  tile, tile, tile],
        out_specs=[tile] * 4, out_shape=[jax.ShapeDtypeStruct((r, c), F32)] * 4,
        compiler_params=_params(("parallel",)),
    )(c_all_t, dmod_cols, w, m, v)


def _sum_rows(parts, *, name):
    def body(p_ref, o_ref):
        acc = p_ref[0:1, :]
        for src in range(1, N_DEV):
            acc = acc + p_ref[src:src + 1, :]
        o_ref[...] = acc

    return pl.pallas_call(
        body, name=name, out_shape=jax.ShapeDtypeStruct((1, parts.shape[1]), F32), compiler_params=_params(),
    )(parts)


def _adamw_rows(g, w, m, v, *, name):
    def body(g_ref, w_ref, m_ref, v_ref, d_ref, mo_ref, vo_ref):
        d_ref[...], mo_ref[...], vo_ref[...] = _adamw_math(w_ref[...], g_ref[...], m_ref[...], v_ref[...])

    return pl.pallas_call(
        body, name=name, out_shape=[jax.ShapeDtypeStruct(g.shape, F32)] * 3, compiler_params=_params(),
    )(g, w, m, v)


WEIGHTS = ["w_ada", "b_ada", "norm_ffn1", "w_ffn1_up", "w_ffn1_down", "norm_mix", "w_in", "conv_a", "conv_dn",
           "a_log_fwd", "dt_bias_fwd", "a_log_bwd", "dt_bias_bwd", "dn_norm", "w_a_out", "w_b_out", "w_out",
           "norm_ffn2", "w_ffn2_up", "w_ffn2_down", "norm_final"]
COL_SHARDED = ["w_ffn1_up", "w_in", "w_a_out", "w_ffn2_up"]
ROW_SHARDED = ["w_ffn1_down", "w_b_out", "w_out", "w_ffn2_down"]
CONV_SHARDED = ["conv_a", "conv_dn"]
REPLICATED = ["b_ada", "norm_ffn1", "norm_mix", "a_log_fwd", "dt_bias_fwd", "a_log_bwd", "dt_bias_bwd",
              "dn_norm", "norm_ffn2", "norm_final"]
SMALL_ORDER = ["mod", "norm_ffn1", "norm_mix", "norm_ffn2", "norm_final", "a_log_fwd", "dt_bias_fwd",
               "a_log_bwd", "dt_bias_bwd", "dn_norm", "conv_a", "conv_dn"]
REDUCE_ROWS = {"w_ffn1_up": 256, "w_in": 256, "w_a_out": 256, "w_ffn2_up": 256,
               "w_ffn1_down": 176, "w_b_out": 128, "w_out": 128, "w_ffn2_down": 176}


def _pad_lanes(row):
    pad = (-row.shape[1]) % 128
    return jnp.pad(row, ((0, 0), (0, pad)))


def _unstack_cols(g):
    return g.transpose(1, 0, 2).reshape(g.shape[1], -1)


def _stack_cols(w):
    k = w.shape[0]
    return w.reshape(k, N_DEV, -1).transpose(1, 0, 2)


def kernel(x, c, w_ada, b_ada, norm_ffn1, w_ffn1_up, w_ffn1_down, norm_mix, w_in, conv_a, conv_dn, a_log_fwd, dt_bias_fwd, a_log_bwd, dt_bias_bwd, dn_norm, w_a_out, w_b_out, w_out, norm_ffn2, w_ffn2_up, w_ffn2_down, norm_final, loss_target, m_w_ada, m_b_ada, m_norm_ffn1, m_w_ffn1_up, m_w_ffn1_down, m_norm_mix, m_w_in, m_conv_a, m_conv_dn, m_a_log_fwd, m_dt_bias_fwd, m_a_log_bwd, m_dt_bias_bwd, m_dn_norm, m_w_a_out, m_w_b_out, m_w_out, m_norm_ffn2, m_w_ffn2_up, m_w_ffn2_down, m_norm_final, v_w_ada, v_b_ada, v_norm_ffn1, v_w_ffn1_up, v_w_ffn1_down, v_norm_mix, v_w_in, v_conv_a, v_conv_dn, v_a_log_fwd, v_dt_bias_fwd, v_a_log_bwd, v_dt_bias_bwd, v_dn_norm, v_w_a_out, v_w_b_out, v_w_out, v_norm_ffn2, v_w_ffn2_up, v_w_ffn2_down, v_norm_final):
    args = dict(locals())
    w_loc = {n: args[n] for n in WEIGHTS}
    m_loc = {n: args["m_" + n] for n in WEIGHTS}
    v_loc = {n: args["v_" + n] for n in WEIGHTS}
    me = _flat_index(_my_position())
    d_model = x.shape[-1]

    big = COL_SHARDED + ROW_SHARDED
    shards = {n: w_loc[n][0].astype(BF16) for n in big}
    shards.update({n: w_loc[n][0] for n in CONV_SHARDED})
    shards["c"] = c
    comm = _MeshComm(shards)
    wt = comm.gather(["c", "conv_a", "conv_dn", "w_ffn1_up"], name="gather_first")[1]
    c_all = wt.pop("c")
    for n in REPLICATED[1:]:
        wt[n] = w_loc[n].reshape(1, -1)

    mod_cols = _mod_fwd(c_all, w_ada[0], name="mod_fwd")
    mod_all = _exchange([mod_cols], name="gather_mod", gather=True)[0]
    mod_mine = lax.dynamic_index_in_dim(mod_all, me, axis=1, keepdims=False).reshape(1, -1) + b_ada
    mod9 = mod_mine.reshape(9, d_model)

    loss_loc, dx, recv, small = _local_step(x[0], loss_target[0], mod9, wt, comm)
    loss = lax.psum(loss_loc[0, 0], MESH_AXES)

    res = {}
    for n in big:
        res[n] = _reduce_adamw(recv[n], w_loc[n][0], m_loc[n][0], v_loc[n][0], name="adamw_" + n,
                               tr=REDUCE_ROWS[n])

    packed = _pad_lanes(jnp.concatenate([small[n].reshape(1, -1) for n in SMALL_ORDER], axis=1))
    parts = _exchange([packed], name="gather_small", gather=True)[0].reshape(N_DEV, -1)
    total = _sum_rows(parts, name="sum_small")
    off = 0
    gsmall = {}
    for n in SMALL_ORDER:
        size = small[n].size
        gsmall[n] = total[:, off:off + size]
        off += size
    dmod_all = parts[:, 0:9 * d_model]
    ada_cols = w_ada.shape[-1]
    dmod_cols = lax.dynamic_slice_in_dim(dmod_all, me * ada_cols, ada_cols, axis=1)
    res["w_ada"] = _ada_grad_adamw(c_all.T, dmod_cols, w_ada[0], m_w_ada[0], v_w_ada[0], name="adamw_w_ada")
    g_rows = {"b_ada": gsmall["mod"]}
    for n in REPLICATED[1:]:
        g_rows[n] = gsmall[n]
    for n in CONV_SHARDED:
        taps, width = w_loc[n].shape[1], w_loc[n].shape[2]
        full = gsmall[n].reshape(taps, -1)
        g_rows[n] = lax.dynamic_slice_in_dim(full, me * width, width, axis=1).reshape(1, -1)
    row_names = REPLICATED + CONV_SHARDED
    cat = lambda src: _pad_lanes(jnp.concatenate([src[n].reshape(1, -1) for n in row_names], axis=1))
    g_cat = cat(g_rows)
    d_cat, m_cat, v_cat = _adamw_rows(g_cat, cat(w_loc), cat(m_loc), cat(v_loc), name="adamw_small")
    off = 0
    for n in row_names:
        size = w_loc[n].size
        res[n] = tuple(t[:, off:off + size] for t in (g_cat, d_cat, m_cat, v_cat))
        off += size

    outs = [loss, dx[None]]
    for kind in range(4):
        for n in WEIGHTS:
            outs.append(res[n][kind].reshape(w_loc[n].shape))
    return tuple(outs)
```

```python
import functools
import math

import jax
import jax.numpy as jnp
from jax import lax
from jax.experimental import pallas as pl
from jax.experimental.pallas import tpu as pltpu

F32 = jnp.float32
BF16 = jnp.bfloat16
EPS = 1e-6
N_DEV = 8
CHUNK = 64
HEADS = 8
HEAD_DIM = 128
MESH_AXES = ("x", "y", "c")
VMEM_LIMIT_BYTES = 56 * 1024 * 1024

ADAM_LR = 0.001
ADAM_B1 = 0.9
ADAM_B2 = 0.999
ADAM_EPS = 1e-08
ADAM_WD = 0.01
ADAM_STEP = 10


def _params(sem=None):
    return pltpu.CompilerParams(dimension_semantics=sem, vmem_limit_bytes=VMEM_LIMIT_BYTES)


def _row(n):
    return pl.BlockSpec((1, n), lambda *_: (0, 0))


def _resident(shape):
    nd = len(shape)
    return pl.BlockSpec(shape, lambda *_: (0,) * nd, pipeline_mode=pl.Buffered(1))


def _col_window(w, width, col_block):
    return pl.BlockSpec((w.shape[0], width), lambda *_: (0, col_block), pipeline_mode=pl.Buffered(1))


def _sigmoid(x):
    return 1.0 / (1.0 + jnp.exp(-x))


def _dot(a, b):
    return jnp.dot(a, b, preferred_element_type=F32)


def _dot_nt(a, b):
    return lax.dot_general(a, b, (((1,), (1,)), ((), ())), preferred_element_type=F32)


def _dot_tn(a, b):
    return lax.dot_general(a, b, (((0,), (0,)), ((), ())), preferred_element_type=F32)


def _split_bf16(x):
    hi = x.astype(BF16)
    lo = (x - hi.astype(F32)).astype(BF16)
    return hi, lo


def _dot3(a, b, dot=_dot):
    ah, al = a if isinstance(a, tuple) else _split_bf16(a)
    bh, bl = b if isinstance(b, tuple) else _split_bf16(b)
    return dot(ah, bh) + dot(ah, bl) + dot(al, bh)


def _dot_exact(a, b):
    return jnp.dot(a, b, preferred_element_type=F32, precision=lax.Precision.HIGHEST)


def _my_position():
    return tuple(lax.axis_index(a) for a in MESH_AXES)


def _peer(pos, kk):
    return tuple((1 - p) if (kk >> (2 - b)) & 1 else p for b, p in enumerate(pos))


def _flat_index(pos):
    return pos[0] * 4 + pos[1] * 2 + pos[2]


_ANY = pl.BlockSpec(memory_space=pl.ANY)


class _AllToAll:
    def __init__(self, in_refs, out_refs, send_sems, recv_sems, local_sems):
        pos = _my_position()
        me = _flat_index(pos)
        self.copies = []
        for t in range(len(in_refs)):
            self.copies.append(pltpu.make_async_copy(in_refs[t].at[me], out_refs[t].at[me], local_sems.at[t]))
            for kk in range(1, N_DEV):
                peer = _peer(pos, kk)
                self.copies.append(pltpu.make_async_remote_copy(
                    src_ref=in_refs[t].at[_flat_index(peer)], dst_ref=out_refs[t].at[me],
                    send_sem=send_sems.at[t, kk - 1], recv_sem=recv_sems.at[t, kk - 1],
                    device_id=peer, device_id_type=pl.DeviceIdType.MESH))

    def start(self):
        for cp in self.copies:
            cp.start()

    def finish(self):
        for cp in self.copies:
            cp.wait()


class _AllGather:
    def __init__(self, in_refs, out_refs, send_sems, recv_sems, local_sems):
        self.refs = (in_refs, out_refs, send_sems, recv_sems, local_sems)
        x, y, c = _my_position()
        self.me, self.sibling = (x, y, c), (x, y, 1 - c)
        self.chips = [(1 - x, y), (x, 1 - y), (1 - x, 1 - y)]
        self.core = c

    def _copy(self, t, k, block, to, own=False):
        in_refs, out_refs, send_sems, recv_sems, _ = self.refs
        rows = out_refs[t].at[_flat_index(block)]
        return pltpu.make_async_remote_copy(
            src_ref=in_refs[t] if own else rows, dst_ref=rows,
            send_sem=send_sems.at[t, k], recv_sem=recv_sems.at[t, k],
            device_id=to, device_id_type=pl.DeviceIdType.MESH)

    def _local(self, t):
        in_refs, out_refs, _, _, local_sems = self.refs
        return pltpu.make_async_copy(in_refs[t], out_refs[t].at[_flat_index(self.me)], local_sems.at[t])

    def start(self):
        c = self.core
        for t in range(len(self.refs[0])):
            self._local(t).start()
            self._copy(t, 0, self.me, self.sibling, own=True).start()
            for j, chip in enumerate(self.chips):
                self._copy(t, 1 + j, self.me, (*chip, c), own=True).start()

    def finish(self):
        c = self.core
        n_t = len(self.refs[0])
        for t in range(n_t):
            for j, chip in enumerate(self.chips):
                self._copy(t, 1 + j, (*chip, c), self.me).wait_recv()
                self._copy(t, 4 + j, (*chip, c), self.sibling).start()
        for t in range(n_t):
            self._copy(t, 0, self.sibling, self.me).wait_recv()
            for j, chip in enumerate(self.chips):
                self._copy(t, 4 + j, (*chip, 1 - c), self.me).wait_recv()
            self._copy(t, 0, self.me, self.sibling, own=True).wait_send()
            for j, chip in enumerate(self.chips):
                self._copy(t, 1 + j, self.me, (*chip, c), own=True).wait_send()
                self._copy(t, 4 + j, (*chip, c), self.sibling).wait_send()
            self._local(t).wait()


def _exchange_plan(in_refs, out_refs, send_sems, recv_sems, local_sems, gather):
    return (_AllGather if gather else _AllToAll)(in_refs, out_refs, send_sems, recv_sems, local_sems)


def _exchange_shapes(xs, gather):
    out_shape = [jax.ShapeDtypeStruct(((N_DEV,) + x.shape) if gather else x.shape, x.dtype) for x in xs]
    sems = [pltpu.SemaphoreType.DMA((len(xs), N_DEV - 1)), pltpu.SemaphoreType.DMA((len(xs), N_DEV - 1)),
            pltpu.SemaphoreType.DMA((len(xs),))]
    return out_shape, sems


def _exchange(xs, *, name, gather):
    nt = len(xs)

    def body(*refs):
        plan = _exchange_plan(refs[:nt], refs[nt:2 * nt], *refs[2 * nt:], gather)
        plan.start()
        plan.finish()

    out_shape, sems = _exchange_shapes(xs, gather)
    return pl.pallas_call(body, name=name, in_specs=[_ANY] * nt, out_specs=[_ANY] * nt, out_shape=out_shape,
                          scratch_shapes=sems)(*xs)


def _launch(body, carry, args, *, name, grid, in_specs, out_specs, out_shape, scratch_shapes=(), sem):
    single = not isinstance(out_shape, (list, tuple))
    out_specs = [out_specs] if single else list(out_specs)
    out_shape = [out_shape] if single else list(out_shape)
    if carry is None:
        outs = pl.pallas_call(body, name=name, grid=grid, in_specs=list(in_specs), out_specs=out_specs,
                              out_shape=out_shape, scratch_shapes=list(scratch_shapes),
                              compiler_params=_params(sem))(*args)
        return outs[0] if single else outs
    xs, gather = carry
    nt, n_in, n_out, n_scr = len(xs), len(args), len(out_shape), len(scratch_shapes)
    x_shape, sems = _exchange_shapes(xs, gather)

    def wrapped(*refs):
        c_in, x_in = refs[:n_in], refs[n_in:n_in + nt]
        c_out = refs[n_in + nt:n_in + nt + n_out]
        x_out = refs[n_in + nt + n_out:n_in + 2 * nt + n_out]
        scr = refs[n_in + 2 * nt + n_out:]
        ids = [pl.program_id(a) for a in range(len(grid))]
        first = functools.reduce(jnp.logical_and, [i == 0 for i in ids])
        last = functools.reduce(jnp.logical_and, [i == g - 1 for i, g in zip(ids, grid)])
        plan = lambda: _exchange_plan(x_in, x_out, *scr[n_scr:], gather)

        @pl.when(first)
        def _():
            plan().start()

        body(*c_in, *c_out, *scr[:n_scr])

        @pl.when(last)
        def _():
            plan().finish()

    outs = pl.pallas_call(
        wrapped, name=name, grid=grid, in_specs=list(in_specs) + [_ANY] * nt,
        out_specs=out_specs + [_ANY] * nt, out_shape=out_shape + x_shape,
        scratch_shapes=list(scratch_shapes) + sems,
        compiler_params=_params(("arbitrary",) * len(grid)))(*args, *xs)
    compute = outs[:n_out]
    return (compute[0] if single else compute), outs[n_out:]


def _norm_mod(x, nw, sc, sh):
    r = lax.rsqrt(jnp.mean(x * x, axis=-1, keepdims=True) + EPS)
    return (x * r * nw) * (1.0 + sc) + sh


def _norm_mod_bwd(x, nw, sc, du):
    r = lax.rsqrt(jnp.mean(x * x, axis=-1, keepdims=True) + EPS)
    xhat = x * r
    n = xhat * nw
    dsh = jnp.sum(du, axis=0, keepdims=True)
    dsc = jnp.sum(du * n, axis=0, keepdims=True)
    dn = du * (1.0 + sc)
    dnw = jnp.sum(dn * xhat, axis=0, keepdims=True)
    dxhat = dn * nw
    dx = r * (dxhat - xhat * jnp.mean(dxhat * xhat, axis=-1, keepdims=True))
    return dx, dnw, dsc, dsh


def _ffn_up_fwd(h, nw, sc, sh, wup, *, name, ts=512, tn=1408, carry=None):
    s, d = h.shape
    f_dim = wup.shape[1] // 2
    nj = f_dim // tn

    def body(h_ref, nw_ref, sc_ref, sh_ref, wa_ref, wb_ref, u_ref, a_ref, b_ref, f_ref):
        @pl.when(pl.program_id(1) == 0)
        def _():
            u_ref[...] = _norm_mod(h_ref[...], nw_ref[...], sc_ref[...], sh_ref[...]).astype(BF16)

        u = u_ref[...]
        a = _dot(u, wa_ref[...])
        b = _dot(u, wb_ref[...])
        a_ref[...] = a.astype(BF16)
        b_ref[...] = b.astype(BF16)
        f_ref[...] = (a * _sigmoid(a) * b).astype(BF16)

    return _launch(
        body, carry, (h, nw, sc, sh, wup, wup), name=name, grid=(s // ts, nj),
        in_specs=[pl.BlockSpec((ts, d), lambda i, j: (i, 0)), _row(d), _row(d), _row(d),
                  pl.BlockSpec((d, tn), lambda i, j: (0, j)),
                  pl.BlockSpec((d, tn), lambda i, j: (0, j + nj))],
        out_specs=[pl.BlockSpec((ts, d), lambda i, j: (i, 0)),
                   pl.BlockSpec((ts, tn), lambda i, j: (i, j)),
                   pl.BlockSpec((ts, tn), lambda i, j: (i, j)),
                   pl.BlockSpec((ts, tn), lambda i, j: (i, j))],
        out_shape=[jax.ShapeDtypeStruct((s, d), BF16)] + [jax.ShapeDtypeStruct((s, f_dim), BF16)] * 3,
        sem=("parallel", "arbitrary"))


def _ffn_down_fwd(f, wd, h, g, *, name, ts=512):
    s, f_dim = f.shape
    d = wd.shape[1]

    def body(f_ref, wd_ref, h_ref, g_ref, y_ref, ho_ref):
        y = _dot(f_ref[...], wd_ref[...])
        y_ref[...] = y.astype(BF16)
        ho_ref[...] = h_ref[...] + (0.5 * g_ref[...]) * y

    return pl.pallas_call(
        body, name=name, grid=(s // ts,),
        in_specs=[pl.BlockSpec((ts, f_dim), lambda i: (i, 0)), _resident((f_dim, d)),
                  pl.BlockSpec((ts, d), lambda i: (i, 0)), _row(d)],
        out_specs=[pl.BlockSpec((ts, d), lambda i: (i, 0)), pl.BlockSpec((ts, d), lambda i: (i, 0))],
        out_shape=[jax.ShapeDtypeStruct((s, d), BF16), jax.ShapeDtypeStruct((s, d), F32)],
        compiler_params=_params(("parallel",)),
    )(f, wd, h, g)


def _ffn_bwd_act(dh, g, y, a, b, wd, *, name, ts=256, carry=None):
    s, d = dh.shape
    f_dim = a.shape[1]

    def body(dh_ref, g_ref, y_ref, a_ref, b_ref, wd_ref, dy_ref, dab_ref, dg_ref):
        dh_v = dh_ref[...]
        dy = ((0.5 * g_ref[...]) * dh_v).astype(BF16)
        dy_ref[...] = dy
        part = jnp.sum(0.5 * dh_v * y_ref[...].astype(F32), axis=0, keepdims=True)

        @pl.when(pl.program_id(0) == 0)
        def _():
            dg_ref[...] = jnp.zeros_like(dg_ref)

        dg_ref[...] += part
        df = _dot_nt(dy, wd_ref[...])
        av = a_ref[...].astype(F32)
        bv = b_ref[...].astype(F32)
        sg = _sigmoid(av)
        dab_ref[:, :f_dim] = (df * bv * (sg * (1.0 + av * (1.0 - sg)))).astype(BF16)
        dab_ref[:, f_dim:] = (df * (av * sg)).astype(BF16)

    return _launch(
        body, carry, (dh, g, y, a, b, wd), name=name, grid=(s // ts,),
        in_specs=[pl.BlockSpec((ts, d), lambda i: (i, 0)), _row(d),
                  pl.BlockSpec((ts, d), lambda i: (i, 0)),
                  pl.BlockSpec((ts, f_dim), lambda i: (i, 0)),
                  pl.BlockSpec((ts, f_dim), lambda i: (i, 0)),
                  _resident((f_dim, d))],
        out_specs=[pl.BlockSpec((ts, d), lambda i: (i, 0)),
                   pl.BlockSpec((ts, 2 * f_dim), lambda i: (i, 0)), _row(d)],
        out_shape=[jax.ShapeDtypeStruct((s, d), BF16), jax.ShapeDtypeStruct((s, 2 * f_dim), BF16),
                   jax.ShapeDtypeStruct((1, d), F32)],
        sem=("arbitrary",))


def _norm_mod_matmul_bwd(pairs, h, nw, sc, dh_in, *, name, ts=256, carry=None):
    s, d = h.shape
    n_pairs = len(pairs)

    def body(*refs):
        dx_refs = refs[:n_pairs]
        w_refs = refs[n_pairs:2 * n_pairs]
        h_ref, nw_ref, sc_ref, dhi_ref, dho_ref, dnw_ref, dsc_ref, dsh_ref = refs[2 * n_pairs:]
        du = _dot_nt(dx_refs[0][...], w_refs[0][...])
        for k in range(1, n_pairs):
            du = du + _dot_nt(dx_refs[k][...], w_refs[k][...])
        dx, dnw, dsc, dsh = _norm_mod_bwd(h_ref[...], nw_ref[...], sc_ref[...], du)
        dho_ref[...] = dhi_ref[...] + dx

        @pl.when(pl.program_id(0) == 0)
        def _():
            dnw_ref[...] = jnp.zeros_like(dnw_ref)
            dsc_ref[...] = jnp.zeros_like(dsc_ref)
            dsh_ref[...] = jnp.zeros_like(dsh_ref)

        dnw_ref[...] += dnw
        dsc_ref[...] += dsc
        dsh_ref[...] += dsh

    dxs = [p[0] for p in pairs]
    ws = [p[1] for p in pairs]
    tile = pl.BlockSpec((ts, d), lambda i: (i, 0))
    return _launch(
        body, carry, (*dxs, *ws, h, nw, sc, dh_in), name=name, grid=(s // ts,),
        in_specs=([pl.BlockSpec((ts, x.shape[1]), lambda i: (i, 0)) for x in dxs]
                  + [_col_window(w, x.shape[1], p[2] if len(p) > 2 else 0) for p, x, w in zip(pairs, dxs, ws)]
                  + [tile, _row(d), _row(d), tile]),
        out_specs=[tile, _row(d), _row(d), _row(d)],
        out_shape=[jax.ShapeDtypeStruct((s, d), F32)] + [jax.ShapeDtypeStruct((1, d), F32)] * 3,
        sem=("arbitrary",))


def _matmul_tn(a, b, *, name, tm, tn, tk=1024, carry=None):
    s, m = a.shape
    n = b.shape[1]
    tk = min(tk, s)
    nk = s // tk

    def body(a_ref, b_ref, o_ref, acc_ref):
        k = pl.program_id(2)

        @pl.when(k == 0)
        def _():
            acc_ref[...] = jnp.zeros_like(acc_ref)

        acc_ref[...] += _dot_tn(a_ref[...], b_ref[...])

        @pl.when(k == nk - 1)
        def _():
            o_ref[...] = acc_ref[...].astype(o_ref.dtype)

    return _launch(
        body, carry, (a, b), name=name, grid=(m // tm, n // tn, nk),
        in_specs=[pl.BlockSpec((tk, tm), lambda i, j, k: (k, i)),
                  pl.BlockSpec((tk, tn), lambda i, j, k: (k, j))],
        out_specs=pl.BlockSpec((tm, tn), lambda i, j, k: (i, j)),
        out_shape=jax.ShapeDtypeStruct((m, n), BF16),
        scratch_shapes=[pltpu.VMEM((tm, tn), F32)],
        sem=("parallel", "parallel", "arbitrary"))


def _in_proj_fwd(h, nw, sc, sh, w_main, w_ba, *, name, ts=512, tn=1536, carry=None):
    s, d = h.shape
    n_main = w_main.shape[1]
    n_ba = w_ba.shape[1]

    def body(h_ref, nw_ref, sc_ref, sh_ref, w_ref, wba_ref, u_ref, p_ref, ba_ref):
        @pl.when(pl.program_id(1) == 0)
        def _():
            u0 = _norm_mod(h_ref[...], nw_ref[...], sc_ref[...], sh_ref[...]).astype(BF16)
            u_ref[...] = u0
            ba_ref[...] = _dot(u0, wba_ref[...])

        p_ref[...] = _dot(u_ref[...], w_ref[...]).astype(BF16)

    return _launch(
        body, carry, (h, nw, sc, sh, w_main, w_ba), name=name, grid=(s // ts, n_main // tn),
        in_specs=[pl.BlockSpec((ts, d), lambda i, j: (i, 0)), _row(d), _row(d), _row(d),
                  pl.BlockSpec((d, tn), lambda i, j: (0, j)), _resident((d, n_ba))],
        out_specs=[pl.BlockSpec((ts, d), lambda i, j: (i, 0)),
                   pl.BlockSpec((ts, tn), lambda i, j: (i, j)),
                   pl.BlockSpec((ts, n_ba), lambda i, j: (i, 0))],
        out_shape=[jax.ShapeDtypeStruct((s, d), BF16), jax.ShapeDtypeStruct((s, n_main), BF16),
                   jax.ShapeDtypeStruct((s, n_ba), F32)],
        sem=("parallel", "arbitrary"))


QKV_W = 3 * HEADS * HEAD_DIM
Z_OFF, Z_W = 3072, 1024
GATE_OFF, GATE_W = 4096, 2048
A_OFF, A_W = 6144, 1536
N_MAIN = 7680
CONV_A = 512
BA_W = 128

L_BETA, L_G, L_EG, L_EKD, L_EGC = 0, 8, 16, 24, 32


def _softplus(z):
    e = jnp.exp(-jnp.abs(z))
    small = e * (1.0 - e * (0.5 - e * (1.0 / 3.0)))
    return jnp.maximum(z, 0.0) + jnp.where(e < 1e-3, small, jnp.log(1.0 + e))


def _tri(n, sgn, strict=False):
    i = lax.broadcasted_iota(jnp.int32, (n, n), 0)
    j = lax.broadcasted_iota(jnp.int32, (n, n), 1)
    dlt = (i - j) * sgn
    return (dlt > 0) if strict else (dlt >= 0)


def _scal_fwd(ba, alog, dtb, *, name, ts=512):
    s = ba.shape[0]

    def body(ba_ref, al_ref, dt_ref, o_ref):
        d = pl.program_id(0)
        sgn = 1 - 2 * d
        x = ba_ref[...]
        lane = lax.broadcasted_iota(jnp.int32, x.shape, 1)
        beta = _sigmoid(x)
        g = -jnp.exp(al_ref[0]) * _softplus(x + dt_ref[0])
        g = jnp.where((lane >= L_G) & (lane < L_EGC + 8), g, 0.0)
        ltri = jnp.where(_tri(CHUNK, sgn), 1.0, 0.0).astype(F32)
        for c in range(ts // CHUNK):
            rows = slice(c * CHUNK, (c + 1) * CHUNK)
            gc = _dot_exact(ltri, g[rows])
            g_end = jnp.where(d == 0, gc[CHUNK - 1:CHUNK], gc[0:1])
            ln = lane[rows]
            out = jnp.where(ln < L_G, beta[rows],
                  jnp.where(ln < L_EG, gc,
                  jnp.where(ln < L_EKD, jnp.exp(gc),
                  jnp.where(ln < L_EGC, jnp.exp(g_end - gc),
                  jnp.where(ln < L_EGC + 8, jnp.broadcast_to(jnp.exp(g_end), gc.shape), 0.0)))))
            o_ref[0, rows, :] = out

    return pl.pallas_call(
        body, name=name, grid=(2, s // ts),
        in_specs=[pl.BlockSpec((ts, BA_W), lambda d, i: (i, d)),
                  pl.BlockSpec((1, 1, BA_W), lambda d, i: (d, 0, 0)),
                  pl.BlockSpec((1, 1, BA_W), lambda d, i: (d, 0, 0))],
        out_specs=pl.BlockSpec((1, ts, BA_W), lambda d, i: (d, i, 0)),
        out_shape=jax.ShapeDtypeStruct((2, s, BA_W), F32),
        compiler_params=_params(("parallel", "parallel")),
    )(ba, alog, dtb)


def _scal_bwd(dscal, drow, ba, alog, dtb, *, name, ts=512):
    s = ba.shape[0]

    def body(ds_ref, dr_ref, ba_ref, al_ref, dt_ref, dba_ref, dal_ref, ddt_ref):
        d = pl.program_id(0)
        sgn = 1 - 2 * d
        x = ba_ref[...]
        lane = lax.broadcasted_iota(jnp.int32, x.shape, 1)
        in_g = (lane >= L_G) & (lane < L_G + 8)
        beta = _sigmoid(x)
        z = x + dt_ref[0]
        neg_a = -jnp.exp(al_ref[0])
        g = neg_a * _softplus(z)
        dsv = ds_ref[0]
        dgc = jnp.where(in_g, dsv + dr_ref[0], 0.0)
        utri = jnp.where(_tri(CHUNK, -sgn), 1.0, 0.0).astype(F32)
        dal = jnp.zeros((1, BA_W), F32)
        ddt = jnp.zeros((1, BA_W), F32)
        for c in range(ts // CHUNK):
            rows = slice(c * CHUNK, (c + 1) * CHUNK)
            dg = _dot_exact(utri, dgc[rows])
            dz = dg * neg_a * _sigmoid(z[rows])
            dal = dal + jnp.sum(dg * g[rows], axis=0, keepdims=True)
            ddt = ddt + jnp.sum(dz, axis=0, keepdims=True)
            b = beta[rows]
            out = jnp.where(lane[rows] < L_G, dsv[rows] * b * (1.0 - b), jnp.where(in_g[rows], dz, 0.0))
            dba_ref[rows, :] = out.astype(BF16)

        @pl.when(pl.program_id(1) == 0)
        def _():
            dal_ref[...] = jnp.zeros_like(dal_ref)
            ddt_ref[...] = jnp.zeros_like(ddt_ref)

        dal_ref[0] += dal
        ddt_ref[0] += ddt

    row3 = pl.BlockSpec((1, 1, BA_W), lambda d, i: (d, 0, 0))
    tok3 = pl.BlockSpec((1, ts, BA_W), lambda d, i: (d, i, 0))
    return pl.pallas_call(
        body, name=name, grid=(2, s // ts),
        in_specs=[tok3, tok3, pl.BlockSpec((ts, BA_W), lambda d, i: (i, d)), row3, row3],
        out_specs=[pl.BlockSpec((ts, BA_W), lambda d, i: (i, d)), row3, row3],
        out_shape=[jax.ShapeDtypeStruct((s, 2 * BA_W), BF16), jax.ShapeDtypeStruct((2, 1, BA_W), F32),
                   jax.ShapeDtypeStruct((2, 1, BA_W), F32)],
        compiler_params=_params(("arbitrary", "arbitrary")),
    )(dscal, drow, ba, alog, dtb)


HALO = 16


def _halo_specs(ts, width, col_block, n_rows, rows=HALO):
    r = ts // rows
    last = n_rows // rows - 1
    return [pl.BlockSpec((rows, width), lambda i: (jnp.maximum(i * r - 1, 0), col_block)),
            pl.BlockSpec((ts, width), lambda i: (i, col_block)),
            pl.BlockSpec((rows, width), lambda i: (jnp.minimum((i + 1) * r, last), col_block))]


def _fill_halo(dst_ref, prev_ref, cur_ref, next_ref, first, last, fn=lambda r: r[...].astype(F32)):
    h = prev_ref.shape[0]
    ts = cur_ref.shape[0]
    p = fn(prev_ref)
    n = fn(next_ref)
    dst_ref[0:h, :] = jnp.where(first, 0.0, p)
    dst_ref[h:h + ts, :] = fn(cur_ref)
    dst_ref[h + ts:h + ts + h, :] = jnp.where(last, 0.0, n)


def _dwconv_rows(src_ref, w, start, n_rows, cols):
    acc = w[0:1, :] * src_ref[start:start + n_rows, cols]
    for i in range(1, w.shape[0]):
        acc = acc + w[i:i + 1, :] * src_ref[start + i:start + i + n_rows, cols]
    return acc


def _l2norm_heads(act, scale):
    outs = []
    for hd in range(HEADS):
        seg = act[:, hd * HEAD_DIM:(hd + 1) * HEAD_DIM]
        outs.append(seg * (lax.rsqrt(jnp.sum(seg * seg, axis=-1, keepdims=True) + EPS) * scale))
    return jnp.concatenate(outs, axis=-1)


Q_SCALE = HEAD_DIM ** -0.5


def _conv_fwd(proj, conv_dn, conv_a, *, name, ts=256):
    s = proj.shape[0]
    hd = HEADS * HEAD_DIM
    nt = s // ts

    def body(qp_ref, qc_ref, qn_ref, ap_ref, ac_ref, an_ref, wdn_ref, wa_ref,
             q_ref, k_ref, v_ref, ya_ref, xs_ref, xa_ref):
        i = pl.program_id(0)
        first, last = i == 0, i == nt - 1
        _fill_halo(xs_ref, qp_ref, qc_ref, qn_ref, first, last)
        wdn = wdn_ref[...]
        for part, o_ref in enumerate((q_ref, k_ref, v_ref)):
            cols = slice(part * hd, (part + 1) * hd)
            pre = _dwconv_rows(xs_ref, wdn[:, cols], HALO - 2, ts, cols)
            act = pre * _sigmoid(pre)
            if part == 0:
                act = _l2norm_heads(act, Q_SCALE)
            elif part == 1:
                act = _l2norm_heads(act, 1.0)
            o_ref[...] = act
        cv = lambda r: r[:, CONV_A:2 * CONV_A].astype(F32) * r[:, 2 * CONV_A:].astype(F32)
        _fill_halo(xa_ref, ap_ref, ac_ref, an_ref, first, last, fn=cv)
        conv = _dwconv_rows(xa_ref, wa_ref[...], HALO - 1, ts, slice(0, CONV_A))
        ya_ref[...] = (ac_ref[:, 0:CONV_A].astype(F32) * conv).astype(BF16)

    tile = lambda w: pl.BlockSpec((ts, w), lambda i: (i, 0))
    return pl.pallas_call(
        body, name=name, grid=(nt,),
        in_specs=(_halo_specs(ts, QKV_W, 0, s) + _halo_specs(ts, A_W, A_OFF // A_W, s)
                  + [_resident(conv_dn.shape), _resident(conv_a.shape)]),
        out_specs=[tile(hd), tile(hd), tile(hd), tile(CONV_A)],
        out_shape=[jax.ShapeDtypeStruct((s, hd), F32)] * 3 + [jax.ShapeDtypeStruct((s, CONV_A), BF16)],
        scratch_shapes=[pltpu.VMEM((ts + 2 * HALO, QKV_W), F32), pltpu.VMEM((ts + 2 * HALO, CONV_A), F32)],
        compiler_params=_params(("parallel",)),
    )(proj, proj, proj, proj, proj, proj, conv_dn, conv_a)


def _chunk_of_step(d, c, n):
    return c + d * (n - 1 - 2 * c)


def _head_scalars(scv, grv, hd):
    col = lambda base: scv[:, base + hd:base + hd + 1]
    return (col(L_BETA), col(L_G), col(L_EG), col(L_EKD),
            scv[0:1, L_EGC + hd:L_EGC + hd + 1], grv[hd:hd + 1, :])


def _decay_matrix(gcol, grow, incl):
    return jnp.where(incl, jnp.exp(jnp.minimum(gcol - grow, 0.0)), 0.0)


INV_BASE = 8


def _unit_lower_inverse(a_m, top=None):
    n = a_m.shape[0]
    top = top or n
    i = lax.broadcasted_iota(jnp.int32, (n, n), 0)
    j = lax.broadcasted_iota(jnp.int32, (n, n), 1)

    def same_block(m):
        sh = int(math.log2(m))
        return jnp.right_shift(i, sh) == jnp.right_shift(j, sh)

    x = jnp.where(same_block(INV_BASE), -a_m, 0.0)
    t = jnp.where(i == j, 1.0, 0.0) + x
    p = x
    for _ in range(int(math.log2(INV_BASE)) - 1):
        p_b = p.astype(BF16)
        p = _dot(p_b, p_b)
        t = t + _dot(t.astype(BF16), p.astype(BF16))
    m = INV_BASE
    while m < top:
        join = jnp.where(same_block(2 * m) & jnp.logical_not(same_block(m)), a_m, 0.0)
        t_b = t.astype(BF16)
        t = t - _dot(_dot(t_b, join.astype(BF16)).astype(BF16), t_b)
        m *= 2
    return t


def _delta_fwd_per_head(q, k, v, scal, grow, *, name):
    s = q.shape[0]
    n = s // CHUNK
    hd_all = HEADS * HEAD_DIM

    def body(q_ref, k_ref, v_ref, sc_ref, gr_ref, o_ref, st_ref, t_ref, vn_ref, state):
        d = pl.program_id(0)
        sgn = 1 - 2 * d

        @pl.when(pl.program_id(1) == 0)
        def _():
            state[...] = jnp.zeros_like(state)

        incl = _tri(CHUNK, sgn)
        strict = _tri(CHUNK, sgn, strict=True)
        scv = sc_ref[0]
        grv = gr_ref[0, 0]
        for hd in range(HEADS):
            cols = slice(hd * HEAD_DIM, (hd + 1) * HEAD_DIM)
            qh, kh, vh = q_ref[:, cols], k_ref[:, cols], v_ref[:, cols]
            beta, gcol, eg, ekd, egc, grow_h = _head_scalars(scv, grv, hd)
            dm = _decay_matrix(gcol, grow_h, incl)
            k_b = kh.astype(BF16)
            kk = _dot_nt((kh * beta).astype(BF16), k_b)
            t = _unit_lower_inverse(jnp.where(strict, kk * dm, 0.0))
            p_m = jnp.where(incl, _dot_nt(qh.astype(BF16), k_b) * dm, 0.0)
            sh = state[hd]
            sh_b = sh.astype(BF16)
            st_ref[0, 0, hd] = sh_b
            r = vh - _dot((kh * eg).astype(BF16), sh_b)
            vn = _dot3(t, beta * r)
            vn_b = vn.astype(BF16)
            o_ref[0, :, cols] = _dot((qh * eg).astype(BF16), sh_b) + _dot(p_m.astype(BF16), vn_b)
            state[hd] = egc * sh + _dot_tn((kh * ekd).astype(BF16), vn_b)
            t_ref[0, 0, hd] = t
            vn_ref[0, :, cols] = vn_b

    tok = lambda d, c: (_chunk_of_step(d, c, n), 0)
    dtok = lambda d, c: (d, _chunk_of_step(d, c, n), 0)
    dchunk4 = lambda d, c: (d, _chunk_of_step(d, c, n), 0, 0)
    dchunk5 = lambda d, c: (d, _chunk_of_step(d, c, n), 0, 0, 0)
    return pl.pallas_call(
        body, name=name, grid=(2, n),
        in_specs=[pl.BlockSpec((CHUNK, hd_all), tok)] * 3
                 + [pl.BlockSpec((1, CHUNK, BA_W), dtok), pl.BlockSpec((1, 1, HEADS, CHUNK), dchunk4)],
        out_specs=[pl.BlockSpec((1, CHUNK, hd_all), dtok),
                   pl.BlockSpec((1, 1, HEADS, HEAD_DIM, HEAD_DIM), dchunk5),
                   pl.BlockSpec((1, 1, HEADS, CHUNK, CHUNK), dchunk5),
                   pl.BlockSpec((1, CHUNK, hd_all), dtok)],
        out_shape=[jax.ShapeDtypeStruct((2, s, hd_all), F32),
                   jax.ShapeDtypeStruct((2, n, HEADS, HEAD_DIM, HEAD_DIM), BF16),
                   jax.ShapeDtypeStruct((2, n, HEADS, CHUNK, CHUNK), F32),
                   jax.ShapeDtypeStruct((2, s, hd_all), BF16)],
        scratch_shapes=[pltpu.VMEM((HEADS, HEAD_DIM, HEAD_DIM), F32)],
        compiler_params=_params(("arbitrary", "arbitrary")),
    )(q, k, v, scal, grow)


def _delta_bwd_per_head(q, k, v, scal, grow, states, tinv, vn, do, *, name):
    s = q.shape[0]
    n = s // CHUNK
    hd_all = HEADS * HEAD_DIM

    def body(q_ref, k_ref, v_ref, sc_ref, gr_ref, st_ref, t_ref, vn_ref, do_ref,
             dq_ref, dk_ref, dv_ref, dsc_ref, dgr_ref, dstate):
        d = pl.program_id(0)
        sgn = 1 - 2 * d

        @pl.when(pl.program_id(1) == 0)
        def _():
            dstate[...] = jnp.zeros_like(dstate)

        incl = _tri(CHUNK, sgn)
        strict = _tri(CHUNK, sgn, strict=True)
        scv = sc_ref[0]
        grv = gr_ref[0, 0]
        lane = lax.broadcasted_iota(jnp.int32, (CHUNK, BA_W), 1)
        row = lax.broadcasted_iota(jnp.int32, (CHUNK, 1), 0)
        sub = lax.broadcasted_iota(jnp.int32, (HEADS, CHUNK), 0)
        end_row = jnp.where(d == 0, CHUNK - 1, 0)
        dsc_acc = jnp.zeros((CHUNK, BA_W), F32)
        dgr_acc = jnp.zeros((HEADS, CHUNK), F32)
        for hd in range(HEADS):
            cols = slice(hd * HEAD_DIM, (hd + 1) * HEAD_DIM)
            qh, kh, vh = q_ref[:, cols], k_ref[:, cols], v_ref[:, cols]
            beta, gcol, eg, ekd, egc, grow_h = _head_scalars(scv, grv, hd)
            dm = _decay_matrix(gcol, grow_h, incl)
            q_b, k_b = qh.astype(BF16), kh.astype(BF16)
            kb_b = (kh * beta).astype(BF16)
            kk = _dot_nt(kb_b, k_b)
            qk = _dot_nt(q_b, k_b)
            p_m = jnp.where(incl, qk * dm, 0.0)
            t = t_ref[0, 0, hd]
            vn_b = vn_ref[0, :, cols]
            sh_b = st_ref[0, 0, hd]
            dsp = dstate[hd]
            dsp_b = dsp.astype(BF16)
            do_b = do_ref[:, cols].astype(BF16)
            kg, qg, kd = kh * eg, qh * eg, kh * ekd
            kg_b, qg_b, kd_b = kg.astype(BF16), qg.astype(BF16), kd.astype(BF16)
            r = vh - _dot(kg_b, sh_b)
            dvn = _dot_tn(p_m.astype(BF16), do_b) + _dot(kd_b, dsp_b)
            db = _dot3(t, dvn, dot=_dot_tn)
            dr = db * beta
            dbeta = jnp.sum(db * r, axis=-1, keepdims=True)
            dr_b, db_b = dr.astype(BF16), db.astype(BF16)
            dkg = -_dot_nt(dr_b, sh_b)
            dqg = _dot_nt(do_b, sh_b)
            dkd = _dot_nt(vn_b, dsp_b)
            dpm = jnp.where(incl, _dot_nt(do_b, vn_b), 0.0) * dm
            dam = jnp.where(strict, -_dot_nt(db_b, vn_b), 0.0) * dm
            dpm_b, dam_b = dpm.astype(BF16), dam.astype(BF16)
            dkb = _dot(dam_b, k_b)
            dq_ref[0, :, cols] = dqg * eg + _dot(dpm_b, k_b)
            dk_ref[0, :, cols] = (dkg * eg + dkd * ekd + _dot_tn(dpm_b, q_b) + _dot_tn(dam_b, kb_b)
                                  + dkb * beta)
            dv_ref[0, :, cols] = dr
            dbeta = dbeta + jnp.sum(dkb * kh, axis=-1, keepdims=True)
            m = dpm * qk + dam * kk
            kd_term = jnp.sum(dkd * kd, axis=-1, keepdims=True)
            dgcol = (jnp.sum(dqg * qg, axis=-1, keepdims=True) + jnp.sum(dkg * kg, axis=-1, keepdims=True)
                     - kd_term + jnp.sum(m, axis=-1, keepdims=True))
            dg_end = jnp.sum(kd_term) + egc * jnp.sum(dsp * sh_b.astype(F32))
            dgcol = dgcol + jnp.where(row == end_row, dg_end, 0.0)
            dsc_acc = jnp.where(lane == L_BETA + hd, dbeta, dsc_acc)
            dsc_acc = jnp.where(lane == L_G + hd, dgcol, dsc_acc)
            dgr_acc = jnp.where(sub == hd, -jnp.sum(m, axis=0, keepdims=True), dgr_acc)
            dstate[hd] = _dot_tn(qg_b, do_b) + egc * dsp - _dot_tn(kg_b, dr_b)
        dsc_ref[0] = dsc_acc
        dgr_ref[0, 0] = dgr_acc

    step = lambda d, c: n - 1 - _chunk_of_step(d, c, n)
    tok = lambda d, c: (step(d, c), 0)
    dtok = lambda d, c: (d, step(d, c), 0)
    dchunk4 = lambda d, c: (d, step(d, c), 0, 0)
    dchunk5 = lambda d, c: (d, step(d, c), 0, 0, 0)
    tok_spec = pl.BlockSpec((CHUNK, hd_all), tok)
    dtok_spec = pl.BlockSpec((1, CHUNK, hd_all), dtok)
    return pl.pallas_call(
        body, name=name, grid=(2, n),
        in_specs=[tok_spec] * 3
                 + [pl.BlockSpec((1, CHUNK, BA_W), dtok), pl.BlockSpec((1, 1, HEADS, CHUNK), dchunk4),
                    pl.BlockSpec((1, 1, HEADS, HEAD_DIM, HEAD_DIM), dchunk5),
                    pl.BlockSpec((1, 1, HEADS, CHUNK, CHUNK), dchunk5), dtok_spec, tok_spec],
        out_specs=[dtok_spec] * 3
                  + [pl.BlockSpec((1, CHUNK, BA_W), dtok), pl.BlockSpec((1, 1, HEADS, CHUNK), dchunk4)],
        out_shape=[jax.ShapeDtypeStruct((2, s, hd_all), F32)] * 3
                  + [jax.ShapeDtypeStruct((2, s, BA_W), F32), jax.ShapeDtypeStruct((2, n, HEADS, CHUNK), F32)],
        scratch_shapes=[pltpu.VMEM((HEADS, HEAD_DIM, HEAD_DIM), F32)],
        compiler_params=_params(("arbitrary", "arbitrary")),
    )(q, k, v, scal, grow, states, tinv, vn, do)


GROUP = 4
GROWS = GROUP * CHUNK
N_GROUPS = HEADS // GROUP


def _stack(parts):
    return jnp.concatenate(parts, axis=0)


M_INCL, M_STRICT, M_EYE, M_BASE, M_JOIN = 0, 1, 2, 3, 4
JOIN_SIZES = (16, 32, 64)
N_MASKS = M_JOIN + len(JOIN_SIZES)


def _write_group_masks(mask_ref, sgn, n_masks):
    i = lax.broadcasted_iota(jnp.int32, (GROWS, GROWS), 0)
    j = lax.broadcasted_iota(jnp.int32, (GROWS, GROWS), 1)
    same = lambda m: jnp.right_shift(i, int(math.log2(m))) == jnp.right_shift(j, int(math.log2(m)))
    dlt = (i - j) * sgn
    one = lambda cond: jnp.where(cond, 1.0, 0.0).astype(F32)
    mask_ref[M_INCL] = one(same(CHUNK) & (dlt >= 0))
    mask_ref[M_STRICT] = one(same(CHUNK) & (dlt > 0))
    if n_masks > M_EYE:
        mask_ref[M_EYE] = one(i == j)
        mask_ref[M_BASE] = one(same(INV_BASE))
        for lvl, m in enumerate(JOIN_SIZES):
            mask_ref[M_JOIN + lvl] = one(same(m) & jnp.logical_not(same(m // 2)))


def _group_decay(gcol, grow, mask_ref):
    return jnp.exp(jnp.minimum(gcol - grow, 0.0)) * mask_ref[M_INCL]


def _block_inverse(a_m, mask_ref):
    x = -(a_m * mask_ref[M_BASE])
    t = mask_ref[M_EYE] + x
    p = x
    for _ in range(int(math.log2(INV_BASE)) - 1):
        p_b = p.astype(BF16)
        p = _dot(p_b, p_b)
        t = t + _dot(t.astype(BF16), p.astype(BF16))
    for lvl in range(len(JOIN_SIZES)):
        t_b = t.astype(BF16)
        t = t - _dot(_dot(t_b, (a_m * mask_ref[M_JOIN + lvl]).astype(BF16)).astype(BF16), t_b)
    return t


def _group_operands(q_ref, k_ref, v_ref, scv, grp):
    heads = [GROUP * grp + t for t in range(GROUP)]
    tiles = lambda ref: [ref[:, h * HEAD_DIM:(h + 1) * HEAD_DIM] for h in heads]
    col = lambda base: [scv[:, base + h:base + h + 1] for h in heads]
    egc = [scv[0:1, L_EGC + h:L_EGC + h + 1] for h in heads]
    return heads, tiles(q_ref), tiles(k_ref), tiles(v_ref), col(L_BETA), col(L_G), col(L_EG), col(L_EKD), egc


def _delta_fwd(q, k, v, scal, grow, *, name):
    s = q.shape[0]
    n = s // CHUNK
    hd_all = HEADS * HEAD_DIM

    def one_direction(q_ref, k_ref, v_ref, sc_ref, gr_ref, o_ref, st_ref, t_ref, vn_ref, state, mask_ref):
        scv = sc_ref[0]
        for grp in range(N_GROUPS):
            heads, qs, ks, vs, beta, gcol, eg, ekd, egc = _group_operands(q_ref, k_ref, v_ref, scv, grp)
            dm = _group_decay(_stack(gcol), gr_ref[0, 0, grp:grp + 1, :], mask_ref)
            k_b = _stack(ks).astype(BF16)
            kk = _dot_nt(_stack([ks[t] * beta[t] for t in range(GROUP)]).astype(BF16), k_b)
            tinv = _block_inverse(kk * dm * mask_ref[M_STRICT], mask_ref).astype(BF16)
            t_ref[0, grp] = tinv
            p_m = _dot_nt(_stack(qs).astype(BF16), k_b) * dm
            sh, sh_b, br = [], [], []
            for t, h in enumerate(heads):
                sh.append(state[h])
                sh_b.append(sh[t].astype(BF16))
                st_ref[0, h] = sh_b[t]
                br.append(beta[t] * (vs[t] - _dot((ks[t] * eg[t]).astype(BF16), sh_b[t])))
            vn_b = _dot(tinv, _stack(br).astype(BF16)).astype(BF16)
            o_intra = _dot(p_m.astype(BF16), vn_b)
            for t, h in enumerate(heads):
                rows = slice(t * CHUNK, (t + 1) * CHUNK)
                cols = slice(h * HEAD_DIM, (h + 1) * HEAD_DIM)
                o_ref[:, cols] = _dot((qs[t] * eg[t]).astype(BF16), sh_b[t]) + o_intra[rows]
                state[h] = egc[t] * sh[t] + _dot_tn((ks[t] * ekd[t]).astype(BF16), vn_b[rows])
                vn_ref[:, cols] = vn_b[rows]

    def body(*refs):
        ins, outs, (state, mask_ref) = refs[:10], refs[10:18], refs[18:]

        @pl.when(pl.program_id(0) == 0)
        def _():
            state[...] = jnp.zeros_like(state)
            for d in range(2):
                _write_group_masks(mask_ref.at[d], 1 - 2 * d, N_MASKS)

        for d in range(2):
            one_direction(*ins[5 * d:5 * d + 5], *outs[4 * d:4 * d + 4], state.at[d], mask_ref.at[d])

    at = [lambda c: c, lambda c: n - 1 - c]
    in_specs, out_specs = [], []
    for d in range(2):
        tok = pl.BlockSpec((CHUNK, hd_all), lambda c, d=d: (at[d](c), 0))
        in_specs += [tok] * 3 + [pl.BlockSpec((1, CHUNK, BA_W), lambda c, d=d: (d, at[d](c), 0)),
                                 pl.BlockSpec((1, 1, N_GROUPS, GROWS), lambda c, d=d: (d, at[d](c), 0, 0))]
        out_specs += [tok, pl.BlockSpec((1, HEADS, HEAD_DIM, HEAD_DIM), lambda c, d=d: (at[d](c), 0, 0, 0)),
                      pl.BlockSpec((1, N_GROUPS, GROWS, GROWS), lambda c, d=d: (at[d](c), 0, 0, 0)), tok]
    per_dir_shape = [jax.ShapeDtypeStruct((s, hd_all), F32),
                     jax.ShapeDtypeStruct((n, HEADS, HEAD_DIM, HEAD_DIM), BF16),
                     jax.ShapeDtypeStruct((n, N_GROUPS, GROWS, GROWS), BF16),
                     jax.ShapeDtypeStruct((s, hd_all), BF16)]
    outs = pl.pallas_call(
        body, name=name, grid=(n,), in_specs=in_specs, out_specs=out_specs, out_shape=per_dir_shape * 2,
        scratch_shapes=[pltpu.VMEM((2, HEADS, HEAD_DIM, HEAD_DIM), F32),
                        pltpu.VMEM((2, N_MASKS, GROWS, GROWS), F32)],
        compiler_params=_params(("arbitrary",)),
    )(*([q, k, v, scal, grow] * 2))
    return tuple((outs[i], outs[4 + i]) for i in range(4))


def _delta_bwd(q, k, v, scal, grow, states, tinv, vn, do, *, name, carry=None):
    s = q.shape[0]
    n = s // CHUNK
    hd_all = HEADS * HEAD_DIM

    def one_direction(d, q_ref, k_ref, v_ref, sc_ref, gr_ref, st_ref, t_ref, vn_ref, do_ref,
                      dq_ref, dk_ref, dv_ref, dsc_ref, dgr_ref, dstate, mask_ref):
        scv = sc_ref[0]
        lane = lax.broadcasted_iota(jnp.int32, (CHUNK, BA_W), 1)
        row = lax.broadcasted_iota(jnp.int32, (CHUNK, 1), 0)
        end_row = CHUNK - 1 if d == 0 else 0
        dsc_acc = jnp.zeros((CHUNK, BA_W), F32)
        for grp in range(N_GROUPS):
            heads, qs, ks, vs, beta, gcol, eg, ekd, egc = _group_operands(q_ref, k_ref, v_ref, scv, grp)
            dm = _group_decay(_stack(gcol), gr_ref[0, 0, grp:grp + 1, :], mask_ref)
            dm_strict = dm * mask_ref[M_STRICT]
            beta_st, eg_st, ekd_st = _stack(beta), _stack(eg), _stack(ekd)
            q_st, k_st = _stack(qs), _stack(ks)
            q_b, k_b = q_st.astype(BF16), k_st.astype(BF16)
            kb_b = (k_st * beta_st).astype(BF16)
            kk = _dot_nt(kb_b, k_b)
            qk = _dot_nt(q_b, k_b)
            p_m = qk * dm
            kg_st, qg_st, kd_st = k_st * eg_st, q_st * eg_st, k_st * ekd_st
            kg_b, qg_b, kd_b = kg_st.astype(BF16), qg_st.astype(BF16), kd_st.astype(BF16)
            tok_cols = [slice(h * HEAD_DIM, (h + 1) * HEAD_DIM) for h in heads]
            grp_rows = [slice(t * CHUNK, (t + 1) * CHUNK) for t in range(GROUP)]
            vn_b = _stack([vn_ref[:, c] for c in tok_cols])
            do_b = _stack([do_ref[:, c] for c in tok_cols]).astype(BF16)
            sh_b = [st_ref[0, h] for h in heads]
            dsp = [dstate[h] for h in heads]
            dsp_b = [x.astype(BF16) for x in dsp]
            r_st = _stack([vs[t] - _dot(kg_b[grp_rows[t]], sh_b[t]) for t in range(GROUP)])
            dvn = _dot_tn(p_m.astype(BF16), do_b) + _stack(
                [_dot(kd_b[grp_rows[t]], dsp_b[t]) for t in range(GROUP)])
            db = _dot_tn(t_ref[0, grp], dvn.astype(BF16))
            dr = db * beta_st
            dbeta = jnp.sum(db * r_st, axis=-1, keepdims=True)
            dr_b, db_b = dr.astype(BF16), db.astype(BF16)
            dkg = -_stack([_dot_nt(dr_b[grp_rows[t]], sh_b[t]) for t in range(GROUP)])
            dqg = _stack([_dot_nt(do_b[grp_rows[t]], sh_b[t]) for t in range(GROUP)])
            dkd = _stack([_dot_nt(vn_b[grp_rows[t]], dsp_b[t]) for t in range(GROUP)])
            dpm = _dot_nt(do_b, vn_b) * dm
            dam = -_dot_nt(db_b, vn_b) * dm_strict
            dpm_b, dam_b = dpm.astype(BF16), dam.astype(BF16)
            dkb = _dot(dam_b, k_b)
            dq_st = dqg * eg_st + _dot(dpm_b, k_b)
            dk_st = (dkg * eg_st + dkd * ekd_st + _dot_tn(dpm_b, q_b) + _dot_tn(dam_b, kb_b) + dkb * beta_st)
            dbeta = dbeta + jnp.sum(dkb * k_st, axis=-1, keepdims=True)
            m = dpm * qk + dam * kk
            kd_term = jnp.sum(dkd * kd_st, axis=-1, keepdims=True)
            dgcol = (jnp.sum(dqg * qg_st, axis=-1, keepdims=True) + jnp.sum(dkg * kg_st, axis=-1, keepdims=True)
                     - kd_term + jnp.sum(m, axis=-1, keepdims=True))
            dgr_ref[0, grp:grp + 1, :] = -jnp.sum(m, axis=0, keepdims=True)
            for t, h in enumerate(heads):
                rows, cols = grp_rows[t], tok_cols[t]
                dq_ref[:, cols] = dq_st[rows]
                dk_ref[:, cols] = dk_st[rows]
                dv_ref[:, cols] = dr[rows]
                dg_end = jnp.sum(kd_term[rows]) + egc[t] * jnp.sum(dsp[t] * sh_b[t].astype(F32))
                dgcol_h = dgcol[rows] + jnp.where(row == end_row, dg_end, 0.0)
                dsc_acc = jnp.where(lane == L_BETA + h, dbeta[rows], dsc_acc)
                dsc_acc = jnp.where(lane == L_G + h, dgcol_h, dsc_acc)
                dstate[h] = (_dot_tn(qg_b[rows], do_b[rows]) + egc[t] * dsp[t]
                             - _dot_tn(kg_b[rows], dr_b[rows]))
        dsc_ref[...] = dsc_acc

    def body(*refs):
        ins, outs, (dstate, mask_ref) = refs[:18], refs[18:28], refs[28:]

        @pl.when(pl.program_id(0) == 0)
        def _():
            dstate[...] = jnp.zeros_like(dstate)
            for d in range(2):
                _write_group_masks(mask_ref.at[d], 1 - 2 * d, M_EYE)

        for d in range(2):
            one_direction(d, *ins[9 * d:9 * d + 9], *outs[5 * d:5 * d + 5], dstate.at[d], mask_ref.at[d])

    at = [lambda c: n - 1 - c, lambda c: c]
    in_specs, out_specs, args = [], [], []
    for d in range(2):
        tok = pl.BlockSpec((CHUNK, hd_all), lambda c, d=d: (at[d](c), 0))
        in_specs += [tok] * 3 + [pl.BlockSpec((1, CHUNK, BA_W), lambda c, d=d: (d, at[d](c), 0)),
                                 pl.BlockSpec((1, 1, N_GROUPS, GROWS), lambda c, d=d: (d, at[d](c), 0, 0)),
                                 pl.BlockSpec((1, HEADS, HEAD_DIM, HEAD_DIM), lambda c, d=d: (at[d](c), 0, 0, 0)),
                                 pl.BlockSpec((1, N_GROUPS, GROWS, GROWS), lambda c, d=d: (at[d](c), 0, 0, 0)),
                                 tok, tok]
        args += [q, k, v, scal, grow, states[d], tinv[d], vn[d], do]
        out_specs += [tok] * 3 + [pl.BlockSpec((CHUNK, BA_W), lambda c, d=d: (at[d](c), 0)),
                                  pl.BlockSpec((1, N_GROUPS, GROWS), lambda c, d=d: (at[d](c), 0, 0))]
    per_dir_shape = ([jax.ShapeDtypeStruct((s, hd_all), F32)] * 3
                     + [jax.ShapeDtypeStruct((s, BA_W), F32), jax.ShapeDtypeStruct((n, N_GROUPS, GROWS), F32)])
    res = _launch(
        body, carry, tuple(args), name=name, grid=(n,), in_specs=in_specs, out_specs=out_specs,
        out_shape=per_dir_shape * 2,
        scratch_shapes=[pltpu.VMEM((2, HEADS, HEAD_DIM, HEAD_DIM), F32), pltpu.VMEM((2, M_EYE, GROWS, GROWS), F32)],
        sem=("arbitrary",))
    outs, got = res if carry is not None else (res, None)
    paired = tuple((outs[i], outs[5 + i]) for i in range(5))
    return paired if carry is None else (paired, got)


def _gate_norm_fwd(o2, proj, dnw, *, name, ts=512):
    s = o2[0].shape[0]
    hd_all = HEADS * HEAD_DIM

    def body(of_ref, ob_ref, z_ref, w_ref, y_ref):
        w = w_ref[...]
        for hd in range(HEADS):
            cols = slice(hd * HEAD_DIM, (hd + 1) * HEAD_DIM)
            seg = of_ref[:, cols] + ob_ref[:, cols]
            r = lax.rsqrt(jnp.mean(seg * seg, axis=-1, keepdims=True) + EPS)
            z = z_ref[:, cols].astype(F32)
            y_ref[:, cols] = ((seg * r * w) * (z * _sigmoid(z))).astype(BF16)

    tile = pl.BlockSpec((ts, hd_all), lambda i: (i, 0))
    return pl.pallas_call(
        body, name=name, grid=(s // ts,),
        in_specs=[tile, tile, pl.BlockSpec((ts, Z_W), lambda i: (i, Z_OFF // Z_W)), _row(HEAD_DIM)],
        out_specs=tile,
        out_shape=jax.ShapeDtypeStruct((s, hd_all), BF16),
        compiler_params=_params(("parallel",)),
    )(o2[0], o2[1], proj, dnw)


def _gate_norm_bwd(dyb, o2, proj, dnw, *, name, ts=512):
    s = o2[0].shape[0]
    hd_all = HEADS * HEAD_DIM

    def body(dy_ref, of_ref, ob_ref, z_ref, w_ref, do_ref, dz_ref, dw_ref):
        w = w_ref[...]
        dw = jnp.zeros((1, HEAD_DIM), F32)
        for hd in range(HEADS):
            cols = slice(hd * HEAD_DIM, (hd + 1) * HEAD_DIM)
            seg = of_ref[:, cols] + ob_ref[:, cols]
            r = lax.rsqrt(jnp.mean(seg * seg, axis=-1, keepdims=True) + EPS)
            xhat = seg * r
            z = z_ref[:, cols].astype(F32)
            sg = _sigmoid(z)
            dy = dy_ref[:, cols]
            dnrm = dy * (z * sg)
            dz_ref[:, cols] = (dy * (xhat * w) * (sg * (1.0 + z * (1.0 - sg)))).astype(BF16)
            dw = dw + jnp.sum(dnrm * xhat, axis=0, keepdims=True)
            dxhat = dnrm * w
            do_ref[:, cols] = r * (dxhat - xhat * jnp.mean(dxhat * xhat, axis=-1, keepdims=True))

        @pl.when(pl.program_id(0) == 0)
        def _():
            dw_ref[...] = jnp.zeros_like(dw_ref)

        dw_ref[...] += dw

    tile = pl.BlockSpec((ts, hd_all), lambda i: (i, 0))
    return pl.pallas_call(
        body, name=name, grid=(s // ts,),
        in_specs=[tile, tile, tile, pl.BlockSpec((ts, Z_W), lambda i: (i, Z_OFF // Z_W)), _row(HEAD_DIM)],
        out_specs=[tile, tile, _row(HEAD_DIM)],
        out_shape=[jax.ShapeDtypeStruct((s, hd_all), F32), jax.ShapeDtypeStruct((s, hd_all), BF16),
                   jax.ShapeDtypeStruct((1, HEAD_DIM), F32)],
        compiler_params=_params(("arbitrary",)),
    )(dyb, o2[0], o2[1], proj, dnw)


def _merge_fwd(ya, yb, proj, wa, wb, wo, h, g, *, name, ts=512):
    s, d = h.shape

    def body(ya_ref, yb_ref, gt_ref, wa_ref, wb_ref, wo_ref, h_ref, g_ref, pa_ref, pb_ref, mix_ref, ho_ref):
        pa = _dot(ya_ref[...], wa_ref[...])
        pb = _dot(yb_ref[...], wb_ref[...])
        pa_ref[...] = pa.astype(BF16)
        pb_ref[...] = pb.astype(BF16)
        merged = (_sigmoid(gt_ref[:, :d].astype(F32)) * pa + _sigmoid(gt_ref[:, d:].astype(F32)) * pb)
        mix = _dot(merged.astype(BF16), wo_ref[...])
        mix_ref[...] = mix.astype(BF16)
        ho_ref[...] = h_ref[...] + g_ref[...] * mix

    tile = pl.BlockSpec((ts, d), lambda i: (i, 0))
    return pl.pallas_call(
        body, name=name, grid=(s // ts,),
        in_specs=[pl.BlockSpec((ts, CONV_A), lambda i: (i, 0)), tile,
                  pl.BlockSpec((ts, GATE_W), lambda i: (i, GATE_OFF // GATE_W)),
                  _resident(wa.shape), _resident(wb.shape), _resident(wo.shape), tile, _row(d)],
        out_specs=[tile, tile, tile, tile],
        out_shape=[jax.ShapeDtypeStruct((s, d), BF16)] * 3 + [jax.ShapeDtypeStruct((s, d), F32)],
        compiler_params=_params(("parallel",)),
    )(ya, yb, proj, wa, wb, wo, h, g)


def _merge_bwd(dh, g, mix, pa, pb, proj, wa, wb, wo, *, name, ts=256):
    s, d = dh.shape

    def body(dh_ref, g_ref, mix_ref, pa_ref, pb_ref, gt_ref, wa_ref, wb_ref, wo_ref,
             dmix_ref, mg_ref, dpa_ref, dpb_ref, dgt_ref, dya_ref, dyb_ref, dg_ref):
        dh_v = dh_ref[...]
        dmix = (g_ref[...] * dh_v).astype(BF16)
        dmix_ref[...] = dmix

        @pl.when(pl.program_id(0) == 0)
        def _():
            dg_ref[...] = jnp.zeros_like(dg_ref)

        dg_ref[...] += jnp.sum(dh_v * mix_ref[...].astype(F32), axis=0, keepdims=True)
        dmerged = _dot_nt(dmix, wo_ref[...])
        pa = pa_ref[...].astype(F32)
        pb = pb_ref[...].astype(F32)
        sa = _sigmoid(gt_ref[:, :d].astype(F32))
        sb = _sigmoid(gt_ref[:, d:].astype(F32))
        mg_ref[...] = (sa * pa + sb * pb).astype(BF16)
        dpa = (dmerged * sa).astype(BF16)
        dpb = (dmerged * sb).astype(BF16)
        dpa_ref[...] = dpa
        dpb_ref[...] = dpb
        dgt_ref[:, :d] = (dmerged * pa * sa * (1.0 - sa)).astype(BF16)
        dgt_ref[:, d:] = (dmerged * pb * sb * (1.0 - sb)).astype(BF16)
        dya_ref[...] = _dot_nt(dpa, wa_ref[...])
        dyb_ref[...] = _dot_nt(dpb, wb_ref[...])

    tile = pl.BlockSpec((ts, d), lambda i: (i, 0))
    return pl.pallas_call(
        body, name=name, grid=(s // ts,),
        in_specs=[tile, _row(d), tile, tile, tile,
                  pl.BlockSpec((ts, GATE_W), lambda i: (i, GATE_OFF // GATE_W)),
                  _resident(wa.shape), _resident(wb.shape), _resident(wo.shape)],
        out_specs=[tile, tile, tile, tile, pl.BlockSpec((ts, GATE_W), lambda i: (i, 0)),
                   pl.BlockSpec((ts, CONV_A), lambda i: (i, 0)), tile, _row(d)],
        out_shape=[jax.ShapeDtypeStruct((s, d), BF16)] * 4
                  + [jax.ShapeDtypeStruct((s, GATE_W), BF16), jax.ShapeDtypeStruct((s, CONV_A), F32),
                     jax.ShapeDtypeStruct((s, d), F32), jax.ShapeDtypeStruct((1, d), F32)],
        compiler_params=_params(("arbitrary",)),
    )(dh, g, mix, pa, pb, proj, wa, wb, wo)


def _final_fwd_bwd(h, nw, target, *, name, ts=512):
    s, d = h.shape

    def body(h_ref, nw_ref, t_ref, loss_ref, dh_ref, dnw_ref):
        x = h_ref[...]
        w = nw_ref[...]
        r = lax.rsqrt(jnp.mean(x * x, axis=-1, keepdims=True) + EPS)
        xhat = x * r
        e = xhat * w - t_ref[...]
        part = 0.5 * jnp.sum(jnp.mean(e * e, axis=-1, keepdims=True))
        dy = e * (1.0 / d)
        dxhat = dy * w
        dh_ref[...] = r * (dxhat - xhat * jnp.mean(dxhat * xhat, axis=-1, keepdims=True))

        @pl.when(pl.program_id(0) == 0)
        def _():
            loss_ref[...] = jnp.zeros_like(loss_ref)
            dnw_ref[...] = jnp.zeros_like(dnw_ref)

        loss_ref[...] += jnp.broadcast_to(part, loss_ref.shape)
        dnw_ref[...] += jnp.sum(dy * xhat, axis=0, keepdims=True)

    tile = pl.BlockSpec((ts, d), lambda i: (i, 0))
    return pl.pallas_call(
        body, name=name, grid=(s // ts,),
        in_specs=[tile, _row(d), tile],
        out_specs=[_row(128), tile, _row(d)],
        out_shape=[jax.ShapeDtypeStruct((1, 128), F32), jax.ShapeDtypeStruct((s, d), F32),
                   jax.ShapeDtypeStruct((1, d), F32)],
        compiler_params=_params(("arbitrary",)),
    )(h, nw, target)


EXT = 8


def _l2norm_heads_bwd(act, dout, scale):
    outs = []
    for hd in range(HEADS):
        cols = slice(hd * HEAD_DIM, (hd + 1) * HEAD_DIM)
        seg = act[:, cols]
        nrm = lax.rsqrt(jnp.sum(seg * seg, axis=-1, keepdims=True) + EPS)
        yhat = seg * nrm
        dsg = dout[:, cols]
        outs.append((scale * nrm) * (dsg - yhat * jnp.sum(yhat * dsg, axis=-1, keepdims=True)))
    return jnp.concatenate(outs, axis=-1)


def _conv_bwd(dq2, dk2, dv2, dya, proj, conv_dn, conv_a, *, name, ts=256, carry=None):
    s = proj.shape[0]
    hd = HEADS * HEAD_DIM
    nt = s // ts
    te = ts + 2 * EXT
    kdn, ka = conv_dn.shape[0], conv_a.shape[0]

    def body(*refs):
        (qp_ref, qc_ref, qn_ref, ap_ref, ac_ref, an_ref) = refs[0:6]
        d3 = refs[6:24]
        (yp_ref, yc_ref, yn_ref, wdn_ref, wa_ref) = refs[24:29]
        (dqkv_ref, da_ref, dwdn_ref, dwa_ref) = refs[29:33]
        xs_ref, dps_ref, xa_ref, dca_ref = refs[33:37]
        i = pl.program_id(0)
        first, last = i == 0, i == nt - 1

        @pl.when(first)
        def _():
            dwdn_ref[...] = jnp.zeros_like(dwdn_ref)
            dwa_ref[...] = jnp.zeros_like(dwa_ref)

        rowe = lax.broadcasted_iota(jnp.int32, (te, 1), 0)
        inside = ~((first & (rowe < EXT)) | (last & (rowe >= EXT + ts)))
        _fill_halo(xs_ref, qp_ref, qc_ref, qn_ref, first, last)
        wdn = wdn_ref[...]
        for part in range(3):
            cols = slice(part * hd, (part + 1) * hd)
            pre = _dwconv_rows(xs_ref, wdn[:, cols], HALO - EXT - 2, te, cols)
            sg = _sigmoid(pre)
            act = pre * sg
            pf, cf, nf, pb, cb, nb = d3[6 * part:6 * part + 6]
            dout = jnp.concatenate([pf[...] + pb[...], cf[...] + cb[...], nf[...] + nb[...]], axis=0)
            if part == 0:
                dact = _l2norm_heads_bwd(act, dout, Q_SCALE)
            elif part == 1:
                dact = _l2norm_heads_bwd(act, dout, 1.0)
            else:
                dact = dout
            dpre = jnp.where(inside, dact * (sg * (1.0 + pre * (1.0 - sg))), 0.0)
            dps_ref[:, cols] = dpre
            acc = wdn[0:1, cols] * dps_ref[EXT + 2:EXT + 2 + ts, cols]
            for tap in range(1, kdn):
                acc = acc + wdn[tap:tap + 1, cols] * dps_ref[EXT + 2 - tap:EXT + 2 - tap + ts, cols]
            dqkv_ref[:, cols] = acc.astype(BF16)
            dcur = dps_ref[EXT:EXT + ts, cols]
            for tap in range(kdn):
                dwdn_ref[tap:tap + 1, cols] += jnp.sum(
                    dcur * xs_ref[HALO - 2 + tap:HALO - 2 + tap + ts, cols], axis=0, keepdims=True)

        cv = lambda r: r[:, CONV_A:2 * CONV_A].astype(F32) * r[:, 2 * CONV_A:].astype(F32)
        _fill_halo(xa_ref, ap_ref, ac_ref, an_ref, first, last, fn=cv)
        wa = wa_ref[...]
        gate_b = jnp.concatenate([ap_ref[HALO - EXT:, 0:CONV_A], ac_ref[:, 0:CONV_A], an_ref[0:EXT, 0:CONV_A]],
                                 axis=0).astype(F32)
        dya_e = jnp.concatenate([yp_ref[...], yc_ref[...], yn_ref[...]], axis=0)
        dca_ref[...] = jnp.where(inside, dya_e * gate_b, 0.0)
        conv = _dwconv_rows(xa_ref, wa, HALO - 1, ts, slice(0, CONV_A))
        acc = wa[0:1, :] * dca_ref[EXT + 1:EXT + 1 + ts, :]
        for tap in range(1, ka):
            acc = acc + wa[tap:tap + 1, :] * dca_ref[EXT + 1 - tap:EXT + 1 - tap + ts, :]
        gc = ac_ref[:, CONV_A:2 * CONV_A].astype(F32)
        val = ac_ref[:, 2 * CONV_A:].astype(F32)
        da_ref[:, 0:CONV_A] = (yc_ref[...] * conv).astype(BF16)
        da_ref[:, CONV_A:2 * CONV_A] = (acc * val).astype(BF16)
        da_ref[:, 2 * CONV_A:] = (acc * gc).astype(BF16)
        dcur = dca_ref[EXT:EXT + ts, :]
        for tap in range(ka):
            dwa_ref[tap:tap + 1, :] += jnp.sum(
                dcur * xa_ref[HALO - 1 + tap:HALO - 1 + tap + ts, :], axis=0, keepdims=True)

    cot = [arr for pair in (dq2, dk2, dv2) for arr in pair for _ in range(3)]
    return _launch(
        body, carry, (proj, proj, proj, proj, proj, proj, *cot, dya, dya, dya, conv_dn, conv_a),
        name=name, grid=(nt,),
        in_specs=(_halo_specs(ts, QKV_W, 0, s) + _halo_specs(ts, A_W, A_OFF // A_W, s)
                  + _halo_specs(ts, hd, 0, s, rows=EXT) * 6 + _halo_specs(ts, CONV_A, 0, s, rows=EXT)
                  + [_resident(conv_dn.shape), _resident(conv_a.shape)]),
        out_specs=[pl.BlockSpec((ts, QKV_W), lambda i: (i, 0)), pl.BlockSpec((ts, A_W), lambda i: (i, 0)),
                   pl.BlockSpec((8, QKV_W), lambda i: (0, 0)), pl.BlockSpec((8, CONV_A), lambda i: (0, 0))],
        out_shape=[jax.ShapeDtypeStruct((s, QKV_W), BF16), jax.ShapeDtypeStruct((s, A_W), BF16),
                   jax.ShapeDtypeStruct((8, QKV_W), F32), jax.ShapeDtypeStruct((8, CONV_A), F32)],
        scratch_shapes=[pltpu.VMEM((ts + 2 * HALO, QKV_W), F32), pltpu.VMEM((te, QKV_W), F32),
                        pltpu.VMEM((ts + 2 * HALO, CONV_A), F32), pltpu.VMEM((te, CONV_A), F32)],
        sem=("arbitrary",))


IN_A = (0, 1536)
IN_QKV = (1536, 4608)
IN_Z = (4608, 5632)
IN_BA = 5632
IN_GATE = (5664, 7712)
IN_COLS = 7712
G_REPL = 4


def _split_w_in(w_in):
    sl = lambda ab: w_in[:, ab[0]:ab[1]]
    w_main = jnp.concatenate([sl(IN_QKV), sl(IN_Z), sl(IN_GATE), sl(IN_A)], axis=1)
    blocks = []
    for d in range(2):
        beta = w_in[:, IN_BA + 8 * d:IN_BA + 8 * d + 8]
        alpha = w_in[:, IN_BA + 16 + 8 * d:IN_BA + 24 + 8 * d]
        pad = jnp.zeros((w_in.shape[0], BA_W - 8 - 8 * G_REPL), w_in.dtype)
        blocks += [beta] + [alpha] * G_REPL + [pad]
    return w_main, jnp.concatenate(blocks, axis=1)


def _merge_dw_in(dw_qkv, dw_z, dw_gate, dw_a, dw_ba):
    ba = [dw_ba[:, 0:8], dw_ba[:, BA_W:BA_W + 8], dw_ba[:, 8:16], dw_ba[:, BA_W + 8:BA_W + 16]]
    return jnp.concatenate([dw_a, dw_qkv, dw_z] + ba + [dw_gate], axis=1)


def _decay_rows(a_log_fwd, dt_bias_fwd, a_log_bwd, dt_bias_bwd):
    def rows(f, b):
        out = []
        for vec in (f, b):
            vec = vec.reshape(HEADS)
            out.append(jnp.concatenate([jnp.zeros((8,), F32)] + [vec] * G_REPL
                                       + [jnp.zeros((BA_W - 8 - 8 * G_REPL,), F32)])[None])
        return jnp.stack(out)
    return rows(a_log_fwd, a_log_bwd), rows(dt_bias_fwd, dt_bias_bwd)


def _local_step(x, target, mod9, wt, comm):
    s, d = x.shape
    n = s // CHUNK
    wt = dict(wt)
    sh1, sc1, g1, sh2, sc2, g2, sh3, sc3, g3 = [mod9[i:i + 1] for i in range(9)]
    alog, dtb = _decay_rows(wt["a_log_fwd"], wt["dt_bias_fwd"], wt["a_log_bwd"], wt["dt_bias_bwd"])

    (u1, a1, b1, f1), got = comm.gather(
        ["w_ffn1_down", "w_in"],
        lambda c: _ffn_up_fwd(x, wt["norm_ffn1"], sc1, sh1, wt["w_ffn1_up"], name="ffn1_up", carry=c))
    wt.update(got)
    w_main, w_ba = _split_w_in(wt["w_in"])
    y1, h1 = _ffn_down_fwd(f1, wt["w_ffn1_down"], x, g1, name="ffn1_down")
    (u2, proj, ba), got = comm.gather(
        ["w_a_out", "w_b_out", "w_out", "w_ffn2_up", "w_ffn2_down"],
        lambda c: _in_proj_fwd(h1, wt["norm_mix"], sc2, sh2, w_main, w_ba, name="in_proj", carry=c))
    wt.update(got)
    scal = _scal_fwd(ba, alog, dtb, name="scal_fwd")
    grow = scal[:, :, L_G:L_G + 8].reshape(2, n, CHUNK, HEADS).transpose(0, 1, 3, 2).reshape(
        2, n, N_GROUPS, GROWS)
    q, k, v, ya = _conv_fwd(proj, wt["conv_dn"], wt["conv_a"], name="conv_fwd")
    o2, states, tinv, vn = _delta_fwd(q, k, v, scal, grow, name="delta_fwd")
    yb = _gate_norm_fwd(o2, proj, wt["dn_norm"], name="gate_norm_fwd")
    pa, pb, mix, h2 = _merge_fwd(ya, yb, proj, wt["w_a_out"], wt["w_b_out"], wt["w_out"], h1, g2,
                                 name="merge_fwd")
    u3, a3, b3, f3 = _ffn_up_fwd(h2, wt["norm_ffn2"], sc3, sh3, wt["w_ffn2_up"], name="ffn2_up")
    y3, h3 = _ffn_down_fwd(f3, wt["w_ffn2_down"], h2, g3, name="ffn2_down")
    loss, dh3, dnorm_final = _final_fwd_bwd(h3, wt["norm_final"], target, name="final")

    dy3, dab3, dg3 = _ffn_bwd_act(dh3, g3, y3, a3, b3, wt["w_ffn2_down"], name="ffn2_bwd_act")
    dh2, dn3, dsc3, dsh3 = _norm_mod_matmul_bwd([(dab3, wt["w_ffn2_up"])], h2, wt["norm_ffn2"], sc3, dh3,
                                                name="ffn2_bwd_up")
    gw = {}
    gw["w_ffn2_up"] = _matmul_tn(u3, dab3, name="dw_ffn2_up", tm=1024, tn=1408)
    gw["w_ffn2_down"] = _matmul_tn(f3, dy3, name="dw_ffn2_down", tm=1408, tn=1024)

    dmix, merged, dpa, dpb, dgates, dya, dyb, dg2 = _merge_bwd(
        dh2, g2, mix, pa, pb, proj, wt["w_a_out"], wt["w_b_out"], wt["w_out"], name="merge_bwd")
    gw["w_out"] = _matmul_tn(merged, dmix, name="dw_out", tm=1024, tn=1024)
    gw["w_a_out"] = _matmul_tn(ya, dpa, name="dw_a_out", tm=512, tn=1024)
    gw["w_b_out"] = _matmul_tn(yb, dpb, name="dw_b_out", tm=1024, tn=1024)
    do, dz, ddn = _gate_norm_bwd(dyb, o2, proj, wt["dn_norm"], name="gate_norm_bwd")
    recv = {}
    (dq2, dk2, dv2, dscal, drow), got = comm.scatter(
        {nm: gw.pop(nm) for nm in ("w_ffn2_up", "w_ffn2_down")},
        lambda c: _delta_bwd(q, k, v, scal, grow, states, tinv, vn, do, name="delta_bwd", carry=c))
    recv.update(got)
    drow_p = jnp.pad(jnp.stack(drow).reshape(2, n, HEADS, CHUNK).transpose(0, 1, 3, 2).reshape(2, s, HEADS),
                     ((0, 0), (0, 0), (L_G, BA_W - L_G - HEADS)))
    dba, dalog, ddtb = _scal_bwd(jnp.stack(dscal), drow_p, ba, alog, dtb, name="scal_bwd")
    (dqkv, dbr_a, dconv_dn, dconv_a), got = comm.scatter(
        {nm: gw.pop(nm) for nm in ("w_out", "w_a_out", "w_b_out")},
        lambda c: _conv_bwd(dq2, dk2, dv2, dya, proj, wt["conv_dn"], wt["conv_a"], name="conv_bwd", carry=c))
    recv.update(got)
    dw_in = _merge_dw_in(
        _matmul_tn(u2, dqkv, name="dw_in_qkv", tm=1024, tn=1536),
        _matmul_tn(u2, dz, name="dw_in_z", tm=1024, tn=1024),
        _matmul_tn(u2, dgates, name="dw_in_gate", tm=1024, tn=1024),
        _matmul_tn(u2, dbr_a, name="dw_in_a", tm=1024, tn=1536),
        _matmul_tn(u2, dba, name="dw_in_ba", tm=1024, tn=2 * BA_W))
    (dh1, dn2, dsc2, dsh2), got = comm.scatter(
        {"w_in": dw_in},
        lambda c: _norm_mod_matmul_bwd(
            [(dqkv, w_main, 0), (dz, w_main, Z_OFF // Z_W), (dgates, w_main, GATE_OFF // GATE_W),
             (dbr_a, w_main, A_OFF // A_W), (dba, w_ba)],
            h1, wt["norm_mix"], sc2, dh2, name="in_proj_bwd", carry=c))
    recv.update(got)

    dy1, dab1, dg1 = _ffn_bwd_act(dh1, g1, y1, a1, b1, wt["w_ffn1_down"], name="ffn1_bwd_act")
    dw_down1 = _matmul_tn(f1, dy1, name="dw_ffn1_down", tm=1408, tn=1024)
    dw_up1, got = comm.scatter(
        {"w_ffn1_down": dw_down1},
        lambda c: _matmul_tn(u1, dab1, name="dw_ffn1_up", tm=1024, tn=1408, carry=c))
    recv.update(got)
    (dx, dn1, dsc1, dsh1), got = comm.scatter(
        {"w_ffn1_up": dw_up1},
        lambda c: _norm_mod_matmul_bwd([(dab1, wt["w_ffn1_up"])], x, wt["norm_ffn1"], sc1, dh1,
                                       name="ffn1_bwd_up", carry=c))
    recv.update(got)

    small = {
        "mod": jnp.concatenate([dsh1, dsc1, dg1, dsh2, dsc2, dg2, dsh3, dsc3, dg3], axis=1),
        "norm_ffn1": dn1, "norm_mix": dn2, "norm_ffn2": dn3, "norm_final": dnorm_final,
        "a_log_fwd": dalog[0, :, L_G:L_G + 8], "dt_bias_fwd": ddtb[0, :, L_G:L_G + 8],
        "a_log_bwd": dalog[1, :, L_G:L_G + 8], "dt_bias_bwd": ddtb[1, :, L_G:L_G + 8],
        "dn_norm": ddn,
        "conv_a": dconv_a[0:3].reshape(1, -1), "conv_dn": dconv_dn[0:5].reshape(1, -1),
    }
    return loss, dx, recv, small


def _full_weight(name, g):
    if name in COL_SHARDED + CONV_SHARDED:
        return g.transpose(1, 0, 2).reshape(g.shape[1], -1)
    return g.reshape(-1, g.shape[-1])


def _grad_pieces(name, g):
    g = g.astype(BF16)
    if name in COL_SHARDED:
        return g.reshape(g.shape[0], N_DEV, -1).transpose(1, 0, 2)
    return g.reshape(N_DEV, -1, g.shape[-1])


class _MeshComm:
    def __init__(self, shards):
        self.shards = shards

    def _run(self, xs, carrier, name, gather):
        if carrier is None:
            return None, _exchange(xs, name=name, gather=gather)
        return carrier((xs, gather))

    def gather(self, names, carrier=None, name=None):
        outs, got = self._run([self.shards[nm] for nm in names], carrier, name, True)
        return outs, {nm: _full_weight(nm, g) for nm, g in zip(names, got)}

    def scatter(self, grads, carrier=None, name=None):
        names = list(grads)
        outs, got = self._run([_grad_pieces(nm, grads[nm]) for nm in names], carrier, name, False)
        return outs, dict(zip(names, got))


def _mod_fwd(c_all, w_ada, *, name):
    def body(c_ref, w_ref, o_ref):
        cv = c_ref[...]
        o_ref[...] = _dot3(cv * _sigmoid(cv), w_ref[...])

    return pl.pallas_call(
        body, name=name, out_shape=jax.ShapeDtypeStruct((c_all.shape[0], w_ada.shape[1]), F32),
        compiler_params=_params(),
    )(c_all, w_ada)


def _adamw_math(w, g, m, v):
    m_new = ADAM_B1 * m + (1.0 - ADAM_B1) * g
    v_new = ADAM_B2 * v + (1.0 - ADAM_B2) * (g * g)
    m_hat = m_new / (1.0 - ADAM_B1 ** ADAM_STEP)
    v_hat = v_new / (1.0 - ADAM_B2 ** ADAM_STEP)
    delta = -ADAM_LR * (m_hat / (jnp.sqrt(v_hat) + ADAM_EPS) + ADAM_WD * w)
    return delta, m_new, v_new


def _reduce_adamw(pieces, w, m, v, *, name, tr):
    r, c = w.shape

    def body(p_ref, w_ref, m_ref, v_ref, g_ref, d_ref, mo_ref, vo_ref):
        g = p_ref[0].astype(F32)
        for src in range(1, N_DEV):
            g = g + p_ref[src].astype(F32)
        g_ref[...] = g
        d_ref[...], mo_ref[...], vo_ref[...] = _adamw_math(w_ref[...], g, m_ref[...], v_ref[...])

    tile = pl.BlockSpec((tr, c), lambda i: (i, 0))
    return pl.pallas_call(
        body, name=name, grid=(r // tr,),
        in_specs=[pl.BlockSpec((N_DEV, tr, c), lambda i: (0, i, 0)), tile, tile, tile],
        out_specs=[tile] * 4, out_shape=[jax.ShapeDtypeStruct((r, c), F32)] * 4,
        compiler_params=_params(("parallel",)),
    )(pieces, w, m, v)


def _ada_grad_adamw(c_all_t, dmod_cols, w, m, v, *, name, tr=256):
    r, c = w.shape

    def body(c_ref, dm_ref, w_ref, m_ref, v_ref, g_ref, d_ref, mo_ref, vo_ref):
        cv = c_ref[...]
        act = cv * _sigmoid(cv)
        dm = dm_ref[...]
        g = act[:, 0:1] * dm[0:1, :]
        for b in range(1, N_DEV):
            g = g + act[:, b:b + 1] * dm[b:b + 1, :]
        g_ref[...] = g
        d_ref[...], mo_ref[...], vo_ref[...] = _adamw_math(w_ref[...], g, m_ref[...], v_ref[...])

    tile = pl.BlockSpec((tr, c), lambda i: (i, 0))
    return pl.pallas_call(
        body, name=name, grid=(r // tr,),
        in_specs=[pl.BlockSpec((tr, N_DEV), lambda i: (i, 0)), pl.BlockSpec((N_DEV, c), lambda i: (0, 0)),
                  tile, tile, tile],
        out_specs=[tile] * 4, out_shape=[jax.ShapeDtypeStruct((r, c), F32)] * 4,
        compiler_params=_params(("parallel",)),
    )(c_all_t, dmod_cols, w, m, v)


def _sum_rows(parts, *, name):
    def body(p_ref, o_ref):
        acc = p_ref[0:1, :]
        for src in range(1, N_DEV):
            acc = acc + p_ref[src:src + 1, :]
        o_ref[...] = acc

    return pl.pallas_call(
        body, name=name, out_shape=jax.ShapeDtypeStruct((1, parts.shape[1]), F32), compiler_params=_params(),
    )(parts)


def _adamw_rows(g, w, m, v, *, name):
    def body(g_ref, w_ref, m_ref, v_ref, d_ref, mo_ref, vo_ref):
        d_ref[...], mo_ref[...], vo_ref[...] = _adamw_math(w_ref[...], g_ref[...], m_ref[...], v_ref[...])

    return pl.pallas_call(
        body, name=name, out_shape=[jax.ShapeDtypeStruct(g.shape, F32)] * 3, compiler_params=_params(),
    )(g, w, m, v)


WEIGHTS = ["w_ada", "b_ada", "norm_ffn1", "w_ffn1_up", "w_ffn1_down", "norm_mix", "w_in", "conv_a", "conv_dn",
           "a_log_fwd", "dt_bias_fwd", "a_log_bwd", "dt_bias_bwd", "dn_norm", "w_a_out", "w_b_out", "w_out",
           "norm_ffn2", "w_ffn2_up", "w_ffn2_down", "norm_final"]
COL_SHARDED = ["w_ffn1_up", "w_in", "w_a_out", "w_ffn2_up"]
ROW_SHARDED = ["w_ffn1_down", "w_b_out", "w_out", "w_ffn2_down"]
CONV_SHARDED = ["conv_a", "conv_dn"]
REPLICATED = ["b_ada", "norm_ffn1", "norm_mix", "a_log_fwd", "dt_bias_fwd", "a_log_bwd", "dt_bias_bwd",
              "dn_norm", "norm_ffn2", "norm_final"]
SMALL_ORDER = ["mod", "norm_ffn1", "norm_mix", "norm_ffn2", "norm_final", "a_log_fwd", "dt_bias_fwd",
               "a_log_bwd", "dt_bias_bwd", "dn_norm", "conv_a", "conv_dn"]
REDUCE_ROWS = {"w_ffn1_up": 256, "w_in": 256, "w_a_out": 256, "w_ffn2_up": 256,
               "w_ffn1_down": 176, "w_b_out": 128, "w_out": 128, "w_ffn2_down": 176}


def _pad_lanes(row):
    pad = (-row.shape[1]) % 128
    return jnp.pad(row, ((0, 0), (0, pad)))


def _unstack_cols(g):
    return g.transpose(1, 0, 2).reshape(g.shape[1], -1)


def _stack_cols(w):
    k = w.shape[0]
    return w.reshape(k, N_DEV, -1).transpose(1, 0, 2)


def kernel(x, c, w_ada, b_ada, norm_ffn1, w_ffn1_up, w_ffn1_down, norm_mix, w_in, conv_a, conv_dn, a_log_fwd, dt_bias_fwd, a_log_bwd, dt_bias_bwd, dn_norm, w_a_out, w_b_out, w_out, norm_ffn2, w_ffn2_up, w_ffn2_down, norm_final, loss_target, m_w_ada, m_b_ada, m_norm_ffn1, m_w_ffn1_up, m_w_ffn1_down, m_norm_mix, m_w_in, m_conv_a, m_conv_dn, m_a_log_fwd, m_dt_bias_fwd, m_a_log_bwd, m_dt_bias_bwd, m_dn_norm, m_w_a_out, m_w_b_out, m_w_out, m_norm_ffn2, m_w_ffn2_up, m_w_ffn2_down, m_norm_final, v_w_ada, v_b_ada, v_norm_ffn1, v_w_ffn1_up, v_w_ffn1_down, v_norm_mix, v_w_in, v_conv_a, v_conv_dn, v_a_log_fwd, v_dt_bias_fwd, v_a_log_bwd, v_dt_bias_bwd, v_dn_norm, v_w_a_out, v_w_b_out, v_w_out, v_norm_ffn2, v_w_ffn2_up, v_w_ffn2_down, v_norm_final):
    args = dict(locals())
    w_loc = {n: args[n] for n in WEIGHTS}
    m_loc = {n: args["m_" + n] for n in WEIGHTS}
    v_loc = {n: args["v_" + n] for n in WEIGHTS}
    me = _flat_index(_my_position())
    d_model = x.shape[-1]

    big = COL_SHARDED + ROW_SHARDED
    shards = {n: w_loc[n][0].astype(BF16) for n in big}
    shards.update({n: w_loc[n][0] for n in CONV_SHARDED})
    shards["c"] = c
    comm = _MeshComm(shards)
    wt = comm.gather(["c", "conv_a", "conv_dn", "w_ffn1_up"], name="gather_first")[1]
    c_all = wt.pop("c")
    for n in REPLICATED[1:]:
        wt[n] = w_loc[n].reshape(1, -1)

    mod_cols = _mod_fwd(c_all, w_ada[0], name="mod_fwd")
    mod_all = _exchange([mod_cols], name="gather_mod", gather=True)[0]
    mod_mine = lax.dynamic_index_in_dim(mod_all, me, axis=1, keepdims=False).reshape(1, -1) + b_ada
    mod9 = mod_mine.reshape(9, d_model)

    loss_loc, dx, recv, small = _local_step(x[0], loss_target[0], mod9, wt, comm)
    loss = lax.psum(loss_loc[0, 0], MESH_AXES)

    res = {}
    for n in big:
        res[n] = _reduce_adamw(recv[n], w_loc[n][0], m_loc[n][0], v_loc[n][0], name="adamw_" + n,
                               tr=REDUCE_ROWS[n])

    packed = _pad_lanes(jnp.concatenate([small[n].reshape(1, -1) for n in SMALL_ORDER], axis=1))
    parts = _exchange([packed], name="gather_small", gather=True)[0].reshape(N_DEV, -1)
    total = _sum_rows(parts, name="sum_small")
    off = 0
    gsmall = {}
    for n in SMALL_ORDER:
        size = small[n].size
        gsmall[n] = total[:, off:off + size]
        off += size
    dmod_all = parts[:, 0:9 * d_model]
    ada_cols = w_ada.shape[-1]
    dmod_cols = lax.dynamic_slice_in_dim(dmod_all, me * ada_cols, ada_cols, axis=1)
    res["w_ada"] = _ada_grad_adamw(c_all.T, dmod_cols, w_ada[0], m_w_ada[0], v_w_ada[0], name="adamw_w_ada")
    g_rows = {"b_ada": gsmall["mod"]}
    for n in REPLICATED[1:]:
        g_rows[n] = gsmall[n]
    for n in CONV_SHARDED:
        taps, width = w_loc[n].shape[1], w_loc[n].shape[2]
        full = gsmall[n].reshape(taps, -1)
        g_rows[n] = lax.dynamic_slice_in_dim(full, me * width, width, axis=1).reshape(1, -1)
    row_names = REPLICATED + CONV_SHARDED
    cat = lambda src: _pad_lanes(jnp.concatenate([src[n].reshape(1, -1) for n in row_names], axis=1))
    g_cat = cat(g_rows)
    d_cat, m_cat, v_cat = _adamw_rows(g_cat, cat(w_loc), cat(m_loc), cat(v_loc), name="adamw_small")
    off = 0
    for n in row_names:
        size = w_loc[n].size
        res[n] = tuple(t[:, off:off + size] for t in (g_cat, d_cat, m_cat, v_cat))
        off += size

    outs = [loss, dx[None]]
    for kind in range(4):
        for n in WEIGHTS:
            outs.append(res[n][kind].reshape(w_loc[n].shape))
    return tuple(outs)
```

```python
import functools
import math
import types

import jax
import jax.numpy as jnp
from jax import lax
from jax.experimental import pallas as pl
from jax.experimental.pallas import tpu as pltpu

F32 = jnp.float32
BF16 = jnp.bfloat16
EPS = 1e-6
N_DEV = 8
CHUNK = 64
HEADS = 8
HEAD_DIM = 128
MESH_AXES = ("x", "y", "c")
VMEM_LIMIT_BYTES = 56 * 1024 * 1024

ADAM_LR = 0.001
ADAM_B1 = 0.9
ADAM_B2 = 0.999
ADAM_EPS = 1e-08
ADAM_WD = 0.01
ADAM_STEP = 10


def _params(sem=None):
    return pltpu.CompilerParams(dimension_semantics=sem, vmem_limit_bytes=VMEM_LIMIT_BYTES)


def _row(n):
    return pl.BlockSpec((1, n), lambda *_: (0, 0))


def _resident(shape):
    nd = len(shape)
    return pl.BlockSpec(shape, lambda *_: (0,) * nd, pipeline_mode=pl.Buffered(1))


def _col_window(w, width, col_block):
    return pl.BlockSpec((w.shape[0], width), lambda *_: (0, col_block), pipeline_mode=pl.Buffered(1))


def _sigmoid(x):
    return 1.0 / (1.0 + jnp.exp(-x))


def _dot(a, b):
    return jnp.dot(a, b, preferred_element_type=F32)


def _dot_nt(a, b):
    return lax.dot_general(a, b, (((1,), (1,)), ((), ())), preferred_element_type=F32)


def _dot_tn(a, b):
    return lax.dot_general(a, b, (((0,), (0,)), ((), ())), preferred_element_type=F32)


def _split_bf16(x):
    hi = x.astype(BF16)
    lo = (x - hi.astype(F32)).astype(BF16)
    return hi, lo


def _dot3(a, b, dot=_dot):
    ah, al = a if isinstance(a, tuple) else _split_bf16(a)
    bh, bl = b if isinstance(b, tuple) else _split_bf16(b)
    return dot(ah, bh) + dot(ah, bl) + dot(al, bh)


def _dot_exact(a, b):
    return jnp.dot(a, b, preferred_element_type=F32, precision=lax.Precision.HIGHEST)


def _my_position():
    return tuple(lax.axis_index(a) for a in MESH_AXES)


def _peer(pos, kk):
    return tuple((1 - p) if (kk >> (2 - b)) & 1 else p for b, p in enumerate(pos))


def _flat_index(pos):
    return pos[0] * 4 + pos[1] * 2 + pos[2]


_ANY = pl.BlockSpec(memory_space=pl.ANY)


class _AllToAll:
    def __init__(self, in_refs, out_refs, send_sems, recv_sems, local_sems):
        pos = _my_position()
        me = _flat_index(pos)
        self.copies = []
        for t in range(len(in_refs)):
            self.copies.append(pltpu.make_async_copy(in_refs[t].at[me], out_refs[t].at[me], local_sems.at[t]))
            for kk in range(1, N_DEV):
                peer = _peer(pos, kk)
                self.copies.append(pltpu.make_async_remote_copy(
                    src_ref=in_refs[t].at[_flat_index(peer)], dst_ref=out_refs[t].at[me],
                    send_sem=send_sems.at[t, kk - 1], recv_sem=recv_sems.at[t, kk - 1],
                    device_id=peer, device_id_type=pl.DeviceIdType.MESH))

    def start(self):
        for cp in self.copies:
            cp.start()

    def finish(self):
        for cp in self.copies:
            cp.wait()


class _AllGather:
    def __init__(self, in_refs, out_refs, send_sems, recv_sems, local_sems):
        self.refs = (in_refs, out_refs, send_sems, recv_sems, local_sems)
        x, y, c = _my_position()
        self.me, self.sibling = (x, y, c), (x, y, 1 - c)
        self.chips = [(1 - x, y), (x, 1 - y), (1 - x, 1 - y)]
        self.core = c

    def _copy(self, t, k, block, to, own=False):
        in_refs, out_refs, send_sems, recv_sems, _ = self.refs
        rows = out_refs[t].at[_flat_index(block)]
        return pltpu.make_async_remote_copy(
            src_ref=in_refs[t] if own else rows, dst_ref=rows,
            send_sem=send_sems.at[t, k], recv_sem=recv_sems.at[t, k],
            device_id=to, device_id_type=pl.DeviceIdType.MESH)

    def _local(self, t):
        in_refs, out_refs, _, _, local_sems = self.refs
        return pltpu.make_async_copy(in_refs[t], out_refs[t].at[_flat_index(self.me)], local_sems.at[t])

    def start(self):
        c = self.core
        for t in range(len(self.refs[0])):
            self._local(t).start()
            self._copy(t, 0, self.me, self.sibling, own=True).start()
            for j, chip in enumerate(self.chips):
                self._copy(t, 1 + j, self.me, (*chip, c), own=True).start()

    def finish(self):
        c = self.core
        n_t = len(self.refs[0])
        for t in range(n_t):
            for j, chip in enumerate(self.chips):
                self._copy(t, 1 + j, (*chip, c), self.me).wait_recv()
                self._copy(t, 4 + j, (*chip, c), self.sibling).start()
        for t in range(n_t):
            self._copy(t, 0, self.sibling, self.me).wait_recv()
            for j, chip in enumerate(self.chips):
                self._copy(t, 4 + j, (*chip, 1 - c), self.me).wait_recv()
            self._copy(t, 0, self.me, self.sibling, own=True).wait_send()
            for j, chip in enumerate(self.chips):
                self._copy(t, 1 + j, self.me, (*chip, c), own=True).wait_send()
                self._copy(t, 4 + j, (*chip, c), self.sibling).wait_send()
            self._local(t).wait()


def _exchange_plan(in_refs, out_refs, send_sems, recv_sems, local_sems, gather):
    return (_AllGather if gather else _AllToAll)(in_refs, out_refs, send_sems, recv_sems, local_sems)


def _exchange_shapes(xs, gather):
    out_shape = [jax.ShapeDtypeStruct(((N_DEV,) + x.shape) if gather else x.shape, x.dtype) for x in xs]
    sems = [pltpu.SemaphoreType.DMA((len(xs), N_DEV - 1)), pltpu.SemaphoreType.DMA((len(xs), N_DEV - 1)),
            pltpu.SemaphoreType.DMA((len(xs),))]
    return out_shape, sems


def _exchange(xs, *, name, gather):
    nt = len(xs)

    def body(*refs):
        plan = _exchange_plan(refs[:nt], refs[nt:2 * nt], *refs[2 * nt:], gather)
        plan.start()
        plan.finish()

    out_shape, sems = _exchange_shapes(xs, gather)
    return pl.pallas_call(body, name=name, in_specs=[_ANY] * nt, out_specs=[_ANY] * nt, out_shape=out_shape,
                          scratch_shapes=sems)(*xs)


def _launch(body, carry, args, *, name, grid, in_specs, out_specs, out_shape, scratch_shapes=(), sem):
    single = not isinstance(out_shape, (list, tuple))
    out_specs = [out_specs] if single else list(out_specs)
    out_shape = [out_shape] if single else list(out_shape)
    if carry is None:
        outs = pl.pallas_call(body, name=name, grid=grid, in_specs=list(in_specs), out_specs=out_specs,
                              out_shape=out_shape, scratch_shapes=list(scratch_shapes),
                              compiler_params=_params(sem))(*args)
        return outs[0] if single else outs
    xs, gather = carry
    nt, n_in, n_out, n_scr = len(xs), len(args), len(out_shape), len(scratch_shapes)
    x_shape, sems = _exchange_shapes(xs, gather)

    def wrapped(*refs):
        c_in, x_in = refs[:n_in], refs[n_in:n_in + nt]
        c_out = refs[n_in + nt:n_in + nt + n_out]
        x_out = refs[n_in + nt + n_out:n_in + 2 * nt + n_out]
        scr = refs[n_in + 2 * nt + n_out:]
        ids = [pl.program_id(a) for a in range(len(grid))]
        first = functools.reduce(jnp.logical_and, [i == 0 for i in ids])
        last = functools.reduce(jnp.logical_and, [i == g - 1 for i, g in zip(ids, grid)])
        plan = lambda: _exchange_plan(x_in, x_out, *scr[n_scr:], gather)

        @pl.when(first)
        def _():
            plan().start()

        body(*c_in, *c_out, *scr[:n_scr])

        @pl.when(last)
        def _():
            plan().finish()

    outs = pl.pallas_call(
        wrapped, name=name, grid=grid, in_specs=list(in_specs) + [_ANY] * nt,
        out_specs=out_specs + [_ANY] * nt, out_shape=out_shape + x_shape,
        scratch_shapes=list(scratch_shapes) + sems,
        compiler_params=_params(("arbitrary",) * len(grid)))(*args, *xs)
    compute = outs[:n_out]
    return (compute[0] if single else compute), outs[n_out:]


def _norm_mod(x, nw, sc, sh):
    r = lax.rsqrt(jnp.mean(x * x, axis=-1, keepdims=True) + EPS)
    return (x * r * nw) * (1.0 + sc) + sh


def _norm_mod_bwd(x, nw, sc, du):
    r = lax.rsqrt(jnp.mean(x * x, axis=-1, keepdims=True) + EPS)
    xhat = x * r
    n = xhat * nw
    dsh = jnp.sum(du, axis=0, keepdims=True)
    dsc = jnp.sum(du * n, axis=0, keepdims=True)
    dn = du * (1.0 + sc)
    dnw = jnp.sum(dn * xhat, axis=0, keepdims=True)
    dxhat = dn * nw
    dx = r * (dxhat - xhat * jnp.mean(dxhat * xhat, axis=-1, keepdims=True))
    return dx, dnw, dsc, dsh


def _ffn_up_fwd(h, nw, sc, sh, wup, *, name, ts=512, tn=1408, carry=None):
    s, d = h.shape
    f_dim = wup.shape[1] // 2
    nj = f_dim // tn

    def body(h_ref, nw_ref, sc_ref, sh_ref, wa_ref, wb_ref, u_ref, a_ref, b_ref, f_ref):
        @pl.when(pl.program_id(1) == 0)
        def _():
            u_ref[...] = _norm_mod(h_ref[...], nw_ref[...], sc_ref[...], sh_ref[...]).astype(BF16)

        u = u_ref[...]
        a = _dot(u, wa_ref[...])
        b = _dot(u, wb_ref[...])
        a_ref[...] = a.astype(BF16)
        b_ref[...] = b.astype(BF16)
        f_ref[...] = (a * _sigmoid(a) * b).astype(BF16)

    return _launch(
        body, carry, (h, nw, sc, sh, wup, wup), name=name, grid=(s // ts, nj),
        in_specs=[pl.BlockSpec((ts, d), lambda i, j: (i, 0)), _row(d), _row(d), _row(d),
                  pl.BlockSpec((d, tn), lambda i, j: (0, j)),
                  pl.BlockSpec((d, tn), lambda i, j: (0, j + nj))],
        out_specs=[pl.BlockSpec((ts, d), lambda i, j: (i, 0)),
                   pl.BlockSpec((ts, tn), lambda i, j: (i, j)),
                   pl.BlockSpec((ts, tn), lambda i, j: (i, j)),
                   pl.BlockSpec((ts, tn), lambda i, j: (i, j))],
        out_shape=[jax.ShapeDtypeStruct((s, d), BF16)] + [jax.ShapeDtypeStruct((s, f_dim), BF16)] * 3,
        sem=("parallel", "arbitrary"))


def _ffn_down_fwd(f, wd, h, g, *, name, ts=512):
    s, f_dim = f.shape
    d = wd.shape[1]

    def body(f_ref, wd_ref, h_ref, g_ref, y_ref, ho_ref):
        y = _dot(f_ref[...], wd_ref[...])
        y_ref[...] = y.astype(BF16)
        ho_ref[...] = h_ref[...] + (0.5 * g_ref[...]) * y

    return pl.pallas_call(
        body, name=name, grid=(s // ts,),
        in_specs=[pl.BlockSpec((ts, f_dim), lambda i: (i, 0)), _resident((f_dim, d)),
                  pl.BlockSpec((ts, d), lambda i: (i, 0)), _row(d)],
        out_specs=[pl.BlockSpec((ts, d), lambda i: (i, 0)), pl.BlockSpec((ts, d), lambda i: (i, 0))],
        out_shape=[jax.ShapeDtypeStruct((s, d), BF16), jax.ShapeDtypeStruct((s, d), F32)],
        compiler_params=_params(("parallel",)),
    )(f, wd, h, g)


def _ffn_bwd_act(dh, g, y, a, b, wd, *, name, ts=256, carry=None):
    s, d = dh.shape
    f_dim = a.shape[1]

    def body(dh_ref, g_ref, y_ref, a_ref, b_ref, wd_ref, dy_ref, dab_ref, dg_ref):
        dh_v = dh_ref[...]
        dy = ((0.5 * g_ref[...]) * dh_v).astype(BF16)
        dy_ref[...] = dy
        part = jnp.sum(0.5 * dh_v * y_ref[...].astype(F32), axis=0, keepdims=True)

        @pl.when(pl.program_id(0) == 0)
        def _():
            dg_ref[...] = jnp.zeros_like(dg_ref)

        dg_ref[...] += part
        df = _dot_nt(dy, wd_ref[...])
        av = a_ref[...].astype(F32)
        bv = b_ref[...].astype(F32)
        sg = _sigmoid(av)
        dab_ref[:, :f_dim] = (df * bv * (sg * (1.0 + av * (1.0 - sg)))).astype(BF16)
        dab_ref[:, f_dim:] = (df * (av * sg)).astype(BF16)

    return _launch(
        body, carry, (dh, g, y, a, b, wd), name=name, grid=(s // ts,),
        in_specs=[pl.BlockSpec((ts, d), lambda i: (i, 0)), _row(d),
                  pl.BlockSpec((ts, d), lambda i: (i, 0)),
                  pl.BlockSpec((ts, f_dim), lambda i: (i, 0)),
                  pl.BlockSpec((ts, f_dim), lambda i: (i, 0)),
                  _resident((f_dim, d))],
        out_specs=[pl.BlockSpec((ts, d), lambda i: (i, 0)),
                   pl.BlockSpec((ts, 2 * f_dim), lambda i: (i, 0)), _row(d)],
        out_shape=[jax.ShapeDtypeStruct((s, d), BF16), jax.ShapeDtypeStruct((s, 2 * f_dim), BF16),
                   jax.ShapeDtypeStruct((1, d), F32)],
        sem=("arbitrary",))


def _norm_mod_matmul_bwd(pairs, h, nw, sc, dh_in, *, name, ts=256, carry=None):
    s, d = h.shape
    n_pairs = len(pairs)

    def body(*refs):
        dx_refs = refs[:n_pairs]
        w_refs = refs[n_pairs:2 * n_pairs]
        h_ref, nw_ref, sc_ref, dhi_ref, dho_ref, dnw_ref, dsc_ref, dsh_ref = refs[2 * n_pairs:]
        du = _dot_nt(dx_refs[0][...], w_refs[0][...])
        for k in range(1, n_pairs):
            du = du + _dot_nt(dx_refs[k][...], w_refs[k][...])
        dx, dnw, dsc, dsh = _norm_mod_bwd(h_ref[...], nw_ref[...], sc_ref[...], du)
        dho_ref[...] = dhi_ref[...] + dx

        @pl.when(pl.program_id(0) == 0)
        def _():
            dnw_ref[...] = jnp.zeros_like(dnw_ref)
            dsc_ref[...] = jnp.zeros_like(dsc_ref)
            dsh_ref[...] = jnp.zeros_like(dsh_ref)

        dnw_ref[...] += dnw
        dsc_ref[...] += dsc
        dsh_ref[...] += dsh

    dxs = [p[0] for p in pairs]
    ws = [p[1] for p in pairs]
    tile = pl.BlockSpec((ts, d), lambda i: (i, 0))
    return _launch(
        body, carry, (*dxs, *ws, h, nw, sc, dh_in), name=name, grid=(s // ts,),
        in_specs=([pl.BlockSpec((ts, x.shape[1]), lambda i: (i, 0)) for x in dxs]
                  + [_col_window(w, x.shape[1], p[2] if len(p) > 2 else 0) for p, x, w in zip(pairs, dxs, ws)]
                  + [tile, _row(d), _row(d), tile]),
        out_specs=[tile, _row(d), _row(d), _row(d)],
        out_shape=[jax.ShapeDtypeStruct((s, d), F32)] + [jax.ShapeDtypeStruct((1, d), F32)] * 3,
        sem=("arbitrary",))


def _matmul_tn(a, b, *, name, tm, tn, tk=1024, carry=None):
    s, m = a.shape
    n = b.shape[1]
    tk = min(tk, s)
    nk = s // tk

    def body(a_ref, b_ref, o_ref, acc_ref):
        k = pl.program_id(2)

        @pl.when(k == 0)
        def _():
            acc_ref[...] = jnp.zeros_like(acc_ref)

        acc_ref[...] += _dot_tn(a_ref[...], b_ref[...])

        @pl.when(k == nk - 1)
        def _():
            o_ref[...] = acc_ref[...].astype(o_ref.dtype)

    return _launch(
        body, carry, (a, b), name=name, grid=(m // tm, n // tn, nk),
        in_specs=[pl.BlockSpec((tk, tm), lambda i, j, k: (k, i)),
                  pl.BlockSpec((tk, tn), lambda i, j, k: (k, j))],
        out_specs=pl.BlockSpec((tm, tn), lambda i, j, k: (i, j)),
        out_shape=jax.ShapeDtypeStruct((m, n), BF16),
        scratch_shapes=[pltpu.VMEM((tm, tn), F32)],
        sem=("parallel", "parallel", "arbitrary"))


def _in_proj_fwd(h, nw, sc, sh, w_main, w_ba, *, name, ts=512, tn=1536, carry=None):
    s, d = h.shape
    n_main = w_main.shape[1]
    n_ba = w_ba.shape[1]

    def body(h_ref, nw_ref, sc_ref, sh_ref, w_ref, wba_ref, u_ref, p_ref, ba_ref):
        @pl.when(pl.program_id(1) == 0)
        def _():
            u0 = _norm_mod(h_ref[...], nw_ref[...], sc_ref[...], sh_ref[...]).astype(BF16)
            u_ref[...] = u0
            ba_ref[...] = _dot(u0, wba_ref[...])

        p_ref[...] = _dot(u_ref[...], w_ref[...]).astype(BF16)

    return _launch(
        body, carry, (h, nw, sc, sh, w_main, w_ba), name=name, grid=(s // ts, n_main // tn),
        in_specs=[pl.BlockSpec((ts, d), lambda i, j: (i, 0)), _row(d), _row(d), _row(d),
                  pl.BlockSpec((d, tn), lambda i, j: (0, j)), _resident((d, n_ba))],
        out_specs=[pl.BlockSpec((ts, d), lambda i, j: (i, 0)),
                   pl.BlockSpec((ts, tn), lambda i, j: (i, j)),
                   pl.BlockSpec((ts, n_ba), lambda i, j: (i, 0))],
        out_shape=[jax.ShapeDtypeStruct((s, d), BF16), jax.ShapeDtypeStruct((s, n_main), BF16),
                   jax.ShapeDtypeStruct((s, n_ba), F32)],
        sem=("parallel", "arbitrary"))


QKV_W = 3 * HEADS * HEAD_DIM
Z_OFF, Z_W = 3072, 1024
GATE_OFF, GATE_W = 4096, 2048
A_OFF, A_W = 6144, 1536
N_MAIN = 7680
CONV_A = 512
BA_W = 128

L_BETA, L_G, L_EG, L_EKD, L_EGC = 0, 8, 16, 24, 32


def _softplus(z):
    e = jnp.exp(-jnp.abs(z))
    small = e * (1.0 - e * (0.5 - e * (1.0 / 3.0)))
    return jnp.maximum(z, 0.0) + jnp.where(e < 1e-3, small, jnp.log(1.0 + e))


def _tri(n, sgn, strict=False):
    i = lax.broadcasted_iota(jnp.int32, (n, n), 0)
    j = lax.broadcasted_iota(jnp.int32, (n, n), 1)
    dlt = (i - j) * sgn
    return (dlt > 0) if strict else (dlt >= 0)


def _scal_fwd(ba, alog, dtb, *, name, ts=512):
    s = ba.shape[0]

    def body(ba_ref, al_ref, dt_ref, o_ref):
        d = pl.program_id(0)
        sgn = 1 - 2 * d
        x = ba_ref[...]
        lane = lax.broadcasted_iota(jnp.int32, x.shape, 1)
        beta = _sigmoid(x)
        g = -jnp.exp(al_ref[0]) * _softplus(x + dt_ref[0])
        g = jnp.where((lane >= L_G) & (lane < L_EGC + 8), g, 0.0)
        ltri = jnp.where(_tri(CHUNK, sgn), 1.0, 0.0).astype(F32)
        for c in range(ts // CHUNK):
            rows = slice(c * CHUNK, (c + 1) * CHUNK)
            gc = _dot_exact(ltri, g[rows])
            g_end = jnp.where(d == 0, gc[CHUNK - 1:CHUNK], gc[0:1])
            ln = lane[rows]
            out = jnp.where(ln < L_G, beta[rows],
                  jnp.where(ln < L_EG, gc,
                  jnp.where(ln < L_EKD, jnp.exp(gc),
                  jnp.where(ln < L_EGC, jnp.exp(g_end - gc),
                  jnp.where(ln < L_EGC + 8, jnp.broadcast_to(jnp.exp(g_end), gc.shape), 0.0)))))
            o_ref[0, rows, :] = out

    return pl.pallas_call(
        body, name=name, grid=(2, s // ts),
        in_specs=[pl.BlockSpec((ts, BA_W), lambda d, i: (i, d)),
                  pl.BlockSpec((1, 1, BA_W), lambda d, i: (d, 0, 0)),
                  pl.BlockSpec((1, 1, BA_W), lambda d, i: (d, 0, 0))],
        out_specs=pl.BlockSpec((1, ts, BA_W), lambda d, i: (d, i, 0)),
        out_shape=jax.ShapeDtypeStruct((2, s, BA_W), F32),
        compiler_params=_params(("parallel", "parallel")),
    )(ba, alog, dtb)


def _scal_bwd(dscal, drow, ba, alog, dtb, *, name, ts=512):
    s = ba.shape[0]

    def body(ds_ref, dr_ref, ba_ref, al_ref, dt_ref, dba_ref, dal_ref, ddt_ref):
        d = pl.program_id(0)
        sgn = 1 - 2 * d
        x = ba_ref[...]
        lane = lax.broadcasted_iota(jnp.int32, x.shape, 1)
        in_g = (lane >= L_G) & (lane < L_G + 8)
        beta = _sigmoid(x)
        z = x + dt_ref[0]
        neg_a = -jnp.exp(al_ref[0])
        g = neg_a * _softplus(z)
        dsv = ds_ref[0]
        dgc = jnp.where(in_g, dsv + dr_ref[0], 0.0)
        utri = jnp.where(_tri(CHUNK, -sgn), 1.0, 0.0).astype(F32)
        dal = jnp.zeros((1, BA_W), F32)
        ddt = jnp.zeros((1, BA_W), F32)
        for c in range(ts // CHUNK):
            rows = slice(c * CHUNK, (c + 1) * CHUNK)
            dg = _dot_exact(utri, dgc[rows])
            dz = dg * neg_a * _sigmoid(z[rows])
            dal = dal + jnp.sum(dg * g[rows], axis=0, keepdims=True)
            ddt = ddt + jnp.sum(dz, axis=0, keepdims=True)
            b = beta[rows]
            out = jnp.where(lane[rows] < L_G, dsv[rows] * b * (1.0 - b), jnp.where(in_g[rows], dz, 0.0))
            dba_ref[rows, :] = out.astype(BF16)

        @pl.when(pl.program_id(1) == 0)
        def _():
            dal_ref[...] = jnp.zeros_like(dal_ref)
            ddt_ref[...] = jnp.zeros_like(ddt_ref)

        dal_ref[0] += dal
        ddt_ref[0] += ddt

    row3 = pl.BlockSpec((1, 1, BA_W), lambda d, i: (d, 0, 0))
    tok3 = pl.BlockSpec((1, ts, BA_W), lambda d, i: (d, i, 0))
    return pl.pallas_call(
        body, name=name, grid=(2, s // ts),
        in_specs=[tok3, tok3, pl.BlockSpec((ts, BA_W), lambda d, i: (i, d)), row3, row3],
        out_specs=[pl.BlockSpec((ts, BA_W), lambda d, i: (i, d)), row3, row3],
        out_shape=[jax.ShapeDtypeStruct((s, 2 * BA_W), BF16), jax.ShapeDtypeStruct((2, 1, BA_W), F32),
                   jax.ShapeDtypeStruct((2, 1, BA_W), F32)],
        compiler_params=_params(("arbitrary", "arbitrary")),
    )(dscal, drow, ba, alog, dtb)


HALO = 16


def _halo_specs(ts, width, col_block, n_rows, rows=HALO):
    r = ts // rows
    last = n_rows // rows - 1
    return [pl.BlockSpec((rows, width), lambda i: (jnp.maximum(i * r - 1, 0), col_block)),
            pl.BlockSpec((ts, width), lambda i: (i, col_block)),
            pl.BlockSpec((rows, width), lambda i: (jnp.minimum((i + 1) * r, last), col_block))]


def _fill_halo(dst_ref, prev_ref, cur_ref, next_ref, first, last, fn=lambda r: r[...].astype(F32)):
    h = prev_ref.shape[0]
    ts = cur_ref.shape[0]
    p = fn(prev_ref)
    n = fn(next_ref)
    dst_ref[0:h, :] = jnp.where(first, 0.0, p)
    dst_ref[h:h + ts, :] = fn(cur_ref)
    dst_ref[h + ts:h + ts + h, :] = jnp.where(last, 0.0, n)


def _dwconv_rows(src_ref, w, start, n_rows, cols):
    acc = w[0:1, :] * src_ref[start:start + n_rows, cols]
    for i in range(1, w.shape[0]):
        acc = acc + w[i:i + 1, :] * src_ref[start + i:start + i + n_rows, cols]
    return acc


def _l2norm_heads(act, scale):
    outs = []
    for hd in range(HEADS):
        seg = act[:, hd * HEAD_DIM:(hd + 1) * HEAD_DIM]
        outs.append(seg * (lax.rsqrt(jnp.sum(seg * seg, axis=-1, keepdims=True) + EPS) * scale))
    return jnp.concatenate(outs, axis=-1)


Q_SCALE = HEAD_DIM ** -0.5


def _conv_fwd(proj, conv_dn, conv_a, *, name, ts=256):
    s = proj.shape[0]
    hd = HEADS * HEAD_DIM
    nt = s // ts

    def body(qp_ref, qc_ref, qn_ref, ap_ref, ac_ref, an_ref, wdn_ref, wa_ref,
             q_ref, k_ref, v_ref, ya_ref, xs_ref, xa_ref):
        i = pl.program_id(0)
        first, last = i == 0, i == nt - 1
        _fill_halo(xs_ref, qp_ref, qc_ref, qn_ref, first, last)
        wdn = wdn_ref[...]
        for part, o_ref in enumerate((q_ref, k_ref, v_ref)):
            cols = slice(part * hd, (part + 1) * hd)
            pre = _dwconv_rows(xs_ref, wdn[:, cols], HALO - 2, ts, cols)
            act = pre * _sigmoid(pre)
            if part == 0:
                act = _l2norm_heads(act, Q_SCALE)
            elif part == 1:
                act = _l2norm_heads(act, 1.0)
            o_ref[...] = act
        cv = lambda r: r[:, CONV_A:2 * CONV_A].astype(F32) * r[:, 2 * CONV_A:].astype(F32)
        _fill_halo(xa_ref, ap_ref, ac_ref, an_ref, first, last, fn=cv)
        conv = _dwconv_rows(xa_ref, wa_ref[...], HALO - 1, ts, slice(0, CONV_A))
        ya_ref[...] = (ac_ref[:, 0:CONV_A].astype(F32) * conv).astype(BF16)

    tile = lambda w: pl.BlockSpec((ts, w), lambda i: (i, 0))
    return pl.pallas_call(
        body, name=name, grid=(nt,),
        in_specs=(_halo_specs(ts, QKV_W, 0, s) + _halo_specs(ts, A_W, A_OFF // A_W, s)
                  + [_resident(conv_dn.shape), _resident(conv_a.shape)]),
        out_specs=[tile(hd), tile(hd), tile(hd), tile(CONV_A)],
        out_shape=[jax.ShapeDtypeStruct((s, hd), F32)] * 3 + [jax.ShapeDtypeStruct((s, CONV_A), BF16)],
        scratch_shapes=[pltpu.VMEM((ts + 2 * HALO, QKV_W), F32), pltpu.VMEM((ts + 2 * HALO, CONV_A), F32)],
        compiler_params=_params(("parallel",)),
    )(proj, proj, proj, proj, proj, proj, conv_dn, conv_a)


def _chunk_of_step(d, c, n):
    return c + d * (n - 1 - 2 * c)


def _head_scalars(scv, grv, hd):
    col = lambda base: scv[:, base + hd:base + hd + 1]
    return (col(L_BETA), col(L_G), col(L_EG), col(L_EKD),
            scv[0:1, L_EGC + hd:L_EGC + hd + 1], grv[hd:hd + 1, :])


def _decay_matrix(gcol, grow, incl):
    return jnp.where(incl, jnp.exp(jnp.minimum(gcol - grow, 0.0)), 0.0)


INV_BASE = 8


def _unit_lower_inverse(a_m, top=None):
    n = a_m.shape[0]
    top = top or n
    i = lax.broadcasted_iota(jnp.int32, (n, n), 0)
    j = lax.broadcasted_iota(jnp.int32, (n, n), 1)

    def same_block(m):
        sh = int(math.log2(m))
        return jnp.right_shift(i, sh) == jnp.right_shift(j, sh)

    x = jnp.where(same_block(INV_BASE), -a_m, 0.0)
    t = jnp.where(i == j, 1.0, 0.0) + x
    p = x
    for _ in range(int(math.log2(INV_BASE)) - 1):
        p_b = p.astype(BF16)
        p = _dot(p_b, p_b)
        t = t + _dot(t.astype(BF16), p.astype(BF16))
    m = INV_BASE
    while m < top:
        join = jnp.where(same_block(2 * m) & jnp.logical_not(same_block(m)), a_m, 0.0)
        t_b = t.astype(BF16)
        t = t - _dot(_dot(t_b, join.astype(BF16)).astype(BF16), t_b)
        m *= 2
    return t


def _delta_fwd_per_head(q, k, v, scal, grow, *, name):
    s = q.shape[0]
    n = s // CHUNK
    hd_all = HEADS * HEAD_DIM

    def body(q_ref, k_ref, v_ref, sc_ref, gr_ref, o_ref, st_ref, t_ref, vn_ref, state):
        d = pl.program_id(0)
        sgn = 1 - 2 * d

        @pl.when(pl.program_id(1) == 0)
        def _():
            state[...] = jnp.zeros_like(state)

        incl = _tri(CHUNK, sgn)
        strict = _tri(CHUNK, sgn, strict=True)
        scv = sc_ref[0]
        grv = gr_ref[0, 0]
        for hd in range(HEADS):
            cols = slice(hd * HEAD_DIM, (hd + 1) * HEAD_DIM)
            qh, kh, vh = q_ref[:, cols], k_ref[:, cols], v_ref[:, cols]
            beta, gcol, eg, ekd, egc, grow_h = _head_scalars(scv, grv, hd)
            dm = _decay_matrix(gcol, grow_h, incl)
            k_b = kh.astype(BF16)
            kk = _dot_nt((kh * beta).astype(BF16), k_b)
            t = _unit_lower_inverse(jnp.where(strict, kk * dm, 0.0))
            p_m = jnp.where(incl, _dot_nt(qh.astype(BF16), k_b) * dm, 0.0)
            sh = state[hd]
            sh_b = sh.astype(BF16)
            st_ref[0, 0, hd] = sh_b
            r = vh - _dot((kh * eg).astype(BF16), sh_b)
            vn = _dot3(t, beta * r)
            vn_b = vn.astype(BF16)
            o_ref[0, :, cols] = _dot((qh * eg).astype(BF16), sh_b) + _dot(p_m.astype(BF16), vn_b)
            state[hd] = egc * sh + _dot_tn((kh * ekd).astype(BF16), vn_b)
            t_ref[0, 0, hd] = t
            vn_ref[0, :, cols] = vn_b

    tok = lambda d, c: (_chunk_of_step(d, c, n), 0)
    dtok = lambda d, c: (d, _chunk_of_step(d, c, n), 0)
    dchunk4 = lambda d, c: (d, _chunk_of_step(d, c, n), 0, 0)
    dchunk5 = lambda d, c: (d, _chunk_of_step(d, c, n), 0, 0, 0)
    return pl.pallas_call(
        body, name=name, grid=(2, n),
        in_specs=[pl.BlockSpec((CHUNK, hd_all), tok)] * 3
                 + [pl.BlockSpec((1, CHUNK, BA_W), dtok), pl.BlockSpec((1, 1, HEADS, CHUNK), dchunk4)],
        out_specs=[pl.BlockSpec((1, CHUNK, hd_all), dtok),
                   pl.BlockSpec((1, 1, HEADS, HEAD_DIM, HEAD_DIM), dchunk5),
                   pl.BlockSpec((1, 1, HEADS, CHUNK, CHUNK), dchunk5),
                   pl.BlockSpec((1, CHUNK, hd_all), dtok)],
        out_shape=[jax.ShapeDtypeStruct((2, s, hd_all), F32),
                   jax.ShapeDtypeStruct((2, n, HEADS, HEAD_DIM, HEAD_DIM), BF16),
                   jax.ShapeDtypeStruct((2, n, HEADS, CHUNK, CHUNK), F32),
                   jax.ShapeDtypeStruct((2, s, hd_all), BF16)],
        scratch_shapes=[pltpu.VMEM((HEADS, HEAD_DIM, HEAD_DIM), F32)],
        compiler_params=_params(("arbitrary", "arbitrary")),
    )(q, k, v, scal, grow)


def _delta_bwd_per_head(q, k, v, scal, grow, states, tinv, vn, do, *, name):
    s = q.shape[0]
    n = s // CHUNK
    hd_all = HEADS * HEAD_DIM

    def body(q_ref, k_ref, v_ref, sc_ref, gr_ref, st_ref, t_ref, vn_ref, do_ref,
             dq_ref, dk_ref, dv_ref, dsc_ref, dgr_ref, dstate):
        d = pl.program_id(0)
        sgn = 1 - 2 * d

        @pl.when(pl.program_id(1) == 0)
        def _():
            dstate[...] = jnp.zeros_like(dstate)

        incl = _tri(CHUNK, sgn)
        strict = _tri(CHUNK, sgn, strict=True)
        scv = sc_ref[0]
        grv = gr_ref[0, 0]
        lane = lax.broadcasted_iota(jnp.int32, (CHUNK, BA_W), 1)
        row = lax.broadcasted_iota(jnp.int32, (CHUNK, 1), 0)
        sub = lax.broadcasted_iota(jnp.int32, (HEADS, CHUNK), 0)
        end_row = jnp.where(d == 0, CHUNK - 1, 0)
        dsc_acc = jnp.zeros((CHUNK, BA_W), F32)
        dgr_acc = jnp.zeros((HEADS, CHUNK), F32)
        for hd in range(HEADS):
            cols = slice(hd * HEAD_DIM, (hd + 1) * HEAD_DIM)
            qh, kh, vh = q_ref[:, cols], k_ref[:, cols], v_ref[:, cols]
            beta, gcol, eg, ekd, egc, grow_h = _head_scalars(scv, grv, hd)
            dm = _decay_matrix(gcol, grow_h, incl)
            q_b, k_b = qh.astype(BF16), kh.astype(BF16)
            kb_b = (kh * beta).astype(BF16)
            kk = _dot_nt(kb_b, k_b)
            qk = _dot_nt(q_b, k_b)
            p_m = jnp.where(incl, qk * dm, 0.0)
            t = t_ref[0, 0, hd]
            vn_b = vn_ref[0, :, cols]
            sh_b = st_ref[0, 0, hd]
            dsp = dstate[hd]
            dsp_b = dsp.astype(BF16)
            do_b = do_ref[:, cols].astype(BF16)
            kg, qg, kd = kh * eg, qh * eg, kh * ekd
            kg_b, qg_b, kd_b = kg.astype(BF16), qg.astype(BF16), kd.astype(BF16)
            r = vh - _dot(kg_b, sh_b)
            dvn = _dot_tn(p_m.astype(BF16), do_b) + _dot(kd_b, dsp_b)
            db = _dot3(t, dvn, dot=_dot_tn)
            dr = db * beta
            dbeta = jnp.sum(db * r, axis=-1, keepdims=True)
            dr_b, db_b = dr.astype(BF16), db.astype(BF16)
            dkg = -_dot_nt(dr_b, sh_b)
            dqg = _dot_nt(do_b, sh_b)
            dkd = _dot_nt(vn_b, dsp_b)
            dpm = jnp.where(incl, _dot_nt(do_b, vn_b), 0.0) * dm
            dam = jnp.where(strict, -_dot_nt(db_b, vn_b), 0.0) * dm
            dpm_b, dam_b = dpm.astype(BF16), dam.astype(BF16)
            dkb = _dot(dam_b, k_b)
            dq_ref[0, :, cols] = dqg * eg + _dot(dpm_b, k_b)
            dk_ref[0, :, cols] = (dkg * eg + dkd * ekd + _dot_tn(dpm_b, q_b) + _dot_tn(dam_b, kb_b)
                                  + dkb * beta)
            dv_ref[0, :, cols] = dr
            dbeta = dbeta + jnp.sum(dkb * kh, axis=-1, keepdims=True)
            m = dpm * qk + dam * kk
            kd_term = jnp.sum(dkd * kd, axis=-1, keepdims=True)
            dgcol = (jnp.sum(dqg * qg, axis=-1, keepdims=True) + jnp.sum(dkg * kg, axis=-1, keepdims=True)
                     - kd_term + jnp.sum(m, axis=-1, keepdims=True))
            dg_end = jnp.sum(kd_term) + egc * jnp.sum(dsp * sh_b.astype(F32))
            dgcol = dgcol + jnp.where(row == end_row, dg_end, 0.0)
            dsc_acc = jnp.where(lane == L_BETA + hd, dbeta, dsc_acc)
            dsc_acc = jnp.where(lane == L_G + hd, dgcol, dsc_acc)
            dgr_acc = jnp.where(sub == hd, -jnp.sum(m, axis=0, keepdims=True), dgr_acc)
            dstate[hd] = _dot_tn(qg_b, do_b) + egc * dsp - _dot_tn(kg_b, dr_b)
        dsc_ref[0] = dsc_acc
        dgr_ref[0, 0] = dgr_acc

    step = lambda d, c: n - 1 - _chunk_of_step(d, c, n)
    tok = lambda d, c: (step(d, c), 0)
    dtok = lambda d, c: (d, step(d, c), 0)
    dchunk4 = lambda d, c: (d, step(d, c), 0, 0)
    dchunk5 = lambda d, c: (d, step(d, c), 0, 0, 0)
    tok_spec = pl.BlockSpec((CHUNK, hd_all), tok)
    dtok_spec = pl.BlockSpec((1, CHUNK, hd_all), dtok)
    return pl.pallas_call(
        body, name=name, grid=(2, n),
        in_specs=[tok_spec] * 3
                 + [pl.BlockSpec((1, CHUNK, BA_W), dtok), pl.BlockSpec((1, 1, HEADS, CHUNK), dchunk4),
                    pl.BlockSpec((1, 1, HEADS, HEAD_DIM, HEAD_DIM), dchunk5),
                    pl.BlockSpec((1, 1, HEADS, CHUNK, CHUNK), dchunk5), dtok_spec, tok_spec],
        out_specs=[dtok_spec] * 3
                  + [pl.BlockSpec((1, CHUNK, BA_W), dtok), pl.BlockSpec((1, 1, HEADS, CHUNK), dchunk4)],
        out_shape=[jax.ShapeDtypeStruct((2, s, hd_all), F32)] * 3
                  + [jax.ShapeDtypeStruct((2, s, BA_W), F32), jax.ShapeDtypeStruct((2, n, HEADS, CHUNK), F32)],
        scratch_shapes=[pltpu.VMEM((HEADS, HEAD_DIM, HEAD_DIM), F32)],
        compiler_params=_params(("arbitrary", "arbitrary")),
    )(q, k, v, scal, grow, states, tinv, vn, do)


GROUP = 4
GROWS = GROUP * CHUNK
N_GROUPS = HEADS // GROUP


def _stack(parts):
    return jnp.concatenate(parts, axis=0)


M_INCL, M_STRICT, M_EYE, M_BASE, M_JOIN = 0, 1, 2, 3, 4
JOIN_SIZES = (16, 32, 64)
N_MASKS = M_JOIN + len(JOIN_SIZES)


def _write_group_masks(mask_ref, sgn, n_masks):
    i = lax.broadcasted_iota(jnp.int32, (GROWS, GROWS), 0)
    j = lax.broadcasted_iota(jnp.int32, (GROWS, GROWS), 1)
    same = lambda m: jnp.right_shift(i, int(math.log2(m))) == jnp.right_shift(j, int(math.log2(m)))
    dlt = (i - j) * sgn
    one = lambda cond: jnp.where(cond, 1.0, 0.0).astype(F32)
    mask_ref[M_INCL] = one(same(CHUNK) & (dlt >= 0))
    mask_ref[M_STRICT] = one(same(CHUNK) & (dlt > 0))
    if n_masks > M_EYE:
        mask_ref[M_EYE] = one(i == j)
        mask_ref[M_BASE] = one(same(INV_BASE))
        for lvl, m in enumerate(JOIN_SIZES):
            mask_ref[M_JOIN + lvl] = one(same(m) & jnp.logical_not(same(m // 2)))


def _group_decay(gcol, grow, mask_ref):
    return jnp.exp(jnp.minimum(gcol - grow, 0.0)) * mask_ref[M_INCL]


def _block_inverse_many(a_ms, mask_refs):
    xs = [-(a * m[M_BASE]) for a, m in zip(a_ms, mask_refs)]
    ts = [m[M_EYE] + x for x, m in zip(xs, mask_refs)]
    ps = xs
    for _ in range(int(math.log2(INV_BASE)) - 1):
        p_bs = [p.astype(BF16) for p in ps]
        ps = [_dot(p_b, p_b) for p_b in p_bs]
        ts = [t + _dot(t.astype(BF16), p.astype(BF16)) for t, p in zip(ts, ps)]
    for lvl in range(len(JOIN_SIZES)):
        t_bs = [t.astype(BF16) for t in ts]
        joins = [(a * m[M_JOIN + lvl]).astype(BF16) for a, m in zip(a_ms, mask_refs)]
        mids = [_dot(t_b, j).astype(BF16) for t_b, j in zip(t_bs, joins)]
        ts = [t - _dot(mid, t_b) for t, mid, t_b in zip(ts, mids, t_bs)]
    return ts


def _block_inverse(a_m, mask_ref):
    x = -(a_m * mask_ref[M_BASE])
    t = mask_ref[M_EYE] + x
    p = x
    for _ in range(int(math.log2(INV_BASE)) - 1):
        p_b = p.astype(BF16)
        p = _dot(p_b, p_b)
        t = t + _dot(t.astype(BF16), p.astype(BF16))
    for lvl in range(len(JOIN_SIZES)):
        t_b = t.astype(BF16)
        t = t - _dot(_dot(t_b, (a_m * mask_ref[M_JOIN + lvl]).astype(BF16)).astype(BF16), t_b)
    return t


def _group_operands(q_ref, k_ref, v_ref, scv, grp):
    heads = [GROUP * grp + t for t in range(GROUP)]
    tiles = lambda ref: [ref[:, h * HEAD_DIM:(h + 1) * HEAD_DIM] for h in heads]
    col = lambda base: [scv[:, base + h:base + h + 1] for h in heads]
    egc = [scv[0:1, L_EGC + h:L_EGC + h + 1] for h in heads]
    return heads, tiles(q_ref), tiles(k_ref), tiles(v_ref), col(L_BETA), col(L_G), col(L_EG), col(L_EKD), egc


def _delta_fwd(q, k, v, scal, grow, *, name):
    s = q.shape[0]
    n = s // CHUNK
    hd_all = HEADS * HEAD_DIM

    def body(*refs):
        ins, outs, (state, mask_ref) = refs[:10], refs[10:18], refs[18:]

        @pl.when(pl.program_id(0) == 0)
        def _():
            state[...] = jnp.zeros_like(state)
            for d in range(2):
                _write_group_masks(mask_ref.at[d], 1 - 2 * d, N_MASKS)

        chains = []
        for d in range(2):
            q_ref, k_ref, v_ref, sc_ref, gr_ref = ins[5 * d:5 * d + 5]
            scv = sc_ref[0]
            for grp in range(N_GROUPS):
                chains.append(dict(
                    d=d, grp=grp, gr_ref=gr_ref, out=outs[4 * d:4 * d + 4], state=state.at[d], masks=mask_ref.at[d],
                    ops=_group_operands(q_ref, k_ref, v_ref, scv, grp)))
        for ch in chains:
            heads, qs, ks, vs, beta, gcol, eg, ekd, egc = ch["ops"]
            ch["dm"] = _group_decay(_stack(gcol), ch["gr_ref"][0, 0, ch["grp"]:ch["grp"] + 1, :], ch["masks"])
            ch["k_b"] = _stack(ks).astype(BF16)
            ch["kb_b"] = _stack([ks[t] * beta[t] for t in range(GROUP)]).astype(BF16)
        for ch in chains:
            ch["a_m"] = _dot_nt(ch["kb_b"], ch["k_b"]) * ch["dm"] * ch["masks"][M_STRICT]
        tinvs = _block_inverse_many([ch["a_m"] for ch in chains], [ch["masks"] for ch in chains])
        for ch, tinv in zip(chains, tinvs):
            heads, qs, ks, vs, beta, gcol, eg, ekd, egc = ch["ops"]
            o_ref, st_ref, t_ref, vn_ref = ch["out"]
            ch["tinv"] = tinv.astype(BF16)
            t_ref[0, ch["grp"]] = ch["tinv"]
            ch["p_b"] = (_dot_nt(_stack(qs).astype(BF16), ch["k_b"]) * ch["dm"]).astype(BF16)
            ch["sh"] = [ch["state"][h] for h in heads]
            ch["sh_b"] = [x.astype(BF16) for x in ch["sh"]]
            for t, h in enumerate(heads):
                st_ref[0, h] = ch["sh_b"][t]
        for ch in chains:
            heads, qs, ks, vs, beta, gcol, eg, ekd, egc = ch["ops"]
            ch["br"] = _stack([beta[t] * (vs[t] - _dot((ks[t] * eg[t]).astype(BF16), ch["sh_b"][t]))
                               for t in range(GROUP)]).astype(BF16)
        for ch in chains:
            ch["vn_b"] = _dot(ch["tinv"], ch["br"]).astype(BF16)
        for ch in chains:
            ch["o_intra"] = _dot(ch["p_b"], ch["vn_b"])
        for ch in chains:
            heads, qs, ks, vs, beta, gcol, eg, ekd, egc = ch["ops"]
            o_ref, st_ref, t_ref, vn_ref = ch["out"]
            for t, h in enumerate(heads):
                rows = slice(t * CHUNK, (t + 1) * CHUNK)
                cols = slice(h * HEAD_DIM, (h + 1) * HEAD_DIM)
                o_ref[:, cols] = _dot((qs[t] * eg[t]).astype(BF16), ch["sh_b"][t]) + ch["o_intra"][rows]
                ch["state"][h] = egc[t] * ch["sh"][t] + _dot_tn((ks[t] * ekd[t]).astype(BF16), ch["vn_b"][rows])
                vn_ref[:, cols] = ch["vn_b"][rows]

    at = [lambda c: c, lambda c: n - 1 - c]
    in_specs, out_specs = [], []
    for d in range(2):
        tok = pl.BlockSpec((CHUNK, hd_all), lambda c, d=d: (at[d](c), 0))
        in_specs += [tok] * 3 + [pl.BlockSpec((1, CHUNK, BA_W), lambda c, d=d: (d, at[d](c), 0)),
                                 pl.BlockSpec((1, 1, N_GROUPS, GROWS), lambda c, d=d: (d, at[d](c), 0, 0))]
        out_specs += [tok, pl.BlockSpec((1, HEADS, HEAD_DIM, HEAD_DIM), lambda c, d=d: (at[d](c), 0, 0, 0)),
                      pl.BlockSpec((1, N_GROUPS, GROWS, GROWS), lambda c, d=d: (at[d](c), 0, 0, 0)), tok]
    per_dir_shape = [jax.ShapeDtypeStruct((s, hd_all), F32),
                     jax.ShapeDtypeStruct((n, HEADS, HEAD_DIM, HEAD_DIM), BF16),
                     jax.ShapeDtypeStruct((n, N_GROUPS, GROWS, GROWS), BF16),
                     jax.ShapeDtypeStruct((s, hd_all), BF16)]
    outs = pl.pallas_call(
        body, name=name, grid=(n,), in_specs=in_specs, out_specs=out_specs, out_shape=per_dir_shape * 2,
        scratch_shapes=[pltpu.VMEM((2, HEADS, HEAD_DIM, HEAD_DIM), F32),
                        pltpu.VMEM((2, N_MASKS, GROWS, GROWS), F32)],
        compiler_params=_params(("arbitrary",)),
    )(*([q, k, v, scal, grow] * 2))
    return tuple((outs[i], outs[4 + i]) for i in range(4))


def _delta_bwd(q, k, v, scal, grow, states, tinv, vn, do, *, name, carry=None):
    s = q.shape[0]
    n = s // CHUNK
    hd_all = HEADS * HEAD_DIM

    grp_rows = [slice(t * CHUNK, (t + 1) * CHUNK) for t in range(GROUP)]
    per_head = lambda fn: _stack([fn(t) for t in range(GROUP)])

    def body(*refs):
        ins, outs, (dstate, mask_ref) = refs[:18], refs[18:28], refs[28:]

        @pl.when(pl.program_id(0) == 0)
        def _():
            dstate[...] = jnp.zeros_like(dstate)
            for d in range(2):
                _write_group_masks(mask_ref.at[d], 1 - 2 * d, M_EYE)

        chains = []
        for d in range(2):
            q_ref, k_ref, v_ref, sc_ref, gr_ref, st_ref, t_ref, vn_ref, do_ref = ins[9 * d:9 * d + 9]
            scv = sc_ref[0]
            for grp in range(N_GROUPS):
                c = types.SimpleNamespace(d=d, grp=grp, out=outs[5 * d:5 * d + 5], dstate=dstate.at[d],
                                          masks=mask_ref.at[d])
                (c.heads, qs, ks, c.vs, beta, gcol, eg, ekd, c.egc) = _group_operands(q_ref, k_ref, v_ref, scv, grp)
                c.cols = [slice(h * HEAD_DIM, (h + 1) * HEAD_DIM) for h in c.heads]
                c.dm = _group_decay(_stack(gcol), gr_ref[0, 0, grp:grp + 1, :], c.masks)
                c.dm_strict = c.dm * c.masks[M_STRICT]
                c.beta, c.eg, c.ekd = _stack(beta), _stack(eg), _stack(ekd)
                c.q, c.k = _stack(qs), _stack(ks)
                c.q_b, c.k_b = c.q.astype(BF16), c.k.astype(BF16)
                c.kb_b = (c.k * c.beta).astype(BF16)
                c.kg, c.qg, c.kd = c.k * c.eg, c.q * c.eg, c.k * c.ekd
                c.kg_b, c.qg_b, c.kd_b = c.kg.astype(BF16), c.qg.astype(BF16), c.kd.astype(BF16)
                c.vn_b = _stack([vn_ref[:, cc] for cc in c.cols])
                c.do_b = _stack([do_ref[:, cc] for cc in c.cols]).astype(BF16)
                c.sh_b = [st_ref[0, h] for h in c.heads]
                c.dsp = [c.dstate[h] for h in c.heads]
                c.dsp_b = [x.astype(BF16) for x in c.dsp]
                c.t_b = t_ref[0, grp]
                chains.append(c)
        for c in chains:
            c.kk = _dot_nt(c.kb_b, c.k_b)
            c.qk = _dot_nt(c.q_b, c.k_b)
        for c in chains:
            c.r = per_head(lambda t: c.vs[t] - _dot(c.kg_b[grp_rows[t]], c.sh_b[t]))
            c.kd_ds = per_head(lambda t: _dot(c.kd_b[grp_rows[t]], c.dsp_b[t]))
        for c in chains:
            c.dvn_b = (_dot_tn((c.qk * c.dm).astype(BF16), c.do_b) + c.kd_ds).astype(BF16)
        for c in chains:
            c.db = _dot_tn(c.t_b, c.dvn_b)
        for c in chains:
            c.dr = c.db * c.beta
            c.dbeta = jnp.sum(c.db * c.r, axis=-1, keepdims=True)
            c.dr_b, c.db_b = c.dr.astype(BF16), c.db.astype(BF16)
        for c in chains:
            c.dkg = -per_head(lambda t: _dot_nt(c.dr_b[grp_rows[t]], c.sh_b[t]))
            c.dqg = per_head(lambda t: _dot_nt(c.do_b[grp_rows[t]], c.sh_b[t]))
            c.dkd = per_head(lambda t: _dot_nt(c.vn_b[grp_rows[t]], c.dsp_b[t]))
        for c in chains:
            c.dpm = _dot_nt(c.do_b, c.vn_b) * c.dm
            c.dam = -_dot_nt(c.db_b, c.vn_b) * c.dm_strict
            c.dpm_b, c.dam_b = c.dpm.astype(BF16), c.dam.astype(BF16)
        for c in chains:
            c.dkb = _dot(c.dam_b, c.k_b)
            c.dq = c.dqg * c.eg + _dot(c.dpm_b, c.k_b)
        for c in chains:
            c.dk = (c.dkg * c.eg + c.dkd * c.ekd + _dot_tn(c.dpm_b, c.q_b) + _dot_tn(c.dam_b, c.kb_b)
                    + c.dkb * c.beta)
        lane = lax.broadcasted_iota(jnp.int32, (CHUNK, BA_W), 1)
        row = lax.broadcasted_iota(jnp.int32, (CHUNK, 1), 0)
        dsc_acc = [jnp.zeros((CHUNK, BA_W), F32) for _ in range(2)]
        for c in chains:
            dq_ref, dk_ref, dv_ref, dsc_ref, dgr_ref = c.out
            end_row = CHUNK - 1 if c.d == 0 else 0
            dbeta = c.dbeta + jnp.sum(c.dkb * c.k, axis=-1, keepdims=True)
            m = c.dpm * c.qk + c.dam * c.kk
            kd_term = jnp.sum(c.dkd * c.kd, axis=-1, keepdims=True)
            dgcol = (jnp.sum(c.dqg * c.qg, axis=-1, keepdims=True) + jnp.sum(c.dkg * c.kg, axis=-1, keepdims=True)
                     - kd_term + jnp.sum(m, axis=-1, keepdims=True))
            dgr_ref[0, c.grp:c.grp + 1, :] = -jnp.sum(m, axis=0, keepdims=True)
            for t, h in enumerate(c.heads):
                rows, cols = grp_rows[t], c.cols[t]
                dq_ref[:, cols] = c.dq[rows]
                dk_ref[:, cols] = c.dk[rows]
                dv_ref[:, cols] = c.dr[rows]
                dg_end = jnp.sum(kd_term[rows]) + c.egc[t] * jnp.sum(c.dsp[t] * c.sh_b[t].astype(F32))
                dgcol_h = dgcol[rows] + jnp.where(row == end_row, dg_end, 0.0)
                dsc_acc[c.d] = jnp.where(lane == L_BETA + h, dbeta[rows], dsc_acc[c.d])
                dsc_acc[c.d] = jnp.where(lane == L_G + h, dgcol_h, dsc_acc[c.d])
                c.dstate[h] = (_dot_tn(c.qg_b[rows], c.do_b[rows]) + c.egc[t] * c.dsp[t]
                               - _dot_tn(c.kg_b[rows], c.dr_b[rows]))
        for d in range(2):
            outs[5 * d + 3][...] = dsc_acc[d]

    at = [lambda c: n - 1 - c, lambda c: c]
    in_specs, out_specs, args = [], [], []
    for d in range(2):
        tok = pl.BlockSpec((CHUNK, hd_all), lambda c, d=d: (at[d](c), 0))
        in_specs += [tok] * 3 + [pl.BlockSpec((1, CHUNK, BA_W), lambda c, d=d: (d, at[d](c), 0)),
                                 pl.BlockSpec((1, 1, N_GROUPS, GROWS), lambda c, d=d: (d, at[d](c), 0, 0)),
                                 pl.BlockSpec((1, HEADS, HEAD_DIM, HEAD_DIM), lambda c, d=d: (at[d](c), 0, 0, 0)),
                                 pl.BlockSpec((1, N_GROUPS, GROWS, GROWS), lambda c, d=d: (at[d](c), 0, 0, 0)),
                                 tok, tok]
        args += [q, k, v, scal, grow, states[d], tinv[d], vn[d], do]
        out_specs += [tok] * 3 + [pl.BlockSpec((CHUNK, BA_W), lambda c, d=d: (at[d](c), 0)),
                                  pl.BlockSpec((1, N_GROUPS, GROWS), lambda c, d=d: (at[d](c), 0, 0))]
    per_dir_shape = ([jax.ShapeDtypeStruct((s, hd_all), F32)] * 3
                     + [jax.ShapeDtypeStruct((s, BA_W), F32), jax.ShapeDtypeStruct((n, N_GROUPS, GROWS), F32)])
    res = _launch(
        body, carry, tuple(args), name=name, grid=(n,), in_specs=in_specs, out_specs=out_specs,
        out_shape=per_dir_shape * 2,
        scratch_shapes=[pltpu.VMEM((2, HEADS, HEAD_DIM, HEAD_DIM), F32), pltpu.VMEM((2, M_EYE, GROWS, GROWS), F32)],
        sem=("arbitrary",))
    outs, got = res if carry is not None else (res, None)
    paired = tuple((outs[i], outs[5 + i]) for i in range(5))
    return paired if carry is None else (paired, got)


def _gate_norm_fwd(o2, proj, dnw, *, name, ts=512):
    s = o2[0].shape[0]
    hd_all = HEADS * HEAD_DIM

    def body(of_ref, ob_ref, z_ref, w_ref, y_ref):
        w = w_ref[...]
        for hd in range(HEADS):
            cols = slice(hd * HEAD_DIM, (hd + 1) * HEAD_DIM)
            seg = of_ref[:, cols] + ob_ref[:, cols]
            r = lax.rsqrt(jnp.mean(seg * seg, axis=-1, keepdims=True) + EPS)
            z = z_ref[:, cols].astype(F32)
            y_ref[:, cols] = ((seg * r * w) * (z * _sigmoid(z))).astype(BF16)

    tile = pl.BlockSpec((ts, hd_all), lambda i: (i, 0))
    return pl.pallas_call(
        body, name=name, grid=(s // ts,),
        in_specs=[tile, tile, pl.BlockSpec((ts, Z_W), lambda i: (i, Z_OFF // Z_W)), _row(HEAD_DIM)],
        out_specs=tile,
        out_shape=jax.ShapeDtypeStruct((s, hd_all), BF16),
        compiler_params=_params(("parallel",)),
    )(o2[0], o2[1], proj, dnw)


def _gate_norm_bwd(dyb, o2, proj, dnw, *, name, ts=512):
    s = o2[0].shape[0]
    hd_all = HEADS * HEAD_DIM

    def body(dy_ref, of_ref, ob_ref, z_ref, w_ref, do_ref, dz_ref, dw_ref):
        w = w_ref[...]
        dw = jnp.zeros((1, HEAD_DIM), F32)
        for hd in range(HEADS):
            cols = slice(hd * HEAD_DIM, (hd + 1) * HEAD_DIM)
            seg = of_ref[:, cols] + ob_ref[:, cols]
            r = lax.rsqrt(jnp.mean(seg * seg, axis=-1, keepdims=True) + EPS)
            xhat = seg * r
            z = z_ref[:, cols].astype(F32)
            sg = _sigmoid(z)
            dy = dy_ref[:, cols]
            dnrm = dy * (z * sg)
            dz_ref[:, cols] = (dy * (xhat * w) * (sg * (1.0 + z * (1.0 - sg)))).astype(BF16)
            dw = dw + jnp.sum(dnrm * xhat, axis=0, keepdims=True)
            dxhat = dnrm * w
            do_ref[:, cols] = r * (dxhat - xhat * jnp.mean(dxhat * xhat, axis=-1, keepdims=True))

        @pl.when(pl.program_id(0) == 0)
        def _():
            dw_ref[...] = jnp.zeros_like(dw_ref)

        dw_ref[...] += dw

    tile = pl.BlockSpec((ts, hd_all), lambda i: (i, 0))
    return pl.pallas_call(
        body, name=name, grid=(s // ts,),
        in_specs=[tile, tile, tile, pl.BlockSpec((ts, Z_W), lambda i: (i, Z_OFF // Z_W)), _row(HEAD_DIM)],
        out_specs=[tile, tile, _row(HEAD_DIM)],
        out_shape=[jax.ShapeDtypeStruct((s, hd_all), F32), jax.ShapeDtypeStruct((s, hd_all), BF16),
                   jax.ShapeDtypeStruct((1, HEAD_DIM), F32)],
        compiler_params=_params(("arbitrary",)),
    )(dyb, o2[0], o2[1], proj, dnw)


def _merge_fwd(ya, yb, proj, wa, wb, wo, h, g, *, name, ts=512):
    s, d = h.shape

    def body(ya_ref, yb_ref, gt_ref, wa_ref, wb_ref, wo_ref, h_ref, g_ref, pa_ref, pb_ref, mix_ref, ho_ref):
        pa = _dot(ya_ref[...], wa_ref[...])
        pb = _dot(yb_ref[...], wb_ref[...])
        pa_ref[...] = pa.astype(BF16)
        pb_ref[...] = pb.astype(BF16)
        merged = (_sigmoid(gt_ref[:, :d].astype(F32)) * pa + _sigmoid(gt_ref[:, d:].astype(F32)) * pb)
        mix = _dot(merged.astype(BF16), wo_ref[...])
        mix_ref[...] = mix.astype(BF16)
        ho_ref[...] = h_ref[...] + g_ref[...] * mix

    tile = pl.BlockSpec((ts, d), lambda i: (i, 0))
    return pl.pallas_call(
        body, name=name, grid=(s // ts,),
        in_specs=[pl.BlockSpec((ts, CONV_A), lambda i: (i, 0)), tile,
                  pl.BlockSpec((ts, GATE_W), lambda i: (i, GATE_OFF // GATE_W)),
                  _resident(wa.shape), _resident(wb.shape), _resident(wo.shape), tile, _row(d)],
        out_specs=[tile, tile, tile, tile],
        out_shape=[jax.ShapeDtypeStruct((s, d), BF16)] * 3 + [jax.ShapeDtypeStruct((s, d), F32)],
        compiler_params=_params(("parallel",)),
    )(ya, yb, proj, wa, wb, wo, h, g)


def _merge_bwd(dh, g, mix, pa, pb, proj, wa, wb, wo, *, name, ts=256):
    s, d = dh.shape

    def body(dh_ref, g_ref, mix_ref, pa_ref, pb_ref, gt_ref, wa_ref, wb_ref, wo_ref,
             dmix_ref, mg_ref, dpa_ref, dpb_ref, dgt_ref, dya_ref, dyb_ref, dg_ref):
        dh_v = dh_ref[...]
        dmix = (g_ref[...] * dh_v).astype(BF16)
        dmix_ref[...] = dmix

        @pl.when(pl.program_id(0) == 0)
        def _():
            dg_ref[...] = jnp.zeros_like(dg_ref)

        dg_ref[...] += jnp.sum(dh_v * mix_ref[...].astype(F32), axis=0, keepdims=True)
        dmerged = _dot_nt(dmix, wo_ref[...])
        pa = pa_ref[...].astype(F32)
        pb = pb_ref[...].astype(F32)
        sa = _sigmoid(gt_ref[:, :d].astype(F32))
        sb = _sigmoid(gt_ref[:, d:].astype(F32))
        mg_ref[...] = (sa * pa + sb * pb).astype(BF16)
        dpa = (dmerged * sa).astype(BF16)
        dpb = (dmerged * sb).astype(BF16)
        dpa_ref[...] = dpa
        dpb_ref[...] = dpb
        dgt_ref[:, :d] = (dmerged * pa * sa * (1.0 - sa)).astype(BF16)
        dgt_ref[:, d:] = (dmerged * pb * sb * (1.0 - sb)).astype(BF16)
        dya_ref[...] = _dot_nt(dpa, wa_ref[...])
        dyb_ref[...] = _dot_nt(dpb, wb_ref[...])

    tile = pl.BlockSpec((ts, d), lambda i: (i, 0))
    return pl.pallas_call(
        body, name=name, grid=(s // ts,),
        in_specs=[tile, _row(d), tile, tile, tile,
                  pl.BlockSpec((ts, GATE_W), lambda i: (i, GATE_OFF // GATE_W)),
                  _resident(wa.shape), _resident(wb.shape), _resident(wo.shape)],
        out_specs=[tile, tile, tile, tile, pl.BlockSpec((ts, GATE_W), lambda i: (i, 0)),
                   pl.BlockSpec((ts, CONV_A), lambda i: (i, 0)), tile, _row(d)],
        out_shape=[jax.ShapeDtypeStruct((s, d), BF16)] * 4
                  + [jax.ShapeDtypeStruct((s, GATE_W), BF16), jax.ShapeDtypeStruct((s, CONV_A), F32),
                     jax.ShapeDtypeStruct((s, d), F32), jax.ShapeDtypeStruct((1, d), F32)],
        compiler_params=_params(("arbitrary",)),
    )(dh, g, mix, pa, pb, proj, wa, wb, wo)


def _final_fwd_bwd(h, nw, target, *, name, ts=512):
    s, d = h.shape

    def body(h_ref, nw_ref, t_ref, loss_ref, dh_ref, dnw_ref):
        x = h_ref[...]
        w = nw_ref[...]
        r = lax.rsqrt(jnp.mean(x * x, axis=-1, keepdims=True) + EPS)
        xhat = x * r
        e = xhat * w - t_ref[...]
        part = 0.5 * jnp.sum(jnp.mean(e * e, axis=-1, keepdims=True))
        dy = e * (1.0 / d)
        dxhat = dy * w
        dh_ref[...] = r * (dxhat - xhat * jnp.mean(dxhat * xhat, axis=-1, keepdims=True))

        @pl.when(pl.program_id(0) == 0)
        def _():
            loss_ref[...] = jnp.zeros_like(loss_ref)
            dnw_ref[...] = jnp.zeros_like(dnw_ref)

        loss_ref[...] += jnp.broadcast_to(part, loss_ref.shape)
        dnw_ref[...] += jnp.sum(dy * xhat, axis=0, keepdims=True)

    tile = pl.BlockSpec((ts, d), lambda i: (i, 0))
    return pl.pallas_call(
        body, name=name, grid=(s // ts,),
        in_specs=[tile, _row(d), tile],
        out_specs=[_row(128), tile, _row(d)],
        out_shape=[jax.ShapeDtypeStruct((1, 128), F32), jax.ShapeDtypeStruct((s, d), F32),
                   jax.ShapeDtypeStruct((1, d), F32)],
        compiler_params=_params(("arbitrary",)),
    )(h, nw, target)


EXT = 8


def _l2norm_heads_bwd(act, dout, scale):
    outs = []
    for hd in range(HEADS):
        cols = slice(hd * HEAD_DIM, (hd + 1) * HEAD_DIM)
        seg = act[:, cols]
        nrm = lax.rsqrt(jnp.sum(seg * seg, axis=-1, keepdims=True) + EPS)
        yhat = seg * nrm
        dsg = dout[:, cols]
        outs.append((scale * nrm) * (dsg - yhat * jnp.sum(yhat * dsg, axis=-1, keepdims=True)))
    return jnp.concatenate(outs, axis=-1)


def _conv_bwd(dq2, dk2, dv2, dya, proj, conv_dn, conv_a, *, name, ts=256, carry=None):
    s = proj.shape[0]
    hd = HEADS * HEAD_DIM
    nt = s // ts
    te = ts + 2 * EXT
    kdn, ka = conv_dn.shape[0], conv_a.shape[0]

    def body(*refs):
        (qp_ref, qc_ref, qn_ref, ap_ref, ac_ref, an_ref) = refs[0:6]
        d3 = refs[6:24]
        (yp_ref, yc_ref, yn_ref, wdn_ref, wa_ref) = refs[24:29]
        (dqkv_ref, da_ref, dwdn_ref, dwa_ref) = refs[29:33]
        xs_ref, dps_ref, xa_ref, dca_ref = refs[33:37]
        i = pl.program_id(0)
        first, last = i == 0, i == nt - 1

        @pl.when(first)
        def _():
            dwdn_ref[...] = jnp.zeros_like(dwdn_ref)
            dwa_ref[...] = jnp.zeros_like(dwa_ref)

        rowe = lax.broadcasted_iota(jnp.int32, (te, 1), 0)
        inside = ~((first & (rowe < EXT)) | (last & (rowe >= EXT + ts)))
        _fill_halo(xs_ref, qp_ref, qc_ref, qn_ref, first, last)
        wdn = wdn_ref[...]
        for part in range(3):
            cols = slice(part * hd, (part + 1) * hd)
            pre = _dwconv_rows(xs_ref, wdn[:, cols], HALO - EXT - 2, te, cols)
            sg = _sigmoid(pre)
            act = pre * sg
            pf, cf, nf, pb, cb, nb = d3[6 * part:6 * part + 6]
            dout = jnp.concatenate([pf[...] + pb[...], cf[...] + cb[...], nf[...] + nb[...]], axis=0)
            if part == 0:
                dact = _l2norm_heads_bwd(act, dout, Q_SCALE)
            elif part == 1:
                dact = _l2norm_heads_bwd(act, dout, 1.0)
            else:
                dact = dout
            dpre = jnp.where(inside, dact * (sg * (1.0 + pre * (1.0 - sg))), 0.0)
            dps_ref[:, cols] = dpre
            acc = wdn[0:1, cols] * dps_ref[EXT + 2:EXT + 2 + ts, cols]
            for tap in range(1, kdn):
                acc = acc + wdn[tap:tap + 1, cols] * dps_ref[EXT + 2 - tap:EXT + 2 - tap + ts, cols]
            dqkv_ref[:, cols] = acc.astype(BF16)
            dcur = dps_ref[EXT:EXT + ts, cols]
            for tap in range(kdn):
                dwdn_ref[tap:tap + 1, cols] += jnp.sum(
                    dcur * xs_ref[HALO - 2 + tap:HALO - 2 + tap + ts, cols], axis=0, keepdims=True)

        cv = lambda r: r[:, CONV_A:2 * CONV_A].astype(F32) * r[:, 2 * CONV_A:].astype(F32)
        _fill_halo(xa_ref, ap_ref, ac_ref, an_ref, first, last, fn=cv)
        wa = wa_ref[...]
        gate_b = jnp.concatenate([ap_ref[HALO - EXT:, 0:CONV_A], ac_ref[:, 0:CONV_A], an_ref[0:EXT, 0:CONV_A]],
                                 axis=0).astype(F32)
        dya_e = jnp.concatenate([yp_ref[...], yc_ref[...], yn_ref[...]], axis=0)
        dca_ref[...] = jnp.where(inside, dya_e * gate_b, 0.0)
        conv = _dwconv_rows(xa_ref, wa, HALO - 1, ts, slice(0, CONV_A))
        acc = wa[0:1, :] * dca_ref[EXT + 1:EXT + 1 + ts, :]
        for tap in range(1, ka):
            acc = acc + wa[tap:tap + 1, :] * dca_ref[EXT + 1 - tap:EXT + 1 - tap + ts, :]
        gc = ac_ref[:, CONV_A:2 * CONV_A].astype(F32)
        val = ac_ref[:, 2 * CONV_A:].astype(F32)
        da_ref[:, 0:CONV_A] = (yc_ref[...] * conv).astype(BF16)
        da_ref[:, CONV_A:2 * CONV_A] = (acc * val).astype(BF16)
        da_ref[:, 2 * CONV_A:] = (acc * gc).astype(BF16)
        dcur = dca_ref[EXT:EXT + ts, :]
        for tap in range(ka):
            dwa_ref[tap:tap + 1, :] += jnp.sum(
                dcur * xa_ref[HALO - 1 + tap:HALO - 1 + tap + ts, :], axis=0, keepdims=True)

    cot = [arr for pair in (dq2, dk2, dv2) for arr in pair for _ in range(3)]
    return _launch(
        body, carry, (proj, proj, proj, proj, proj, proj, *cot, dya, dya, dya, conv_dn, conv_a),
        name=name, grid=(nt,),
        in_specs=(_halo_specs(ts, QKV_W, 0, s) + _halo_specs(ts, A_W, A_OFF // A_W, s)
                  + _halo_specs(ts, hd, 0, s, rows=EXT) * 6 + _halo_specs(ts, CONV_A, 0, s, rows=EXT)
                  + [_resident(conv_dn.shape), _resident(conv_a.shape)]),
        out_specs=[pl.BlockSpec((ts, QKV_W), lambda i: (i, 0)), pl.BlockSpec((ts, A_W), lambda i: (i, 0)),
                   pl.BlockSpec((8, QKV_W), lambda i: (0, 0)), pl.BlockSpec((8, CONV_A), lambda i: (0, 0))],
        out_shape=[jax.ShapeDtypeStruct((s, QKV_W), BF16), jax.ShapeDtypeStruct((s, A_W), BF16),
                   jax.ShapeDtypeStruct((8, QKV_W), F32), jax.ShapeDtypeStruct((8, CONV_A), F32)],
        scratch_shapes=[pltpu.VMEM((ts + 2 * HALO, QKV_W), F32), pltpu.VMEM((te, QKV_W), F32),
                        pltpu.VMEM((ts + 2 * HALO, CONV_A), F32), pltpu.VMEM((te, CONV_A), F32)],
        sem=("arbitrary",))


IN_A = (0, 1536)
IN_QKV = (1536, 4608)
IN_Z = (4608, 5632)
IN_BA = 5632
IN_GATE = (5664, 7712)
IN_COLS = 7712
G_REPL = 4


def _split_w_in(w_in):
    sl = lambda ab: w_in[:, ab[0]:ab[1]]
    w_main = jnp.concatenate([sl(IN_QKV), sl(IN_Z), sl(IN_GATE), sl(IN_A)], axis=1)
    blocks = []
    for d in range(2):
        beta = w_in[:, IN_BA + 8 * d:IN_BA + 8 * d + 8]
        alpha = w_in[:, IN_BA + 16 + 8 * d:IN_BA + 24 + 8 * d]
        pad = jnp.zeros((w_in.shape[0], BA_W - 8 - 8 * G_REPL), w_in.dtype)
        blocks += [beta] + [alpha] * G_REPL + [pad]
    return w_main, jnp.concatenate(blocks, axis=1)


def _merge_dw_in(dw_qkv, dw_z, dw_gate, dw_a, dw_ba):
    ba = [dw_ba[:, 0:8], dw_ba[:, BA_W:BA_W + 8], dw_ba[:, 8:16], dw_ba[:, BA_W + 8:BA_W + 16]]
    return jnp.concatenate([dw_a, dw_qkv, dw_z] + ba + [dw_gate], axis=1)


def _decay_rows(a_log_fwd, dt_bias_fwd, a_log_bwd, dt_bias_bwd):
    def rows(f, b):
        out = []
        for vec in (f, b):
            vec = vec.reshape(HEADS)
            out.append(jnp.concatenate([jnp.zeros((8,), F32)] + [vec] * G_REPL
                                       + [jnp.zeros((BA_W - 8 - 8 * G_REPL,), F32)])[None])
        return jnp.stack(out)
    return rows(a_log_fwd, a_log_bwd), rows(dt_bias_fwd, dt_bias_bwd)


def _local_step(x, target, mod9, wt, comm):
    s, d = x.shape
    n = s // CHUNK
    wt = dict(wt)
    sh1, sc1, g1, sh2, sc2, g2, sh3, sc3, g3 = [mod9[i:i + 1] for i in range(9)]
    alog, dtb = _decay_rows(wt["a_log_fwd"], wt["dt_bias_fwd"], wt["a_log_bwd"], wt["dt_bias_bwd"])

    (u1, a1, b1, f1), got = comm.gather(
        ["w_ffn1_down", "w_in"],
        lambda c: _ffn_up_fwd(x, wt["norm_ffn1"], sc1, sh1, wt["w_ffn1_up"], name="ffn1_up", carry=c))
    wt.update(got)
    w_main, w_ba = _split_w_in(wt["w_in"])
    y1, h1 = _ffn_down_fwd(f1, wt["w_ffn1_down"], x, g1, name="ffn1_down")
    (u2, proj, ba), got = comm.gather(
        ["w_a_out", "w_b_out", "w_out", "w_ffn2_up", "w_ffn2_down"],
        lambda c: _in_proj_fwd(h1, wt["norm_mix"], sc2, sh2, w_main, w_ba, name="in_proj", carry=c))
    wt.update(got)
    scal = _scal_fwd(ba, alog, dtb, name="scal_fwd")
    grow = scal[:, :, L_G:L_G + 8].reshape(2, n, CHUNK, HEADS).transpose(0, 1, 3, 2).reshape(
        2, n, N_GROUPS, GROWS)
    q, k, v, ya = _conv_fwd(proj, wt["conv_dn"], wt["conv_a"], name="conv_fwd")
    o2, states, tinv, vn = _delta_fwd(q, k, v, scal, grow, name="delta_fwd")
    yb = _gate_norm_fwd(o2, proj, wt["dn_norm"], name="gate_norm_fwd")
    pa, pb, mix, h2 = _merge_fwd(ya, yb, proj, wt["w_a_out"], wt["w_b_out"], wt["w_out"], h1, g2,
                                 name="merge_fwd")
    u3, a3, b3, f3 = _ffn_up_fwd(h2, wt["norm_ffn2"], sc3, sh3, wt["w_ffn2_up"], name="ffn2_up")
    y3, h3 = _ffn_down_fwd(f3, wt["w_ffn2_down"], h2, g3, name="ffn2_down")
    loss, dh3, dnorm_final = _final_fwd_bwd(h3, wt["norm_final"], target, name="final")

    dy3, dab3, dg3 = _ffn_bwd_act(dh3, g3, y3, a3, b3, wt["w_ffn2_down"], name="ffn2_bwd_act")
    dh2, dn3, dsc3, dsh3 = _norm_mod_matmul_bwd([(dab3, wt["w_ffn2_up"])], h2, wt["norm_ffn2"], sc3, dh3,
                                                name="ffn2_bwd_up")
    gw = {}
    gw["w_ffn2_up"] = _matmul_tn(u3, dab3, name="dw_ffn2_up", tm=1024, tn=1408)
    gw["w_ffn2_down"] = _matmul_tn(f3, dy3, name="dw_ffn2_down", tm=1408, tn=1024)

    dmix, merged, dpa, dpb, dgates, dya, dyb, dg2 = _merge_bwd(
        dh2, g2, mix, pa, pb, proj, wt["w_a_out"], wt["w_b_out"], wt["w_out"], name="merge_bwd")
    gw["w_out"] = _matmul_tn(merged, dmix, name="dw_out", tm=1024, tn=1024)
    gw["w_a_out"] = _matmul_tn(ya, dpa, name="dw_a_out", tm=512, tn=1024)
    gw["w_b_out"] = _matmul_tn(yb, dpb, name="dw_b_out", tm=1024, tn=1024)
    do, dz, ddn = _gate_norm_bwd(dyb, o2, proj, wt["dn_norm"], name="gate_norm_bwd")
    recv = {}
    (dq2, dk2, dv2, dscal, drow), got = comm.scatter(
        {nm: gw.pop(nm) for nm in ("w_ffn2_up", "w_ffn2_down")},
        lambda c: _delta_bwd(q, k, v, scal, grow, states, tinv, vn, do, name="delta_bwd", carry=c))
    recv.update(got)
    drow_p = jnp.pad(jnp.stack(drow).reshape(2, n, HEADS, CHUNK).transpose(0, 1, 3, 2).reshape(2, s, HEADS),
                     ((0, 0), (0, 0), (L_G, BA_W - L_G - HEADS)))
    dba, dalog, ddtb = _scal_bwd(jnp.stack(dscal), drow_p, ba, alog, dtb, name="scal_bwd")
    (dqkv, dbr_a, dconv_dn, dconv_a), got = comm.scatter(
        {nm: gw.pop(nm) for nm in ("w_out", "w_a_out", "w_b_out")},
        lambda c: _conv_bwd(dq2, dk2, dv2, dya, proj, wt["conv_dn"], wt["conv_a"], name="conv_bwd", carry=c))
    recv.update(got)
    dw_in = _merge_dw_in(
        _matmul_tn(u2, dqkv, name="dw_in_qkv", tm=1024, tn=1536),
        _matmul_tn(u2, dz, name="dw_in_z", tm=1024, tn=1024),
        _matmul_tn(u2, dgates, name="dw_in_gate", tm=1024, tn=1024),
        _matmul_tn(u2, dbr_a, name="dw_in_a", tm=1024, tn=1536),
        _matmul_tn(u2, dba, name="dw_in_ba", tm=1024, tn=2 * BA_W))
    (dh1, dn2, dsc2, dsh2), got = comm.scatter(
        {"w_in": dw_in},
        lambda c: _norm_mod_matmul_bwd(
            [(dqkv, w_main, 0), (dz, w_main, Z_OFF // Z_W), (dgates, w_main, GATE_OFF // GATE_W),
             (dbr_a, w_main, A_OFF // A_W), (dba, w_ba)],
            h1, wt["norm_mix"], sc2, dh2, name="in_proj_bwd", carry=c))
    recv.update(got)

    dy1, dab1, dg1 = _ffn_bwd_act(dh1, g1, y1, a1, b1, wt["w_ffn1_down"], name="ffn1_bwd_act")
    dw_down1 = _matmul_tn(f1, dy1, name="dw_ffn1_down", tm=1408, tn=1024)
    dw_up1, got = comm.scatter(
        {"w_ffn1_down": dw_down1},
        lambda c: _matmul_tn(u1, dab1, name="dw_ffn1_up", tm=1024, tn=1408, carry=c))
    recv.update(got)
    (dx, dn1, dsc1, dsh1), got = comm.scatter(
        {"w_ffn1_up": dw_up1},
        lambda c: _norm_mod_matmul_bwd([(dab1, wt["w_ffn1_up"])], x, wt["norm_ffn1"], sc1, dh1,
                                       name="ffn1_bwd_up", carry=c))
    recv.update(got)

    small = {
        "mod": jnp.concatenate([dsh1, dsc1, dg1, dsh2, dsc2, dg2, dsh3, dsc3, dg3], axis=1),
        "norm_ffn1": dn1, "norm_mix": dn2, "norm_ffn2": dn3, "norm_final": dnorm_final,
        "a_log_fwd": dalog[0, :, L_G:L_G + 8], "dt_bias_fwd": ddtb[0, :, L_G:L_G + 8],
        "a_log_bwd": dalog[1, :, L_G:L_G + 8], "dt_bias_bwd": ddtb[1, :, L_G:L_G + 8],
        "dn_norm": ddn,
        "conv_a": dconv_a[0:3].reshape(1, -1), "conv_dn": dconv_dn[0:5].reshape(1, -1),
    }
    return loss, dx, recv, small


def _full_weight(name, g):
    if name in COL_SHARDED + CONV_SHARDED:
        return g.transpose(1, 0, 2).reshape(g.shape[1], -1)
    return g.reshape(-1, g.shape[-1])


def _grad_pieces(name, g):
    g = g.astype(BF16)
    if name in COL_SHARDED:
        return g.reshape(g.shape[0], N_DEV, -1).transpose(1, 0, 2)
    return g.reshape(N_DEV, -1, g.shape[-1])


class _MeshComm:
    def __init__(self, shards):
        self.shards = shards

    def _run(self, xs, carrier, name, gather):
        if carrier is None:
            return None, _exchange(xs, name=name, gather=gather)
        return carrier((xs, gather))

    def gather(self, names, carrier=None, name=None):
        outs, got = self._run([self.shards[nm] for nm in names], carrier, name, True)
        return outs, {nm: _full_weight(nm, g) for nm, g in zip(names, got)}

    def scatter(self, grads, carrier=None, name=None):
        names = list(grads)
        outs, got = self._run([_grad_pieces(nm, grads[nm]) for nm in names], carrier, name, False)
        return outs, dict(zip(names, got))


def _mod_fwd(c_all, w_ada, *, name):
    def body(c_ref, w_ref, o_ref):
        cv = c_ref[...]
        o_ref[...] = _dot3(cv * _sigmoid(cv), w_ref[...])

    return pl.pallas_call(
        body, name=name, out_shape=jax.ShapeDtypeStruct((c_all.shape[0], w_ada.shape[1]), F32),
        compiler_params=_params(),
    )(c_all, w_ada)


def _adamw_math(w, g, m, v):
    m_new = ADAM_B1 * m + (1.0 - ADAM_B1) * g
    v_new = ADAM_B2 * v + (1.0 - ADAM_B2) * (g * g)
    m_hat = m_new / (1.0 - ADAM_B1 ** ADAM_STEP)
    v_hat = v_new / (1.0 - ADAM_B2 ** ADAM_STEP)
    delta = -ADAM_LR * (m_hat / (jnp.sqrt(v_hat) + ADAM_EPS) + ADAM_WD * w)
    return delta, m_new, v_new


def _reduce_adamw(pieces, w, m, v, *, name, tr):
    r, c = w.shape

    def body(p_ref, w_ref, m_ref, v_ref, g_ref, d_ref, mo_ref, vo_ref):
        g = p_ref[0].astype(F32)
        for src in range(1, N_DEV):
            g = g + p_ref[src].astype(F32)
        g_ref[...] = g
        d_ref[...], mo_ref[...], vo_ref[...] = _adamw_math(w_ref[...], g, m_ref[...], v_ref[...])

    tile = pl.BlockSpec((tr, c), lambda i: (i, 0))
    return pl.pallas_call(
        body, name=name, grid=(r // tr,),
        in_specs=[pl.BlockSpec((N_DEV, tr, c), lambda i: (0, i, 0)), tile, tile, tile],
        out_specs=[tile] * 4, out_shape=[jax.ShapeDtypeStruct((r, c), F32)] * 4,
        compiler_params=_params(("parallel",)),
    )(pieces, w, m, v)


def _ada_grad_adamw(c_all_t, dmod_cols, w, m, v, *, name, tr=256):
    r, c = w.shape

    def body(c_ref, dm_ref, w_ref, m_ref, v_ref, g_ref, d_ref, mo_ref, vo_ref):
        cv = c_ref[...]
        act = cv * _sigmoid(cv)
        dm = dm_ref[...]
        g = act[:, 0:1] * dm[0:1, :]
        for b in range(1, N_DEV):
            g = g + act[:, b:b + 1] * dm[b:b + 1, :]
        g_ref[...] = g
        d_ref[...], mo_ref[...], vo_ref[...] = _adamw_math(w_ref[...], g, m_ref[...], v_ref[...])

    tile = pl.BlockSpec((tr, c), lambda i: (i, 0))
    return pl.pallas_call(
        body, name=name, grid=(r // tr,),
        in_specs=[pl.BlockSpec((tr, N_DEV), lambda i: (i, 0)), pl.BlockSpec((N_DEV, c), lambda i: (0, 0)),
                  tile, tile, tile],
        out_specs=[tile] * 4, out_shape=[jax.ShapeDtypeStruct((r, c), F32)] * 4,
        compiler_params=_params(("parallel",)),
    )(c_all_t, dmod_cols, w, m, v)


def _sum_rows(parts, *, name):
    def body(p_ref, o_ref):
        acc = p_ref[0:1, :]
        for src in range(1, N_DEV):
            acc = acc + p_ref[src:src + 1, :]
        o_ref[...] = acc

    return pl.pallas_call(
        body, name=name, out_shape=jax.ShapeDtypeStruct((1, parts.shape[1]), F32), compiler_params=_params(),
    )(parts)


def _adamw_rows(g, w, m, v, *, name):
    def body(g_ref, w_ref, m_ref, v_ref, d_ref, mo_ref, vo_ref):
        d_ref[...], mo_ref[...], vo_ref[...] = _adamw_math(w_ref[...], g_ref[...], m_ref[...], v_ref[...])

    return pl.pallas_call(
        body, name=name, out_shape=[jax.ShapeDtypeStruct(g.shape, F32)] * 3, compiler_params=_params(),
    )(g, w, m, v)


WEIGHTS = ["w_ada", "b_ada", "norm_ffn1", "w_ffn1_up", "w_ffn1_down", "norm_mix", "w_in", "conv_a", "conv_dn",
           "a_log_fwd", "dt_bias_fwd", "a_log_bwd", "dt_bias_bwd", "dn_norm", "w_a_out", "w_b_out", "w_out",
           "norm_ffn2", "w_ffn2_up", "w_ffn2_down", "norm_final"]
COL_SHARDED = ["w_ffn1_up", "w_in", "w_a_out", "w_ffn2_up"]
ROW_SHARDED = ["w_ffn1_down", "w_b_out", "w_out", "w_ffn2_down"]
CONV_SHARDED = ["conv_a", "conv_dn"]
REPLICATED = ["b_ada", "norm_ffn1", "norm_mix", "a_log_fwd", "dt_bias_fwd", "a_log_bwd", "dt_bias_bwd",
              "dn_norm", "norm_ffn2", "norm_final"]
SMALL_ORDER = ["mod", "norm_ffn1", "norm_mix", "norm_ffn2", "norm_final", "a_log_fwd", "dt_bias_fwd",
               "a_log_bwd", "dt_bias_bwd", "dn_norm", "conv_a", "conv_dn"]
REDUCE_ROWS = {"w_ffn1_up": 256, "w_in": 256, "w_a_out": 256, "w_ffn2_up": 256,
               "w_ffn1_down": 176, "w_b_out": 128, "w_out": 128, "w_ffn2_down": 176}


def _pad_lanes(row):
    pad = (-row.shape[1]) % 128
    return jnp.pad(row, ((0, 0), (0, pad)))


def _unstack_cols(g):
    return g.transpose(1, 0, 2).reshape(g.shape[1], -1)


def _stack_cols(w):
    k = w.shape[0]
    return w.reshape(k, N_DEV, -1).transpose(1, 0, 2)


def kernel(x, c, w_ada, b_ada, norm_ffn1, w_ffn1_up, w_ffn1_down, norm_mix, w_in, conv_a, conv_dn, a_log_fwd, dt_bias_fwd, a_log_bwd, dt_bias_bwd, dn_norm, w_a_out, w_b_out, w_out, norm_ffn2, w_ffn2_up, w_ffn2_down, norm_final, loss_target, m_w_ada, m_b_ada, m_norm_ffn1, m_w_ffn1_up, m_w_ffn1_down, m_norm_mix, m_w_in, m_conv_a, m_conv_dn, m_a_log_fwd, m_dt_bias_fwd, m_a_log_bwd, m_dt_bias_bwd, m_dn_norm, m_w_a_out, m_w_b_out, m_w_out, m_norm_ffn2, m_w_ffn2_up, m_w_ffn2_down, m_norm_final, v_w_ada, v_b_ada, v_norm_ffn1, v_w_ffn1_up, v_w_ffn1_down, v_norm_mix, v_w_in, v_conv_a, v_conv_dn, v_a_log_fwd, v_dt_bias_fwd, v_a_log_bwd, v_dt_bias_bwd, v_dn_norm, v_w_a_out, v_w_b_out, v_w_out, v_norm_ffn2, v_w_ffn2_up, v_w_ffn2_down, v_norm_final):
    args = dict(locals())
    w_loc = {n: args[n] for n in WEIGHTS}
    m_loc = {n: args["m_" + n] for n in WEIGHTS}
    v_loc = {n: args["v_" + n] for n in WEIGHTS}
    me = _flat_index(_my_position())
    d_model = x.shape[-1]

    big = COL_SHARDED + ROW_SHARDED
    shards = {n: w_loc[n][0].astype(BF16) for n in big}
    shards.update({n: w_loc[n][0] for n in CONV_SHARDED})
    shards["c"] = c
    comm = _MeshComm(shards)
    wt = comm.gather(["c", "conv_a", "conv_dn", "w_ffn1_up"], name="gather_first")[1]
    c_all = wt.pop("c")
    for n in REPLICATED[1:]:
        wt[n] = w_loc[n].reshape(1, -1)

    mod_cols = _mod_fwd(c_all, w_ada[0], name="mod_fwd")
    mod_all = _exchange([mod_cols], name="gather_mod", gather=True)[0]
    mod_mine = lax.dynamic_index_in_dim(mod_all, me, axis=1, keepdims=False).reshape(1, -1) + b_ada
    mod9 = mod_mine.reshape(9, d_model)

    loss_loc, dx, recv, small = _local_step(x[0], loss_target[0], mod9, wt, comm)
    loss = lax.psum(loss_loc[0, 0], MESH_AXES)

    res = {}
    for n in big:
        res[n] = _reduce_adamw(recv[n], w_loc[n][0], m_loc[n][0], v_loc[n][0], name="adamw_" + n,
                               tr=REDUCE_ROWS[n])

    packed = _pad_lanes(jnp.concatenate([small[n].reshape(1, -1) for n in SMALL_ORDER], axis=1))
    parts = _exchange([packed], name="gather_small", gather=True)[0].reshape(N_DEV, -1)
    total = _sum_rows(parts, name="sum_small")
    off = 0
    gsmall = {}
    for n in SMALL_ORDER:
        size = small[n].size
        gsmall[n] = total[:, off:off + size]
        off += size
    dmod_all = parts[:, 0:9 * d_model]
    ada_cols = w_ada.shape[-1]
    dmod_cols = lax.dynamic_slice_in_dim(dmod_all, me * ada_cols, ada_cols, axis=1)
    res["w_ada"] = _ada_grad_adamw(c_all.T, dmod_cols, w_ada[0], m_w_ada[0], v_w_ada[0], name="adamw_w_ada")
    g_rows = {"b_ada": gsmall["mod"]}
    for n in REPLICATED[1:]:
        g_rows[n] = gsmall[n]
    for n in CONV_SHARDED:
        taps, width = w_loc[n].shape[1], w_loc[n].shape[2]
        full = gsmall[n].reshape(taps, -1)
        g_rows[n] = lax.dynamic_slice_in_dim(full, me * width, width, axis=1).reshape(1, -1)
    row_names = REPLICATED + CONV_SHARDED
    cat = lambda src: _pad_lanes(jnp.concatenate([src[n].reshape(1, -1) for n in row_names], axis=1))
    g_cat = cat(g_rows)
    d_cat, m_cat, v_cat = _adamw_rows(g_cat, cat(w_loc), cat(m_loc), cat(v_loc), name="adamw_small")
    off = 0
    for n in row_names:
        size = w_loc[n].size
        res[n] = tuple(t[:, off:off + size] for t in (g_cat, d_cat, m_cat, v_cat))
        off += size

    outs = [loss, dx[None]]
    for kind in range(4):
        for n in WEIGHTS:
            outs.append(res[n][kind].reshape(w_loc[n].shape))
    return tuple(outs)
```

```python
import functools
import math
import types

import jax
import jax.numpy as jnp
from jax import lax
from jax.experimental import pallas as pl
from jax.experimental.pallas import tpu as pltpu

F32 = jnp.float32
BF16 = jnp.bfloat16
EPS = 1e-6
N_DEV = 8
CHUNK = 64
HEADS = 8
HEAD_DIM = 128
MESH_AXES = ("x", "y", "c")
VMEM_LIMIT_BYTES = 56 * 1024 * 1024

ADAM_LR = 0.001
ADAM_B1 = 0.9
ADAM_B2 = 0.999
ADAM_EPS = 1e-08
ADAM_WD = 0.01
ADAM_STEP = 10


def _params(sem=None):
    return pltpu.CompilerParams(dimension_semantics=sem, vmem_limit_bytes=VMEM_LIMIT_BYTES)


def _row(n):
    return pl.BlockSpec((1, n), lambda *_: (0, 0))


def _resident(shape):
    nd = len(shape)
    return pl.BlockSpec(shape, lambda *_: (0,) * nd, pipeline_mode=pl.Buffered(1))


def _col_window(w, width, col_block):
    return pl.BlockSpec((w.shape[0], width), lambda *_: (0, col_block), pipeline_mode=pl.Buffered(1))


def _sigmoid(x):
    return 1.0 / (1.0 + jnp.exp(-x))


def _dot(a, b):
    return jnp.dot(a, b, preferred_element_type=F32)


def _dot_nt(a, b):
    return lax.dot_general(a, b, (((1,), (1,)), ((), ())), preferred_element_type=F32)


def _dot_tn(a, b):
    return lax.dot_general(a, b, (((0,), (0,)), ((), ())), preferred_element_type=F32)


def _split_bf16(x):
    hi = x.astype(BF16)
    lo = (x - hi.astype(F32)).astype(BF16)
    return hi, lo


def _dot3(a, b, dot=_dot):
    ah, al = a if isinstance(a, tuple) else _split_bf16(a)
    bh, bl = b if isinstance(b, tuple) else _split_bf16(b)
    return dot(ah, bh) + dot(ah, bl) + dot(al, bh)


def _dot_exact(a, b):
    return jnp.dot(a, b, preferred_element_type=F32, precision=lax.Precision.HIGHEST)


def _my_position():
    return tuple(lax.axis_index(a) for a in MESH_AXES)


def _peer(pos, kk):
    return tuple((1 - p) if (kk >> (2 - b)) & 1 else p for b, p in enumerate(pos))


def _flat_index(pos):
    return pos[0] * 4 + pos[1] * 2 + pos[2]


_ANY = pl.BlockSpec(memory_space=pl.ANY)


class _AllToAll:
    def __init__(self, in_refs, out_refs, send_sems, recv_sems, local_sems):
        pos = _my_position()
        me = _flat_index(pos)
        self.copies = []
        for t in range(len(in_refs)):
            self.copies.append(pltpu.make_async_copy(in_refs[t].at[me], out_refs[t].at[me], local_sems.at[t]))
            for kk in range(1, N_DEV):
                peer = _peer(pos, kk)
                self.copies.append(pltpu.make_async_remote_copy(
                    src_ref=in_refs[t].at[_flat_index(peer)], dst_ref=out_refs[t].at[me],
                    send_sem=send_sems.at[t, kk - 1], recv_sem=recv_sems.at[t, kk - 1],
                    device_id=peer, device_id_type=pl.DeviceIdType.MESH))

    def start(self):
        for cp in self.copies:
            cp.start()

    def finish(self):
        for cp in self.copies:
            cp.wait()


class _AllGather:
    def __init__(self, in_refs, out_refs, send_sems, recv_sems, local_sems):
        self.refs = (in_refs, out_refs, send_sems, recv_sems, local_sems)
        x, y, c = _my_position()
        self.me, self.sibling = (x, y, c), (x, y, 1 - c)
        self.chips = [(1 - x, y), (x, 1 - y), (1 - x, 1 - y)]
        self.core = c

    def _copy(self, t, k, block, to, own=False):
        in_refs, out_refs, send_sems, recv_sems, _ = self.refs
        rows = out_refs[t].at[_flat_index(block)]
        return pltpu.make_async_remote_copy(
            src_ref=in_refs[t] if own else rows, dst_ref=rows,
            send_sem=send_sems.at[t, k], recv_sem=recv_sems.at[t, k],
            device_id=to, device_id_type=pl.DeviceIdType.MESH)

    def _local(self, t):
        in_refs, out_refs, _, _, local_sems = self.refs
        return pltpu.make_async_copy(in_refs[t], out_refs[t].at[_flat_index(self.me)], local_sems.at[t])

    def start(self):
        c = self.core
        for t in range(len(self.refs[0])):
            self._local(t).start()
            self._copy(t, 0, self.me, self.sibling, own=True).start()
            for j, chip in enumerate(self.chips):
                self._copy(t, 1 + j, self.me, (*chip, c), own=True).start()

    def finish(self):
        c = self.core
        n_t = len(self.refs[0])
        for t in range(n_t):
            for j, chip in enumerate(self.chips):
                self._copy(t, 1 + j, (*chip, c), self.me).wait_recv()
                self._copy(t, 4 + j, (*chip, c), self.sibling).start()
        for t in range(n_t):
            self._copy(t, 0, self.sibling, self.me).wait_recv()
            for j, chip in enumerate(self.chips):
                self._copy(t, 4 + j, (*chip, 1 - c), self.me).wait_recv()
            self._copy(t, 0, self.me, self.sibling, own=True).wait_send()
            for j, chip in enumerate(self.chips):
                self._copy(t, 1 + j, self.me, (*chip, c), own=True).wait_send()
                self._copy(t, 4 + j, (*chip, c), self.sibling).wait_send()
            self._local(t).wait()


def _exchange_plan(in_refs, out_refs, send_sems, recv_sems, local_sems, gather):
    return (_AllGather if gather else _AllToAll)(in_refs, out_refs, send_sems, recv_sems, local_sems)


def _exchange_shapes(xs, gather):
    out_shape = [jax.ShapeDtypeStruct(((N_DEV,) + x.shape) if gather else x.shape, x.dtype) for x in xs]
    sems = [pltpu.SemaphoreType.DMA((len(xs), N_DEV - 1)), pltpu.SemaphoreType.DMA((len(xs), N_DEV - 1)),
            pltpu.SemaphoreType.DMA((len(xs),))]
    return out_shape, sems


def _exchange(xs, *, name, gather):
    nt = len(xs)

    def body(*refs):
        plan = _exchange_plan(refs[:nt], refs[nt:2 * nt], *refs[2 * nt:], gather)
        plan.start()
        plan.finish()

    out_shape, sems = _exchange_shapes(xs, gather)
    return pl.pallas_call(body, name=name, in_specs=[_ANY] * nt, out_specs=[_ANY] * nt, out_shape=out_shape,
                          scratch_shapes=sems)(*xs)


def _launch(body, carry, args, *, name, grid, in_specs, out_specs, out_shape, scratch_shapes=(), sem):
    single = not isinstance(out_shape, (list, tuple))
    out_specs = [out_specs] if single else list(out_specs)
    out_shape = [out_shape] if single else list(out_shape)
    if carry is None:
        outs = pl.pallas_call(body, name=name, grid=grid, in_specs=list(in_specs), out_specs=out_specs,
                              out_shape=out_shape, scratch_shapes=list(scratch_shapes),
                              compiler_params=_params(sem))(*args)
        return outs[0] if single else outs
    xs, gather = carry
    nt, n_in, n_out, n_scr = len(xs), len(args), len(out_shape), len(scratch_shapes)
    x_shape, sems = _exchange_shapes(xs, gather)

    def wrapped(*refs):
        c_in, x_in = refs[:n_in], refs[n_in:n_in + nt]
        c_out = refs[n_in + nt:n_in + nt + n_out]
        x_out = refs[n_in + nt + n_out:n_in + 2 * nt + n_out]
        scr = refs[n_in + 2 * nt + n_out:]
        ids = [pl.program_id(a) for a in range(len(grid))]
        first = functools.reduce(jnp.logical_and, [i == 0 for i in ids])
        last = functools.reduce(jnp.logical_and, [i == g - 1 for i, g in zip(ids, grid)])
        plan = lambda: _exchange_plan(x_in, x_out, *scr[n_scr:], gather)

        @pl.when(first)
        def _():
            plan().start()

        body(*c_in, *c_out, *scr[:n_scr])

        @pl.when(last)
        def _():
            plan().finish()

    outs = pl.pallas_call(
        wrapped, name=name, grid=grid, in_specs=list(in_specs) + [_ANY] * nt,
        out_specs=out_specs + [_ANY] * nt, out_shape=out_shape + x_shape,
        scratch_shapes=list(scratch_shapes) + sems,
        compiler_params=_params(("arbitrary",) * len(grid)))(*args, *xs)
    compute = outs[:n_out]
    return (compute[0] if single else compute), outs[n_out:]


def _norm_mod(x, nw, sc, sh):
    r = lax.rsqrt(jnp.mean(x * x, axis=-1, keepdims=True) + EPS)
    return (x * r * nw) * (1.0 + sc) + sh


def _norm_mod_bwd(x, nw, sc, du):
    r = lax.rsqrt(jnp.mean(x * x, axis=-1, keepdims=True) + EPS)
    xhat = x * r
    n = xhat * nw
    dsh = jnp.sum(du, axis=0, keepdims=True)
    dsc = jnp.sum(du * n, axis=0, keepdims=True)
    dn = du * (1.0 + sc)
    dnw = jnp.sum(dn * xhat, axis=0, keepdims=True)
    dxhat = dn * nw
    dx = r * (dxhat - xhat * jnp.mean(dxhat * xhat, axis=-1, keepdims=True))
    return dx, dnw, dsc, dsh


def _ffn_up_fwd(h, nw, sc, sh, wup, *, name, ts=512, tn=1408, carry=None):
    s, d = h.shape
    f_dim = wup.shape[1] // 2
    nj = f_dim // tn

    def body(h_ref, nw_ref, sc_ref, sh_ref, wa_ref, wb_ref, u_ref, a_ref, b_ref, f_ref):
        @pl.when(pl.program_id(1) == 0)
        def _():
            u_ref[...] = _norm_mod(h_ref[...], nw_ref[...], sc_ref[...], sh_ref[...]).astype(BF16)

        u = u_ref[...]
        a = _dot(u, wa_ref[...])
        b = _dot(u, wb_ref[...])
        a_ref[...] = a.astype(BF16)
        b_ref[...] = b.astype(BF16)
        f_ref[...] = (a * _sigmoid(a) * b).astype(BF16)

    return _launch(
        body, carry, (h, nw, sc, sh, wup, wup), name=name, grid=(s // ts, nj),
        in_specs=[pl.BlockSpec((ts, d), lambda i, j: (i, 0)), _row(d), _row(d), _row(d),
                  pl.BlockSpec((d, tn), lambda i, j: (0, j)),
                  pl.BlockSpec((d, tn), lambda i, j: (0, j + nj))],
        out_specs=[pl.BlockSpec((ts, d), lambda i, j: (i, 0)),
                   pl.BlockSpec((ts, tn), lambda i, j: (i, j)),
                   pl.BlockSpec((ts, tn), lambda i, j: (i, j)),
                   pl.BlockSpec((ts, tn), lambda i, j: (i, j))],
        out_shape=[jax.ShapeDtypeStruct((s, d), BF16)] + [jax.ShapeDtypeStruct((s, f_dim), BF16)] * 3,
        sem=("parallel", "arbitrary"))


def _ffn_down_fwd(f, wd, h, g, *, name, ts=512):
    s, f_dim = f.shape
    d = wd.shape[1]

    def body(f_ref, wd_ref, h_ref, g_ref, y_ref, ho_ref):
        y = _dot(f_ref[...], wd_ref[...])
        y_ref[...] = y.astype(BF16)
        ho_ref[...] = h_ref[...] + (0.5 * g_ref[...]) * y

    return pl.pallas_call(
        body, name=name, grid=(s // ts,),
        in_specs=[pl.BlockSpec((ts, f_dim), lambda i: (i, 0)), _resident((f_dim, d)),
                  pl.BlockSpec((ts, d), lambda i: (i, 0)), _row(d)],
        out_specs=[pl.BlockSpec((ts, d), lambda i: (i, 0)), pl.BlockSpec((ts, d), lambda i: (i, 0))],
        out_shape=[jax.ShapeDtypeStruct((s, d), BF16), jax.ShapeDtypeStruct((s, d), F32)],
        compiler_params=_params(("parallel",)),
    )(f, wd, h, g)


def _ffn_bwd_act(dh, g, y, a, b, wd, *, name, ts=256, carry=None):
    s, d = dh.shape
    f_dim = a.shape[1]

    def body(dh_ref, g_ref, y_ref, a_ref, b_ref, wd_ref, dy_ref, dab_ref, dg_ref):
        dh_v = dh_ref[...]
        dy = ((0.5 * g_ref[...]) * dh_v).astype(BF16)
        dy_ref[...] = dy
        part = jnp.sum(0.5 * dh_v * y_ref[...].astype(F32), axis=0, keepdims=True)

        @pl.when(pl.program_id(0) == 0)
        def _():
            dg_ref[...] = jnp.zeros_like(dg_ref)

        dg_ref[...] += part
        df = _dot_nt(dy, wd_ref[...])
        av = a_ref[...].astype(F32)
        bv = b_ref[...].astype(F32)
        sg = _sigmoid(av)
        dab_ref[:, :f_dim] = (df * bv * (sg * (1.0 + av * (1.0 - sg)))).astype(BF16)
        dab_ref[:, f_dim:] = (df * (av * sg)).astype(BF16)

    return _launch(
        body, carry, (dh, g, y, a, b, wd), name=name, grid=(s // ts,),
        in_specs=[pl.BlockSpec((ts, d), lambda i: (i, 0)), _row(d),
                  pl.BlockSpec((ts, d), lambda i: (i, 0)),
                  pl.BlockSpec((ts, f_dim), lambda i: (i, 0)),
                  pl.BlockSpec((ts, f_dim), lambda i: (i, 0)),
                  _resident((f_dim, d))],
        out_specs=[pl.BlockSpec((ts, d), lambda i: (i, 0)),
                   pl.BlockSpec((ts, 2 * f_dim), lambda i: (i, 0)), _row(d)],
        out_shape=[jax.ShapeDtypeStruct((s, d), BF16), jax.ShapeDtypeStruct((s, 2 * f_dim), BF16),
                   jax.ShapeDtypeStruct((1, d), F32)],
        sem=("arbitrary",))


def _norm_mod_matmul_bwd(pairs, h, nw, sc, dh_in, *, name, ts=256, carry=None):
    s, d = h.shape
    n_pairs = len(pairs)

    def body(*refs):
        dx_refs = refs[:n_pairs]
        w_refs = refs[n_pairs:2 * n_pairs]
        h_ref, nw_ref, sc_ref, dhi_ref, dho_ref, dnw_ref, dsc_ref, dsh_ref = refs[2 * n_pairs:]
        du = _dot_nt(dx_refs[0][...], w_refs[0][...])
        for k in range(1, n_pairs):
            du = du + _dot_nt(dx_refs[k][...], w_refs[k][...])
        dx, dnw, dsc, dsh = _norm_mod_bwd(h_ref[...], nw_ref[...], sc_ref[...], du)
        dho_ref[...] = dhi_ref[...] + dx

        @pl.when(pl.program_id(0) == 0)
        def _():
            dnw_ref[...] = jnp.zeros_like(dnw_ref)
            dsc_ref[...] = jnp.zeros_like(dsc_ref)
            dsh_ref[...] = jnp.zeros_like(dsh_ref)

        dnw_ref[...] += dnw
        dsc_ref[...] += dsc
        dsh_ref[...] += dsh

    dxs = [p[0] for p in pairs]
    ws = [p[1] for p in pairs]
    tile = pl.BlockSpec((ts, d), lambda i: (i, 0))
    return _launch(
        body, carry, (*dxs, *ws, h, nw, sc, dh_in), name=name, grid=(s // ts,),
        in_specs=([pl.BlockSpec((ts, x.shape[1]), lambda i: (i, 0)) for x in dxs]
                  + [_col_window(w, x.shape[1], p[2] if len(p) > 2 else 0) for p, x, w in zip(pairs, dxs, ws)]
                  + [tile, _row(d), _row(d), tile]),
        out_specs=[tile, _row(d), _row(d), _row(d)],
        out_shape=[jax.ShapeDtypeStruct((s, d), F32)] + [jax.ShapeDtypeStruct((1, d), F32)] * 3,
        sem=("arbitrary",))


def _matmul_tn(a, b, *, name, tm, tn, tk=1024, carry=None):
    s, m = a.shape
    n = b.shape[1]
    tk = min(tk, s)
    nk = s // tk

    def body(a_ref, b_ref, o_ref, acc_ref):
        k = pl.program_id(2)

        @pl.when(k == 0)
        def _():
            acc_ref[...] = jnp.zeros_like(acc_ref)

        acc_ref[...] += _dot_tn(a_ref[...], b_ref[...])

        @pl.when(k == nk - 1)
        def _():
            o_ref[...] = acc_ref[...].astype(o_ref.dtype)

    return _launch(
        body, carry, (a, b), name=name, grid=(m // tm, n // tn, nk),
        in_specs=[pl.BlockSpec((tk, tm), lambda i, j, k: (k, i)),
                  pl.BlockSpec((tk, tn), lambda i, j, k: (k, j))],
        out_specs=pl.BlockSpec((tm, tn), lambda i, j, k: (i, j)),
        out_shape=jax.ShapeDtypeStruct((m, n), BF16),
        scratch_shapes=[pltpu.VMEM((tm, tn), F32)],
        sem=("parallel", "parallel", "arbitrary"))


def _in_proj_fwd(h, nw, sc, sh, w_main, w_ba, *, name, ts=1024, tn=1536, carry=None):
    s, d = h.shape
    ts = min(ts, s)
    n_main = w_main.shape[1]
    n_ba = w_ba.shape[1]

    def body(h_ref, nw_ref, sc_ref, sh_ref, w_ref, wba_ref, u_ref, p_ref, ba_ref):
        @pl.when(pl.program_id(1) == 0)
        def _():
            u0 = _norm_mod(h_ref[...], nw_ref[...], sc_ref[...], sh_ref[...]).astype(BF16)
            u_ref[...] = u0
            ba_ref[...] = _dot(u0, wba_ref[...])

        p_ref[...] = _dot(u_ref[...], w_ref[...]).astype(BF16)

    return _launch(
        body, carry, (h, nw, sc, sh, w_main, w_ba), name=name, grid=(s // ts, n_main // tn),
        in_specs=[pl.BlockSpec((ts, d), lambda i, j: (i, 0)), _row(d), _row(d), _row(d),
                  pl.BlockSpec((d, tn), lambda i, j: (0, j)), _resident((d, n_ba))],
        out_specs=[pl.BlockSpec((ts, d), lambda i, j: (i, 0)),
                   pl.BlockSpec((ts, tn), lambda i, j: (i, j)),
                   pl.BlockSpec((ts, n_ba), lambda i, j: (i, 0))],
        out_shape=[jax.ShapeDtypeStruct((s, d), BF16), jax.ShapeDtypeStruct((s, n_main), BF16),
                   jax.ShapeDtypeStruct((s, n_ba), F32)],
        sem=("parallel", "arbitrary"))


QKV_W = 3 * HEADS * HEAD_DIM
Z_OFF, Z_W = 3072, 1024
GATE_OFF, GATE_W = 4096, 2048
A_OFF, A_W = 6144, 1536
N_MAIN = 7680
CONV_A = 512
BA_W = 128

L_BETA, L_G, L_EG, L_EKD, L_EGC = 0, 8, 16, 24, 32


def _softplus(z):
    e = jnp.exp(-jnp.abs(z))
    small = e * (1.0 - e * (0.5 - e * (1.0 / 3.0)))
    return jnp.maximum(z, 0.0) + jnp.where(e < 1e-3, small, jnp.log(1.0 + e))


def _tri(n, sgn, strict=False):
    i = lax.broadcasted_iota(jnp.int32, (n, n), 0)
    j = lax.broadcasted_iota(jnp.int32, (n, n), 1)
    dlt = (i - j) * sgn
    return (dlt > 0) if strict else (dlt >= 0)


def _scal_fwd(ba, alog, dtb, *, name, ts=512):
    s = ba.shape[0]

    def body(ba_ref, al_ref, dt_ref, o_ref):
        d = pl.program_id(0)
        sgn = 1 - 2 * d
        x = ba_ref[...]
        lane = lax.broadcasted_iota(jnp.int32, x.shape, 1)
        beta = _sigmoid(x)
        g = -jnp.exp(al_ref[0]) * _softplus(x + dt_ref[0])
        g = jnp.where((lane >= L_G) & (lane < L_EGC + 8), g, 0.0)
        ltri = jnp.where(_tri(CHUNK, sgn), 1.0, 0.0).astype(F32)
        for c in range(ts // CHUNK):
            rows = slice(c * CHUNK, (c + 1) * CHUNK)
            gc = _dot_exact(ltri, g[rows])
            g_end = jnp.where(d == 0, gc[CHUNK - 1:CHUNK], gc[0:1])
            ln = lane[rows]
            out = jnp.where(ln < L_G, beta[rows],
                  jnp.where(ln < L_EG, gc,
                  jnp.where(ln < L_EKD, jnp.exp(gc),
                  jnp.where(ln < L_EGC, jnp.exp(g_end - gc),
                  jnp.where(ln < L_EGC + 8, jnp.broadcast_to(jnp.exp(g_end), gc.shape), 0.0)))))
            o_ref[0, rows, :] = out

    return pl.pallas_call(
        body, name=name, grid=(2, s // ts),
        in_specs=[pl.BlockSpec((ts, BA_W), lambda d, i: (i, d)),
                  pl.BlockSpec((1, 1, BA_W), lambda d, i: (d, 0, 0)),
                  pl.BlockSpec((1, 1, BA_W), lambda d, i: (d, 0, 0))],
        out_specs=pl.BlockSpec((1, ts, BA_W), lambda d, i: (d, i, 0)),
        out_shape=jax.ShapeDtypeStruct((2, s, BA_W), F32),
        compiler_params=_params(("parallel", "parallel")),
    )(ba, alog, dtb)


def _scal_bwd(dscal, drow, ba, alog, dtb, *, name, ts=512):
    s = ba.shape[0]

    def body(ds_ref, dr_ref, ba_ref, al_ref, dt_ref, dba_ref, dal_ref, ddt_ref):
        d = pl.program_id(0)
        sgn = 1 - 2 * d
        x = ba_ref[...]
        lane = lax.broadcasted_iota(jnp.int32, x.shape, 1)
        in_g = (lane >= L_G) & (lane < L_G + 8)
        beta = _sigmoid(x)
        z = x + dt_ref[0]
        neg_a = -jnp.exp(al_ref[0])
        g = neg_a * _softplus(z)
        dsv = ds_ref[0]
        dgc = jnp.where(in_g, dsv + dr_ref[0], 0.0)
        utri = jnp.where(_tri(CHUNK, -sgn), 1.0, 0.0).astype(F32)
        dal = jnp.zeros((1, BA_W), F32)
        ddt = jnp.zeros((1, BA_W), F32)
        for c in range(ts // CHUNK):
            rows = slice(c * CHUNK, (c + 1) * CHUNK)
            dg = _dot_exact(utri, dgc[rows])
            dz = dg * neg_a * _sigmoid(z[rows])
            dal = dal + jnp.sum(dg * g[rows], axis=0, keepdims=True)
            ddt = ddt + jnp.sum(dz, axis=0, keepdims=True)
            b = beta[rows]
            out = jnp.where(lane[rows] < L_G, dsv[rows] * b * (1.0 - b), jnp.where(in_g[rows], dz, 0.0))
            dba_ref[rows, :] = out.astype(BF16)

        @pl.when(pl.program_id(1) == 0)
        def _():
            dal_ref[...] = jnp.zeros_like(dal_ref)
            ddt_ref[...] = jnp.zeros_like(ddt_ref)

        dal_ref[0] += dal
        ddt_ref[0] += ddt

    row3 = pl.BlockSpec((1, 1, BA_W), lambda d, i: (d, 0, 0))
    tok3 = pl.BlockSpec((1, ts, BA_W), lambda d, i: (d, i, 0))
    return pl.pallas_call(
        body, name=name, grid=(2, s // ts),
        in_specs=[tok3, tok3, pl.BlockSpec((ts, BA_W), lambda d, i: (i, d)), row3, row3],
        out_specs=[pl.BlockSpec((ts, BA_W), lambda d, i: (i, d)), row3, row3],
        out_shape=[jax.ShapeDtypeStruct((s, 2 * BA_W), BF16), jax.ShapeDtypeStruct((2, 1, BA_W), F32),
                   jax.ShapeDtypeStruct((2, 1, BA_W), F32)],
        compiler_params=_params(("arbitrary", "arbitrary")),
    )(dscal, drow, ba, alog, dtb)


HALO = 16


def _halo_specs(ts, width, col_block, n_rows, rows=HALO):
    r = ts // rows
    last = n_rows // rows - 1
    return [pl.BlockSpec((rows, width), lambda i: (jnp.maximum(i * r - 1, 0), col_block)),
            pl.BlockSpec((ts, width), lambda i: (i, col_block)),
            pl.BlockSpec((rows, width), lambda i: (jnp.minimum((i + 1) * r, last), col_block))]


def _fill_halo(dst_ref, prev_ref, cur_ref, next_ref, first, last, fn=lambda r: r[...].astype(F32)):
    h = prev_ref.shape[0]
    ts = cur_ref.shape[0]
    p = fn(prev_ref)
    n = fn(next_ref)
    dst_ref[0:h, :] = jnp.where(first, 0.0, p)
    dst_ref[h:h + ts, :] = fn(cur_ref)
    dst_ref[h + ts:h + ts + h, :] = jnp.where(last, 0.0, n)


def _dwconv_rows(src_ref, w, start, n_rows, cols):
    acc = w[0:1, :] * src_ref[start:start + n_rows, cols]
    for i in range(1, w.shape[0]):
        acc = acc + w[i:i + 1, :] * src_ref[start + i:start + i + n_rows, cols]
    return acc


def _l2norm_heads(act, scale):
    outs = []
    for hd in range(HEADS):
        seg = act[:, hd * HEAD_DIM:(hd + 1) * HEAD_DIM]
        outs.append(seg * (lax.rsqrt(jnp.sum(seg * seg, axis=-1, keepdims=True) + EPS) * scale))
    return jnp.concatenate(outs, axis=-1)


Q_SCALE = HEAD_DIM ** -0.5


def _conv_fwd(proj, conv_dn, conv_a, *, name, ts=256):
    s = proj.shape[0]
    hd = HEADS * HEAD_DIM
    nt = s // ts

    def body(qp_ref, qc_ref, qn_ref, ap_ref, ac_ref, an_ref, wdn_ref, wa_ref,
             q_ref, k_ref, v_ref, ya_ref, xs_ref, xa_ref):
        i = pl.program_id(0)
        first, last = i == 0, i == nt - 1
        _fill_halo(xs_ref, qp_ref, qc_ref, qn_ref, first, last)
        wdn = wdn_ref[...]
        for part, o_ref in enumerate((q_ref, k_ref, v_ref)):
            cols = slice(part * hd, (part + 1) * hd)
            pre = _dwconv_rows(xs_ref, wdn[:, cols], HALO - 2, ts, cols)
            act = pre * _sigmoid(pre)
            if part == 0:
                act = _l2norm_heads(act, Q_SCALE)
            elif part == 1:
                act = _l2norm_heads(act, 1.0)
            o_ref[...] = act
        cv = lambda r: r[:, CONV_A:2 * CONV_A].astype(F32) * r[:, 2 * CONV_A:].astype(F32)
        _fill_halo(xa_ref, ap_ref, ac_ref, an_ref, first, last, fn=cv)
        conv = _dwconv_rows(xa_ref, wa_ref[...], HALO - 1, ts, slice(0, CONV_A))
        ya_ref[...] = (ac_ref[:, 0:CONV_A].astype(F32) * conv).astype(BF16)

    tile = lambda w: pl.BlockSpec((ts, w), lambda i: (i, 0))
    return pl.pallas_call(
        body, name=name, grid=(nt,),
        in_specs=(_halo_specs(ts, QKV_W, 0, s) + _halo_specs(ts, A_W, A_OFF // A_W, s)
                  + [_resident(conv_dn.shape), _resident(conv_a.shape)]),
        out_specs=[tile(hd), tile(hd), tile(hd), tile(CONV_A)],
        out_shape=[jax.ShapeDtypeStruct((s, hd), F32)] * 3 + [jax.ShapeDtypeStruct((s, CONV_A), BF16)],
        scratch_shapes=[pltpu.VMEM((ts + 2 * HALO, QKV_W), F32), pltpu.VMEM((ts + 2 * HALO, CONV_A), F32)],
        compiler_params=_params(("parallel",)),
    )(proj, proj, proj, proj, proj, proj, conv_dn, conv_a)


def _chunk_of_step(d, c, n):
    return c + d * (n - 1 - 2 * c)


def _head_scalars(scv, grv, hd):
    col = lambda base: scv[:, base + hd:base + hd + 1]
    return (col(L_BETA), col(L_G), col(L_EG), col(L_EKD),
            scv[0:1, L_EGC + hd:L_EGC + hd + 1], grv[hd:hd + 1, :])


def _decay_matrix(gcol, grow, incl):
    return jnp.where(incl, jnp.exp(jnp.minimum(gcol - grow, 0.0)), 0.0)


INV_BASE = 8


def _unit_lower_inverse(a_m, top=None):
    n = a_m.shape[0]
    top = top or n
    i = lax.broadcasted_iota(jnp.int32, (n, n), 0)
    j = lax.broadcasted_iota(jnp.int32, (n, n), 1)

    def same_block(m):
        sh = int(math.log2(m))
        return jnp.right_shift(i, sh) == jnp.right_shift(j, sh)

    x = jnp.where(same_block(INV_BASE), -a_m, 0.0)
    t = jnp.where(i == j, 1.0, 0.0) + x
    p = x
    for _ in range(int(math.log2(INV_BASE)) - 1):
        p_b = p.astype(BF16)
        p = _dot(p_b, p_b)
        t = t + _dot(t.astype(BF16), p.astype(BF16))
    m = INV_BASE
    while m < top:
        join = jnp.where(same_block(2 * m) & jnp.logical_not(same_block(m)), a_m, 0.0)
        t_b = t.astype(BF16)
        t = t - _dot(_dot(t_b, join.astype(BF16)).astype(BF16), t_b)
        m *= 2
    return t


def _delta_fwd_per_head(q, k, v, scal, grow, *, name):
    s = q.shape[0]
    n = s // CHUNK
    hd_all = HEADS * HEAD_DIM

    def body(q_ref, k_ref, v_ref, sc_ref, gr_ref, o_ref, st_ref, t_ref, vn_ref, state):
        d = pl.program_id(0)
        sgn = 1 - 2 * d

        @pl.when(pl.program_id(1) == 0)
        def _():
            state[...] = jnp.zeros_like(state)

        incl = _tri(CHUNK, sgn)
        strict = _tri(CHUNK, sgn, strict=True)
        scv = sc_ref[0]
        grv = gr_ref[0, 0]
        for hd in range(HEADS):
            cols = slice(hd * HEAD_DIM, (hd + 1) * HEAD_DIM)
            qh, kh, vh = q_ref[:, cols], k_ref[:, cols], v_ref[:, cols]
            beta, gcol, eg, ekd, egc, grow_h = _head_scalars(scv, grv, hd)
            dm = _decay_matrix(gcol, grow_h, incl)
            k_b = kh.astype(BF16)
            kk = _dot_nt((kh * beta).astype(BF16), k_b)
            t = _unit_lower_inverse(jnp.where(strict, kk * dm, 0.0))
            p_m = jnp.where(incl, _dot_nt(qh.astype(BF16), k_b) * dm, 0.0)
            sh = state[hd]
            sh_b = sh.astype(BF16)
            st_ref[0, 0, hd] = sh_b
            r = vh - _dot((kh * eg).astype(BF16), sh_b)
            vn = _dot3(t, beta * r)
            vn_b = vn.astype(BF16)
            o_ref[0, :, cols] = _dot((qh * eg).astype(BF16), sh_b) + _dot(p_m.astype(BF16), vn_b)
            state[hd] = egc * sh + _dot_tn((kh * ekd).astype(BF16), vn_b)
            t_ref[0, 0, hd] = t
            vn_ref[0, :, cols] = vn_b

    tok = lambda d, c: (_chunk_of_step(d, c, n), 0)
    dtok = lambda d, c: (d, _chunk_of_step(d, c, n), 0)
    dchunk4 = lambda d, c: (d, _chunk_of_step(d, c, n), 0, 0)
    dchunk5 = lambda d, c: (d, _chunk_of_step(d, c, n), 0, 0, 0)
    return pl.pallas_call(
        body, name=name, grid=(2, n),
        in_specs=[pl.BlockSpec((CHUNK, hd_all), tok)] * 3
                 + [pl.BlockSpec((1, CHUNK, BA_W), dtok), pl.BlockSpec((1, 1, HEADS, CHUNK), dchunk4)],
        out_specs=[pl.BlockSpec((1, CHUNK, hd_all), dtok),
                   pl.BlockSpec((1, 1, HEADS, HEAD_DIM, HEAD_DIM), dchunk5),
                   pl.BlockSpec((1, 1, HEADS, CHUNK, CHUNK), dchunk5),
                   pl.BlockSpec((1, CHUNK, hd_all), dtok)],
        out_shape=[jax.ShapeDtypeStruct((2, s, hd_all), F32),
                   jax.ShapeDtypeStruct((2, n, HEADS, HEAD_DIM, HEAD_DIM), BF16),
                   jax.ShapeDtypeStruct((2, n, HEADS, CHUNK, CHUNK), F32),
                   jax.ShapeDtypeStruct((2, s, hd_all), BF16)],
        scratch_shapes=[pltpu.VMEM((HEADS, HEAD_DIM, HEAD_DIM), F32)],
        compiler_params=_params(("arbitrary", "arbitrary")),
    )(q, k, v, scal, grow)


def _delta_bwd_per_head(q, k, v, scal, grow, states, tinv, vn, do, *, name):
    s = q.shape[0]
    n = s // CHUNK
    hd_all = HEADS * HEAD_DIM

    def body(q_ref, k_ref, v_ref, sc_ref, gr_ref, st_ref, t_ref, vn_ref, do_ref,
             dq_ref, dk_ref, dv_ref, dsc_ref, dgr_ref, dstate):
        d = pl.program_id(0)
        sgn = 1 - 2 * d

        @pl.when(pl.program_id(1) == 0)
        def _():
            dstate[...] = jnp.zeros_like(dstate)

        incl = _tri(CHUNK, sgn)
        strict = _tri(CHUNK, sgn, strict=True)
        scv = sc_ref[0]
        grv = gr_ref[0, 0]
        lane = lax.broadcasted_iota(jnp.int32, (CHUNK, BA_W), 1)
        row = lax.broadcasted_iota(jnp.int32, (CHUNK, 1), 0)
        sub = lax.broadcasted_iota(jnp.int32, (HEADS, CHUNK), 0)
        end_row = jnp.where(d == 0, CHUNK - 1, 0)
        dsc_acc = jnp.zeros((CHUNK, BA_W), F32)
        dgr_acc = jnp.zeros((HEADS, CHUNK), F32)
        for hd in range(HEADS):
            cols = slice(hd * HEAD_DIM, (hd + 1) * HEAD_DIM)
            qh, kh, vh = q_ref[:, cols], k_ref[:, cols], v_ref[:, cols]
            beta, gcol, eg, ekd, egc, grow_h = _head_scalars(scv, grv, hd)
            dm = _decay_matrix(gcol, grow_h, incl)
            q_b, k_b = qh.astype(BF16), kh.astype(BF16)
            kb_b = (kh * beta).astype(BF16)
            kk = _dot_nt(kb_b, k_b)
            qk = _dot_nt(q_b, k_b)
            p_m = jnp.where(incl, qk * dm, 0.0)
            t = t_ref[0, 0, hd]
            vn_b = vn_ref[0, :, cols]
            sh_b = st_ref[0, 0, hd]
            dsp = dstate[hd]
            dsp_b = dsp.astype(BF16)
            do_b = do_ref[:, cols].astype(BF16)
            kg, qg, kd = kh * eg, qh * eg, kh * ekd
            kg_b, qg_b, kd_b = kg.astype(BF16), qg.astype(BF16), kd.astype(BF16)
            r = vh - _dot(kg_b, sh_b)
            dvn = _dot_tn(p_m.astype(BF16), do_b) + _dot(kd_b, dsp_b)
            db = _dot3(t, dvn, dot=_dot_tn)
            dr = db * beta
            dbeta = jnp.sum(db * r, axis=-1, keepdims=True)
            dr_b, db_b = dr.astype(BF16), db.astype(BF16)
            dkg = -_dot_nt(dr_b, sh_b)
            dqg = _dot_nt(do_b, sh_b)
            dkd = _dot_nt(vn_b, dsp_b)
            dpm = jnp.where(incl, _dot_nt(do_b, vn_b), 0.0) * dm
            dam = jnp.where(strict, -_dot_nt(db_b, vn_b), 0.0) * dm
            dpm_b, dam_b = dpm.astype(BF16), dam.astype(BF16)
            dkb = _dot(dam_b, k_b)
            dq_ref[0, :, cols] = dqg * eg + _dot(dpm_b, k_b)
            dk_ref[0, :, cols] = (dkg * eg + dkd * ekd + _dot_tn(dpm_b, q_b) + _dot_tn(dam_b, kb_b)
                                  + dkb * beta)
            dv_ref[0, :, cols] = dr
            dbeta = dbeta + jnp.sum(dkb * kh, axis=-1, keepdims=True)
            m = dpm * qk + dam * kk
            kd_term = jnp.sum(dkd * kd, axis=-1, keepdims=True)
            dgcol = (jnp.sum(dqg * qg, axis=-1, keepdims=True) + jnp.sum(dkg * kg, axis=-1, keepdims=True)
                     - kd_term + jnp.sum(m, axis=-1, keepdims=True))
            dg_end = jnp.sum(kd_term) + egc * jnp.sum(dsp * sh_b.astype(F32))
            dgcol = dgcol + jnp.where(row == end_row, dg_end, 0.0)
            dsc_acc = jnp.where(lane == L_BETA + hd, dbeta, dsc_acc)
            dsc_acc = jnp.where(lane == L_G + hd, dgcol, dsc_acc)
            dgr_acc = jnp.where(sub == hd, -jnp.sum(m, axis=0, keepdims=True), dgr_acc)
            dstate[hd] = _dot_tn(qg_b, do_b) + egc * dsp - _dot_tn(kg_b, dr_b)
        dsc_ref[0] = dsc_acc
        dgr_ref[0, 0] = dgr_acc

    step = lambda d, c: n - 1 - _chunk_of_step(d, c, n)
    tok = lambda d, c: (step(d, c), 0)
    dtok = lambda d, c: (d, step(d, c), 0)
    dchunk4 = lambda d, c: (d, step(d, c), 0, 0)
    dchunk5 = lambda d, c: (d, step(d, c), 0, 0, 0)
    tok_spec = pl.BlockSpec((CHUNK, hd_all), tok)
    dtok_spec = pl.BlockSpec((1, CHUNK, hd_all), dtok)
    return pl.pallas_call(
        body, name=name, grid=(2, n),
        in_specs=[tok_spec] * 3
                 + [pl.BlockSpec((1, CHUNK, BA_W), dtok), pl.BlockSpec((1, 1, HEADS, CHUNK), dchunk4),
                    pl.BlockSpec((1, 1, HEADS, HEAD_DIM, HEAD_DIM), dchunk5),
                    pl.BlockSpec((1, 1, HEADS, CHUNK, CHUNK), dchunk5), dtok_spec, tok_spec],
        out_specs=[dtok_spec] * 3
                  + [pl.BlockSpec((1, CHUNK, BA_W), dtok), pl.BlockSpec((1, 1, HEADS, CHUNK), dchunk4)],
        out_shape=[jax.ShapeDtypeStruct((2, s, hd_all), F32)] * 3
                  + [jax.ShapeDtypeStruct((2, s, BA_W), F32), jax.ShapeDtypeStruct((2, n, HEADS, CHUNK), F32)],
        scratch_shapes=[pltpu.VMEM((HEADS, HEAD_DIM, HEAD_DIM), F32)],
        compiler_params=_params(("arbitrary", "arbitrary")),
    )(q, k, v, scal, grow, states, tinv, vn, do)


GROUP = 4
GROWS = GROUP * CHUNK
N_GROUPS = HEADS // GROUP


def _stack(parts):
    return jnp.concatenate(parts, axis=0)


M_INCL, M_STRICT, M_EYE, M_BASE, M_JOIN = 0, 1, 2, 3, 4
JOIN_SIZES = (16, 32, 64)
N_MASKS = M_JOIN + len(JOIN_SIZES)


def _write_group_masks(mask_ref, sgn, n_masks):
    i = lax.broadcasted_iota(jnp.int32, (GROWS, GROWS), 0)
    j = lax.broadcasted_iota(jnp.int32, (GROWS, GROWS), 1)
    same = lambda m: jnp.right_shift(i, int(math.log2(m))) == jnp.right_shift(j, int(math.log2(m)))
    dlt = (i - j) * sgn
    one = lambda cond: jnp.where(cond, 1.0, 0.0).astype(F32)
    mask_ref[M_INCL] = one(same(CHUNK) & (dlt >= 0))
    mask_ref[M_STRICT] = one(same(CHUNK) & (dlt > 0))
    if n_masks > M_EYE:
        mask_ref[M_EYE] = one(i == j)
        mask_ref[M_BASE] = one(same(INV_BASE))
        for lvl, m in enumerate(JOIN_SIZES):
            mask_ref[M_JOIN + lvl] = one(same(m) & jnp.logical_not(same(m // 2)))


def _group_decay(gcol, grow, mask_ref):
    return jnp.exp(jnp.minimum(gcol - grow, 0.0)) * mask_ref[M_INCL]


def _block_inverse_many(a_ms, mask_refs):
    xs = [-(a * m[M_BASE]) for a, m in zip(a_ms, mask_refs)]
    ts = [m[M_EYE] + x for x, m in zip(xs, mask_refs)]
    ps = xs
    for _ in range(int(math.log2(INV_BASE)) - 1):
        p_bs = [p.astype(BF16) for p in ps]
        ps = [_dot(p_b, p_b) for p_b in p_bs]
        ts = [t + _dot(t.astype(BF16), p.astype(BF16)) for t, p in zip(ts, ps)]
    for lvl in range(len(JOIN_SIZES)):
        t_bs = [t.astype(BF16) for t in ts]
        joins = [(a * m[M_JOIN + lvl]).astype(BF16) for a, m in zip(a_ms, mask_refs)]
        mids = [_dot(t_b, j).astype(BF16) for t_b, j in zip(t_bs, joins)]
        ts = [t - _dot(mid, t_b) for t, mid, t_b in zip(ts, mids, t_bs)]
    return ts


def _block_inverse(a_m, mask_ref):
    x = -(a_m * mask_ref[M_BASE])
    t = mask_ref[M_EYE] + x
    p = x
    for _ in range(int(math.log2(INV_BASE)) - 1):
        p_b = p.astype(BF16)
        p = _dot(p_b, p_b)
        t = t + _dot(t.astype(BF16), p.astype(BF16))
    for lvl in range(len(JOIN_SIZES)):
        t_b = t.astype(BF16)
        t = t - _dot(_dot(t_b, (a_m * mask_ref[M_JOIN + lvl]).astype(BF16)).astype(BF16), t_b)
    return t


def _group_operands(q_ref, k_ref, v_ref, scv, grp):
    heads = [GROUP * grp + t for t in range(GROUP)]
    tiles = lambda ref: [ref[:, h * HEAD_DIM:(h + 1) * HEAD_DIM] for h in heads]
    col = lambda base: [scv[:, base + h:base + h + 1] for h in heads]
    egc = [scv[0:1, L_EGC + h:L_EGC + h + 1] for h in heads]
    return heads, tiles(q_ref), tiles(k_ref), tiles(v_ref), col(L_BETA), col(L_G), col(L_EG), col(L_EKD), egc


def _delta_fwd(q, k, v, scal, grow, *, name):
    s = q.shape[0]
    n = s // CHUNK
    hd_all = HEADS * HEAD_DIM

    def body(*refs):
        ins, outs, (state, mask_ref) = refs[:10], refs[10:18], refs[18:]

        @pl.when(pl.program_id(0) == 0)
        def _():
            state[...] = jnp.zeros_like(state)
            for d in range(2):
                _write_group_masks(mask_ref.at[d], 1 - 2 * d, N_MASKS)

        chains = []
        for d in range(2):
            q_ref, k_ref, v_ref, sc_ref, gr_ref = ins[5 * d:5 * d + 5]
            scv = sc_ref[0]
            for grp in range(N_GROUPS):
                chains.append(dict(
                    d=d, grp=grp, gr_ref=gr_ref, out=outs[4 * d:4 * d + 4], state=state.at[d], masks=mask_ref.at[d],
                    ops=_group_operands(q_ref, k_ref, v_ref, scv, grp)))
        for ch in chains:
            heads, qs, ks, vs, beta, gcol, eg, ekd, egc = ch["ops"]
            ch["dm"] = _group_decay(_stack(gcol), ch["gr_ref"][0, 0, ch["grp"]:ch["grp"] + 1, :], ch["masks"])
            ch["k_b"] = _stack(ks).astype(BF16)
            ch["kb_b"] = _stack([ks[t] * beta[t] for t in range(GROUP)]).astype(BF16)
        for ch in chains:
            ch["a_m"] = _dot_nt(ch["kb_b"], ch["k_b"]) * ch["dm"] * ch["masks"][M_STRICT]
        tinvs = _block_inverse_many([ch["a_m"] for ch in chains], [ch["masks"] for ch in chains])
        for ch, tinv in zip(chains, tinvs):
            heads, qs, ks, vs, beta, gcol, eg, ekd, egc = ch["ops"]
            o_ref, st_ref, t_ref, vn_ref = ch["out"]
            ch["tinv"] = tinv.astype(BF16)
            t_ref[0, ch["grp"]] = ch["tinv"]
            ch["p_b"] = (_dot_nt(_stack(qs).astype(BF16), ch["k_b"]) * ch["dm"]).astype(BF16)
            ch["sh"] = [ch["state"][h] for h in heads]
            ch["sh_b"] = [x.astype(BF16) for x in ch["sh"]]
            for t, h in enumerate(heads):
                st_ref[0, h] = ch["sh_b"][t]
        for ch in chains:
            heads, qs, ks, vs, beta, gcol, eg, ekd, egc = ch["ops"]
            ch["br"] = _stack([beta[t] * (vs[t] - _dot((ks[t] * eg[t]).astype(BF16), ch["sh_b"][t]))
                               for t in range(GROUP)]).astype(BF16)
        for ch in chains:
            ch["vn_b"] = _dot(ch["tinv"], ch["br"]).astype(BF16)
        for ch in chains:
            ch["o_intra"] = _dot(ch["p_b"], ch["vn_b"])
        for ch in chains:
            heads, qs, ks, vs, beta, gcol, eg, ekd, egc = ch["ops"]
            o_ref, st_ref, t_ref, vn_ref = ch["out"]
            for t, h in enumerate(heads):
                rows = slice(t * CHUNK, (t + 1) * CHUNK)
                cols = slice(h * HEAD_DIM, (h + 1) * HEAD_DIM)
                o_ref[:, cols] = _dot((qs[t] * eg[t]).astype(BF16), ch["sh_b"][t]) + ch["o_intra"][rows]
                ch["state"][h] = egc[t] * ch["sh"][t] + _dot_tn((ks[t] * ekd[t]).astype(BF16), ch["vn_b"][rows])
                vn_ref[:, cols] = ch["vn_b"][rows]

    at = [lambda c: c, lambda c: n - 1 - c]
    in_specs, out_specs = [], []
    for d in range(2):
        tok = pl.BlockSpec((CHUNK, hd_all), lambda c, d=d: (at[d](c), 0))
        in_specs += [tok] * 3 + [pl.BlockSpec((1, CHUNK, BA_W), lambda c, d=d: (d, at[d](c), 0)),
                                 pl.BlockSpec((1, 1, N_GROUPS, GROWS), lambda c, d=d: (d, at[d](c), 0, 0))]
        out_specs += [tok, pl.BlockSpec((1, HEADS, HEAD_DIM, HEAD_DIM), lambda c, d=d: (at[d](c), 0, 0, 0)),
                      pl.BlockSpec((1, N_GROUPS, GROWS, GROWS), lambda c, d=d: (at[d](c), 0, 0, 0)), tok]
    per_dir_shape = [jax.ShapeDtypeStruct((s, hd_all), F32),
                     jax.ShapeDtypeStruct((n, HEADS, HEAD_DIM, HEAD_DIM), BF16),
                     jax.ShapeDtypeStruct((n, N_GROUPS, GROWS, GROWS), BF16),
                     jax.ShapeDtypeStruct((s, hd_all), BF16)]
    outs = pl.pallas_call(
        body, name=name, grid=(n,), in_specs=in_specs, out_specs=out_specs, out_shape=per_dir_shape * 2,
        scratch_shapes=[pltpu.VMEM((2, HEADS, HEAD_DIM, HEAD_DIM), F32),
                        pltpu.VMEM((2, N_MASKS, GROWS, GROWS), F32)],
        compiler_params=_params(("arbitrary",)),
    )(*([q, k, v, scal, grow] * 2))
    return tuple((outs[i], outs[4 + i]) for i in range(4))


def _delta_bwd(q, k, v, scal, grow, states, tinv, vn, do, *, name, carry=None):
    s = q.shape[0]
    n = s // CHUNK
    hd_all = HEADS * HEAD_DIM

    grp_rows = [slice(t * CHUNK, (t + 1) * CHUNK) for t in range(GROUP)]
    per_head = lambda fn: _stack([fn(t) for t in range(GROUP)])

    def body(*refs):
        ins, outs, (dstate, mask_ref) = refs[:18], refs[18:28], refs[28:]

        @pl.when(pl.program_id(0) == 0)
        def _():
            dstate[...] = jnp.zeros_like(dstate)
            for d in range(2):
                _write_group_masks(mask_ref.at[d], 1 - 2 * d, M_EYE)

        chains = []
        for d in range(2):
            q_ref, k_ref, v_ref, sc_ref, gr_ref, st_ref, t_ref, vn_ref, do_ref = ins[9 * d:9 * d + 9]
            scv = sc_ref[0]
            for grp in range(N_GROUPS):
                c = types.SimpleNamespace(d=d, grp=grp, out=outs[5 * d:5 * d + 5], dstate=dstate.at[d],
                                          masks=mask_ref.at[d])
                (c.heads, qs, ks, c.vs, beta, gcol, eg, ekd, c.egc) = _group_operands(q_ref, k_ref, v_ref, scv, grp)
                c.cols = [slice(h * HEAD_DIM, (h + 1) * HEAD_DIM) for h in c.heads]
                c.dm = _group_decay(_stack(gcol), gr_ref[0, 0, grp:grp + 1, :], c.masks)
                c.dm_strict = c.dm * c.masks[M_STRICT]
                other = mask_ref.at[1 - d]
                c.dm_t = jnp.exp(jnp.minimum(gr_ref[0, 0, grp:grp + 1, :] - _stack(gcol), 0.0)) * other[M_INCL]
                c.dm_t_strict = c.dm_t * other[M_STRICT]
                c.beta, c.eg, c.ekd = _stack(beta), _stack(eg), _stack(ekd)
                c.q, c.k = _stack(qs), _stack(ks)
                c.q_b, c.k_b = c.q.astype(BF16), c.k.astype(BF16)
                c.kb_b = (c.k * c.beta).astype(BF16)
                c.kg, c.qg, c.kd = c.k * c.eg, c.q * c.eg, c.k * c.ekd
                c.kg_b, c.qg_b, c.kd_b = c.kg.astype(BF16), c.qg.astype(BF16), c.kd.astype(BF16)
                c.vn_b = _stack([vn_ref[:, cc] for cc in c.cols])
                c.do_b = _stack([do_ref[:, cc] for cc in c.cols]).astype(BF16)
                c.sh_b = [st_ref[0, h] for h in c.heads]
                c.dsp = [c.dstate[h] for h in c.heads]
                c.dsp_b = [x.astype(BF16) for x in c.dsp]
                c.t_b = t_ref[0, grp]
                chains.append(c)
        for c in chains:
            c.kk = _dot_nt(c.kb_b, c.k_b)
            c.qk = _dot_nt(c.q_b, c.k_b)
            c.pt_b = (_dot_nt(c.k_b, c.q_b) * c.dm_t).astype(BF16)
        for c in chains:
            c.r = per_head(lambda t: c.vs[t] - _dot(c.kg_b[grp_rows[t]], c.sh_b[t]))
            c.kd_ds = per_head(lambda t: _dot(c.kd_b[grp_rows[t]], c.dsp_b[t]))
        for c in chains:
            c.dvn_b = (_dot(c.pt_b, c.do_b) + c.kd_ds).astype(BF16)
        for c in chains:
            c.db = _dot_tn(c.t_b, c.dvn_b)
        for c in chains:
            c.dr = c.db * c.beta
            c.dbeta = jnp.sum(c.db * c.r, axis=-1, keepdims=True)
            c.dr_b, c.db_b = c.dr.astype(BF16), c.db.astype(BF16)
        for c in chains:
            c.dkg = -per_head(lambda t: _dot_nt(c.dr_b[grp_rows[t]], c.sh_b[t]))
            c.dqg = per_head(lambda t: _dot_nt(c.do_b[grp_rows[t]], c.sh_b[t]))
            c.dkd = per_head(lambda t: _dot_nt(c.vn_b[grp_rows[t]], c.dsp_b[t]))
        for c in chains:
            c.dpm = _dot_nt(c.do_b, c.vn_b) * c.dm
            c.dam = -_dot_nt(c.db_b, c.vn_b) * c.dm_strict
            c.dpm_b, c.dam_b = c.dpm.astype(BF16), c.dam.astype(BF16)
            c.dpm_t_b = (_dot_nt(c.vn_b, c.do_b) * c.dm_t).astype(BF16)
            c.dam_t_b = (-_dot_nt(c.vn_b, c.db_b) * c.dm_t_strict).astype(BF16)
        for c in chains:
            c.dkb = _dot(c.dam_b, c.k_b)
            c.dq = c.dqg * c.eg + _dot(c.dpm_b, c.k_b)
        for c in chains:
            c.dk = (c.dkg * c.eg + c.dkd * c.ekd + _dot(c.dpm_t_b, c.q_b) + _dot(c.dam_t_b, c.kb_b)
                    + c.dkb * c.beta)
        lane = lax.broadcasted_iota(jnp.int32, (CHUNK, BA_W), 1)
        row = lax.broadcasted_iota(jnp.int32, (CHUNK, 1), 0)
        dsc_acc = [jnp.zeros((CHUNK, BA_W), F32) for _ in range(2)]
        for c in chains:
            dq_ref, dk_ref, dv_ref, dsc_ref, dgr_ref = c.out
            end_row = CHUNK - 1 if c.d == 0 else 0
            dbeta = c.dbeta + jnp.sum(c.dkb * c.k, axis=-1, keepdims=True)
            m = c.dpm * c.qk + c.dam * c.kk
            kd_term = jnp.sum(c.dkd * c.kd, axis=-1, keepdims=True)
            dgcol = (jnp.sum(c.dqg * c.qg, axis=-1, keepdims=True) + jnp.sum(c.dkg * c.kg, axis=-1, keepdims=True)
                     - kd_term + jnp.sum(m, axis=-1, keepdims=True))
            dgr_ref[0, c.grp:c.grp + 1, :] = -jnp.sum(m, axis=0, keepdims=True)
            for t, h in enumerate(c.heads):
                rows, cols = grp_rows[t], c.cols[t]
                dq_ref[:, cols] = c.dq[rows]
                dk_ref[:, cols] = c.dk[rows]
                dv_ref[:, cols] = c.dr[rows]
                dg_end = jnp.sum(kd_term[rows]) + c.egc[t] * jnp.sum(c.dsp[t] * c.sh_b[t].astype(F32))
                dgcol_h = dgcol[rows] + jnp.where(row == end_row, dg_end, 0.0)
                dsc_acc[c.d] = jnp.where(lane == L_BETA + h, dbeta[rows], dsc_acc[c.d])
                dsc_acc[c.d] = jnp.where(lane == L_G + h, dgcol_h, dsc_acc[c.d])
                c.dstate[h] = (_dot_tn(c.qg_b[rows], c.do_b[rows]) + c.egc[t] * c.dsp[t]
                               - _dot_tn(c.kg_b[rows], c.dr_b[rows]))
        for d in range(2):
            outs[5 * d + 3][...] = dsc_acc[d]

    at = [lambda c: n - 1 - c, lambda c: c]
    in_specs, out_specs, args = [], [], []
    for d in range(2):
        tok = pl.BlockSpec((CHUNK, hd_all), lambda c, d=d: (at[d](c), 0))
        in_specs += [tok] * 3 + [pl.BlockSpec((1, CHUNK, BA_W), lambda c, d=d: (d, at[d](c), 0)),
                                 pl.BlockSpec((1, 1, N_GROUPS, GROWS), lambda c, d=d: (d, at[d](c), 0, 0)),
                                 pl.BlockSpec((1, HEADS, HEAD_DIM, HEAD_DIM), lambda c, d=d: (at[d](c), 0, 0, 0)),
                                 pl.BlockSpec((1, N_GROUPS, GROWS, GROWS), lambda c, d=d: (at[d](c), 0, 0, 0)),
                                 tok, tok]
        args += [q, k, v, scal, grow, states[d], tinv[d], vn[d], do]
        out_specs += [tok] * 3 + [pl.BlockSpec((CHUNK, BA_W), lambda c, d=d: (at[d](c), 0)),
                                  pl.BlockSpec((1, N_GROUPS, GROWS), lambda c, d=d: (at[d](c), 0, 0))]
    per_dir_shape = ([jax.ShapeDtypeStruct((s, hd_all), F32)] * 3
                     + [jax.ShapeDtypeStruct((s, BA_W), F32), jax.ShapeDtypeStruct((n, N_GROUPS, GROWS), F32)])
    res = _launch(
        body, carry, tuple(args), name=name, grid=(n,), in_specs=in_specs, out_specs=out_specs,
        out_shape=per_dir_shape * 2,
        scratch_shapes=[pltpu.VMEM((2, HEADS, HEAD_DIM, HEAD_DIM), F32), pltpu.VMEM((2, M_EYE, GROWS, GROWS), F32)],
        sem=("arbitrary",))
    outs, got = res if carry is not None else (res, None)
    paired = tuple((outs[i], outs[5 + i]) for i in range(5))
    return paired if carry is None else (paired, got)


def _gate_norm_fwd(o2, proj, dnw, *, name, ts=512):
    s = o2[0].shape[0]
    hd_all = HEADS * HEAD_DIM

    def body(of_ref, ob_ref, z_ref, w_ref, y_ref):
        w = w_ref[...]
        for hd in range(HEADS):
            cols = slice(hd * HEAD_DIM, (hd + 1) * HEAD_DIM)
            seg = of_ref[:, cols] + ob_ref[:, cols]
            r = lax.rsqrt(jnp.mean(seg * seg, axis=-1, keepdims=True) + EPS)
            z = z_ref[:, cols].astype(F32)
            y_ref[:, cols] = ((seg * r * w) * (z * _sigmoid(z))).astype(BF16)

    tile = pl.BlockSpec((ts, hd_all), lambda i: (i, 0))
    return pl.pallas_call(
        body, name=name, grid=(s // ts,),
        in_specs=[tile, tile, pl.BlockSpec((ts, Z_W), lambda i: (i, Z_OFF // Z_W)), _row(HEAD_DIM)],
        out_specs=tile,
        out_shape=jax.ShapeDtypeStruct((s, hd_all), BF16),
        compiler_params=_params(("parallel",)),
    )(o2[0], o2[1], proj, dnw)


def _gate_norm_bwd(dyb, o2, proj, dnw, *, name, ts=512):
    s = o2[0].shape[0]
    hd_all = HEADS * HEAD_DIM

    def body(dy_ref, of_ref, ob_ref, z_ref, w_ref, do_ref, dz_ref, dw_ref):
        w = w_ref[...]
        dw = jnp.zeros((1, HEAD_DIM), F32)
        for hd in range(HEADS):
            cols = slice(hd * HEAD_DIM, (hd + 1) * HEAD_DIM)
            seg = of_ref[:, cols] + ob_ref[:, cols]
            r = lax.rsqrt(jnp.mean(seg * seg, axis=-1, keepdims=True) + EPS)
            xhat = seg * r
            z = z_ref[:, cols].astype(F32)
            sg = _sigmoid(z)
            dy = dy_ref[:, cols]
            dnrm = dy * (z * sg)
            dz_ref[:, cols] = (dy * (xhat * w) * (sg * (1.0 + z * (1.0 - sg)))).astype(BF16)
            dw = dw + jnp.sum(dnrm * xhat, axis=0, keepdims=True)
            dxhat = dnrm * w
            do_ref[:, cols] = r * (dxhat - xhat * jnp.mean(dxhat * xhat, axis=-1, keepdims=True))

        @pl.when(pl.program_id(0) == 0)
        def _():
            dw_ref[...] = jnp.zeros_like(dw_ref)

        dw_ref[...] += dw

    tile = pl.BlockSpec((ts, hd_all), lambda i: (i, 0))
    return pl.pallas_call(
        body, name=name, grid=(s // ts,),
        in_specs=[tile, tile, tile, pl.BlockSpec((ts, Z_W), lambda i: (i, Z_OFF // Z_W)), _row(HEAD_DIM)],
        out_specs=[tile, tile, _row(HEAD_DIM)],
        out_shape=[jax.ShapeDtypeStruct((s, hd_all), F32), jax.ShapeDtypeStruct((s, hd_all), BF16),
                   jax.ShapeDtypeStruct((1, HEAD_DIM), F32)],
        compiler_params=_params(("arbitrary",)),
    )(dyb, o2[0], o2[1], proj, dnw)


def _merge_fwd(ya, yb, proj, wa, wb, wo, h, g, *, name, ts=512):
    s, d = h.shape

    def body(ya_ref, yb_ref, gt_ref, wa_ref, wb_ref, wo_ref, h_ref, g_ref, pa_ref, pb_ref, mix_ref, ho_ref):
        pa = _dot(ya_ref[...], wa_ref[...])
        pb = _dot(yb_ref[...], wb_ref[...])
        pa_ref[...] = pa.astype(BF16)
        pb_ref[...] = pb.astype(BF16)
        merged = (_sigmoid(gt_ref[:, :d].astype(F32)) * pa + _sigmoid(gt_ref[:, d:].astype(F32)) * pb)
        mix = _dot(merged.astype(BF16), wo_ref[...])
        mix_ref[...] = mix.astype(BF16)
        ho_ref[...] = h_ref[...] + g_ref[...] * mix

    tile = pl.BlockSpec((ts, d), lambda i: (i, 0))
    return pl.pallas_call(
        body, name=name, grid=(s // ts,),
        in_specs=[pl.BlockSpec((ts, CONV_A), lambda i: (i, 0)), tile,
                  pl.BlockSpec((ts, GATE_W), lambda i: (i, GATE_OFF // GATE_W)),
                  _resident(wa.shape), _resident(wb.shape), _resident(wo.shape), tile, _row(d)],
        out_specs=[tile, tile, tile, tile],
        out_shape=[jax.ShapeDtypeStruct((s, d), BF16)] * 3 + [jax.ShapeDtypeStruct((s, d), F32)],
        compiler_params=_params(("parallel",)),
    )(ya, yb, proj, wa, wb, wo, h, g)


def _merge_bwd(dh, g, mix, pa, pb, proj, wa, wb, wo, *, name, ts=256):
    s, d = dh.shape

    def body(dh_ref, g_ref, mix_ref, pa_ref, pb_ref, gt_ref, wa_ref, wb_ref, wo_ref,
             dmix_ref, mg_ref, dpa_ref, dpb_ref, dgt_ref, dya_ref, dyb_ref, dg_ref):
        dh_v = dh_ref[...]
        dmix = (g_ref[...] * dh_v).astype(BF16)
        dmix_ref[...] = dmix

        @pl.when(pl.program_id(0) == 0)
        def _():
            dg_ref[...] = jnp.zeros_like(dg_ref)

        dg_ref[...] += jnp.sum(dh_v * mix_ref[...].astype(F32), axis=0, keepdims=True)
        dmerged = _dot_nt(dmix, wo_ref[...])
        pa = pa_ref[...].astype(F32)
        pb = pb_ref[...].astype(F32)
        sa = _sigmoid(gt_ref[:, :d].astype(F32))
        sb = _sigmoid(gt_ref[:, d:].astype(F32))
        mg_ref[...] = (sa * pa + sb * pb).astype(BF16)
        dpa = (dmerged * sa).astype(BF16)
        dpb = (dmerged * sb).astype(BF16)
        dpa_ref[...] = dpa
        dpb_ref[...] = dpb
        dgt_ref[:, :d] = (dmerged * pa * sa * (1.0 - sa)).astype(BF16)
        dgt_ref[:, d:] = (dmerged * pb * sb * (1.0 - sb)).astype(BF16)
        dya_ref[...] = _dot_nt(dpa, wa_ref[...])
        dyb_ref[...] = _dot_nt(dpb, wb_ref[...])

    tile = pl.BlockSpec((ts, d), lambda i: (i, 0))
    return pl.pallas_call(
        body, name=name, grid=(s // ts,),
        in_specs=[tile, _row(d), tile, tile, tile,
                  pl.BlockSpec((ts, GATE_W), lambda i: (i, GATE_OFF // GATE_W)),
                  _resident(wa.shape), _resident(wb.shape), _resident(wo.shape)],
        out_specs=[tile, tile, tile, tile, pl.BlockSpec((ts, GATE_W), lambda i: (i, 0)),
                   pl.BlockSpec((ts, CONV_A), lambda i: (i, 0)), tile, _row(d)],
        out_shape=[jax.ShapeDtypeStruct((s, d), BF16)] * 4
                  + [jax.ShapeDtypeStruct((s, GATE_W), BF16), jax.ShapeDtypeStruct((s, CONV_A), F32),
                     jax.ShapeDtypeStruct((s, d), F32), jax.ShapeDtypeStruct((1, d), F32)],
        compiler_params=_params(("arbitrary",)),
    )(dh, g, mix, pa, pb, proj, wa, wb, wo)


def _final_fwd_bwd(h, nw, target, *, name, ts=512):
    s, d = h.shape

    def body(h_ref, nw_ref, t_ref, loss_ref, dh_ref, dnw_ref):
        x = h_ref[...]
        w = nw_ref[...]
        r = lax.rsqrt(jnp.mean(x * x, axis=-1, keepdims=True) + EPS)
        xhat = x * r
        e = xhat * w - t_ref[...]
        part = 0.5 * jnp.sum(jnp.mean(e * e, axis=-1, keepdims=True))
        dy = e * (1.0 / d)
        dxhat = dy * w
        dh_ref[...] = r * (dxhat - xhat * jnp.mean(dxhat * xhat, axis=-1, keepdims=True))

        @pl.when(pl.program_id(0) == 0)
        def _():
            loss_ref[...] = jnp.zeros_like(loss_ref)
            dnw_ref[...] = jnp.zeros_like(dnw_ref)

        loss_ref[...] += jnp.broadcast_to(part, loss_ref.shape)
        dnw_ref[...] += jnp.sum(dy * xhat, axis=0, keepdims=True)

    tile = pl.BlockSpec((ts, d), lambda i: (i, 0))
    return pl.pallas_call(
        body, name=name, grid=(s // ts,),
        in_specs=[tile, _row(d), tile],
        out_specs=[_row(128), tile, _row(d)],
        out_shape=[jax.ShapeDtypeStruct((1, 128), F32), jax.ShapeDtypeStruct((s, d), F32),
                   jax.ShapeDtypeStruct((1, d), F32)],
        compiler_params=_params(("arbitrary",)),
    )(h, nw, target)


EXT = 8


def _l2norm_heads_bwd(act, dout, scale):
    outs = []
    for hd in range(HEADS):
        cols = slice(hd * HEAD_DIM, (hd + 1) * HEAD_DIM)
        seg = act[:, cols]
        nrm = lax.rsqrt(jnp.sum(seg * seg, axis=-1, keepdims=True) + EPS)
        yhat = seg * nrm
        dsg = dout[:, cols]
        outs.append((scale * nrm) * (dsg - yhat * jnp.sum(yhat * dsg, axis=-1, keepdims=True)))
    return jnp.concatenate(outs, axis=-1)


def _conv_bwd(dq2, dk2, dv2, dya, proj, conv_dn, conv_a, *, name, ts=256, carry=None):
    s = proj.shape[0]
    hd = HEADS * HEAD_DIM
    nt = s // ts
    te = ts + 2 * EXT
    kdn, ka = conv_dn.shape[0], conv_a.shape[0]

    def body(*refs):
        (qp_ref, qc_ref, qn_ref, ap_ref, ac_ref, an_ref) = refs[0:6]
        d3 = refs[6:24]
        (yp_ref, yc_ref, yn_ref, wdn_ref, wa_ref) = refs[24:29]
        (dqkv_ref, da_ref, dwdn_ref, dwa_ref) = refs[29:33]
        xs_ref, dps_ref, xa_ref, dca_ref = refs[33:37]
        i = pl.program_id(0)
        first, last = i == 0, i == nt - 1

        @pl.when(first)
        def _():
            dwdn_ref[...] = jnp.zeros_like(dwdn_ref)
            dwa_ref[...] = jnp.zeros_like(dwa_ref)

        rowe = lax.broadcasted_iota(jnp.int32, (te, 1), 0)
        inside = ~((first & (rowe < EXT)) | (last & (rowe >= EXT + ts)))
        _fill_halo(xs_ref, qp_ref, qc_ref, qn_ref, first, last)
        wdn = wdn_ref[...]
        for part in range(3):
            cols = slice(part * hd, (part + 1) * hd)
            pre = _dwconv_rows(xs_ref, wdn[:, cols], HALO - EXT - 2, te, cols)
            sg = _sigmoid(pre)
            act = pre * sg
            pf, cf, nf, pb, cb, nb = d3[6 * part:6 * part + 6]
            dout = jnp.concatenate([pf[...] + pb[...], cf[...] + cb[...], nf[...] + nb[...]], axis=0)
            if part == 0:
                dact = _l2norm_heads_bwd(act, dout, Q_SCALE)
            elif part == 1:
                dact = _l2norm_heads_bwd(act, dout, 1.0)
            else:
                dact = dout
            dpre = jnp.where(inside, dact * (sg * (1.0 + pre * (1.0 - sg))), 0.0)
            dps_ref[:, cols] = dpre
            acc = wdn[0:1, cols] * dps_ref[EXT + 2:EXT + 2 + ts, cols]
            for tap in range(1, kdn):
                acc = acc + wdn[tap:tap + 1, cols] * dps_ref[EXT + 2 - tap:EXT + 2 - tap + ts, cols]
            dqkv_ref[:, cols] = acc.astype(BF16)
            dcur = dps_ref[EXT:EXT + ts, cols]
            for tap in range(kdn):
                dwdn_ref[tap:tap + 1, cols] += jnp.sum(
                    dcur * xs_ref[HALO - 2 + tap:HALO - 2 + tap + ts, cols], axis=0, keepdims=True)

        cv = lambda r: r[:, CONV_A:2 * CONV_A].astype(F32) * r[:, 2 * CONV_A:].astype(F32)
        _fill_halo(xa_ref, ap_ref, ac_ref, an_ref, first, last, fn=cv)
        wa = wa_ref[...]
        gate_b = jnp.concatenate([ap_ref[HALO - EXT:, 0:CONV_A], ac_ref[:, 0:CONV_A], an_ref[0:EXT, 0:CONV_A]],
                                 axis=0).astype(F32)
        dya_e = jnp.concatenate([yp_ref[...], yc_ref[...], yn_ref[...]], axis=0)
        dca_ref[...] = jnp.where(inside, dya_e * gate_b, 0.0)
        conv = _dwconv_rows(xa_ref, wa, HALO - 1, ts, slice(0, CONV_A))
        acc = wa[0:1, :] * dca_ref[EXT + 1:EXT + 1 + ts, :]
        for tap in range(1, ka):
            acc = acc + wa[tap:tap + 1, :] * dca_ref[EXT + 1 - tap:EXT + 1 - tap + ts, :]
        gc = ac_ref[:, CONV_A:2 * CONV_A].astype(F32)
        val = ac_ref[:, 2 * CONV_A:].astype(F32)
        da_ref[:, 0:CONV_A] = (yc_ref[...] * conv).astype(BF16)
        da_ref[:, CONV_A:2 * CONV_A] = (acc * val).astype(BF16)
        da_ref[:, 2 * CONV_A:] = (acc * gc).astype(BF16)
        dcur = dca_ref[EXT:EXT + ts, :]
        for tap in range(ka):
            dwa_ref[tap:tap + 1, :] += jnp.sum(
                dcur * xa_ref[HALO - 1 + tap:HALO - 1 + tap + ts, :], axis=0, keepdims=True)

    cot = [arr for pair in (dq2, dk2, dv2) for arr in pair for _ in range(3)]
    return _launch(
        body, carry, (proj, proj, proj, proj, proj, proj, *cot, dya, dya, dya, conv_dn, conv_a),
        name=name, grid=(nt,),
        in_specs=(_halo_specs(ts, QKV_W, 0, s) + _halo_specs(ts, A_W, A_OFF // A_W, s)
                  + _halo_specs(ts, hd, 0, s, rows=EXT) * 6 + _halo_specs(ts, CONV_A, 0, s, rows=EXT)
                  + [_resident(conv_dn.shape), _resident(conv_a.shape)]),
        out_specs=[pl.BlockSpec((ts, QKV_W), lambda i: (i, 0)), pl.BlockSpec((ts, A_W), lambda i: (i, 0)),
                   pl.BlockSpec((8, QKV_W), lambda i: (0, 0)), pl.BlockSpec((8, CONV_A), lambda i: (0, 0))],
        out_shape=[jax.ShapeDtypeStruct((s, QKV_W), BF16), jax.ShapeDtypeStruct((s, A_W), BF16),
                   jax.ShapeDtypeStruct((8, QKV_W), F32), jax.ShapeDtypeStruct((8, CONV_A), F32)],
        scratch_shapes=[pltpu.VMEM((ts + 2 * HALO, QKV_W), F32), pltpu.VMEM((te, QKV_W), F32),
                        pltpu.VMEM((ts + 2 * HALO, CONV_A), F32), pltpu.VMEM((te, CONV_A), F32)],
        sem=("arbitrary",))


IN_A = (0, 1536)
IN_QKV = (1536, 4608)
IN_Z = (4608, 5632)
IN_BA = 5632
IN_GATE = (5664, 7712)
IN_COLS = 7712
G_REPL = 4


def _split_w_in(w_in):
    sl = lambda ab: w_in[:, ab[0]:ab[1]]
    w_main = jnp.concatenate([sl(IN_QKV), sl(IN_Z), sl(IN_GATE), sl(IN_A)], axis=1)
    blocks = []
    for d in range(2):
        beta = w_in[:, IN_BA + 8 * d:IN_BA + 8 * d + 8]
        alpha = w_in[:, IN_BA + 16 + 8 * d:IN_BA + 24 + 8 * d]
        pad = jnp.zeros((w_in.shape[0], BA_W - 8 - 8 * G_REPL), w_in.dtype)
        blocks += [beta] + [alpha] * G_REPL + [pad]
    return w_main, jnp.concatenate(blocks, axis=1)


def _merge_dw_in(dw_qkv, dw_z, dw_gate, dw_a, dw_ba):
    ba = [dw_ba[:, 0:8], dw_ba[:, BA_W:BA_W + 8], dw_ba[:, 8:16], dw_ba[:, BA_W + 8:BA_W + 16]]
    return jnp.concatenate([dw_a, dw_qkv, dw_z] + ba + [dw_gate], axis=1)


def _decay_rows(a_log_fwd, dt_bias_fwd, a_log_bwd, dt_bias_bwd):
    def rows(f, b):
        out = []
        for vec in (f, b):
            vec = vec.reshape(HEADS)
            out.append(jnp.concatenate([jnp.zeros((8,), F32)] + [vec] * G_REPL
                                       + [jnp.zeros((BA_W - 8 - 8 * G_REPL,), F32)])[None])
        return jnp.stack(out)
    return rows(a_log_fwd, a_log_bwd), rows(dt_bias_fwd, dt_bias_bwd)


def _local_step(x, target, mod9, wt, comm):
    s, d = x.shape
    n = s // CHUNK
    wt = dict(wt)
    sh1, sc1, g1, sh2, sc2, g2, sh3, sc3, g3 = [mod9[i:i + 1] for i in range(9)]
    alog, dtb = _decay_rows(wt["a_log_fwd"], wt["dt_bias_fwd"], wt["a_log_bwd"], wt["dt_bias_bwd"])

    (u1, a1, b1, f1), got = comm.gather(
        ["w_ffn1_down", "w_in"],
        lambda c: _ffn_up_fwd(x, wt["norm_ffn1"], sc1, sh1, wt["w_ffn1_up"], name="ffn1_up", carry=c))
    wt.update(got)
    w_main, w_ba = _split_w_in(wt["w_in"])
    y1, h1 = _ffn_down_fwd(f1, wt["w_ffn1_down"], x, g1, name="ffn1_down")
    (u2, proj, ba), got = comm.gather(
        ["w_a_out", "w_b_out", "w_out", "w_ffn2_up", "w_ffn2_down"],
        lambda c: _in_proj_fwd(h1, wt["norm_mix"], sc2, sh2, w_main, w_ba, name="in_proj", carry=c))
    wt.update(got)
    scal = _scal_fwd(ba, alog, dtb, name="scal_fwd")
    grow = scal[:, :, L_G:L_G + 8].reshape(2, n, CHUNK, HEADS).transpose(0, 1, 3, 2).reshape(
        2, n, N_GROUPS, GROWS)
    q, k, v, ya = _conv_fwd(proj, wt["conv_dn"], wt["conv_a"], name="conv_fwd")
    o2, states, tinv, vn = _delta_fwd(q, k, v, scal, grow, name="delta_fwd")
    yb = _gate_norm_fwd(o2, proj, wt["dn_norm"], name="gate_norm_fwd")
    pa, pb, mix, h2 = _merge_fwd(ya, yb, proj, wt["w_a_out"], wt["w_b_out"], wt["w_out"], h1, g2,
                                 name="merge_fwd")
    u3, a3, b3, f3 = _ffn_up_fwd(h2, wt["norm_ffn2"], sc3, sh3, wt["w_ffn2_up"], name="ffn2_up")
    y3, h3 = _ffn_down_fwd(f3, wt["w_ffn2_down"], h2, g3, name="ffn2_down")
    loss, dh3, dnorm_final = _final_fwd_bwd(h3, wt["norm_final"], target, name="final")

    dy3, dab3, dg3 = _ffn_bwd_act(dh3, g3, y3, a3, b3, wt["w_ffn2_down"], name="ffn2_bwd_act")
    dh2, dn3, dsc3, dsh3 = _norm_mod_matmul_bwd([(dab3, wt["w_ffn2_up"])], h2, wt["norm_ffn2"], sc3, dh3,
                                                name="ffn2_bwd_up")
    gw = {}
    gw["w_ffn2_up"] = _matmul_tn(u3, dab3, name="dw_ffn2_up", tm=1024, tn=1408)
    gw["w_ffn2_down"] = _matmul_tn(f3, dy3, name="dw_ffn2_down", tm=1408, tn=1024)

    dmix, merged, dpa, dpb, dgates, dya, dyb, dg2 = _merge_bwd(
        dh2, g2, mix, pa, pb, proj, wt["w_a_out"], wt["w_b_out"], wt["w_out"], name="merge_bwd")
    gw["w_out"] = _matmul_tn(merged, dmix, name="dw_out", tm=1024, tn=1024)
    gw["w_a_out"] = _matmul_tn(ya, dpa, name="dw_a_out", tm=512, tn=1024)
    gw["w_b_out"] = _matmul_tn(yb, dpb, name="dw_b_out", tm=1024, tn=1024)
    do, dz, ddn = _gate_norm_bwd(dyb, o2, proj, wt["dn_norm"], name="gate_norm_bwd")
    recv = {}
    (dq2, dk2, dv2, dscal, drow), got = comm.scatter(
        {nm: gw.pop(nm) for nm in ("w_ffn2_up", "w_ffn2_down")},
        lambda c: _delta_bwd(q, k, v, scal, grow, states, tinv, vn, do, name="delta_bwd", carry=c))
    recv.update(got)
    drow_p = jnp.pad(jnp.stack(drow).reshape(2, n, HEADS, CHUNK).transpose(0, 1, 3, 2).reshape(2, s, HEADS),
                     ((0, 0), (0, 0), (L_G, BA_W - L_G - HEADS)))
    dba, dalog, ddtb = _scal_bwd(jnp.stack(dscal), drow_p, ba, alog, dtb, name="scal_bwd")
    (dqkv, dbr_a, dconv_dn, dconv_a), got = comm.scatter(
        {nm: gw.pop(nm) for nm in ("w_out", "w_a_out", "w_b_out")},
        lambda c: _conv_bwd(dq2, dk2, dv2, dya, proj, wt["conv_dn"], wt["conv_a"], name="conv_bwd", carry=c))
    recv.update(got)
    dw_in = _merge_dw_in(
        _matmul_tn(u2, dqkv, name="dw_in_qkv", tm=1024, tn=1536),
        _matmul_tn(u2, dz, name="dw_in_z", tm=1024, tn=1024),
        _matmul_tn(u2, dgates, name="dw_in_gate", tm=1024, tn=1024),
        _matmul_tn(u2, dbr_a, name="dw_in_a", tm=1024, tn=1536),
        _matmul_tn(u2, dba, name="dw_in_ba", tm=1024, tn=2 * BA_W))
    (dh1, dn2, dsc2, dsh2), got = comm.scatter(
        {"w_in": dw_in},
        lambda c: _norm_mod_matmul_bwd(
            [(dqkv, w_main, 0), (dz, w_main, Z_OFF // Z_W), (dgates, w_main, GATE_OFF // GATE_W),
             (dbr_a, w_main, A_OFF // A_W), (dba, w_ba)],
            h1, wt["norm_mix"], sc2, dh2, name="in_proj_bwd", carry=c))
    recv.update(got)

    dy1, dab1, dg1 = _ffn_bwd_act(dh1, g1, y1, a1, b1, wt["w_ffn1_down"], name="ffn1_bwd_act")
    dw_down1 = _matmul_tn(f1, dy1, name="dw_ffn1_down", tm=1408, tn=1024)
    dw_up1, got = comm.scatter(
        {"w_ffn1_down": dw_down1},
        lambda c: _matmul_tn(u1, dab1, name="dw_ffn1_up", tm=1024, tn=1408, carry=c))
    recv.update(got)
    (dx, dn1, dsc1, dsh1), got = comm.scatter(
        {"w_ffn1_up": dw_up1},
        lambda c: _norm_mod_matmul_bwd([(dab1, wt["w_ffn1_up"])], x, wt["norm_ffn1"], sc1, dh1,
                                       name="ffn1_bwd_up", carry=c))
    recv.update(got)

    small = {
        "mod": jnp.concatenate([dsh1, dsc1, dg1, dsh2, dsc2, dg2, dsh3, dsc3, dg3], axis=1),
        "norm_ffn1": dn1, "norm_mix": dn2, "norm_ffn2": dn3, "norm_final": dnorm_final,
        "a_log_fwd": dalog[0, :, L_G:L_G + 8], "dt_bias_fwd": ddtb[0, :, L_G:L_G + 8],
        "a_log_bwd": dalog[1, :, L_G:L_G + 8], "dt_bias_bwd": ddtb[1, :, L_G:L_G + 8],
        "dn_norm": ddn,
        "conv_a": dconv_a[0:3].reshape(1, -1), "conv_dn": dconv_dn[0:5].reshape(1, -1),
    }
    return loss, dx, recv, small


def _full_weight(name, g):
    if name in COL_SHARDED + CONV_SHARDED:
        return g.transpose(1, 0, 2).reshape(g.shape[1], -1)
    return g.reshape(-1, g.shape[-1])


def _grad_pieces(name, g):
    g = g.astype(BF16)
    if name in COL_SHARDED:
        return g.reshape(g.shape[0], N_DEV, -1).transpose(1, 0, 2)
    return g.reshape(N_DEV, -1, g.shape[-1])


class _MeshComm:
    def __init__(self, shards):
        self.shards = shards

    def _run(self, xs, carrier, name, gather):
        if carrier is None:
            return None, _exchange(xs, name=name, gather=gather)
        return carrier((xs, gather))

    def gather(self, names, carrier=None, name=None):
        outs, got = self._run([self.shards[nm] for nm in names], carrier, name, True)
        return outs, {nm: _full_weight(nm, g) for nm, g in zip(names, got)}

    def scatter(self, grads, carrier=None, name=None):
        names = list(grads)
        outs, got = self._run([_grad_pieces(nm, grads[nm]) for nm in names], carrier, name, False)
        return outs, dict(zip(names, got))


def _mod_fwd(c_all, w_ada, *, name):
    def body(c_ref, w_ref, o_ref):
        cv = c_ref[...]
        o_ref[...] = _dot3(cv * _sigmoid(cv), w_ref[...])

    return pl.pallas_call(
        body, name=name, out_shape=jax.ShapeDtypeStruct((c_all.shape[0], w_ada.shape[1]), F32),
        compiler_params=_params(),
    )(c_all, w_ada)


def _adamw_math(w, g, m, v):
    m_new = ADAM_B1 * m + (1.0 - ADAM_B1) * g
    v_new = ADAM_B2 * v + (1.0 - ADAM_B2) * (g * g)
    m_hat = m_new / (1.0 - ADAM_B1 ** ADAM_STEP)
    v_hat = v_new / (1.0 - ADAM_B2 ** ADAM_STEP)
    delta = -ADAM_LR * (m_hat / (jnp.sqrt(v_hat) + ADAM_EPS) + ADAM_WD * w)
    return delta, m_new, v_new


def _reduce_adamw(pieces, w, m, v, *, name, tr):
    r, c = w.shape

    def body(p_ref, w_ref, m_ref, v_ref, g_ref, d_ref, mo_ref, vo_ref):
        g = p_ref[0].astype(F32)
        for src in range(1, N_DEV):
            g = g + p_ref[src].astype(F32)
        g_ref[...] = g
        d_ref[...], mo_ref[...], vo_ref[...] = _adamw_math(w_ref[...], g, m_ref[...], v_ref[...])

    tile = pl.BlockSpec((tr, c), lambda i: (i, 0))
    return pl.pallas_call(
        body, name=name, grid=(r // tr,),
        in_specs=[pl.BlockSpec((N_DEV, tr, c), lambda i: (0, i, 0)), tile, tile, tile],
        out_specs=[tile] * 4, out_shape=[jax.ShapeDtypeStruct((r, c), F32)] * 4,
        compiler_params=_params(("parallel",)),
    )(pieces, w, m, v)


def _ada_grad_adamw(c_all_t, dmod_cols, w, m, v, *, name, tr=256):
    r, c = w.shape

    def body(c_ref, dm_ref, w_ref, m_ref, v_ref, g_ref, d_ref, mo_ref, vo_ref):
        cv = c_ref[...]
        act = cv * _sigmoid(cv)
        dm = dm_ref[...]
        g = act[:, 0:1] * dm[0:1, :]
        for b in range(1, N_DEV):
            g = g + act[:, b:b + 1] * dm[b:b + 1, :]
        g_ref[...] = g
        d_ref[...], mo_ref[...], vo_ref[...] = _adamw_math(w_ref[...], g, m_ref[...], v_ref[...])

    tile = pl.BlockSpec((tr, c), lambda i: (i, 0))
    return pl.pallas_call(
        body, name=name, grid=(r // tr,),
        in_specs=[pl.BlockSpec((tr, N_DEV), lambda i: (i, 0)), pl.BlockSpec((N_DEV, c), lambda i: (0, 0)),
                  tile, tile, tile],
        out_specs=[tile] * 4, out_shape=[jax.ShapeDtypeStruct((r, c), F32)] * 4,
        compiler_params=_params(("parallel",)),
    )(c_all_t, dmod_cols, w, m, v)


def _sum_rows(parts, *, name):
    def body(p_ref, o_ref):
        acc = p_ref[0:1, :]
        for src in range(1, N_DEV):
            acc = acc + p_ref[src:src + 1, :]
        o_ref[...] = acc

    return pl.pallas_call(
        body, name=name, out_shape=jax.ShapeDtypeStruct((1, parts.shape[1]), F32), compiler_params=_params(),
    )(parts)


def _adamw_rows(g, w, m, v, *, name):
    def body(g_ref, w_ref, m_ref, v_ref, d_ref, mo_ref, vo_ref):
        d_ref[...], mo_ref[...], vo_ref[...] = _adamw_math(w_ref[...], g_ref[...], m_ref[...], v_ref[...])

    return pl.pallas_call(
        body, name=name, out_shape=[jax.ShapeDtypeStruct(g.shape, F32)] * 3, compiler_params=_params(),
    )(g, w, m, v)


WEIGHTS = ["w_ada", "b_ada", "norm_ffn1", "w_ffn1_up", "w_ffn1_down", "norm_mix", "w_in", "conv_a", "conv_dn",
           "a_log_fwd", "dt_bias_fwd", "a_log_bwd", "dt_bias_bwd", "dn_norm", "w_a_out", "w_b_out", "w_out",
           "norm_ffn2", "w_ffn2_up", "w_ffn2_down", "norm_final"]
COL_SHARDED = ["w_ffn1_up", "w_in", "w_a_out", "w_ffn2_up"]
ROW_SHARDED = ["w_ffn1_down", "w_b_out", "w_out", "w_ffn2_down"]
CONV_SHARDED = ["conv_a", "conv_dn"]
REPLICATED = ["b_ada", "norm_ffn1", "norm_mix", "a_log_fwd", "dt_bias_fwd", "a_log_bwd", "dt_bias_bwd",
              "dn_norm", "norm_ffn2", "norm_final"]
SMALL_ORDER = ["mod", "norm_ffn1", "norm_mix", "norm_ffn2", "norm_final", "a_log_fwd", "dt_bias_fwd",
               "a_log_bwd", "dt_bias_bwd", "dn_norm", "conv_a", "conv_dn"]
REDUCE_ROWS = {"w_ffn1_up": 256, "w_in": 256, "w_a_out": 256, "w_ffn2_up": 256,
               "w_ffn1_down": 176, "w_b_out": 128, "w_out": 128, "w_ffn2_down": 176}


def _pad_lanes(row):
    pad = (-row.shape[1]) % 128
    return jnp.pad(row, ((0, 0), (0, pad)))


def _unstack_cols(g):
    return g.transpose(1, 0, 2).reshape(g.shape[1], -1)


def _stack_cols(w):
    k = w.shape[0]
    return w.reshape(k, N_DEV, -1).transpose(1, 0, 2)


def kernel(x, c, w_ada, b_ada, norm_ffn1, w_ffn1_up, w_ffn1_down, norm_mix, w_in, conv_a, conv_dn, a_log_fwd, dt_bias_fwd, a_log_bwd, dt_bias_bwd, dn_norm, w_a_out, w_b_out, w_out, norm_ffn2, w_ffn2_up, w_ffn2_down, norm_final, loss_target, m_w_ada, m_b_ada, m_norm_ffn1, m_w_ffn1_up, m_w_ffn1_down, m_norm_mix, m_w_in, m_conv_a, m_conv_dn, m_a_log_fwd, m_dt_bias_fwd, m_a_log_bwd, m_dt_bias_bwd, m_dn_norm, m_w_a_out, m_w_b_out, m_w_out, m_norm_ffn2, m_w_ffn2_up, m_w_ffn2_down, m_norm_final, v_w_ada, v_b_ada, v_norm_ffn1, v_w_ffn1_up, v_w_ffn1_down, v_norm_mix, v_w_in, v_conv_a, v_conv_dn, v_a_log_fwd, v_dt_bias_fwd, v_a_log_bwd, v_dt_bias_bwd, v_dn_norm, v_w_a_out, v_w_b_out, v_w_out, v_norm_ffn2, v_w_ffn2_up, v_w_ffn2_down, v_norm_final):
    args = dict(locals())
    w_loc = {n: args[n] for n in WEIGHTS}
    m_loc = {n: args["m_" + n] for n in WEIGHTS}
    v_loc = {n: args["v_" + n] for n in WEIGHTS}
    me = _flat_index(_my_position())
    d_model = x.shape[-1]

    big = COL_SHARDED + ROW_SHARDED
    shards = {n: w_loc[n][0].astype(BF16) for n in big}
    shards.update({n: w_loc[n][0] for n in CONV_SHARDED})
    shards["c"] = c
    comm = _MeshComm(shards)
    wt = comm.gather(["c", "conv_a", "conv_dn", "w_ffn1_up"], name="gather_first")[1]
    c_all = wt.pop("c")
    for n in REPLICATED[1:]:
        wt[n] = w_loc[n].reshape(1, -1)

    mod_cols = _mod_fwd(c_all, w_ada[0], name="mod_fwd")
    mod_all = _exchange([mod_cols], name="gather_mod", gather=True)[0]
    mod_mine = lax.dynamic_index_in_dim(mod_all, me, axis=1, keepdims=False).reshape(1, -1) + b_ada
    mod9 = mod_mine.reshape(9, d_model)

    loss_loc, dx, recv, small = _local_step(x[0], loss_target[0], mod9, wt, comm)
    loss = lax.psum(loss_loc[0, 0], MESH_AXES)

    res = {}
    for n in big:
        res[n] = _reduce_adamw(recv[n], w_loc[n][0], m_loc[n][0], v_loc[n][0], name="adamw_" + n,
                               tr=REDUCE_ROWS[n])

    packed = _pad_lanes(jnp.concatenate([small[n].reshape(1, -1) for n in SMALL_ORDER], axis=1))
    parts = _exchange([packed], name="gather_small", gather=True)[0].reshape(N_DEV, -1)
    total = _sum_rows(parts, name="sum_small")
    off = 0
    gsmall = {}
    for n in SMALL_ORDER:
        size = small[n].size
        gsmall[n] = total[:, off:off + size]
        off += size
    dmod_all = parts[:, 0:9 * d_model]
    ada_cols = w_ada.shape[-1]
    dmod_cols = lax.dynamic_slice_in_dim(dmod_all, me * ada_cols, ada_cols, axis=1)
    res["w_ada"] = _ada_grad_adamw(c_all.T, dmod_cols, w_ada[0], m_w_ada[0], v_w_ada[0], name="adamw_w_ada")
    g_rows = {"b_ada": gsmall["mod"]}
    for n in REPLICATED[1:]:
        g_rows[n] = gsmall[n]
    for n in CONV_SHARDED:
        taps, width = w_loc[n].shape[1], w_loc[n].shape[2]
        full = gsmall[n].reshape(taps, -1)
        g_rows[n] = lax.dynamic_slice_in_dim(full, me * width, width, axis=1).reshape(1, -1)
    row_names = REPLICATED + CONV_SHARDED
    cat = lambda src: _pad_lanes(jnp.concatenate([src[n].reshape(1, -1) for n in row_names], axis=1))
    g_cat = cat(g_rows)
    d_cat, m_cat, v_cat = _adamw_rows(g_cat, cat(w_loc), cat(m_loc), cat(v_loc), name="adamw_small")
    off = 0
    for n in row_names:
        size = w_loc[n].size
        res[n] = tuple(t[:, off:off + size] for t in (g_cat, d_cat, m_cat, v_cat))
        off += size

    outs = [loss, dx[None]]
    for kind in range(4):
        for n in WEIGHTS:
            outs.append(res[n][kind].reshape(w_loc[n].shape))
    return tuple(outs)
```

```python
import functools
import math
import types

import jax
import jax.numpy as jnp
from jax import lax
from jax.experimental import pallas as pl
from jax.experimental.pallas import tpu as pltpu

F32 = jnp.float32
BF16 = jnp.bfloat16
EPS = 1e-6
N_DEV = 8
CHUNK = 64
HEADS = 8
HEAD_DIM = 128
MESH_AXES = ("x", "y", "c")
VMEM_LIMIT_BYTES = 56 * 1024 * 1024

ADAM_LR = 0.001
ADAM_B1 = 0.9
ADAM_B2 = 0.999
ADAM_EPS = 1e-08
ADAM_WD = 0.01
ADAM_STEP = 10


def _params(sem=None):
    return pltpu.CompilerParams(dimension_semantics=sem, vmem_limit_bytes=VMEM_LIMIT_BYTES)


def _row(n):
    return pl.BlockSpec((1, n), lambda *_: (0, 0))


def _resident(shape):
    nd = len(shape)
    return pl.BlockSpec(shape, lambda *_: (0,) * nd, pipeline_mode=pl.Buffered(1))


def _col_chunks(width, chunk=512):
    return [slice(lo, min(lo + chunk, width)) for lo in range(0, width, chunk)]


def _col_window(w, width, col_block):
    return pl.BlockSpec((w.shape[0], width), lambda *_: (0, col_block), pipeline_mode=pl.Buffered(1))


def _sigmoid(x):
    return 1.0 / (1.0 + jnp.exp(-x))


def _dot(a, b):
    return jnp.dot(a, b, preferred_element_type=F32)


def _dot_nt(a, b):
    return lax.dot_general(a, b, (((1,), (1,)), ((), ())), preferred_element_type=F32)


def _dot_tn(a, b):
    return lax.dot_general(a, b, (((0,), (0,)), ((), ())), preferred_element_type=F32)


def _split_bf16(x):
    hi = x.astype(BF16)
    lo = (x - hi.astype(F32)).astype(BF16)
    return hi, lo


def _dot3(a, b, dot=_dot):
    ah, al = a if isinstance(a, tuple) else _split_bf16(a)
    bh, bl = b if isinstance(b, tuple) else _split_bf16(b)
    return dot(ah, bh) + dot(ah, bl) + dot(al, bh)


def _dot_exact(a, b):
    return jnp.dot(a, b, preferred_element_type=F32, precision=lax.Precision.HIGHEST)


def _my_position():
    return tuple(lax.axis_index(a) for a in MESH_AXES)


def _peer(pos, kk):
    return tuple((1 - p) if (kk >> (2 - b)) & 1 else p for b, p in enumerate(pos))


def _flat_index(pos):
    return pos[0] * 4 + pos[1] * 2 + pos[2]


_ANY = pl.BlockSpec(memory_space=pl.ANY)


class _AllToAll:
    def __init__(self, in_refs, out_refs, send_sems, recv_sems, local_sems):
        pos = _my_position()
        me = _flat_index(pos)
        self.copies = []
        for t in range(len(in_refs)):
            self.copies.append(pltpu.make_async_copy(in_refs[t].at[me], out_refs[t].at[me], local_sems.at[t]))
            for kk in range(1, N_DEV):
                peer = _peer(pos, kk)
                self.copies.append(pltpu.make_async_remote_copy(
                    src_ref=in_refs[t].at[_flat_index(peer)], dst_ref=out_refs[t].at[me],
                    send_sem=send_sems.at[t, kk - 1], recv_sem=recv_sems.at[t, kk - 1],
                    device_id=peer, device_id_type=pl.DeviceIdType.MESH))

    def start(self):
        for cp in self.copies:
            cp.start()

    def finish(self):
        for cp in self.copies:
            cp.wait()


class _AllGather:
    def __init__(self, in_refs, out_refs, send_sems, recv_sems, local_sems):
        self.refs = (in_refs, out_refs, send_sems, recv_sems, local_sems)
        x, y, c = _my_position()
        self.me, self.sibling = (x, y, c), (x, y, 1 - c)
        self.chips = [(1 - x, y), (x, 1 - y), (1 - x, 1 - y)]
        self.core = c

    def _copy(self, t, k, block, to, own=False):
        in_refs, out_refs, send_sems, recv_sems, _ = self.refs
        rows = out_refs[t].at[_flat_index(block)]
        return pltpu.make_async_remote_copy(
            src_ref=in_refs[t] if own else rows, dst_ref=rows,
            send_sem=send_sems.at[t, k], recv_sem=recv_sems.at[t, k],
            device_id=to, device_id_type=pl.DeviceIdType.MESH)

    def _local(self, t):
        in_refs, out_refs, _, _, local_sems = self.refs
        return pltpu.make_async_copy(in_refs[t], out_refs[t].at[_flat_index(self.me)], local_sems.at[t])

    def start(self):
        c = self.core
        for t in range(len(self.refs[0])):
            self._local(t).start()
            self._copy(t, 0, self.me, self.sibling, own=True).start()
            for j, chip in enumerate(self.chips):
                self._copy(t, 1 + j, self.me, (*chip, c), own=True).start()

    def finish(self):
        c = self.core
        n_t = len(self.refs[0])
        for t in range(n_t):
            for j, chip in enumerate(self.chips):
                self._copy(t, 1 + j, (*chip, c), self.me).wait_recv()
                self._copy(t, 4 + j, (*chip, c), self.sibling).start()
        for t in range(n_t):
            self._copy(t, 0, self.sibling, self.me).wait_recv()
            for j, chip in enumerate(self.chips):
                self._copy(t, 4 + j, (*chip, 1 - c), self.me).wait_recv()
            self._copy(t, 0, self.me, self.sibling, own=True).wait_send()
            for j, chip in enumerate(self.chips):
                self._copy(t, 1 + j, self.me, (*chip, c), own=True).wait_send()
                self._copy(t, 4 + j, (*chip, c), self.sibling).wait_send()
            self._local(t).wait()


def _exchange_plan(in_refs, out_refs, send_sems, recv_sems, local_sems, gather):
    return (_AllGather if gather else _AllToAll)(in_refs, out_refs, send_sems, recv_sems, local_sems)


def _exchange_shapes(xs, gather):
    out_shape = [jax.ShapeDtypeStruct(((N_DEV,) + x.shape) if gather else x.shape, x.dtype) for x in xs]
    sems = [pltpu.SemaphoreType.DMA((len(xs), N_DEV - 1)), pltpu.SemaphoreType.DMA((len(xs), N_DEV - 1)),
            pltpu.SemaphoreType.DMA((len(xs),))]
    return out_shape, sems


def _exchange(xs, *, name, gather):
    nt = len(xs)

    def body(*refs):
        plan = _exchange_plan(refs[:nt], refs[nt:2 * nt], *refs[2 * nt:], gather)
        plan.start()
        plan.finish()

    out_shape, sems = _exchange_shapes(xs, gather)
    return pl.pallas_call(body, name=name, in_specs=[_ANY] * nt, out_specs=[_ANY] * nt, out_shape=out_shape,
                          scratch_shapes=sems)(*xs)


def _launch(body, carry, args, *, name, grid, in_specs, out_specs, out_shape, scratch_shapes=(), sem):
    single = not isinstance(out_shape, (list, tuple))
    out_specs = [out_specs] if single else list(out_specs)
    out_shape = [out_shape] if single else list(out_shape)
    if carry is None:
        outs = pl.pallas_call(body, name=name, grid=grid, in_specs=list(in_specs), out_specs=out_specs,
                              out_shape=out_shape, scratch_shapes=list(scratch_shapes),
                              compiler_params=_params(sem))(*args)
        return outs[0] if single else outs
    xs, gather = carry
    nt, n_in, n_out, n_scr = len(xs), len(args), len(out_shape), len(scratch_shapes)
    x_shape, sems = _exchange_shapes(xs, gather)

    def wrapped(*refs):
        c_in, x_in = refs[:n_in], refs[n_in:n_in + nt]
        c_out = refs[n_in + nt:n_in + nt + n_out]
        x_out = refs[n_in + nt + n_out:n_in + 2 * nt + n_out]
        scr = refs[n_in + 2 * nt + n_out:]
        ids = [pl.program_id(a) for a in range(len(grid))]
        first = functools.reduce(jnp.logical_and, [i == 0 for i in ids])
        last = functools.reduce(jnp.logical_and, [i == g - 1 for i, g in zip(ids, grid)])
        plan = lambda: _exchange_plan(x_in, x_out, *scr[n_scr:], gather)

        @pl.when(first)
        def _():
            plan().start()

        body(*c_in, *c_out, *scr[:n_scr])

        @pl.when(last)
        def _():
            plan().finish()

    outs = pl.pallas_call(
        wrapped, name=name, grid=grid, in_specs=list(in_specs) + [_ANY] * nt,
        out_specs=out_specs + [_ANY] * nt, out_shape=out_shape + x_shape,
        scratch_shapes=list(scratch_shapes) + sems,
        compiler_params=_params(("arbitrary",) * len(grid)))(*args, *xs)
    compute = outs[:n_out]
    return (compute[0] if single else compute), outs[n_out:]


def _norm_mod(x, nw, sc, sh):
    r = lax.rsqrt(jnp.mean(x * x, axis=-1, keepdims=True) + EPS)
    return (x * r * nw) * (1.0 + sc) + sh


def _norm_mod_bwd(x, nw, sc, du):
    r = lax.rsqrt(jnp.mean(x * x, axis=-1, keepdims=True) + EPS)
    xhat = x * r
    n = xhat * nw
    dsh = jnp.sum(du, axis=0, keepdims=True)
    dsc = jnp.sum(du * n, axis=0, keepdims=True)
    dn = du * (1.0 + sc)
    dnw = jnp.sum(dn * xhat, axis=0, keepdims=True)
    dxhat = dn * nw
    dx = r * (dxhat - xhat * jnp.mean(dxhat * xhat, axis=-1, keepdims=True))
    return dx, dnw, dsc, dsh


def _ffn_up_fwd(h, nw, sc, sh, wup, *, name, ts=512, tn=1408, carry=None):
    s, d = h.shape
    f_dim = wup.shape[1] // 2
    nj = f_dim // tn

    def body(h_ref, nw_ref, sc_ref, sh_ref, wa_ref, wb_ref, u_ref, a_ref, b_ref, f_ref):
        @pl.when(pl.program_id(1) == 0)
        def _():
            u_ref[...] = _norm_mod(h_ref[...], nw_ref[...], sc_ref[...], sh_ref[...]).astype(BF16)

        u = u_ref[...]

        def epilogue(a, b, cs):
            a_ref[:, cs] = a.astype(BF16)
            b_ref[:, cs] = b.astype(BF16)
            f_ref[:, cs] = (a * _sigmoid(a) * b).astype(BF16)

        pending = None
        for cs in _col_chunks(tn):
            products = (_dot(u, wa_ref[:, cs]), _dot(u, wb_ref[:, cs]), cs)
            if pending is not None:
                epilogue(*pending)
            pending = products
        epilogue(*pending)

    return _launch(
        body, carry, (h, nw, sc, sh, wup, wup), name=name, grid=(s // ts, nj),
        in_specs=[pl.BlockSpec((ts, d), lambda i, j: (i, 0)), _row(d), _row(d), _row(d),
                  pl.BlockSpec((d, tn), lambda i, j: (0, j)),
                  pl.BlockSpec((d, tn), lambda i, j: (0, j + nj))],
        out_specs=[pl.BlockSpec((ts, d), lambda i, j: (i, 0)),
                   pl.BlockSpec((ts, tn), lambda i, j: (i, j)),
                   pl.BlockSpec((ts, tn), lambda i, j: (i, j)),
                   pl.BlockSpec((ts, tn), lambda i, j: (i, j))],
        out_shape=[jax.ShapeDtypeStruct((s, d), BF16)] + [jax.ShapeDtypeStruct((s, f_dim), BF16)] * 3,
        sem=("parallel", "arbitrary"))


def _ffn_down_fwd(f, wd, h, g, *, name, ts=512):
    s, f_dim = f.shape
    d = wd.shape[1]

    def body(f_ref, wd_ref, h_ref, g_ref, y_ref, ho_ref):
        y = _dot(f_ref[...], wd_ref[...])
        y_ref[...] = y.astype(BF16)
        ho_ref[...] = h_ref[...] + (0.5 * g_ref[...]) * y

    return pl.pallas_call(
        body, name=name, grid=(s // ts,),
        in_specs=[pl.BlockSpec((ts, f_dim), lambda i: (i, 0)), _resident((f_dim, d)),
                  pl.BlockSpec((ts, d), lambda i: (i, 0)), _row(d)],
        out_specs=[pl.BlockSpec((ts, d), lambda i: (i, 0)), pl.BlockSpec((ts, d), lambda i: (i, 0))],
        out_shape=[jax.ShapeDtypeStruct((s, d), BF16), jax.ShapeDtypeStruct((s, d), F32)],
        compiler_params=_params(("parallel",)),
    )(f, wd, h, g)


def _ffn_bwd_act(dh, g, y, a, b, wd, *, name, ts=256, carry=None):
    s, d = dh.shape
    f_dim = a.shape[1]

    def body(dh_ref, g_ref, y_ref, a_ref, b_ref, wd_ref, dy_ref, dab_ref, dg_ref):
        dh_v = dh_ref[...]
        dy = ((0.5 * g_ref[...]) * dh_v).astype(BF16)
        dy_ref[...] = dy
        part = jnp.sum(0.5 * dh_v * y_ref[...].astype(F32), axis=0, keepdims=True)

        @pl.when(pl.program_id(0) == 0)
        def _():
            dg_ref[...] = jnp.zeros_like(dg_ref)

        dg_ref[...] += part

        def epilogue(df, cs):
            av = a_ref[:, cs].astype(F32)
            bv = b_ref[:, cs].astype(F32)
            sg = _sigmoid(av)
            dab_ref[:, cs] = (df * bv * (sg * (1.0 + av * (1.0 - sg)))).astype(BF16)
            dab_ref[:, slice(f_dim + cs.start, f_dim + cs.stop)] = (df * (av * sg)).astype(BF16)

        pending = None
        for cs in _col_chunks(f_dim):
            product = (_dot_nt(dy, wd_ref[cs, :]), cs)
            if pending is not None:
                epilogue(*pending)
            pending = product
        epilogue(*pending)

    return _launch(
        body, carry, (dh, g, y, a, b, wd), name=name, grid=(s // ts,),
        in_specs=[pl.BlockSpec((ts, d), lambda i: (i, 0)), _row(d),
                  pl.BlockSpec((ts, d), lambda i: (i, 0)),
                  pl.BlockSpec((ts, f_dim), lambda i: (i, 0)),
                  pl.BlockSpec((ts, f_dim), lambda i: (i, 0)),
                  _resident((f_dim, d))],
        out_specs=[pl.BlockSpec((ts, d), lambda i: (i, 0)),
                   pl.BlockSpec((ts, 2 * f_dim), lambda i: (i, 0)), _row(d)],
        out_shape=[jax.ShapeDtypeStruct((s, d), BF16), jax.ShapeDtypeStruct((s, 2 * f_dim), BF16),
                   jax.ShapeDtypeStruct((1, d), F32)],
        sem=("arbitrary",))


def _norm_mod_matmul_bwd(pairs, h, nw, sc, dh_in, *, name, ts=256, carry=None):
    s, d = h.shape
    n_pairs = len(pairs)

    def body(*refs):
        dx_refs = refs[:n_pairs]
        w_refs = refs[n_pairs:2 * n_pairs]
        h_ref, nw_ref, sc_ref, dhi_ref, dho_ref, dnw_ref, dsc_ref, dsh_ref = refs[2 * n_pairs:]
        du = _dot_nt(dx_refs[0][...], w_refs[0][...])
        for k in range(1, n_pairs):
            du = du + _dot_nt(dx_refs[k][...], w_refs[k][...])
        dx, dnw, dsc, dsh = _norm_mod_bwd(h_ref[...], nw_ref[...], sc_ref[...], du)
        dho_ref[...] = dhi_ref[...] + dx

        @pl.when(pl.program_id(0) == 0)
        def _():
            dnw_ref[...] = jnp.zeros_like(dnw_ref)
            dsc_ref[...] = jnp.zeros_like(dsc_ref)
            dsh_ref[...] = jnp.zeros_like(dsh_ref)

        dnw_ref[...] += dnw
        dsc_ref[...] += dsc
        dsh_ref[...] += dsh

    dxs = [p[0] for p in pairs]
    ws = [p[1] for p in pairs]
    tile = pl.BlockSpec((ts, d), lambda i: (i, 0))
    return _launch(
        body, carry, (*dxs, *ws, h, nw, sc, dh_in), name=name, grid=(s // ts,),
        in_specs=([pl.BlockSpec((ts, x.shape[1]), lambda i: (i, 0)) for x in dxs]
                  + [_col_window(w, x.shape[1], p[2] if len(p) > 2 else 0) for p, x, w in zip(pairs, dxs, ws)]
                  + [tile, _row(d), _row(d), tile]),
        out_specs=[tile, _row(d), _row(d), _row(d)],
        out_shape=[jax.ShapeDtypeStruct((s, d), F32)] + [jax.ShapeDtypeStruct((1, d), F32)] * 3,
        sem=("arbitrary",))


def _matmul_tn(a, b, *, name, tm, tn, tk=1024, carry=None):
    s, m = a.shape
    n = b.shape[1]
    tk = min(tk, s)
    nk = s // tk

    def body(a_ref, b_ref, o_ref, acc_ref):
        k = pl.program_id(2)

        @pl.when(k == 0)
        def _():
            acc_ref[...] = jnp.zeros_like(acc_ref)

        acc_ref[...] += _dot_tn(a_ref[...], b_ref[...])

        @pl.when(k == nk - 1)
        def _():
            o_ref[...] = acc_ref[...].astype(o_ref.dtype)

    return _launch(
        body, carry, (a, b), name=name, grid=(m // tm, n // tn, nk),
        in_specs=[pl.BlockSpec((tk, tm), lambda i, j, k: (k, i)),
                  pl.BlockSpec((tk, tn), lambda i, j, k: (k, j))],
        out_specs=pl.BlockSpec((tm, tn), lambda i, j, k: (i, j)),
        out_shape=jax.ShapeDtypeStruct((m, n), BF16),
        scratch_shapes=[pltpu.VMEM((tm, tn), F32)],
        sem=("parallel", "parallel", "arbitrary"))


def _in_proj_fwd(h, nw, sc, sh, w_main, w_ba, *, name, ts=1024, tn=1536, carry=None):
    s, d = h.shape
    ts = min(ts, s)
    n_main = w_main.shape[1]
    n_ba = w_ba.shape[1]

    def body(h_ref, nw_ref, sc_ref, sh_ref, w_ref, wba_ref, u_ref, p_ref, ba_ref):
        @pl.when(pl.program_id(1) == 0)
        def _():
            u0 = _norm_mod(h_ref[...], nw_ref[...], sc_ref[...], sh_ref[...]).astype(BF16)
            u_ref[...] = u0
            ba_ref[...] = _dot(u0, wba_ref[...])

        p_ref[...] = _dot(u_ref[...], w_ref[...]).astype(BF16)

    return _launch(
        body, carry, (h, nw, sc, sh, w_main, w_ba), name=name, grid=(s // ts, n_main // tn),
        in_specs=[pl.BlockSpec((ts, d), lambda i, j: (i, 0)), _row(d), _row(d), _row(d),
                  pl.BlockSpec((d, tn), lambda i, j: (0, j)), _resident((d, n_ba))],
        out_specs=[pl.BlockSpec((ts, d), lambda i, j: (i, 0)),
                   pl.BlockSpec((ts, tn), lambda i, j: (i, j)),
                   pl.BlockSpec((ts, n_ba), lambda i, j: (i, 0))],
        out_shape=[jax.ShapeDtypeStruct((s, d), BF16), jax.ShapeDtypeStruct((s, n_main), BF16),
                   jax.ShapeDtypeStruct((s, n_ba), F32)],
        sem=("parallel", "arbitrary"))


QKV_W = 3 * HEADS * HEAD_DIM
Z_OFF, Z_W = 3072, 1024
GATE_OFF, GATE_W = 4096, 2048
A_OFF, A_W = 6144, 1536
N_MAIN = 7680
CONV_A = 512
BA_W = 128

L_BETA, L_G, L_EG, L_EKD, L_EGC = 0, 8, 16, 24, 32


def _softplus(z):
    e = jnp.exp(-jnp.abs(z))
    small = e * (1.0 - e * (0.5 - e * (1.0 / 3.0)))
    return jnp.maximum(z, 0.0) + jnp.where(e < 1e-3, small, jnp.log(1.0 + e))


def _tri(n, sgn, strict=False):
    i = lax.broadcasted_iota(jnp.int32, (n, n), 0)
    j = lax.broadcasted_iota(jnp.int32, (n, n), 1)
    dlt = (i - j) * sgn
    return (dlt > 0) if strict else (dlt >= 0)


def _scal_fwd(ba, alog, dtb, *, name, ts=512):
    s = ba.shape[0]

    def body(ba_ref, al_ref, dt_ref, o_ref):
        d = pl.program_id(0)
        sgn = 1 - 2 * d
        x = ba_ref[...]
        lane = lax.broadcasted_iota(jnp.int32, x.shape, 1)
        beta = _sigmoid(x)
        g = -jnp.exp(al_ref[0]) * _softplus(x + dt_ref[0])
        g = jnp.where((lane >= L_G) & (lane < L_EGC + 8), g, 0.0)
        ltri = jnp.where(_tri(CHUNK, sgn), 1.0, 0.0).astype(F32)
        for c in range(ts // CHUNK):
            rows = slice(c * CHUNK, (c + 1) * CHUNK)
            gc = _dot_exact(ltri, g[rows])
            g_end = jnp.where(d == 0, gc[CHUNK - 1:CHUNK], gc[0:1])
            ln = lane[rows]
            out = jnp.where(ln < L_G, beta[rows],
                  jnp.where(ln < L_EG, gc,
                  jnp.where(ln < L_EKD, jnp.exp(gc),
                  jnp.where(ln < L_EGC, jnp.exp(g_end - gc),
                  jnp.where(ln < L_EGC + 8, jnp.broadcast_to(jnp.exp(g_end), gc.shape), 0.0)))))
            o_ref[0, rows, :] = out

    return pl.pallas_call(
        body, name=name, grid=(2, s // ts),
        in_specs=[pl.BlockSpec((ts, BA_W), lambda d, i: (i, d)),
                  pl.BlockSpec((1, 1, BA_W), lambda d, i: (d, 0, 0)),
                  pl.BlockSpec((1, 1, BA_W), lambda d, i: (d, 0, 0))],
        out_specs=pl.BlockSpec((1, ts, BA_W), lambda d, i: (d, i, 0)),
        out_shape=jax.ShapeDtypeStruct((2, s, BA_W), F32),
        compiler_params=_params(("parallel", "parallel")),
    )(ba, alog, dtb)


def _scal_bwd(dscal, drow, ba, alog, dtb, *, name, ts=512):
    s = ba.shape[0]

    def body(ds_ref, dr_ref, ba_ref, al_ref, dt_ref, dba_ref, dal_ref, ddt_ref):
        d = pl.program_id(0)
        sgn = 1 - 2 * d
        x = ba_ref[...]
        lane = lax.broadcasted_iota(jnp.int32, x.shape, 1)
        in_g = (lane >= L_G) & (lane < L_G + 8)
        beta = _sigmoid(x)
        z = x + dt_ref[0]
        neg_a = -jnp.exp(al_ref[0])
        g = neg_a * _softplus(z)
        dsv = ds_ref[0]
        dgc = jnp.where(in_g, dsv + dr_ref[0], 0.0)
        utri = jnp.where(_tri(CHUNK, -sgn), 1.0, 0.0).astype(F32)
        dal = jnp.zeros((1, BA_W), F32)
        ddt = jnp.zeros((1, BA_W), F32)
        for c in range(ts // CHUNK):
            rows = slice(c * CHUNK, (c + 1) * CHUNK)
            dg = _dot_exact(utri, dgc[rows])
            dz = dg * neg_a * _sigmoid(z[rows])
            dal = dal + jnp.sum(dg * g[rows], axis=0, keepdims=True)
            ddt = ddt + jnp.sum(dz, axis=0, keepdims=True)
            b = beta[rows]
            out = jnp.where(lane[rows] < L_G, dsv[rows] * b * (1.0 - b), jnp.where(in_g[rows], dz, 0.0))
            dba_ref[rows, :] = out.astype(BF16)

        @pl.when(pl.program_id(1) == 0)
        def _():
            dal_ref[...] = jnp.zeros_like(dal_ref)
            ddt_ref[...] = jnp.zeros_like(ddt_ref)

        dal_ref[0] += dal
        ddt_ref[0] += ddt

    row3 = pl.BlockSpec((1, 1, BA_W), lambda d, i: (d, 0, 0))
    tok3 = pl.BlockSpec((1, ts, BA_W), lambda d, i: (d, i, 0))
    return pl.pallas_call(
        body, name=name, grid=(2, s // ts),
        in_specs=[tok3, tok3, pl.BlockSpec((ts, BA_W), lambda d, i: (i, d)), row3, row3],
        out_specs=[pl.BlockSpec((ts, BA_W), lambda d, i: (i, d)), row3, row3],
        out_shape=[jax.ShapeDtypeStruct((s, 2 * BA_W), BF16), jax.ShapeDtypeStruct((2, 1, BA_W), F32),
                   jax.ShapeDtypeStruct((2, 1, BA_W), F32)],
        compiler_params=_params(("arbitrary", "arbitrary")),
    )(dscal, drow, ba, alog, dtb)


HALO = 16


def _halo_specs(ts, width, col_block, n_rows, rows=HALO):
    r = ts // rows
    last = n_rows // rows - 1
    return [pl.BlockSpec((rows, width), lambda i: (jnp.maximum(i * r - 1, 0), col_block)),
            pl.BlockSpec((ts, width), lambda i: (i, col_block)),
            pl.BlockSpec((rows, width), lambda i: (jnp.minimum((i + 1) * r, last), col_block))]


def _fill_halo(dst_ref, prev_ref, cur_ref, next_ref, first, last, fn=lambda r: r[...].astype(F32)):
    h = prev_ref.shape[0]
    ts = cur_ref.shape[0]
    p = fn(prev_ref)
    n = fn(next_ref)
    dst_ref[0:h, :] = jnp.where(first, 0.0, p)
    dst_ref[h:h + ts, :] = fn(cur_ref)
    dst_ref[h + ts:h + ts + h, :] = jnp.where(last, 0.0, n)


def _dwconv_rows(src_ref, w, start, n_rows, cols):
    acc = w[0:1, :] * src_ref[start:start + n_rows, cols]
    for i in range(1, w.shape[0]):
        acc = acc + w[i:i + 1, :] * src_ref[start + i:start + i + n_rows, cols]
    return acc


def _l2norm_heads(act, scale):
    outs = []
    for hd in range(HEADS):
        seg = act[:, hd * HEAD_DIM:(hd + 1) * HEAD_DIM]
        outs.append(seg * (lax.rsqrt(jnp.sum(seg * seg, axis=-1, keepdims=True) + EPS) * scale))
    return jnp.concatenate(outs, axis=-1)


Q_SCALE = HEAD_DIM ** -0.5


def _conv_fwd(proj, conv_dn, conv_a, *, name, ts=256):
    s = proj.shape[0]
    hd = HEADS * HEAD_DIM
    nt = s // ts

    def body(qp_ref, qc_ref, qn_ref, ap_ref, ac_ref, an_ref, wdn_ref, wa_ref,
             q_ref, k_ref, v_ref, ya_ref, xs_ref, xa_ref):
        i = pl.program_id(0)
        first, last = i == 0, i == nt - 1
        _fill_halo(xs_ref, qp_ref, qc_ref, qn_ref, first, last)
        wdn = wdn_ref[...]
        for part, o_ref in enumerate((q_ref, k_ref, v_ref)):
            cols = slice(part * hd, (part + 1) * hd)
            pre = _dwconv_rows(xs_ref, wdn[:, cols], HALO - 2, ts, cols)
            act = pre * _sigmoid(pre)
            if part == 0:
                act = _l2norm_heads(act, Q_SCALE)
            elif part == 1:
                act = _l2norm_heads(act, 1.0)
            o_ref[...] = act
        cv = lambda r: r[:, CONV_A:2 * CONV_A].astype(F32) * r[:, 2 * CONV_A:].astype(F32)
        _fill_halo(xa_ref, ap_ref, ac_ref, an_ref, first, last, fn=cv)
        conv = _dwconv_rows(xa_ref, wa_ref[...], HALO - 1, ts, slice(0, CONV_A))
        ya_ref[...] = (ac_ref[:, 0:CONV_A].astype(F32) * conv).astype(BF16)

    tile = lambda w: pl.BlockSpec((ts, w), lambda i: (i, 0))
    return pl.pallas_call(
        body, name=name, grid=(nt,),
        in_specs=(_halo_specs(ts, QKV_W, 0, s) + _halo_specs(ts, A_W, A_OFF // A_W, s)
                  + [_resident(conv_dn.shape), _resident(conv_a.shape)]),
        out_specs=[tile(hd), tile(hd), tile(hd), tile(CONV_A)],
        out_shape=[jax.ShapeDtypeStruct((s, hd), F32)] * 3 + [jax.ShapeDtypeStruct((s, CONV_A), BF16)],
        scratch_shapes=[pltpu.VMEM((ts + 2 * HALO, QKV_W), F32), pltpu.VMEM((ts + 2 * HALO, CONV_A), F32)],
        compiler_params=_params(("parallel",)),
    )(proj, proj, proj, proj, proj, proj, conv_dn, conv_a)


def _chunk_of_step(d, c, n):
    return c + d * (n - 1 - 2 * c)


def _head_scalars(scv, grv, hd):
    col = lambda base: scv[:, base + hd:base + hd + 1]
    return (col(L_BETA), col(L_G), col(L_EG), col(L_EKD),
            scv[0:1, L_EGC + hd:L_EGC + hd + 1], grv[hd:hd + 1, :])


def _decay_matrix(gcol, grow, incl):
    return jnp.where(incl, jnp.exp(jnp.minimum(gcol - grow, 0.0)), 0.0)


INV_BASE = 8


def _unit_lower_inverse(a_m, top=None):
    n = a_m.shape[0]
    top = top or n
    i = lax.broadcasted_iota(jnp.int32, (n, n), 0)
    j = lax.broadcasted_iota(jnp.int32, (n, n), 1)

    def same_block(m):
        sh = int(math.log2(m))
        return jnp.right_shift(i, sh) == jnp.right_shift(j, sh)

    x = jnp.where(same_block(INV_BASE), -a_m, 0.0)
    t = jnp.where(i == j, 1.0, 0.0) + x
    p = x
    for _ in range(int(math.log2(INV_BASE)) - 1):
        p_b = p.astype(BF16)
        p = _dot(p_b, p_b)
        t = t + _dot(t.astype(BF16), p.astype(BF16))
    m = INV_BASE
    while m < top:
        join = jnp.where(same_block(2 * m) & jnp.logical_not(same_block(m)), a_m, 0.0)
        t_b = t.astype(BF16)
        t = t - _dot(_dot(t_b, join.astype(BF16)).astype(BF16), t_b)
        m *= 2
    return t


def _delta_fwd_per_head(q, k, v, scal, grow, *, name):
    s = q.shape[0]
    n = s // CHUNK
    hd_all = HEADS * HEAD_DIM

    def body(q_ref, k_ref, v_ref, sc_ref, gr_ref, o_ref, st_ref, t_ref, vn_ref, state):
        d = pl.program_id(0)
        sgn = 1 - 2 * d

        @pl.when(pl.program_id(1) == 0)
        def _():
            state[...] = jnp.zeros_like(state)

        incl = _tri(CHUNK, sgn)
        strict = _tri(CHUNK, sgn, strict=True)
        scv = sc_ref[0]
        grv = gr_ref[0, 0]
        for hd in range(HEADS):
            cols = slice(hd * HEAD_DIM, (hd + 1) * HEAD_DIM)
            qh, kh, vh = q_ref[:, cols], k_ref[:, cols], v_ref[:, cols]
            beta, gcol, eg, ekd, egc, grow_h = _head_scalars(scv, grv, hd)
            dm = _decay_matrix(gcol, grow_h, incl)
            k_b = kh.astype(BF16)
            kk = _dot_nt((kh * beta).astype(BF16), k_b)
            t = _unit_lower_inverse(jnp.where(strict, kk * dm, 0.0))
            p_m = jnp.where(incl, _dot_nt(qh.astype(BF16), k_b) * dm, 0.0)
            sh = state[hd]
            sh_b = sh.astype(BF16)
            st_ref[0, 0, hd] = sh_b
            r = vh - _dot((kh * eg).astype(BF16), sh_b)
            vn = _dot3(t, beta * r)
            vn_b = vn.astype(BF16)
            o_ref[0, :, cols] = _dot((qh * eg).astype(BF16), sh_b) + _dot(p_m.astype(BF16), vn_b)
            state[hd] = egc * sh + _dot_tn((kh * ekd).astype(BF16), vn_b)
            t_ref[0, 0, hd] = t
            vn_ref[0, :, cols] = vn_b

    tok = lambda d, c: (_chunk_of_step(d, c, n), 0)
    dtok = lambda d, c: (d, _chunk_of_step(d, c, n), 0)
    dchunk4 = lambda d, c: (d, _chunk_of_step(d, c, n), 0, 0)
    dchunk5 = lambda d, c: (d, _chunk_of_step(d, c, n), 0, 0, 0)
    return pl.pallas_call(
        body, name=name, grid=(2, n),
        in_specs=[pl.BlockSpec((CHUNK, hd_all), tok)] * 3
                 + [pl.BlockSpec((1, CHUNK, BA_W), dtok), pl.BlockSpec((1, 1, HEADS, CHUNK), dchunk4)],
        out_specs=[pl.BlockSpec((1, CHUNK, hd_all), dtok),
                   pl.BlockSpec((1, 1, HEADS, HEAD_DIM, HEAD_DIM), dchunk5),
                   pl.BlockSpec((1, 1, HEADS, CHUNK, CHUNK), dchunk5),
                   pl.BlockSpec((1, CHUNK, hd_all), dtok)],
        out_shape=[jax.ShapeDtypeStruct((2, s, hd_all), F32),
                   jax.ShapeDtypeStruct((2, n, HEADS, HEAD_DIM, HEAD_DIM), BF16),
                   jax.ShapeDtypeStruct((2, n, HEADS, CHUNK, CHUNK), F32),
                   jax.ShapeDtypeStruct((2, s, hd_all), BF16)],
        scratch_shapes=[pltpu.VMEM((HEADS, HEAD_DIM, HEAD_DIM), F32)],
        compiler_params=_params(("arbitrary", "arbitrary")),
    )(q, k, v, scal, grow)


def _delta_bwd_per_head(q, k, v, scal, grow, states, tinv, vn, do, *, name):
    s = q.shape[0]
    n = s // CHUNK
    hd_all = HEADS * HEAD_DIM

    def body(q_ref, k_ref, v_ref, sc_ref, gr_ref, st_ref, t_ref, vn_ref, do_ref,
             dq_ref, dk_ref, dv_ref, dsc_ref, dgr_ref, dstate):
        d = pl.program_id(0)
        sgn = 1 - 2 * d

        @pl.when(pl.program_id(1) == 0)
        def _():
            dstate[...] = jnp.zeros_like(dstate)

        incl = _tri(CHUNK, sgn)
        strict = _tri(CHUNK, sgn, strict=True)
        scv = sc_ref[0]
        grv = gr_ref[0, 0]
        lane = lax.broadcasted_iota(jnp.int32, (CHUNK, BA_W), 1)
        row = lax.broadcasted_iota(jnp.int32, (CHUNK, 1), 0)
        sub = lax.broadcasted_iota(jnp.int32, (HEADS, CHUNK), 0)
        end_row = jnp.where(d == 0, CHUNK - 1, 0)
        dsc_acc = jnp.zeros((CHUNK, BA_W), F32)
        dgr_acc = jnp.zeros((HEADS, CHUNK), F32)
        for hd in range(HEADS):
            cols = slice(hd * HEAD_DIM, (hd + 1) * HEAD_DIM)
            qh, kh, vh = q_ref[:, cols], k_ref[:, cols], v_ref[:, cols]
            beta, gcol, eg, ekd, egc, grow_h = _head_scalars(scv, grv, hd)
            dm = _decay_matrix(gcol, grow_h, incl)
            q_b, k_b = qh.astype(BF16), kh.astype(BF16)
            kb_b = (kh * beta).astype(BF16)
            kk = _dot_nt(kb_b, k_b)
            qk = _dot_nt(q_b, k_b)
            p_m = jnp.where(incl, qk * dm, 0.0)
            t = t_ref[0, 0, hd]
            vn_b = vn_ref[0, :, cols]
            sh_b = st_ref[0, 0, hd]
            dsp = dstate[hd]
            dsp_b = dsp.astype(BF16)
            do_b = do_ref[:, cols].astype(BF16)
            kg, qg, kd = kh * eg, qh * eg, kh * ekd
            kg_b, qg_b, kd_b = kg.astype(BF16), qg.astype(BF16), kd.astype(BF16)
            r = vh - _dot(kg_b, sh_b)
            dvn = _dot_tn(p_m.astype(BF16), do_b) + _dot(kd_b, dsp_b)
            db = _dot3(t, dvn, dot=_dot_tn)
            dr = db * beta
            dbeta = jnp.sum(db * r, axis=-1, keepdims=True)
            dr_b, db_b = dr.astype(BF16), db.astype(BF16)
            dkg = -_dot_nt(dr_b, sh_b)
            dqg = _dot_nt(do_b, sh_b)
            dkd = _dot_nt(vn_b, dsp_b)
            dpm = jnp.where(incl, _dot_nt(do_b, vn_b), 0.0) * dm
            dam = jnp.where(strict, -_dot_nt(db_b, vn_b), 0.0) * dm
            dpm_b, dam_b = dpm.astype(BF16), dam.astype(BF16)
            dkb = _dot(dam_b, k_b)
            dq_ref[0, :, cols] = dqg * eg + _dot(dpm_b, k_b)
            dk_ref[0, :, cols] = (dkg * eg + dkd * ekd + _dot_tn(dpm_b, q_b) + _dot_tn(dam_b, kb_b)
                                  + dkb * beta)
            dv_ref[0, :, cols] = dr
            dbeta = dbeta + jnp.sum(dkb * kh, axis=-1, keepdims=True)
            m = dpm * qk + dam * kk
            kd_term = jnp.sum(dkd * kd, axis=-1, keepdims=True)
            dgcol = (jnp.sum(dqg * qg, axis=-1, keepdims=True) + jnp.sum(dkg * kg, axis=-1, keepdims=True)
                     - kd_term + jnp.sum(m, axis=-1, keepdims=True))
            dg_end = jnp.sum(kd_term) + egc * jnp.sum(dsp * sh_b.astype(F32))
            dgcol = dgcol + jnp.where(row == end_row, dg_end, 0.0)
            dsc_acc = jnp.where(lane == L_BETA + hd, dbeta, dsc_acc)
            dsc_acc = jnp.where(lane == L_G + hd, dgcol, dsc_acc)
            dgr_acc = jnp.where(sub == hd, -jnp.sum(m, axis=0, keepdims=True), dgr_acc)
            dstate[hd] = _dot_tn(qg_b, do_b) + egc * dsp - _dot_tn(kg_b, dr_b)
        dsc_ref[0] = dsc_acc
        dgr_ref[0, 0] = dgr_acc

    step = lambda d, c: n - 1 - _chunk_of_step(d, c, n)
    tok = lambda d, c: (step(d, c), 0)
    dtok = lambda d, c: (d, step(d, c), 0)
    dchunk4 = lambda d, c: (d, step(d, c), 0, 0)
    dchunk5 = lambda d, c: (d, step(d, c), 0, 0, 0)
    tok_spec = pl.BlockSpec((CHUNK, hd_all), tok)
    dtok_spec = pl.BlockSpec((1, CHUNK, hd_all), dtok)
    return pl.pallas_call(
        body, name=name, grid=(2, n),
        in_specs=[tok_spec] * 3
                 + [pl.BlockSpec((1, CHUNK, BA_W), dtok), pl.BlockSpec((1, 1, HEADS, CHUNK), dchunk4),
                    pl.BlockSpec((1, 1, HEADS, HEAD_DIM, HEAD_DIM), dchunk5),
                    pl.BlockSpec((1, 1, HEADS, CHUNK, CHUNK), dchunk5), dtok_spec, tok_spec],
        out_specs=[dtok_spec] * 3
                  + [pl.BlockSpec((1, CHUNK, BA_W), dtok), pl.BlockSpec((1, 1, HEADS, CHUNK), dchunk4)],
        out_shape=[jax.ShapeDtypeStruct((2, s, hd_all), F32)] * 3
                  + [jax.ShapeDtypeStruct((2, s, BA_W), F32), jax.ShapeDtypeStruct((2, n, HEADS, CHUNK), F32)],
        scratch_shapes=[pltpu.VMEM((HEADS, HEAD_DIM, HEAD_DIM), F32)],
        compiler_params=_params(("arbitrary", "arbitrary")),
    )(q, k, v, scal, grow, states, tinv, vn, do)


GROUP = 4
GROWS = GROUP * CHUNK
N_GROUPS = HEADS // GROUP


def _stack(parts):
    return jnp.concatenate(parts, axis=0)


M_INCL, M_STRICT, M_EYE, M_BASE, M_JOIN = 0, 1, 2, 3, 4
JOIN_SIZES = (16, 32, 64)
N_MASKS = M_JOIN + len(JOIN_SIZES)


def _write_group_masks(mask_ref, sgn, n_masks):
    i = lax.broadcasted_iota(jnp.int32, (GROWS, GROWS), 0)
    j = lax.broadcasted_iota(jnp.int32, (GROWS, GROWS), 1)
    same = lambda m: jnp.right_shift(i, int(math.log2(m))) == jnp.right_shift(j, int(math.log2(m)))
    dlt = (i - j) * sgn
    one = lambda cond: jnp.where(cond, 1.0, 0.0).astype(F32)
    mask_ref[M_INCL] = one(same(CHUNK) & (dlt >= 0))
    mask_ref[M_STRICT] = one(same(CHUNK) & (dlt > 0))
    if n_masks > M_EYE:
        mask_ref[M_EYE] = one(i == j)
        mask_ref[M_BASE] = one(same(INV_BASE))
        for lvl, m in enumerate(JOIN_SIZES):
            mask_ref[M_JOIN + lvl] = one(same(m) & jnp.logical_not(same(m // 2)))


def _group_decay(gcol, grow, mask_ref):
    return jnp.exp(jnp.minimum(gcol - grow, 0.0)) * mask_ref[M_INCL]


def _block_inverse_many(a_ms, mask_refs):
    xs = [-(a * m[M_BASE]) for a, m in zip(a_ms, mask_refs)]
    ts = [m[M_EYE] + x for x, m in zip(xs, mask_refs)]
    ps = xs
    for _ in range(int(math.log2(INV_BASE)) - 1):
        p_bs = [p.astype(BF16) for p in ps]
        ps = [_dot(p_b, p_b) for p_b in p_bs]
        ts = [t + _dot(t.astype(BF16), p.astype(BF16)) for t, p in zip(ts, ps)]
    for lvl in range(len(JOIN_SIZES)):
        t_bs = [t.astype(BF16) for t in ts]
        joins = [(a * m[M_JOIN + lvl]).astype(BF16) for a, m in zip(a_ms, mask_refs)]
        mids = [_dot(t_b, j).astype(BF16) for t_b, j in zip(t_bs, joins)]
        ts = [t - _dot(mid, t_b) for t, mid, t_b in zip(ts, mids, t_bs)]
    return ts


def _block_inverse(a_m, mask_ref):
    x = -(a_m * mask_ref[M_BASE])
    t = mask_ref[M_EYE] + x
    p = x
    for _ in range(int(math.log2(INV_BASE)) - 1):
        p_b = p.astype(BF16)
        p = _dot(p_b, p_b)
        t = t + _dot(t.astype(BF16), p.astype(BF16))
    for lvl in range(len(JOIN_SIZES)):
        t_b = t.astype(BF16)
        t = t - _dot(_dot(t_b, (a_m * mask_ref[M_JOIN + lvl]).astype(BF16)).astype(BF16), t_b)
    return t


def _group_operands(q_ref, k_ref, v_ref, scv, grp):
    heads = [GROUP * grp + t for t in range(GROUP)]
    tiles = lambda ref: [ref[:, h * HEAD_DIM:(h + 1) * HEAD_DIM] for h in heads]
    col = lambda base: [scv[:, base + h:base + h + 1] for h in heads]
    egc = [scv[0:1, L_EGC + h:L_EGC + h + 1] for h in heads]
    return heads, tiles(q_ref), tiles(k_ref), tiles(v_ref), col(L_BETA), col(L_G), col(L_EG), col(L_EKD), egc


def _delta_fwd(q, k, v, scal, grow, *, name):
    s = q.shape[0]
    n = s // CHUNK
    hd_all = HEADS * HEAD_DIM

    def body(*refs):
        ins, outs, (state, mask_ref) = refs[:10], refs[10:18], refs[18:]

        @pl.when(pl.program_id(0) == 0)
        def _():
            state[...] = jnp.zeros_like(state)
            for d in range(2):
                _write_group_masks(mask_ref.at[d], 1 - 2 * d, N_MASKS)

        chains = []
        for d in range(2):
            q_ref, k_ref, v_ref, sc_ref, gr_ref = ins[5 * d:5 * d + 5]
            scv = sc_ref[0]
            for grp in range(N_GROUPS):
                chains.append(dict(
                    d=d, grp=grp, gr_ref=gr_ref, out=outs[4 * d:4 * d + 4], state=state.at[d], masks=mask_ref.at[d],
                    ops=_group_operands(q_ref, k_ref, v_ref, scv, grp)))
        for ch in chains:
            heads, qs, ks, vs, beta, gcol, eg, ekd, egc = ch["ops"]
            ch["dm"] = _group_decay(_stack(gcol), ch["gr_ref"][0, 0, ch["grp"]:ch["grp"] + 1, :], ch["masks"])
            ch["k_b"] = _stack(ks).astype(BF16)
            ch["kb_b"] = _stack([ks[t] * beta[t] for t in range(GROUP)]).astype(BF16)
        for ch in chains:
            ch["a_m"] = _dot_nt(ch["kb_b"], ch["k_b"]) * ch["dm"] * ch["masks"][M_STRICT]
        tinvs = _block_inverse_many([ch["a_m"] for ch in chains], [ch["masks"] for ch in chains])
        for ch, tinv in zip(chains, tinvs):
            heads, qs, ks, vs, beta, gcol, eg, ekd, egc = ch["ops"]
            o_ref, st_ref, t_ref, vn_ref = ch["out"]
            ch["tinv"] = tinv.astype(BF16)
            t_ref[0, ch["grp"]] = ch["tinv"]
            ch["p_b"] = (_dot_nt(_stack(qs).astype(BF16), ch["k_b"]) * ch["dm"]).astype(BF16)
            ch["sh"] = [ch["state"][h] for h in heads]
            ch["sh_b"] = [x.astype(BF16) for x in ch["sh"]]
            for t, h in enumerate(heads):
                st_ref[0, h] = ch["sh_b"][t]
        for ch in chains:
            heads, qs, ks, vs, beta, gcol, eg, ekd, egc = ch["ops"]
            ch["br"] = _stack([beta[t] * (vs[t] - _dot((ks[t] * eg[t]).astype(BF16), ch["sh_b"][t]))
                               for t in range(GROUP)]).astype(BF16)
        for ch in chains:
            ch["vn_b"] = _dot(ch["tinv"], ch["br"]).astype(BF16)
        for ch in chains:
            ch["o_intra"] = _dot(ch["p_b"], ch["vn_b"])
        for ch in chains:
            heads, qs, ks, vs, beta, gcol, eg, ekd, egc = ch["ops"]
            o_ref, st_ref, t_ref, vn_ref = ch["out"]
            for t, h in enumerate(heads):
                rows = slice(t * CHUNK, (t + 1) * CHUNK)
                cols = slice(h * HEAD_DIM, (h + 1) * HEAD_DIM)
                o_ref[:, cols] = _dot((qs[t] * eg[t]).astype(BF16), ch["sh_b"][t]) + ch["o_intra"][rows]
                ch["state"][h] = egc[t] * ch["sh"][t] + _dot_tn((ks[t] * ekd[t]).astype(BF16), ch["vn_b"][rows])
                vn_ref[:, cols] = ch["vn_b"][rows]

    at = [lambda c: c, lambda c: n - 1 - c]
    in_specs, out_specs = [], []
    for d in range(2):
        tok = pl.BlockSpec((CHUNK, hd_all), lambda c, d=d: (at[d](c), 0))
        in_specs += [tok] * 3 + [pl.BlockSpec((1, CHUNK, BA_W), lambda c, d=d: (d, at[d](c), 0)),
                                 pl.BlockSpec((1, 1, N_GROUPS, GROWS), lambda c, d=d: (d, at[d](c), 0, 0))]
        out_specs += [tok, pl.BlockSpec((1, HEADS, HEAD_DIM, HEAD_DIM), lambda c, d=d: (at[d](c), 0, 0, 0)),
                      pl.BlockSpec((1, N_GROUPS, GROWS, GROWS), lambda c, d=d: (at[d](c), 0, 0, 0)), tok]
    per_dir_shape = [jax.ShapeDtypeStruct((s, hd_all), F32),
                     jax.ShapeDtypeStruct((n, HEADS, HEAD_DIM, HEAD_DIM), BF16),
                     jax.ShapeDtypeStruct((n, N_GROUPS, GROWS, GROWS), BF16),
                     jax.ShapeDtypeStruct((s, hd_all), BF16)]
    outs = pl.pallas_call(
        body, name=name, grid=(n,), in_specs=in_specs, out_specs=out_specs, out_shape=per_dir_shape * 2,
        scratch_shapes=[pltpu.VMEM((2, HEADS, HEAD_DIM, HEAD_DIM), F32),
                        pltpu.VMEM((2, N_MASKS, GROWS, GROWS), F32)],
        compiler_params=_params(("arbitrary",)),
    )(*([q, k, v, scal, grow] * 2))
    return tuple((outs[i], outs[4 + i]) for i in range(4))


def _delta_bwd(q, k, v, scal, grow, states, tinv, vn, do, *, name, carry=None):
    s = q.shape[0]
    n = s // CHUNK
    hd_all = HEADS * HEAD_DIM

    grp_rows = [slice(t * CHUNK, (t + 1) * CHUNK) for t in range(GROUP)]
    per_head = lambda fn: _stack([fn(t) for t in range(GROUP)])

    def body(*refs):
        ins, outs, (dstate, mask_ref) = refs[:18], refs[18:28], refs[28:]

        @pl.when(pl.program_id(0) == 0)
        def _():
            dstate[...] = jnp.zeros_like(dstate)
            for d in range(2):
                _write_group_masks(mask_ref.at[d], 1 - 2 * d, M_EYE)

        chains = []
        for d in range(2):
            q_ref, k_ref, v_ref, sc_ref, gr_ref, st_ref, t_ref, vn_ref, do_ref = ins[9 * d:9 * d + 9]
            scv = sc_ref[0]
            for grp in range(N_GROUPS):
                c = types.SimpleNamespace(d=d, grp=grp, out=outs[5 * d:5 * d + 5], dstate=dstate.at[d],
                                          masks=mask_ref.at[d])
                (c.heads, qs, ks, c.vs, beta, gcol, eg, ekd, c.egc) = _group_operands(q_ref, k_ref, v_ref, scv, grp)
                c.cols = [slice(h * HEAD_DIM, (h + 1) * HEAD_DIM) for h in c.heads]
                c.dm = _group_decay(_stack(gcol), gr_ref[0, 0, grp:grp + 1, :], c.masks)
                c.dm_strict = c.dm * c.masks[M_STRICT]
                other = mask_ref.at[1 - d]
                c.dm_t = jnp.exp(jnp.minimum(gr_ref[0, 0, grp:grp + 1, :] - _stack(gcol), 0.0)) * other[M_INCL]
                c.dm_t_strict = c.dm_t * other[M_STRICT]
                c.beta, c.eg, c.ekd = _stack(beta), _stack(eg), _stack(ekd)
                c.q, c.k = _stack(qs), _stack(ks)
                c.q_b, c.k_b = c.q.astype(BF16), c.k.astype(BF16)
                c.kb_b = (c.k * c.beta).astype(BF16)
                c.kg, c.qg, c.kd = c.k * c.eg, c.q * c.eg, c.k * c.ekd
                c.kg_b, c.qg_b, c.kd_b = c.kg.astype(BF16), c.qg.astype(BF16), c.kd.astype(BF16)
                c.vn_b = _stack([vn_ref[:, cc] for cc in c.cols])
                c.do_b = _stack([do_ref[:, cc] for cc in c.cols]).astype(BF16)
                c.sh_b = [st_ref[0, h] for h in c.heads]
                c.dsp = [c.dstate[h] for h in c.heads]
                c.dsp_b = [x.astype(BF16) for x in c.dsp]
                c.t_b = t_ref[0, grp]
                chains.append(c)
        for c in chains:
            c.kk = _dot_nt(c.kb_b, c.k_b)
            c.qk = _dot_nt(c.q_b, c.k_b)
            c.pt_b = (_dot_nt(c.k_b, c.q_b) * c.dm_t).astype(BF16)
        for c in chains:
            c.r = per_head(lambda t: c.vs[t] - _dot(c.kg_b[grp_rows[t]], c.sh_b[t]))
            c.kd_ds = per_head(lambda t: _dot(c.kd_b[grp_rows[t]], c.dsp_b[t]))
        for c in chains:
            c.dvn_b = (_dot(c.pt_b, c.do_b) + c.kd_ds).astype(BF16)
        for c in chains:
            c.db = _dot_tn(c.t_b, c.dvn_b)
        for c in chains:
            c.dr = c.db * c.beta
            c.dbeta = jnp.sum(c.db * c.r, axis=-1, keepdims=True)
            c.dr_b, c.db_b = c.dr.astype(BF16), c.db.astype(BF16)
        for c in chains:
            c.dkg = -per_head(lambda t: _dot_nt(c.dr_b[grp_rows[t]], c.sh_b[t]))
            c.dqg = per_head(lambda t: _dot_nt(c.do_b[grp_rows[t]], c.sh_b[t]))
            c.dkd = per_head(lambda t: _dot_nt(c.vn_b[grp_rows[t]], c.dsp_b[t]))
        for c in chains:
            c.dpm = _dot_nt(c.do_b, c.vn_b) * c.dm
            c.dam = -_dot_nt(c.db_b, c.vn_b) * c.dm_strict
            c.dpm_b, c.dam_b = c.dpm.astype(BF16), c.dam.astype(BF16)
            c.dpm_t_b = (_dot_nt(c.vn_b, c.do_b) * c.dm_t).astype(BF16)
            c.dam_t_b = (-_dot_nt(c.vn_b, c.db_b) * c.dm_t_strict).astype(BF16)
        for c in chains:
            c.dkb = _dot(c.dam_b, c.k_b)
            c.dq = c.dqg * c.eg + _dot(c.dpm_b, c.k_b)
        for c in chains:
            c.dk = (c.dkg * c.eg + c.dkd * c.ekd + _dot(c.dpm_t_b, c.q_b) + _dot(c.dam_t_b, c.kb_b)
                    + c.dkb * c.beta)
        lane = lax.broadcasted_iota(jnp.int32, (CHUNK, BA_W), 1)
        row = lax.broadcasted_iota(jnp.int32, (CHUNK, 1), 0)
        dsc_acc = [jnp.zeros((CHUNK, BA_W), F32) for _ in range(2)]
        for c in chains:
            dq_ref, dk_ref, dv_ref, dsc_ref, dgr_ref = c.out
            end_row = CHUNK - 1 if c.d == 0 else 0
            dbeta = c.dbeta + jnp.sum(c.dkb * c.k, axis=-1, keepdims=True)
            m = c.dpm * c.qk + c.dam * c.kk
            kd_term = jnp.sum(c.dkd * c.kd, axis=-1, keepdims=True)
            dgcol = (jnp.sum(c.dqg * c.qg, axis=-1, keepdims=True) + jnp.sum(c.dkg * c.kg, axis=-1, keepdims=True)
                     - kd_term + jnp.sum(m, axis=-1, keepdims=True))
            dgr_ref[0, c.grp:c.grp + 1, :] = -jnp.sum(m, axis=0, keepdims=True)
            for t, h in enumerate(c.heads):
                rows, cols = grp_rows[t], c.cols[t]
                dq_ref[:, cols] = c.dq[rows]
                dk_ref[:, cols] = c.dk[rows]
                dv_ref[:, cols] = c.dr[rows]
                dg_end = jnp.sum(kd_term[rows]) + c.egc[t] * jnp.sum(c.dsp[t] * c.sh_b[t].astype(F32))
                dgcol_h = dgcol[rows] + jnp.where(row == end_row, dg_end, 0.0)
                dsc_acc[c.d] = jnp.where(lane == L_BETA + h, dbeta[rows], dsc_acc[c.d])
                dsc_acc[c.d] = jnp.where(lane == L_G + h, dgcol_h, dsc_acc[c.d])
                c.dstate[h] = (_dot_tn(c.qg_b[rows], c.do_b[rows]) + c.egc[t] * c.dsp[t]
                               - _dot_tn(c.kg_b[rows], c.dr_b[rows]))
        for d in range(2):
            outs[5 * d + 3][...] = dsc_acc[d]

    at = [lambda c: n - 1 - c, lambda c: c]
    in_specs, out_specs, args = [], [], []
    for d in range(2):
        tok = pl.BlockSpec((CHUNK, hd_all), lambda c, d=d: (at[d](c), 0))
        in_specs += [tok] * 3 + [pl.BlockSpec((1, CHUNK, BA_W), lambda c, d=d: (d, at[d](c), 0)),
                                 pl.BlockSpec((1, 1, N_GROUPS, GROWS), lambda c, d=d: (d, at[d](c), 0, 0)),
                                 pl.BlockSpec((1, HEADS, HEAD_DIM, HEAD_DIM), lambda c, d=d: (at[d](c), 0, 0, 0)),
                                 pl.BlockSpec((1, N_GROUPS, GROWS, GROWS), lambda c, d=d: (at[d](c), 0, 0, 0)),
                                 tok, tok]
        args += [q, k, v, scal, grow, states[d], tinv[d], vn[d], do]
        out_specs += [tok] * 3 + [pl.BlockSpec((CHUNK, BA_W), lambda c, d=d: (at[d](c), 0)),
                                  pl.BlockSpec((1, N_GROUPS, GROWS), lambda c, d=d: (at[d](c), 0, 0))]
    per_dir_shape = ([jax.ShapeDtypeStruct((s, hd_all), F32)] * 3
                     + [jax.ShapeDtypeStruct((s, BA_W), F32), jax.ShapeDtypeStruct((n, N_GROUPS, GROWS), F32)])
    res = _launch(
        body, carry, tuple(args), name=name, grid=(n,), in_specs=in_specs, out_specs=out_specs,
        out_shape=per_dir_shape * 2,
        scratch_shapes=[pltpu.VMEM((2, HEADS, HEAD_DIM, HEAD_DIM), F32), pltpu.VMEM((2, M_EYE, GROWS, GROWS), F32)],
        sem=("arbitrary",))
    outs, got = res if carry is not None else (res, None)
    paired = tuple((outs[i], outs[5 + i]) for i in range(5))
    return paired if carry is None else (paired, got)


def _gate_norm_fwd(o2, proj, dnw, *, name, ts=512):
    s = o2[0].shape[0]
    hd_all = HEADS * HEAD_DIM

    def body(of_ref, ob_ref, z_ref, w_ref, y_ref):
        w = w_ref[...]
        for hd in range(HEADS):
            cols = slice(hd * HEAD_DIM, (hd + 1) * HEAD_DIM)
            seg = of_ref[:, cols] + ob_ref[:, cols]
            r = lax.rsqrt(jnp.mean(seg * seg, axis=-1, keepdims=True) + EPS)
            z = z_ref[:, cols].astype(F32)
            y_ref[:, cols] = ((seg * r * w) * (z * _sigmoid(z))).astype(BF16)

    tile = pl.BlockSpec((ts, hd_all), lambda i: (i, 0))
    return pl.pallas_call(
        body, name=name, grid=(s // ts,),
        in_specs=[tile, tile, pl.BlockSpec((ts, Z_W), lambda i: (i, Z_OFF // Z_W)), _row(HEAD_DIM)],
        out_specs=tile,
        out_shape=jax.ShapeDtypeStruct((s, hd_all), BF16),
        compiler_params=_params(("parallel",)),
    )(o2[0], o2[1], proj, dnw)


def _gate_norm_bwd(dyb, o2, proj, dnw, *, name, ts=512):
    s = o2[0].shape[0]
    hd_all = HEADS * HEAD_DIM

    def body(dy_ref, of_ref, ob_ref, z_ref, w_ref, do_ref, dz_ref, dw_ref):
        w = w_ref[...]
        dw = jnp.zeros((1, HEAD_DIM), F32)
        for hd in range(HEADS):
            cols = slice(hd * HEAD_DIM, (hd + 1) * HEAD_DIM)
            seg = of_ref[:, cols] + ob_ref[:, cols]
            r = lax.rsqrt(jnp.mean(seg * seg, axis=-1, keepdims=True) + EPS)
            xhat = seg * r
            z = z_ref[:, cols].astype(F32)
            sg = _sigmoid(z)
            dy = dy_ref[:, cols]
            dnrm = dy * (z * sg)
            dz_ref[:, cols] = (dy * (xhat * w) * (sg * (1.0 + z * (1.0 - sg)))).astype(BF16)
            dw = dw + jnp.sum(dnrm * xhat, axis=0, keepdims=True)
            dxhat = dnrm * w
            do_ref[:, cols] = r * (dxhat - xhat * jnp.mean(dxhat * xhat, axis=-1, keepdims=True))

        @pl.when(pl.program_id(0) == 0)
        def _():
            dw_ref[...] = jnp.zeros_like(dw_ref)

        dw_ref[...] += dw

    tile = pl.BlockSpec((ts, hd_all), lambda i: (i, 0))
    return pl.pallas_call(
        body, name=name, grid=(s // ts,),
        in_specs=[tile, tile, tile, pl.BlockSpec((ts, Z_W), lambda i: (i, Z_OFF // Z_W)), _row(HEAD_DIM)],
        out_specs=[tile, tile, _row(HEAD_DIM)],
        out_shape=[jax.ShapeDtypeStruct((s, hd_all), F32), jax.ShapeDtypeStruct((s, hd_all), BF16),
                   jax.ShapeDtypeStruct((1, HEAD_DIM), F32)],
        compiler_params=_params(("arbitrary",)),
    )(dyb, o2[0], o2[1], proj, dnw)


def _merge_fwd(ya, yb, proj, wa, wb, wo, h, g, *, name, ts=512):
    s, d = h.shape

    def body(ya_ref, yb_ref, gt_ref, wa_ref, wb_ref, wo_ref, h_ref, g_ref, pa_ref, pb_ref, mix_ref, ho_ref):
        pa = _dot(ya_ref[...], wa_ref[...])
        pb = _dot(yb_ref[...], wb_ref[...])
        pa_ref[...] = pa.astype(BF16)
        pb_ref[...] = pb.astype(BF16)
        merged = (_sigmoid(gt_ref[:, :d].astype(F32)) * pa + _sigmoid(gt_ref[:, d:].astype(F32)) * pb)
        mix = _dot(merged.astype(BF16), wo_ref[...])
        mix_ref[...] = mix.astype(BF16)
        ho_ref[...] = h_ref[...] + g_ref[...] * mix

    tile = pl.BlockSpec((ts, d), lambda i: (i, 0))
    return pl.pallas_call(
        body, name=name, grid=(s // ts,),
        in_specs=[pl.BlockSpec((ts, CONV_A), lambda i: (i, 0)), tile,
                  pl.BlockSpec((ts, GATE_W), lambda i: (i, GATE_OFF // GATE_W)),
                  _resident(wa.shape), _resident(wb.shape), _resident(wo.shape), tile, _row(d)],
        out_specs=[tile, tile, tile, tile],
        out_shape=[jax.ShapeDtypeStruct((s, d), BF16)] * 3 + [jax.ShapeDtypeStruct((s, d), F32)],
        compiler_params=_params(("parallel",)),
    )(ya, yb, proj, wa, wb, wo, h, g)


def _merge_bwd(dh, g, mix, pa, pb, proj, wa, wb, wo, *, name, ts=256):
    s, d = dh.shape

    def body(dh_ref, g_ref, mix_ref, pa_ref, pb_ref, gt_ref, wa_ref, wb_ref, wo_ref,
             dmix_ref, mg_ref, dpa_ref, dpb_ref, dgt_ref, dya_ref, dyb_ref, dg_ref):
        dh_v = dh_ref[...]
        dmix = (g_ref[...] * dh_v).astype(BF16)
        dmix_ref[...] = dmix

        @pl.when(pl.program_id(0) == 0)
        def _():
            dg_ref[...] = jnp.zeros_like(dg_ref)

        dg_ref[...] += jnp.sum(dh_v * mix_ref[...].astype(F32), axis=0, keepdims=True)
        dmerged = _dot_nt(dmix, wo_ref[...])
        pa = pa_ref[...].astype(F32)
        pb = pb_ref[...].astype(F32)
        sa = _sigmoid(gt_ref[:, :d].astype(F32))
        sb = _sigmoid(gt_ref[:, d:].astype(F32))
        mg_ref[...] = (sa * pa + sb * pb).astype(BF16)
        dpa = (dmerged * sa).astype(BF16)
        dpb = (dmerged * sb).astype(BF16)
        dpa_ref[...] = dpa
        dpb_ref[...] = dpb
        dgt_ref[:, :d] = (dmerged * pa * sa * (1.0 - sa)).astype(BF16)
        dgt_ref[:, d:] = (dmerged * pb * sb * (1.0 - sb)).astype(BF16)
        dya_ref[...] = _dot_nt(dpa, wa_ref[...])
        dyb_ref[...] = _dot_nt(dpb, wb_ref[...])

    tile = pl.BlockSpec((ts, d), lambda i: (i, 0))
    return pl.pallas_call(
        body, name=name, grid=(s // ts,),
        in_specs=[tile, _row(d), tile, tile, tile,
                  pl.BlockSpec((ts, GATE_W), lambda i: (i, GATE_OFF // GATE_W)),
                  _resident(wa.shape), _resident(wb.shape), _resident(wo.shape)],
        out_specs=[tile, tile, tile, tile, pl.BlockSpec((ts, GATE_W), lambda i: (i, 0)),
                   pl.BlockSpec((ts, CONV_A), lambda i: (i, 0)), tile, _row(d)],
        out_shape=[jax.ShapeDtypeStruct((s, d), BF16)] * 4
                  + [jax.ShapeDtypeStruct((s, GATE_W), BF16), jax.ShapeDtypeStruct((s, CONV_A), F32),
                     jax.ShapeDtypeStruct((s, d), F32), jax.ShapeDtypeStruct((1, d), F32)],
        compiler_params=_params(("arbitrary",)),
    )(dh, g, mix, pa, pb, proj, wa, wb, wo)


def _final_fwd_bwd(h, nw, target, *, name, ts=512):
    s, d = h.shape

    def body(h_ref, nw_ref, t_ref, loss_ref, dh_ref, dnw_ref):
        x = h_ref[...]
        w = nw_ref[...]
        r = lax.rsqrt(jnp.mean(x * x, axis=-1, keepdims=True) + EPS)
        xhat = x * r
        e = xhat * w - t_ref[...]
        part = 0.5 * jnp.sum(jnp.mean(e * e, axis=-1, keepdims=True))
        dy = e * (1.0 / d)
        dxhat = dy * w
        dh_ref[...] = r * (dxhat - xhat * jnp.mean(dxhat * xhat, axis=-1, keepdims=True))

        @pl.when(pl.program_id(0) == 0)
        def _():
            loss_ref[...] = jnp.zeros_like(loss_ref)
            dnw_ref[...] = jnp.zeros_like(dnw_ref)

        loss_ref[...] += jnp.broadcast_to(part, loss_ref.shape)
        dnw_ref[...] += jnp.sum(dy * xhat, axis=0, keepdims=True)

    tile = pl.BlockSpec((ts, d), lambda i: (i, 0))
    return pl.pallas_call(
        body, name=name, grid=(s // ts,),
        in_specs=[tile, _row(d), tile],
        out_specs=[_row(128), tile, _row(d)],
        out_shape=[jax.ShapeDtypeStruct((1, 128), F32), jax.ShapeDtypeStruct((s, d), F32),
                   jax.ShapeDtypeStruct((1, d), F32)],
        compiler_params=_params(("arbitrary",)),
    )(h, nw, target)


EXT = 8


def _l2norm_heads_bwd(act, dout, scale):
    outs = []
    for hd in range(HEADS):
        cols = slice(hd * HEAD_DIM, (hd + 1) * HEAD_DIM)
        seg = act[:, cols]
        nrm = lax.rsqrt(jnp.sum(seg * seg, axis=-1, keepdims=True) + EPS)
        yhat = seg * nrm
        dsg = dout[:, cols]
        outs.append((scale * nrm) * (dsg - yhat * jnp.sum(yhat * dsg, axis=-1, keepdims=True)))
    return jnp.concatenate(outs, axis=-1)


def _conv_bwd(dq2, dk2, dv2, dya, proj, conv_dn, conv_a, *, name, ts=256, carry=None):
    s = proj.shape[0]
    hd = HEADS * HEAD_DIM
    nt = s // ts
    te = ts + 2 * EXT
    kdn, ka = conv_dn.shape[0], conv_a.shape[0]

    def body(*refs):
        (qp_ref, qc_ref, qn_ref, ap_ref, ac_ref, an_ref) = refs[0:6]
        d3 = refs[6:24]
        (yp_ref, yc_ref, yn_ref, wdn_ref, wa_ref) = refs[24:29]
        (dqkv_ref, da_ref, dwdn_ref, dwa_ref) = refs[29:33]
        xs_ref, dps_ref, xa_ref, dca_ref = refs[33:37]
        i = pl.program_id(0)
        first, last = i == 0, i == nt - 1

        @pl.when(first)
        def _():
            dwdn_ref[...] = jnp.zeros_like(dwdn_ref)
            dwa_ref[...] = jnp.zeros_like(dwa_ref)

        rowe = lax.broadcasted_iota(jnp.int32, (te, 1), 0)
        inside = ~((first & (rowe < EXT)) | (last & (rowe >= EXT + ts)))
        _fill_halo(xs_ref, qp_ref, qc_ref, qn_ref, first, last)
        wdn = wdn_ref[...]
        for part in range(3):
            cols = slice(part * hd, (part + 1) * hd)
            pre = _dwconv_rows(xs_ref, wdn[:, cols], HALO - EXT - 2, te, cols)
            sg = _sigmoid(pre)
            act = pre * sg
            pf, cf, nf, pb, cb, nb = d3[6 * part:6 * part + 6]
            dout = jnp.concatenate([pf[...] + pb[...], cf[...] + cb[...], nf[...] + nb[...]], axis=0)
            if part == 0:
                dact = _l2norm_heads_bwd(act, dout, Q_SCALE)
            elif part == 1:
                dact = _l2norm_heads_bwd(act, dout, 1.0)
            else:
                dact = dout
            dpre = jnp.where(inside, dact * (sg * (1.0 + pre * (1.0 - sg))), 0.0)
            dps_ref[:, cols] = dpre
            acc = wdn[0:1, cols] * dps_ref[EXT + 2:EXT + 2 + ts, cols]
            for tap in range(1, kdn):
                acc = acc + wdn[tap:tap + 1, cols] * dps_ref[EXT + 2 - tap:EXT + 2 - tap + ts, cols]
            dqkv_ref[:, cols] = acc.astype(BF16)
            dcur = dps_ref[EXT:EXT + ts, cols]
            for tap in range(kdn):
                dwdn_ref[tap:tap + 1, cols] += jnp.sum(
                    dcur * xs_ref[HALO - 2 + tap:HALO - 2 + tap + ts, cols], axis=0, keepdims=True)

        cv = lambda r: r[:, CONV_A:2 * CONV_A].astype(F32) * r[:, 2 * CONV_A:].astype(F32)
        _fill_halo(xa_ref, ap_ref, ac_ref, an_ref, first, last, fn=cv)
        wa = wa_ref[...]
        gate_b = jnp.concatenate([ap_ref[HALO - EXT:, 0:CONV_A], ac_ref[:, 0:CONV_A], an_ref[0:EXT, 0:CONV_A]],
                                 axis=0).astype(F32)
        dya_e = jnp.concatenate([yp_ref[...], yc_ref[...], yn_ref[...]], axis=0)
        dca_ref[...] = jnp.where(inside, dya_e * gate_b, 0.0)
        conv = _dwconv_rows(xa_ref, wa, HALO - 1, ts, slice(0, CONV_A))
        acc = wa[0:1, :] * dca_ref[EXT + 1:EXT + 1 + ts, :]
        for tap in range(1, ka):
            acc = acc + wa[tap:tap + 1, :] * dca_ref[EXT + 1 - tap:EXT + 1 - tap + ts, :]
        gc = ac_ref[:, CONV_A:2 * CONV_A].astype(F32)
        val = ac_ref[:, 2 * CONV_A:].astype(F32)
        da_ref[:, 0:CONV_A] = (yc_ref[...] * conv).astype(BF16)
        da_ref[:, CONV_A:2 * CONV_A] = (acc * val).astype(BF16)
        da_ref[:, 2 * CONV_A:] = (acc * gc).astype(BF16)
        dcur = dca_ref[EXT:EXT + ts, :]
        for tap in range(ka):
            dwa_ref[tap:tap + 1, :] += jnp.sum(
                dcur * xa_ref[HALO - 1 + tap:HALO - 1 + tap + ts, :], axis=0, keepdims=True)

    cot = [arr for pair in (dq2, dk2, dv2) for arr in pair for _ in range(3)]
    return _launch(
        body, carry, (proj, proj, proj, proj, proj, proj, *cot, dya, dya, dya, conv_dn, conv_a),
        name=name, grid=(nt,),
        in_specs=(_halo_specs(ts, QKV_W, 0, s) + _halo_specs(ts, A_W, A_OFF // A_W, s)
                  + _halo_specs(ts, hd, 0, s, rows=EXT) * 6 + _halo_specs(ts, CONV_A, 0, s, rows=EXT)
                  + [_resident(conv_dn.shape), _resident(conv_a.shape)]),
        out_specs=[pl.BlockSpec((ts, QKV_W), lambda i: (i, 0)), pl.BlockSpec((ts, A_W), lambda i: (i, 0)),
                   pl.BlockSpec((8, QKV_W), lambda i: (0, 0)), pl.BlockSpec((8, CONV_A), lambda i: (0, 0))],
        out_shape=[jax.ShapeDtypeStruct((s, QKV_W), BF16), jax.ShapeDtypeStruct((s, A_W), BF16),
                   jax.ShapeDtypeStruct((8, QKV_W), F32), jax.ShapeDtypeStruct((8, CONV_A), F32)],
        scratch_shapes=[pltpu.VMEM((ts + 2 * HALO, QKV_W), F32), pltpu.VMEM((te, QKV_W), F32),
                        pltpu.VMEM((ts + 2 * HALO, CONV_A), F32), pltpu.VMEM((te, CONV_A), F32)],
        sem=("arbitrary",))


IN_A = (0, 1536)
IN_QKV = (1536, 4608)
IN_Z = (4608, 5632)
IN_BA = 5632
IN_GATE = (5664, 7712)
IN_COLS = 7712
G_REPL = 4


def _split_w_in(w_in):
    sl = lambda ab: w_in[:, ab[0]:ab[1]]
    w_main = jnp.concatenate([sl(IN_QKV), sl(IN_Z), sl(IN_GATE), sl(IN_A)], axis=1)
    blocks = []
    for d in range(2):
        beta = w_in[:, IN_BA + 8 * d:IN_BA + 8 * d + 8]
        alpha = w_in[:, IN_BA + 16 + 8 * d:IN_BA + 24 + 8 * d]
        pad = jnp.zeros((w_in.shape[0], BA_W - 8 - 8 * G_REPL), w_in.dtype)
        blocks += [beta] + [alpha] * G_REPL + [pad]
    return w_main, jnp.concatenate(blocks, axis=1)


def _merge_dw_in(dw_qkv, dw_z, dw_gate, dw_a, dw_ba):
    ba = [dw_ba[:, 0:8], dw_ba[:, BA_W:BA_W + 8], dw_ba[:, 8:16], dw_ba[:, BA_W + 8:BA_W + 16]]
    return jnp.concatenate([dw_a, dw_qkv, dw_z] + ba + [dw_gate], axis=1)


def _decay_rows(a_log_fwd, dt_bias_fwd, a_log_bwd, dt_bias_bwd):
    def rows(f, b):
        out = []
        for vec in (f, b):
            vec = vec.reshape(HEADS)
            out.append(jnp.concatenate([jnp.zeros((8,), F32)] + [vec] * G_REPL
                                       + [jnp.zeros((BA_W - 8 - 8 * G_REPL,), F32)])[None])
        return jnp.stack(out)
    return rows(a_log_fwd, a_log_bwd), rows(dt_bias_fwd, dt_bias_bwd)


def _local_step(x, target, mod9, wt, comm):
    s, d = x.shape
    n = s // CHUNK
    wt = dict(wt)
    sh1, sc1, g1, sh2, sc2, g2, sh3, sc3, g3 = [mod9[i:i + 1] for i in range(9)]
    alog, dtb = _decay_rows(wt["a_log_fwd"], wt["dt_bias_fwd"], wt["a_log_bwd"], wt["dt_bias_bwd"])

    (u1, a1, b1, f1), got = comm.gather(
        ["w_ffn1_down", "w_in"],
        lambda c: _ffn_up_fwd(x, wt["norm_ffn1"], sc1, sh1, wt["w_ffn1_up"], name="ffn1_up", carry=c))
    wt.update(got)
    w_main, w_ba = _split_w_in(wt["w_in"])
    y1, h1 = _ffn_down_fwd(f1, wt["w_ffn1_down"], x, g1, name="ffn1_down")
    (u2, proj, ba), got = comm.gather(
        ["w_a_out", "w_b_out", "w_out", "w_ffn2_up", "w_ffn2_down"],
        lambda c: _in_proj_fwd(h1, wt["norm_mix"], sc2, sh2, w_main, w_ba, name="in_proj", carry=c))
    wt.update(got)
    scal = _scal_fwd(ba, alog, dtb, name="scal_fwd")
    grow = scal[:, :, L_G:L_G + 8].reshape(2, n, CHUNK, HEADS).transpose(0, 1, 3, 2).reshape(
        2, n, N_GROUPS, GROWS)
    q, k, v, ya = _conv_fwd(proj, wt["conv_dn"], wt["conv_a"], name="conv_fwd")
    o2, states, tinv, vn = _delta_fwd(q, k, v, scal, grow, name="delta_fwd")
    yb = _gate_norm_fwd(o2, proj, wt["dn_norm"], name="gate_norm_fwd")
    pa, pb, mix, h2 = _merge_fwd(ya, yb, proj, wt["w_a_out"], wt["w_b_out"], wt["w_out"], h1, g2,
                                 name="merge_fwd")
    u3, a3, b3, f3 = _ffn_up_fwd(h2, wt["norm_ffn2"], sc3, sh3, wt["w_ffn2_up"], name="ffn2_up")
    y3, h3 = _ffn_down_fwd(f3, wt["w_ffn2_down"], h2, g3, name="ffn2_down")
    loss, dh3, dnorm_final = _final_fwd_bwd(h3, wt["norm_final"], target, name="final")

    dy3, dab3, dg3 = _ffn_bwd_act(dh3, g3, y3, a3, b3, wt["w_ffn2_down"], name="ffn2_bwd_act")
    dh2, dn3, dsc3, dsh3 = _norm_mod_matmul_bwd([(dab3, wt["w_ffn2_up"])], h2, wt["norm_ffn2"], sc3, dh3,
                                                name="ffn2_bwd_up")
    gw = {}
    gw["w_ffn2_up"] = _matmul_tn(u3, dab3, name="dw_ffn2_up", tm=1024, tn=1408)
    gw["w_ffn2_down"] = _matmul_tn(f3, dy3, name="dw_ffn2_down", tm=1408, tn=1024)

    dmix, merged, dpa, dpb, dgates, dya, dyb, dg2 = _merge_bwd(
        dh2, g2, mix, pa, pb, proj, wt["w_a_out"], wt["w_b_out"], wt["w_out"], name="merge_bwd")
    gw["w_out"] = _matmul_tn(merged, dmix, name="dw_out", tm=1024, tn=1024)
    gw["w_a_out"] = _matmul_tn(ya, dpa, name="dw_a_out", tm=512, tn=1024)
    gw["w_b_out"] = _matmul_tn(yb, dpb, name="dw_b_out", tm=1024, tn=1024)
    do, dz, ddn = _gate_norm_bwd(dyb, o2, proj, wt["dn_norm"], name="gate_norm_bwd")
    recv = {}
    (dq2, dk2, dv2, dscal, drow), got = comm.scatter(
        {nm: gw.pop(nm) for nm in ("w_ffn2_up", "w_ffn2_down")},
        lambda c: _delta_bwd(q, k, v, scal, grow, states, tinv, vn, do, name="delta_bwd", carry=c))
    recv.update(got)
    drow_p = jnp.pad(jnp.stack(drow).reshape(2, n, HEADS, CHUNK).transpose(0, 1, 3, 2).reshape(2, s, HEADS),
                     ((0, 0), (0, 0), (L_G, BA_W - L_G - HEADS)))
    dba, dalog, ddtb = _scal_bwd(jnp.stack(dscal), drow_p, ba, alog, dtb, name="scal_bwd")
    (dqkv, dbr_a, dconv_dn, dconv_a), got = comm.scatter(
        {nm: gw.pop(nm) for nm in ("w_out", "w_a_out", "w_b_out")},
        lambda c: _conv_bwd(dq2, dk2, dv2, dya, proj, wt["conv_dn"], wt["conv_a"], name="conv_bwd", carry=c))
    recv.update(got)
    dw_in = _merge_dw_in(
        _matmul_tn(u2, dqkv, name="dw_in_qkv", tm=1024, tn=1536),
        _matmul_tn(u2, dz, name="dw_in_z", tm=1024, tn=1024),
        _matmul_tn(u2, dgates, name="dw_in_gate", tm=1024, tn=1024),
        _matmul_tn(u2, dbr_a, name="dw_in_a", tm=1024, tn=1536),
        _matmul_tn(u2, dba, name="dw_in_ba", tm=1024, tn=2 * BA_W))
    (dh1, dn2, dsc2, dsh2), got = comm.scatter(
        {"w_in": dw_in},
        lambda c: _norm_mod_matmul_bwd(
            [(dqkv, w_main, 0), (dz, w_main, Z_OFF // Z_W), (dgates, w_main, GATE_OFF // GATE_W),
             (dbr_a, w_main, A_OFF // A_W), (dba, w_ba)],
            h1, wt["norm_mix"], sc2, dh2, name="in_proj_bwd", carry=c))
    recv.update(got)

    dy1, dab1, dg1 = _ffn_bwd_act(dh1, g1, y1, a1, b1, wt["w_ffn1_down"], name="ffn1_bwd_act")
    dw_down1 = _matmul_tn(f1, dy1, name="dw_ffn1_down", tm=1408, tn=1024)
    dw_up1, got = comm.scatter(
        {"w_ffn1_down": dw_down1},
        lambda c: _matmul_tn(u1, dab1, name="dw_ffn1_up", tm=1024, tn=1408, carry=c))
    recv.update(got)
    (dx, dn1, dsc1, dsh1), got = comm.scatter(
        {"w_ffn1_up": dw_up1},
        lambda c: _norm_mod_matmul_bwd([(dab1, wt["w_ffn1_up"])], x, wt["norm_ffn1"], sc1, dh1,
                                       name="ffn1_bwd_up", carry=c))
    recv.update(got)

    small = {
        "mod": jnp.concatenate([dsh1, dsc1, dg1, dsh2, dsc2, dg2, dsh3, dsc3, dg3], axis=1),
        "norm_ffn1": dn1, "norm_mix": dn2, "norm_ffn2": dn3, "norm_final": dnorm_final,
        "a_log_fwd": dalog[0, :, L_G:L_G + 8], "dt_bias_fwd": ddtb[0, :, L_G:L_G + 8],
        "a_log_bwd": dalog[1, :, L_G:L_G + 8], "dt_bias_bwd": ddtb[1, :, L_G:L_G + 8],
        "dn_norm": ddn,
        "conv_a": dconv_a[0:3].reshape(1, -1), "conv_dn": dconv_dn[0:5].reshape(1, -1),
    }
    return loss, dx, recv, small


def _full_weight(name, g):
    if name in COL_SHARDED + CONV_SHARDED:
        return g.transpose(1, 0, 2).reshape(g.shape[1], -1)
    return g.reshape(-1, g.shape[-1])


def _grad_pieces(name, g):
    g = g.astype(BF16)
    if name in COL_SHARDED:
        return g.reshape(g.shape[0], N_DEV, -1).transpose(1, 0, 2)
    return g.reshape(N_DEV, -1, g.shape[-1])


class _MeshComm:
    def __init__(self, shards):
        self.shards = shards

    def _run(self, xs, carrier, name, gather):
        if carrier is None:
            return None, _exchange(xs, name=name, gather=gather)
        return carrier((xs, gather))

    def gather(self, names, carrier=None, name=None):
        outs, got = self._run([self.shards[nm] for nm in names], carrier, name, True)
        return outs, {nm: _full_weight(nm, g) for nm, g in zip(names, got)}

    def scatter(self, grads, carrier=None, name=None):
        names = list(grads)
        outs, got = self._run([_grad_pieces(nm, grads[nm]) for nm in names], carrier, name, False)
        return outs, dict(zip(names, got))


def _mod_fwd(c_all, w_ada, *, name):
    def body(c_ref, w_ref, o_ref):
        cv = c_ref[...]
        o_ref[...] = _dot3(cv * _sigmoid(cv), w_ref[...])

    return pl.pallas_call(
        body, name=name, out_shape=jax.ShapeDtypeStruct((c_all.shape[0], w_ada.shape[1]), F32),
        compiler_params=_params(),
    )(c_all, w_ada)


def _adamw_math(w, g, m, v):
    m_new = ADAM_B1 * m + (1.0 - ADAM_B1) * g
    v_new = ADAM_B2 * v + (1.0 - ADAM_B2) * (g * g)
    m_hat = m_new / (1.0 - ADAM_B1 ** ADAM_STEP)
    v_hat = v_new / (1.0 - ADAM_B2 ** ADAM_STEP)
    delta = -ADAM_LR * (m_hat / (jnp.sqrt(v_hat) + ADAM_EPS) + ADAM_WD * w)
    return delta, m_new, v_new


def _reduce_adamw(pieces, w, m, v, *, name, tr):
    r, c = w.shape

    def body(p_ref, w_ref, m_ref, v_ref, g_ref, d_ref, mo_ref, vo_ref):
        g = p_ref[0].astype(F32)
        for src in range(1, N_DEV):
            g = g + p_ref[src].astype(F32)
        g_ref[...] = g
        d_ref[...], mo_ref[...], vo_ref[...] = _adamw_math(w_ref[...], g, m_ref[...], v_ref[...])

    tile = pl.BlockSpec((tr, c), lambda i: (i, 0))
    return pl.pallas_call(
        body, name=name, grid=(r // tr,),
        in_specs=[pl.BlockSpec((N_DEV, tr, c), lambda i: (0, i, 0)), tile, tile, tile],
        out_specs=[tile] * 4, out_shape=[jax.ShapeDtypeStruct((r, c), F32)] * 4,
        compiler_params=_params(("parallel",)),
    )(pieces, w, m, v)


def _ada_grad_adamw(c_all_t, dmod_cols, w, m, v, *, name, tr=256):
    r, c = w.shape

    def body(c_ref, dm_ref, w_ref, m_ref, v_ref, g_ref, d_ref, mo_ref, vo_ref):
        cv = c_ref[...]
        act = cv * _sigmoid(cv)
        dm = dm_ref[...]
        g = act[:, 0:1] * dm[0:1, :]
        for b in range(1, N_DEV):
            g = g + act[:, b:b + 1] * dm[b:b + 1, :]
        g_ref[...] = g
        d_ref[...], mo_ref[...], vo_ref[...] = _adamw_math(w_ref[...], g, m_ref[...], v_ref[...])

    tile = pl.BlockSpec((tr, c), lambda i: (i, 0))
    return pl.pallas_call(
        body, name=name, grid=(r // tr,),
        in_specs=[pl.BlockSpec((tr, N_DEV), lambda i: (i, 0)), pl.BlockSpec((N_DEV, c), lambda i: (0, 0)),
                  tile, tile, tile],
        out_specs=[tile] * 4, out_shape=[jax.ShapeDtypeStruct((r, c), F32)] * 4,
        compiler_params=_params(("parallel",)),
    )(c_all_t, dmod_cols, w, m, v)


def _sum_rows(parts, *, name):
    def body(p_ref, o_ref):
        acc = p_ref[0:1, :]
        for src in range(1, N_DEV):
            acc = acc + p_ref[src:src + 1, :]
        o_ref[...] = acc

    return pl.pallas_call(
        body, name=name, out_shape=jax.ShapeDtypeStruct((1, parts.shape[1]), F32), compiler_params=_params(),
    )(parts)


def _adamw_rows(g, w, m, v, *, name):
    def body(g_ref, w_ref, m_ref, v_ref, d_ref, mo_ref, vo_ref):
        d_ref[...], mo_ref[...], vo_ref[...] = _adamw_math(w_ref[...], g_ref[...], m_ref[...], v_ref[...])

    return pl.pallas_call(
        body, name=name, out_shape=[jax.ShapeDtypeStruct(g.shape, F32)] * 3, compiler_params=_params(),
    )(g, w, m, v)


WEIGHTS = ["w_ada", "b_ada", "norm_ffn1", "w_ffn1_up", "w_ffn1_down", "norm_mix", "w_in", "conv_a", "conv_dn",
           "a_log_fwd", "dt_bias_fwd", "a_log_bwd", "dt_bias_bwd", "dn_norm", "w_a_out", "w_b_out", "w_out",
           "norm_ffn2", "w_ffn2_up", "w_ffn2_down", "norm_final"]
COL_SHARDED = ["w_ffn1_up", "w_in", "w_a_out", "w_ffn2_up"]
ROW_SHARDED = ["w_ffn1_down", "w_b_out", "w_out", "w_ffn2_down"]
CONV_SHARDED = ["conv_a", "conv_dn"]
REPLICATED = ["b_ada", "norm_ffn1", "norm_mix", "a_log_fwd", "dt_bias_fwd", "a_log_bwd", "dt_bias_bwd",
              "dn_norm", "norm_ffn2", "norm_final"]
SMALL_ORDER = ["mod", "norm_ffn1", "norm_mix", "norm_ffn2", "norm_final", "a_log_fwd", "dt_bias_fwd",
               "a_log_bwd", "dt_bias_bwd", "dn_norm", "conv_a", "conv_dn"]
REDUCE_ROWS = {"w_ffn1_up": 256, "w_in": 256, "w_a_out": 256, "w_ffn2_up": 256,
               "w_ffn1_down": 176, "w_b_out": 128, "w_out": 128, "w_ffn2_down": 176}


def _pad_lanes(row):
    pad = (-row.shape[1]) % 128
    return jnp.pad(row, ((0, 0), (0, pad)))


def _unstack_cols(g):
    return g.transpose(1, 0, 2).reshape(g.shape[1], -1)


def _stack_cols(w):
    k = w.shape[0]
    return w.reshape(k, N_DEV, -1).transpose(1, 0, 2)


def kernel(x, c, w_ada, b_ada, norm_ffn1, w_ffn1_up, w_ffn1_down, norm_mix, w_in, conv_a, conv_dn, a_log_fwd, dt_bias_fwd, a_log_bwd, dt_bias_bwd, dn_norm, w_a_out, w_b_out, w_out, norm_ffn2, w_ffn2_up, w_ffn2_down, norm_final, loss_target, m_w_ada, m_b_ada, m_norm_ffn1, m_w_ffn1_up, m_w_ffn1_down, m_norm_mix, m_w_in, m_conv_a, m_conv_dn, m_a_log_fwd, m_dt_bias_fwd, m_a_log_bwd, m_dt_bias_bwd, m_dn_norm, m_w_a_out, m_w_b_out, m_w_out, m_norm_ffn2, m_w_ffn2_up, m_w_ffn2_down, m_norm_final, v_w_ada, v_b_ada, v_norm_ffn1, v_w_ffn1_up, v_w_ffn1_down, v_norm_mix, v_w_in, v_conv_a, v_conv_dn, v_a_log_fwd, v_dt_bias_fwd, v_a_log_bwd, v_dt_bias_bwd, v_dn_norm, v_w_a_out, v_w_b_out, v_w_out, v_norm_ffn2, v_w_ffn2_up, v_w_ffn2_down, v_norm_final):
    args = dict(locals())
    w_loc = {n: args[n] for n in WEIGHTS}
    m_loc = {n: args["m_" + n] for n in WEIGHTS}
    v_loc = {n: args["v_" + n] for n in WEIGHTS}
    me = _flat_index(_my_position())
    d_model = x.shape[-1]

    big = COL_SHARDED + ROW_SHARDED
    shards = {n: w_loc[n][0].astype(BF16) for n in big}
    shards.update({n: w_loc[n][0] for n in CONV_SHARDED})
    shards["c"] = c
    comm = _MeshComm(shards)
    wt = comm.gather(["c", "conv_a", "conv_dn", "w_ffn1_up"], name="gather_first")[1]
    c_all = wt.pop("c")
    for n in REPLICATED[1:]:
        wt[n] = w_loc[n].reshape(1, -1)

    mod_cols = _mod_fwd(c_all, w_ada[0], name="mod_fwd")
    mod_all = _exchange([mod_cols], name="gather_mod", gather=True)[0]
    mod_mine = lax.dynamic_index_in_dim(mod_all, me, axis=1, keepdims=False).reshape(1, -1) + b_ada
    mod9 = mod_mine.reshape(9, d_model)

    loss_loc, dx, recv, small = _local_step(x[0], loss_target[0], mod9, wt, comm)
    loss = lax.psum(loss_loc[0, 0], MESH_AXES)

    res = {}
    for n in big:
        res[n] = _reduce_adamw(recv[n], w_loc[n][0], m_loc[n][0], v_loc[n][0], name="adamw_" + n,
                               tr=REDUCE_ROWS[n])

    packed = _pad_lanes(jnp.concatenate([small[n].reshape(1, -1) for n in SMALL_ORDER], axis=1))
    parts = _exchange([packed], name="gather_small", gather=True)[0].reshape(N_DEV, -1)
    total = _sum_rows(parts, name="sum_small")
    off = 0
    gsmall = {}
    for n in SMALL_ORDER:
        size = small[n].size
        gsmall[n] = total[:, off:off + size]
        off += size
    dmod_all = parts[:, 0:9 * d_model]
    ada_cols = w_ada.shape[-1]
    dmod_cols = lax.dynamic_slice_in_dim(dmod_all, me * ada_cols, ada_cols, axis=1)
    res["w_ada"] = _ada_grad_adamw(c_all.T, dmod_cols, w_ada[0], m_w_ada[0], v_w_ada[0], name="adamw_w_ada")
    g_rows = {"b_ada": gsmall["mod"]}
    for n in REPLICATED[1:]:
        g_rows[n] = gsmall[n]
    for n in CONV_SHARDED:
        taps, width = w_loc[n].shape[1], w_loc[n].shape[2]
        full = gsmall[n].reshape(taps, -1)
        g_rows[n] = lax.dynamic_slice_in_dim(full, me * width, width, axis=1).reshape(1, -1)
    row_names = REPLICATED + CONV_SHARDED
    cat = lambda src: _pad_lanes(jnp.concatenate([src[n].reshape(1, -1) for n in row_names], axis=1))
    g_cat = cat(g_rows)
    d_cat, m_cat, v_cat = _adamw_rows(g_cat, cat(w_loc), cat(m_loc), cat(v_loc), name="adamw_small")
    off = 0
    for n in row_names:
        size = w_loc[n].size
        res[n] = tuple(t[:, off:off + size] for t in (g_cat, d_cat, m_cat, v_cat))
        off += size

    outs = [loss, dx[None]]
    for kind in range(4):
        for n in WEIGHTS:
            outs.append(res[n][kind].reshape(w_loc[n].shape))
    return tuple(outs)
```

```python
import functools
import math
import types

import jax
import jax.numpy as jnp
from jax import lax
from jax.experimental import pallas as pl
from jax.experimental.pallas import tpu as pltpu

F32 = jnp.float32
BF16 = jnp.bfloat16
EPS = 1e-6
N_DEV = 8
CHUNK = 64
HEADS = 8
HEAD_DIM = 128
MESH_AXES = ("x", "y", "c")
VMEM_LIMIT_BYTES = 56 * 1024 * 1024

ADAM_LR = 0.001
ADAM_B1 = 0.9
ADAM_B2 = 0.999
ADAM_EPS = 1e-08
ADAM_WD = 0.01
ADAM_STEP = 10


def _params(sem=None):
    return pltpu.CompilerParams(dimension_semantics=sem, vmem_limit_bytes=VMEM_LIMIT_BYTES)


def _row(n):
    return pl.BlockSpec((1, n), lambda *_: (0, 0))


def _resident(shape):
    nd = len(shape)
    return pl.BlockSpec(shape, lambda *_: (0,) * nd, pipeline_mode=pl.Buffered(1))


def _col_chunks(width, chunk=512):
    return [slice(lo, min(lo + chunk, width)) for lo in range(0, width, chunk)]


def _col_window(w, width, col_block):
    return pl.BlockSpec((w.shape[0], width), lambda *_: (0, col_block), pipeline_mode=pl.Buffered(1))


def _sigmoid(x):
    return 1.0 / (1.0 + jnp.exp(-x))


def _dot(a, b):
    return jnp.dot(a, b, preferred_element_type=F32)


def _dot_nt(a, b):
    return lax.dot_general(a, b, (((1,), (1,)), ((), ())), preferred_element_type=F32)


def _dot_tn(a, b):
    return lax.dot_general(a, b, (((0,), (0,)), ((), ())), preferred_element_type=F32)


def _split_bf16(x):
    hi = x.astype(BF16)
    lo = (x - hi.astype(F32)).astype(BF16)
    return hi, lo


def _dot3(a, b, dot=_dot):
    ah, al = a if isinstance(a, tuple) else _split_bf16(a)
    bh, bl = b if isinstance(b, tuple) else _split_bf16(b)
    return dot(ah, bh) + dot(ah, bl) + dot(al, bh)


def _dot_exact(a, b):
    return jnp.dot(a, b, preferred_element_type=F32, precision=lax.Precision.HIGHEST)


def _my_position():
    return tuple(lax.axis_index(a) for a in MESH_AXES)


def _peer(pos, kk):
    return tuple((1 - p) if (kk >> (2 - b)) & 1 else p for b, p in enumerate(pos))


def _flat_index(pos):
    return pos[0] * 4 + pos[1] * 2 + pos[2]


_ANY = pl.BlockSpec(memory_space=pl.ANY)


class _AllToAll:
    def __init__(self, in_refs, out_refs, send_sems, recv_sems, local_sems):
        pos = _my_position()
        me = _flat_index(pos)
        self.copies = []
        for t in range(len(in_refs)):
            self.copies.append(pltpu.make_async_copy(in_refs[t].at[me], out_refs[t].at[me], local_sems.at[t]))
            for kk in range(1, N_DEV):
                peer = _peer(pos, kk)
                self.copies.append(pltpu.make_async_remote_copy(
                    src_ref=in_refs[t].at[_flat_index(peer)], dst_ref=out_refs[t].at[me],
                    send_sem=send_sems.at[t, kk - 1], recv_sem=recv_sems.at[t, kk - 1],
                    device_id=peer, device_id_type=pl.DeviceIdType.MESH))

    def start(self):
        for cp in self.copies:
            cp.start()

    def finish(self):
        for cp in self.copies:
            cp.wait()


class _AllGather:
    def __init__(self, in_refs, out_refs, send_sems, recv_sems, local_sems):
        self.refs = (in_refs, out_refs, send_sems, recv_sems, local_sems)
        x, y, c = _my_position()
        self.me, self.sibling = (x, y, c), (x, y, 1 - c)
        self.chips = [(1 - x, y), (x, 1 - y), (1 - x, 1 - y)]
        self.core = c

    def _copy(self, t, k, block, to, own=False):
        in_refs, out_refs, send_sems, recv_sems, _ = self.refs
        rows = out_refs[t].at[_flat_index(block)]
        return pltpu.make_async_remote_copy(
            src_ref=in_refs[t] if own else rows, dst_ref=rows,
            send_sem=send_sems.at[t, k], recv_sem=recv_sems.at[t, k],
            device_id=to, device_id_type=pl.DeviceIdType.MESH)

    def _local(self, t):
        in_refs, out_refs, _, _, local_sems = self.refs
        return pltpu.make_async_copy(in_refs[t], out_refs[t].at[_flat_index(self.me)], local_sems.at[t])

    def start(self):
        c = self.core
        for t in range(len(self.refs[0])):
            self._local(t).start()
            self._copy(t, 0, self.me, self.sibling, own=True).start()
            for j, chip in enumerate(self.chips):
                self._copy(t, 1 + j, self.me, (*chip, c), own=True).start()

    def finish(self):
        c = self.core
        n_t = len(self.refs[0])
        for t in range(n_t):
            for j, chip in enumerate(self.chips):
                self._copy(t, 1 + j, (*chip, c), self.me).wait_recv()
                self._copy(t, 4 + j, (*chip, c), self.sibling).start()
        for t in range(n_t):
            self._copy(t, 0, self.sibling, self.me).wait_recv()
            for j, chip in enumerate(self.chips):
                self._copy(t, 4 + j, (*chip, 1 - c), self.me).wait_recv()
            self._copy(t, 0, self.me, self.sibling, own=True).wait_send()
            for j, chip in enumerate(self.chips):
                self._copy(t, 1 + j, self.me, (*chip, c), own=True).wait_send()
                self._copy(t, 4 + j, (*chip, c), self.sibling).wait_send()
            self._local(t).wait()


def _exchange_plan(in_refs, out_refs, send_sems, recv_sems, local_sems, gather):
    return (_AllGather if gather else _AllToAll)(in_refs, out_refs, send_sems, recv_sems, local_sems)


def _exchange_shapes(xs, gather):
    out_shape = [jax.ShapeDtypeStruct(((N_DEV,) + x.shape) if gather else x.shape, x.dtype) for x in xs]
    sems = [pltpu.SemaphoreType.DMA((len(xs), N_DEV - 1)), pltpu.SemaphoreType.DMA((len(xs), N_DEV - 1)),
            pltpu.SemaphoreType.DMA((len(xs),))]
    return out_shape, sems


def _exchange(xs, *, name, gather):
    nt = len(xs)

    def body(*refs):
        plan = _exchange_plan(refs[:nt], refs[nt:2 * nt], *refs[2 * nt:], gather)
        plan.start()
        plan.finish()

    out_shape, sems = _exchange_shapes(xs, gather)
    return pl.pallas_call(body, name=name, in_specs=[_ANY] * nt, out_specs=[_ANY] * nt, out_shape=out_shape,
                          scratch_shapes=sems)(*xs)


def _launch(body, carry, args, *, name, grid, in_specs, out_specs, out_shape, scratch_shapes=(), sem):
    single = not isinstance(out_shape, (list, tuple))
    out_specs = [out_specs] if single else list(out_specs)
    out_shape = [out_shape] if single else list(out_shape)
    if carry is None:
        outs = pl.pallas_call(body, name=name, grid=grid, in_specs=list(in_specs), out_specs=out_specs,
                              out_shape=out_shape, scratch_shapes=list(scratch_shapes),
                              compiler_params=_params(sem))(*args)
        return outs[0] if single else outs
    xs, gather = carry
    nt, n_in, n_out, n_scr = len(xs), len(args), len(out_shape), len(scratch_shapes)
    x_shape, sems = _exchange_shapes(xs, gather)

    def wrapped(*refs):
        c_in, x_in = refs[:n_in], refs[n_in:n_in + nt]
        c_out = refs[n_in + nt:n_in + nt + n_out]
        x_out = refs[n_in + nt + n_out:n_in + 2 * nt + n_out]
        scr = refs[n_in + 2 * nt + n_out:]
        ids = [pl.program_id(a) for a in range(len(grid))]
        first = functools.reduce(jnp.logical_and, [i == 0 for i in ids])
        last = functools.reduce(jnp.logical_and, [i == g - 1 for i, g in zip(ids, grid)])
        plan = lambda: _exchange_plan(x_in, x_out, *scr[n_scr:], gather)

        @pl.when(first)
        def _():
            plan().start()

        body(*c_in, *c_out, *scr[:n_scr])

        @pl.when(last)
        def _():
            plan().finish()

    outs = pl.pallas_call(
        wrapped, name=name, grid=grid, in_specs=list(in_specs) + [_ANY] * nt,
        out_specs=out_specs + [_ANY] * nt, out_shape=out_shape + x_shape,
        scratch_shapes=list(scratch_shapes) + sems,
        compiler_params=_params(("arbitrary",) * len(grid)))(*args, *xs)
    compute = outs[:n_out]
    return (compute[0] if single else compute), outs[n_out:]


def _norm_mod(x, nw, sc, sh):
    r = lax.rsqrt(jnp.mean(x * x, axis=-1, keepdims=True) + EPS)
    return (x * r * nw) * (1.0 + sc) + sh


def _norm_mod_bwd(x, nw, sc, du):
    r = lax.rsqrt(jnp.mean(x * x, axis=-1, keepdims=True) + EPS)
    xhat = x * r
    n = xhat * nw
    dsh = jnp.sum(du, axis=0, keepdims=True)
    dsc = jnp.sum(du * n, axis=0, keepdims=True)
    dn = du * (1.0 + sc)
    dnw = jnp.sum(dn * xhat, axis=0, keepdims=True)
    dxhat = dn * nw
    dx = r * (dxhat - xhat * jnp.mean(dxhat * xhat, axis=-1, keepdims=True))
    return dx, dnw, dsc, dsh


def _ffn_up_fwd(h, nw, sc, sh, wup, *, name, ts=1024, tn=1408, carry=None):
    s, d = h.shape
    ts = min(ts, s)
    f_dim = wup.shape[1] // 2
    nj = f_dim // tn

    def body(h_ref, nw_ref, sc_ref, sh_ref, wa_ref, wb_ref, u_ref, a_ref, b_ref, f_ref):
        @pl.when(pl.program_id(1) == 0)
        def _():
            u_ref[...] = _norm_mod(h_ref[...], nw_ref[...], sc_ref[...], sh_ref[...]).astype(BF16)

        u = u_ref[...]

        def epilogue(a, b, cs):
            a_ref[:, cs] = a.astype(BF16)
            b_ref[:, cs] = b.astype(BF16)
            f_ref[:, cs] = (a * _sigmoid(a) * b).astype(BF16)

        pending = None
        for cs in _col_chunks(tn):
            products = (_dot(u, wa_ref[:, cs]), _dot(u, wb_ref[:, cs]), cs)
            if pending is not None:
                epilogue(*pending)
            pending = products
        epilogue(*pending)

    return _launch(
        body, carry, (h, nw, sc, sh, wup, wup), name=name, grid=(s // ts, nj),
        in_specs=[pl.BlockSpec((ts, d), lambda i, j: (i, 0)), _row(d), _row(d), _row(d),
                  pl.BlockSpec((d, tn), lambda i, j: (0, j)),
                  pl.BlockSpec((d, tn), lambda i, j: (0, j + nj))],
        out_specs=[pl.BlockSpec((ts, d), lambda i, j: (i, 0)),
                   pl.BlockSpec((ts, tn), lambda i, j: (i, j)),
                   pl.BlockSpec((ts, tn), lambda i, j: (i, j)),
                   pl.BlockSpec((ts, tn), lambda i, j: (i, j))],
        out_shape=[jax.ShapeDtypeStruct((s, d), BF16)] + [jax.ShapeDtypeStruct((s, f_dim), BF16)] * 3,
        sem=("parallel", "arbitrary"))


def _ffn_down_fwd(f, wd, h, g, *, name, ts=512):
    s, f_dim = f.shape
    d = wd.shape[1]

    def body(f_ref, wd_ref, h_ref, g_ref, y_ref, ho_ref):
        y = _dot(f_ref[...], wd_ref[...])
        y_ref[...] = y.astype(BF16)
        ho_ref[...] = h_ref[...] + (0.5 * g_ref[...]) * y

    return pl.pallas_call(
        body, name=name, grid=(s // ts,),
        in_specs=[pl.BlockSpec((ts, f_dim), lambda i: (i, 0)), _resident((f_dim, d)),
                  pl.BlockSpec((ts, d), lambda i: (i, 0)), _row(d)],
        out_specs=[pl.BlockSpec((ts, d), lambda i: (i, 0)), pl.BlockSpec((ts, d), lambda i: (i, 0))],
        out_shape=[jax.ShapeDtypeStruct((s, d), BF16), jax.ShapeDtypeStruct((s, d), F32)],
        compiler_params=_params(("parallel",)),
    )(f, wd, h, g)


def _ffn_bwd_act(dh, g, y, a, b, wd, *, name, ts=256, carry=None):
    s, d = dh.shape
    f_dim = a.shape[1]

    def body(dh_ref, g_ref, y_ref, a_ref, b_ref, wd_ref, dy_ref, dab_ref, dg_ref):
        dh_v = dh_ref[...]
        dy = ((0.5 * g_ref[...]) * dh_v).astype(BF16)
        dy_ref[...] = dy
        part = jnp.sum(0.5 * dh_v * y_ref[...].astype(F32), axis=0, keepdims=True)

        @pl.when(pl.program_id(0) == 0)
        def _():
            dg_ref[...] = jnp.zeros_like(dg_ref)

        dg_ref[...] += part

        def epilogue(df, cs):
            av = a_ref[:, cs].astype(F32)
            bv = b_ref[:, cs].astype(F32)
            sg = _sigmoid(av)
            dab_ref[:, cs] = (df * bv * (sg * (1.0 + av * (1.0 - sg)))).astype(BF16)
            dab_ref[:, slice(f_dim + cs.start, f_dim + cs.stop)] = (df * (av * sg)).astype(BF16)

        pending = None
        for cs in _col_chunks(f_dim):
            product = (_dot_nt(dy, wd_ref[cs, :]), cs)
            if pending is not None:
                epilogue(*pending)
            pending = product
        epilogue(*pending)

    return _launch(
        body, carry, (dh, g, y, a, b, wd), name=name, grid=(s // ts,),
        in_specs=[pl.BlockSpec((ts, d), lambda i: (i, 0)), _row(d),
                  pl.BlockSpec((ts, d), lambda i: (i, 0)),
                  pl.BlockSpec((ts, f_dim), lambda i: (i, 0)),
                  pl.BlockSpec((ts, f_dim), lambda i: (i, 0)),
                  _resident((f_dim, d))],
        out_specs=[pl.BlockSpec((ts, d), lambda i: (i, 0)),
                   pl.BlockSpec((ts, 2 * f_dim), lambda i: (i, 0)), _row(d)],
        out_shape=[jax.ShapeDtypeStruct((s, d), BF16), jax.ShapeDtypeStruct((s, 2 * f_dim), BF16),
                   jax.ShapeDtypeStruct((1, d), F32)],
        sem=("arbitrary",))


def _norm_mod_matmul_bwd(pairs, h, nw, sc, dh_in, *, name, ts=256, carry=None):
    s, d = h.shape
    n_pairs = len(pairs)

    def body(*refs):
        dx_refs = refs[:n_pairs]
        w_refs = refs[n_pairs:2 * n_pairs]
        h_ref, nw_ref, sc_ref, dhi_ref, dho_ref, dnw_ref, dsc_ref, dsh_ref = refs[2 * n_pairs:]
        du = _dot_nt(dx_refs[0][...], w_refs[0][...])
        for k in range(1, n_pairs):
            du = du + _dot_nt(dx_refs[k][...], w_refs[k][...])
        dx, dnw, dsc, dsh = _norm_mod_bwd(h_ref[...], nw_ref[...], sc_ref[...], du)
        dho_ref[...] = dhi_ref[...] + dx

        @pl.when(pl.program_id(0) == 0)
        def _():
            dnw_ref[...] = jnp.zeros_like(dnw_ref)
            dsc_ref[...] = jnp.zeros_like(dsc_ref)
            dsh_ref[...] = jnp.zeros_like(dsh_ref)

        dnw_ref[...] += dnw
        dsc_ref[...] += dsc
        dsh_ref[...] += dsh

    dxs = [p[0] for p in pairs]
    ws = [p[1] for p in pairs]
    tile = pl.BlockSpec((ts, d), lambda i: (i, 0))
    return _launch(
        body, carry, (*dxs, *ws, h, nw, sc, dh_in), name=name, grid=(s // ts,),
        in_specs=([pl.BlockSpec((ts, x.shape[1]), lambda i: (i, 0)) for x in dxs]
                  + [_col_window(w, x.shape[1], p[2] if len(p) > 2 else 0) for p, x, w in zip(pairs, dxs, ws)]
                  + [tile, _row(d), _row(d), tile]),
        out_specs=[tile, _row(d), _row(d), _row(d)],
        out_shape=[jax.ShapeDtypeStruct((s, d), F32)] + [jax.ShapeDtypeStruct((1, d), F32)] * 3,
        sem=("arbitrary",))


def _matmul_tn(a, b, *, name, tm, tn, tk=1024, carry=None):
    s, m = a.shape
    n = b.shape[1]
    tk = min(tk, s)
    nk = s // tk

    def body(a_ref, b_ref, o_ref, acc_ref):
        k = pl.program_id(2)

        @pl.when(k == 0)
        def _():
            acc_ref[...] = jnp.zeros_like(acc_ref)

        acc_ref[...] += _dot_tn(a_ref[...], b_ref[...])

        @pl.when(k == nk - 1)
        def _():
            o_ref[...] = acc_ref[...].astype(o_ref.dtype)

    return _launch(
        body, carry, (a, b), name=name, grid=(m // tm, n // tn, nk),
        in_specs=[pl.BlockSpec((tk, tm), lambda i, j, k: (k, i)),
                  pl.BlockSpec((tk, tn), lambda i, j, k: (k, j))],
        out_specs=pl.BlockSpec((tm, tn), lambda i, j, k: (i, j)),
        out_shape=jax.ShapeDtypeStruct((m, n), BF16),
        scratch_shapes=[pltpu.VMEM((tm, tn), F32)],
        sem=("parallel", "parallel", "arbitrary"))


def _in_proj_fwd(h, nw, sc, sh, w_main, w_ba, *, name, ts=1024, tn=1536, carry=None):
    s, d = h.shape
    ts = min(ts, s)
    n_main = w_main.shape[1]
    n_ba = w_ba.shape[1]

    def body(h_ref, nw_ref, sc_ref, sh_ref, w_ref, wba_ref, u_ref, p_ref, ba_ref):
        @pl.when(pl.program_id(1) == 0)
        def _():
            u0 = _norm_mod(h_ref[...], nw_ref[...], sc_ref[...], sh_ref[...]).astype(BF16)
            u_ref[...] = u0
            ba_ref[...] = _dot(u0, wba_ref[...])

        p_ref[...] = _dot(u_ref[...], w_ref[...]).astype(BF16)

    return _launch(
        body, carry, (h, nw, sc, sh, w_main, w_ba), name=name, grid=(s // ts, n_main // tn),
        in_specs=[pl.BlockSpec((ts, d), lambda i, j: (i, 0)), _row(d), _row(d), _row(d),
                  pl.BlockSpec((d, tn), lambda i, j: (0, j)), _resident((d, n_ba))],
        out_specs=[pl.BlockSpec((ts, d), lambda i, j: (i, 0)),
                   pl.BlockSpec((ts, tn), lambda i, j: (i, j)),
                   pl.BlockSpec((ts, n_ba), lambda i, j: (i, 0))],
        out_shape=[jax.ShapeDtypeStruct((s, d), BF16), jax.ShapeDtypeStruct((s, n_main), BF16),
                   jax.ShapeDtypeStruct((s, n_ba), F32)],
        sem=("parallel", "arbitrary"))


QKV_W = 3 * HEADS * HEAD_DIM
Z_OFF, Z_W = 3072, 1024
GATE_OFF, GATE_W = 4096, 2048
A_OFF, A_W = 6144, 1536
N_MAIN = 7680
CONV_A = 512
BA_W = 128

L_BETA, L_G, L_EG, L_EKD, L_EGC = 0, 8, 16, 24, 32


def _softplus(z):
    e = jnp.exp(-jnp.abs(z))
    small = e * (1.0 - e * (0.5 - e * (1.0 / 3.0)))
    return jnp.maximum(z, 0.0) + jnp.where(e < 1e-3, small, jnp.log(1.0 + e))


def _tri(n, sgn, strict=False):
    i = lax.broadcasted_iota(jnp.int32, (n, n), 0)
    j = lax.broadcasted_iota(jnp.int32, (n, n), 1)
    dlt = (i - j) * sgn
    return (dlt > 0) if strict else (dlt >= 0)


def _scal_fwd(ba, alog, dtb, *, name, ts=512):
    s = ba.shape[0]

    def body(ba_ref, al_ref, dt_ref, o_ref):
        d = pl.program_id(0)
        sgn = 1 - 2 * d
        x = ba_ref[...]
        lane = lax.broadcasted_iota(jnp.int32, x.shape, 1)
        beta = _sigmoid(x)
        g = -jnp.exp(al_ref[0]) * _softplus(x + dt_ref[0])
        g = jnp.where((lane >= L_G) & (lane < L_EGC + 8), g, 0.0)
        ltri = jnp.where(_tri(CHUNK, sgn), 1.0, 0.0).astype(F32)
        for c in range(ts // CHUNK):
            rows = slice(c * CHUNK, (c + 1) * CHUNK)
            gc = _dot_exact(ltri, g[rows])
            g_end = jnp.where(d == 0, gc[CHUNK - 1:CHUNK], gc[0:1])
            ln = lane[rows]
            out = jnp.where(ln < L_G, beta[rows],
                  jnp.where(ln < L_EG, gc,
                  jnp.where(ln < L_EKD, jnp.exp(gc),
                  jnp.where(ln < L_EGC, jnp.exp(g_end - gc),
                  jnp.where(ln < L_EGC + 8, jnp.broadcast_to(jnp.exp(g_end), gc.shape), 0.0)))))
            o_ref[0, rows, :] = out

    return pl.pallas_call(
        body, name=name, grid=(2, s // ts),
        in_specs=[pl.BlockSpec((ts, BA_W), lambda d, i: (i, d)),
                  pl.BlockSpec((1, 1, BA_W), lambda d, i: (d, 0, 0)),
                  pl.BlockSpec((1, 1, BA_W), lambda d, i: (d, 0, 0))],
        out_specs=pl.BlockSpec((1, ts, BA_W), lambda d, i: (d, i, 0)),
        out_shape=jax.ShapeDtypeStruct((2, s, BA_W), F32),
        compiler_params=_params(("parallel", "parallel")),
    )(ba, alog, dtb)


def _scal_bwd(dscal, drow, ba, alog, dtb, *, name, ts=512):
    s = ba.shape[0]

    def body(ds_ref, dr_ref, ba_ref, al_ref, dt_ref, dba_ref, dal_ref, ddt_ref):
        d = pl.program_id(0)
        sgn = 1 - 2 * d
        x = ba_ref[...]
        lane = lax.broadcasted_iota(jnp.int32, x.shape, 1)
        in_g = (lane >= L_G) & (lane < L_G + 8)
        beta = _sigmoid(x)
        z = x + dt_ref[0]
        neg_a = -jnp.exp(al_ref[0])
        g = neg_a * _softplus(z)
        dsv = ds_ref[0]
        dgc = jnp.where(in_g, dsv + dr_ref[0], 0.0)
        utri = jnp.where(_tri(CHUNK, -sgn), 1.0, 0.0).astype(F32)
        dal = jnp.zeros((1, BA_W), F32)
        ddt = jnp.zeros((1, BA_W), F32)
        for c in range(ts // CHUNK):
            rows = slice(c * CHUNK, (c + 1) * CHUNK)
            dg = _dot_exact(utri, dgc[rows])
            dz = dg * neg_a * _sigmoid(z[rows])
            dal = dal + jnp.sum(dg * g[rows], axis=0, keepdims=True)
            ddt = ddt + jnp.sum(dz, axis=0, keepdims=True)
            b = beta[rows]
            out = jnp.where(lane[rows] < L_G, dsv[rows] * b * (1.0 - b), jnp.where(in_g[rows], dz, 0.0))
            dba_ref[rows, :] = out.astype(BF16)

        @pl.when(pl.program_id(1) == 0)
        def _():
            dal_ref[...] = jnp.zeros_like(dal_ref)
            ddt_ref[...] = jnp.zeros_like(ddt_ref)

        dal_ref[0] += dal
        ddt_ref[0] += ddt

    row3 = pl.BlockSpec((1, 1, BA_W), lambda d, i: (d, 0, 0))
    tok3 = pl.BlockSpec((1, ts, BA_W), lambda d, i: (d, i, 0))
    return pl.pallas_call(
        body, name=name, grid=(2, s // ts),
        in_specs=[tok3, tok3, pl.BlockSpec((ts, BA_W), lambda d, i: (i, d)), row3, row3],
        out_specs=[pl.BlockSpec((ts, BA_W), lambda d, i: (i, d)), row3, row3],
        out_shape=[jax.ShapeDtypeStruct((s, 2 * BA_W), BF16), jax.ShapeDtypeStruct((2, 1, BA_W), F32),
                   jax.ShapeDtypeStruct((2, 1, BA_W), F32)],
        compiler_params=_params(("arbitrary", "arbitrary")),
    )(dscal, drow, ba, alog, dtb)


HALO = 16


def _halo_specs(ts, width, col_block, n_rows, rows=HALO):
    r = ts // rows
    last = n_rows // rows - 1
    return [pl.BlockSpec((rows, width), lambda i: (jnp.maximum(i * r - 1, 0), col_block)),
            pl.BlockSpec((ts, width), lambda i: (i, col_block)),
            pl.BlockSpec((rows, width), lambda i: (jnp.minimum((i + 1) * r, last), col_block))]


def _fill_halo(dst_ref, prev_ref, cur_ref, next_ref, first, last, fn=lambda r: r[...].astype(F32)):
    h = prev_ref.shape[0]
    ts = cur_ref.shape[0]
    p = fn(prev_ref)
    n = fn(next_ref)
    dst_ref[0:h, :] = jnp.where(first, 0.0, p)
    dst_ref[h:h + ts, :] = fn(cur_ref)
    dst_ref[h + ts:h + ts + h, :] = jnp.where(last, 0.0, n)


def _dwconv_rows(src_ref, w, start, n_rows, cols):
    acc = w[0:1, :] * src_ref[start:start + n_rows, cols]
    for i in range(1, w.shape[0]):
        acc = acc + w[i:i + 1, :] * src_ref[start + i:start + i + n_rows, cols]
    return acc


def _l2norm_heads(act, scale):
    outs = []
    for hd in range(HEADS):
        seg = act[:, hd * HEAD_DIM:(hd + 1) * HEAD_DIM]
        outs.append(seg * (lax.rsqrt(jnp.sum(seg * seg, axis=-1, keepdims=True) + EPS) * scale))
    return jnp.concatenate(outs, axis=-1)


Q_SCALE = HEAD_DIM ** -0.5


def _conv_fwd(proj, conv_dn, conv_a, *, name, ts=256):
    s = proj.shape[0]
    hd = HEADS * HEAD_DIM
    nt = s // ts

    def body(qp_ref, qc_ref, qn_ref, ap_ref, ac_ref, an_ref, wdn_ref, wa_ref,
             q_ref, k_ref, v_ref, ya_ref, xs_ref, xa_ref):
        i = pl.program_id(0)
        first, last = i == 0, i == nt - 1
        _fill_halo(xs_ref, qp_ref, qc_ref, qn_ref, first, last)
        wdn = wdn_ref[...]
        for part, o_ref in enumerate((q_ref, k_ref, v_ref)):
            cols = slice(part * hd, (part + 1) * hd)
            pre = _dwconv_rows(xs_ref, wdn[:, cols], HALO - 2, ts, cols)
            act = pre * _sigmoid(pre)
            if part == 0:
                act = _l2norm_heads(act, Q_SCALE)
            elif part == 1:
                act = _l2norm_heads(act, 1.0)
            o_ref[...] = act
        cv = lambda r: r[:, CONV_A:2 * CONV_A].astype(F32) * r[:, 2 * CONV_A:].astype(F32)
        _fill_halo(xa_ref, ap_ref, ac_ref, an_ref, first, last, fn=cv)
        conv = _dwconv_rows(xa_ref, wa_ref[...], HALO - 1, ts, slice(0, CONV_A))
        ya_ref[...] = (ac_ref[:, 0:CONV_A].astype(F32) * conv).astype(BF16)

    tile = lambda w: pl.BlockSpec((ts, w), lambda i: (i, 0))
    return pl.pallas_call(
        body, name=name, grid=(nt,),
        in_specs=(_halo_specs(ts, QKV_W, 0, s) + _halo_specs(ts, A_W, A_OFF // A_W, s)
                  + [_resident(conv_dn.shape), _resident(conv_a.shape)]),
        out_specs=[tile(hd), tile(hd), tile(hd), tile(CONV_A)],
        out_shape=[jax.ShapeDtypeStruct((s, hd), F32)] * 3 + [jax.ShapeDtypeStruct((s, CONV_A), BF16)],
        scratch_shapes=[pltpu.VMEM((ts + 2 * HALO, QKV_W), F32), pltpu.VMEM((ts + 2 * HALO, CONV_A), F32)],
        compiler_params=_params(("parallel",)),
    )(proj, proj, proj, proj, proj, proj, conv_dn, conv_a)


INV_BASE = 8


GROUP = 4
GROWS = GROUP * CHUNK
N_GROUPS = HEADS // GROUP


def _stack(parts):
    return jnp.concatenate(parts, axis=0)


M_INCL, M_STRICT, M_EYE, M_BASE, M_JOIN = 0, 1, 2, 3, 4
JOIN_SIZES = (16, 32, 64)
N_MASKS = M_JOIN + len(JOIN_SIZES)


def _write_group_masks(mask_ref, sgn, n_masks):
    i = lax.broadcasted_iota(jnp.int32, (GROWS, GROWS), 0)
    j = lax.broadcasted_iota(jnp.int32, (GROWS, GROWS), 1)
    same = lambda m: jnp.right_shift(i, int(math.log2(m))) == jnp.right_shift(j, int(math.log2(m)))
    dlt = (i - j) * sgn
    one = lambda cond: jnp.where(cond, 1.0, 0.0).astype(F32)
    mask_ref[M_INCL] = one(same(CHUNK) & (dlt >= 0))
    mask_ref[M_STRICT] = one(same(CHUNK) & (dlt > 0))
    if n_masks > M_EYE:
        mask_ref[M_EYE] = one(i == j)
        mask_ref[M_BASE] = one(same(INV_BASE))
        for lvl, m in enumerate(JOIN_SIZES):
            mask_ref[M_JOIN + lvl] = one(same(m) & jnp.logical_not(same(m // 2)))


def _group_decay(gcol, grow, mask_ref):
    return jnp.exp(jnp.minimum(gcol - grow, 0.0)) * mask_ref[M_INCL]


def _block_inverse_many(a_ms, mask_refs):
    xs = [-(a * m[M_BASE]) for a, m in zip(a_ms, mask_refs)]
    ts = [m[M_EYE] + x for x, m in zip(xs, mask_refs)]
    ps = xs
    for _ in range(int(math.log2(INV_BASE)) - 1):
        p_bs = [p.astype(BF16) for p in ps]
        ps = [_dot(p_b, p_b) for p_b in p_bs]
        ts = [t + _dot(t.astype(BF16), p.astype(BF16)) for t, p in zip(ts, ps)]
    for lvl in range(len(JOIN_SIZES)):
        t_bs = [t.astype(BF16) for t in ts]
        joins = [(a * m[M_JOIN + lvl]).astype(BF16) for a, m in zip(a_ms, mask_refs)]
        mids = [_dot(t_b, j).astype(BF16) for t_b, j in zip(t_bs, joins)]
        ts = [t - _dot(mid, t_b) for t, mid, t_b in zip(ts, mids, t_bs)]
    return ts


def _group_operands(q_ref, k_ref, v_ref, scv, grp):
    heads = [GROUP * grp + t for t in range(GROUP)]
    tiles = lambda ref: [ref[:, h * HEAD_DIM:(h + 1) * HEAD_DIM] for h in heads]
    col = lambda base: [scv[:, base + h:base + h + 1] for h in heads]
    egc = [scv[0:1, L_EGC + h:L_EGC + h + 1] for h in heads]
    return heads, tiles(q_ref), tiles(k_ref), tiles(v_ref), col(L_BETA), col(L_G), col(L_EG), col(L_EKD), egc


def _delta_fwd(q, k, v, scal, grow, *, name):
    s = q.shape[0]
    n = s // CHUNK
    hd_all = HEADS * HEAD_DIM

    def body(*refs):
        ins, outs, (state, mask_ref) = refs[:10], refs[10:18], refs[18:]

        @pl.when(pl.program_id(0) == 0)
        def _():
            state[...] = jnp.zeros_like(state)
            for d in range(2):
                _write_group_masks(mask_ref.at[d], 1 - 2 * d, N_MASKS)

        chains = []
        for d in range(2):
            q_ref, k_ref, v_ref, sc_ref, gr_ref = ins[5 * d:5 * d + 5]
            scv = sc_ref[0]
            for grp in range(N_GROUPS):
                chains.append(dict(
                    d=d, grp=grp, gr_ref=gr_ref, out=outs[4 * d:4 * d + 4], state=state.at[d], masks=mask_ref.at[d],
                    ops=_group_operands(q_ref, k_ref, v_ref, scv, grp)))
        for ch in chains:
            heads, qs, ks, vs, beta, gcol, eg, ekd, egc = ch["ops"]
            ch["dm"] = _group_decay(_stack(gcol), ch["gr_ref"][0, 0, ch["grp"]:ch["grp"] + 1, :], ch["masks"])
            ch["k_b"] = _stack(ks).astype(BF16)
            ch["kb_b"] = _stack([ks[t] * beta[t] for t in range(GROUP)]).astype(BF16)
        for ch in chains:
            ch["a_m"] = _dot_nt(ch["kb_b"], ch["k_b"]) * ch["dm"] * ch["masks"][M_STRICT]
        tinvs = _block_inverse_many([ch["a_m"] for ch in chains], [ch["masks"] for ch in chains])
        for ch, tinv in zip(chains, tinvs):
            heads, qs, ks, vs, beta, gcol, eg, ekd, egc = ch["ops"]
            o_ref, st_ref, t_ref, vn_ref = ch["out"]
            ch["tinv"] = tinv.astype(BF16)
            t_ref[0, ch["grp"]] = ch["tinv"]
            ch["p_b"] = (_dot_nt(_stack(qs).astype(BF16), ch["k_b"]) * ch["dm"]).astype(BF16)
            ch["sh"] = [ch["state"][h] for h in heads]
            ch["sh_b"] = [x.astype(BF16) for x in ch["sh"]]
            for t, h in enumerate(heads):
                st_ref[0, h] = ch["sh_b"][t]
        for ch in chains:
            heads, qs, ks, vs, beta, gcol, eg, ekd, egc = ch["ops"]
            ch["br"] = _stack([beta[t] * (vs[t] - _dot((ks[t] * eg[t]).astype(BF16), ch["sh_b"][t]))
                               for t in range(GROUP)]).astype(BF16)
        for ch in chains:
            ch["vn_b"] = _dot(ch["tinv"], ch["br"]).astype(BF16)
        for ch in chains:
            ch["o_intra"] = _dot(ch["p_b"], ch["vn_b"])
        for ch in chains:
            heads, qs, ks, vs, beta, gcol, eg, ekd, egc = ch["ops"]
            o_ref, st_ref, t_ref, vn_ref = ch["out"]
            for t, h in enumerate(heads):
                rows = slice(t * CHUNK, (t + 1) * CHUNK)
                cols = slice(h * HEAD_DIM, (h + 1) * HEAD_DIM)
                o_ref[:, cols] = _dot((qs[t] * eg[t]).astype(BF16), ch["sh_b"][t]) + ch["o_intra"][rows]
                ch["state"][h] = egc[t] * ch["sh"][t] + _dot_tn((ks[t] * ekd[t]).astype(BF16), ch["vn_b"][rows])
                vn_ref[:, cols] = ch["vn_b"][rows]

    at = [lambda c: c, lambda c: n - 1 - c]
    in_specs, out_specs = [], []
    for d in range(2):
        tok = pl.BlockSpec((CHUNK, hd_all), lambda c, d=d: (at[d](c), 0))
        in_specs += [tok] * 3 + [pl.BlockSpec((1, CHUNK, BA_W), lambda c, d=d: (d, at[d](c), 0)),
                                 pl.BlockSpec((1, 1, N_GROUPS, GROWS), lambda c, d=d: (d, at[d](c), 0, 0))]
        out_specs += [tok, pl.BlockSpec((1, HEADS, HEAD_DIM, HEAD_DIM), lambda c, d=d: (at[d](c), 0, 0, 0)),
                      pl.BlockSpec((1, N_GROUPS, GROWS, GROWS), lambda c, d=d: (at[d](c), 0, 0, 0)), tok]
    per_dir_shape = [jax.ShapeDtypeStruct((s, hd_all), F32),
                     jax.ShapeDtypeStruct((n, HEADS, HEAD_DIM, HEAD_DIM), BF16),
                     jax.ShapeDtypeStruct((n, N_GROUPS, GROWS, GROWS), BF16),
                     jax.ShapeDtypeStruct((s, hd_all), BF16)]
    outs = pl.pallas_call(
        body, name=name, grid=(n,), in_specs=in_specs, out_specs=out_specs, out_shape=per_dir_shape * 2,
        scratch_shapes=[pltpu.VMEM((2, HEADS, HEAD_DIM, HEAD_DIM), F32),
                        pltpu.VMEM((2, N_MASKS, GROWS, GROWS), F32)],
        compiler_params=_params(("arbitrary",)),
    )(*([q, k, v, scal, grow] * 2))
    return tuple((outs[i], outs[4 + i]) for i in range(4))


def _delta_bwd(q, k, v, scal, grow, states, tinv, vn, do, *, name, carry=None):
    s = q.shape[0]
    n = s // CHUNK
    hd_all = HEADS * HEAD_DIM

    grp_rows = [slice(t * CHUNK, (t + 1) * CHUNK) for t in range(GROUP)]
    per_head = lambda fn: _stack([fn(t) for t in range(GROUP)])

    def body(*refs):
        ins, outs, (dstate, mask_ref) = refs[:18], refs[18:28], refs[28:]

        @pl.when(pl.program_id(0) == 0)
        def _():
            dstate[...] = jnp.zeros_like(dstate)
            for d in range(2):
                _write_group_masks(mask_ref.at[d], 1 - 2 * d, M_EYE)

        chains = []
        for d in range(2):
            q_ref, k_ref, v_ref, sc_ref, gr_ref, st_ref, t_ref, vn_ref, do_ref = ins[9 * d:9 * d + 9]
            scv = sc_ref[0]
            for grp in range(N_GROUPS):
                c = types.SimpleNamespace(d=d, grp=grp, out=outs[5 * d:5 * d + 5], dstate=dstate.at[d],
                                          masks=mask_ref.at[d])
                (c.heads, qs, ks, c.vs, beta, gcol, eg, ekd, c.egc) = _group_operands(q_ref, k_ref, v_ref, scv, grp)
                c.cols = [slice(h * HEAD_DIM, (h + 1) * HEAD_DIM) for h in c.heads]
                c.dm = _group_decay(_stack(gcol), gr_ref[0, 0, grp:grp + 1, :], c.masks)
                c.dm_strict = c.dm * c.masks[M_STRICT]
                other = mask_ref.at[1 - d]
                c.dm_t = jnp.exp(jnp.minimum(gr_ref[0, 0, grp:grp + 1, :] - _stack(gcol), 0.0)) * other[M_INCL]
                c.dm_t_strict = c.dm_t * other[M_STRICT]
                c.beta, c.eg, c.ekd = _stack(beta), _stack(eg), _stack(ekd)
                c.q, c.k = _stack(qs), _stack(ks)
                c.q_b, c.k_b = c.q.astype(BF16), c.k.astype(BF16)
                c.kb_b = (c.k * c.beta).astype(BF16)
                c.kg, c.qg, c.kd = c.k * c.eg, c.q * c.eg, c.k * c.ekd
                c.kg_b, c.qg_b, c.kd_b = c.kg.astype(BF16), c.qg.astype(BF16), c.kd.astype(BF16)
                c.vn_b = _stack([vn_ref[:, cc] for cc in c.cols])
                c.do_b = _stack([do_ref[:, cc] for cc in c.cols]).astype(BF16)
                c.sh_b = [st_ref[0, h] for h in c.heads]
                c.dsp = [c.dstate[h] for h in c.heads]
                c.dsp_b = [x.astype(BF16) for x in c.dsp]
                c.t_b = t_ref[0, grp]
                chains.append(c)
        for c in chains:
            c.kk = _dot_nt(c.kb_b, c.k_b)
            c.qk = _dot_nt(c.q_b, c.k_b)
            c.pt_b = (_dot_nt(c.k_b, c.q_b) * c.dm_t).astype(BF16)
        for c in chains:
            c.r = per_head(lambda t: c.vs[t] - _dot(c.kg_b[grp_rows[t]], c.sh_b[t]))
            c.kd_ds = per_head(lambda t: _dot(c.kd_b[grp_rows[t]], c.dsp_b[t]))
        for c in chains:
            c.dvn_b = (_dot(c.pt_b, c.do_b) + c.kd_ds).astype(BF16)
        for c in chains:
            c.db = _dot_tn(c.t_b, c.dvn_b)
        for c in chains:
            c.dr = c.db * c.beta
            c.dbeta = jnp.sum(c.db * c.r, axis=-1, keepdims=True)
            c.dr_b, c.db_b = c.dr.astype(BF16), c.db.astype(BF16)
        for c in chains:
            c.dkg = -per_head(lambda t: _dot_nt(c.dr_b[grp_rows[t]], c.sh_b[t]))
            c.dqg = per_head(lambda t: _dot_nt(c.do_b[grp_rows[t]], c.sh_b[t]))
            c.dkd = per_head(lambda t: _dot_nt(c.vn_b[grp_rows[t]], c.dsp_b[t]))
        for c in chains:
            c.dpm = _dot_nt(c.do_b, c.vn_b) * c.dm
            c.dam = -_dot_nt(c.db_b, c.vn_b) * c.dm_strict
            c.dpm_b, c.dam_b = c.dpm.astype(BF16), c.dam.astype(BF16)
            c.dpm_t_b = (_dot_nt(c.vn_b, c.do_b) * c.dm_t).astype(BF16)
            c.dam_t_b = (-_dot_nt(c.vn_b, c.db_b) * c.dm_t_strict).astype(BF16)
        for c in chains:
            c.dkb = _dot(c.dam_b, c.k_b)
            c.dq = c.dqg * c.eg + _dot(c.dpm_b, c.k_b)
        for c in chains:
            c.dk = (c.dkg * c.eg + c.dkd * c.ekd + _dot(c.dpm_t_b, c.q_b) + _dot(c.dam_t_b, c.kb_b)
                    + c.dkb * c.beta)
        lane = lax.broadcasted_iota(jnp.int32, (CHUNK, BA_W), 1)
        row = lax.broadcasted_iota(jnp.int32, (CHUNK, 1), 0)
        dsc_acc = [jnp.zeros((CHUNK, BA_W), F32) for _ in range(2)]
        for c in chains:
            dq_ref, dk_ref, dv_ref, dsc_ref, dgr_ref = c.out
            end_row = CHUNK - 1 if c.d == 0 else 0
            dbeta = c.dbeta + jnp.sum(c.dkb * c.k, axis=-1, keepdims=True)
            m = c.dpm * c.qk + c.dam * c.kk
            kd_term = jnp.sum(c.dkd * c.kd, axis=-1, keepdims=True)
            dgcol = (jnp.sum(c.dqg * c.qg, axis=-1, keepdims=True) + jnp.sum(c.dkg * c.kg, axis=-1, keepdims=True)
                     - kd_term + jnp.sum(m, axis=-1, keepdims=True))
            dgr_ref[0, c.grp:c.grp + 1, :] = -jnp.sum(m, axis=0, keepdims=True)
            for t, h in enumerate(c.heads):
                rows, cols = grp_rows[t], c.cols[t]
                dq_ref[:, cols] = c.dq[rows]
                dk_ref[:, cols] = c.dk[rows]
                dv_ref[:, cols] = c.dr[rows]
                dg_end = jnp.sum(kd_term[rows]) + c.egc[t] * jnp.sum(c.dsp[t] * c.sh_b[t].astype(F32))
                dgcol_h = dgcol[rows] + jnp.where(row == end_row, dg_end, 0.0)
                dsc_acc[c.d] = jnp.where(lane == L_BETA + h, dbeta[rows], dsc_acc[c.d])
                dsc_acc[c.d] = jnp.where(lane == L_G + h, dgcol_h, dsc_acc[c.d])
                c.dstate[h] = (_dot_tn(c.qg_b[rows], c.do_b[rows]) + c.egc[t] * c.dsp[t]
                               - _dot_tn(c.kg_b[rows], c.dr_b[rows]))
        for d in range(2):
            outs[5 * d + 3][...] = dsc_acc[d]

    at = [lambda c: n - 1 - c, lambda c: c]
    in_specs, out_specs, args = [], [], []
    for d in range(2):
        tok = pl.BlockSpec((CHUNK, hd_all), lambda c, d=d: (at[d](c), 0))
        in_specs += [tok] * 3 + [pl.BlockSpec((1, CHUNK, BA_W), lambda c, d=d: (d, at[d](c), 0)),
                                 pl.BlockSpec((1, 1, N_GROUPS, GROWS), lambda c, d=d: (d, at[d](c), 0, 0)),
                                 pl.BlockSpec((1, HEADS, HEAD_DIM, HEAD_DIM), lambda c, d=d: (at[d](c), 0, 0, 0)),
                                 pl.BlockSpec((1, N_GROUPS, GROWS, GROWS), lambda c, d=d: (at[d](c), 0, 0, 0)),
                                 tok, tok]
        args += [q, k, v, scal, grow, states[d], tinv[d], vn[d], do]
        out_specs += [tok] * 3 + [pl.BlockSpec((CHUNK, BA_W), lambda c, d=d: (at[d](c), 0)),
                                  pl.BlockSpec((1, N_GROUPS, GROWS), lambda c, d=d: (at[d](c), 0, 0))]
    per_dir_shape = ([jax.ShapeDtypeStruct((s, hd_all), F32)] * 3
                     + [jax.ShapeDtypeStruct((s, BA_W), F32), jax.ShapeDtypeStruct((n, N_GROUPS, GROWS), F32)])
    res = _launch(
        body, carry, tuple(args), name=name, grid=(n,), in_specs=in_specs, out_specs=out_specs,
        out_shape=per_dir_shape * 2,
        scratch_shapes=[pltpu.VMEM((2, HEADS, HEAD_DIM, HEAD_DIM), F32), pltpu.VMEM((2, M_EYE, GROWS, GROWS), F32)],
        sem=("arbitrary",))
    outs, got = res if carry is not None else (res, None)
    paired = tuple((outs[i], outs[5 + i]) for i in range(5))
    return paired if carry is None else (paired, got)


def _gate_norm_fwd(o2, proj, dnw, *, name, ts=512):
    s = o2[0].shape[0]
    hd_all = HEADS * HEAD_DIM

    def body(of_ref, ob_ref, z_ref, w_ref, y_ref):
        w = w_ref[...]
        for hd in range(HEADS):
            cols = slice(hd * HEAD_DIM, (hd + 1) * HEAD_DIM)
            seg = of_ref[:, cols] + ob_ref[:, cols]
            r = lax.rsqrt(jnp.mean(seg * seg, axis=-1, keepdims=True) + EPS)
            z = z_ref[:, cols].astype(F32)
            y_ref[:, cols] = ((seg * r * w) * (z * _sigmoid(z))).astype(BF16)

    tile = pl.BlockSpec((ts, hd_all), lambda i: (i, 0))
    return pl.pallas_call(
        body, name=name, grid=(s // ts,),
        in_specs=[tile, tile, pl.BlockSpec((ts, Z_W), lambda i: (i, Z_OFF // Z_W)), _row(HEAD_DIM)],
        out_specs=tile,
        out_shape=jax.ShapeDtypeStruct((s, hd_all), BF16),
        compiler_params=_params(("parallel",)),
    )(o2[0], o2[1], proj, dnw)


def _gate_norm_bwd(dyb, o2, proj, dnw, *, name, ts=512):
    s = o2[0].shape[0]
    hd_all = HEADS * HEAD_DIM

    def body(dy_ref, of_ref, ob_ref, z_ref, w_ref, do_ref, dz_ref, dw_ref):
        w = w_ref[...]
        dw = jnp.zeros((1, HEAD_DIM), F32)
        for hd in range(HEADS):
            cols = slice(hd * HEAD_DIM, (hd + 1) * HEAD_DIM)
            seg = of_ref[:, cols] + ob_ref[:, cols]
            r = lax.rsqrt(jnp.mean(seg * seg, axis=-1, keepdims=True) + EPS)
            xhat = seg * r
            z = z_ref[:, cols].astype(F32)
            sg = _sigmoid(z)
            dy = dy_ref[:, cols]
            dnrm = dy * (z * sg)
            dz_ref[:, cols] = (dy * (xhat * w) * (sg * (1.0 + z * (1.0 - sg)))).astype(BF16)
            dw = dw + jnp.sum(dnrm * xhat, axis=0, keepdims=True)
            dxhat = dnrm * w
            do_ref[:, cols] = r * (dxhat - xhat * jnp.mean(dxhat * xhat, axis=-1, keepdims=True))

        @pl.when(pl.program_id(0) == 0)
        def _():
            dw_ref[...] = jnp.zeros_like(dw_ref)

        dw_ref[...] += dw

    tile = pl.BlockSpec((ts, hd_all), lambda i: (i, 0))
    return pl.pallas_call(
        body, name=name, grid=(s // ts,),
        in_specs=[tile, tile, tile, pl.BlockSpec((ts, Z_W), lambda i: (i, Z_OFF // Z_W)), _row(HEAD_DIM)],
        out_specs=[tile, tile, _row(HEAD_DIM)],
        out_shape=[jax.ShapeDtypeStruct((s, hd_all), F32), jax.ShapeDtypeStruct((s, hd_all), BF16),
                   jax.ShapeDtypeStruct((1, HEAD_DIM), F32)],
        compiler_params=_params(("arbitrary",)),
    )(dyb, o2[0], o2[1], proj, dnw)


def _merge_fwd(ya, yb, proj, wa, wb, wo, h, g, *, name, ts=512):
    s, d = h.shape

    def body(ya_ref, yb_ref, gt_ref, wa_ref, wb_ref, wo_ref, h_ref, g_ref, pa_ref, pb_ref, mix_ref, ho_ref):
        pa = _dot(ya_ref[...], wa_ref[...])
        pb = _dot(yb_ref[...], wb_ref[...])
        pa_ref[...] = pa.astype(BF16)
        pb_ref[...] = pb.astype(BF16)
        merged = (_sigmoid(gt_ref[:, :d].astype(F32)) * pa + _sigmoid(gt_ref[:, d:].astype(F32)) * pb)
        mix = _dot(merged.astype(BF16), wo_ref[...])
        mix_ref[...] = mix.astype(BF16)
        ho_ref[...] = h_ref[...] + g_ref[...] * mix

    tile = pl.BlockSpec((ts, d), lambda i: (i, 0))
    return pl.pallas_call(
        body, name=name, grid=(s // ts,),
        in_specs=[pl.BlockSpec((ts, CONV_A), lambda i: (i, 0)), tile,
                  pl.BlockSpec((ts, GATE_W), lambda i: (i, GATE_OFF // GATE_W)),
                  _resident(wa.shape), _resident(wb.shape), _resident(wo.shape), tile, _row(d)],
        out_specs=[tile, tile, tile, tile],
        out_shape=[jax.ShapeDtypeStruct((s, d), BF16)] * 3 + [jax.ShapeDtypeStruct((s, d), F32)],
        compiler_params=_params(("parallel",)),
    )(ya, yb, proj, wa, wb, wo, h, g)


def _merge_bwd(dh, g, mix, pa, pb, proj, wa, wb, wo, *, name, ts=256):
    s, d = dh.shape

    def body(dh_ref, g_ref, mix_ref, pa_ref, pb_ref, gt_ref, wa_ref, wb_ref, wo_ref,
             dmix_ref, mg_ref, dpa_ref, dpb_ref, dgt_ref, dya_ref, dyb_ref, dg_ref):
        dh_v = dh_ref[...]
        dmix = (g_ref[...] * dh_v).astype(BF16)
        dmix_ref[...] = dmix

        @pl.when(pl.program_id(0) == 0)
        def _():
            dg_ref[...] = jnp.zeros_like(dg_ref)

        dg_ref[...] += jnp.sum(dh_v * mix_ref[...].astype(F32), axis=0, keepdims=True)
        dmerged = _dot_nt(dmix, wo_ref[...])
        pa = pa_ref[...].astype(F32)
        pb = pb_ref[...].astype(F32)
        sa = _sigmoid(gt_ref[:, :d].astype(F32))
        sb = _sigmoid(gt_ref[:, d:].astype(F32))
        mg_ref[...] = (sa * pa + sb * pb).astype(BF16)
        dpa = (dmerged * sa).astype(BF16)
        dpb = (dmerged * sb).astype(BF16)
        dpa_ref[...] = dpa
        dpb_ref[...] = dpb
        dgt_ref[:, :d] = (dmerged * pa * sa * (1.0 - sa)).astype(BF16)
        dgt_ref[:, d:] = (dmerged * pb * sb * (1.0 - sb)).astype(BF16)
        dya_ref[...] = _dot_nt(dpa, wa_ref[...])
        dyb_ref[...] = _dot_nt(dpb, wb_ref[...])

    tile = pl.BlockSpec((ts, d), lambda i: (i, 0))
    return pl.pallas_call(
        body, name=name, grid=(s // ts,),
        in_specs=[tile, _row(d), tile, tile, tile,
                  pl.BlockSpec((ts, GATE_W), lambda i: (i, GATE_OFF // GATE_W)),
                  _resident(wa.shape), _resident(wb.shape), _resident(wo.shape)],
        out_specs=[tile, tile, tile, tile, pl.BlockSpec((ts, GATE_W), lambda i: (i, 0)),
                   pl.BlockSpec((ts, CONV_A), lambda i: (i, 0)), tile, _row(d)],
        out_shape=[jax.ShapeDtypeStruct((s, d), BF16)] * 4
                  + [jax.ShapeDtypeStruct((s, GATE_W), BF16), jax.ShapeDtypeStruct((s, CONV_A), F32),
                     jax.ShapeDtypeStruct((s, d), F32), jax.ShapeDtypeStruct((1, d), F32)],
        compiler_params=_params(("arbitrary",)),
    )(dh, g, mix, pa, pb, proj, wa, wb, wo)


def _final_fwd_bwd(h, nw, target, *, name, ts=512):
    s, d = h.shape

    def body(h_ref, nw_ref, t_ref, loss_ref, dh_ref, dnw_ref):
        x = h_ref[...]
        w = nw_ref[...]
        r = lax.rsqrt(jnp.mean(x * x, axis=-1, keepdims=True) + EPS)
        xhat = x * r
        e = xhat * w - t_ref[...]
        part = 0.5 * jnp.sum(jnp.mean(e * e, axis=-1, keepdims=True))
        dy = e * (1.0 / d)
        dxhat = dy * w
        dh_ref[...] = r * (dxhat - xhat * jnp.mean(dxhat * xhat, axis=-1, keepdims=True))

        @pl.when(pl.program_id(0) == 0)
        def _():
            loss_ref[...] = jnp.zeros_like(loss_ref)
            dnw_ref[...] = jnp.zeros_like(dnw_ref)

        loss_ref[...] += jnp.broadcast_to(part, loss_ref.shape)
        dnw_ref[...] += jnp.sum(dy * xhat, axis=0, keepdims=True)

    tile = pl.BlockSpec((ts, d), lambda i: (i, 0))
    return pl.pallas_call(
        body, name=name, grid=(s // ts,),
        in_specs=[tile, _row(d), tile],
        out_specs=[_row(128), tile, _row(d)],
        out_shape=[jax.ShapeDtypeStruct((1, 128), F32), jax.ShapeDtypeStruct((s, d), F32),
                   jax.ShapeDtypeStruct((1, d), F32)],
        compiler_params=_params(("arbitrary",)),
    )(h, nw, target)


EXT = 8


def _l2norm_heads_bwd(act, dout, scale):
    outs = []
    for hd in range(HEADS):
        cols = slice(hd * HEAD_DIM, (hd + 1) * HEAD_DIM)
        seg = act[:, cols]
        nrm = lax.rsqrt(jnp.sum(seg * seg, axis=-1, keepdims=True) + EPS)
        yhat = seg * nrm
        dsg = dout[:, cols]
        outs.append((scale * nrm) * (dsg - yhat * jnp.sum(yhat * dsg, axis=-1, keepdims=True)))
    return jnp.concatenate(outs, axis=-1)


def _conv_bwd(dq2, dk2, dv2, dya, proj, conv_dn, conv_a, *, name, ts=256, carry=None):
    s = proj.shape[0]
    hd = HEADS * HEAD_DIM
    nt = s // ts
    te = ts + 2 * EXT
    kdn, ka = conv_dn.shape[0], conv_a.shape[0]

    def body(*refs):
        (qp_ref, qc_ref, qn_ref, ap_ref, ac_ref, an_ref) = refs[0:6]
        d3 = refs[6:24]
        (yp_ref, yc_ref, yn_ref, wdn_ref, wa_ref) = refs[24:29]
        (dqkv_ref, da_ref, dwdn_ref, dwa_ref) = refs[29:33]
        xs_ref, dps_ref, xa_ref, dca_ref = refs[33:37]
        i = pl.program_id(0)
        first, last = i == 0, i == nt - 1

        @pl.when(first)
        def _():
            dwdn_ref[...] = jnp.zeros_like(dwdn_ref)
            dwa_ref[...] = jnp.zeros_like(dwa_ref)

        rowe = lax.broadcasted_iota(jnp.int32, (te, 1), 0)
        inside = ~((first & (rowe < EXT)) | (last & (rowe >= EXT + ts)))
        _fill_halo(xs_ref, qp_ref, qc_ref, qn_ref, first, last)
        wdn = wdn_ref[...]
        for part in range(3):
            cols = slice(part * hd, (part + 1) * hd)
            pre = _dwconv_rows(xs_ref, wdn[:, cols], HALO - EXT - 2, te, cols)
            sg = _sigmoid(pre)
            act = pre * sg
            pf, cf, nf, pb, cb, nb = d3[6 * part:6 * part + 6]
            dout = jnp.concatenate([pf[...] + pb[...], cf[...] + cb[...], nf[...] + nb[...]], axis=0)
            if part == 0:
                dact = _l2norm_heads_bwd(act, dout, Q_SCALE)
            elif part == 1:
                dact = _l2norm_heads_bwd(act, dout, 1.0)
            else:
                dact = dout
            dpre = jnp.where(inside, dact * (sg * (1.0 + pre * (1.0 - sg))), 0.0)
            dps_ref[:, cols] = dpre
            acc = wdn[0:1, cols] * dps_ref[EXT + 2:EXT + 2 + ts, cols]
            for tap in range(1, kdn):
                acc = acc + wdn[tap:tap + 1, cols] * dps_ref[EXT + 2 - tap:EXT + 2 - tap + ts, cols]
            dqkv_ref[:, cols] = acc.astype(BF16)
            dcur = dps_ref[EXT:EXT + ts, cols]
            for tap in range(kdn):
                dwdn_ref[tap:tap + 1, cols] += jnp.sum(
                    dcur * xs_ref[HALO - 2 + tap:HALO - 2 + tap + ts, cols], axis=0, keepdims=True)

        cv = lambda r: r[:, CONV_A:2 * CONV_A].astype(F32) * r[:, 2 * CONV_A:].astype(F32)
        _fill_halo(xa_ref, ap_ref, ac_ref, an_ref, first, last, fn=cv)
        wa = wa_ref[...]
        gate_b = jnp.concatenate([ap_ref[HALO - EXT:, 0:CONV_A], ac_ref[:, 0:CONV_A], an_ref[0:EXT, 0:CONV_A]],
                                 axis=0).astype(F32)
        dya_e = jnp.concatenate([yp_ref[...], yc_ref[...], yn_ref[...]], axis=0)
        dca_ref[...] = jnp.where(inside, dya_e * gate_b, 0.0)
        conv = _dwconv_rows(xa_ref, wa, HALO - 1, ts, slice(0, CONV_A))
        acc = wa[0:1, :] * dca_ref[EXT + 1:EXT + 1 + ts, :]
        for tap in range(1, ka):
            acc = acc + wa[tap:tap + 1, :] * dca_ref[EXT + 1 - tap:EXT + 1 - tap + ts, :]
        gc = ac_ref[:, CONV_A:2 * CONV_A].astype(F32)
        val = ac_ref[:, 2 * CONV_A:].astype(F32)
        da_ref[:, 0:CONV_A] = (yc_ref[...] * conv).astype(BF16)
        da_ref[:, CONV_A:2 * CONV_A] = (acc * val).astype(BF16)
        da_ref[:, 2 * CONV_A:] = (acc * gc).astype(BF16)
        dcur = dca_ref[EXT:EXT + ts, :]
        for tap in range(ka):
            dwa_ref[tap:tap + 1, :] += jnp.sum(
                dcur * xa_ref[HALO - 1 + tap:HALO - 1 + tap + ts, :], axis=0, keepdims=True)

    cot = [arr for pair in (dq2, dk2, dv2) for arr in pair for _ in range(3)]
    return _launch(
        body, carry, (proj, proj, proj, proj, proj, proj, *cot, dya, dya, dya, conv_dn, conv_a),
        name=name, grid=(nt,),
        in_specs=(_halo_specs(ts, QKV_W, 0, s) + _halo_specs(ts, A_W, A_OFF // A_W, s)
                  + _halo_specs(ts, hd, 0, s, rows=EXT) * 6 + _halo_specs(ts, CONV_A, 0, s, rows=EXT)
                  + [_resident(conv_dn.shape), _resident(conv_a.shape)]),
        out_specs=[pl.BlockSpec((ts, QKV_W), lambda i: (i, 0)), pl.BlockSpec((ts, A_W), lambda i: (i, 0)),
                   pl.BlockSpec((8, QKV_W), lambda i: (0, 0)), pl.BlockSpec((8, CONV_A), lambda i: (0, 0))],
        out_shape=[jax.ShapeDtypeStruct((s, QKV_W), BF16), jax.ShapeDtypeStruct((s, A_W), BF16),
                   jax.ShapeDtypeStruct((8, QKV_W), F32), jax.ShapeDtypeStruct((8, CONV_A), F32)],
        scratch_shapes=[pltpu.VMEM((ts + 2 * HALO, QKV_W), F32), pltpu.VMEM((te, QKV_W), F32),
                        pltpu.VMEM((ts + 2 * HALO, CONV_A), F32), pltpu.VMEM((te, CONV_A), F32)],
        sem=("arbitrary",))


IN_A = (0, 1536)
IN_QKV = (1536, 4608)
IN_Z = (4608, 5632)
IN_BA = 5632
IN_GATE = (5664, 7712)
IN_COLS = 7712
G_REPL = 4


def _split_w_in(w_in):
    sl = lambda ab: w_in[:, ab[0]:ab[1]]
    w_main = jnp.concatenate([sl(IN_QKV), sl(IN_Z), sl(IN_GATE), sl(IN_A)], axis=1)
    blocks = []
    for d in range(2):
        beta = w_in[:, IN_BA + 8 * d:IN_BA + 8 * d + 8]
        alpha = w_in[:, IN_BA + 16 + 8 * d:IN_BA + 24 + 8 * d]
        pad = jnp.zeros((w_in.shape[0], BA_W - 8 - 8 * G_REPL), w_in.dtype)
        blocks += [beta] + [alpha] * G_REPL + [pad]
    return w_main, jnp.concatenate(blocks, axis=1)


def _merge_dw_in(dw_qkv, dw_z, dw_gate, dw_a, dw_ba):
    ba = [dw_ba[:, 0:8], dw_ba[:, BA_W:BA_W + 8], dw_ba[:, 8:16], dw_ba[:, BA_W + 8:BA_W + 16]]
    return jnp.concatenate([dw_a, dw_qkv, dw_z] + ba + [dw_gate], axis=1)


def _decay_rows(a_log_fwd, dt_bias_fwd, a_log_bwd, dt_bias_bwd):
    def rows(f, b):
        out = []
        for vec in (f, b):
            vec = vec.reshape(HEADS)
            out.append(jnp.concatenate([jnp.zeros((8,), F32)] + [vec] * G_REPL
                                       + [jnp.zeros((BA_W - 8 - 8 * G_REPL,), F32)])[None])
        return jnp.stack(out)
    return rows(a_log_fwd, a_log_bwd), rows(dt_bias_fwd, dt_bias_bwd)


def _local_step(x, target, mod9, wt, comm):
    s, d = x.shape
    n = s // CHUNK
    wt = dict(wt)
    sh1, sc1, g1, sh2, sc2, g2, sh3, sc3, g3 = [mod9[i:i + 1] for i in range(9)]
    alog, dtb = _decay_rows(wt["a_log_fwd"], wt["dt_bias_fwd"], wt["a_log_bwd"], wt["dt_bias_bwd"])

    (u1, a1, b1, f1), got = comm.gather(
        ["w_ffn1_down", "w_in"],
        lambda c: _ffn_up_fwd(x, wt["norm_ffn1"], sc1, sh1, wt["w_ffn1_up"], name="ffn1_up", carry=c))
    wt.update(got)
    w_main, w_ba = _split_w_in(wt["w_in"])
    y1, h1 = _ffn_down_fwd(f1, wt["w_ffn1_down"], x, g1, name="ffn1_down")
    (u2, proj, ba), got = comm.gather(
        ["w_a_out", "w_b_out", "w_out", "w_ffn2_up", "w_ffn2_down"],
        lambda c: _in_proj_fwd(h1, wt["norm_mix"], sc2, sh2, w_main, w_ba, name="in_proj", carry=c))
    wt.update(got)
    scal = _scal_fwd(ba, alog, dtb, name="scal_fwd")
    grow = scal[:, :, L_G:L_G + 8].reshape(2, n, CHUNK, HEADS).transpose(0, 1, 3, 2).reshape(
        2, n, N_GROUPS, GROWS)
    q, k, v, ya = _conv_fwd(proj, wt["conv_dn"], wt["conv_a"], name="conv_fwd")
    o2, states, tinv, vn = _delta_fwd(q, k, v, scal, grow, name="delta_fwd")
    yb = _gate_norm_fwd(o2, proj, wt["dn_norm"], name="gate_norm_fwd")
    pa, pb, mix, h2 = _merge_fwd(ya, yb, proj, wt["w_a_out"], wt["w_b_out"], wt["w_out"], h1, g2,
                                 name="merge_fwd")
    u3, a3, b3, f3 = _ffn_up_fwd(h2, wt["norm_ffn2"], sc3, sh3, wt["w_ffn2_up"], name="ffn2_up")
    y3, h3 = _ffn_down_fwd(f3, wt["w_ffn2_down"], h2, g3, name="ffn2_down")
    loss, dh3, dnorm_final = _final_fwd_bwd(h3, wt["norm_final"], target, name="final")

    dy3, dab3, dg3 = _ffn_bwd_act(dh3, g3, y3, a3, b3, wt["w_ffn2_down"], name="ffn2_bwd_act")
    dh2, dn3, dsc3, dsh3 = _norm_mod_matmul_bwd([(dab3, wt["w_ffn2_up"])], h2, wt["norm_ffn2"], sc3, dh3,
                                                name="ffn2_bwd_up")
    gw = {}
    gw["w_ffn2_up"] = _matmul_tn(u3, dab3, name="dw_ffn2_up", tm=1024, tn=1408)
    gw["w_ffn2_down"] = _matmul_tn(f3, dy3, name="dw_ffn2_down", tm=1408, tn=1024)

    dmix, merged, dpa, dpb, dgates, dya, dyb, dg2 = _merge_bwd(
        dh2, g2, mix, pa, pb, proj, wt["w_a_out"], wt["w_b_out"], wt["w_out"], name="merge_bwd")
    gw["w_out"] = _matmul_tn(merged, dmix, name="dw_out", tm=1024, tn=1024)
    gw["w_a_out"] = _matmul_tn(ya, dpa, name="dw_a_out", tm=512, tn=1024)
    gw["w_b_out"] = _matmul_tn(yb, dpb, name="dw_b_out", tm=1024, tn=1024)
    do, dz, ddn = _gate_norm_bwd(dyb, o2, proj, wt["dn_norm"], name="gate_norm_bwd")
    recv = {}
    (dq2, dk2, dv2, dscal, drow), got = comm.scatter(
        {nm: gw.pop(nm) for nm in ("w_ffn2_up", "w_ffn2_down")},
        lambda c: _delta_bwd(q, k, v, scal, grow, states, tinv, vn, do, name="delta_bwd", carry=c))
    recv.update(got)
    drow_p = jnp.pad(jnp.stack(drow).reshape(2, n, HEADS, CHUNK).transpose(0, 1, 3, 2).reshape(2, s, HEADS),
                     ((0, 0), (0, 0), (L_G, BA_W - L_G - HEADS)))
    dba, dalog, ddtb = _scal_bwd(jnp.stack(dscal), drow_p, ba, alog, dtb, name="scal_bwd")
    (dqkv, dbr_a, dconv_dn, dconv_a), got = comm.scatter(
        {nm: gw.pop(nm) for nm in ("w_out", "w_a_out", "w_b_out")},
        lambda c: _conv_bwd(dq2, dk2, dv2, dya, proj, wt["conv_dn"], wt["conv_a"], name="conv_bwd", carry=c))
    recv.update(got)
    dw_in = _merge_dw_in(
        _matmul_tn(u2, dqkv, name="dw_in_qkv", tm=1024, tn=1536),
        _matmul_tn(u2, dz, name="dw_in_z", tm=1024, tn=1024),
        _matmul_tn(u2, dgates, name="dw_in_gate", tm=1024, tn=1024),
        _matmul_tn(u2, dbr_a, name="dw_in_a", tm=1024, tn=1536),
        _matmul_tn(u2, dba, name="dw_in_ba", tm=1024, tn=2 * BA_W))
    (dh1, dn2, dsc2, dsh2), got = comm.scatter(
        {"w_in": dw_in},
        lambda c: _norm_mod_matmul_bwd(
            [(dqkv, w_main, 0), (dz, w_main, Z_OFF // Z_W), (dgates, w_main, GATE_OFF // GATE_W),
             (dbr_a, w_main, A_OFF // A_W), (dba, w_ba)],
            h1, wt["norm_mix"], sc2, dh2, name="in_proj_bwd", carry=c))
    recv.update(got)

    dy1, dab1, dg1 = _ffn_bwd_act(dh1, g1, y1, a1, b1, wt["w_ffn1_down"], name="ffn1_bwd_act")
    dw_down1 = _matmul_tn(f1, dy1, name="dw_ffn1_down", tm=1408, tn=1024)
    dw_up1, got = comm.scatter(
        {"w_ffn1_down": dw_down1},
        lambda c: _matmul_tn(u1, dab1, name="dw_ffn1_up", tm=1024, tn=1408, carry=c))
    recv.update(got)
    (dx, dn1, dsc1, dsh1), got = comm.scatter(
        {"w_ffn1_up": dw_up1},
        lambda c: _norm_mod_matmul_bwd([(dab1, wt["w_ffn1_up"])], x, wt["norm_ffn1"], sc1, dh1,
                                       name="ffn1_bwd_up", carry=c))
    recv.update(got)

    small = {
        "mod": jnp.concatenate([dsh1, dsc1, dg1, dsh2, dsc2, dg2, dsh3, dsc3, dg3], axis=1),
        "norm_ffn1": dn1, "norm_mix": dn2, "norm_ffn2": dn3, "norm_final": dnorm_final,
        "a_log_fwd": dalog[0, :, L_G:L_G + 8], "dt_bias_fwd": ddtb[0, :, L_G:L_G + 8],
        "a_log_bwd": dalog[1, :, L_G:L_G + 8], "dt_bias_bwd": ddtb[1, :, L_G:L_G + 8],
        "dn_norm": ddn,
        "conv_a": dconv_a[0:3].reshape(1, -1), "conv_dn": dconv_dn[0:5].reshape(1, -1),
    }
    return loss, dx, recv, small


def _full_weight(name, g):
    if name in COL_SHARDED + CONV_SHARDED:
        return g.transpose(1, 0, 2).reshape(g.shape[1], -1)
    return g.reshape(-1, g.shape[-1])


def _grad_pieces(name, g):
    g = g.astype(BF16)
    if name in COL_SHARDED:
        return g.reshape(g.shape[0], N_DEV, -1).transpose(1, 0, 2)
    return g.reshape(N_DEV, -1, g.shape[-1])


class _MeshComm:
    def __init__(self, shards):
        self.shards = shards

    def _run(self, xs, carrier, name, gather):
        if carrier is None:
            return None, _exchange(xs, name=name, gather=gather)
        return carrier((xs, gather))

    def gather(self, names, carrier=None, name=None):
        outs, got = self._run([self.shards[nm] for nm in names], carrier, name, True)
        return outs, {nm: _full_weight(nm, g) for nm, g in zip(names, got)}

    def scatter(self, grads, carrier=None, name=None):
        names = list(grads)
        outs, got = self._run([_grad_pieces(nm, grads[nm]) for nm in names], carrier, name, False)
        return outs, dict(zip(names, got))


def _mod_fwd(c_all, w_ada, *, name):
    def body(c_ref, w_ref, o_ref):
        cv = c_ref[...]
        o_ref[...] = _dot3(cv * _sigmoid(cv), w_ref[...])

    return pl.pallas_call(
        body, name=name, out_shape=jax.ShapeDtypeStruct((c_all.shape[0], w_ada.shape[1]), F32),
        compiler_params=_params(),
    )(c_all, w_ada)


def _adamw_math(w, g, m, v):
    m_new = ADAM_B1 * m + (1.0 - ADAM_B1) * g
    v_new = ADAM_B2 * v + (1.0 - ADAM_B2) * (g * g)
    m_hat = m_new / (1.0 - ADAM_B1 ** ADAM_STEP)
    v_hat = v_new / (1.0 - ADAM_B2 ** ADAM_STEP)
    delta = -ADAM_LR * (m_hat / (jnp.sqrt(v_hat) + ADAM_EPS) + ADAM_WD * w)
    return delta, m_new, v_new


def _reduce_adamw(pieces, w, m, v, *, name, tr):
    r, c = w.shape

    def body(p_ref, w_ref, m_ref, v_ref, g_ref, d_ref, mo_ref, vo_ref):
        g = p_ref[0].astype(F32)
        for src in range(1, N_DEV):
            g = g + p_ref[src].astype(F32)
        g_ref[...] = g
        d_ref[...], mo_ref[...], vo_ref[...] = _adamw_math(w_ref[...], g, m_ref[...], v_ref[...])

    tile = pl.BlockSpec((tr, c), lambda i: (i, 0))
    return pl.pallas_call(
        body, name=name, grid=(r // tr,),
        in_specs=[pl.BlockSpec((N_DEV, tr, c), lambda i: (0, i, 0)), tile, tile, tile],
        out_specs=[tile] * 4, out_shape=[jax.ShapeDtypeStruct((r, c), F32)] * 4,
        compiler_params=_params(("parallel",)),
    )(pieces, w, m, v)


def _ada_grad_adamw(c_all_t, dmod_cols, w, m, v, *, name, tr=256):
    r, c = w.shape

    def body(c_ref, dm_ref, w_ref, m_ref, v_ref, g_ref, d_ref, mo_ref, vo_ref):
        cv = c_ref[...]
        act = cv * _sigmoid(cv)
        dm = dm_ref[...]
        g = act[:, 0:1] * dm[0:1, :]
        for b in range(1, N_DEV):
            g = g + act[:, b:b + 1] * dm[b:b + 1, :]
        g_ref[...] = g
        d_ref[...], mo_ref[...], vo_ref[...] = _adamw_math(w_ref[...], g, m_ref[...], v_ref[...])

    tile = pl.BlockSpec((tr, c), lambda i: (i, 0))
    return pl.pallas_call(
        body, name=name, grid=(r // tr,),
        in_specs=[pl.BlockSpec((tr, N_DEV), lambda i: (i, 0)), pl.BlockSpec((N_DEV, c), lambda i: (0, 0)),
                  tile, tile, tile],
        out_specs=[tile] * 4, out_shape=[jax.ShapeDtypeStruct((r, c), F32)] * 4,
        compiler_params=_params(("parallel",)),
    )(c_all_t, dmod_cols, w, m, v)


def _sum_rows(parts, *, name):
    def body(p_ref, o_ref):
        acc = p_ref[0:1, :]
        for src in range(1, N_DEV):
            acc = acc + p_ref[src:src + 1, :]
        o_ref[...] = acc

    return pl.pallas_call(
        body, name=name, out_shape=jax.ShapeDtypeStruct((1, parts.shape[1]), F32), compiler_params=_params(),
    )(parts)


def _adamw_rows(g, w, m, v, *, name):
    def body(g_ref, w_ref, m_ref, v_ref, d_ref, mo_ref, vo_ref):
        d_ref[...], mo_ref[...], vo_ref[...] = _adamw_math(w_ref[...], g_ref[...], m_ref[...], v_ref[...])

    return pl.pallas_call(
        body, name=name, out_shape=[jax.ShapeDtypeStruct(g.shape, F32)] * 3, compiler_params=_params(),
    )(g, w, m, v)


WEIGHTS = ["w_ada", "b_ada", "norm_ffn1", "w_ffn1_up", "w_ffn1_down", "norm_mix", "w_in", "conv_a", "conv_dn",
           "a_log_fwd", "dt_bias_fwd", "a_log_bwd", "dt_bias_bwd", "dn_norm", "w_a_out", "w_b_out", "w_out",
           "norm_ffn2", "w_ffn2_up", "w_ffn2_down", "norm_final"]
COL_SHARDED = ["w_ffn1_up", "w_in", "w_a_out", "w_ffn2_up"]
ROW_SHARDED = ["w_ffn1_down", "w_b_out", "w_out", "w_ffn2_down"]
CONV_SHARDED = ["conv_a", "conv_dn"]
REPLICATED = ["b_ada", "norm_ffn1", "norm_mix", "a_log_fwd", "dt_bias_fwd", "a_log_bwd", "dt_bias_bwd",
              "dn_norm", "norm_ffn2", "norm_final"]
SMALL_ORDER = ["mod", "norm_ffn1", "norm_mix", "norm_ffn2", "norm_final", "a_log_fwd", "dt_bias_fwd",
               "a_log_bwd", "dt_bias_bwd", "dn_norm", "conv_a", "conv_dn"]
REDUCE_ROWS = {"w_ffn1_up": 256, "w_in": 256, "w_a_out": 256, "w_ffn2_up": 256,
               "w_ffn1_down": 176, "w_b_out": 128, "w_out": 128, "w_ffn2_down": 176}


def _pad_lanes(row):
    pad = (-row.shape[1]) % 128
    return jnp.pad(row, ((0, 0), (0, pad)))


def kernel(x, c, w_ada, b_ada, norm_ffn1, w_ffn1_up, w_ffn1_down, norm_mix, w_in, conv_a, conv_dn, a_log_fwd, dt_bias_fwd, a_log_bwd, dt_bias_bwd, dn_norm, w_a_out, w_b_out, w_out, norm_ffn2, w_ffn2_up, w_ffn2_down, norm_final, loss_target, m_w_ada, m_b_ada, m_norm_ffn1, m_w_ffn1_up, m_w_ffn1_down, m_norm_mix, m_w_in, m_conv_a, m_conv_dn, m_a_log_fwd, m_dt_bias_fwd, m_a_log_bwd, m_dt_bias_bwd, m_dn_norm, m_w_a_out, m_w_b_out, m_w_out, m_norm_ffn2, m_w_ffn2_up, m_w_ffn2_down, m_norm_final, v_w_ada, v_b_ada, v_norm_ffn1, v_w_ffn1_up, v_w_ffn1_down, v_norm_mix, v_w_in, v_conv_a, v_conv_dn, v_a_log_fwd, v_dt_bias_fwd, v_a_log_bwd, v_dt_bias_bwd, v_dn_norm, v_w_a_out, v_w_b_out, v_w_out, v_norm_ffn2, v_w_ffn2_up, v_w_ffn2_down, v_norm_final):
    args = dict(locals())
    w_loc = {n: args[n] for n in WEIGHTS}
    m_loc = {n: args["m_" + n] for n in WEIGHTS}
    v_loc = {n: args["v_" + n] for n in WEIGHTS}
    me = _flat_index(_my_position())
    d_model = x.shape[-1]

    big = COL_SHARDED + ROW_SHARDED
    shards = {n: w_loc[n][0].astype(BF16) for n in big}
    shards.update({n: w_loc[n][0] for n in CONV_SHARDED})
    shards["c"] = c
    comm = _MeshComm(shards)
    wt = comm.gather(["c", "conv_a", "conv_dn", "w_ffn1_up"], name="gather_first")[1]
    c_all = wt.pop("c")
    for n in REPLICATED[1:]:
        wt[n] = w_loc[n].reshape(1, -1)

    mod_cols = _mod_fwd(c_all, w_ada[0], name="mod_fwd")
    mod_all = _exchange([mod_cols], name="gather_mod", gather=True)[0]
    mod_mine = lax.dynamic_index_in_dim(mod_all, me, axis=1, keepdims=False).reshape(1, -1) + b_ada
    mod9 = mod_mine.reshape(9, d_model)

    loss_loc, dx, recv, small = _local_step(x[0], loss_target[0], mod9, wt, comm)
    loss = lax.psum(loss_loc[0, 0], MESH_AXES)

    res = {}
    for n in big:
        res[n] = _reduce_adamw(recv[n], w_loc[n][0], m_loc[n][0], v_loc[n][0], name="adamw_" + n,
                               tr=REDUCE_ROWS[n])

    packed = _pad_lanes(jnp.concatenate([small[n].reshape(1, -1) for n in SMALL_ORDER], axis=1))
    parts = _exchange([packed], name="gather_small", gather=True)[0].reshape(N_DEV, -1)
    total = _sum_rows(parts, name="sum_small")
    off = 0
    gsmall = {}
    for n in SMALL_ORDER:
        size = small[n].size
        gsmall[n] = total[:, off:off + size]
        off += size
    dmod_all = parts[:, 0:9 * d_model]
    ada_cols = w_ada.shape[-1]
    dmod_cols = lax.dynamic_slice_in_dim(dmod_all, me * ada_cols, ada_cols, axis=1)
    res["w_ada"] = _ada_grad_adamw(c_all.T, dmod_cols, w_ada[0], m_w_ada[0], v_w_ada[0], name="adamw_w_ada")
    g_rows = {"b_ada": gsmall["mod"]}
    for n in REPLICATED[1:]:
        g_rows[n] = gsmall[n]
    for n in CONV_SHARDED:
        taps, width = w_loc[n].shape[1], w_loc[n].shape[2]
        full = gsmall[n].reshape(taps, -1)
        g_rows[n] = lax.dynamic_slice_in_dim(full, me * width, width, axis=1).reshape(1, -1)
    row_names = REPLICATED + CONV_SHARDED
    cat = lambda src: _pad_lanes(jnp.concatenate([src[n].reshape(1, -1) for n in row_names], axis=1))
    g_cat = cat(g_rows)
    d_cat, m_cat, v_cat = _adamw_rows(g_cat, cat(w_loc), cat(m_loc), cat(v_loc), name="adamw_small")
    off = 0
    for n in row_names:
        size = w_loc[n].size
        res[n] = tuple(t[:, off:off + size] for t in (g_cat, d_cat, m_cat, v_cat))
        off += size

    outs = [loss, dx[None]]
    for kind in range(4):
        for n in WEIGHTS:
            outs.append(res[n][kind].reshape(w_loc[n].shape))
    return tuple(outs)
```

```python
import functools
import math
import types

import jax
import jax.numpy as jnp
from jax import lax
from jax.experimental import pallas as pl
from jax.experimental.pallas import tpu as pltpu

F32 = jnp.float32
BF16 = jnp.bfloat16
EPS = 1e-6
N_DEV = 8
CHUNK = 64
HEADS = 8
HEAD_DIM = 128
MESH_AXES = ("x", "y", "c")
VMEM_LIMIT_BYTES = 56 * 1024 * 1024

ADAM_LR = 0.001
ADAM_B1 = 0.9
ADAM_B2 = 0.999
ADAM_EPS = 1e-08
ADAM_WD = 0.01
ADAM_STEP = 10


def _params(sem=None):
    return pltpu.CompilerParams(dimension_semantics=sem, vmem_limit_bytes=VMEM_LIMIT_BYTES)


def _row(n):
    return pl.BlockSpec((1, n), lambda *_: (0, 0))


def _resident(shape):
    nd = len(shape)
    return pl.BlockSpec(shape, lambda *_: (0,) * nd, pipeline_mode=pl.Buffered(1))


def _col_chunks(width, chunk=512):
    return [slice(lo, min(lo + chunk, width)) for lo in range(0, width, chunk)]


def _col_window(w, width, col_block):
    return pl.BlockSpec((w.shape[0], width), lambda *_: (0, col_block), pipeline_mode=pl.Buffered(1))


def _sigmoid(x):
    return 1.0 / (1.0 + jnp.exp(-x))


def _dot(a, b):
    return jnp.dot(a, b, preferred_element_type=F32)


def _dot_nt(a, b):
    return lax.dot_general(a, b, (((1,), (1,)), ((), ())), preferred_element_type=F32)


def _dot_tn(a, b):
    return lax.dot_general(a, b, (((0,), (0,)), ((), ())), preferred_element_type=F32)


def _split_bf16(x):
    hi = x.astype(BF16)
    lo = (x - hi.astype(F32)).astype(BF16)
    return hi, lo


def _dot3(a, b, dot=_dot):
    ah, al = a if isinstance(a, tuple) else _split_bf16(a)
    bh, bl = b if isinstance(b, tuple) else _split_bf16(b)
    return dot(ah, bh) + dot(ah, bl) + dot(al, bh)


def _dot_exact(a, b):
    return jnp.dot(a, b, preferred_element_type=F32, precision=lax.Precision.HIGHEST)


def _my_position():
    return tuple(lax.axis_index(a) for a in MESH_AXES)


def _peer(pos, kk):
    return tuple((1 - p) if (kk >> (2 - b)) & 1 else p for b, p in enumerate(pos))


def _flat_index(pos):
    return pos[0] * 4 + pos[1] * 2 + pos[2]


_ANY = pl.BlockSpec(memory_space=pl.ANY)


class _AllToAll:
    def __init__(self, in_refs, out_refs, send_sems, recv_sems, local_sems):
        pos = _my_position()
        me = _flat_index(pos)
        self.copies = []
        for t in range(len(in_refs)):
            self.copies.append(pltpu.make_async_copy(in_refs[t].at[me], out_refs[t].at[me], local_sems.at[t]))
            for kk in range(1, N_DEV):
                peer = _peer(pos, kk)
                self.copies.append(pltpu.make_async_remote_copy(
                    src_ref=in_refs[t].at[_flat_index(peer)], dst_ref=out_refs[t].at[me],
                    send_sem=send_sems.at[t, kk - 1], recv_sem=recv_sems.at[t, kk - 1],
                    device_id=peer, device_id_type=pl.DeviceIdType.MESH))

    def start(self):
        for cp in self.copies:
            cp.start()

    def finish(self):
        for cp in self.copies:
            cp.wait()


class _AllGather:
    def __init__(self, in_refs, out_refs, send_sems, recv_sems, local_sems):
        self.refs = (in_refs, out_refs, send_sems, recv_sems, local_sems)
        x, y, c = _my_position()
        self.me, self.sibling = (x, y, c), (x, y, 1 - c)
        self.chips = [(1 - x, y), (x, 1 - y), (1 - x, 1 - y)]
        self.core = c

    def _copy(self, t, k, block, to, own=False):
        in_refs, out_refs, send_sems, recv_sems, _ = self.refs
        rows = out_refs[t].at[_flat_index(block)]
        return pltpu.make_async_remote_copy(
            src_ref=in_refs[t] if own else rows, dst_ref=rows,
            send_sem=send_sems.at[t, k], recv_sem=recv_sems.at[t, k],
            device_id=to, device_id_type=pl.DeviceIdType.MESH)

    def _local(self, t):
        in_refs, out_refs, _, _, local_sems = self.refs
        return pltpu.make_async_copy(in_refs[t], out_refs[t].at[_flat_index(self.me)], local_sems.at[t])

    def start(self):
        c = self.core
        for t in range(len(self.refs[0])):
            self._local(t).start()
            self._copy(t, 0, self.me, self.sibling, own=True).start()
            for j, chip in enumerate(self.chips):
                self._copy(t, 1 + j, self.me, (*chip, c), own=True).start()

    def finish(self):
        c = self.core
        n_t = len(self.refs[0])
        for t in range(n_t):
            for j, chip in enumerate(self.chips):
                self._copy(t, 1 + j, (*chip, c), self.me).wait_recv()
                self._copy(t, 4 + j, (*chip, c), self.sibling).start()
        for t in range(n_t):
            self._copy(t, 0, self.sibling, self.me).wait_recv()
            for j, chip in enumerate(self.chips):
                self._copy(t, 4 + j, (*chip, 1 - c), self.me).wait_recv()
            self._copy(t, 0, self.me, self.sibling, own=True).wait_send()
            for j, chip in enumerate(self.chips):
                self._copy(t, 1 + j, self.me, (*chip, c), own=True).wait_send()
                self._copy(t, 4 + j, (*chip, c), self.sibling).wait_send()
            self._local(t).wait()


def _exchange_plan(in_refs, out_refs, send_sems, recv_sems, local_sems, gather):
    return (_AllGather if gather else _AllToAll)(in_refs, out_refs, send_sems, recv_sems, local_sems)


def _exchange_shapes(xs, gather):
    out_shape = [jax.ShapeDtypeStruct(((N_DEV,) + x.shape) if gather else x.shape, x.dtype) for x in xs]
    sems = [pltpu.SemaphoreType.DMA((len(xs), N_DEV - 1)), pltpu.SemaphoreType.DMA((len(xs), N_DEV - 1)),
            pltpu.SemaphoreType.DMA((len(xs),))]
    return out_shape, sems


def _exchange(xs, *, name, gather):
    nt = len(xs)

    def body(*refs):
        plan = _exchange_plan(refs[:nt], refs[nt:2 * nt], *refs[2 * nt:], gather)
        plan.start()
        plan.finish()

    out_shape, sems = _exchange_shapes(xs, gather)
    return pl.pallas_call(body, name=name, in_specs=[_ANY] * nt, out_specs=[_ANY] * nt, out_shape=out_shape,
                          scratch_shapes=sems)(*xs)


def _launch(body, carry, args, *, name, grid, in_specs, out_specs, out_shape, scratch_shapes=(), sem):
    single = not isinstance(out_shape, (list, tuple))
    out_specs = [out_specs] if single else list(out_specs)
    out_shape = [out_shape] if single else list(out_shape)
    if carry is None:
        outs = pl.pallas_call(body, name=name, grid=grid, in_specs=list(in_specs), out_specs=out_specs,
                              out_shape=out_shape, scratch_shapes=list(scratch_shapes),
                              compiler_params=_params(sem))(*args)
        return outs[0] if single else outs
    xs, gather = carry
    nt, n_in, n_out, n_scr = len(xs), len(args), len(out_shape), len(scratch_shapes)
    x_shape, sems = _exchange_shapes(xs, gather)

    def wrapped(*refs):
        c_in, x_in = refs[:n_in], refs[n_in:n_in + nt]
        c_out = refs[n_in + nt:n_in + nt + n_out]
        x_out = refs[n_in + nt + n_out:n_in + 2 * nt + n_out]
        scr = refs[n_in + 2 * nt + n_out:]
        ids = [pl.program_id(a) for a in range(len(grid))]
        first = functools.reduce(jnp.logical_and, [i == 0 for i in ids])
        last = functools.reduce(jnp.logical_and, [i == g - 1 for i, g in zip(ids, grid)])
        plan = lambda: _exchange_plan(x_in, x_out, *scr[n_scr:], gather)

        @pl.when(first)
        def _():
            plan().start()

        body(*c_in, *c_out, *scr[:n_scr])

        @pl.when(last)
        def _():
            plan().finish()

    outs = pl.pallas_call(
        wrapped, name=name, grid=grid, in_specs=list(in_specs) + [_ANY] * nt,
        out_specs=out_specs + [_ANY] * nt, out_shape=out_shape + x_shape,
        scratch_shapes=list(scratch_shapes) + sems,
        compiler_params=_params(("arbitrary",) * len(grid)))(*args, *xs)
    compute = outs[:n_out]
    return (compute[0] if single else compute), outs[n_out:]


def _norm_mod(x, nw, sc, sh):
    r = lax.rsqrt(jnp.mean(x * x, axis=-1, keepdims=True) + EPS)
    return (x * r * nw) * (1.0 + sc) + sh


def _norm_mod_bwd(x, nw, sc, du):
    r = lax.rsqrt(jnp.mean(x * x, axis=-1, keepdims=True) + EPS)
    xhat = x * r
    n = xhat * nw
    dsh = jnp.sum(du, axis=0, keepdims=True)
    dsc = jnp.sum(du * n, axis=0, keepdims=True)
    dn = du * (1.0 + sc)
    dnw = jnp.sum(dn * xhat, axis=0, keepdims=True)
    dxhat = dn * nw
    dx = r * (dxhat - xhat * jnp.mean(dxhat * xhat, axis=-1, keepdims=True))
    return dx, dnw, dsc, dsh


def _ffn_up_fwd(h, nw, sc, sh, wup, *, name, ts=1024, tn=1408, carry=None):
    s, d = h.shape
    ts = min(ts, s)
    f_dim = wup.shape[1] // 2
    nj = f_dim // tn

    def body(h_ref, nw_ref, sc_ref, sh_ref, wa_ref, wb_ref, u_ref, a_ref, b_ref, f_ref):
        @pl.when(pl.program_id(1) == 0)
        def _():
            u_ref[...] = _norm_mod(h_ref[...], nw_ref[...], sc_ref[...], sh_ref[...]).astype(BF16)

        u = u_ref[...]

        def epilogue(a, b, cs):
            a_ref[:, cs] = a.astype(BF16)
            b_ref[:, cs] = b.astype(BF16)
            f_ref[:, cs] = (a * _sigmoid(a) * b).astype(BF16)

        pending = None
        for cs in _col_chunks(tn):
            products = (_dot(u, wa_ref[:, cs]), _dot(u, wb_ref[:, cs]), cs)
            if pending is not None:
                epilogue(*pending)
            pending = products
        epilogue(*pending)

    return _launch(
        body, carry, (h, nw, sc, sh, wup, wup), name=name, grid=(s // ts, nj),
        in_specs=[pl.BlockSpec((ts, d), lambda i, j: (i, 0)), _row(d), _row(d), _row(d),
                  pl.BlockSpec((d, tn), lambda i, j: (0, j)),
                  pl.BlockSpec((d, tn), lambda i, j: (0, j + nj))],
        out_specs=[pl.BlockSpec((ts, d), lambda i, j: (i, 0)),
                   pl.BlockSpec((ts, tn), lambda i, j: (i, j)),
                   pl.BlockSpec((ts, tn), lambda i, j: (i, j)),
                   pl.BlockSpec((ts, tn), lambda i, j: (i, j))],
        out_shape=[jax.ShapeDtypeStruct((s, d), BF16)] + [jax.ShapeDtypeStruct((s, f_dim), BF16)] * 3,
        sem=("parallel", "arbitrary"))


def _ffn_down_fwd(f, wd, h, g, *, name, ts=512):
    s, f_dim = f.shape
    d = wd.shape[1]

    def body(f_ref, wd_ref, h_ref, g_ref, y_ref, ho_ref):
        y = _dot(f_ref[...], wd_ref[...])
        y_ref[...] = y.astype(BF16)
        ho_ref[...] = h_ref[...] + (0.5 * g_ref[...]) * y

    return pl.pallas_call(
        body, name=name, grid=(s // ts,),
        in_specs=[pl.BlockSpec((ts, f_dim), lambda i: (i, 0)), _resident((f_dim, d)),
                  pl.BlockSpec((ts, d), lambda i: (i, 0)), _row(d)],
        out_specs=[pl.BlockSpec((ts, d), lambda i: (i, 0)), pl.BlockSpec((ts, d), lambda i: (i, 0))],
        out_shape=[jax.ShapeDtypeStruct((s, d), BF16), jax.ShapeDtypeStruct((s, d), F32)],
        compiler_params=_params(("parallel",)),
    )(f, wd, h, g)


def _ffn_bwd_act(dh, g, y, a, b, wd, *, name, ts=256, carry=None):
    s, d = dh.shape
    f_dim = a.shape[1]

    def body(dh_ref, g_ref, y_ref, a_ref, b_ref, wd_ref, dy_ref, dab_ref, dg_ref):
        dh_v = dh_ref[...]
        dy = ((0.5 * g_ref[...]) * dh_v).astype(BF16)
        dy_ref[...] = dy
        part = jnp.sum(0.5 * dh_v * y_ref[...].astype(F32), axis=0, keepdims=True)

        @pl.when(pl.program_id(0) == 0)
        def _():
            dg_ref[...] = jnp.zeros_like(dg_ref)

        dg_ref[...] += part

        def epilogue(df, cs):
            av = a_ref[:, cs].astype(F32)
            bv = b_ref[:, cs].astype(F32)
            sg = _sigmoid(av)
            dab_ref[:, cs] = (df * bv * (sg * (1.0 + av * (1.0 - sg)))).astype(BF16)
            dab_ref[:, slice(f_dim + cs.start, f_dim + cs.stop)] = (df * (av * sg)).astype(BF16)

        pending = None
        for cs in _col_chunks(f_dim):
            product = (_dot_nt(dy, wd_ref[cs, :]), cs)
            if pending is not None:
                epilogue(*pending)
            pending = product
        epilogue(*pending)

    return _launch(
        body, carry, (dh, g, y, a, b, wd), name=name, grid=(s // ts,),
        in_specs=[pl.BlockSpec((ts, d), lambda i: (i, 0)), _row(d),
                  pl.BlockSpec((ts, d), lambda i: (i, 0)),
                  pl.BlockSpec((ts, f_dim), lambda i: (i, 0)),
                  pl.BlockSpec((ts, f_dim), lambda i: (i, 0)),
                  _resident((f_dim, d))],
        out_specs=[pl.BlockSpec((ts, d), lambda i: (i, 0)),
                   pl.BlockSpec((ts, 2 * f_dim), lambda i: (i, 0)), _row(d)],
        out_shape=[jax.ShapeDtypeStruct((s, d), BF16), jax.ShapeDtypeStruct((s, 2 * f_dim), BF16),
                   jax.ShapeDtypeStruct((1, d), F32)],
        sem=("arbitrary",))


def _norm_mod_matmul_bwd(pairs, h, nw, sc, dh_in, *, name, ts=256, carry=None):
    s, d = h.shape
    n_pairs = len(pairs)

    def body(*refs):
        dx_refs = refs[:n_pairs]
        w_refs = refs[n_pairs:2 * n_pairs]
        h_ref, nw_ref, sc_ref, dhi_ref, dho_ref, dnw_ref, dsc_ref, dsh_ref = refs[2 * n_pairs:]
        du = _dot_nt(dx_refs[0][...], w_refs[0][...])
        for k in range(1, n_pairs):
            du = du + _dot_nt(dx_refs[k][...], w_refs[k][...])
        dx, dnw, dsc, dsh = _norm_mod_bwd(h_ref[...], nw_ref[...], sc_ref[...], du)
        dho_ref[...] = dhi_ref[...] + dx

        @pl.when(pl.program_id(0) == 0)
        def _():
            dnw_ref[...] = jnp.zeros_like(dnw_ref)
            dsc_ref[...] = jnp.zeros_like(dsc_ref)
            dsh_ref[...] = jnp.zeros_like(dsh_ref)

        dnw_ref[...] += dnw
        dsc_ref[...] += dsc
        dsh_ref[...] += dsh

    dxs = [p[0] for p in pairs]
    ws = [p[1] for p in pairs]
    tile = pl.BlockSpec((ts, d), lambda i: (i, 0))
    return _launch(
        body, carry, (*dxs, *ws, h, nw, sc, dh_in), name=name, grid=(s // ts,),
        in_specs=([pl.BlockSpec((ts, x.shape[1]), lambda i: (i, 0)) for x in dxs]
                  + [_col_window(w, x.shape[1], p[2] if len(p) > 2 else 0) for p, x, w in zip(pairs, dxs, ws)]
                  + [tile, _row(d), _row(d), tile]),
        out_specs=[tile, _row(d), _row(d), _row(d)],
        out_shape=[jax.ShapeDtypeStruct((s, d), F32)] + [jax.ShapeDtypeStruct((1, d), F32)] * 3,
        sem=("arbitrary",))


def _matmul_tn(a, b, *, name, tm, tn, tk=1024, carry=None):
    s, m = a.shape
    n = b.shape[1]
    tk = min(tk, s)
    nk = s // tk

    def body(a_ref, b_ref, o_ref, acc_ref):
        k = pl.program_id(2)

        @pl.when(k == 0)
        def _():
            acc_ref[...] = jnp.zeros_like(acc_ref)

        acc_ref[...] += _dot_tn(a_ref[...], b_ref[...])

        @pl.when(k == nk - 1)
        def _():
            o_ref[...] = acc_ref[...].astype(o_ref.dtype)

    return _launch(
        body, carry, (a, b), name=name, grid=(m // tm, n // tn, nk),
        in_specs=[pl.BlockSpec((tk, tm), lambda i, j, k: (k, i)),
                  pl.BlockSpec((tk, tn), lambda i, j, k: (k, j))],
        out_specs=pl.BlockSpec((tm, tn), lambda i, j, k: (i, j)),
        out_shape=jax.ShapeDtypeStruct((m, n), BF16),
        scratch_shapes=[pltpu.VMEM((tm, tn), F32)],
        sem=("parallel", "parallel", "arbitrary"))


def _in_proj_fwd(h, nw, sc, sh, w_main, w_ba, *, name, ts=1024, tn=1536, carry=None):
    s, d = h.shape
    ts = min(ts, s)
    n_main = w_main.shape[1]
    n_ba = w_ba.shape[1]

    def body(h_ref, nw_ref, sc_ref, sh_ref, w_ref, wba_ref, u_ref, p_ref, ba_ref):
        @pl.when(pl.program_id(1) == 0)
        def _():
            u0 = _norm_mod(h_ref[...], nw_ref[...], sc_ref[...], sh_ref[...]).astype(BF16)
            u_ref[...] = u0
            ba_ref[...] = _dot(u0, wba_ref[...])

        p_ref[...] = _dot(u_ref[...], w_ref[...]).astype(BF16)

    return _launch(
        body, carry, (h, nw, sc, sh, w_main, w_ba), name=name, grid=(s // ts, n_main // tn),
        in_specs=[pl.BlockSpec((ts, d), lambda i, j: (i, 0)), _row(d), _row(d), _row(d),
                  pl.BlockSpec((d, tn), lambda i, j: (0, j)), _resident((d, n_ba))],
        out_specs=[pl.BlockSpec((ts, d), lambda i, j: (i, 0)),
                   pl.BlockSpec((ts, tn), lambda i, j: (i, j)),
                   pl.BlockSpec((ts, n_ba), lambda i, j: (i, 0))],
        out_shape=[jax.ShapeDtypeStruct((s, d), BF16), jax.ShapeDtypeStruct((s, n_main), BF16),
                   jax.ShapeDtypeStruct((s, n_ba), F32)],
        sem=("parallel", "arbitrary"))


QKV_W = 3 * HEADS * HEAD_DIM
Z_OFF, Z_W = 3072, 1024
GATE_OFF, GATE_W = 4096, 2048
A_OFF, A_W = 6144, 1536
N_MAIN = 7680
CONV_A = 512
BA_W = 128

L_BETA, L_G, L_EG, L_EKD, L_EGC = 0, 8, 16, 24, 32


def _softplus(z):
    e = jnp.exp(-jnp.abs(z))
    small = e * (1.0 - e * (0.5 - e * (1.0 / 3.0)))
    return jnp.maximum(z, 0.0) + jnp.where(e < 1e-3, small, jnp.log(1.0 + e))


def _tri(n, sgn, strict=False):
    i = lax.broadcasted_iota(jnp.int32, (n, n), 0)
    j = lax.broadcasted_iota(jnp.int32, (n, n), 1)
    dlt = (i - j) * sgn
    return (dlt > 0) if strict else (dlt >= 0)


def _scal_fwd(ba, alog, dtb, *, name, ts=512):
    s = ba.shape[0]

    def body(ba_ref, al_ref, dt_ref, o_ref):
        d = pl.program_id(0)
        sgn = 1 - 2 * d
        x = ba_ref[...]
        lane = lax.broadcasted_iota(jnp.int32, x.shape, 1)
        beta = _sigmoid(x)
        g = -jnp.exp(al_ref[0]) * _softplus(x + dt_ref[0])
        g = jnp.where((lane >= L_G) & (lane < L_EGC + 8), g, 0.0)
        ltri = jnp.where(_tri(CHUNK, sgn), 1.0, 0.0).astype(F32)
        for c in range(ts // CHUNK):
            rows = slice(c * CHUNK, (c + 1) * CHUNK)
            gc = _dot_exact(ltri, g[rows])
            g_end = jnp.where(d == 0, gc[CHUNK - 1:CHUNK], gc[0:1])
            ln = lane[rows]
            out = jnp.where(ln < L_G, beta[rows],
                  jnp.where(ln < L_EG, gc,
                  jnp.where(ln < L_EKD, jnp.exp(gc),
                  jnp.where(ln < L_EGC, jnp.exp(g_end - gc),
                  jnp.where(ln < L_EGC + 8, jnp.broadcast_to(jnp.exp(g_end), gc.shape), 0.0)))))
            o_ref[0, rows, :] = out

    return pl.pallas_call(
        body, name=name, grid=(2, s // ts),
        in_specs=[pl.BlockSpec((ts, BA_W), lambda d, i: (i, d)),
                  pl.BlockSpec((1, 1, BA_W), lambda d, i: (d, 0, 0)),
                  pl.BlockSpec((1, 1, BA_W), lambda d, i: (d, 0, 0))],
        out_specs=pl.BlockSpec((1, ts, BA_W), lambda d, i: (d, i, 0)),
        out_shape=jax.ShapeDtypeStruct((2, s, BA_W), F32),
        compiler_params=_params(("parallel", "parallel")),
    )(ba, alog, dtb)


def _scal_bwd(dscal, drow, ba, alog, dtb, *, name, ts=512):
    s = ba.shape[0]

    def body(ds_ref, dr_ref, ba_ref, al_ref, dt_ref, dba_ref, dal_ref, ddt_ref):
        d = pl.program_id(0)
        sgn = 1 - 2 * d
        x = ba_ref[...]
        lane = lax.broadcasted_iota(jnp.int32, x.shape, 1)
        in_g = (lane >= L_G) & (lane < L_G + 8)
        beta = _sigmoid(x)
        z = x + dt_ref[0]
        neg_a = -jnp.exp(al_ref[0])
        g = neg_a * _softplus(z)
        dsv = ds_ref[0]
        dgc = jnp.where(in_g, dsv + dr_ref[0], 0.0)
        utri = jnp.where(_tri(CHUNK, -sgn), 1.0, 0.0).astype(F32)
        dal = jnp.zeros((1, BA_W), F32)
        ddt = jnp.zeros((1, BA_W), F32)
        for c in range(ts // CHUNK):
            rows = slice(c * CHUNK, (c + 1) * CHUNK)
            dg = _dot_exact(utri, dgc[rows])
            dz = dg * neg_a * _sigmoid(z[rows])
            dal = dal + jnp.sum(dg * g[rows], axis=0, keepdims=True)
            ddt = ddt + jnp.sum(dz, axis=0, keepdims=True)
            b = beta[rows]
            out = jnp.where(lane[rows] < L_G, dsv[rows] * b * (1.0 - b), jnp.where(in_g[rows], dz, 0.0))
            dba_ref[rows, :] = out.astype(BF16)

        @pl.when(pl.program_id(1) == 0)
        def _():
            dal_ref[...] = jnp.zeros_like(dal_ref)
            ddt_ref[...] = jnp.zeros_like(ddt_ref)

        dal_ref[0] += dal
        ddt_ref[0] += ddt

    row3 = pl.BlockSpec((1, 1, BA_W), lambda d, i: (d, 0, 0))
    tok3 = pl.BlockSpec((1, ts, BA_W), lambda d, i: (d, i, 0))
    return pl.pallas_call(
        body, name=name, grid=(2, s // ts),
        in_specs=[tok3, tok3, pl.BlockSpec((ts, BA_W), lambda d, i: (i, d)), row3, row3],
        out_specs=[pl.BlockSpec((ts, BA_W), lambda d, i: (i, d)), row3, row3],
        out_shape=[jax.ShapeDtypeStruct((s, 2 * BA_W), BF16), jax.ShapeDtypeStruct((2, 1, BA_W), F32),
                   jax.ShapeDtypeStruct((2, 1, BA_W), F32)],
        compiler_params=_params(("arbitrary", "arbitrary")),
    )(dscal, drow, ba, alog, dtb)


HALO = 16


def _halo_specs(ts, width, col_block, n_rows, rows=HALO):
    r = ts // rows
    last = n_rows // rows - 1
    return [pl.BlockSpec((rows, width), lambda i: (jnp.maximum(i * r - 1, 0), col_block)),
            pl.BlockSpec((ts, width), lambda i: (i, col_block)),
            pl.BlockSpec((rows, width), lambda i: (jnp.minimum((i + 1) * r, last), col_block))]


def _fill_halo(dst_ref, prev_ref, cur_ref, next_ref, first, last, fn=lambda r: r[...].astype(F32)):
    h = prev_ref.shape[0]
    ts = cur_ref.shape[0]
    p = fn(prev_ref)
    n = fn(next_ref)
    dst_ref[0:h, :] = jnp.where(first, 0.0, p)
    dst_ref[h:h + ts, :] = fn(cur_ref)
    dst_ref[h + ts:h + ts + h, :] = jnp.where(last, 0.0, n)


def _dwconv_rows(src_ref, w, start, n_rows, cols):
    acc = w[0:1, :] * src_ref[start:start + n_rows, cols]
    for i in range(1, w.shape[0]):
        acc = acc + w[i:i + 1, :] * src_ref[start + i:start + i + n_rows, cols]
    return acc


def _l2norm_heads(act, scale):
    outs = []
    for hd in range(HEADS):
        seg = act[:, hd * HEAD_DIM:(hd + 1) * HEAD_DIM]
        outs.append(seg * (lax.rsqrt(jnp.sum(seg * seg, axis=-1, keepdims=True) + EPS) * scale))
    return jnp.concatenate(outs, axis=-1)


Q_SCALE = HEAD_DIM ** -0.5


def _conv_fwd(proj, conv_dn, conv_a, *, name, ts=256):
    s = proj.shape[0]
    hd = HEADS * HEAD_DIM
    nt = s // ts

    def body(qp_ref, qc_ref, qn_ref, ap_ref, ac_ref, an_ref, wdn_ref, wa_ref,
             q_ref, k_ref, v_ref, ya_ref, xs_ref, xa_ref):
        i = pl.program_id(0)
        first, last = i == 0, i == nt - 1
        _fill_halo(xs_ref, qp_ref, qc_ref, qn_ref, first, last)
        wdn = wdn_ref[...]
        for part, o_ref in enumerate((q_ref, k_ref, v_ref)):
            cols = slice(part * hd, (part + 1) * hd)
            pre = _dwconv_rows(xs_ref, wdn[:, cols], HALO - 2, ts, cols)
            act = pre * _sigmoid(pre)
            if part == 0:
                act = _l2norm_heads(act, Q_SCALE)
            elif part == 1:
                act = _l2norm_heads(act, 1.0)
            o_ref[...] = act
        cv = lambda r: r[:, CONV_A:2 * CONV_A].astype(F32) * r[:, 2 * CONV_A:].astype(F32)
        _fill_halo(xa_ref, ap_ref, ac_ref, an_ref, first, last, fn=cv)
        conv = _dwconv_rows(xa_ref, wa_ref[...], HALO - 1, ts, slice(0, CONV_A))
        ya_ref[...] = (ac_ref[:, 0:CONV_A].astype(F32) * conv).astype(BF16)

    tile = lambda w: pl.BlockSpec((ts, w), lambda i: (i, 0))
    return pl.pallas_call(
        body, name=name, grid=(nt,),
        in_specs=(_halo_specs(ts, QKV_W, 0, s) + _halo_specs(ts, A_W, A_OFF // A_W, s)
                  + [_resident(conv_dn.shape), _resident(conv_a.shape)]),
        out_specs=[tile(hd), tile(hd), tile(hd), tile(CONV_A)],
        out_shape=[jax.ShapeDtypeStruct((s, hd), F32)] * 3 + [jax.ShapeDtypeStruct((s, CONV_A), BF16)],
        scratch_shapes=[pltpu.VMEM((ts + 2 * HALO, QKV_W), F32), pltpu.VMEM((ts + 2 * HALO, CONV_A), F32)],
        compiler_params=_params(("parallel",)),
    )(proj, proj, proj, proj, proj, proj, conv_dn, conv_a)


INV_BASE = 8


GROUP = 4
GROWS = GROUP * CHUNK
N_GROUPS = HEADS // GROUP


def _stack(parts):
    return jnp.concatenate(parts, axis=0)


M_INCL, M_STRICT, M_EYE, M_BASE, M_JOIN = 0, 1, 2, 3, 4
JOIN_SIZES = (16, 32, 64)
N_MASKS = M_JOIN + len(JOIN_SIZES)


def _write_group_masks(mask_ref, sgn, n_masks):
    i = lax.broadcasted_iota(jnp.int32, (GROWS, GROWS), 0)
    j = lax.broadcasted_iota(jnp.int32, (GROWS, GROWS), 1)
    same = lambda m: jnp.right_shift(i, int(math.log2(m))) == jnp.right_shift(j, int(math.log2(m)))
    dlt = (i - j) * sgn
    one = lambda cond: jnp.where(cond, 1.0, 0.0).astype(F32)
    mask_ref[M_INCL] = one(same(CHUNK) & (dlt >= 0))
    mask_ref[M_STRICT] = one(same(CHUNK) & (dlt > 0))
    if n_masks > M_EYE:
        mask_ref[M_EYE] = one(i == j)
        mask_ref[M_BASE] = one(same(INV_BASE))
        for lvl, m in enumerate(JOIN_SIZES):
            mask_ref[M_JOIN + lvl] = one(same(m) & jnp.logical_not(same(m // 2)))


def _group_decay(gcol, grow, mask_ref):
    return jnp.exp(jnp.minimum(gcol - grow, 0.0)) * mask_ref[M_INCL]


def _block_inverse_many(a_ms, mask_refs):
    xs = [-(a * m[M_BASE]) for a, m in zip(a_ms, mask_refs)]
    ts = [m[M_EYE] + x for x, m in zip(xs, mask_refs)]
    ps = xs
    for _ in range(int(math.log2(INV_BASE)) - 1):
        p_bs = [p.astype(BF16) for p in ps]
        ps = [_dot(p_b, p_b) for p_b in p_bs]
        ts = [t + _dot(t.astype(BF16), p.astype(BF16)) for t, p in zip(ts, ps)]
    for lvl in range(len(JOIN_SIZES)):
        t_bs = [t.astype(BF16) for t in ts]
        joins = [(a * m[M_JOIN + lvl]).astype(BF16) for a, m in zip(a_ms, mask_refs)]
        mids = [_dot(t_b, j).astype(BF16) for t_b, j in zip(t_bs, joins)]
        ts = [t - _dot(mid, t_b) for t, mid, t_b in zip(ts, mids, t_bs)]
    return ts


def _group_operands(q_ref, k_ref, v_ref, scv, grp):
    heads = [GROUP * grp + t for t in range(GROUP)]
    tiles = lambda ref: [ref[:, h * HEAD_DIM:(h + 1) * HEAD_DIM] for h in heads]
    col = lambda base: [scv[:, base + h:base + h + 1] for h in heads]
    egc = [scv[0:1, L_EGC + h:L_EGC + h + 1] for h in heads]
    return heads, tiles(q_ref), tiles(k_ref), tiles(v_ref), col(L_BETA), col(L_G), col(L_EG), col(L_EKD), egc


def _delta_fwd(q, k, v, scal, grow, *, name):
    s = q.shape[0]
    n = s // CHUNK
    hd_all = HEADS * HEAD_DIM

    def body(*refs):
        ins, outs, (state, mask_ref) = refs[:10], refs[10:18], refs[18:]

        @pl.when(pl.program_id(0) == 0)
        def _():
            state[...] = jnp.zeros_like(state)
            for d in range(2):
                _write_group_masks(mask_ref.at[d], 1 - 2 * d, N_MASKS)

        chains = []
        for d in range(2):
            q_ref, k_ref, v_ref, sc_ref, gr_ref = ins[5 * d:5 * d + 5]
            scv = sc_ref[0]
            for grp in range(N_GROUPS):
                chains.append(dict(
                    d=d, grp=grp, gr_ref=gr_ref, out=outs[4 * d:4 * d + 4], state=state.at[d], masks=mask_ref.at[d],
                    ops=_group_operands(q_ref, k_ref, v_ref, scv, grp)))
        for ch in chains:
            heads, qs, ks, vs, beta, gcol, eg, ekd, egc = ch["ops"]
            ch["dm"] = _group_decay(_stack(gcol), ch["gr_ref"][0, 0, ch["grp"]:ch["grp"] + 1, :], ch["masks"])
            ch["k_b"] = _stack(ks).astype(BF16)
            ch["kb_b"] = _stack([ks[t] * beta[t] for t in range(GROUP)]).astype(BF16)
        for ch in chains:
            ch["a_m"] = _dot_nt(ch["kb_b"], ch["k_b"]) * ch["dm"] * ch["masks"][M_STRICT]
        tinvs = _block_inverse_many([ch["a_m"] for ch in chains], [ch["masks"] for ch in chains])
        for ch, tinv in zip(chains, tinvs):
            heads, qs, ks, vs, beta, gcol, eg, ekd, egc = ch["ops"]
            o_ref, st_ref, t_ref, vn_ref = ch["out"]
            ch["tinv"] = tinv.astype(BF16)
            t_ref[0, ch["grp"]] = tinv.T.astype(BF16)
            ch["p_b"] = (_dot_nt(_stack(qs).astype(BF16), ch["k_b"]) * ch["dm"]).astype(BF16)
            ch["sh"] = [ch["state"][h] for h in heads]
            ch["sh_b"] = [x.astype(BF16) for x in ch["sh"]]
            for t, h in enumerate(heads):
                st_ref[0, h] = ch["sh_b"][t]
        for ch in chains:
            heads, qs, ks, vs, beta, gcol, eg, ekd, egc = ch["ops"]
            ch["br"] = _stack([beta[t] * (vs[t] - _dot((ks[t] * eg[t]).astype(BF16), ch["sh_b"][t]))
                               for t in range(GROUP)]).astype(BF16)
        for ch in chains:
            ch["vn_b"] = _dot(ch["tinv"], ch["br"]).astype(BF16)
        for ch in chains:
            ch["o_intra"] = _dot(ch["p_b"], ch["vn_b"])
        for ch in chains:
            heads, qs, ks, vs, beta, gcol, eg, ekd, egc = ch["ops"]
            o_ref, st_ref, t_ref, vn_ref = ch["out"]
            for t, h in enumerate(heads):
                rows = slice(t * CHUNK, (t + 1) * CHUNK)
                cols = slice(h * HEAD_DIM, (h + 1) * HEAD_DIM)
                o_ref[:, cols] = _dot((qs[t] * eg[t]).astype(BF16), ch["sh_b"][t]) + ch["o_intra"][rows]
                ch["state"][h] = egc[t] * ch["sh"][t] + _dot_tn((ks[t] * ekd[t]).astype(BF16), ch["vn_b"][rows])
                vn_ref[:, cols] = ch["vn_b"][rows]

    at = [lambda c: c, lambda c: n - 1 - c]
    in_specs, out_specs = [], []
    for d in range(2):
        tok = pl.BlockSpec((CHUNK, hd_all), lambda c, d=d: (at[d](c), 0))
        in_specs += [tok] * 3 + [pl.BlockSpec((1, CHUNK, BA_W), lambda c, d=d: (d, at[d](c), 0)),
                                 pl.BlockSpec((1, 1, N_GROUPS, GROWS), lambda c, d=d: (d, at[d](c), 0, 0))]
        out_specs += [tok, pl.BlockSpec((1, HEADS, HEAD_DIM, HEAD_DIM), lambda c, d=d: (at[d](c), 0, 0, 0)),
                      pl.BlockSpec((1, N_GROUPS, GROWS, GROWS), lambda c, d=d: (at[d](c), 0, 0, 0)), tok]
    per_dir_shape = [jax.ShapeDtypeStruct((s, hd_all), F32),
                     jax.ShapeDtypeStruct((n, HEADS, HEAD_DIM, HEAD_DIM), BF16),
                     jax.ShapeDtypeStruct((n, N_GROUPS, GROWS, GROWS), BF16),
                     jax.ShapeDtypeStruct((s, hd_all), BF16)]
    outs = pl.pallas_call(
        body, name=name, grid=(n,), in_specs=in_specs, out_specs=out_specs, out_shape=per_dir_shape * 2,
        scratch_shapes=[pltpu.VMEM((2, HEADS, HEAD_DIM, HEAD_DIM), F32),
                        pltpu.VMEM((2, N_MASKS, GROWS, GROWS), F32)],
        compiler_params=_params(("arbitrary",)),
    )(*([q, k, v, scal, grow] * 2))
    return tuple((outs[i], outs[4 + i]) for i in range(4))


def _delta_bwd(q, k, v, scal, grow, states, tinv, vn, do, *, name, carry=None):
    s = q.shape[0]
    n = s // CHUNK
    hd_all = HEADS * HEAD_DIM

    grp_rows = [slice(t * CHUNK, (t + 1) * CHUNK) for t in range(GROUP)]
    per_head = lambda fn: _stack([fn(t) for t in range(GROUP)])

    def body(*refs):
        ins, outs, (dstate, mask_ref) = refs[:18], refs[18:28], refs[28:]

        @pl.when(pl.program_id(0) == 0)
        def _():
            dstate[...] = jnp.zeros_like(dstate)
            for d in range(2):
                _write_group_masks(mask_ref.at[d], 1 - 2 * d, M_EYE)

        chains = []
        for d in range(2):
            q_ref, k_ref, v_ref, sc_ref, gr_ref, st_ref, t_ref, vn_ref, do_ref = ins[9 * d:9 * d + 9]
            scv = sc_ref[0]
            for grp in range(N_GROUPS):
                c = types.SimpleNamespace(d=d, grp=grp, out=outs[5 * d:5 * d + 5], dstate=dstate.at[d],
                                          masks=mask_ref.at[d])
                (c.heads, qs, ks, c.vs, beta, gcol, eg, ekd, c.egc) = _group_operands(q_ref, k_ref, v_ref, scv, grp)
                c.cols = [slice(h * HEAD_DIM, (h + 1) * HEAD_DIM) for h in c.heads]
                c.dm = _group_decay(_stack(gcol), gr_ref[0, 0, grp:grp + 1, :], c.masks)
                c.dm_strict = c.dm * c.masks[M_STRICT]
                other = mask_ref.at[1 - d]
                c.dm_t = jnp.exp(jnp.minimum(gr_ref[0, 0, grp:grp + 1, :] - _stack(gcol), 0.0)) * other[M_INCL]
                c.dm_t_strict = c.dm_t * other[M_STRICT]
                c.beta, c.eg, c.ekd = _stack(beta), _stack(eg), _stack(ekd)
                c.q, c.k = _stack(qs), _stack(ks)
                c.q_b, c.k_b = c.q.astype(BF16), c.k.astype(BF16)
                c.kb_b = (c.k * c.beta).astype(BF16)
                c.kg, c.qg, c.kd = c.k * c.eg, c.q * c.eg, c.k * c.ekd
                c.kg_b, c.qg_b, c.kd_b = c.kg.astype(BF16), c.qg.astype(BF16), c.kd.astype(BF16)
                c.vn_b = _stack([vn_ref[:, cc] for cc in c.cols])
                c.do_b = _stack([do_ref[:, cc] for cc in c.cols]).astype(BF16)
                c.sh_b = [st_ref[0, h] for h in c.heads]
                c.dsp = [c.dstate[h] for h in c.heads]
                c.dsp_b = [x.astype(BF16) for x in c.dsp]
                c.t_b = t_ref[0, grp]
                chains.append(c)
        for c in chains:
            c.kk = _dot_nt(c.kb_b, c.k_b)
            c.qk = _dot_nt(c.q_b, c.k_b)
            c.pt_b = (_dot_nt(c.k_b, c.q_b) * c.dm_t).astype(BF16)
        for c in chains:
            c.r = per_head(lambda t: c.vs[t] - _dot(c.kg_b[grp_rows[t]], c.sh_b[t]))
            c.kd_ds = per_head(lambda t: _dot(c.kd_b[grp_rows[t]], c.dsp_b[t]))
        for c in chains:
            c.dvn_b = (_dot(c.pt_b, c.do_b) + c.kd_ds).astype(BF16)
        for c in chains:
            c.db = _dot(c.t_b, c.dvn_b)
        for c in chains:
            c.dr = c.db * c.beta
            c.dbeta = jnp.sum(c.db * c.r, axis=-1, keepdims=True)
            c.dr_b, c.db_b = c.dr.astype(BF16), c.db.astype(BF16)
        for c in chains:
            c.dkg = -per_head(lambda t: _dot_nt(c.dr_b[grp_rows[t]], c.sh_b[t]))
            c.dqg = per_head(lambda t: _dot_nt(c.do_b[grp_rows[t]], c.sh_b[t]))
            c.dkd = per_head(lambda t: _dot_nt(c.vn_b[grp_rows[t]], c.dsp_b[t]))
        for c in chains:
            c.dpm = _dot_nt(c.do_b, c.vn_b) * c.dm
            c.dam = -_dot_nt(c.db_b, c.vn_b) * c.dm_strict
            c.dpm_b, c.dam_b = c.dpm.astype(BF16), c.dam.astype(BF16)
            c.dpm_t_b = (_dot_nt(c.vn_b, c.do_b) * c.dm_t).astype(BF16)
            c.dam_t_b = (-_dot_nt(c.vn_b, c.db_b) * c.dm_t_strict).astype(BF16)
        for c in chains:
            c.dkb = _dot(c.dam_b, c.k_b)
            c.dq = c.dqg * c.eg + _dot(c.dpm_b, c.k_b)
        for c in chains:
            c.dk = (c.dkg * c.eg + c.dkd * c.ekd + _dot(c.dpm_t_b, c.q_b) + _dot(c.dam_t_b, c.kb_b)
                    + c.dkb * c.beta)
        lane = lax.broadcasted_iota(jnp.int32, (CHUNK, BA_W), 1)
        row = lax.broadcasted_iota(jnp.int32, (CHUNK, 1), 0)
        dsc_acc = [jnp.zeros((CHUNK, BA_W), F32) for _ in range(2)]
        for c in chains:
            dq_ref, dk_ref, dv_ref, dsc_ref, dgr_ref = c.out
            end_row = CHUNK - 1 if c.d == 0 else 0
            dbeta = c.dbeta + jnp.sum(c.dkb * c.k, axis=-1, keepdims=True)
            m = c.dpm * c.qk + c.dam * c.kk
            kd_term = jnp.sum(c.dkd * c.kd, axis=-1, keepdims=True)
            dgcol = (jnp.sum(c.dqg * c.qg, axis=-1, keepdims=True) + jnp.sum(c.dkg * c.kg, axis=-1, keepdims=True)
                     - kd_term + jnp.sum(m, axis=-1, keepdims=True))
            dgr_ref[0, c.grp:c.grp + 1, :] = -jnp.sum(m, axis=0, keepdims=True)
            for t, h in enumerate(c.heads):
                rows, cols = grp_rows[t], c.cols[t]
                dq_ref[:, cols] = c.dq[rows]
                dk_ref[:, cols] = c.dk[rows]
                dv_ref[:, cols] = c.dr[rows]
                dg_end = jnp.sum(kd_term[rows]) + c.egc[t] * jnp.sum(c.dsp[t] * c.sh_b[t].astype(F32))
                dgcol_h = dgcol[rows] + jnp.where(row == end_row, dg_end, 0.0)
                dsc_acc[c.d] = jnp.where(lane == L_BETA + h, dbeta[rows], dsc_acc[c.d])
                dsc_acc[c.d] = jnp.where(lane == L_G + h, dgcol_h, dsc_acc[c.d])
                c.dstate[h] = (_dot_tn(c.qg_b[rows], c.do_b[rows]) + c.egc[t] * c.dsp[t]
                               - _dot_tn(c.kg_b[rows], c.dr_b[rows]))
        for d in range(2):
            outs[5 * d + 3][...] = dsc_acc[d]

    at = [lambda c: n - 1 - c, lambda c: c]
    in_specs, out_specs, args = [], [], []
    for d in range(2):
        tok = pl.BlockSpec((CHUNK, hd_all), lambda c, d=d: (at[d](c), 0))
        in_specs += [tok] * 3 + [pl.BlockSpec((1, CHUNK, BA_W), lambda c, d=d: (d, at[d](c), 0)),
                                 pl.BlockSpec((1, 1, N_GROUPS, GROWS), lambda c, d=d: (d, at[d](c), 0, 0)),
                                 pl.BlockSpec((1, HEADS, HEAD_DIM, HEAD_DIM), lambda c, d=d: (at[d](c), 0, 0, 0)),
                                 pl.BlockSpec((1, N_GROUPS, GROWS, GROWS), lambda c, d=d: (at[d](c), 0, 0, 0)),
                                 tok, tok]
        args += [q, k, v, scal, grow, states[d], tinv[d], vn[d], do]
        out_specs += [tok] * 3 + [pl.BlockSpec((CHUNK, BA_W), lambda c, d=d: (at[d](c), 0)),
                                  pl.BlockSpec((1, N_GROUPS, GROWS), lambda c, d=d: (at[d](c), 0, 0))]
    per_dir_shape = ([jax.ShapeDtypeStruct((s, hd_all), F32)] * 3
                     + [jax.ShapeDtypeStruct((s, BA_W), F32), jax.ShapeDtypeStruct((n, N_GROUPS, GROWS), F32)])
    res = _launch(
        body, carry, tuple(args), name=name, grid=(n,), in_specs=in_specs, out_specs=out_specs,
        out_shape=per_dir_shape * 2,
        scratch_shapes=[pltpu.VMEM((2, HEADS, HEAD_DIM, HEAD_DIM), F32), pltpu.VMEM((2, M_EYE, GROWS, GROWS), F32)],
        sem=("arbitrary",))
    outs, got = res if carry is not None else (res, None)
    paired = tuple((outs[i], outs[5 + i]) for i in range(5))
    return paired if carry is None else (paired, got)


def _gate_norm_fwd(o2, proj, dnw, *, name, ts=512):
    s = o2[0].shape[0]
    hd_all = HEADS * HEAD_DIM

    def body(of_ref, ob_ref, z_ref, w_ref, y_ref):
        w = w_ref[...]
        for hd in range(HEADS):
            cols = slice(hd * HEAD_DIM, (hd + 1) * HEAD_DIM)
            seg = of_ref[:, cols] + ob_ref[:, cols]
            r = lax.rsqrt(jnp.mean(seg * seg, axis=-1, keepdims=True) + EPS)
            z = z_ref[:, cols].astype(F32)
            y_ref[:, cols] = ((seg * r * w) * (z * _sigmoid(z))).astype(BF16)

    tile = pl.BlockSpec((ts, hd_all), lambda i: (i, 0))
    return pl.pallas_call(
        body, name=name, grid=(s // ts,),
        in_specs=[tile, tile, pl.BlockSpec((ts, Z_W), lambda i: (i, Z_OFF // Z_W)), _row(HEAD_DIM)],
        out_specs=tile,
        out_shape=jax.ShapeDtypeStruct((s, hd_all), BF16),
        compiler_params=_params(("parallel",)),
    )(o2[0], o2[1], proj, dnw)


def _gate_norm_bwd(dyb, o2, proj, dnw, *, name, ts=512):
    s = o2[0].shape[0]
    hd_all = HEADS * HEAD_DIM

    def body(dy_ref, of_ref, ob_ref, z_ref, w_ref, do_ref, dz_ref, dw_ref):
        w = w_ref[...]
        dw = jnp.zeros((1, HEAD_DIM), F32)
        for hd in range(HEADS):
            cols = slice(hd * HEAD_DIM, (hd + 1) * HEAD_DIM)
            seg = of_ref[:, cols] + ob_ref[:, cols]
            r = lax.rsqrt(jnp.mean(seg * seg, axis=-1, keepdims=True) + EPS)
            xhat = seg * r
            z = z_ref[:, cols].astype(F32)
            sg = _sigmoid(z)
            dy = dy_ref[:, cols]
            dnrm = dy * (z * sg)
            dz_ref[:, cols] = (dy * (xhat * w) * (sg * (1.0 + z * (1.0 - sg)))).astype(BF16)
            dw = dw + jnp.sum(dnrm * xhat, axis=0, keepdims=True)
            dxhat = dnrm * w
            do_ref[:, cols] = r * (dxhat - xhat * jnp.mean(dxhat * xhat, axis=-1, keepdims=True))

        @pl.when(pl.program_id(0) == 0)
        def _():
            dw_ref[...] = jnp.zeros_like(dw_ref)

        dw_ref[...] += dw

    tile = pl.BlockSpec((ts, hd_all), lambda i: (i, 0))
    return pl.pallas_call(
        body, name=name, grid=(s // ts,),
        in_specs=[tile, tile, tile, pl.BlockSpec((ts, Z_W), lambda i: (i, Z_OFF // Z_W)), _row(HEAD_DIM)],
        out_specs=[tile, tile, _row(HEAD_DIM)],
        out_shape=[jax.ShapeDtypeStruct((s, hd_all), F32), jax.ShapeDtypeStruct((s, hd_all), BF16),
                   jax.ShapeDtypeStruct((1, HEAD_DIM), F32)],
        compiler_params=_params(("arbitrary",)),
    )(dyb, o2[0], o2[1], proj, dnw)


def _merge_fwd(ya, yb, proj, wa, wb, wo, h, g, *, name, ts=512):
    s, d = h.shape

    def body(ya_ref, yb_ref, gt_ref, wa_ref, wb_ref, wo_ref, h_ref, g_ref, pa_ref, pb_ref, mix_ref, ho_ref):
        pa = _dot(ya_ref[...], wa_ref[...])
        pb = _dot(yb_ref[...], wb_ref[...])
        pa_ref[...] = pa.astype(BF16)
        pb_ref[...] = pb.astype(BF16)
        merged = (_sigmoid(gt_ref[:, :d].astype(F32)) * pa + _sigmoid(gt_ref[:, d:].astype(F32)) * pb)
        mix = _dot(merged.astype(BF16), wo_ref[...])
        mix_ref[...] = mix.astype(BF16)
        ho_ref[...] = h_ref[...] + g_ref[...] * mix

    tile = pl.BlockSpec((ts, d), lambda i: (i, 0))
    return pl.pallas_call(
        body, name=name, grid=(s // ts,),
        in_specs=[pl.BlockSpec((ts, CONV_A), lambda i: (i, 0)), tile,
                  pl.BlockSpec((ts, GATE_W), lambda i: (i, GATE_OFF // GATE_W)),
                  _resident(wa.shape), _resident(wb.shape), _resident(wo.shape), tile, _row(d)],
        out_specs=[tile, tile, tile, tile],
        out_shape=[jax.ShapeDtypeStruct((s, d), BF16)] * 3 + [jax.ShapeDtypeStruct((s, d), F32)],
        compiler_params=_params(("parallel",)),
    )(ya, yb, proj, wa, wb, wo, h, g)


def _merge_bwd(dh, g, mix, pa, pb, proj, wa, wb, wo, *, name, ts=256):
    s, d = dh.shape

    def body(dh_ref, g_ref, mix_ref, pa_ref, pb_ref, gt_ref, wa_ref, wb_ref, wo_ref,
             dmix_ref, mg_ref, dpa_ref, dpb_ref, dgt_ref, dya_ref, dyb_ref, dg_ref):
        dh_v = dh_ref[...]
        dmix = (g_ref[...] * dh_v).astype(BF16)
        dmix_ref[...] = dmix

        @pl.when(pl.program_id(0) == 0)
        def _():
            dg_ref[...] = jnp.zeros_like(dg_ref)

        dg_ref[...] += jnp.sum(dh_v * mix_ref[...].astype(F32), axis=0, keepdims=True)
        dmerged = _dot_nt(dmix, wo_ref[...])
        pa = pa_ref[...].astype(F32)
        pb = pb_ref[...].astype(F32)
        sa = _sigmoid(gt_ref[:, :d].astype(F32))
        sb = _sigmoid(gt_ref[:, d:].astype(F32))
        mg_ref[...] = (sa * pa + sb * pb).astype(BF16)
        dpa = (dmerged * sa).astype(BF16)
        dpb = (dmerged * sb).astype(BF16)
        dpa_ref[...] = dpa
        dpb_ref[...] = dpb
        dgt_ref[:, :d] = (dmerged * pa * sa * (1.0 - sa)).astype(BF16)
        dgt_ref[:, d:] = (dmerged * pb * sb * (1.0 - sb)).astype(BF16)
        dya_ref[...] = _dot_nt(dpa, wa_ref[...])
        dyb_ref[...] = _dot_nt(dpb, wb_ref[...])

    tile = pl.BlockSpec((ts, d), lambda i: (i, 0))
    return pl.pallas_call(
        body, name=name, grid=(s // ts,),
        in_specs=[tile, _row(d), tile, tile, tile,
                  pl.BlockSpec((ts, GATE_W), lambda i: (i, GATE_OFF // GATE_W)),
                  _resident(wa.shape), _resident(wb.shape), _resident(wo.shape)],
        out_specs=[tile, tile, tile, tile, pl.BlockSpec((ts, GATE_W), lambda i: (i, 0)),
                   pl.BlockSpec((ts, CONV_A), lambda i: (i, 0)), tile, _row(d)],
        out_shape=[jax.ShapeDtypeStruct((s, d), BF16)] * 4
                  + [jax.ShapeDtypeStruct((s, GATE_W), BF16), jax.ShapeDtypeStruct((s, CONV_A), F32),
                     jax.ShapeDtypeStruct((s, d), F32), jax.ShapeDtypeStruct((1, d), F32)],
        compiler_params=_params(("arbitrary",)),
    )(dh, g, mix, pa, pb, proj, wa, wb, wo)


def _final_fwd_bwd(h, nw, target, *, name, ts=512):
    s, d = h.shape

    def body(h_ref, nw_ref, t_ref, loss_ref, dh_ref, dnw_ref):
        x = h_ref[...]
        w = nw_ref[...]
        r = lax.rsqrt(jnp.mean(x * x, axis=-1, keepdims=True) + EPS)
        xhat = x * r
        e = xhat * w - t_ref[...]
        part = 0.5 * jnp.sum(jnp.mean(e * e, axis=-1, keepdims=True))
        dy = e * (1.0 / d)
        dxhat = dy * w
        dh_ref[...] = r * (dxhat - xhat * jnp.mean(dxhat * xhat, axis=-1, keepdims=True))

        @pl.when(pl.program_id(0) == 0)
        def _():
            loss_ref[...] = jnp.zeros_like(loss_ref)
            dnw_ref[...] = jnp.zeros_like(dnw_ref)

        loss_ref[...] += jnp.broadcast_to(part, loss_ref.shape)
        dnw_ref[...] += jnp.sum(dy * xhat, axis=0, keepdims=True)

    tile = pl.BlockSpec((ts, d), lambda i: (i, 0))
    return pl.pallas_call(
        body, name=name, grid=(s // ts,),
        in_specs=[tile, _row(d), tile],
        out_specs=[_row(128), tile, _row(d)],
        out_shape=[jax.ShapeDtypeStruct((1, 128), F32), jax.ShapeDtypeStruct((s, d), F32),
                   jax.ShapeDtypeStruct((1, d), F32)],
        compiler_params=_params(("arbitrary",)),
    )(h, nw, target)


EXT = 8


def _l2norm_heads_bwd(act, dout, scale):
    outs = []
    for hd in range(HEADS):
        cols = slice(hd * HEAD_DIM, (hd + 1) * HEAD_DIM)
        seg = act[:, cols]
        nrm = lax.rsqrt(jnp.sum(seg * seg, axis=-1, keepdims=True) + EPS)
        yhat = seg * nrm
        dsg = dout[:, cols]
        outs.append((scale * nrm) * (dsg - yhat * jnp.sum(yhat * dsg, axis=-1, keepdims=True)))
    return jnp.concatenate(outs, axis=-1)


def _conv_bwd(dq2, dk2, dv2, dya, proj, conv_dn, conv_a, *, name, ts=256, carry=None):
    s = proj.shape[0]
    hd = HEADS * HEAD_DIM
    nt = s // ts
    te = ts + 2 * EXT
    kdn, ka = conv_dn.shape[0], conv_a.shape[0]

    def body(*refs):
        (qp_ref, qc_ref, qn_ref, ap_ref, ac_ref, an_ref) = refs[0:6]
        d3 = refs[6:24]
        (yp_ref, yc_ref, yn_ref, wdn_ref, wa_ref) = refs[24:29]
        (dqkv_ref, da_ref, dwdn_ref, dwa_ref) = refs[29:33]
        xs_ref, dps_ref, xa_ref, dca_ref = refs[33:37]
        i = pl.program_id(0)
        first, last = i == 0, i == nt - 1

        @pl.when(first)
        def _():
            dwdn_ref[...] = jnp.zeros_like(dwdn_ref)
            dwa_ref[...] = jnp.zeros_like(dwa_ref)

        rowe = lax.broadcasted_iota(jnp.int32, (te, 1), 0)
        inside = ~((first & (rowe < EXT)) | (last & (rowe >= EXT + ts)))
        _fill_halo(xs_ref, qp_ref, qc_ref, qn_ref, first, last)
        wdn = wdn_ref[...]
        for part in range(3):
            cols = slice(part * hd, (part + 1) * hd)
            pre = _dwconv_rows(xs_ref, wdn[:, cols], HALO - EXT - 2, te, cols)
            sg = _sigmoid(pre)
            act = pre * sg
            pf, cf, nf, pb, cb, nb = d3[6 * part:6 * part + 6]
            dout = jnp.concatenate([pf[...] + pb[...], cf[...] + cb[...], nf[...] + nb[...]], axis=0)
            if part == 0:
                dact = _l2norm_heads_bwd(act, dout, Q_SCALE)
            elif part == 1:
                dact = _l2norm_heads_bwd(act, dout, 1.0)
            else:
                dact = dout
            dpre = jnp.where(inside, dact * (sg * (1.0 + pre * (1.0 - sg))), 0.0)
            dps_ref[:, cols] = dpre
            acc = wdn[0:1, cols] * dps_ref[EXT + 2:EXT + 2 + ts, cols]
            for tap in range(1, kdn):
                acc = acc + wdn[tap:tap + 1, cols] * dps_ref[EXT + 2 - tap:EXT + 2 - tap + ts, cols]
            dqkv_ref[:, cols] = acc.astype(BF16)
            dcur = dps_ref[EXT:EXT + ts, cols]
            for tap in range(kdn):
                dwdn_ref[tap:tap + 1, cols] += jnp.sum(
                    dcur * xs_ref[HALO - 2 + tap:HALO - 2 + tap + ts, cols], axis=0, keepdims=True)

        cv = lambda r: r[:, CONV_A:2 * CONV_A].astype(F32) * r[:, 2 * CONV_A:].astype(F32)
        _fill_halo(xa_ref, ap_ref, ac_ref, an_ref, first, last, fn=cv)
        wa = wa_ref[...]
        gate_b = jnp.concatenate([ap_ref[HALO - EXT:, 0:CONV_A], ac_ref[:, 0:CONV_A], an_ref[0:EXT, 0:CONV_A]],
                                 axis=0).astype(F32)
        dya_e = jnp.concatenate([yp_ref[...], yc_ref[...], yn_ref[...]], axis=0)
        dca_ref[...] = jnp.where(inside, dya_e * gate_b, 0.0)
        conv = _dwconv_rows(xa_ref, wa, HALO - 1, ts, slice(0, CONV_A))
        acc = wa[0:1, :] * dca_ref[EXT + 1:EXT + 1 + ts, :]
        for tap in range(1, ka):
            acc = acc + wa[tap:tap + 1, :] * dca_ref[EXT + 1 - tap:EXT + 1 - tap + ts, :]
        gc = ac_ref[:, CONV_A:2 * CONV_A].astype(F32)
        val = ac_ref[:, 2 * CONV_A:].astype(F32)
        da_ref[:, 0:CONV_A] = (yc_ref[...] * conv).astype(BF16)
        da_ref[:, CONV_A:2 * CONV_A] = (acc * val).astype(BF16)
        da_ref[:, 2 * CONV_A:] = (acc * gc).astype(BF16)
        dcur = dca_ref[EXT:EXT + ts, :]
        for tap in range(ka):
            dwa_ref[tap:tap + 1, :] += jnp.sum(
                dcur * xa_ref[HALO - 1 + tap:HALO - 1 + tap + ts, :], axis=0, keepdims=True)

    cot = [arr for pair in (dq2, dk2, dv2) for arr in pair for _ in range(3)]
    return _launch(
        body, carry, (proj, proj, proj, proj, proj, proj, *cot, dya, dya, dya, conv_dn, conv_a),
        name=name, grid=(nt,),
        in_specs=(_halo_specs(ts, QKV_W, 0, s) + _halo_specs(ts, A_W, A_OFF // A_W, s)
                  + _halo_specs(ts, hd, 0, s, rows=EXT) * 6 + _halo_specs(ts, CONV_A, 0, s, rows=EXT)
                  + [_resident(conv_dn.shape), _resident(conv_a.shape)]),
        out_specs=[pl.BlockSpec((ts, QKV_W), lambda i: (i, 0)), pl.BlockSpec((ts, A_W), lambda i: (i, 0)),
                   pl.BlockSpec((8, QKV_W), lambda i: (0, 0)), pl.BlockSpec((8, CONV_A), lambda i: (0, 0))],
        out_shape=[jax.ShapeDtypeStruct((s, QKV_W), BF16), jax.ShapeDtypeStruct((s, A_W), BF16),
                   jax.ShapeDtypeStruct((8, QKV_W), F32), jax.ShapeDtypeStruct((8, CONV_A), F32)],
        scratch_shapes=[pltpu.VMEM((ts + 2 * HALO, QKV_W), F32), pltpu.VMEM((te, QKV_W), F32),
                        pltpu.VMEM((ts + 2 * HALO, CONV_A), F32), pltpu.VMEM((te, CONV_A), F32)],
        sem=("arbitrary",))


IN_A = (0, 1536)
IN_QKV = (1536, 4608)
IN_Z = (4608, 5632)
IN_BA = 5632
IN_GATE = (5664, 7712)
IN_COLS = 7712
G_REPL = 4


def _split_w_in(w_in):
    sl = lambda ab: w_in[:, ab[0]:ab[1]]
    w_main = jnp.concatenate([sl(IN_QKV), sl(IN_Z), sl(IN_GATE), sl(IN_A)], axis=1)
    blocks = []
    for d in range(2):
        beta = w_in[:, IN_BA + 8 * d:IN_BA + 8 * d + 8]
        alpha = w_in[:, IN_BA + 16 + 8 * d:IN_BA + 24 + 8 * d]
        pad = jnp.zeros((w_in.shape[0], BA_W - 8 - 8 * G_REPL), w_in.dtype)
        blocks += [beta] + [alpha] * G_REPL + [pad]
    return w_main, jnp.concatenate(blocks, axis=1)


def _merge_dw_in(dw_qkv, dw_z, dw_gate, dw_a, dw_ba):
    ba = [dw_ba[:, 0:8], dw_ba[:, BA_W:BA_W + 8], dw_ba[:, 8:16], dw_ba[:, BA_W + 8:BA_W + 16]]
    return jnp.concatenate([dw_a, dw_qkv, dw_z] + ba + [dw_gate], axis=1)


def _decay_rows(a_log_fwd, dt_bias_fwd, a_log_bwd, dt_bias_bwd):
    def rows(f, b):
        out = []
        for vec in (f, b):
            vec = vec.reshape(HEADS)
            out.append(jnp.concatenate([jnp.zeros((8,), F32)] + [vec] * G_REPL
                                       + [jnp.zeros((BA_W - 8 - 8 * G_REPL,), F32)])[None])
        return jnp.stack(out)
    return rows(a_log_fwd, a_log_bwd), rows(dt_bias_fwd, dt_bias_bwd)


def _local_step(x, target, mod9, wt, comm):
    s, d = x.shape
    n = s // CHUNK
    wt = dict(wt)
    sh1, sc1, g1, sh2, sc2, g2, sh3, sc3, g3 = [mod9[i:i + 1] for i in range(9)]
    alog, dtb = _decay_rows(wt["a_log_fwd"], wt["dt_bias_fwd"], wt["a_log_bwd"], wt["dt_bias_bwd"])

    (u1, a1, b1, f1), got = comm.gather(
        ["w_ffn1_down", "w_in"],
        lambda c: _ffn_up_fwd(x, wt["norm_ffn1"], sc1, sh1, wt["w_ffn1_up"], name="ffn1_up", carry=c))
    wt.update(got)
    w_main, w_ba = _split_w_in(wt["w_in"])
    y1, h1 = _ffn_down_fwd(f1, wt["w_ffn1_down"], x, g1, name="ffn1_down")
    (u2, proj, ba), got = comm.gather(
        ["w_a_out", "w_b_out", "w_out", "w_ffn2_up", "w_ffn2_down"],
        lambda c: _in_proj_fwd(h1, wt["norm_mix"], sc2, sh2, w_main, w_ba, name="in_proj", carry=c))
    wt.update(got)
    scal = _scal_fwd(ba, alog, dtb, name="scal_fwd")
    grow = scal[:, :, L_G:L_G + 8].reshape(2, n, CHUNK, HEADS).transpose(0, 1, 3, 2).reshape(
        2, n, N_GROUPS, GROWS)
    q, k, v, ya = _conv_fwd(proj, wt["conv_dn"], wt["conv_a"], name="conv_fwd")
    o2, states, tinv, vn = _delta_fwd(q, k, v, scal, grow, name="delta_fwd")
    yb = _gate_norm_fwd(o2, proj, wt["dn_norm"], name="gate_norm_fwd")
    pa, pb, mix, h2 = _merge_fwd(ya, yb, proj, wt["w_a_out"], wt["w_b_out"], wt["w_out"], h1, g2,
                                 name="merge_fwd")
    u3, a3, b3, f3 = _ffn_up_fwd(h2, wt["norm_ffn2"], sc3, sh3, wt["w_ffn2_up"], name="ffn2_up")
    y3, h3 = _ffn_down_fwd(f3, wt["w_ffn2_down"], h2, g3, name="ffn2_down")
    loss, dh3, dnorm_final = _final_fwd_bwd(h3, wt["norm_final"], target, name="final")

    dy3, dab3, dg3 = _ffn_bwd_act(dh3, g3, y3, a3, b3, wt["w_ffn2_down"], name="ffn2_bwd_act")
    dh2, dn3, dsc3, dsh3 = _norm_mod_matmul_bwd([(dab3, wt["w_ffn2_up"])], h2, wt["norm_ffn2"], sc3, dh3,
                                                name="ffn2_bwd_up")
    gw = {}
    gw["w_ffn2_up"] = _matmul_tn(u3, dab3, name="dw_ffn2_up", tm=1024, tn=2816)
    gw["w_ffn2_down"] = _matmul_tn(f3, dy3, name="dw_ffn2_down", tm=1408, tn=1024)

    dmix, merged, dpa, dpb, dgates, dya, dyb, dg2 = _merge_bwd(
        dh2, g2, mix, pa, pb, proj, wt["w_a_out"], wt["w_b_out"], wt["w_out"], name="merge_bwd")
    gw["w_out"] = _matmul_tn(merged, dmix, name="dw_out", tm=1024, tn=1024)
    gw["w_a_out"] = _matmul_tn(ya, dpa, name="dw_a_out", tm=512, tn=1024)
    gw["w_b_out"] = _matmul_tn(yb, dpb, name="dw_b_out", tm=1024, tn=1024)
    do, dz, ddn = _gate_norm_bwd(dyb, o2, proj, wt["dn_norm"], name="gate_norm_bwd")
    recv = {}
    (dq2, dk2, dv2, dscal, drow), got = comm.scatter(
        {nm: gw.pop(nm) for nm in ("w_ffn2_up", "w_ffn2_down")},
        lambda c: _delta_bwd(q, k, v, scal, grow, states, tinv, vn, do, name="delta_bwd", carry=c))
    recv.update(got)
    drow_p = jnp.pad(jnp.stack(drow).reshape(2, n, HEADS, CHUNK).transpose(0, 1, 3, 2).reshape(2, s, HEADS),
                     ((0, 0), (0, 0), (L_G, BA_W - L_G - HEADS)))
    dba, dalog, ddtb = _scal_bwd(jnp.stack(dscal), drow_p, ba, alog, dtb, name="scal_bwd")
    (dqkv, dbr_a, dconv_dn, dconv_a), got = comm.scatter(
        {nm: gw.pop(nm) for nm in ("w_out", "w_a_out", "w_b_out")},
        lambda c: _conv_bwd(dq2, dk2, dv2, dya, proj, wt["conv_dn"], wt["conv_a"], name="conv_bwd", carry=c))
    recv.update(got)
    dw_in = _merge_dw_in(
        _matmul_tn(u2, dqkv, name="dw_in_qkv", tm=1024, tn=1536),
        _matmul_tn(u2, dz, name="dw_in_z", tm=1024, tn=1024),
        _matmul_tn(u2, dgates, name="dw_in_gate", tm=1024, tn=1024),
        _matmul_tn(u2, dbr_a, name="dw_in_a", tm=1024, tn=1536),
        _matmul_tn(u2, dba, name="dw_in_ba", tm=1024, tn=2 * BA_W))
    (dh1, dn2, dsc2, dsh2), got = comm.scatter(
        {"w_in": dw_in},
        lambda c: _norm_mod_matmul_bwd(
            [(dqkv, w_main, 0), (dz, w_main, Z_OFF // Z_W), (dgates, w_main, GATE_OFF // GATE_W),
             (dbr_a, w_main, A_OFF // A_W), (dba, w_ba)],
            h1, wt["norm_mix"], sc2, dh2, name="in_proj_bwd", carry=c))
    recv.update(got)

    dy1, dab1, dg1 = _ffn_bwd_act(dh1, g1, y1, a1, b1, wt["w_ffn1_down"], name="ffn1_bwd_act")
    dw_down1 = _matmul_tn(f1, dy1, name="dw_ffn1_down", tm=1408, tn=1024)
    dw_up1, got = comm.scatter(
        {"w_ffn1_down": dw_down1},
        lambda c: _matmul_tn(u1, dab1, name="dw_ffn1_up", tm=1024, tn=2816, carry=c))
    recv.update(got)
    (dx, dn1, dsc1, dsh1), got = comm.scatter(
        {"w_ffn1_up": dw_up1},
        lambda c: _norm_mod_matmul_bwd([(dab1, wt["w_ffn1_up"])], x, wt["norm_ffn1"], sc1, dh1,
                                       name="ffn1_bwd_up", carry=c))
    recv.update(got)

    small = {
        "mod": jnp.concatenate([dsh1, dsc1, dg1, dsh2, dsc2, dg2, dsh3, dsc3, dg3], axis=1),
        "norm_ffn1": dn1, "norm_mix": dn2, "norm_ffn2": dn3, "norm_final": dnorm_final,
        "a_log_fwd": dalog[0, :, L_G:L_G + 8], "dt_bias_fwd": ddtb[0, :, L_G:L_G + 8],
        "a_log_bwd": dalog[1, :, L_G:L_G + 8], "dt_bias_bwd": ddtb[1, :, L_G:L_G + 8],
        "dn_norm": ddn,
        "conv_a": dconv_a[0:3].reshape(1, -1), "conv_dn": dconv_dn[0:5].reshape(1, -1),
    }
    return loss, dx, recv, small


def _full_weight(name, g):
    if name in COL_SHARDED + CONV_SHARDED:
        return g.transpose(1, 0, 2).reshape(g.shape[1], -1)
    return g.reshape(-1, g.shape[-1])


def _grad_pieces(name, g):
    g = g.astype(BF16)
    if name in COL_SHARDED:
        return g.reshape(g.shape[0], N_DEV, -1).transpose(1, 0, 2)
    return g.reshape(N_DEV, -1, g.shape[-1])


class _MeshComm:
    def __init__(self, shards):
        self.shards = shards

    def _run(self, xs, carrier, name, gather):
        if carrier is None:
            return None, _exchange(xs, name=name, gather=gather)
        return carrier((xs, gather))

    def gather(self, names, carrier=None, name=None):
        outs, got = self._run([self.shards[nm] for nm in names], carrier, name, True)
        return outs, {nm: _full_weight(nm, g) for nm, g in zip(names, got)}

    def scatter(self, grads, carrier=None, name=None):
        names = list(grads)
        outs, got = self._run([_grad_pieces(nm, grads[nm]) for nm in names], carrier, name, False)
        return outs, dict(zip(names, got))


def _mod_fwd(c_all, w_ada, *, name):
    def body(c_ref, w_ref, o_ref):
        cv = c_ref[...]
        o_ref[...] = _dot3(cv * _sigmoid(cv), w_ref[...])

    return pl.pallas_call(
        body, name=name, out_shape=jax.ShapeDtypeStruct((c_all.shape[0], w_ada.shape[1]), F32),
        compiler_params=_params(),
    )(c_all, w_ada)


def _adamw_math(w, g, m, v):
    m_new = ADAM_B1 * m + (1.0 - ADAM_B1) * g
    v_new = ADAM_B2 * v + (1.0 - ADAM_B2) * (g * g)
    m_hat = m_new / (1.0 - ADAM_B1 ** ADAM_STEP)
    v_hat = v_new / (1.0 - ADAM_B2 ** ADAM_STEP)
    delta = -ADAM_LR * (m_hat / (jnp.sqrt(v_hat) + ADAM_EPS) + ADAM_WD * w)
    return delta, m_new, v_new


def _reduce_adamw(pieces, w, m, v, *, name, tr):
    r, c = w.shape

    def body(p_ref, w_ref, m_ref, v_ref, g_ref, d_ref, mo_ref, vo_ref):
        g = p_ref[0].astype(F32)
        for src in range(1, N_DEV):
            g = g + p_ref[src].astype(F32)
        g_ref[...] = g
        d_ref[...], mo_ref[...], vo_ref[...] = _adamw_math(w_ref[...], g, m_ref[...], v_ref[...])

    tile = pl.BlockSpec((tr, c), lambda i: (i, 0))
    return pl.pallas_call(
        body, name=name, grid=(r // tr,),
        in_specs=[pl.BlockSpec((N_DEV, tr, c), lambda i: (0, i, 0)), tile, tile, tile],
        out_specs=[tile] * 4, out_shape=[jax.ShapeDtypeStruct((r, c), F32)] * 4,
        compiler_params=_params(("parallel",)),
    )(pieces, w, m, v)


def _ada_grad_adamw(c_all_t, dmod_cols, w, m, v, *, name, tr=256):
    r, c = w.shape

    def body(c_ref, dm_ref, w_ref, m_ref, v_ref, g_ref, d_ref, mo_ref, vo_ref):
        cv = c_ref[...]
        act = cv * _sigmoid(cv)
        dm = dm_ref[...]
        g = act[:, 0:1] * dm[0:1, :]
        for b in range(1, N_DEV):
            g = g + act[:, b:b + 1] * dm[b:b + 1, :]
        g_ref[...] = g
        d_ref[...], mo_ref[...], vo_ref[...] = _adamw_math(w_ref[...], g, m_ref[...], v_ref[...])

    tile = pl.BlockSpec((tr, c), lambda i: (i, 0))
    return pl.pallas_call(
        body, name=name, grid=(r // tr,),
        in_specs=[pl.BlockSpec((tr, N_DEV), lambda i: (i, 0)), pl.BlockSpec((N_DEV, c), lambda i: (0, 0)),
                  tile, tile, tile],
        out_specs=[tile] * 4, out_shape=[jax.ShapeDtypeStruct((r, c), F32)] * 4,
        compiler_params=_params(("parallel",)),
    )(c_all_t, dmod_cols, w, m, v)


def _sum_rows(parts, *, name):
    def body(p_ref, o_ref):
        acc = p_ref[0:1, :]
        for src in range(1, N_DEV):
            acc = acc + p_ref[src:src + 1, :]
        o_ref[...] = acc

    return pl.pallas_call(
        body, name=name, out_shape=jax.ShapeDtypeStruct((1, parts.shape[1]), F32), compiler_params=_params(),
    )(parts)


def _adamw_rows(g, w, m, v, *, name):
    def body(g_ref, w_ref, m_ref, v_ref, d_ref, mo_ref, vo_ref):
        d_ref[...], mo_ref[...], vo_ref[...] = _adamw_math(w_ref[...], g_ref[...], m_ref[...], v_ref[...])

    return pl.pallas_call(
        body, name=name, out_shape=[jax.ShapeDtypeStruct(g.shape, F32)] * 3, compiler_params=_params(),
    )(g, w, m, v)


WEIGHTS = ["w_ada", "b_ada", "norm_ffn1", "w_ffn1_up", "w_ffn1_down", "norm_mix", "w_in", "conv_a", "conv_dn",
           "a_log_fwd", "dt_bias_fwd", "a_log_bwd", "dt_bias_bwd", "dn_norm", "w_a_out", "w_b_out", "w_out",
           "norm_ffn2", "w_ffn2_up", "w_ffn2_down", "norm_final"]
COL_SHARDED = ["w_ffn1_up", "w_in", "w_a_out", "w_ffn2_up"]
ROW_SHARDED = ["w_ffn1_down", "w_b_out", "w_out", "w_ffn2_down"]
CONV_SHARDED = ["conv_a", "conv_dn"]
REPLICATED = ["b_ada", "norm_ffn1", "norm_mix", "a_log_fwd", "dt_bias_fwd", "a_log_bwd", "dt_bias_bwd",
              "dn_norm", "norm_ffn2", "norm_final"]
SMALL_ORDER = ["mod", "norm_ffn1", "norm_mix", "norm_ffn2", "norm_final", "a_log_fwd", "dt_bias_fwd",
               "a_log_bwd", "dt_bias_bwd", "dn_norm", "conv_a", "conv_dn"]
REDUCE_ROWS = {"w_ffn1_up": 256, "w_in": 256, "w_a_out": 256, "w_ffn2_up": 256,
               "w_ffn1_down": 176, "w_b_out": 128, "w_out": 128, "w_ffn2_down": 176}


def _pad_lanes(row):
    pad = (-row.shape[1]) % 128
    return jnp.pad(row, ((0, 0), (0, pad)))


def kernel(x, c, w_ada, b_ada, norm_ffn1, w_ffn1_up, w_ffn1_down, norm_mix, w_in, conv_a, conv_dn, a_log_fwd, dt_bias_fwd, a_log_bwd, dt_bias_bwd, dn_norm, w_a_out, w_b_out, w_out, norm_ffn2, w_ffn2_up, w_ffn2_down, norm_final, loss_target, m_w_ada, m_b_ada, m_norm_ffn1, m_w_ffn1_up, m_w_ffn1_down, m_norm_mix, m_w_in, m_conv_a, m_conv_dn, m_a_log_fwd, m_dt_bias_fwd, m_a_log_bwd, m_dt_bias_bwd, m_dn_norm, m_w_a_out, m_w_b_out, m_w_out, m_norm_ffn2, m_w_ffn2_up, m_w_ffn2_down, m_norm_final, v_w_ada, v_b_ada, v_norm_ffn1, v_w_ffn1_up, v_w_ffn1_down, v_norm_mix, v_w_in, v_conv_a, v_conv_dn, v_a_log_fwd, v_dt_bias_fwd, v_a_log_bwd, v_dt_bias_bwd, v_dn_norm, v_w_a_out, v_w_b_out, v_w_out, v_norm_ffn2, v_w_ffn2_up, v_w_ffn2_down, v_norm_final):
    args = dict(locals())
    w_loc = {n: args[n] for n in WEIGHTS}
    m_loc = {n: args["m_" + n] for n in WEIGHTS}
    v_loc = {n: args["v_" + n] for n in WEIGHTS}
    me = _flat_index(_my_position())
    d_model = x.shape[-1]

    big = COL_SHARDED + ROW_SHARDED
    shards = {n: w_loc[n][0].astype(BF16) for n in big}
    shards.update({n: w_loc[n][0] for n in CONV_SHARDED})
    shards["c"] = c
    comm = _MeshComm(shards)
    wt = comm.gather(["c", "conv_a", "conv_dn", "w_ffn1_up"], name="gather_first")[1]
    c_all = wt.pop("c")
    for n in REPLICATED[1:]:
        wt[n] = w_loc[n].reshape(1, -1)

    mod_cols = _mod_fwd(c_all, w_ada[0], name="mod_fwd")
    mod_all = _exchange([mod_cols], name="gather_mod", gather=True)[0]
    mod_mine = lax.dynamic_index_in_dim(mod_all, me, axis=1, keepdims=False).reshape(1, -1) + b_ada
    mod9 = mod_mine.reshape(9, d_model)

    loss_loc, dx, recv, small = _local_step(x[0], loss_target[0], mod9, wt, comm)
    loss = lax.psum(loss_loc[0, 0], MESH_AXES)

    res = {}
    for n in big:
        res[n] = _reduce_adamw(recv[n], w_loc[n][0], m_loc[n][0], v_loc[n][0], name="adamw_" + n,
                               tr=REDUCE_ROWS[n])

    packed = _pad_lanes(jnp.concatenate([small[n].reshape(1, -1) for n in SMALL_ORDER], axis=1))
    parts = _exchange([packed], name="gather_small", gather=True)[0].reshape(N_DEV, -1)
    total = _sum_rows(parts, name="sum_small")
    off = 0
    gsmall = {}
    for n in SMALL_ORDER:
        size = small[n].size
        gsmall[n] = total[:, off:off + size]
        off += size
    dmod_all = parts[:, 0:9 * d_model]
    ada_cols = w_ada.shape[-1]
    dmod_cols = lax.dynamic_slice_in_dim(dmod_all, me * ada_cols, ada_cols, axis=1)
    res["w_ada"] = _ada_grad_adamw(c_all.T, dmod_cols, w_ada[0], m_w_ada[0], v_w_ada[0], name="adamw_w_ada")
    g_rows = {"b_ada": gsmall["mod"]}
    for n in REPLICATED[1:]:
        g_rows[n] = gsmall[n]
    for n in CONV_SHARDED:
        taps, width = w_loc[n].shape[1], w_loc[n].shape[2]
        full = gsmall[n].reshape(taps, -1)
        g_rows[n] = lax.dynamic_slice_in_dim(full, me * width, width, axis=1).reshape(1, -1)
    row_names = REPLICATED + CONV_SHARDED
    cat = lambda src: _pad_lanes(jnp.concatenate([src[n].reshape(1, -1) for n in row_names], axis=1))
    g_cat = cat(g_rows)
    d_cat, m_cat, v_cat = _adamw_rows(g_cat, cat(w_loc), cat(m_loc), cat(v_loc), name="adamw_small")
    off = 0
    for n in row_names:
        size = w_loc[n].size
        res[n] = tuple(t[:, off:off + size] for t in (g_cat, d_cat, m_cat, v_cat))
        off += size

    outs = [loss, dx[None]]
    for kind in range(4):
        for n in WEIGHTS:
            outs.append(res[n][kind].reshape(w_loc[n].shape))
    return tuple(outs)
```

```python
import functools
import math
import types

import jax
import jax.numpy as jnp
from jax import lax
from jax.experimental import pallas as pl
from jax.experimental.pallas import tpu as pltpu

F32 = jnp.float32
BF16 = jnp.bfloat16
EPS = 1e-6
N_DEV = 8
CHUNK = 64
HEADS = 8
HEAD_DIM = 128
MESH_AXES = ("x", "y", "c")
VMEM_LIMIT_BYTES = 56 * 1024 * 1024

ADAM_LR = 0.001
ADAM_B1 = 0.9
ADAM_B2 = 0.999
ADAM_EPS = 1e-08
ADAM_WD = 0.01
ADAM_STEP = 10


def _params(sem=None):
    return pltpu.CompilerParams(dimension_semantics=sem, vmem_limit_bytes=VMEM_LIMIT_BYTES)


def _row(n):
    return pl.BlockSpec((1, n), lambda *_: (0, 0))


def _resident(shape):
    nd = len(shape)
    return pl.BlockSpec(shape, lambda *_: (0,) * nd, pipeline_mode=pl.Buffered(1))


def _col_chunks(width, chunk=512):
    return [slice(lo, min(lo + chunk, width)) for lo in range(0, width, chunk)]


def _col_window(w, width, col_block):
    return pl.BlockSpec((w.shape[0], width), lambda *_: (0, col_block), pipeline_mode=pl.Buffered(1))


def _sigmoid(x):
    return 1.0 / (1.0 + jnp.exp(-x))


def _dot(a, b):
    return jnp.dot(a, b, preferred_element_type=F32)


def _dot_nt(a, b):
    return lax.dot_general(a, b, (((1,), (1,)), ((), ())), preferred_element_type=F32)


def _dot_tn(a, b):
    return lax.dot_general(a, b, (((0,), (0,)), ((), ())), preferred_element_type=F32)


def _split_bf16(x):
    hi = x.astype(BF16)
    lo = (x - hi.astype(F32)).astype(BF16)
    return hi, lo


def _dot3(a, b, dot=_dot):
    ah, al = a if isinstance(a, tuple) else _split_bf16(a)
    bh, bl = b if isinstance(b, tuple) else _split_bf16(b)
    return dot(ah, bh) + dot(ah, bl) + dot(al, bh)


def _dot_exact(a, b):
    return jnp.dot(a, b, preferred_element_type=F32, precision=lax.Precision.HIGHEST)


def _my_position():
    return tuple(lax.axis_index(a) for a in MESH_AXES)


def _peer(pos, kk):
    return tuple((1 - p) if (kk >> (2 - b)) & 1 else p for b, p in enumerate(pos))


def _flat_index(pos):
    return pos[0] * 4 + pos[1] * 2 + pos[2]


_ANY = pl.BlockSpec(memory_space=pl.ANY)


class _AllToAll:
    def __init__(self, in_refs, out_refs, send_sems, recv_sems, local_sems):
        pos = _my_position()
        me = _flat_index(pos)
        self.copies = []
        for t in range(len(in_refs)):
            self.copies.append(pltpu.make_async_copy(in_refs[t].at[me], out_refs[t].at[me], local_sems.at[t]))
            for kk in range(1, N_DEV):
                peer = _peer(pos, kk)
                self.copies.append(pltpu.make_async_remote_copy(
                    src_ref=in_refs[t].at[_flat_index(peer)], dst_ref=out_refs[t].at[me],
                    send_sem=send_sems.at[t, kk - 1], recv_sem=recv_sems.at[t, kk - 1],
                    device_id=peer, device_id_type=pl.DeviceIdType.MESH))

    def start(self):
        for cp in self.copies:
            cp.start()

    def finish(self):
        for cp in self.copies:
            cp.wait()


class _AllGather:
    def __init__(self, in_refs, out_refs, send_sems, recv_sems, local_sems):
        self.refs = (in_refs, out_refs, send_sems, recv_sems, local_sems)
        x, y, c = _my_position()
        self.me, self.sibling = (x, y, c), (x, y, 1 - c)
        self.chips = [(1 - x, y), (x, 1 - y), (1 - x, 1 - y)]
        self.core = c

    def _copy(self, t, k, block, to, own=False):
        in_refs, out_refs, send_sems, recv_sems, _ = self.refs
        rows = out_refs[t].at[_flat_index(block)]
        return pltpu.make_async_remote_copy(
            src_ref=in_refs[t] if own else rows, dst_ref=rows,
            send_sem=send_sems.at[t, k], recv_sem=recv_sems.at[t, k],
            device_id=to, device_id_type=pl.DeviceIdType.MESH)

    def _local(self, t):
        in_refs, out_refs, _, _, local_sems = self.refs
        return pltpu.make_async_copy(in_refs[t], out_refs[t].at[_flat_index(self.me)], local_sems.at[t])

    def start(self):
        c = self.core
        for t in range(len(self.refs[0])):
            self._local(t).start()
            self._copy(t, 0, self.me, self.sibling, own=True).start()
            for j, chip in enumerate(self.chips):
                self._copy(t, 1 + j, self.me, (*chip, c), own=True).start()

    def finish(self):
        c = self.core
        n_t = len(self.refs[0])
        for t in range(n_t):
            for j, chip in enumerate(self.chips):
                self._copy(t, 1 + j, (*chip, c), self.me).wait_recv()
                self._copy(t, 4 + j, (*chip, c), self.sibling).start()
        for t in range(n_t):
            self._copy(t, 0, self.sibling, self.me).wait_recv()
            for j, chip in enumerate(self.chips):
                self._copy(t, 4 + j, (*chip, 1 - c), self.me).wait_recv()
            self._copy(t, 0, self.me, self.sibling, own=True).wait_send()
            for j, chip in enumerate(self.chips):
                self._copy(t, 1 + j, self.me, (*chip, c), own=True).wait_send()
                self._copy(t, 4 + j, (*chip, c), self.sibling).wait_send()
            self._local(t).wait()


def _exchange_plan(in_refs, out_refs, send_sems, recv_sems, local_sems, gather):
    return (_AllGather if gather else _AllToAll)(in_refs, out_refs, send_sems, recv_sems, local_sems)


def _exchange_shapes(xs, gather):
    out_shape = [jax.ShapeDtypeStruct(((N_DEV,) + x.shape) if gather else x.shape, x.dtype) for x in xs]
    sems = [pltpu.SemaphoreType.DMA((len(xs), N_DEV - 1)), pltpu.SemaphoreType.DMA((len(xs), N_DEV - 1)),
            pltpu.SemaphoreType.DMA((len(xs),))]
    return out_shape, sems


def _exchange(xs, *, name, gather):
    nt = len(xs)

    def body(*refs):
        plan = _exchange_plan(refs[:nt], refs[nt:2 * nt], *refs[2 * nt:], gather)
        plan.start()
        plan.finish()

    out_shape, sems = _exchange_shapes(xs, gather)
    return pl.pallas_call(body, name=name, in_specs=[_ANY] * nt, out_specs=[_ANY] * nt, out_shape=out_shape,
                          scratch_shapes=sems)(*xs)


def _launch(body, carry, args, *, name, grid, in_specs, out_specs, out_shape, scratch_shapes=(), sem):
    single = not isinstance(out_shape, (list, tuple))
    out_specs = [out_specs] if single else list(out_specs)
    out_shape = [out_shape] if single else list(out_shape)
    if carry is None:
        outs = pl.pallas_call(body, name=name, grid=grid, in_specs=list(in_specs), out_specs=out_specs,
                              out_shape=out_shape, scratch_shapes=list(scratch_shapes),
                              compiler_params=_params(sem))(*args)
        return outs[0] if single else outs
    xs, gather = carry
    nt, n_in, n_out, n_scr = len(xs), len(args), len(out_shape), len(scratch_shapes)
    x_shape, sems = _exchange_shapes(xs, gather)

    def wrapped(*refs):
        c_in, x_in = refs[:n_in], refs[n_in:n_in + nt]
        c_out = refs[n_in + nt:n_in + nt + n_out]
        x_out = refs[n_in + nt + n_out:n_in + 2 * nt + n_out]
        scr = refs[n_in + 2 * nt + n_out:]
        ids = [pl.program_id(a) for a in range(len(grid))]
        first = functools.reduce(jnp.logical_and, [i == 0 for i in ids])
        last = functools.reduce(jnp.logical_and, [i == g - 1 for i, g in zip(ids, grid)])
        plan = lambda: _exchange_plan(x_in, x_out, *scr[n_scr:], gather)

        @pl.when(first)
        def _():
            plan().start()

        body(*c_in, *c_out, *scr[:n_scr])

        @pl.when(last)
        def _():
            plan().finish()

    outs = pl.pallas_call(
        wrapped, name=name, grid=grid, in_specs=list(in_specs) + [_ANY] * nt,
        out_specs=out_specs + [_ANY] * nt, out_shape=out_shape + x_shape,
        scratch_shapes=list(scratch_shapes) + sems,
        compiler_params=_params(("arbitrary",) * len(grid)))(*args, *xs)
    compute = outs[:n_out]
    return (compute[0] if single else compute), outs[n_out:]


def _norm_mod(x, nw, sc, sh):
    r = lax.rsqrt(jnp.mean(x * x, axis=-1, keepdims=True) + EPS)
    return (x * r * nw) * (1.0 + sc) + sh


def _norm_mod_bwd(x, nw, sc, du):
    r = lax.rsqrt(jnp.mean(x * x, axis=-1, keepdims=True) + EPS)
    xhat = x * r
    n = xhat * nw
    dsh = jnp.sum(du, axis=0, keepdims=True)
    dsc = jnp.sum(du * n, axis=0, keepdims=True)
    dn = du * (1.0 + sc)
    dnw = jnp.sum(dn * xhat, axis=0, keepdims=True)
    dxhat = dn * nw
    dx = r * (dxhat - xhat * jnp.mean(dxhat * xhat, axis=-1, keepdims=True))
    return dx, dnw, dsc, dsh


def _ffn_up_fwd(h, nw, sc, sh, wup, *, name, ts=1024, tn=1408, carry=None):
    s, d = h.shape
    ts = min(ts, s)
    f_dim = wup.shape[1] // 2
    nj = f_dim // tn

    def body(h_ref, nw_ref, sc_ref, sh_ref, wa_ref, wb_ref, u_ref, a_ref, b_ref, f_ref):
        @pl.when(pl.program_id(1) == 0)
        def _():
            u_ref[...] = _norm_mod(h_ref[...], nw_ref[...], sc_ref[...], sh_ref[...]).astype(BF16)

        u = u_ref[...]

        def epilogue(a, b, cs):
            a_ref[:, cs] = a.astype(BF16)
            b_ref[:, cs] = b.astype(BF16)
            f_ref[:, cs] = (a * _sigmoid(a) * b).astype(BF16)

        pending = None
        for cs in _col_chunks(tn):
            products = (_dot(u, wa_ref[:, cs]), _dot(u, wb_ref[:, cs]), cs)
            if pending is not None:
                epilogue(*pending)
            pending = products
        epilogue(*pending)

    return _launch(
        body, carry, (h, nw, sc, sh, wup, wup), name=name, grid=(s // ts, nj),
        in_specs=[pl.BlockSpec((ts, d), lambda i, j: (i, 0)), _row(d), _row(d), _row(d),
                  pl.BlockSpec((d, tn), lambda i, j: (0, j)),
                  pl.BlockSpec((d, tn), lambda i, j: (0, j + nj))],
        out_specs=[pl.BlockSpec((ts, d), lambda i, j: (i, 0)),
                   pl.BlockSpec((ts, tn), lambda i, j: (i, j)),
                   pl.BlockSpec((ts, tn), lambda i, j: (i, j)),
                   pl.BlockSpec((ts, tn), lambda i, j: (i, j))],
        out_shape=[jax.ShapeDtypeStruct((s, d), BF16)] + [jax.ShapeDtypeStruct((s, f_dim), BF16)] * 3,
        sem=("parallel", "arbitrary"))


def _ffn_down_fwd(f, wd, h, g, *, name, ts=512):
    s, f_dim = f.shape
    d = wd.shape[1]

    def body(f_ref, wd_ref, h_ref, g_ref, y_ref, ho_ref):
        y = _dot(f_ref[...], wd_ref[...])
        y_ref[...] = y.astype(BF16)
        ho_ref[...] = h_ref[...] + (0.5 * g_ref[...]) * y

    return pl.pallas_call(
        body, name=name, grid=(s // ts,),
        in_specs=[pl.BlockSpec((ts, f_dim), lambda i: (i, 0)), _resident((f_dim, d)),
                  pl.BlockSpec((ts, d), lambda i: (i, 0)), _row(d)],
        out_specs=[pl.BlockSpec((ts, d), lambda i: (i, 0)), pl.BlockSpec((ts, d), lambda i: (i, 0))],
        out_shape=[jax.ShapeDtypeStruct((s, d), BF16), jax.ShapeDtypeStruct((s, d), F32)],
        compiler_params=_params(("parallel",)),
    )(f, wd, h, g)


def _ffn_bwd_act(dh, g, y, a, b, wd, *, name, ts=256, carry=None):
    s, d = dh.shape
    f_dim = a.shape[1]

    def body(dh_ref, g_ref, y_ref, a_ref, b_ref, wd_ref, dy_ref, dab_ref, dg_ref):
        dh_v = dh_ref[...]
        dy = ((0.5 * g_ref[...]) * dh_v).astype(BF16)
        dy_ref[...] = dy
        part = jnp.sum(0.5 * dh_v * y_ref[...].astype(F32), axis=0, keepdims=True)

        @pl.when(pl.program_id(0) == 0)
        def _():
            dg_ref[...] = jnp.zeros_like(dg_ref)

        dg_ref[...] += part

        def epilogue(df, cs):
            av = a_ref[:, cs].astype(F32)
            bv = b_ref[:, cs].astype(F32)
            sg = _sigmoid(av)
            dab_ref[:, cs] = (df * bv * (sg * (1.0 + av * (1.0 - sg)))).astype(BF16)
            dab_ref[:, slice(f_dim + cs.start, f_dim + cs.stop)] = (df * (av * sg)).astype(BF16)

        pending = None
        for cs in _col_chunks(f_dim):
            product = (_dot_nt(dy, wd_ref[cs, :]), cs)
            if pending is not None:
                epilogue(*pending)
            pending = product
        epilogue(*pending)

    return _launch(
        body, carry, (dh, g, y, a, b, wd), name=name, grid=(s // ts,),
        in_specs=[pl.BlockSpec((ts, d), lambda i: (i, 0)), _row(d),
                  pl.BlockSpec((ts, d), lambda i: (i, 0)),
                  pl.BlockSpec((ts, f_dim), lambda i: (i, 0)),
                  pl.BlockSpec((ts, f_dim), lambda i: (i, 0)),
                  _resident((f_dim, d))],
        out_specs=[pl.BlockSpec((ts, d), lambda i: (i, 0)),
                   pl.BlockSpec((ts, 2 * f_dim), lambda i: (i, 0)), _row(d)],
        out_shape=[jax.ShapeDtypeStruct((s, d), BF16), jax.ShapeDtypeStruct((s, 2 * f_dim), BF16),
                   jax.ShapeDtypeStruct((1, d), F32)],
        sem=("arbitrary",))


def _norm_mod_matmul_bwd(pairs, h, nw, sc, dh_in, *, name, ts=256, carry=None):
    s, d = h.shape
    n_pairs = len(pairs)

    def body(*refs):
        dx_refs = refs[:n_pairs]
        w_refs = refs[n_pairs:2 * n_pairs]
        h_ref, nw_ref, sc_ref, dhi_ref, dho_ref, dnw_ref, dsc_ref, dsh_ref = refs[2 * n_pairs:]
        du = _dot_nt(dx_refs[0][...], w_refs[0][...])
        for k in range(1, n_pairs):
            du = du + _dot_nt(dx_refs[k][...], w_refs[k][...])
        dx, dnw, dsc, dsh = _norm_mod_bwd(h_ref[...], nw_ref[...], sc_ref[...], du)
        dho_ref[...] = dhi_ref[...] + dx

        @pl.when(pl.program_id(0) == 0)
        def _():
            dnw_ref[...] = jnp.zeros_like(dnw_ref)
            dsc_ref[...] = jnp.zeros_like(dsc_ref)
            dsh_ref[...] = jnp.zeros_like(dsh_ref)

        dnw_ref[...] += dnw
        dsc_ref[...] += dsc
        dsh_ref[...] += dsh

    dxs = [p[0] for p in pairs]
    ws = [p[1] for p in pairs]
    tile = pl.BlockSpec((ts, d), lambda i: (i, 0))
    return _launch(
        body, carry, (*dxs, *ws, h, nw, sc, dh_in), name=name, grid=(s // ts,),
        in_specs=([pl.BlockSpec((ts, x.shape[1]), lambda i: (i, 0)) for x in dxs]
                  + [_col_window(w, x.shape[1], p[2] if len(p) > 2 else 0) for p, x, w in zip(pairs, dxs, ws)]
                  + [tile, _row(d), _row(d), tile]),
        out_specs=[tile, _row(d), _row(d), _row(d)],
        out_shape=[jax.ShapeDtypeStruct((s, d), F32)] + [jax.ShapeDtypeStruct((1, d), F32)] * 3,
        sem=("arbitrary",))


def _matmul_tn(a, b, *, name, tm, tn, tk=1024, carry=None):
    s, m = a.shape
    n = b.shape[1]
    tk = min(tk, s)
    nk = s // tk

    def body(a_ref, b_ref, o_ref, acc_ref):
        k = pl.program_id(2)

        @pl.when(k == 0)
        def _():
            acc_ref[...] = jnp.zeros_like(acc_ref)

        acc_ref[...] += _dot_tn(a_ref[...], b_ref[...])

        @pl.when(k == nk - 1)
        def _():
            o_ref[...] = acc_ref[...].astype(o_ref.dtype)

    return _launch(
        body, carry, (a, b), name=name, grid=(m // tm, n // tn, nk),
        in_specs=[pl.BlockSpec((tk, tm), lambda i, j, k: (k, i)),
                  pl.BlockSpec((tk, tn), lambda i, j, k: (k, j))],
        out_specs=pl.BlockSpec((tm, tn), lambda i, j, k: (i, j)),
        out_shape=jax.ShapeDtypeStruct((m, n), BF16),
        scratch_shapes=[pltpu.VMEM((tm, tn), F32)],
        sem=("parallel", "parallel", "arbitrary"))


def _in_proj_fwd(h, nw, sc, sh, w_main, w_ba, *, name, ts=1024, tn=1536, carry=None):
    s, d = h.shape
    ts = min(ts, s)
    n_main = w_main.shape[1]
    n_ba = w_ba.shape[1]

    def body(h_ref, nw_ref, sc_ref, sh_ref, w_ref, wba_ref, u_ref, p_ref, ba_ref):
        @pl.when(pl.program_id(1) == 0)
        def _():
            u0 = _norm_mod(h_ref[...], nw_ref[...], sc_ref[...], sh_ref[...]).astype(BF16)
            u_ref[...] = u0
            ba_ref[...] = _dot(u0, wba_ref[...])

        p_ref[...] = _dot(u_ref[...], w_ref[...]).astype(BF16)

    return _launch(
        body, carry, (h, nw, sc, sh, w_main, w_ba), name=name, grid=(s // ts, n_main // tn),
        in_specs=[pl.BlockSpec((ts, d), lambda i, j: (i, 0)), _row(d), _row(d), _row(d),
                  pl.BlockSpec((d, tn), lambda i, j: (0, j)), _resident((d, n_ba))],
        out_specs=[pl.BlockSpec((ts, d), lambda i, j: (i, 0)),
                   pl.BlockSpec((ts, tn), lambda i, j: (i, j)),
                   pl.BlockSpec((ts, n_ba), lambda i, j: (i, 0))],
        out_shape=[jax.ShapeDtypeStruct((s, d), BF16), jax.ShapeDtypeStruct((s, n_main), BF16),
                   jax.ShapeDtypeStruct((s, n_ba), F32)],
        sem=("parallel", "arbitrary"))


QKV_W = 3 * HEADS * HEAD_DIM
Z_OFF, Z_W = 3072, 1024
GATE_OFF, GATE_W = 4096, 2048
A_OFF, A_W = 6144, 1536
N_MAIN = 7680
CONV_A = 512
BA_W = 128

L_BETA, L_G, L_EG, L_EKD, L_EGC = 0, 8, 16, 24, 32


def _softplus(z):
    e = jnp.exp(-jnp.abs(z))
    small = e * (1.0 - e * (0.5 - e * (1.0 / 3.0)))
    return jnp.maximum(z, 0.0) + jnp.where(e < 1e-3, small, jnp.log(1.0 + e))


def _tri(n, sgn, strict=False):
    i = lax.broadcasted_iota(jnp.int32, (n, n), 0)
    j = lax.broadcasted_iota(jnp.int32, (n, n), 1)
    dlt = (i - j) * sgn
    return (dlt > 0) if strict else (dlt >= 0)


def _scal_fwd(ba, alog, dtb, *, name, ts=512):
    s = ba.shape[0]

    def body(ba_ref, al_ref, dt_ref, o_ref):
        d = pl.program_id(0)
        sgn = 1 - 2 * d
        x = ba_ref[...]
        lane = lax.broadcasted_iota(jnp.int32, x.shape, 1)
        beta = _sigmoid(x)
        g = -jnp.exp(al_ref[0]) * _softplus(x + dt_ref[0])
        g = jnp.where((lane >= L_G) & (lane < L_EGC + 8), g, 0.0)
        ltri = jnp.where(_tri(CHUNK, sgn), 1.0, 0.0).astype(F32)
        for c in range(ts // CHUNK):
            rows = slice(c * CHUNK, (c + 1) * CHUNK)
            gc = _dot_exact(ltri, g[rows])
            g_end = jnp.where(d == 0, gc[CHUNK - 1:CHUNK], gc[0:1])
            ln = lane[rows]
            out = jnp.where(ln < L_G, beta[rows],
                  jnp.where(ln < L_EG, gc,
                  jnp.where(ln < L_EKD, jnp.exp(gc),
                  jnp.where(ln < L_EGC, jnp.exp(g_end - gc),
                  jnp.where(ln < L_EGC + 8, jnp.broadcast_to(jnp.exp(g_end), gc.shape), 0.0)))))
            o_ref[0, rows, :] = out

    return pl.pallas_call(
        body, name=name, grid=(2, s // ts),
        in_specs=[pl.BlockSpec((ts, BA_W), lambda d, i: (i, d)),
                  pl.BlockSpec((1, 1, BA_W), lambda d, i: (d, 0, 0)),
                  pl.BlockSpec((1, 1, BA_W), lambda d, i: (d, 0, 0))],
        out_specs=pl.BlockSpec((1, ts, BA_W), lambda d, i: (d, i, 0)),
        out_shape=jax.ShapeDtypeStruct((2, s, BA_W), F32),
        compiler_params=_params(("parallel", "parallel")),
    )(ba, alog, dtb)


def _scal_bwd(dscal, drow, ba, alog, dtb, *, name, ts=512):
    s = ba.shape[0]

    def body(ds_ref, dr_ref, ba_ref, al_ref, dt_ref, dba_ref, dal_ref, ddt_ref):
        d = pl.program_id(0)
        sgn = 1 - 2 * d
        x = ba_ref[...]
        lane = lax.broadcasted_iota(jnp.int32, x.shape, 1)
        in_g = (lane >= L_G) & (lane < L_G + 8)
        beta = _sigmoid(x)
        z = x + dt_ref[0]
        neg_a = -jnp.exp(al_ref[0])
        g = neg_a * _softplus(z)
        dsv = ds_ref[0]
        dgc = jnp.where(in_g, dsv + dr_ref[0], 0.0)
        utri = jnp.where(_tri(CHUNK, -sgn), 1.0, 0.0).astype(F32)
        dal = jnp.zeros((1, BA_W), F32)
        ddt = jnp.zeros((1, BA_W), F32)
        for c in range(ts // CHUNK):
            rows = slice(c * CHUNK, (c + 1) * CHUNK)
            dg = _dot_exact(utri, dgc[rows])
            dz = dg * neg_a * _sigmoid(z[rows])
            dal = dal + jnp.sum(dg * g[rows], axis=0, keepdims=True)
            ddt = ddt + jnp.sum(dz, axis=0, keepdims=True)
            b = beta[rows]
            out = jnp.where(lane[rows] < L_G, dsv[rows] * b * (1.0 - b), jnp.where(in_g[rows], dz, 0.0))
            dba_ref[rows, :] = out.astype(BF16)

        @pl.when(pl.program_id(1) == 0)
        def _():
            dal_ref[...] = jnp.zeros_like(dal_ref)
            ddt_ref[...] = jnp.zeros_like(ddt_ref)

        dal_ref[0] += dal
        ddt_ref[0] += ddt

    row3 = pl.BlockSpec((1, 1, BA_W), lambda d, i: (d, 0, 0))
    tok3 = pl.BlockSpec((1, ts, BA_W), lambda d, i: (d, i, 0))
    return pl.pallas_call(
        body, name=name, grid=(2, s // ts),
        in_specs=[tok3, tok3, pl.BlockSpec((ts, BA_W), lambda d, i: (i, d)), row3, row3],
        out_specs=[pl.BlockSpec((ts, BA_W), lambda d, i: (i, d)), row3, row3],
        out_shape=[jax.ShapeDtypeStruct((s, 2 * BA_W), BF16), jax.ShapeDtypeStruct((2, 1, BA_W), F32),
                   jax.ShapeDtypeStruct((2, 1, BA_W), F32)],
        compiler_params=_params(("arbitrary", "arbitrary")),
    )(dscal, drow, ba, alog, dtb)


HALO = 16


def _halo_specs(ts, width, col_block, n_rows, rows=HALO):
    r = ts // rows
    last = n_rows // rows - 1
    return [pl.BlockSpec((rows, width), lambda i: (jnp.maximum(i * r - 1, 0), col_block)),
            pl.BlockSpec((ts, width), lambda i: (i, col_block)),
            pl.BlockSpec((rows, width), lambda i: (jnp.minimum((i + 1) * r, last), col_block))]


def _fill_halo(dst_ref, prev_ref, cur_ref, next_ref, first, last, fn=lambda r: r[...].astype(F32)):
    h = prev_ref.shape[0]
    ts = cur_ref.shape[0]
    p = fn(prev_ref)
    n = fn(next_ref)
    dst_ref[0:h, :] = jnp.where(first, 0.0, p)
    dst_ref[h:h + ts, :] = fn(cur_ref)
    dst_ref[h + ts:h + ts + h, :] = jnp.where(last, 0.0, n)


def _dwconv_rows(src_ref, w, start, n_rows, cols):
    acc = w[0:1, :] * src_ref[start:start + n_rows, cols]
    for i in range(1, w.shape[0]):
        acc = acc + w[i:i + 1, :] * src_ref[start + i:start + i + n_rows, cols]
    return acc


def _l2norm_heads(act, scale):
    outs = []
    for hd in range(HEADS):
        seg = act[:, hd * HEAD_DIM:(hd + 1) * HEAD_DIM]
        outs.append(seg * (lax.rsqrt(jnp.sum(seg * seg, axis=-1, keepdims=True) + EPS) * scale))
    return jnp.concatenate(outs, axis=-1)


Q_SCALE = HEAD_DIM ** -0.5


def _conv_fwd(proj, conv_dn, conv_a, *, name, ts=256):
    s = proj.shape[0]
    hd = HEADS * HEAD_DIM
    nt = s // ts

    def body(qp_ref, qc_ref, qn_ref, ap_ref, ac_ref, an_ref, wdn_ref, wa_ref,
             q_ref, k_ref, v_ref, ya_ref, xs_ref, xa_ref):
        i = pl.program_id(0)
        first, last = i == 0, i == nt - 1
        _fill_halo(xs_ref, qp_ref, qc_ref, qn_ref, first, last)
        wdn = wdn_ref[...]
        for part, o_ref in enumerate((q_ref, k_ref, v_ref)):
            cols = slice(part * hd, (part + 1) * hd)
            pre = _dwconv_rows(xs_ref, wdn[:, cols], HALO - 2, ts, cols)
            act = pre * _sigmoid(pre)
            if part == 0:
                act = _l2norm_heads(act, Q_SCALE)
            elif part == 1:
                act = _l2norm_heads(act, 1.0)
            o_ref[...] = act
        cv = lambda r: r[:, CONV_A:2 * CONV_A].astype(F32) * r[:, 2 * CONV_A:].astype(F32)
        _fill_halo(xa_ref, ap_ref, ac_ref, an_ref, first, last, fn=cv)
        conv = _dwconv_rows(xa_ref, wa_ref[...], HALO - 1, ts, slice(0, CONV_A))
        ya_ref[...] = (ac_ref[:, 0:CONV_A].astype(F32) * conv).astype(BF16)

    tile = lambda w: pl.BlockSpec((ts, w), lambda i: (i, 0))
    return pl.pallas_call(
        body, name=name, grid=(nt,),
        in_specs=(_halo_specs(ts, QKV_W, 0, s) + _halo_specs(ts, A_W, A_OFF // A_W, s)
                  + [_resident(conv_dn.shape), _resident(conv_a.shape)]),
        out_specs=[tile(hd), tile(hd), tile(hd), tile(CONV_A)],
        out_shape=[jax.ShapeDtypeStruct((s, hd), F32)] * 3 + [jax.ShapeDtypeStruct((s, CONV_A), BF16)],
        scratch_shapes=[pltpu.VMEM((ts + 2 * HALO, QKV_W), F32), pltpu.VMEM((ts + 2 * HALO, CONV_A), F32)],
        compiler_params=_params(("parallel",)),
    )(proj, proj, proj, proj, proj, proj, conv_dn, conv_a)


INV_BASE = 8


GROUP = 4
GROWS = GROUP * CHUNK
N_GROUPS = HEADS // GROUP


def _stack(parts):
    return jnp.concatenate(parts, axis=0)


M_INCL, M_STRICT, M_EYE, M_BASE, M_JOIN = 0, 1, 2, 3, 4
JOIN_SIZES = (16, 32, 64)
N_MASKS = M_JOIN + len(JOIN_SIZES)


def _write_group_masks(mask_ref, sgn, n_masks):
    i = lax.broadcasted_iota(jnp.int32, (GROWS, GROWS), 0)
    j = lax.broadcasted_iota(jnp.int32, (GROWS, GROWS), 1)
    same = lambda m: jnp.right_shift(i, int(math.log2(m))) == jnp.right_shift(j, int(math.log2(m)))
    dlt = (i - j) * sgn
    one = lambda cond: jnp.where(cond, 1.0, 0.0).astype(F32)
    mask_ref[M_INCL] = one(same(CHUNK) & (dlt >= 0))
    mask_ref[M_STRICT] = one(same(CHUNK) & (dlt > 0))
    if n_masks > M_EYE:
        mask_ref[M_EYE] = one(i == j)
        mask_ref[M_BASE] = one(same(INV_BASE))
        for lvl, m in enumerate(JOIN_SIZES):
            mask_ref[M_JOIN + lvl] = one(same(m) & jnp.logical_not(same(m // 2)))


def _group_decay(gcol, grow, mask_ref):
    return jnp.exp(jnp.minimum(gcol - grow, 0.0)) * mask_ref[M_INCL]


def _block_inverse_many(a_ms, mask_refs):
    xs = [-(a * m[M_BASE]) for a, m in zip(a_ms, mask_refs)]
    ts = [m[M_EYE] + x for x, m in zip(xs, mask_refs)]
    ps = xs
    for _ in range(int(math.log2(INV_BASE)) - 1):
        p_bs = [p.astype(BF16) for p in ps]
        ps = [_dot(p_b, p_b) for p_b in p_bs]
        ts = [t + _dot(t.astype(BF16), p.astype(BF16)) for t, p in zip(ts, ps)]
    for lvl in range(len(JOIN_SIZES)):
        t_bs = [t.astype(BF16) for t in ts]
        joins = [(a * m[M_JOIN + lvl]).astype(BF16) for a, m in zip(a_ms, mask_refs)]
        mids = [_dot(t_b, j).astype(BF16) for t_b, j in zip(t_bs, joins)]
        ts = [t - _dot(mid, t_b) for t, mid, t_b in zip(ts, mids, t_bs)]
    return ts


def _group_operands(q_ref, k_ref, v_ref, scv, grp):
    heads = [GROUP * grp + t for t in range(GROUP)]
    tiles = lambda ref: [ref[:, h * HEAD_DIM:(h + 1) * HEAD_DIM] for h in heads]
    col = lambda base: [scv[:, base + h:base + h + 1] for h in heads]
    egc = [scv[0:1, L_EGC + h:L_EGC + h + 1] for h in heads]
    return heads, tiles(q_ref), tiles(k_ref), tiles(v_ref), col(L_BETA), col(L_G), col(L_EG), col(L_EKD), egc


def _delta_fwd(q, k, v, scal, grow, *, name):
    s = q.shape[0]
    n = s // CHUNK
    hd_all = HEADS * HEAD_DIM

    def body(*refs):
        ins, outs, (state, mask_ref) = refs[:10], refs[10:18], refs[18:]

        @pl.when(pl.program_id(0) == 0)
        def _():
            state[...] = jnp.zeros_like(state)
            for d in range(2):
                _write_group_masks(mask_ref.at[d], 1 - 2 * d, N_MASKS)

        chains = []
        for d in range(2):
            q_ref, k_ref, v_ref, sc_ref, gr_ref = ins[5 * d:5 * d + 5]
            scv = sc_ref[0]
            for grp in range(N_GROUPS):
                chains.append(dict(
                    d=d, grp=grp, gr_ref=gr_ref, out=outs[4 * d:4 * d + 4], state=state.at[d], masks=mask_ref.at[d],
                    ops=_group_operands(q_ref, k_ref, v_ref, scv, grp)))
        for ch in chains:
            heads, qs, ks, vs, beta, gcol, eg, ekd, egc = ch["ops"]
            ch["dm"] = _group_decay(_stack(gcol), ch["gr_ref"][0, 0, ch["grp"]:ch["grp"] + 1, :], ch["masks"])
            ch["k_b"] = _stack(ks).astype(BF16)
            ch["kb_b"] = _stack([ks[t] * beta[t] for t in range(GROUP)]).astype(BF16)
        for ch in chains:
            ch["a_m"] = _dot_nt(ch["kb_b"], ch["k_b"]) * ch["dm"] * ch["masks"][M_STRICT]
        tinvs = _block_inverse_many([ch["a_m"] for ch in chains], [ch["masks"] for ch in chains])
        for ch, tinv in zip(chains, tinvs):
            heads, qs, ks, vs, beta, gcol, eg, ekd, egc = ch["ops"]
            o_ref, st_ref, t_ref, vn_ref = ch["out"]
            ch["tinv"] = tinv.astype(BF16)
            t_ref[0, ch["grp"]] = ch["tinv"]
            ch["p_b"] = (_dot_nt(_stack(qs).astype(BF16), ch["k_b"]) * ch["dm"]).astype(BF16)
            ch["sh"] = [ch["state"][h] for h in heads]
            ch["sh_b"] = [x.astype(BF16) for x in ch["sh"]]
            for t, h in enumerate(heads):
                st_ref[0, h] = ch["sh_b"][t]
        for ch in chains:
            heads, qs, ks, vs, beta, gcol, eg, ekd, egc = ch["ops"]
            ch["br"] = _stack([beta[t] * (vs[t] - _dot((ks[t] * eg[t]).astype(BF16), ch["sh_b"][t]))
                               for t in range(GROUP)]).astype(BF16)
        for ch in chains:
            ch["vn_b"] = _dot(ch["tinv"], ch["br"]).astype(BF16)
        for ch in chains:
            ch["o_intra"] = _dot(ch["p_b"], ch["vn_b"])
        for ch in chains:
            heads, qs, ks, vs, beta, gcol, eg, ekd, egc = ch["ops"]
            o_ref, st_ref, t_ref, vn_ref = ch["out"]
            for t, h in enumerate(heads):
                rows = slice(t * CHUNK, (t + 1) * CHUNK)
                cols = slice(h * HEAD_DIM, (h + 1) * HEAD_DIM)
                o_ref[:, cols] = _dot((qs[t] * eg[t]).astype(BF16), ch["sh_b"][t]) + ch["o_intra"][rows]
                ch["state"][h] = egc[t] * ch["sh"][t] + _dot_tn((ks[t] * ekd[t]).astype(BF16), ch["vn_b"][rows])
                vn_ref[:, cols] = ch["vn_b"][rows]

    at = [lambda c: c, lambda c: n - 1 - c]
    in_specs, out_specs = [], []
    for d in range(2):
        tok = pl.BlockSpec((CHUNK, hd_all), lambda c, d=d: (at[d](c), 0))
        in_specs += [tok] * 3 + [pl.BlockSpec((1, CHUNK, BA_W), lambda c, d=d: (d, at[d](c), 0)),
                                 pl.BlockSpec((1, 1, N_GROUPS, GROWS), lambda c, d=d: (d, at[d](c), 0, 0))]
        out_specs += [tok, pl.BlockSpec((1, HEADS, HEAD_DIM, HEAD_DIM), lambda c, d=d: (at[d](c), 0, 0, 0)),
                      pl.BlockSpec((1, N_GROUPS, GROWS, GROWS), lambda c, d=d: (at[d](c), 0, 0, 0)), tok]
    per_dir_shape = [jax.ShapeDtypeStruct((s, hd_all), F32),
                     jax.ShapeDtypeStruct((n, HEADS, HEAD_DIM, HEAD_DIM), BF16),
                     jax.ShapeDtypeStruct((n, N_GROUPS, GROWS, GROWS), BF16),
                     jax.ShapeDtypeStruct((s, hd_all), BF16)]
    outs = pl.pallas_call(
        body, name=name, grid=(n,), in_specs=in_specs, out_specs=out_specs, out_shape=per_dir_shape * 2,
        scratch_shapes=[pltpu.VMEM((2, HEADS, HEAD_DIM, HEAD_DIM), F32),
                        pltpu.VMEM((2, N_MASKS, GROWS, GROWS), F32)],
        compiler_params=_params(("arbitrary",)),
    )(*([q, k, v, scal, grow] * 2))
    return tuple((outs[i], outs[4 + i]) for i in range(4))


def _delta_bwd(q, k, v, scal, grow, states, tinv, vn, do, *, name, carry=None):
    s = q.shape[0]
    n = s // CHUNK
    hd_all = HEADS * HEAD_DIM

    grp_rows = [slice(t * CHUNK, (t + 1) * CHUNK) for t in range(GROUP)]
    per_head = lambda fn: _stack([fn(t) for t in range(GROUP)])

    def body(*refs):
        ins, outs, (dstate, mask_ref) = refs[:18], refs[18:28], refs[28:]

        @pl.when(pl.program_id(0) == 0)
        def _():
            dstate[...] = jnp.zeros_like(dstate)
            for d in range(2):
                _write_group_masks(mask_ref.at[d], 1 - 2 * d, M_EYE)

        chains = []
        for d in range(2):
            q_ref, k_ref, v_ref, sc_ref, gr_ref, st_ref, t_ref, vn_ref, do_ref = ins[9 * d:9 * d + 9]
            scv = sc_ref[0]
            for grp in range(N_GROUPS):
                c = types.SimpleNamespace(d=d, grp=grp, out=outs[5 * d:5 * d + 5], dstate=dstate.at[d],
                                          masks=mask_ref.at[d])
                (c.heads, qs, ks, c.vs, beta, gcol, eg, ekd, c.egc) = _group_operands(q_ref, k_ref, v_ref, scv, grp)
                c.cols = [slice(h * HEAD_DIM, (h + 1) * HEAD_DIM) for h in c.heads]
                c.dm = _group_decay(_stack(gcol), gr_ref[0, 0, grp:grp + 1, :], c.masks)
                c.dm_strict = c.dm * c.masks[M_STRICT]
                other = mask_ref.at[1 - d]
                c.dm_t = jnp.exp(jnp.minimum(gr_ref[0, 0, grp:grp + 1, :] - _stack(gcol), 0.0)) * other[M_INCL]
                c.dm_t_strict = c.dm_t * other[M_STRICT]
                c.beta, c.eg, c.ekd = _stack(beta), _stack(eg), _stack(ekd)
                c.q, c.k = _stack(qs), _stack(ks)
                c.q_b, c.k_b = c.q.astype(BF16), c.k.astype(BF16)
                c.kb_b = (c.k * c.beta).astype(BF16)
                c.kg, c.qg, c.kd = c.k * c.eg, c.q * c.eg, c.k * c.ekd
                c.kg_b, c.qg_b, c.kd_b = c.kg.astype(BF16), c.qg.astype(BF16), c.kd.astype(BF16)
                c.vn_b = _stack([vn_ref[:, cc] for cc in c.cols])
                c.do_b = _stack([do_ref[:, cc] for cc in c.cols]).astype(BF16)
                c.sh_b = [st_ref[0, h] for h in c.heads]
                c.dsp = [c.dstate[h] for h in c.heads]
                c.dsp_b = [x.astype(BF16) for x in c.dsp]
                c.t_b = t_ref[0, grp]
                chains.append(c)
        for c in chains:
            c.kk = _dot_nt(c.kb_b, c.k_b)
            c.qk = _dot_nt(c.q_b, c.k_b)
            c.pt_b = (_dot_nt(c.k_b, c.q_b) * c.dm_t).astype(BF16)
        for c in chains:
            c.r = per_head(lambda t: c.vs[t] - _dot(c.kg_b[grp_rows[t]], c.sh_b[t]))
            c.kd_ds = per_head(lambda t: _dot(c.kd_b[grp_rows[t]], c.dsp_b[t]))
        for c in chains:
            c.dvn_b = (_dot(c.pt_b, c.do_b) + c.kd_ds).astype(BF16)
        for c in chains:
            c.db = _dot_tn(c.t_b, c.dvn_b)
        for c in chains:
            c.dr = c.db * c.beta
            c.dbeta = jnp.sum(c.db * c.r, axis=-1, keepdims=True)
            c.dr_b, c.db_b = c.dr.astype(BF16), c.db.astype(BF16)
        for c in chains:
            c.dkg = -per_head(lambda t: _dot_nt(c.dr_b[grp_rows[t]], c.sh_b[t]))
            c.dqg = per_head(lambda t: _dot_nt(c.do_b[grp_rows[t]], c.sh_b[t]))
            c.dkd = per_head(lambda t: _dot_nt(c.vn_b[grp_rows[t]], c.dsp_b[t]))
        for c in chains:
            c.dpm = _dot_nt(c.do_b, c.vn_b) * c.dm
            c.dam = -_dot_nt(c.db_b, c.vn_b) * c.dm_strict
            c.dpm_b, c.dam_b = c.dpm.astype(BF16), c.dam.astype(BF16)
            c.dpm_t_b = (_dot_nt(c.vn_b, c.do_b) * c.dm_t).astype(BF16)
            c.dam_t_b = (-_dot_nt(c.vn_b, c.db_b) * c.dm_t_strict).astype(BF16)
        for c in chains:
            c.dkb = _dot(c.dam_b, c.k_b)
            c.dq = c.dqg * c.eg + _dot(c.dpm_b, c.k_b)
        for c in chains:
            c.dk = (c.dkg * c.eg + c.dkd * c.ekd + _dot(c.dpm_t_b, c.q_b) + _dot(c.dam_t_b, c.kb_b)
                    + c.dkb * c.beta)
        lane = lax.broadcasted_iota(jnp.int32, (CHUNK, BA_W), 1)
        row = lax.broadcasted_iota(jnp.int32, (CHUNK, 1), 0)
        dsc_acc = [jnp.zeros((CHUNK, BA_W), F32) for _ in range(2)]
        for c in chains:
            dq_ref, dk_ref, dv_ref, dsc_ref, dgr_ref = c.out
            end_row = CHUNK - 1 if c.d == 0 else 0
            dbeta = c.dbeta + jnp.sum(c.dkb * c.k, axis=-1, keepdims=True)
            m = c.dpm * c.qk + c.dam * c.kk
            kd_term = jnp.sum(c.dkd * c.kd, axis=-1, keepdims=True)
            dgcol = (jnp.sum(c.dqg * c.qg, axis=-1, keepdims=True) + jnp.sum(c.dkg * c.kg, axis=-1, keepdims=True)
                     - kd_term + jnp.sum(m, axis=-1, keepdims=True))
            dgr_ref[0, c.grp:c.grp + 1, :] = -jnp.sum(m, axis=0, keepdims=True)
            for t, h in enumerate(c.heads):
                rows, cols = grp_rows[t], c.cols[t]
                dq_ref[:, cols] = c.dq[rows]
                dk_ref[:, cols] = c.dk[rows]
                dv_ref[:, cols] = c.dr[rows]
                dg_end = jnp.sum(kd_term[rows]) + c.egc[t] * jnp.sum(c.dsp[t] * c.sh_b[t].astype(F32))
                dgcol_h = dgcol[rows] + jnp.where(row == end_row, dg_end, 0.0)
                dsc_acc[c.d] = jnp.where(lane == L_BETA + h, dbeta[rows], dsc_acc[c.d])
                dsc_acc[c.d] = jnp.where(lane == L_G + h, dgcol_h, dsc_acc[c.d])
                c.dstate[h] = (_dot_tn(c.qg_b[rows], c.do_b[rows]) + c.egc[t] * c.dsp[t]
                               - _dot_tn(c.kg_b[rows], c.dr_b[rows]))
        for d in range(2):
            outs[5 * d + 3][...] = dsc_acc[d]

    at = [lambda c: n - 1 - c, lambda c: c]
    in_specs, out_specs, args = [], [], []
    for d in range(2):
        tok = pl.BlockSpec((CHUNK, hd_all), lambda c, d=d: (at[d](c), 0))
        in_specs += [tok] * 3 + [pl.BlockSpec((1, CHUNK, BA_W), lambda c, d=d: (d, at[d](c), 0)),
                                 pl.BlockSpec((1, 1, N_GROUPS, GROWS), lambda c, d=d: (d, at[d](c), 0, 0)),
                                 pl.BlockSpec((1, HEADS, HEAD_DIM, HEAD_DIM), lambda c, d=d: (at[d](c), 0, 0, 0)),
                                 pl.BlockSpec((1, N_GROUPS, GROWS, GROWS), lambda c, d=d: (at[d](c), 0, 0, 0)),
                                 tok, tok]
        args += [q, k, v, scal, grow, states[d], tinv[d], vn[d], do]
        out_specs += [tok] * 3 + [pl.BlockSpec((CHUNK, BA_W), lambda c, d=d: (at[d](c), 0)),
                                  pl.BlockSpec((1, N_GROUPS, GROWS), lambda c, d=d: (at[d](c), 0, 0))]
    per_dir_shape = ([jax.ShapeDtypeStruct((s, hd_all), F32)] * 3
                     + [jax.ShapeDtypeStruct((s, BA_W), F32), jax.ShapeDtypeStruct((n, N_GROUPS, GROWS), F32)])
    res = _launch(
        body, carry, tuple(args), name=name, grid=(n,), in_specs=in_specs, out_specs=out_specs,
        out_shape=per_dir_shape * 2,
        scratch_shapes=[pltpu.VMEM((2, HEADS, HEAD_DIM, HEAD_DIM), F32), pltpu.VMEM((2, M_EYE, GROWS, GROWS), F32)],
        sem=("arbitrary",))
    outs, got = res if carry is not None else (res, None)
    paired = tuple((outs[i], outs[5 + i]) for i in range(5))
    return paired if carry is None else (paired, got)


def _gate_norm_fwd(o2, proj, dnw, *, name, ts=512):
    s = o2[0].shape[0]
    hd_all = HEADS * HEAD_DIM

    def body(of_ref, ob_ref, z_ref, w_ref, y_ref):
        w = w_ref[...]
        for hd in range(HEADS):
            cols = slice(hd * HEAD_DIM, (hd + 1) * HEAD_DIM)
            seg = of_ref[:, cols] + ob_ref[:, cols]
            r = lax.rsqrt(jnp.mean(seg * seg, axis=-1, keepdims=True) + EPS)
            z = z_ref[:, cols].astype(F32)
            y_ref[:, cols] = ((seg * r * w) * (z * _sigmoid(z))).astype(BF16)

    tile = pl.BlockSpec((ts, hd_all), lambda i: (i, 0))
    return pl.pallas_call(
        body, name=name, grid=(s // ts,),
        in_specs=[tile, tile, pl.BlockSpec((ts, Z_W), lambda i: (i, Z_OFF // Z_W)), _row(HEAD_DIM)],
        out_specs=tile,
        out_shape=jax.ShapeDtypeStruct((s, hd_all), BF16),
        compiler_params=_params(("parallel",)),
    )(o2[0], o2[1], proj, dnw)


def _gate_norm_bwd(dyb, o2, proj, dnw, *, name, ts=512):
    s = o2[0].shape[0]
    hd_all = HEADS * HEAD_DIM

    def body(dy_ref, of_ref, ob_ref, z_ref, w_ref, do_ref, dz_ref, dw_ref):
        w = w_ref[...]
        dw = jnp.zeros((1, HEAD_DIM), F32)
        for hd in range(HEADS):
            cols = slice(hd * HEAD_DIM, (hd + 1) * HEAD_DIM)
            seg = of_ref[:, cols] + ob_ref[:, cols]
            r = lax.rsqrt(jnp.mean(seg * seg, axis=-1, keepdims=True) + EPS)
            xhat = seg * r
            z = z_ref[:, cols].astype(F32)
            sg = _sigmoid(z)
            dy = dy_ref[:, cols]
            dnrm = dy * (z * sg)
            dz_ref[:, cols] = (dy * (xhat * w) * (sg * (1.0 + z * (1.0 - sg)))).astype(BF16)
            dw = dw + jnp.sum(dnrm * xhat, axis=0, keepdims=True)
            dxhat = dnrm * w
            do_ref[:, cols] = r * (dxhat - xhat * jnp.mean(dxhat * xhat, axis=-1, keepdims=True))

        @pl.when(pl.program_id(0) == 0)
        def _():
            dw_ref[...] = jnp.zeros_like(dw_ref)

        dw_ref[...] += dw

    tile = pl.BlockSpec((ts, hd_all), lambda i: (i, 0))
    return pl.pallas_call(
        body, name=name, grid=(s // ts,),
        in_specs=[tile, tile, tile, pl.BlockSpec((ts, Z_W), lambda i: (i, Z_OFF // Z_W)), _row(HEAD_DIM)],
        out_specs=[tile, tile, _row(HEAD_DIM)],
        out_shape=[jax.ShapeDtypeStruct((s, hd_all), F32), jax.ShapeDtypeStruct((s, hd_all), BF16),
                   jax.ShapeDtypeStruct((1, HEAD_DIM), F32)],
        compiler_params=_params(("arbitrary",)),
    )(dyb, o2[0], o2[1], proj, dnw)


def _merge_fwd(ya, yb, proj, wa, wb, wo, h, g, *, name, ts=512):
    s, d = h.shape

    def body(ya_ref, yb_ref, gt_ref, wa_ref, wb_ref, wo_ref, h_ref, g_ref, pa_ref, pb_ref, mix_ref, ho_ref):
        pa = _dot(ya_ref[...], wa_ref[...])
        pb = _dot(yb_ref[...], wb_ref[...])
        pa_ref[...] = pa.astype(BF16)
        pb_ref[...] = pb.astype(BF16)
        merged = (_sigmoid(gt_ref[:, :d].astype(F32)) * pa + _sigmoid(gt_ref[:, d:].astype(F32)) * pb)
        mix = _dot(merged.astype(BF16), wo_ref[...])
        mix_ref[...] = mix.astype(BF16)
        ho_ref[...] = h_ref[...] + g_ref[...] * mix

    tile = pl.BlockSpec((ts, d), lambda i: (i, 0))
    return pl.pallas_call(
        body, name=name, grid=(s // ts,),
        in_specs=[pl.BlockSpec((ts, CONV_A), lambda i: (i, 0)), tile,
                  pl.BlockSpec((ts, GATE_W), lambda i: (i, GATE_OFF // GATE_W)),
                  _resident(wa.shape), _resident(wb.shape), _resident(wo.shape), tile, _row(d)],
        out_specs=[tile, tile, tile, tile],
        out_shape=[jax.ShapeDtypeStruct((s, d), BF16)] * 3 + [jax.ShapeDtypeStruct((s, d), F32)],
        compiler_params=_params(("parallel",)),
    )(ya, yb, proj, wa, wb, wo, h, g)


def _merge_bwd(dh, g, mix, pa, pb, proj, wa, wb, wo, *, name, ts=256):
    s, d = dh.shape

    def body(dh_ref, g_ref, mix_ref, pa_ref, pb_ref, gt_ref, wa_ref, wb_ref, wo_ref,
             dmix_ref, mg_ref, dpa_ref, dpb_ref, dgt_ref, dya_ref, dyb_ref, dg_ref):
        dh_v = dh_ref[...]
        dmix = (g_ref[...] * dh_v).astype(BF16)
        dmix_ref[...] = dmix

        @pl.when(pl.program_id(0) == 0)
        def _():
            dg_ref[...] = jnp.zeros_like(dg_ref)

        dg_ref[...] += jnp.sum(dh_v * mix_ref[...].astype(F32), axis=0, keepdims=True)
        dmerged = _dot_nt(dmix, wo_ref[...])
        pa = pa_ref[...].astype(F32)
        pb = pb_ref[...].astype(F32)
        sa = _sigmoid(gt_ref[:, :d].astype(F32))
        sb = _sigmoid(gt_ref[:, d:].astype(F32))
        mg_ref[...] = (sa * pa + sb * pb).astype(BF16)
        dpa = (dmerged * sa).astype(BF16)
        dpb = (dmerged * sb).astype(BF16)
        dpa_ref[...] = dpa
        dpb_ref[...] = dpb
        dgt_ref[:, :d] = (dmerged * pa * sa * (1.0 - sa)).astype(BF16)
        dgt_ref[:, d:] = (dmerged * pb * sb * (1.0 - sb)).astype(BF16)
        dya_ref[...] = _dot_nt(dpa, wa_ref[...])
        dyb_ref[...] = _dot_nt(dpb, wb_ref[...])

    tile = pl.BlockSpec((ts, d), lambda i: (i, 0))
    return pl.pallas_call(
        body, name=name, grid=(s // ts,),
        in_specs=[tile, _row(d), tile, tile, tile,
                  pl.BlockSpec((ts, GATE_W), lambda i: (i, GATE_OFF // GATE_W)),
                  _resident(wa.shape), _resident(wb.shape), _resident(wo.shape)],
        out_specs=[tile, tile, tile, tile, pl.BlockSpec((ts, GATE_W), lambda i: (i, 0)),
                   pl.BlockSpec((ts, CONV_A), lambda i: (i, 0)), tile, _row(d)],
        out_shape=[jax.ShapeDtypeStruct((s, d), BF16)] * 4
                  + [jax.ShapeDtypeStruct((s, GATE_W), BF16), jax.ShapeDtypeStruct((s, CONV_A), F32),
                     jax.ShapeDtypeStruct((s, d), F32), jax.ShapeDtypeStruct((1, d), F32)],
        compiler_params=_params(("arbitrary",)),
    )(dh, g, mix, pa, pb, proj, wa, wb, wo)


def _final_fwd_bwd(h, nw, target, *, name, ts=512):
    s, d = h.shape

    def body(h_ref, nw_ref, t_ref, loss_ref, dh_ref, dnw_ref):
        x = h_ref[...]
        w = nw_ref[...]
        r = lax.rsqrt(jnp.mean(x * x, axis=-1, keepdims=True) + EPS)
        xhat = x * r
        e = xhat * w - t_ref[...]
        part = 0.5 * jnp.sum(jnp.mean(e * e, axis=-1, keepdims=True))
        dy = e * (1.0 / d)
        dxhat = dy * w
        dh_ref[...] = r * (dxhat - xhat * jnp.mean(dxhat * xhat, axis=-1, keepdims=True))

        @pl.when(pl.program_id(0) == 0)
        def _():
            loss_ref[...] = jnp.zeros_like(loss_ref)
            dnw_ref[...] = jnp.zeros_like(dnw_ref)

        loss_ref[...] += jnp.broadcast_to(part, loss_ref.shape)
        dnw_ref[...] += jnp.sum(dy * xhat, axis=0, keepdims=True)

    tile = pl.BlockSpec((ts, d), lambda i: (i, 0))
    return pl.pallas_call(
        body, name=name, grid=(s // ts,),
        in_specs=[tile, _row(d), tile],
        out_specs=[_row(128), tile, _row(d)],
        out_shape=[jax.ShapeDtypeStruct((1, 128), F32), jax.ShapeDtypeStruct((s, d), F32),
                   jax.ShapeDtypeStruct((1, d), F32)],
        compiler_params=_params(("arbitrary",)),
    )(h, nw, target)


EXT = 8


def _l2norm_heads_bwd(act, dout, scale):
    outs = []
    for hd in range(HEADS):
        cols = slice(hd * HEAD_DIM, (hd + 1) * HEAD_DIM)
        seg = act[:, cols]
        nrm = lax.rsqrt(jnp.sum(seg * seg, axis=-1, keepdims=True) + EPS)
        yhat = seg * nrm
        dsg = dout[:, cols]
        outs.append((scale * nrm) * (dsg - yhat * jnp.sum(yhat * dsg, axis=-1, keepdims=True)))
    return jnp.concatenate(outs, axis=-1)


def _conv_bwd(dq2, dk2, dv2, dya, proj, conv_dn, conv_a, *, name, ts=256, carry=None):
    s = proj.shape[0]
    hd = HEADS * HEAD_DIM
    nt = s // ts
    te = ts + 2 * EXT
    kdn, ka = conv_dn.shape[0], conv_a.shape[0]

    def body(*refs):
        (qp_ref, qc_ref, qn_ref, ap_ref, ac_ref, an_ref) = refs[0:6]
        d3 = refs[6:24]
        (yp_ref, yc_ref, yn_ref, wdn_ref, wa_ref) = refs[24:29]
        (dqkv_ref, da_ref, dwdn_ref, dwa_ref) = refs[29:33]
        xs_ref, dps_ref, xa_ref, dca_ref = refs[33:37]
        i = pl.program_id(0)
        first, last = i == 0, i == nt - 1

        @pl.when(first)
        def _():
            dwdn_ref[...] = jnp.zeros_like(dwdn_ref)
            dwa_ref[...] = jnp.zeros_like(dwa_ref)

        rowe = lax.broadcasted_iota(jnp.int32, (te, 1), 0)
        inside = ~((first & (rowe < EXT)) | (last & (rowe >= EXT + ts)))
        _fill_halo(xs_ref, qp_ref, qc_ref, qn_ref, first, last)
        wdn = wdn_ref[...]
        for part in range(3):
            cols = slice(part * hd, (part + 1) * hd)
            pre = _dwconv_rows(xs_ref, wdn[:, cols], HALO - EXT - 2, te, cols)
            sg = _sigmoid(pre)
            act = pre * sg
            pf, cf, nf, pb, cb, nb = d3[6 * part:6 * part + 6]
            dout = jnp.concatenate([pf[...] + pb[...], cf[...] + cb[...], nf[...] + nb[...]], axis=0)
            if part == 0:
                dact = _l2norm_heads_bwd(act, dout, Q_SCALE)
            elif part == 1:
                dact = _l2norm_heads_bwd(act, dout, 1.0)
            else:
                dact = dout
            dpre = jnp.where(inside, dact * (sg * (1.0 + pre * (1.0 - sg))), 0.0)
            dps_ref[:, cols] = dpre
            acc = wdn[0:1, cols] * dps_ref[EXT + 2:EXT + 2 + ts, cols]
            for tap in range(1, kdn):
                acc = acc + wdn[tap:tap + 1, cols] * dps_ref[EXT + 2 - tap:EXT + 2 - tap + ts, cols]
            dqkv_ref[:, cols] = acc.astype(BF16)
            dcur = dps_ref[EXT:EXT + ts, cols]
            for tap in range(kdn):
                dwdn_ref[tap:tap + 1, cols] += jnp.sum(
                    dcur * xs_ref[HALO - 2 + tap:HALO - 2 + tap + ts, cols], axis=0, keepdims=True)

        cv = lambda r: r[:, CONV_A:2 * CONV_A].astype(F32) * r[:, 2 * CONV_A:].astype(F32)
        _fill_halo(xa_ref, ap_ref, ac_ref, an_ref, first, last, fn=cv)
        wa = wa_ref[...]
        gate_b = jnp.concatenate([ap_ref[HALO - EXT:, 0:CONV_A], ac_ref[:, 0:CONV_A], an_ref[0:EXT, 0:CONV_A]],
                                 axis=0).astype(F32)
        dya_e = jnp.concatenate([yp_ref[...], yc_ref[...], yn_ref[...]], axis=0)
        dca_ref[...] = jnp.where(inside, dya_e * gate_b, 0.0)
        conv = _dwconv_rows(xa_ref, wa, HALO - 1, ts, slice(0, CONV_A))
        acc = wa[0:1, :] * dca_ref[EXT + 1:EXT + 1 + ts, :]
        for tap in range(1, ka):
            acc = acc + wa[tap:tap + 1, :] * dca_ref[EXT + 1 - tap:EXT + 1 - tap + ts, :]
        gc = ac_ref[:, CONV_A:2 * CONV_A].astype(F32)
        val = ac_ref[:, 2 * CONV_A:].astype(F32)
        da_ref[:, 0:CONV_A] = (yc_ref[...] * conv).astype(BF16)
        da_ref[:, CONV_A:2 * CONV_A] = (acc * val).astype(BF16)
        da_ref[:, 2 * CONV_A:] = (acc * gc).astype(BF16)
        dcur = dca_ref[EXT:EXT + ts, :]
        for tap in range(ka):
            dwa_ref[tap:tap + 1, :] += jnp.sum(
                dcur * xa_ref[HALO - 1 + tap:HALO - 1 + tap + ts, :], axis=0, keepdims=True)

    cot = [arr for pair in (dq2, dk2, dv2) for arr in pair for _ in range(3)]
    return _launch(
        body, carry, (proj, proj, proj, proj, proj, proj, *cot, dya, dya, dya, conv_dn, conv_a),
        name=name, grid=(nt,),
        in_specs=(_halo_specs(ts, QKV_W, 0, s) + _halo_specs(ts, A_W, A_OFF // A_W, s)
                  + _halo_specs(ts, hd, 0, s, rows=EXT) * 6 + _halo_specs(ts, CONV_A, 0, s, rows=EXT)
                  + [_resident(conv_dn.shape), _resident(conv_a.shape)]),
        out_specs=[pl.BlockSpec((ts, QKV_W), lambda i: (i, 0)), pl.BlockSpec((ts, A_W), lambda i: (i, 0)),
                   pl.BlockSpec((8, QKV_W), lambda i: (0, 0)), pl.BlockSpec((8, CONV_A), lambda i: (0, 0))],
        out_shape=[jax.ShapeDtypeStruct((s, QKV_W), BF16), jax.ShapeDtypeStruct((s, A_W), BF16),
                   jax.ShapeDtypeStruct((8, QKV_W), F32), jax.ShapeDtypeStruct((8, CONV_A), F32)],
        scratch_shapes=[pltpu.VMEM((ts + 2 * HALO, QKV_W), F32), pltpu.VMEM((te, QKV_W), F32),
                        pltpu.VMEM((ts + 2 * HALO, CONV_A), F32), pltpu.VMEM((te, CONV_A), F32)],
        sem=("arbitrary",))


IN_A = (0, 1536)
IN_QKV = (1536, 4608)
IN_Z = (4608, 5632)
IN_BA = 5632
IN_GATE = (5664, 7712)
IN_COLS = 7712
G_REPL = 4


def _split_w_in(w_in):
    sl = lambda ab: w_in[:, ab[0]:ab[1]]
    w_main = jnp.concatenate([sl(IN_QKV), sl(IN_Z), sl(IN_GATE), sl(IN_A)], axis=1)
    blocks = []
    for d in range(2):
        beta = w_in[:, IN_BA + 8 * d:IN_BA + 8 * d + 8]
        alpha = w_in[:, IN_BA + 16 + 8 * d:IN_BA + 24 + 8 * d]
        pad = jnp.zeros((w_in.shape[0], BA_W - 8 - 8 * G_REPL), w_in.dtype)
        blocks += [beta] + [alpha] * G_REPL + [pad]
    return w_main, jnp.concatenate(blocks, axis=1)


def _merge_dw_in(dw_qkv, dw_z, dw_gate, dw_a, dw_ba):
    ba = [dw_ba[:, 0:8], dw_ba[:, BA_W:BA_W + 8], dw_ba[:, 8:16], dw_ba[:, BA_W + 8:BA_W + 16]]
    return jnp.concatenate([dw_a, dw_qkv, dw_z] + ba + [dw_gate], axis=1)


def _decay_rows(a_log_fwd, dt_bias_fwd, a_log_bwd, dt_bias_bwd):
    def rows(f, b):
        out = []
        for vec in (f, b):
            vec = vec.reshape(HEADS)
            out.append(jnp.concatenate([jnp.zeros((8,), F32)] + [vec] * G_REPL
                                       + [jnp.zeros((BA_W - 8 - 8 * G_REPL,), F32)])[None])
        return jnp.stack(out)
    return rows(a_log_fwd, a_log_bwd), rows(dt_bias_fwd, dt_bias_bwd)


def _local_step(x, target, mod9, wt, comm):
    s, d = x.shape
    n = s // CHUNK
    wt = dict(wt)
    sh1, sc1, g1, sh2, sc2, g2, sh3, sc3, g3 = [mod9[i:i + 1] for i in range(9)]
    alog, dtb = _decay_rows(wt["a_log_fwd"], wt["dt_bias_fwd"], wt["a_log_bwd"], wt["dt_bias_bwd"])

    (u1, a1, b1, f1), got = comm.gather(
        ["w_ffn1_down", "w_in"],
        lambda c: _ffn_up_fwd(x, wt["norm_ffn1"], sc1, sh1, wt["w_ffn1_up"], name="ffn1_up", carry=c))
    wt.update(got)
    w_main, w_ba = _split_w_in(wt["w_in"])
    y1, h1 = _ffn_down_fwd(f1, wt["w_ffn1_down"], x, g1, name="ffn1_down")
    (u2, proj, ba), got = comm.gather(
        ["w_a_out", "w_b_out", "w_out", "w_ffn2_up", "w_ffn2_down"],
        lambda c: _in_proj_fwd(h1, wt["norm_mix"], sc2, sh2, w_main, w_ba, name="in_proj", carry=c))
    wt.update(got)
    scal = _scal_fwd(ba, alog, dtb, name="scal_fwd")
    grow = scal[:, :, L_G:L_G + 8].reshape(2, n, CHUNK, HEADS).transpose(0, 1, 3, 2).reshape(
        2, n, N_GROUPS, GROWS)
    q, k, v, ya = _conv_fwd(proj, wt["conv_dn"], wt["conv_a"], name="conv_fwd")
    o2, states, tinv, vn = _delta_fwd(q, k, v, scal, grow, name="delta_fwd")
    yb = _gate_norm_fwd(o2, proj, wt["dn_norm"], name="gate_norm_fwd")
    pa, pb, mix, h2 = _merge_fwd(ya, yb, proj, wt["w_a_out"], wt["w_b_out"], wt["w_out"], h1, g2,
                                 name="merge_fwd")
    u3, a3, b3, f3 = _ffn_up_fwd(h2, wt["norm_ffn2"], sc3, sh3, wt["w_ffn2_up"], name="ffn2_up")
    y3, h3 = _ffn_down_fwd(f3, wt["w_ffn2_down"], h2, g3, name="ffn2_down")
    loss, dh3, dnorm_final = _final_fwd_bwd(h3, wt["norm_final"], target, name="final")

    dy3, dab3, dg3 = _ffn_bwd_act(dh3, g3, y3, a3, b3, wt["w_ffn2_down"], name="ffn2_bwd_act")
    dh2, dn3, dsc3, dsh3 = _norm_mod_matmul_bwd([(dab3, wt["w_ffn2_up"])], h2, wt["norm_ffn2"], sc3, dh3,
                                                name="ffn2_bwd_up")
    gw = {}
    gw["w_ffn2_up"] = _matmul_tn(u3, dab3, name="dw_ffn2_up", tm=1024, tn=2816)
    gw["w_ffn2_down"] = _matmul_tn(f3, dy3, name="dw_ffn2_down", tm=1408, tn=1024)

    dmix, merged, dpa, dpb, dgates, dya, dyb, dg2 = _merge_bwd(
        dh2, g2, mix, pa, pb, proj, wt["w_a_out"], wt["w_b_out"], wt["w_out"], name="merge_bwd")
    gw["w_out"] = _matmul_tn(merged, dmix, name="dw_out", tm=1024, tn=1024)
    gw["w_a_out"] = _matmul_tn(ya, dpa, name="dw_a_out", tm=512, tn=1024)
    gw["w_b_out"] = _matmul_tn(yb, dpb, name="dw_b_out", tm=1024, tn=1024)
    do, dz, ddn = _gate_norm_bwd(dyb, o2, proj, wt["dn_norm"], name="gate_norm_bwd")
    recv = {}
    (dq2, dk2, dv2, dscal, drow), got = comm.scatter(
        {nm: gw.pop(nm) for nm in ("w_ffn2_up", "w_ffn2_down")},
        lambda c: _delta_bwd(q, k, v, scal, grow, states, tinv, vn, do, name="delta_bwd", carry=c))
    recv.update(got)
    drow_p = jnp.pad(jnp.stack(drow).reshape(2, n, HEADS, CHUNK).transpose(0, 1, 3, 2).reshape(2, s, HEADS),
                     ((0, 0), (0, 0), (L_G, BA_W - L_G - HEADS)))
    dba, dalog, ddtb = _scal_bwd(jnp.stack(dscal), drow_p, ba, alog, dtb, name="scal_bwd")
    (dqkv, dbr_a, dconv_dn, dconv_a), got = comm.scatter(
        {nm: gw.pop(nm) for nm in ("w_out", "w_a_out", "w_b_out")},
        lambda c: _conv_bwd(dq2, dk2, dv2, dya, proj, wt["conv_dn"], wt["conv_a"], name="conv_bwd", carry=c))
    recv.update(got)
    dw_in = _merge_dw_in(
        _matmul_tn(u2, dqkv, name="dw_in_qkv", tm=1024, tn=1536),
        _matmul_tn(u2, dz, name="dw_in_z", tm=1024, tn=1024),
        _matmul_tn(u2, dgates, name="dw_in_gate", tm=1024, tn=1024),
        _matmul_tn(u2, dbr_a, name="dw_in_a", tm=1024, tn=1536),
        _matmul_tn(u2, dba, name="dw_in_ba", tm=1024, tn=2 * BA_W))
    (dh1, dn2, dsc2, dsh2), got = comm.scatter(
        {"w_in": dw_in},
        lambda c: _norm_mod_matmul_bwd(
            [(dqkv, w_main, 0), (dz, w_main, Z_OFF // Z_W), (dgates, w_main, GATE_OFF // GATE_W),
             (dbr_a, w_main, A_OFF // A_W), (dba, w_ba)],
            h1, wt["norm_mix"], sc2, dh2, name="in_proj_bwd", carry=c))
    recv.update(got)

    dy1, dab1, dg1 = _ffn_bwd_act(dh1, g1, y1, a1, b1, wt["w_ffn1_down"], name="ffn1_bwd_act")
    dw_down1 = _matmul_tn(f1, dy1, name="dw_ffn1_down", tm=1408, tn=1024)
    dw_up1, got = comm.scatter(
        {"w_ffn1_down": dw_down1},
        lambda c: _matmul_tn(u1, dab1, name="dw_ffn1_up", tm=1024, tn=2816, carry=c))
    recv.update(got)
    (dx, dn1, dsc1, dsh1), got = comm.scatter(
        {"w_ffn1_up": dw_up1},
        lambda c: _norm_mod_matmul_bwd([(dab1, wt["w_ffn1_up"])], x, wt["norm_ffn1"], sc1, dh1,
                                       name="ffn1_bwd_up", carry=c))
    recv.update(got)

    small = {
        "mod": jnp.concatenate([dsh1, dsc1, dg1, dsh2, dsc2, dg2, dsh3, dsc3, dg3], axis=1),
        "norm_ffn1": dn1, "norm_mix": dn2, "norm_ffn2": dn3, "norm_final": dnorm_final,
        "a_log_fwd": dalog[0, :, L_G:L_G + 8], "dt_bias_fwd": ddtb[0, :, L_G:L_G + 8],
        "a_log_bwd": dalog[1, :, L_G:L_G + 8], "dt_bias_bwd": ddtb[1, :, L_G:L_G + 8],
        "dn_norm": ddn,
        "conv_a": dconv_a[0:3].reshape(1, -1), "conv_dn": dconv_dn[0:5].reshape(1, -1),
    }
    return loss, dx, recv, small


def _full_weight(name, g):
    if name in COL_SHARDED + CONV_SHARDED:
        return g.transpose(1, 0, 2).reshape(g.shape[1], -1)
    return g.reshape(-1, g.shape[-1])


def _grad_pieces(name, g):
    g = g.astype(BF16)
    if name in COL_SHARDED:
        return g.reshape(g.shape[0], N_DEV, -1).transpose(1, 0, 2)
    return g.reshape(N_DEV, -1, g.shape[-1])


class _MeshComm:
    def __init__(self, shards):
        self.shards = shards

    def _run(self, xs, carrier, name, gather):
        if carrier is None:
            return None, _exchange(xs, name=name, gather=gather)
        return carrier((xs, gather))

    def gather(self, names, carrier=None, name=None):
        outs, got = self._run([self.shards[nm] for nm in names], carrier, name, True)
        return outs, {nm: _full_weight(nm, g) for nm, g in zip(names, got)}

    def scatter(self, grads, carrier=None, name=None):
        names = list(grads)
        outs, got = self._run([_grad_pieces(nm, grads[nm]) for nm in names], carrier, name, False)
        return outs, dict(zip(names, got))


def _mod_fwd(c_all, w_ada, *, name):
    def body(c_ref, w_ref, o_ref):
        cv = c_ref[...]
        o_ref[...] = _dot3(cv * _sigmoid(cv), w_ref[...])

    return pl.pallas_call(
        body, name=name, out_shape=jax.ShapeDtypeStruct((c_all.shape[0], w_ada.shape[1]), F32),
        compiler_params=_params(),
    )(c_all, w_ada)


def _adamw_math(w, g, m, v):
    m_new = ADAM_B1 * m + (1.0 - ADAM_B1) * g
    v_new = ADAM_B2 * v + (1.0 - ADAM_B2) * (g * g)
    m_hat = m_new / (1.0 - ADAM_B1 ** ADAM_STEP)
    v_hat = v_new / (1.0 - ADAM_B2 ** ADAM_STEP)
    delta = -ADAM_LR * (m_hat / (jnp.sqrt(v_hat) + ADAM_EPS) + ADAM_WD * w)
    return delta, m_new, v_new


def _reduce_adamw(pieces, w, m, v, *, name, tr):
    r, c = w.shape

    def body(p_ref, w_ref, m_ref, v_ref, g_ref, d_ref, mo_ref, vo_ref):
        g = p_ref[0].astype(F32)
        for src in range(1, N_DEV):
            g = g + p_ref[src].astype(F32)
        g_ref[...] = g
        d_ref[...], mo_ref[...], vo_ref[...] = _adamw_math(w_ref[...], g, m_ref[...], v_ref[...])

    tile = pl.BlockSpec((tr, c), lambda i: (i, 0))
    return pl.pallas_call(
        body, name=name, grid=(r // tr,),
        in_specs=[pl.BlockSpec((N_DEV, tr, c), lambda i: (0, i, 0)), tile, tile, tile],
        out_specs=[tile] * 4, out_shape=[jax.ShapeDtypeStruct((r, c), F32)] * 4,
        compiler_params=_params(("parallel",)),
    )(pieces, w, m, v)


def _ada_grad_adamw(c_all_t, dmod_cols, w, m, v, *, name, tr=256):
    r, c = w.shape

    def body(c_ref, dm_ref, w_ref, m_ref, v_ref, g_ref, d_ref, mo_ref, vo_ref):
        cv = c_ref[...]
        act = cv * _sigmoid(cv)
        dm = dm_ref[...]
        g = act[:, 0:1] * dm[0:1, :]
        for b in range(1, N_DEV):
            g = g + act[:, b:b + 1] * dm[b:b + 1, :]
        g_ref[...] = g
        d_ref[...], mo_ref[...], vo_ref[...] = _adamw_math(w_ref[...], g, m_ref[...], v_ref[...])

    tile = pl.BlockSpec((tr, c), lambda i: (i, 0))
    return pl.pallas_call(
        body, name=name, grid=(r // tr,),
        in_specs=[pl.BlockSpec((tr, N_DEV), lambda i: (i, 0)), pl.BlockSpec((N_DEV, c), lambda i: (0, 0)),
                  tile, tile, tile],
        out_specs=[tile] * 4, out_shape=[jax.ShapeDtypeStruct((r, c), F32)] * 4,
        compiler_params=_params(("parallel",)),
    )(c_all_t, dmod_cols, w, m, v)


def _sum_rows(parts, *, name):
    def body(p_ref, o_ref):
        acc = p_ref[0:1, :]
        for src in range(1, N_DEV):
            acc = acc + p_ref[src:src + 1, :]
        o_ref[...] = acc

    return pl.pallas_call(
        body, name=name, out_shape=jax.ShapeDtypeStruct((1, parts.shape[1]), F32), compiler_params=_params(),
    )(parts)


def _adamw_rows(g, w, m, v, *, name):
    def body(g_ref, w_ref, m_ref, v_ref, d_ref, mo_ref, vo_ref):
        d_ref[...], mo_ref[...], vo_ref[...] = _adamw_math(w_ref[...], g_ref[...], m_ref[...], v_ref[...])

    return pl.pallas_call(
        body, name=name, out_shape=[jax.ShapeDtypeStruct(g.shape, F32)] * 3, compiler_params=_params(),
    )(g, w, m, v)


WEIGHTS = ["w_ada", "b_ada", "norm_ffn1", "w_ffn1_up", "w_ffn1_down", "norm_mix", "w_in", "conv_a", "conv_dn",
           "a_log_fwd", "dt_bias_fwd", "a_log_bwd", "dt_bias_bwd", "dn_norm", "w_a_out", "w_b_out", "w_out",
           "norm_ffn2", "w_ffn2_up", "w_ffn2_down", "norm_final"]
COL_SHARDED = ["w_ffn1_up", "w_in", "w_a_out", "w_ffn2_up"]
ROW_SHARDED = ["w_ffn1_down", "w_b_out", "w_out", "w_ffn2_down"]
CONV_SHARDED = ["conv_a", "conv_dn"]
REPLICATED = ["b_ada", "norm_ffn1", "norm_mix", "a_log_fwd", "dt_bias_fwd", "a_log_bwd", "dt_bias_bwd",
              "dn_norm", "norm_ffn2", "norm_final"]
SMALL_ORDER = ["mod", "norm_ffn1", "norm_mix", "norm_ffn2", "norm_final", "a_log_fwd", "dt_bias_fwd",
               "a_log_bwd", "dt_bias_bwd", "dn_norm", "conv_a", "conv_dn"]
REDUCE_ROWS = {"w_ffn1_up": 256, "w_in": 256, "w_a_out": 256, "w_ffn2_up": 256,
               "w_ffn1_down": 176, "w_b_out": 128, "w_out": 128, "w_ffn2_down": 176}


def _pad_lanes(row):
    pad = (-row.shape[1]) % 128
    return jnp.pad(row, ((0, 0), (0, pad)))


def kernel(x, c, w_ada, b_ada, norm_ffn1, w_ffn1_up, w_ffn1_down, norm_mix, w_in, conv_a, conv_dn, a_log_fwd, dt_bias_fwd, a_log_bwd, dt_bias_bwd, dn_norm, w_a_out, w_b_out, w_out, norm_ffn2, w_ffn2_up, w_ffn2_down, norm_final, loss_target, m_w_ada, m_b_ada, m_norm_ffn1, m_w_ffn1_up, m_w_ffn1_down, m_norm_mix, m_w_in, m_conv_a, m_conv_dn, m_a_log_fwd, m_dt_bias_fwd, m_a_log_bwd, m_dt_bias_bwd, m_dn_norm, m_w_a_out, m_w_b_out, m_w_out, m_norm_ffn2, m_w_ffn2_up, m_w_ffn2_down, m_norm_final, v_w_ada, v_b_ada, v_norm_ffn1, v_w_ffn1_up, v_w_ffn1_down, v_norm_mix, v_w_in, v_conv_a, v_conv_dn, v_a_log_fwd, v_dt_bias_fwd, v_a_log_bwd, v_dt_bias_bwd, v_dn_norm, v_w_a_out, v_w_b_out, v_w_out, v_norm_ffn2, v_w_ffn2_up, v_w_ffn2_down, v_norm_final):
    args = dict(locals())
    w_loc = {n: args[n] for n in WEIGHTS}
    m_loc = {n: args["m_" + n] for n in WEIGHTS}
    v_loc = {n: args["v_" + n] for n in WEIGHTS}
    me = _flat_index(_my_position())
    d_model = x.shape[-1]

    big = COL_SHARDED + ROW_SHARDED
    shards = {n: w_loc[n][0].astype(BF16) for n in big}
    shards.update({n: w_loc[n][0] for n in CONV_SHARDED})
    shards["c"] = c
    comm = _MeshComm(shards)
    wt = comm.gather(["c", "conv_a", "conv_dn", "w_ffn1_up"], name="gather_first")[1]
    c_all = wt.pop("c")
    for n in REPLICATED[1:]:
        wt[n] = w_loc[n].reshape(1, -1)

    mod_cols = _mod_fwd(c_all, w_ada[0], name="mod_fwd")
    mod_all = _exchange([mod_cols], name="gather_mod", gather=True)[0]
    mod_mine = lax.dynamic_index_in_dim(mod_all, me, axis=1, keepdims=False).reshape(1, -1) + b_ada
    mod9 = mod_mine.reshape(9, d_model)

    loss_loc, dx, recv, small = _local_step(x[0], loss_target[0], mod9, wt, comm)
    loss = lax.psum(loss_loc[0, 0], MESH_AXES)

    res = {}
    for n in big:
        res[n] = _reduce_adamw(recv[n], w_loc[n][0], m_loc[n][0], v_loc[n][0], name="adamw_" + n,
                               tr=REDUCE_ROWS[n])

    packed = _pad_lanes(jnp.concatenate([small[n].reshape(1, -1) for n in SMALL_ORDER], axis=1))
    parts = _exchange([packed], name="gather_small", gather=True)[0].reshape(N_DEV, -1)
    total = _sum_rows(parts, name="sum_small")
    off = 0
    gsmall = {}
    for n in SMALL_ORDER:
        size = small[n].size
        gsmall[n] = total[:, off:off + size]
        off += size
    dmod_all = parts[:, 0:9 * d_model]
    ada_cols = w_ada.shape[-1]
    dmod_cols = lax.dynamic_slice_in_dim(dmod_all, me * ada_cols, ada_cols, axis=1)
    res["w_ada"] = _ada_grad_adamw(c_all.T, dmod_cols, w_ada[0], m_w_ada[0], v_w_ada[0], name="adamw_w_ada")
    g_rows = {"b_ada": gsmall["mod"]}
    for n in REPLICATED[1:]:
        g_rows[n] = gsmall[n]
    for n in CONV_SHARDED:
        taps, width = w_loc[n].shape[1], w_loc[n].shape[2]
        full = gsmall[n].reshape(taps, -1)
        g_rows[n] = lax.dynamic_slice_in_dim(full, me * width, width, axis=1).reshape(1, -1)
    row_names = REPLICATED + CONV_SHARDED
    cat = lambda src: _pad_lanes(jnp.concatenate([src[n].reshape(1, -1) for n in row_names], axis=1))
    g_cat = cat(g_rows)
    d_cat, m_cat, v_cat = _adamw_rows(g_cat, cat(w_loc), cat(m_loc), cat(v_loc), name="adamw_small")
    off = 0
    for n in row_names:
        size = w_loc[n].size
        res[n] = tuple(t[:, off:off + size] for t in (g_cat, d_cat, m_cat, v_cat))
        off += size

    outs = [loss, dx[None]]
    for kind in range(4):
        for n in WEIGHTS:
            outs.append(res[n][kind].reshape(w_loc[n].shape))
    return tuple(outs)
```

```python
import functools
import math
import types

import jax
import jax.numpy as jnp
from jax import lax
from jax.experimental import pallas as pl
from jax.experimental.pallas import tpu as pltpu

F32 = jnp.float32
BF16 = jnp.bfloat16
EPS = 1e-6
N_DEV = 8
CHUNK = 64
HEADS = 8
HEAD_DIM = 128
MESH_AXES = ("x", "y", "c")
VMEM_LIMIT_BYTES = 56 * 1024 * 1024

ADAM_LR = 0.001
ADAM_B1 = 0.9
ADAM_B2 = 0.999
ADAM_EPS = 1e-08
ADAM_WD = 0.01
ADAM_STEP = 10


def _params(sem=None):
    return pltpu.CompilerParams(dimension_semantics=sem, vmem_limit_bytes=VMEM_LIMIT_BYTES)


def _row(n):
    return pl.BlockSpec((1, n), lambda *_: (0, 0))


def _resident(shape):
    nd = len(shape)
    return pl.BlockSpec(shape, lambda *_: (0,) * nd, pipeline_mode=pl.Buffered(1))


def _col_chunks(width, chunk=512):
    return [slice(lo, min(lo + chunk, width)) for lo in range(0, width, chunk)]


def _col_window(w, width, col_block):
    return pl.BlockSpec((w.shape[0], width), lambda *_: (0, col_block), pipeline_mode=pl.Buffered(1))


def _sigmoid(x):
    return 1.0 / (1.0 + jnp.exp(-x))


def _dot(a, b):
    return jnp.dot(a, b, preferred_element_type=F32)


def _dot_nt(a, b):
    return lax.dot_general(a, b, (((1,), (1,)), ((), ())), preferred_element_type=F32)


def _dot_tn(a, b):
    return lax.dot_general(a, b, (((0,), (0,)), ((), ())), preferred_element_type=F32)


def _split_bf16(x):
    hi = x.astype(BF16)
    lo = (x - hi.astype(F32)).astype(BF16)
    return hi, lo


def _dot3(a, b, dot=_dot):
    ah, al = a if isinstance(a, tuple) else _split_bf16(a)
    bh, bl = b if isinstance(b, tuple) else _split_bf16(b)
    return dot(ah, bh) + dot(ah, bl) + dot(al, bh)


def _dot_exact(a, b):
    return jnp.dot(a, b, preferred_element_type=F32, precision=lax.Precision.HIGHEST)


def _my_position():
    return tuple(lax.axis_index(a) for a in MESH_AXES)


def _peer(pos, kk):
    return tuple((1 - p) if (kk >> (2 - b)) & 1 else p for b, p in enumerate(pos))


def _flat_index(pos):
    return pos[0] * 4 + pos[1] * 2 + pos[2]


_ANY = pl.BlockSpec(memory_space=pl.ANY)


class _AllToAll:
    def __init__(self, in_refs, out_refs, send_sems, recv_sems, local_sems):
        pos = _my_position()
        me = _flat_index(pos)
        self.copies = []
        for t in range(len(in_refs)):
            self.copies.append(pltpu.make_async_copy(in_refs[t].at[me], out_refs[t].at[me], local_sems.at[t]))
            for kk in range(1, N_DEV):
                peer = _peer(pos, kk)
                self.copies.append(pltpu.make_async_remote_copy(
                    src_ref=in_refs[t].at[_flat_index(peer)], dst_ref=out_refs[t].at[me],
                    send_sem=send_sems.at[t, kk - 1], recv_sem=recv_sems.at[t, kk - 1],
                    device_id=peer, device_id_type=pl.DeviceIdType.MESH))

    def start(self):
        for cp in self.copies:
            cp.start()

    def finish(self):
        for cp in self.copies:
            cp.wait()


class _AllGather:
    def __init__(self, in_refs, out_refs, send_sems, recv_sems, local_sems):
        self.refs = (in_refs, out_refs, send_sems, recv_sems, local_sems)
        x, y, c = _my_position()
        self.me, self.sibling = (x, y, c), (x, y, 1 - c)
        self.chips = [(1 - x, y), (x, 1 - y), (1 - x, 1 - y)]
        self.core = c

    def _copy(self, t, k, block, to, own=False):
        in_refs, out_refs, send_sems, recv_sems, _ = self.refs
        rows = out_refs[t].at[_flat_index(block)]
        return pltpu.make_async_remote_copy(
            src_ref=in_refs[t] if own else rows, dst_ref=rows,
            send_sem=send_sems.at[t, k], recv_sem=recv_sems.at[t, k],
            device_id=to, device_id_type=pl.DeviceIdType.MESH)

    def _local(self, t):
        in_refs, out_refs, _, _, local_sems = self.refs
        return pltpu.make_async_copy(in_refs[t], out_refs[t].at[_flat_index(self.me)], local_sems.at[t])

    def start(self):
        c = self.core
        for t in range(len(self.refs[0])):
            self._local(t).start()
            self._copy(t, 0, self.me, self.sibling, own=True).start()
            for j, chip in enumerate(self.chips):
                self._copy(t, 1 + j, self.me, (*chip, c), own=True).start()

    def finish(self):
        c = self.core
        n_t = len(self.refs[0])
        for t in range(n_t):
            for j, chip in enumerate(self.chips):
                self._copy(t, 1 + j, (*chip, c), self.me).wait_recv()
                self._copy(t, 4 + j, (*chip, c), self.sibling).start()
        for t in range(n_t):
            self._copy(t, 0, self.sibling, self.me).wait_recv()
            for j, chip in enumerate(self.chips):
                self._copy(t, 4 + j, (*chip, 1 - c), self.me).wait_recv()
            self._copy(t, 0, self.me, self.sibling, own=True).wait_send()
            for j, chip in enumerate(self.chips):
                self._copy(t, 1 + j, self.me, (*chip, c), own=True).wait_send()
                self._copy(t, 4 + j, (*chip, c), self.sibling).wait_send()
            self._local(t).wait()


def _exchange_plan(in_refs, out_refs, send_sems, recv_sems, local_sems, gather):
    return (_AllGather if gather else _AllToAll)(in_refs, out_refs, send_sems, recv_sems, local_sems)


def _exchange_shapes(xs, gather):
    out_shape = [jax.ShapeDtypeStruct(((N_DEV,) + x.shape) if gather else x.shape, x.dtype) for x in xs]
    sems = [pltpu.SemaphoreType.DMA((len(xs), N_DEV - 1)), pltpu.SemaphoreType.DMA((len(xs), N_DEV - 1)),
            pltpu.SemaphoreType.DMA((len(xs),))]
    return out_shape, sems


def _exchange(xs, *, name, gather):
    nt = len(xs)

    def body(*refs):
        plan = _exchange_plan(refs[:nt], refs[nt:2 * nt], *refs[2 * nt:], gather)
        plan.start()
        plan.finish()

    out_shape, sems = _exchange_shapes(xs, gather)
    return pl.pallas_call(body, name=name, in_specs=[_ANY] * nt, out_specs=[_ANY] * nt, out_shape=out_shape,
                          scratch_shapes=sems)(*xs)


def _launch(body, carry, args, *, name, grid, in_specs, out_specs, out_shape, scratch_shapes=(), sem):
    single = not isinstance(out_shape, (list, tuple))
    out_specs = [out_specs] if single else list(out_specs)
    out_shape = [out_shape] if single else list(out_shape)
    if carry is None:
        outs = pl.pallas_call(body, name=name, grid=grid, in_specs=list(in_specs), out_specs=out_specs,
                              out_shape=out_shape, scratch_shapes=list(scratch_shapes),
                              compiler_params=_params(sem))(*args)
        return outs[0] if single else outs
    xs, gather = carry
    nt, n_in, n_out, n_scr = len(xs), len(args), len(out_shape), len(scratch_shapes)
    x_shape, sems = _exchange_shapes(xs, gather)

    def wrapped(*refs):
        c_in, x_in = refs[:n_in], refs[n_in:n_in + nt]
        c_out = refs[n_in + nt:n_in + nt + n_out]
        x_out = refs[n_in + nt + n_out:n_in + 2 * nt + n_out]
        scr = refs[n_in + 2 * nt + n_out:]
        ids = [pl.program_id(a) for a in range(len(grid))]
        first = functools.reduce(jnp.logical_and, [i == 0 for i in ids])
        last = functools.reduce(jnp.logical_and, [i == g - 1 for i, g in zip(ids, grid)])
        plan = lambda: _exchange_plan(x_in, x_out, *scr[n_scr:], gather)

        @pl.when(first)
        def _():
            plan().start()

        body(*c_in, *c_out, *scr[:n_scr])

        @pl.when(last)
        def _():
            plan().finish()

    outs = pl.pallas_call(
        wrapped, name=name, grid=grid, in_specs=list(in_specs) + [_ANY] * nt,
        out_specs=out_specs + [_ANY] * nt, out_shape=out_shape + x_shape,
        scratch_shapes=list(scratch_shapes) + sems,
        compiler_params=_params(("arbitrary",) * len(grid)))(*args, *xs)
    compute = outs[:n_out]
    return (compute[0] if single else compute), outs[n_out:]


def _norm_mod(x, nw, sc, sh):
    r = lax.rsqrt(jnp.mean(x * x, axis=-1, keepdims=True) + EPS)
    return (x * r * nw) * (1.0 + sc) + sh


def _norm_mod_bwd(x, nw, sc, du):
    r = lax.rsqrt(jnp.mean(x * x, axis=-1, keepdims=True) + EPS)
    xhat = x * r
    n = xhat * nw
    dsh = jnp.sum(du, axis=0, keepdims=True)
    dsc = jnp.sum(du * n, axis=0, keepdims=True)
    dn = du * (1.0 + sc)
    dnw = jnp.sum(dn * xhat, axis=0, keepdims=True)
    dxhat = dn * nw
    dx = r * (dxhat - xhat * jnp.mean(dxhat * xhat, axis=-1, keepdims=True))
    return dx, dnw, dsc, dsh


def _ffn_up_fwd(h, nw, sc, sh, wup, *, name, ts=1024, tn=1408, carry=None):
    s, d = h.shape
    ts = min(ts, s)
    f_dim = wup.shape[1] // 2
    nj = f_dim // tn

    def body(h_ref, nw_ref, sc_ref, sh_ref, wa_ref, wb_ref, u_ref, a_ref, b_ref, f_ref):
        @pl.when(pl.program_id(1) == 0)
        def _():
            u_ref[...] = _norm_mod(h_ref[...], nw_ref[...], sc_ref[...], sh_ref[...]).astype(BF16)

        u = u_ref[...]

        def epilogue(a, b, cs):
            a_ref[:, cs] = a.astype(BF16)
            b_ref[:, cs] = b.astype(BF16)
            f_ref[:, cs] = (a * _sigmoid(a) * b).astype(BF16)

        pending = None
        for cs in _col_chunks(tn):
            products = (_dot(u, wa_ref[:, cs]), _dot(u, wb_ref[:, cs]), cs)
            if pending is not None:
                epilogue(*pending)
            pending = products
        epilogue(*pending)

    return _launch(
        body, carry, (h, nw, sc, sh, wup, wup), name=name, grid=(s // ts, nj),
        in_specs=[pl.BlockSpec((ts, d), lambda i, j: (i, 0)), _row(d), _row(d), _row(d),
                  pl.BlockSpec((d, tn), lambda i, j: (0, j)),
                  pl.BlockSpec((d, tn), lambda i, j: (0, j + nj))],
        out_specs=[pl.BlockSpec((ts, d), lambda i, j: (i, 0)),
                   pl.BlockSpec((ts, tn), lambda i, j: (i, j)),
                   pl.BlockSpec((ts, tn), lambda i, j: (i, j)),
                   pl.BlockSpec((ts, tn), lambda i, j: (i, j))],
        out_shape=[jax.ShapeDtypeStruct((s, d), BF16)] + [jax.ShapeDtypeStruct((s, f_dim), BF16)] * 3,
        sem=("parallel", "arbitrary"))


def _ffn_down_fwd(f, wd, h, g, *, name, ts=512):
    s, f_dim = f.shape
    d = wd.shape[1]

    def body(f_ref, wd_ref, h_ref, g_ref, y_ref, ho_ref):
        y = _dot(f_ref[...], wd_ref[...])
        y_ref[...] = y.astype(BF16)
        ho_ref[...] = h_ref[...] + (0.5 * g_ref[...]) * y

    return pl.pallas_call(
        body, name=name, grid=(s // ts,),
        in_specs=[pl.BlockSpec((ts, f_dim), lambda i: (i, 0)), _resident((f_dim, d)),
                  pl.BlockSpec((ts, d), lambda i: (i, 0)), _row(d)],
        out_specs=[pl.BlockSpec((ts, d), lambda i: (i, 0)), pl.BlockSpec((ts, d), lambda i: (i, 0))],
        out_shape=[jax.ShapeDtypeStruct((s, d), BF16), jax.ShapeDtypeStruct((s, d), F32)],
        compiler_params=_params(("parallel",)),
    )(f, wd, h, g)


def _ffn_bwd_act(dh, g, y, a, b, wd, *, name, ts=256, carry=None):
    s, d = dh.shape
    f_dim = a.shape[1]

    def body(dh_ref, g_ref, y_ref, a_ref, b_ref, wd_ref, dy_ref, dab_ref, dg_ref):
        dh_v = dh_ref[...]
        dy = ((0.5 * g_ref[...]) * dh_v).astype(BF16)
        dy_ref[...] = dy
        part = jnp.sum(0.5 * dh_v * y_ref[...].astype(F32), axis=0, keepdims=True)

        @pl.when(pl.program_id(0) == 0)
        def _():
            dg_ref[...] = jnp.zeros_like(dg_ref)

        dg_ref[...] += part

        def epilogue(df, cs):
            av = a_ref[:, cs].astype(F32)
            bv = b_ref[:, cs].astype(F32)
            sg = _sigmoid(av)
            dab_ref[:, cs] = (df * bv * (sg * (1.0 + av * (1.0 - sg)))).astype(BF16)
            dab_ref[:, slice(f_dim + cs.start, f_dim + cs.stop)] = (df * (av * sg)).astype(BF16)

        pending = None
        for cs in _col_chunks(f_dim):
            product = (_dot_nt(dy, wd_ref[cs, :]), cs)
            if pending is not None:
                epilogue(*pending)
            pending = product
        epilogue(*pending)

    return _launch(
        body, carry, (dh, g, y, a, b, wd), name=name, grid=(s // ts,),
        in_specs=[pl.BlockSpec((ts, d), lambda i: (i, 0)), _row(d),
                  pl.BlockSpec((ts, d), lambda i: (i, 0)),
                  pl.BlockSpec((ts, f_dim), lambda i: (i, 0)),
                  pl.BlockSpec((ts, f_dim), lambda i: (i, 0)),
                  _resident((f_dim, d))],
        out_specs=[pl.BlockSpec((ts, d), lambda i: (i, 0)),
                   pl.BlockSpec((ts, 2 * f_dim), lambda i: (i, 0)), _row(d)],
        out_shape=[jax.ShapeDtypeStruct((s, d), BF16), jax.ShapeDtypeStruct((s, 2 * f_dim), BF16),
                   jax.ShapeDtypeStruct((1, d), F32)],
        sem=("arbitrary",))


def _norm_mod_matmul_bwd(pairs, h, nw, sc, dh_in, *, name, ts=256, carry=None):
    s, d = h.shape
    n_pairs = len(pairs)

    def body(*refs):
        dx_refs = refs[:n_pairs]
        w_refs = refs[n_pairs:2 * n_pairs]
        h_ref, nw_ref, sc_ref, dhi_ref, dho_ref, dnw_ref, dsc_ref, dsh_ref = refs[2 * n_pairs:]
        du = _dot_nt(dx_refs[0][...], w_refs[0][...])
        for k in range(1, n_pairs):
            du = du + _dot_nt(dx_refs[k][...], w_refs[k][...])
        dx, dnw, dsc, dsh = _norm_mod_bwd(h_ref[...], nw_ref[...], sc_ref[...], du)
        dho_ref[...] = dhi_ref[...] + dx

        @pl.when(pl.program_id(0) == 0)
        def _():
            dnw_ref[...] = jnp.zeros_like(dnw_ref)
            dsc_ref[...] = jnp.zeros_like(dsc_ref)
            dsh_ref[...] = jnp.zeros_like(dsh_ref)

        dnw_ref[...] += dnw
        dsc_ref[...] += dsc
        dsh_ref[...] += dsh

    dxs = [p[0] for p in pairs]
    ws = [p[1] for p in pairs]
    tile = pl.BlockSpec((ts, d), lambda i: (i, 0))
    return _launch(
        body, carry, (*dxs, *ws, h, nw, sc, dh_in), name=name, grid=(s // ts,),
        in_specs=([pl.BlockSpec((ts, x.shape[1]), lambda i: (i, 0)) for x in dxs]
                  + [_col_window(w, x.shape[1], p[2] if len(p) > 2 else 0) for p, x, w in zip(pairs, dxs, ws)]
                  + [tile, _row(d), _row(d), tile]),
        out_specs=[tile, _row(d), _row(d), _row(d)],
        out_shape=[jax.ShapeDtypeStruct((s, d), F32)] + [jax.ShapeDtypeStruct((1, d), F32)] * 3,
        sem=("arbitrary",))


def _matmul_tn(a, b, *, name, tm, tn, tk=1024, carry=None):
    s, m = a.shape
    n = b.shape[1]
    tk = min(tk, s)
    nk = s // tk

    def body(a_ref, b_ref, o_ref, acc_ref):
        k = pl.program_id(2)

        @pl.when(k == 0)
        def _():
            acc_ref[...] = jnp.zeros_like(acc_ref)

        acc_ref[...] += _dot_tn(a_ref[...], b_ref[...])

        @pl.when(k == nk - 1)
        def _():
            o_ref[...] = acc_ref[...].astype(o_ref.dtype)

    return _launch(
        body, carry, (a, b), name=name, grid=(m // tm, n // tn, nk),
        in_specs=[pl.BlockSpec((tk, tm), lambda i, j, k: (k, i)),
                  pl.BlockSpec((tk, tn), lambda i, j, k: (k, j))],
        out_specs=pl.BlockSpec((tm, tn), lambda i, j, k: (i, j)),
        out_shape=jax.ShapeDtypeStruct((m, n), BF16),
        scratch_shapes=[pltpu.VMEM((tm, tn), F32)],
        sem=("parallel", "parallel", "arbitrary"))


def _in_proj_fwd(h, nw, sc, sh, w_main, w_ba, *, name, ts=1024, tn=1536, carry=None):
    s, d = h.shape
    ts = min(ts, s)
    n_main = w_main.shape[1]
    n_ba = w_ba.shape[1]

    def body(h_ref, nw_ref, sc_ref, sh_ref, w_ref, wba_ref, u_ref, p_ref, ba_ref):
        @pl.when(pl.program_id(1) == 0)
        def _():
            u0 = _norm_mod(h_ref[...], nw_ref[...], sc_ref[...], sh_ref[...]).astype(BF16)
            u_ref[...] = u0
            ba_ref[...] = _dot(u0, wba_ref[...])

        p_ref[...] = _dot(u_ref[...], w_ref[...]).astype(BF16)

    return _launch(
        body, carry, (h, nw, sc, sh, w_main, w_ba), name=name, grid=(s // ts, n_main // tn),
        in_specs=[pl.BlockSpec((ts, d), lambda i, j: (i, 0)), _row(d), _row(d), _row(d),
                  pl.BlockSpec((d, tn), lambda i, j: (0, j)), _resident((d, n_ba))],
        out_specs=[pl.BlockSpec((ts, d), lambda i, j: (i, 0)),
                   pl.BlockSpec((ts, tn), lambda i, j: (i, j)),
                   pl.BlockSpec((ts, n_ba), lambda i, j: (i, 0))],
        out_shape=[jax.ShapeDtypeStruct((s, d), BF16), jax.ShapeDtypeStruct((s, n_main), BF16),
                   jax.ShapeDtypeStruct((s, n_ba), F32)],
        sem=("parallel", "arbitrary"))


QKV_W = 3 * HEADS * HEAD_DIM
Z_OFF, Z_W = 3072, 1024
GATE_OFF, GATE_W = 4096, 2048
A_OFF, A_W = 6144, 1536
N_MAIN = 7680
CONV_A = 512
BA_W = 128

L_BETA, L_G, L_EG, L_EKD, L_EGC = 0, 8, 16, 24, 32


def _softplus(z):
    e = jnp.exp(-jnp.abs(z))
    small = e * (1.0 - e * (0.5 - e * (1.0 / 3.0)))
    return jnp.maximum(z, 0.0) + jnp.where(e < 1e-3, small, jnp.log(1.0 + e))


def _tri(n, sgn, strict=False):
    i = lax.broadcasted_iota(jnp.int32, (n, n), 0)
    j = lax.broadcasted_iota(jnp.int32, (n, n), 1)
    dlt = (i - j) * sgn
    return (dlt > 0) if strict else (dlt >= 0)


def _scal_fwd(ba, alog, dtb, *, name, ts=512):
    s = ba.shape[0]

    def body(ba_ref, al_ref, dt_ref, o_ref):
        d = pl.program_id(0)
        sgn = 1 - 2 * d
        x = ba_ref[...]
        lane = lax.broadcasted_iota(jnp.int32, x.shape, 1)
        beta = _sigmoid(x)
        g = -jnp.exp(al_ref[0]) * _softplus(x + dt_ref[0])
        g = jnp.where((lane >= L_G) & (lane < L_EGC + 8), g, 0.0)
        ltri = jnp.where(_tri(CHUNK, sgn), 1.0, 0.0).astype(F32)
        for c in range(ts // CHUNK):
            rows = slice(c * CHUNK, (c + 1) * CHUNK)
            gc = _dot_exact(ltri, g[rows])
            g_end = jnp.where(d == 0, gc[CHUNK - 1:CHUNK], gc[0:1])
            ln = lane[rows]
            out = jnp.where(ln < L_G, beta[rows],
                  jnp.where(ln < L_EG, gc,
                  jnp.where(ln < L_EKD, jnp.exp(gc),
                  jnp.where(ln < L_EGC, jnp.exp(g_end - gc),
                  jnp.where(ln < L_EGC + 8, jnp.broadcast_to(jnp.exp(g_end), gc.shape), 0.0)))))
            o_ref[0, rows, :] = out

    return pl.pallas_call(
        body, name=name, grid=(2, s // ts),
        in_specs=[pl.BlockSpec((ts, BA_W), lambda d, i: (i, d)),
                  pl.BlockSpec((1, 1, BA_W), lambda d, i: (d, 0, 0)),
                  pl.BlockSpec((1, 1, BA_W), lambda d, i: (d, 0, 0))],
        out_specs=pl.BlockSpec((1, ts, BA_W), lambda d, i: (d, i, 0)),
        out_shape=jax.ShapeDtypeStruct((2, s, BA_W), F32),
        compiler_params=_params(("parallel", "parallel")),
    )(ba, alog, dtb)


def _scal_bwd(dscal, drow, ba, alog, dtb, *, name, ts=512):
    s = ba.shape[0]

    def body(ds_ref, dr_ref, ba_ref, al_ref, dt_ref, dba_ref, dal_ref, ddt_ref):
        d = pl.program_id(0)
        sgn = 1 - 2 * d
        x = ba_ref[...]
        lane = lax.broadcasted_iota(jnp.int32, x.shape, 1)
        in_g = (lane >= L_G) & (lane < L_G + 8)
        beta = _sigmoid(x)
        z = x + dt_ref[0]
        neg_a = -jnp.exp(al_ref[0])
        g = neg_a * _softplus(z)
        dsv = ds_ref[0]
        dgc = jnp.where(in_g, dsv + dr_ref[0], 0.0)
        utri = jnp.where(_tri(CHUNK, -sgn), 1.0, 0.0).astype(F32)
        dal = jnp.zeros((1, BA_W), F32)
        ddt = jnp.zeros((1, BA_W), F32)
        for c in range(ts // CHUNK):
            rows = slice(c * CHUNK, (c + 1) * CHUNK)
            dg = _dot_exact(utri, dgc[rows])
            dz = dg * neg_a * _sigmoid(z[rows])
            dal = dal + jnp.sum(dg * g[rows], axis=0, keepdims=True)
            ddt = ddt + jnp.sum(dz, axis=0, keepdims=True)
            b = beta[rows]
            out = jnp.where(lane[rows] < L_G, dsv[rows] * b * (1.0 - b), jnp.where(in_g[rows], dz, 0.0))
            dba_ref[rows, :] = out.astype(BF16)

        @pl.when(pl.program_id(1) == 0)
        def _():
            dal_ref[...] = jnp.zeros_like(dal_ref)
            ddt_ref[...] = jnp.zeros_like(ddt_ref)

        dal_ref[0] += dal
        ddt_ref[0] += ddt

    row3 = pl.BlockSpec((1, 1, BA_W), lambda d, i: (d, 0, 0))
    tok3 = pl.BlockSpec((1, ts, BA_W), lambda d, i: (d, i, 0))
    return pl.pallas_call(
        body, name=name, grid=(2, s // ts),
        in_specs=[tok3, tok3, pl.BlockSpec((ts, BA_W), lambda d, i: (i, d)), row3, row3],
        out_specs=[pl.BlockSpec((ts, BA_W), lambda d, i: (i, d)), row3, row3],
        out_shape=[jax.ShapeDtypeStruct((s, 2 * BA_W), BF16), jax.ShapeDtypeStruct((2, 1, BA_W), F32),
                   jax.ShapeDtypeStruct((2, 1, BA_W), F32)],
        compiler_params=_params(("arbitrary", "arbitrary")),
    )(dscal, drow, ba, alog, dtb)


HALO = 16


def _halo_specs(ts, width, col_block, n_rows, rows=HALO):
    r = ts // rows
    last = n_rows // rows - 1
    return [pl.BlockSpec((rows, width), lambda i: (jnp.maximum(i * r - 1, 0), col_block)),
            pl.BlockSpec((ts, width), lambda i: (i, col_block)),
            pl.BlockSpec((rows, width), lambda i: (jnp.minimum((i + 1) * r, last), col_block))]


def _fill_halo(dst_ref, prev_ref, cur_ref, next_ref, first, last, fn=lambda r: r[...].astype(F32)):
    h = prev_ref.shape[0]
    ts = cur_ref.shape[0]
    p = fn(prev_ref)
    n = fn(next_ref)
    dst_ref[0:h, :] = jnp.where(first, 0.0, p)
    dst_ref[h:h + ts, :] = fn(cur_ref)
    dst_ref[h + ts:h + ts + h, :] = jnp.where(last, 0.0, n)


def _dwconv_rows(src_ref, w, start, n_rows, cols):
    acc = w[0:1, :] * src_ref[start:start + n_rows, cols]
    for i in range(1, w.shape[0]):
        acc = acc + w[i:i + 1, :] * src_ref[start + i:start + i + n_rows, cols]
    return acc


def _l2norm_heads(act, scale):
    outs = []
    for hd in range(HEADS):
        seg = act[:, hd * HEAD_DIM:(hd + 1) * HEAD_DIM]
        outs.append(seg * (lax.rsqrt(jnp.sum(seg * seg, axis=-1, keepdims=True) + EPS) * scale))
    return jnp.concatenate(outs, axis=-1)


Q_SCALE = HEAD_DIM ** -0.5


def _conv_fwd(proj, conv_dn, conv_a, *, name, ts=256):
    s = proj.shape[0]
    hd = HEADS * HEAD_DIM
    nt = s // ts

    def body(qp_ref, qc_ref, qn_ref, ap_ref, ac_ref, an_ref, wdn_ref, wa_ref,
             q_ref, k_ref, v_ref, ya_ref, xs_ref, xa_ref):
        i = pl.program_id(0)
        first, last = i == 0, i == nt - 1
        _fill_halo(xs_ref, qp_ref, qc_ref, qn_ref, first, last)
        wdn = wdn_ref[...]
        for part, o_ref in enumerate((q_ref, k_ref, v_ref)):
            cols = slice(part * hd, (part + 1) * hd)
            pre = _dwconv_rows(xs_ref, wdn[:, cols], HALO - 2, ts, cols)
            act = pre * _sigmoid(pre)
            if part == 0:
                act = _l2norm_heads(act, Q_SCALE)
            elif part == 1:
                act = _l2norm_heads(act, 1.0)
            o_ref[...] = act
        cv = lambda r: r[:, CONV_A:2 * CONV_A].astype(F32) * r[:, 2 * CONV_A:].astype(F32)
        _fill_halo(xa_ref, ap_ref, ac_ref, an_ref, first, last, fn=cv)
        conv = _dwconv_rows(xa_ref, wa_ref[...], HALO - 1, ts, slice(0, CONV_A))
        ya_ref[...] = (ac_ref[:, 0:CONV_A].astype(F32) * conv).astype(BF16)

    tile = lambda w: pl.BlockSpec((ts, w), lambda i: (i, 0))
    return pl.pallas_call(
        body, name=name, grid=(nt,),
        in_specs=(_halo_specs(ts, QKV_W, 0, s) + _halo_specs(ts, A_W, A_OFF // A_W, s)
                  + [_resident(conv_dn.shape), _resident(conv_a.shape)]),
        out_specs=[tile(hd), tile(hd), tile(hd), tile(CONV_A)],
        out_shape=[jax.ShapeDtypeStruct((s, hd), F32)] * 3 + [jax.ShapeDtypeStruct((s, CONV_A), BF16)],
        scratch_shapes=[pltpu.VMEM((ts + 2 * HALO, QKV_W), F32), pltpu.VMEM((ts + 2 * HALO, CONV_A), F32)],
        compiler_params=_params(("parallel",)),
    )(proj, proj, proj, proj, proj, proj, conv_dn, conv_a)


INV_BASE = 8


GROUP = 4
GROWS = GROUP * CHUNK
N_GROUPS = HEADS // GROUP


def _stack(parts):
    return jnp.concatenate(parts, axis=0)


M_INCL, M_STRICT, M_EYE, M_BASE, M_JOIN = 0, 1, 2, 3, 4
JOIN_SIZES = (16, 32, 64)
N_MASKS = M_JOIN + len(JOIN_SIZES)


def _write_group_masks(mask_ref, sgn, n_masks):
    i = lax.broadcasted_iota(jnp.int32, (GROWS, GROWS), 0)
    j = lax.broadcasted_iota(jnp.int32, (GROWS, GROWS), 1)
    same = lambda m: jnp.right_shift(i, int(math.log2(m))) == jnp.right_shift(j, int(math.log2(m)))
    dlt = (i - j) * sgn
    one = lambda cond: jnp.where(cond, 1.0, 0.0).astype(F32)
    mask_ref[M_INCL] = one(same(CHUNK) & (dlt >= 0))
    mask_ref[M_STRICT] = one(same(CHUNK) & (dlt > 0))
    if n_masks > M_EYE:
        mask_ref[M_EYE] = one(i == j)
        mask_ref[M_BASE] = one(same(INV_BASE))
        for lvl, m in enumerate(JOIN_SIZES):
            mask_ref[M_JOIN + lvl] = one(same(m) & jnp.logical_not(same(m // 2)))


def _group_decay(gcol, grow, mask_ref):
    return jnp.exp(jnp.minimum(gcol - grow, 0.0)) * mask_ref[M_INCL]


def _block_inverse_many(a_ms, mask_refs):
    xs = [-(a * m[M_BASE]) for a, m in zip(a_ms, mask_refs)]
    ts = [m[M_EYE] + x for x, m in zip(xs, mask_refs)]
    ps = xs
    for _ in range(int(math.log2(INV_BASE)) - 1):
        p_bs = [p.astype(BF16) for p in ps]
        ps = [_dot(p_b, p_b) for p_b in p_bs]
        ts = [t + _dot(t.astype(BF16), p.astype(BF16)) for t, p in zip(ts, ps)]
    for lvl in range(len(JOIN_SIZES)):
        t_bs = [t.astype(BF16) for t in ts]
        joins = [(a * m[M_JOIN + lvl]).astype(BF16) for a, m in zip(a_ms, mask_refs)]
        mids = [_dot(t_b, j).astype(BF16) for t_b, j in zip(t_bs, joins)]
        ts = [t - _dot(mid, t_b) for t, mid, t_b in zip(ts, mids, t_bs)]
    return ts


def _group_operands(q_ref, k_ref, v_ref, scv, grp):
    heads = [GROUP * grp + t for t in range(GROUP)]
    tiles = lambda ref: [ref[:, h * HEAD_DIM:(h + 1) * HEAD_DIM] for h in heads]
    col = lambda base: [scv[:, base + h:base + h + 1] for h in heads]
    egc = [scv[0:1, L_EGC + h:L_EGC + h + 1] for h in heads]
    return heads, tiles(q_ref), tiles(k_ref), tiles(v_ref), col(L_BETA), col(L_G), col(L_EG), col(L_EKD), egc


def _delta_fwd(q, k, v, scal, grow, *, name):
    s = q.shape[0]
    n = s // CHUNK
    hd_all = HEADS * HEAD_DIM

    def body(*refs):
        ins, outs, (state, mask_ref) = refs[:10], refs[10:18], refs[18:]

        @pl.when(pl.program_id(0) == 0)
        def _():
            state[...] = jnp.zeros_like(state)
            for d in range(2):
                _write_group_masks(mask_ref.at[d], 1 - 2 * d, N_MASKS)

        chains = []
        for d in range(2):
            q_ref, k_ref, v_ref, sc_ref, gr_ref = ins[5 * d:5 * d + 5]
            scv = sc_ref[0]
            for grp in range(N_GROUPS):
                chains.append(dict(
                    d=d, grp=grp, gr_ref=gr_ref, out=outs[4 * d:4 * d + 4], state=state.at[d], masks=mask_ref.at[d],
                    ops=_group_operands(q_ref, k_ref, v_ref, scv, grp)))
        for ch in chains:
            heads, qs, ks, vs, beta, gcol, eg, ekd, egc = ch["ops"]
            ch["dm"] = _group_decay(_stack(gcol), ch["gr_ref"][0, 0, ch["grp"]:ch["grp"] + 1, :], ch["masks"])
            ch["k_b"] = _stack(ks).astype(BF16)
            ch["kb_b"] = _stack([ks[t] * beta[t] for t in range(GROUP)]).astype(BF16)
        for ch in chains:
            ch["a_m"] = _dot_nt(ch["kb_b"], ch["k_b"]) * ch["dm"] * ch["masks"][M_STRICT]
        tinvs = _block_inverse_many([ch["a_m"] for ch in chains], [ch["masks"] for ch in chains])
        for ch, tinv in zip(chains, tinvs):
            heads, qs, ks, vs, beta, gcol, eg, ekd, egc = ch["ops"]
            o_ref, st_ref, t_ref, vn_ref = ch["out"]
            ch["tinv"] = tinv.astype(BF16)
            t_ref[0, ch["grp"]] = ch["tinv"]
            ch["p_b"] = (_dot_nt(_stack(qs).astype(BF16), ch["k_b"]) * ch["dm"]).astype(BF16)
            ch["sh"] = [ch["state"][h] for h in heads]
            ch["sh_b"] = [x.astype(BF16) for x in ch["sh"]]
            for t, h in enumerate(heads):
                st_ref[0, h] = ch["sh_b"][t]
        for ch in chains:
            heads, qs, ks, vs, beta, gcol, eg, ekd, egc = ch["ops"]
            ch["br"] = _stack([beta[t] * (vs[t] - _dot((ks[t] * eg[t]).astype(BF16), ch["sh_b"][t]))
                               for t in range(GROUP)]).astype(BF16)
        for ch in chains:
            ch["vn_b"] = _dot(ch["tinv"], ch["br"]).astype(BF16)
        for ch in chains:
            ch["o_intra"] = _dot(ch["p_b"], ch["vn_b"])
        for ch in chains:
            heads, qs, ks, vs, beta, gcol, eg, ekd, egc = ch["ops"]
            o_ref, st_ref, t_ref, vn_ref = ch["out"]
            for t, h in enumerate(heads):
                rows = slice(t * CHUNK, (t + 1) * CHUNK)
                cols = slice(h * HEAD_DIM, (h + 1) * HEAD_DIM)
                o_ref[:, cols] = _dot((qs[t] * eg[t]).astype(BF16), ch["sh_b"][t]) + ch["o_intra"][rows]
                ch["state"][h] = egc[t] * ch["sh"][t] + _dot_tn((ks[t] * ekd[t]).astype(BF16), ch["vn_b"][rows])
                vn_ref[:, cols] = ch["vn_b"][rows]

    at = [lambda c: c, lambda c: n - 1 - c]
    in_specs, out_specs = [], []
    for d in range(2):
        tok = pl.BlockSpec((CHUNK, hd_all), lambda c, d=d: (at[d](c), 0))
        in_specs += [tok] * 3 + [pl.BlockSpec((1, CHUNK, BA_W), lambda c, d=d: (d, at[d](c), 0)),
                                 pl.BlockSpec((1, 1, N_GROUPS, GROWS), lambda c, d=d: (d, at[d](c), 0, 0))]
        out_specs += [tok, pl.BlockSpec((1, HEADS, HEAD_DIM, HEAD_DIM), lambda c, d=d: (at[d](c), 0, 0, 0)),
                      pl.BlockSpec((1, N_GROUPS, GROWS, GROWS), lambda c, d=d: (at[d](c), 0, 0, 0)), tok]
    per_dir_shape = [jax.ShapeDtypeStruct((s, hd_all), F32),
                     jax.ShapeDtypeStruct((n, HEADS, HEAD_DIM, HEAD_DIM), BF16),
                     jax.ShapeDtypeStruct((n, N_GROUPS, GROWS, GROWS), BF16),
                     jax.ShapeDtypeStruct((s, hd_all), BF16)]
    outs = pl.pallas_call(
        body, name=name, grid=(n,), in_specs=in_specs, out_specs=out_specs, out_shape=per_dir_shape * 2,
        scratch_shapes=[pltpu.VMEM((2, HEADS, HEAD_DIM, HEAD_DIM), F32),
                        pltpu.VMEM((2, N_MASKS, GROWS, GROWS), F32)],
        compiler_params=_params(("arbitrary",)),
    )(*([q, k, v, scal, grow] * 2))
    return tuple((outs[i], outs[4 + i]) for i in range(4))


def _delta_bwd(q, k, v, scal, grow, states, tinv, vn, do, *, name, carry=None):
    s = q.shape[0]
    n = s // CHUNK
    hd_all = HEADS * HEAD_DIM

    grp_rows = [slice(t * CHUNK, (t + 1) * CHUNK) for t in range(GROUP)]
    per_head = lambda fn: _stack([fn(t) for t in range(GROUP)])

    def body(*refs):
        ins, outs, (dstate, mask_ref) = refs[:18], refs[18:28], refs[28:]

        @pl.when(pl.program_id(0) == 0)
        def _():
            dstate[...] = jnp.zeros_like(dstate)
            for d in range(2):
                _write_group_masks(mask_ref.at[d], 1 - 2 * d, M_EYE)

        chains = []
        for d in range(2):
            q_ref, k_ref, v_ref, sc_ref, gr_ref, st_ref, t_ref, vn_ref, do_ref = ins[9 * d:9 * d + 9]
            scv = sc_ref[0]
            for grp in range(N_GROUPS):
                c = types.SimpleNamespace(d=d, grp=grp, out=outs[5 * d:5 * d + 5], dstate=dstate.at[d],
                                          masks=mask_ref.at[d])
                (c.heads, qs, ks, c.vs, beta, gcol, eg, ekd, c.egc) = _group_operands(q_ref, k_ref, v_ref, scv, grp)
                c.cols = [slice(h * HEAD_DIM, (h + 1) * HEAD_DIM) for h in c.heads]
                c.dm = _group_decay(_stack(gcol), gr_ref[0, 0, grp:grp + 1, :], c.masks)
                c.dm_strict = c.dm * c.masks[M_STRICT]
                other = mask_ref.at[1 - d]
                c.dm_t = jnp.exp(jnp.minimum(gr_ref[0, 0, grp:grp + 1, :] - _stack(gcol), 0.0)) * other[M_INCL]
                c.dm_t_strict = c.dm_t * other[M_STRICT]
                c.beta, c.eg, c.ekd = _stack(beta), _stack(eg), _stack(ekd)
                c.q, c.k = _stack(qs), _stack(ks)
                c.q_b, c.k_b = c.q.astype(BF16), c.k.astype(BF16)
                c.kb_b = (c.k * c.beta).astype(BF16)
                c.kg, c.qg, c.kd = c.k * c.eg, c.q * c.eg, c.k * c.ekd
                c.kg_b, c.qg_b, c.kd_b = c.kg.astype(BF16), c.qg.astype(BF16), c.kd.astype(BF16)
                c.vn_b = _stack([vn_ref[:, cc] for cc in c.cols])
                c.do_b = _stack([do_ref[:, cc] for cc in c.cols]).astype(BF16)
                c.sh_b = [st_ref[0, h] for h in c.heads]
                c.dsp = [c.dstate[h] for h in c.heads]
                c.dsp_b = [x.astype(BF16) for x in c.dsp]
                c.t_b = t_ref[0, grp]
                chains.append(c)
        for c in chains:
            c.kk = _dot_nt(c.kb_b, c.k_b)
            c.qk = _dot_nt(c.q_b, c.k_b)
            c.pt_b = (_dot_nt(c.k_b, c.q_b) * c.dm_t).astype(BF16)
        for c in chains:
            c.r = per_head(lambda t: c.vs[t] - _dot(c.kg_b[grp_rows[t]], c.sh_b[t]))
            c.kd_ds = per_head(lambda t: _dot(c.kd_b[grp_rows[t]], c.dsp_b[t]))
        for c in chains:
            c.dvn_b = (_dot(c.pt_b, c.do_b) + c.kd_ds).astype(BF16)
        for c in chains:
            c.db = _dot_tn(c.t_b, c.dvn_b)
        for c in chains:
            c.dr = c.db * c.beta
            c.dbeta = jnp.sum(c.db * c.r, axis=-1, keepdims=True)
            c.dr_b, c.db_b = c.dr.astype(BF16), c.db.astype(BF16)
        for c in chains:
            c.dkg = -per_head(lambda t: _dot_nt(c.dr_b[grp_rows[t]], c.sh_b[t]))
            c.dqg = per_head(lambda t: _dot_nt(c.do_b[grp_rows[t]], c.sh_b[t]))
            c.dkd = per_head(lambda t: _dot_nt(c.vn_b[grp_rows[t]], c.dsp_b[t]))
        for c in chains:
            c.dpm = _dot_nt(c.do_b, c.vn_b) * c.dm
            c.dam = -_dot_nt(c.db_b, c.vn_b) * c.dm_strict
            c.dpm_b, c.dam_b = c.dpm.astype(BF16), c.dam.astype(BF16)
            c.dpm_t_b = (_dot_nt(c.vn_b, c.do_b) * c.dm_t).astype(BF16)
            c.dam_t_b = (-_dot_nt(c.vn_b, c.db_b) * c.dm_t_strict).astype(BF16)
        for c in chains:
            c.dkb = _dot(c.dam_b, c.k_b)
            c.dq = c.dqg * c.eg + _dot(c.dpm_b, c.k_b)
        for c in chains:
            c.dk = (c.dkg * c.eg + c.dkd * c.ekd + _dot(c.dpm_t_b, c.q_b) + _dot(c.dam_t_b, c.kb_b)
                    + c.dkb * c.beta)
        lane = lax.broadcasted_iota(jnp.int32, (CHUNK, BA_W), 1)
        row = lax.broadcasted_iota(jnp.int32, (CHUNK, 1), 0)
        dsc_acc = [jnp.zeros((CHUNK, BA_W), F32) for _ in range(2)]
        for c in chains:
            dq_ref, dk_ref, dv_ref, dsc_ref, dgr_ref = c.out
            end_row = CHUNK - 1 if c.d == 0 else 0
            dbeta = c.dbeta + jnp.sum(c.dkb * c.k, axis=-1, keepdims=True)
            m = c.dpm * c.qk + c.dam * c.kk
            kd_term = jnp.sum(c.dkd * c.kd, axis=-1, keepdims=True)
            dgcol = (jnp.sum(c.dqg * c.qg, axis=-1, keepdims=True) + jnp.sum(c.dkg * c.kg, axis=-1, keepdims=True)
                     - kd_term + jnp.sum(m, axis=-1, keepdims=True))
            dgr_ref[0, c.grp:c.grp + 1, :] = -jnp.sum(m, axis=0, keepdims=True)
            for t, h in enumerate(c.heads):
                rows, cols = grp_rows[t], c.cols[t]
                dq_ref[:, cols] = c.dq[rows]
                dk_ref[:, cols] = c.dk[rows]
                dv_ref[:, cols] = c.dr[rows]
                dg_end = jnp.sum(kd_term[rows]) + c.egc[t] * jnp.sum(c.dsp[t] * c.sh_b[t].astype(F32))
                dgcol_h = dgcol[rows] + jnp.where(row == end_row, dg_end, 0.0)
                dsc_acc[c.d] = jnp.where(lane == L_BETA + h, dbeta[rows], dsc_acc[c.d])
                dsc_acc[c.d] = jnp.where(lane == L_G + h, dgcol_h, dsc_acc[c.d])
                c.dstate[h] = (_dot_tn(c.qg_b[rows], c.do_b[rows]) + c.egc[t] * c.dsp[t]
                               - _dot_tn(c.kg_b[rows], c.dr_b[rows]))
        for d in range(2):
            outs[5 * d + 3][...] = dsc_acc[d]

    at = [lambda c: n - 1 - c, lambda c: c]
    in_specs, out_specs, args = [], [], []
    for d in range(2):
        tok = pl.BlockSpec((CHUNK, hd_all), lambda c, d=d: (at[d](c), 0))
        in_specs += [tok] * 3 + [pl.BlockSpec((1, CHUNK, BA_W), lambda c, d=d: (d, at[d](c), 0)),
                                 pl.BlockSpec((1, 1, N_GROUPS, GROWS), lambda c, d=d: (d, at[d](c), 0, 0)),
                                 pl.BlockSpec((1, HEADS, HEAD_DIM, HEAD_DIM), lambda c, d=d: (at[d](c), 0, 0, 0)),
                                 pl.BlockSpec((1, N_GROUPS, GROWS, GROWS), lambda c, d=d: (at[d](c), 0, 0, 0)),
                                 tok, tok]
        args += [q, k, v, scal, grow, states[d], tinv[d], vn[d], do]
        out_specs += [tok] * 3 + [pl.BlockSpec((CHUNK, BA_W), lambda c, d=d: (at[d](c), 0)),
                                  pl.BlockSpec((1, N_GROUPS, GROWS), lambda c, d=d: (at[d](c), 0, 0))]
    per_dir_shape = ([jax.ShapeDtypeStruct((s, hd_all), F32)] * 3
                     + [jax.ShapeDtypeStruct((s, BA_W), F32), jax.ShapeDtypeStruct((n, N_GROUPS, GROWS), F32)])
    res = _launch(
        body, carry, tuple(args), name=name, grid=(n,), in_specs=in_specs, out_specs=out_specs,
        out_shape=per_dir_shape * 2,
        scratch_shapes=[pltpu.VMEM((2, HEADS, HEAD_DIM, HEAD_DIM), F32), pltpu.VMEM((2, M_EYE, GROWS, GROWS), F32)],
        sem=("arbitrary",))
    outs, got = res if carry is not None else (res, None)
    paired = tuple((outs[i], outs[5 + i]) for i in range(5))
    return paired if carry is None else (paired, got)


def _gate_norm_fwd(o2, proj, dnw, *, name, ts=512):
    s = o2[0].shape[0]
    hd_all = HEADS * HEAD_DIM

    def body(of_ref, ob_ref, z_ref, w_ref, y_ref):
        w = w_ref[...]
        for hd in range(HEADS):
            cols = slice(hd * HEAD_DIM, (hd + 1) * HEAD_DIM)
            seg = of_ref[:, cols] + ob_ref[:, cols]
            r = lax.rsqrt(jnp.mean(seg * seg, axis=-1, keepdims=True) + EPS)
            z = z_ref[:, cols].astype(F32)
            y_ref[:, cols] = ((seg * r * w) * (z * _sigmoid(z))).astype(BF16)

    tile = pl.BlockSpec((ts, hd_all), lambda i: (i, 0))
    return pl.pallas_call(
        body, name=name, grid=(s // ts,),
        in_specs=[tile, tile, pl.BlockSpec((ts, Z_W), lambda i: (i, Z_OFF // Z_W)), _row(HEAD_DIM)],
        out_specs=tile,
        out_shape=jax.ShapeDtypeStruct((s, hd_all), BF16),
        compiler_params=_params(("parallel",)),
    )(o2[0], o2[1], proj, dnw)


def _gate_norm_bwd(dyb, o2, proj, dnw, *, name, ts=512):
    s = o2[0].shape[0]
    hd_all = HEADS * HEAD_DIM

    def body(dy_ref, of_ref, ob_ref, z_ref, w_ref, do_ref, dz_ref, dw_ref):
        w = w_ref[...]
        dw = jnp.zeros((1, HEAD_DIM), F32)
        for hd in range(HEADS):
            cols = slice(hd * HEAD_DIM, (hd + 1) * HEAD_DIM)
            seg = of_ref[:, cols] + ob_ref[:, cols]
            r = lax.rsqrt(jnp.mean(seg * seg, axis=-1, keepdims=True) + EPS)
            xhat = seg * r
            z = z_ref[:, cols].astype(F32)
            sg = _sigmoid(z)
            dy = dy_ref[:, cols]
            dnrm = dy * (z * sg)
            dz_ref[:, cols] = (dy * (xhat * w) * (sg * (1.0 + z * (1.0 - sg)))).astype(BF16)
            dw = dw + jnp.sum(dnrm * xhat, axis=0, keepdims=True)
            dxhat = dnrm * w
            do_ref[:, cols] = r * (dxhat - xhat * jnp.mean(dxhat * xhat, axis=-1, keepdims=True))

        @pl.when(pl.program_id(0) == 0)
        def _():
            dw_ref[...] = jnp.zeros_like(dw_ref)

        dw_ref[...] += dw

    tile = pl.BlockSpec((ts, hd_all), lambda i: (i, 0))
    return pl.pallas_call(
        body, name=name, grid=(s // ts,),
        in_specs=[tile, tile, tile, pl.BlockSpec((ts, Z_W), lambda i: (i, Z_OFF // Z_W)), _row(HEAD_DIM)],
        out_specs=[tile, tile, _row(HEAD_DIM)],
        out_shape=[jax.ShapeDtypeStruct((s, hd_all), F32), jax.ShapeDtypeStruct((s, hd_all), BF16),
                   jax.ShapeDtypeStruct((1, HEAD_DIM), F32)],
        compiler_params=_params(("arbitrary",)),
    )(dyb, o2[0], o2[1], proj, dnw)


def _merge_fwd(ya, yb, proj, wa, wb, wo, h, g, *, name, ts=512):
    s, d = h.shape

    def body(ya_ref, yb_ref, gt_ref, wa_ref, wb_ref, wo_ref, h_ref, g_ref, pa_ref, pb_ref, mix_ref, ho_ref):
        pa = _dot(ya_ref[...], wa_ref[...])
        pb = _dot(yb_ref[...], wb_ref[...])
        pa_ref[...] = pa.astype(BF16)
        pb_ref[...] = pb.astype(BF16)
        merged = (_sigmoid(gt_ref[:, :d].astype(F32)) * pa + _sigmoid(gt_ref[:, d:].astype(F32)) * pb)
        mix = _dot(merged.astype(BF16), wo_ref[...])
        mix_ref[...] = mix.astype(BF16)
        ho_ref[...] = h_ref[...] + g_ref[...] * mix

    tile = pl.BlockSpec((ts, d), lambda i: (i, 0))
    return pl.pallas_call(
        body, name=name, grid=(s // ts,),
        in_specs=[pl.BlockSpec((ts, CONV_A), lambda i: (i, 0)), tile,
                  pl.BlockSpec((ts, GATE_W), lambda i: (i, GATE_OFF // GATE_W)),
                  _resident(wa.shape), _resident(wb.shape), _resident(wo.shape), tile, _row(d)],
        out_specs=[tile, tile, tile, tile],
        out_shape=[jax.ShapeDtypeStruct((s, d), BF16)] * 3 + [jax.ShapeDtypeStruct((s, d), F32)],
        compiler_params=_params(("parallel",)),
    )(ya, yb, proj, wa, wb, wo, h, g)


def _merge_bwd(dh, g, mix, pa, pb, proj, wa, wb, wo, *, name, ts=256):
    s, d = dh.shape

    def body(dh_ref, g_ref, mix_ref, pa_ref, pb_ref, gt_ref, wa_ref, wb_ref, wo_ref,
             dmix_ref, mg_ref, dpa_ref, dpb_ref, dgt_ref, dya_ref, dyb_ref, dg_ref):
        dh_v = dh_ref[...]
        dmix = (g_ref[...] * dh_v).astype(BF16)
        dmix_ref[...] = dmix

        @pl.when(pl.program_id(0) == 0)
        def _():
            dg_ref[...] = jnp.zeros_like(dg_ref)

        dg_ref[...] += jnp.sum(dh_v * mix_ref[...].astype(F32), axis=0, keepdims=True)
        dmerged = _dot_nt(dmix, wo_ref[...])
        pa = pa_ref[...].astype(F32)
        pb = pb_ref[...].astype(F32)
        sa = _sigmoid(gt_ref[:, :d].astype(F32))
        sb = _sigmoid(gt_ref[:, d:].astype(F32))
        mg_ref[...] = (sa * pa + sb * pb).astype(BF16)
        dpa = (dmerged * sa).astype(BF16)
        dpb = (dmerged * sb).astype(BF16)
        dpa_ref[...] = dpa
        dpb_ref[...] = dpb
        dgt_ref[:, :d] = (dmerged * pa * sa * (1.0 - sa)).astype(BF16)
        dgt_ref[:, d:] = (dmerged * pb * sb * (1.0 - sb)).astype(BF16)
        dya_ref[...] = _dot_nt(dpa, wa_ref[...])
        dyb_ref[...] = _dot_nt(dpb, wb_ref[...])

    tile = pl.BlockSpec((ts, d), lambda i: (i, 0))
    return pl.pallas_call(
        body, name=name, grid=(s // ts,),
        in_specs=[tile, _row(d), tile, tile, tile,
                  pl.BlockSpec((ts, GATE_W), lambda i: (i, GATE_OFF // GATE_W)),
                  _resident(wa.shape), _resident(wb.shape), _resident(wo.shape)],
        out_specs=[tile, tile, tile, tile, pl.BlockSpec((ts, GATE_W), lambda i: (i, 0)),
                   pl.BlockSpec((ts, CONV_A), lambda i: (i, 0)), tile, _row(d)],
        out_shape=[jax.ShapeDtypeStruct((s, d), BF16)] * 4
                  + [jax.ShapeDtypeStruct((s, GATE_W), BF16), jax.ShapeDtypeStruct((s, CONV_A), F32),
                     jax.ShapeDtypeStruct((s, d), F32), jax.ShapeDtypeStruct((1, d), F32)],
        compiler_params=_params(("arbitrary",)),
    )(dh, g, mix, pa, pb, proj, wa, wb, wo)


def _final_fwd_bwd(h, nw, target, *, name, ts=512):
    s, d = h.shape

    def body(h_ref, nw_ref, t_ref, loss_ref, dh_ref, dnw_ref):
        x = h_ref[...]
        w = nw_ref[...]
        r = lax.rsqrt(jnp.mean(x * x, axis=-1, keepdims=True) + EPS)
        xhat = x * r
        e = xhat * w - t_ref[...]
        part = 0.5 * jnp.sum(jnp.mean(e * e, axis=-1, keepdims=True))
        dy = e * (1.0 / d)
        dxhat = dy * w
        dh_ref[...] = r * (dxhat - xhat * jnp.mean(dxhat * xhat, axis=-1, keepdims=True))

        @pl.when(pl.program_id(0) == 0)
        def _():
            loss_ref[...] = jnp.zeros_like(loss_ref)
            dnw_ref[...] = jnp.zeros_like(dnw_ref)

        loss_ref[...] += jnp.broadcast_to(part, loss_ref.shape)
        dnw_ref[...] += jnp.sum(dy * xhat, axis=0, keepdims=True)

    tile = pl.BlockSpec((ts, d), lambda i: (i, 0))
    return pl.pallas_call(
        body, name=name, grid=(s // ts,),
        in_specs=[tile, _row(d), tile],
        out_specs=[_row(128), tile, _row(d)],
        out_shape=[jax.ShapeDtypeStruct((1, 128), F32), jax.ShapeDtypeStruct((s, d), F32),
                   jax.ShapeDtypeStruct((1, d), F32)],
        compiler_params=_params(("arbitrary",)),
    )(h, nw, target)


EXT = 8


def _l2norm_heads_bwd(act, dout, scale):
    outs = []
    for hd in range(HEADS):
        cols = slice(hd * HEAD_DIM, (hd + 1) * HEAD_DIM)
        seg = act[:, cols]
        nrm = lax.rsqrt(jnp.sum(seg * seg, axis=-1, keepdims=True) + EPS)
        yhat = seg * nrm
        dsg = dout[:, cols]
        outs.append((scale * nrm) * (dsg - yhat * jnp.sum(yhat * dsg, axis=-1, keepdims=True)))
    return jnp.concatenate(outs, axis=-1)


def _conv_bwd(dq2, dk2, dv2, dya, proj, conv_dn, conv_a, *, name, ts=256, carry=None):
    s = proj.shape[0]
    hd = HEADS * HEAD_DIM
    nt = s // ts
    te = ts + 2 * EXT
    kdn, ka = conv_dn.shape[0], conv_a.shape[0]

    def body(*refs):
        (qp_ref, qc_ref, qn_ref, ap_ref, ac_ref, an_ref) = refs[0:6]
        d3 = refs[6:24]
        (yp_ref, yc_ref, yn_ref, wdn_ref, wa_ref) = refs[24:29]
        (dqkv_ref, da_ref, dwdn_ref, dwa_ref) = refs[29:33]
        xs_ref, dps_ref, xa_ref, dca_ref = refs[33:37]
        i = pl.program_id(0)
        first, last = i == 0, i == nt - 1

        @pl.when(first)
        def _():
            dwdn_ref[...] = jnp.zeros_like(dwdn_ref)
            dwa_ref[...] = jnp.zeros_like(dwa_ref)

        rowe = lax.broadcasted_iota(jnp.int32, (te, 1), 0)
        inside = ~((first & (rowe < EXT)) | (last & (rowe >= EXT + ts)))
        _fill_halo(xs_ref, qp_ref, qc_ref, qn_ref, first, last)
        wdn = wdn_ref[...]
        for part in range(3):
            cols = slice(part * hd, (part + 1) * hd)
            pre = _dwconv_rows(xs_ref, wdn[:, cols], HALO - EXT - 2, te, cols)
            sg = _sigmoid(pre)
            act = pre * sg
            pf, cf, nf, pb, cb, nb = d3[6 * part:6 * part + 6]
            dout = jnp.concatenate([pf[...] + pb[...], cf[...] + cb[...], nf[...] + nb[...]], axis=0)
            if part == 0:
                dact = _l2norm_heads_bwd(act, dout, Q_SCALE)
            elif part == 1:
                dact = _l2norm_heads_bwd(act, dout, 1.0)
            else:
                dact = dout
            dpre = jnp.where(inside, dact * (sg * (1.0 + pre * (1.0 - sg))), 0.0)
            dps_ref[:, cols] = dpre
            acc = wdn[0:1, cols] * dps_ref[EXT + 2:EXT + 2 + ts, cols]
            for tap in range(1, kdn):
                acc = acc + wdn[tap:tap + 1, cols] * dps_ref[EXT + 2 - tap:EXT + 2 - tap + ts, cols]
            dqkv_ref[:, cols] = acc.astype(BF16)
            dcur = dps_ref[EXT:EXT + ts, cols]
            for tap in range(kdn):
                dwdn_ref[tap:tap + 1, cols] += jnp.sum(
                    dcur * xs_ref[HALO - 2 + tap:HALO - 2 + tap + ts, cols], axis=0, keepdims=True)

        cv = lambda r: r[:, CONV_A:2 * CONV_A].astype(F32) * r[:, 2 * CONV_A:].astype(F32)
        _fill_halo(xa_ref, ap_ref, ac_ref, an_ref, first, last, fn=cv)
        wa = wa_ref[...]
        gate_b = jnp.concatenate([ap_ref[HALO - EXT:, 0:CONV_A], ac_ref[:, 0:CONV_A], an_ref[0:EXT, 0:CONV_A]],
                                 axis=0).astype(F32)
        dya_e = jnp.concatenate([yp_ref[...], yc_ref[...], yn_ref[...]], axis=0)
        dca_ref[...] = jnp.where(inside, dya_e * gate_b, 0.0)
        conv = _dwconv_rows(xa_ref, wa, HALO - 1, ts, slice(0, CONV_A))
        acc = wa[0:1, :] * dca_ref[EXT + 1:EXT + 1 + ts, :]
        for tap in range(1, ka):
            acc = acc + wa[tap:tap + 1, :] * dca_ref[EXT + 1 - tap:EXT + 1 - tap + ts, :]
        gc = ac_ref[:, CONV_A:2 * CONV_A].astype(F32)
        val = ac_ref[:, 2 * CONV_A:].astype(F32)
        da_ref[:, 0:CONV_A] = (yc_ref[...] * conv).astype(BF16)
        da_ref[:, CONV_A:2 * CONV_A] = (acc * val).astype(BF16)
        da_ref[:, 2 * CONV_A:] = (acc * gc).astype(BF16)
        dcur = dca_ref[EXT:EXT + ts, :]
        for tap in range(ka):
            dwa_ref[tap:tap + 1, :] += jnp.sum(
                dcur * xa_ref[HALO - 1 + tap:HALO - 1 + tap + ts, :], axis=0, keepdims=True)

    cot = [arr for pair in (dq2, dk2, dv2) for arr in pair for _ in range(3)]
    return _launch(
        body, carry, (proj, proj, proj, proj, proj, proj, *cot, dya, dya, dya, conv_dn, conv_a),
        name=name, grid=(nt,),
        in_specs=(_halo_specs(ts, QKV_W, 0, s) + _halo_specs(ts, A_W, A_OFF // A_W, s)
                  + _halo_specs(ts, hd, 0, s, rows=EXT) * 6 + _halo_specs(ts, CONV_A, 0, s, rows=EXT)
                  + [_resident(conv_dn.shape), _resident(conv_a.shape)]),
        out_specs=[pl.BlockSpec((ts, QKV_W), lambda i: (i, 0)), pl.BlockSpec((ts, A_W), lambda i: (i, 0)),
                   pl.BlockSpec((8, QKV_W), lambda i: (0, 0)), pl.BlockSpec((8, CONV_A), lambda i: (0, 0))],
        out_shape=[jax.ShapeDtypeStruct((s, QKV_W), BF16), jax.ShapeDtypeStruct((s, A_W), BF16),
                   jax.ShapeDtypeStruct((8, QKV_W), F32), jax.ShapeDtypeStruct((8, CONV_A), F32)],
        scratch_shapes=[pltpu.VMEM((ts + 2 * HALO, QKV_W), F32), pltpu.VMEM((te, QKV_W), F32),
                        pltpu.VMEM((ts + 2 * HALO, CONV_A), F32), pltpu.VMEM((te, CONV_A), F32)],
        sem=("arbitrary",))


IN_A = (0, 1536)
IN_QKV = (1536, 4608)
IN_Z = (4608, 5632)
IN_BA = 5632
IN_GATE = (5664, 7712)
IN_COLS = 7712
G_REPL = 4


def _split_w_in(w_in):
    sl = lambda ab: w_in[:, ab[0]:ab[1]]
    w_main = jnp.concatenate([sl(IN_QKV), sl(IN_Z), sl(IN_GATE), sl(IN_A)], axis=1)
    blocks = []
    for d in range(2):
        beta = w_in[:, IN_BA + 8 * d:IN_BA + 8 * d + 8]
        alpha = w_in[:, IN_BA + 16 + 8 * d:IN_BA + 24 + 8 * d]
        pad = jnp.zeros((w_in.shape[0], BA_W - 8 - 8 * G_REPL), w_in.dtype)
        blocks += [beta] + [alpha] * G_REPL + [pad]
    return w_main, jnp.concatenate(blocks, axis=1)


def _merge_dw_in(dw_qkv, dw_z, dw_gate, dw_a, dw_ba):
    ba = [dw_ba[:, 0:8], dw_ba[:, BA_W:BA_W + 8], dw_ba[:, 8:16], dw_ba[:, BA_W + 8:BA_W + 16]]
    return jnp.concatenate([dw_a, dw_qkv, dw_z] + ba + [dw_gate], axis=1)


def _decay_rows(a_log_fwd, dt_bias_fwd, a_log_bwd, dt_bias_bwd):
    def rows(f, b):
        out = []
        for vec in (f, b):
            vec = vec.reshape(HEADS)
            out.append(jnp.concatenate([jnp.zeros((8,), F32)] + [vec] * G_REPL
                                       + [jnp.zeros((BA_W - 8 - 8 * G_REPL,), F32)])[None])
        return jnp.stack(out)
    return rows(a_log_fwd, a_log_bwd), rows(dt_bias_fwd, dt_bias_bwd)


def _local_step(x, target, mod9, wt, comm):
    s, d = x.shape
    n = s // CHUNK
    wt = dict(wt)
    sh1, sc1, g1, sh2, sc2, g2, sh3, sc3, g3 = [mod9[i:i + 1] for i in range(9)]
    alog, dtb = _decay_rows(wt["a_log_fwd"], wt["dt_bias_fwd"], wt["a_log_bwd"], wt["dt_bias_bwd"])

    (u1, a1, b1, f1), got = comm.gather(
        ["w_ffn1_down", "w_in"],
        lambda c: _ffn_up_fwd(x, wt["norm_ffn1"], sc1, sh1, wt["w_ffn1_up"], name="ffn1_up", carry=c))
    wt.update(got)
    w_main, w_ba = _split_w_in(wt["w_in"])
    y1, h1 = _ffn_down_fwd(f1, wt["w_ffn1_down"], x, g1, name="ffn1_down")
    (u2, proj, ba), got = comm.gather(
        ["w_a_out", "w_b_out", "w_out", "w_ffn2_up", "w_ffn2_down"],
        lambda c: _in_proj_fwd(h1, wt["norm_mix"], sc2, sh2, w_main, w_ba, name="in_proj", carry=c))
    wt.update(got)
    scal = _scal_fwd(ba, alog, dtb, name="scal_fwd")
    grow = scal[:, :, L_G:L_G + 8].reshape(2, n, CHUNK, HEADS).transpose(0, 1, 3, 2).reshape(
        2, n, N_GROUPS, GROWS)
    q, k, v, ya = _conv_fwd(proj, wt["conv_dn"], wt["conv_a"], name="conv_fwd")
    o2, states, tinv, vn = _delta_fwd(q, k, v, scal, grow, name="delta_fwd")
    yb = _gate_norm_fwd(o2, proj, wt["dn_norm"], name="gate_norm_fwd")
    pa, pb, mix, h2 = _merge_fwd(ya, yb, proj, wt["w_a_out"], wt["w_b_out"], wt["w_out"], h1, g2,
                                 name="merge_fwd")
    u3, a3, b3, f3 = _ffn_up_fwd(h2, wt["norm_ffn2"], sc3, sh3, wt["w_ffn2_up"], name="ffn2_up")
    y3, h3 = _ffn_down_fwd(f3, wt["w_ffn2_down"], h2, g3, name="ffn2_down")
    loss, dh3, dnorm_final = _final_fwd_bwd(h3, wt["norm_final"], target, name="final")

    dy3, dab3, dg3 = _ffn_bwd_act(dh3, g3, y3, a3, b3, wt["w_ffn2_down"], name="ffn2_bwd_act")
    dh2, dn3, dsc3, dsh3 = _norm_mod_matmul_bwd([(dab3, wt["w_ffn2_up"])], h2, wt["norm_ffn2"], sc3, dh3,
                                                name="ffn2_bwd_up")
    gw = {}
    gw["w_ffn2_up"] = _matmul_tn(u3, dab3, name="dw_ffn2_up", tm=1024, tn=2816)
    gw["w_ffn2_down"] = _matmul_tn(f3, dy3, name="dw_ffn2_down", tm=1408, tn=1024)

    dmix, merged, dpa, dpb, dgates, dya, dyb, dg2 = _merge_bwd(
        dh2, g2, mix, pa, pb, proj, wt["w_a_out"], wt["w_b_out"], wt["w_out"], name="merge_bwd")
    gw["w_out"] = _matmul_tn(merged, dmix, name="dw_out", tm=1024, tn=1024)
    gw["w_a_out"] = _matmul_tn(ya, dpa, name="dw_a_out", tm=512, tn=1024)
    gw["w_b_out"] = _matmul_tn(yb, dpb, name="dw_b_out", tm=1024, tn=1024)
    do, dz, ddn = _gate_norm_bwd(dyb, o2, proj, wt["dn_norm"], name="gate_norm_bwd")
    recv = {}
    (dq2, dk2, dv2, dscal, drow), got = comm.scatter(
        {nm: gw.pop(nm) for nm in ("w_ffn2_up", "w_ffn2_down")},
        lambda c: _delta_bwd(q, k, v, scal, grow, states, tinv, vn, do, name="delta_bwd", carry=c))
    recv.update(got)
    drow_p = jnp.pad(jnp.stack(drow).reshape(2, n, HEADS, CHUNK).transpose(0, 1, 3, 2).reshape(2, s, HEADS),
                     ((0, 0), (0, 0), (L_G, BA_W - L_G - HEADS)))
    dba, dalog, ddtb = _scal_bwd(jnp.stack(dscal), drow_p, ba, alog, dtb, name="scal_bwd")
    (dqkv, dbr_a, dconv_dn, dconv_a), got = comm.scatter(
        {nm: gw.pop(nm) for nm in ("w_out", "w_a_out", "w_b_out")},
        lambda c: _conv_bwd(dq2, dk2, dv2, dya, proj, wt["conv_dn"], wt["conv_a"], name="conv_bwd", carry=c))
    recv.update(got)
    dw_in = _merge_dw_in(
        _matmul_tn(u2, dqkv, name="dw_in_qkv", tm=1024, tn=3072),
        _matmul_tn(u2, dz, name="dw_in_z", tm=1024, tn=1024),
        _matmul_tn(u2, dgates, name="dw_in_gate", tm=1024, tn=2048),
        _matmul_tn(u2, dbr_a, name="dw_in_a", tm=1024, tn=1536),
        _matmul_tn(u2, dba, name="dw_in_ba", tm=1024, tn=2 * BA_W))
    (dh1, dn2, dsc2, dsh2), got = comm.scatter(
        {"w_in": dw_in},
        lambda c: _norm_mod_matmul_bwd(
            [(dqkv, w_main, 0), (dz, w_main, Z_OFF // Z_W), (dgates, w_main, GATE_OFF // GATE_W),
             (dbr_a, w_main, A_OFF // A_W), (dba, w_ba)],
            h1, wt["norm_mix"], sc2, dh2, name="in_proj_bwd", carry=c))
    recv.update(got)

    dy1, dab1, dg1 = _ffn_bwd_act(dh1, g1, y1, a1, b1, wt["w_ffn1_down"], name="ffn1_bwd_act")
    dw_down1 = _matmul_tn(f1, dy1, name="dw_ffn1_down", tm=1408, tn=1024)
    dw_up1, got = comm.scatter(
        {"w_ffn1_down": dw_down1},
        lambda c: _matmul_tn(u1, dab1, name="dw_ffn1_up", tm=1024, tn=2816, carry=c))
    recv.update(got)
    (dx, dn1, dsc1, dsh1), got = comm.scatter(
        {"w_ffn1_up": dw_up1},
        lambda c: _norm_mod_matmul_bwd([(dab1, wt["w_ffn1_up"])], x, wt["norm_ffn1"], sc1, dh1,
                                       name="ffn1_bwd_up", carry=c))
    recv.update(got)

    small = {
        "mod": jnp.concatenate([dsh1, dsc1, dg1, dsh2, dsc2, dg2, dsh3, dsc3, dg3], axis=1),
        "norm_ffn1": dn1, "norm_mix": dn2, "norm_ffn2": dn3, "norm_final": dnorm_final,
        "a_log_fwd": dalog[0, :, L_G:L_G + 8], "dt_bias_fwd": ddtb[0, :, L_G:L_G + 8],
        "a_log_bwd": dalog[1, :, L_G:L_G + 8], "dt_bias_bwd": ddtb[1, :, L_G:L_G + 8],
        "dn_norm": ddn,
        "conv_a": dconv_a[0:3].reshape(1, -1), "conv_dn": dconv_dn[0:5].reshape(1, -1),
    }
    return loss, dx, recv, small


def _full_weight(name, g):
    if name in COL_SHARDED + CONV_SHARDED:
        return g.transpose(1, 0, 2).reshape(g.shape[1], -1)
    return g.reshape(-1, g.shape[-1])


def _grad_pieces(name, g):
    g = g.astype(BF16)
    if name in COL_SHARDED:
        return g.reshape(g.shape[0], N_DEV, -1).transpose(1, 0, 2)
    return g.reshape(N_DEV, -1, g.shape[-1])


class _MeshComm:
    def __init__(self, shards):
        self.shards = shards

    def _run(self, xs, carrier, name, gather):
        if carrier is None:
            return None, _exchange(xs, name=name, gather=gather)
        return carrier((xs, gather))

    def gather(self, names, carrier=None, name=None):
        outs, got = self._run([self.shards[nm] for nm in names], carrier, name, True)
        return outs, {nm: _full_weight(nm, g) for nm, g in zip(names, got)}

    def scatter(self, grads, carrier=None, name=None):
        names = list(grads)
        outs, got = self._run([_grad_pieces(nm, grads[nm]) for nm in names], carrier, name, False)
        return outs, dict(zip(names, got))


def _mod_fwd(c_all, w_ada, *, name):
    def body(c_ref, w_ref, o_ref):
        cv = c_ref[...]
        o_ref[...] = _dot3(cv * _sigmoid(cv), w_ref[...])

    return pl.pallas_call(
        body, name=name, out_shape=jax.ShapeDtypeStruct((c_all.shape[0], w_ada.shape[1]), F32),
        compiler_params=_params(),
    )(c_all, w_ada)


def _adamw_math(w, g, m, v):
    m_new = ADAM_B1 * m + (1.0 - ADAM_B1) * g
    v_new = ADAM_B2 * v + (1.0 - ADAM_B2) * (g * g)
    m_hat = m_new / (1.0 - ADAM_B1 ** ADAM_STEP)
    v_hat = v_new / (1.0 - ADAM_B2 ** ADAM_STEP)
    delta = -ADAM_LR * (m_hat / (jnp.sqrt(v_hat) + ADAM_EPS) + ADAM_WD * w)
    return delta, m_new, v_new


def _reduce_adamw(pieces, w, m, v, *, name, tr):
    r, c = w.shape

    def body(p_ref, w_ref, m_ref, v_ref, g_ref, d_ref, mo_ref, vo_ref):
        g = p_ref[0].astype(F32)
        for src in range(1, N_DEV):
            g = g + p_ref[src].astype(F32)
        g_ref[...] = g
        d_ref[...], mo_ref[...], vo_ref[...] = _adamw_math(w_ref[...], g, m_ref[...], v_ref[...])

    tile = pl.BlockSpec((tr, c), lambda i: (i, 0))
    return pl.pallas_call(
        body, name=name, grid=(r // tr,),
        in_specs=[pl.BlockSpec((N_DEV, tr, c), lambda i: (0, i, 0)), tile, tile, tile],
        out_specs=[tile] * 4, out_shape=[jax.ShapeDtypeStruct((r, c), F32)] * 4,
        compiler_params=_params(("parallel",)),
    )(pieces, w, m, v)


def _ada_grad_adamw(c_all_t, dmod_cols, w, m, v, *, name, tr=256):
    r, c = w.shape

    def body(c_ref, dm_ref, w_ref, m_ref, v_ref, g_ref, d_ref, mo_ref, vo_ref):
        cv = c_ref[...]
        act = cv * _sigmoid(cv)
        dm = dm_ref[...]
        g = act[:, 0:1] * dm[0:1, :]
        for b in range(1, N_DEV):
            g = g + act[:, b:b + 1] * dm[b:b + 1, :]
        g_ref[...] = g
        d_ref[...], mo_ref[...], vo_ref[...] = _adamw_math(w_ref[...], g, m_ref[...], v_ref[...])

    tile = pl.BlockSpec((tr, c), lambda i: (i, 0))
    return pl.pallas_call(
        body, name=name, grid=(r // tr,),
        in_specs=[pl.BlockSpec((tr, N_DEV), lambda i: (i, 0)), pl.BlockSpec((N_DEV, c), lambda i: (0, 0)),
                  tile, tile, tile],
        out_specs=[tile] * 4, out_shape=[jax.ShapeDtypeStruct((r, c), F32)] * 4,
        compiler_params=_params(("parallel",)),
    )(c_all_t, dmod_cols, w, m, v)


def _sum_rows(parts, *, name):
    def body(p_ref, o_ref):
        acc = p_ref[0:1, :]
        for src in range(1, N_DEV):
            acc = acc + p_ref[src:src + 1, :]
        o_ref[...] = acc

    return pl.pallas_call(
        body, name=name, out_shape=jax.ShapeDtypeStruct((1, parts.shape[1]), F32), compiler_params=_params(),
    )(parts)


def _adamw_rows(g, w, m, v, *, name):
    def body(g_ref, w_ref, m_ref, v_ref, d_ref, mo_ref, vo_ref):
        d_ref[...], mo_ref[...], vo_ref[...] = _adamw_math(w_ref[...], g_ref[...], m_ref[...], v_ref[...])

    return pl.pallas_call(
        body, name=name, out_shape=[jax.ShapeDtypeStruct(g.shape, F32)] * 3, compiler_params=_params(),
    )(g, w, m, v)


WEIGHTS = ["w_ada", "b_ada", "norm_ffn1", "w_ffn1_up", "w_ffn1_down", "norm_mix", "w_in", "conv_a", "conv_dn",
           "a_log_fwd", "dt_bias_fwd", "a_log_bwd", "dt_bias_bwd", "dn_norm", "w_a_out", "w_b_out", "w_out",
           "norm_ffn2", "w_ffn2_up", "w_ffn2_down", "norm_final"]
COL_SHARDED = ["w_ffn1_up", "w_in", "w_a_out", "w_ffn2_up"]
ROW_SHARDED = ["w_ffn1_down", "w_b_out", "w_out", "w_ffn2_down"]
CONV_SHARDED = ["conv_a", "conv_dn"]
REPLICATED = ["b_ada", "norm_ffn1", "norm_mix", "a_log_fwd", "dt_bias_fwd", "a_log_bwd", "dt_bias_bwd",
              "dn_norm", "norm_ffn2", "norm_final"]
SMALL_ORDER = ["mod", "norm_ffn1", "norm_mix", "norm_ffn2", "norm_final", "a_log_fwd", "dt_bias_fwd",
               "a_log_bwd", "dt_bias_bwd", "dn_norm", "conv_a", "conv_dn"]
REDUCE_ROWS = {"w_ffn1_up": 256, "w_in": 256, "w_a_out": 256, "w_ffn2_up": 256,
               "w_ffn1_down": 176, "w_b_out": 128, "w_out": 128, "w_ffn2_down": 176}


def _pad_lanes(row):
    pad = (-row.shape[1]) % 128
    return jnp.pad(row, ((0, 0), (0, pad)))


def kernel(x, c, w_ada, b_ada, norm_ffn1, w_ffn1_up, w_ffn1_down, norm_mix, w_in, conv_a, conv_dn, a_log_fwd, dt_bias_fwd, a_log_bwd, dt_bias_bwd, dn_norm, w_a_out, w_b_out, w_out, norm_ffn2, w_ffn2_up, w_ffn2_down, norm_final, loss_target, m_w_ada, m_b_ada, m_norm_ffn1, m_w_ffn1_up, m_w_ffn1_down, m_norm_mix, m_w_in, m_conv_a, m_conv_dn, m_a_log_fwd, m_dt_bias_fwd, m_a_log_bwd, m_dt_bias_bwd, m_dn_norm, m_w_a_out, m_w_b_out, m_w_out, m_norm_ffn2, m_w_ffn2_up, m_w_ffn2_down, m_norm_final, v_w_ada, v_b_ada, v_norm_ffn1, v_w_ffn1_up, v_w_ffn1_down, v_norm_mix, v_w_in, v_conv_a, v_conv_dn, v_a_log_fwd, v_dt_bias_fwd, v_a_log_bwd, v_dt_bias_bwd, v_dn_norm, v_w_a_out, v_w_b_out, v_w_out, v_norm_ffn2, v_w_ffn2_up, v_w_ffn2_down, v_norm_final):
    args = dict(locals())
    w_loc = {n: args[n] for n in WEIGHTS}
    m_loc = {n: args["m_" + n] for n in WEIGHTS}
    v_loc = {n: args["v_" + n] for n in WEIGHTS}
    me = _flat_index(_my_position())
    d_model = x.shape[-1]

    big = COL_SHARDED + ROW_SHARDED
    shards = {n: w_loc[n][0].astype(BF16) for n in big}
    shards.update({n: w_loc[n][0] for n in CONV_SHARDED})
    shards["c"] = c
    comm = _MeshComm(shards)
    wt = comm.gather(["c", "conv_a", "conv_dn", "w_ffn1_up"], name="gather_first")[1]
    c_all = wt.pop("c")
    for n in REPLICATED[1:]:
        wt[n] = w_loc[n].reshape(1, -1)

    mod_cols = _mod_fwd(c_all, w_ada[0], name="mod_fwd")
    mod_all = _exchange([mod_cols], name="gather_mod", gather=True)[0]
    mod_mine = lax.dynamic_index_in_dim(mod_all, me, axis=1, keepdims=False).reshape(1, -1) + b_ada
    mod9 = mod_mine.reshape(9, d_model)

    loss_loc, dx, recv, small = _local_step(x[0], loss_target[0], mod9, wt, comm)
    loss = lax.psum(loss_loc[0, 0], MESH_AXES)

    res = {}
    for n in big:
        res[n] = _reduce_adamw(recv[n], w_loc[n][0], m_loc[n][0], v_loc[n][0], name="adamw_" + n,
                               tr=REDUCE_ROWS[n])

    packed = _pad_lanes(jnp.concatenate([small[n].reshape(1, -1) for n in SMALL_ORDER], axis=1))
    parts = _exchange([packed], name="gather_small", gather=True)[0].reshape(N_DEV, -1)
    total = _sum_rows(parts, name="sum_small")
    off = 0
    gsmall = {}
    for n in SMALL_ORDER:
        size = small[n].size
        gsmall[n] = total[:, off:off + size]
        off += size
    dmod_all = parts[:, 0:9 * d_model]
    ada_cols = w_ada.shape[-1]
    dmod_cols = lax.dynamic_slice_in_dim(dmod_all, me * ada_cols, ada_cols, axis=1)
    res["w_ada"] = _ada_grad_adamw(c_all.T, dmod_cols, w_ada[0], m_w_ada[0], v_w_ada[0], name="adamw_w_ada")
    g_rows = {"b_ada": gsmall["mod"]}
    for n in REPLICATED[1:]:
        g_rows[n] = gsmall[n]
    for n in CONV_SHARDED:
        taps, width = w_loc[n].shape[1], w_loc[n].shape[2]
        full = gsmall[n].reshape(taps, -1)
        g_rows[n] = lax.dynamic_slice_in_dim(full, me * width, width, axis=1).reshape(1, -1)
    row_names = REPLICATED + CONV_SHARDED
    cat = lambda src: _pad_lanes(jnp.concatenate([src[n].reshape(1, -1) for n in row_names], axis=1))
    g_cat = cat(g_rows)
    d_cat, m_cat, v_cat = _adamw_rows(g_cat, cat(w_loc), cat(m_loc), cat(v_loc), name="adamw_small")
    off = 0
    for n in row_names:
        size = w_loc[n].size
        res[n] = tuple(t[:, off:off + size] for t in (g_cat, d_cat, m_cat, v_cat))
        off += size

    outs = [loss, dx[None]]
    for kind in range(4):
        for n in WEIGHTS:
            outs.append(res[n][kind].reshape(w_loc[n].shape))
    return tuple(outs)
```

```python
import functools
import math
import types

import jax
import jax.numpy as jnp
from jax import lax
from jax.experimental import pallas as pl
from jax.experimental.pallas import tpu as pltpu

F32 = jnp.float32
BF16 = jnp.bfloat16
EPS = 1e-6
N_DEV = 8
CHUNK = 64
HEADS = 8
HEAD_DIM = 128
MESH_AXES = ("x", "y", "c")
VMEM_LIMIT_BYTES = 56 * 1024 * 1024

ADAM_LR = 0.001
ADAM_B1 = 0.9
ADAM_B2 = 0.999
ADAM_EPS = 1e-08
ADAM_WD = 0.01
ADAM_STEP = 10


def _params(sem=None):
    return pltpu.CompilerParams(dimension_semantics=sem, vmem_limit_bytes=VMEM_LIMIT_BYTES)


def _row(n):
    return pl.BlockSpec((1, n), lambda *_: (0, 0))


def _resident(shape):
    nd = len(shape)
    return pl.BlockSpec(shape, lambda *_: (0,) * nd, pipeline_mode=pl.Buffered(1))


def _col_chunks(width, chunk=512):
    return [slice(lo, min(lo + chunk, width)) for lo in range(0, width, chunk)]


def _col_window(w, width, col_block):
    return pl.BlockSpec((w.shape[0], width), lambda *_: (0, col_block), pipeline_mode=pl.Buffered(1))


def _sigmoid(x):
    return 1.0 / (1.0 + jnp.exp(-x))


def _dot(a, b):
    return jnp.dot(a, b, preferred_element_type=F32)


def _dot_nt(a, b):
    return lax.dot_general(a, b, (((1,), (1,)), ((), ())), preferred_element_type=F32)


def _dot_tn(a, b):
    return lax.dot_general(a, b, (((0,), (0,)), ((), ())), preferred_element_type=F32)


def _split_bf16(x):
    hi = x.astype(BF16)
    lo = (x - hi.astype(F32)).astype(BF16)
    return hi, lo


def _dot3(a, b, dot=_dot):
    ah, al = a if isinstance(a, tuple) else _split_bf16(a)
    bh, bl = b if isinstance(b, tuple) else _split_bf16(b)
    return dot(ah, bh) + dot(ah, bl) + dot(al, bh)


def _dot_exact(a, b):
    return jnp.dot(a, b, preferred_element_type=F32, precision=lax.Precision.HIGHEST)


def _my_position():
    return tuple(lax.axis_index(a) for a in MESH_AXES)


def _peer(pos, kk):
    return tuple((1 - p) if (kk >> (2 - b)) & 1 else p for b, p in enumerate(pos))


def _flat_index(pos):
    return pos[0] * 4 + pos[1] * 2 + pos[2]


_ANY = pl.BlockSpec(memory_space=pl.ANY)


class _AllToAll:
    def __init__(self, in_refs, out_refs, send_sems, recv_sems, local_sems):
        pos = _my_position()
        me = _flat_index(pos)
        self.copies = []
        for t in range(len(in_refs)):
            self.copies.append(pltpu.make_async_copy(in_refs[t].at[me], out_refs[t].at[me], local_sems.at[t]))
            for kk in range(1, N_DEV):
                peer = _peer(pos, kk)
                self.copies.append(pltpu.make_async_remote_copy(
                    src_ref=in_refs[t].at[_flat_index(peer)], dst_ref=out_refs[t].at[me],
                    send_sem=send_sems.at[t, kk - 1], recv_sem=recv_sems.at[t, kk - 1],
                    device_id=peer, device_id_type=pl.DeviceIdType.MESH))

    def start(self):
        for cp in self.copies:
            cp.start()

    def finish(self):
        for cp in self.copies:
            cp.wait()


class _AllGather:
    def __init__(self, in_refs, out_refs, send_sems, recv_sems, local_sems):
        self.refs = (in_refs, out_refs, send_sems, recv_sems, local_sems)
        x, y, c = _my_position()
        self.me, self.sibling = (x, y, c), (x, y, 1 - c)
        self.chips = [(1 - x, y), (x, 1 - y), (1 - x, 1 - y)]
        self.core = c

    def _copy(self, t, k, block, to, own=False):
        in_refs, out_refs, send_sems, recv_sems, _ = self.refs
        rows = out_refs[t].at[_flat_index(block)]
        return pltpu.make_async_remote_copy(
            src_ref=in_refs[t] if own else rows, dst_ref=rows,
            send_sem=send_sems.at[t, k], recv_sem=recv_sems.at[t, k],
            device_id=to, device_id_type=pl.DeviceIdType.MESH)

    def _local(self, t):
        in_refs, out_refs, _, _, local_sems = self.refs
        return pltpu.make_async_copy(in_refs[t], out_refs[t].at[_flat_index(self.me)], local_sems.at[t])

    def start(self):
        c = self.core
        for t in range(len(self.refs[0])):
            self._local(t).start()
            self._copy(t, 0, self.me, self.sibling, own=True).start()
            for j, chip in enumerate(self.chips):
                self._copy(t, 1 + j, self.me, (*chip, c), own=True).start()

    def finish(self):
        c = self.core
        n_t = len(self.refs[0])
        for t in range(n_t):
            for j, chip in enumerate(self.chips):
                self._copy(t, 1 + j, (*chip, c), self.me).wait_recv()
                self._copy(t, 4 + j, (*chip, c), self.sibling).start()
        for t in range(n_t):
            self._copy(t, 0, self.sibling, self.me).wait_recv()
            for j, chip in enumerate(self.chips):
                self._copy(t, 4 + j, (*chip, 1 - c), self.me).wait_recv()
            self._copy(t, 0, self.me, self.sibling, own=True).wait_send()
            for j, chip in enumerate(self.chips):
                self._copy(t, 1 + j, self.me, (*chip, c), own=True).wait_send()
                self._copy(t, 4 + j, (*chip, c), self.sibling).wait_send()
            self._local(t).wait()


def _exchange_plan(in_refs, out_refs, send_sems, recv_sems, local_sems, gather):
    return (_AllGather if gather else _AllToAll)(in_refs, out_refs, send_sems, recv_sems, local_sems)


def _exchange_shapes(xs, gather):
    out_shape = [jax.ShapeDtypeStruct(((N_DEV,) + x.shape) if gather else x.shape, x.dtype) for x in xs]
    sems = [pltpu.SemaphoreType.DMA((len(xs), N_DEV - 1)), pltpu.SemaphoreType.DMA((len(xs), N_DEV - 1)),
            pltpu.SemaphoreType.DMA((len(xs),))]
    return out_shape, sems


def _exchange(xs, *, name, gather):
    nt = len(xs)

    def body(*refs):
        plan = _exchange_plan(refs[:nt], refs[nt:2 * nt], *refs[2 * nt:], gather)
        plan.start()
        plan.finish()

    out_shape, sems = _exchange_shapes(xs, gather)
    return pl.pallas_call(body, name=name, in_specs=[_ANY] * nt, out_specs=[_ANY] * nt, out_shape=out_shape,
                          scratch_shapes=sems)(*xs)


def _launch(body, carry, args, *, name, grid, in_specs, out_specs, out_shape, scratch_shapes=(), sem):
    single = not isinstance(out_shape, (list, tuple))
    out_specs = [out_specs] if single else list(out_specs)
    out_shape = [out_shape] if single else list(out_shape)
    if carry is None:
        outs = pl.pallas_call(body, name=name, grid=grid, in_specs=list(in_specs), out_specs=out_specs,
                              out_shape=out_shape, scratch_shapes=list(scratch_shapes),
                              compiler_params=_params(sem))(*args)
        return outs[0] if single else outs
    xs, gather = carry
    nt, n_in, n_out, n_scr = len(xs), len(args), len(out_shape), len(scratch_shapes)
    x_shape, sems = _exchange_shapes(xs, gather)

    def wrapped(*refs):
        c_in, x_in = refs[:n_in], refs[n_in:n_in + nt]
        c_out = refs[n_in + nt:n_in + nt + n_out]
        x_out = refs[n_in + nt + n_out:n_in + 2 * nt + n_out]
        scr = refs[n_in + 2 * nt + n_out:]
        ids = [pl.program_id(a) for a in range(len(grid))]
        first = functools.reduce(jnp.logical_and, [i == 0 for i in ids])
        last = functools.reduce(jnp.logical_and, [i == g - 1 for i, g in zip(ids, grid)])
        plan = lambda: _exchange_plan(x_in, x_out, *scr[n_scr:], gather)

        @pl.when(first)
        def _():
            plan().start()

        body(*c_in, *c_out, *scr[:n_scr])

        @pl.when(last)
        def _():
            plan().finish()

    outs = pl.pallas_call(
        wrapped, name=name, grid=grid, in_specs=list(in_specs) + [_ANY] * nt,
        out_specs=out_specs + [_ANY] * nt, out_shape=out_shape + x_shape,
        scratch_shapes=list(scratch_shapes) + sems,
        compiler_params=_params(("arbitrary",) * len(grid)))(*args, *xs)
    compute = outs[:n_out]
    return (compute[0] if single else compute), outs[n_out:]


def _norm_mod(x, nw, sc, sh):
    r = lax.rsqrt(jnp.mean(x * x, axis=-1, keepdims=True) + EPS)
    return (x * r * nw) * (1.0 + sc) + sh


def _norm_mod_bwd(x, nw, sc, du):
    r = lax.rsqrt(jnp.mean(x * x, axis=-1, keepdims=True) + EPS)
    xhat = x * r
    n = xhat * nw
    dsh = jnp.sum(du, axis=0, keepdims=True)
    dsc = jnp.sum(du * n, axis=0, keepdims=True)
    dn = du * (1.0 + sc)
    dnw = jnp.sum(dn * xhat, axis=0, keepdims=True)
    dxhat = dn * nw
    dx = r * (dxhat - xhat * jnp.mean(dxhat * xhat, axis=-1, keepdims=True))
    return dx, dnw, dsc, dsh


def _ffn_up_fwd(h, nw, sc, sh, wup, *, name, ts=1024, tn=1408, carry=None):
    s, d = h.shape
    ts = min(ts, s)
    f_dim = wup.shape[1] // 2
    nj = f_dim // tn

    def body(h_ref, nw_ref, sc_ref, sh_ref, wa_ref, wb_ref, u_ref, a_ref, b_ref, f_ref):
        @pl.when(pl.program_id(1) == 0)
        def _():
            u_ref[...] = _norm_mod(h_ref[...], nw_ref[...], sc_ref[...], sh_ref[...]).astype(BF16)

        u = u_ref[...]

        def epilogue(a, b, cs):
            a_ref[:, cs] = a.astype(BF16)
            b_ref[:, cs] = b.astype(BF16)
            f_ref[:, cs] = (a * _sigmoid(a) * b).astype(BF16)

        pending = None
        for cs in _col_chunks(tn):
            products = (_dot(u, wa_ref[:, cs]), _dot(u, wb_ref[:, cs]), cs)
            if pending is not None:
                epilogue(*pending)
            pending = products
        epilogue(*pending)

    return _launch(
        body, carry, (h, nw, sc, sh, wup, wup), name=name, grid=(s // ts, nj),
        in_specs=[pl.BlockSpec((ts, d), lambda i, j: (i, 0)), _row(d), _row(d), _row(d),
                  pl.BlockSpec((d, tn), lambda i, j: (0, j)),
                  pl.BlockSpec((d, tn), lambda i, j: (0, j + nj))],
        out_specs=[pl.BlockSpec((ts, d), lambda i, j: (i, 0)),
                   pl.BlockSpec((ts, tn), lambda i, j: (i, j)),
                   pl.BlockSpec((ts, tn), lambda i, j: (i, j)),
                   pl.BlockSpec((ts, tn), lambda i, j: (i, j))],
        out_shape=[jax.ShapeDtypeStruct((s, d), BF16)] + [jax.ShapeDtypeStruct((s, f_dim), BF16)] * 3,
        sem=("parallel", "arbitrary"))


def _ffn_down_fwd(f, wd, h, g, *, name, ts=512):
    s, f_dim = f.shape
    d = wd.shape[1]

    def body(f_ref, wd_ref, h_ref, g_ref, y_ref, ho_ref):
        y = _dot(f_ref[...], wd_ref[...])
        y_ref[...] = y.astype(BF16)
        ho_ref[...] = h_ref[...] + (0.5 * g_ref[...]) * y

    return pl.pallas_call(
        body, name=name, grid=(s // ts,),
        in_specs=[pl.BlockSpec((ts, f_dim), lambda i: (i, 0)), _resident((f_dim, d)),
                  pl.BlockSpec((ts, d), lambda i: (i, 0)), _row(d)],
        out_specs=[pl.BlockSpec((ts, d), lambda i: (i, 0)), pl.BlockSpec((ts, d), lambda i: (i, 0))],
        out_shape=[jax.ShapeDtypeStruct((s, d), BF16), jax.ShapeDtypeStruct((s, d), F32)],
        compiler_params=_params(("parallel",)),
    )(f, wd, h, g)


def _ffn_bwd_act(dh, g, y, a, b, wd, *, name, ts=256, carry=None):
    s, d = dh.shape
    f_dim = a.shape[1]

    def body(dh_ref, g_ref, y_ref, a_ref, b_ref, wd_ref, dy_ref, dab_ref, dg_ref):
        dh_v = dh_ref[...]
        dy = ((0.5 * g_ref[...]) * dh_v).astype(BF16)
        dy_ref[...] = dy
        part = jnp.sum(0.5 * dh_v * y_ref[...].astype(F32), axis=0, keepdims=True)

        @pl.when(pl.program_id(0) == 0)
        def _():
            dg_ref[...] = jnp.zeros_like(dg_ref)

        dg_ref[...] += part

        def epilogue(df, cs):
            av = a_ref[:, cs].astype(F32)
            bv = b_ref[:, cs].astype(F32)
            sg = _sigmoid(av)
            dab_ref[:, cs] = (df * bv * (sg * (1.0 + av * (1.0 - sg)))).astype(BF16)
            dab_ref[:, slice(f_dim + cs.start, f_dim + cs.stop)] = (df * (av * sg)).astype(BF16)

        pending = None
        for cs in _col_chunks(f_dim):
            product = (_dot_nt(dy, wd_ref[cs, :]), cs)
            if pending is not None:
                epilogue(*pending)
            pending = product
        epilogue(*pending)

    return _launch(
        body, carry, (dh, g, y, a, b, wd), name=name, grid=(s // ts,),
        in_specs=[pl.BlockSpec((ts, d), lambda i: (i, 0)), _row(d),
                  pl.BlockSpec((ts, d), lambda i: (i, 0)),
                  pl.BlockSpec((ts, f_dim), lambda i: (i, 0)),
                  pl.BlockSpec((ts, f_dim), lambda i: (i, 0)),
                  _resident((f_dim, d))],
        out_specs=[pl.BlockSpec((ts, d), lambda i: (i, 0)),
                   pl.BlockSpec((ts, 2 * f_dim), lambda i: (i, 0)), _row(d)],
        out_shape=[jax.ShapeDtypeStruct((s, d), BF16), jax.ShapeDtypeStruct((s, 2 * f_dim), BF16),
                   jax.ShapeDtypeStruct((1, d), F32)],
        sem=("arbitrary",))


def _norm_mod_matmul_bwd(pairs, h, nw, sc, dh_in, *, name, ts=256, carry=None):
    s, d = h.shape
    n_pairs = len(pairs)

    def body(*refs):
        dx_refs = refs[:n_pairs]
        w_refs = refs[n_pairs:2 * n_pairs]
        h_ref, nw_ref, sc_ref, dhi_ref, dho_ref, dnw_ref, dsc_ref, dsh_ref = refs[2 * n_pairs:]
        du = _dot_nt(dx_refs[0][...], w_refs[0][...])
        for k in range(1, n_pairs):
            du = du + _dot_nt(dx_refs[k][...], w_refs[k][...])
        dx, dnw, dsc, dsh = _norm_mod_bwd(h_ref[...], nw_ref[...], sc_ref[...], du)
        dho_ref[...] = dhi_ref[...] + dx

        @pl.when(pl.program_id(0) == 0)
        def _():
            dnw_ref[...] = jnp.zeros_like(dnw_ref)
            dsc_ref[...] = jnp.zeros_like(dsc_ref)
            dsh_ref[...] = jnp.zeros_like(dsh_ref)

        dnw_ref[...] += dnw
        dsc_ref[...] += dsc
        dsh_ref[...] += dsh

    dxs = [p[0] for p in pairs]
    ws = [p[1] for p in pairs]
    tile = pl.BlockSpec((ts, d), lambda i: (i, 0))
    return _launch(
        body, carry, (*dxs, *ws, h, nw, sc, dh_in), name=name, grid=(s // ts,),
        in_specs=([pl.BlockSpec((ts, x.shape[1]), lambda i: (i, 0)) for x in dxs]
                  + [_col_window(w, x.shape[1], p[2] if len(p) > 2 else 0) for p, x, w in zip(pairs, dxs, ws)]
                  + [tile, _row(d), _row(d), tile]),
        out_specs=[tile, _row(d), _row(d), _row(d)],
        out_shape=[jax.ShapeDtypeStruct((s, d), F32)] + [jax.ShapeDtypeStruct((1, d), F32)] * 3,
        sem=("arbitrary",))


def _matmul_tn(a, b, *, name, tm, tn, tk=1024, carry=None):
    s, m = a.shape
    n = b.shape[1]
    tk = min(tk, s)
    nk = s // tk

    def body(a_ref, b_ref, o_ref, acc_ref):
        k = pl.program_id(2)

        @pl.when(k == 0)
        def _():
            acc_ref[...] = jnp.zeros_like(acc_ref)

        acc_ref[...] += _dot_tn(a_ref[...], b_ref[...])

        @pl.when(k == nk - 1)
        def _():
            o_ref[...] = acc_ref[...].astype(o_ref.dtype)

    return _launch(
        body, carry, (a, b), name=name, grid=(m // tm, n // tn, nk),
        in_specs=[pl.BlockSpec((tk, tm), lambda i, j, k: (k, i)),
                  pl.BlockSpec((tk, tn), lambda i, j, k: (k, j))],
        out_specs=pl.BlockSpec((tm, tn), lambda i, j, k: (i, j)),
        out_shape=jax.ShapeDtypeStruct((m, n), BF16),
        scratch_shapes=[pltpu.VMEM((tm, tn), F32)],
        sem=("parallel", "parallel", "arbitrary"))


def _in_proj_fwd(h, nw, sc, sh, w_main, w_ba, *, name, ts=1024, tn=1536, carry=None):
    s, d = h.shape
    ts = min(ts, s)
    n_main = w_main.shape[1]
    n_ba = w_ba.shape[1]

    def body(h_ref, nw_ref, sc_ref, sh_ref, w_ref, wba_ref, u_ref, p_ref, ba_ref):
        @pl.when(pl.program_id(1) == 0)
        def _():
            u0 = _norm_mod(h_ref[...], nw_ref[...], sc_ref[...], sh_ref[...]).astype(BF16)
            u_ref[...] = u0
            ba_ref[...] = _dot(u0, wba_ref[...])

        p_ref[...] = _dot(u_ref[...], w_ref[...]).astype(BF16)

    return _launch(
        body, carry, (h, nw, sc, sh, w_main, w_ba), name=name, grid=(s // ts, n_main // tn),
        in_specs=[pl.BlockSpec((ts, d), lambda i, j: (i, 0)), _row(d), _row(d), _row(d),
                  pl.BlockSpec((d, tn), lambda i, j: (0, j)), _resident((d, n_ba))],
        out_specs=[pl.BlockSpec((ts, d), lambda i, j: (i, 0)),
                   pl.BlockSpec((ts, tn), lambda i, j: (i, j)),
                   pl.BlockSpec((ts, n_ba), lambda i, j: (i, 0))],
        out_shape=[jax.ShapeDtypeStruct((s, d), BF16), jax.ShapeDtypeStruct((s, n_main), BF16),
                   jax.ShapeDtypeStruct((s, n_ba), F32)],
        sem=("parallel", "arbitrary"))


QKV_W = 3 * HEADS * HEAD_DIM
Z_OFF, Z_W = 3072, 1024
GATE_OFF, GATE_W = 4096, 2048
A_OFF, A_W = 6144, 1536
N_MAIN = 7680
CONV_A = 512
BA_W = 128

L_BETA, L_G, L_EG, L_EKD, L_EGC = 0, 8, 16, 24, 32


def _softplus(z):
    e = jnp.exp(-jnp.abs(z))
    small = e * (1.0 - e * (0.5 - e * (1.0 / 3.0)))
    return jnp.maximum(z, 0.0) + jnp.where(e < 1e-3, small, jnp.log(1.0 + e))


def _tri(n, sgn, strict=False):
    i = lax.broadcasted_iota(jnp.int32, (n, n), 0)
    j = lax.broadcasted_iota(jnp.int32, (n, n), 1)
    dlt = (i - j) * sgn
    return (dlt > 0) if strict else (dlt >= 0)


def _scal_fwd(ba, alog, dtb, *, name, ts=512):
    s = ba.shape[0]

    def body(ba_ref, al_ref, dt_ref, o_ref):
        d = pl.program_id(0)
        sgn = 1 - 2 * d
        x = ba_ref[...]
        lane = lax.broadcasted_iota(jnp.int32, x.shape, 1)
        beta = _sigmoid(x)
        g = -jnp.exp(al_ref[0]) * _softplus(x + dt_ref[0])
        g = jnp.where((lane >= L_G) & (lane < L_EGC + 8), g, 0.0)
        ltri = jnp.where(_tri(CHUNK, sgn), 1.0, 0.0).astype(F32)
        for c in range(ts // CHUNK):
            rows = slice(c * CHUNK, (c + 1) * CHUNK)
            gc = _dot_exact(ltri, g[rows])
            g_end = jnp.where(d == 0, gc[CHUNK - 1:CHUNK], gc[0:1])
            ln = lane[rows]
            out = jnp.where(ln < L_G, beta[rows],
                  jnp.where(ln < L_EG, gc,
                  jnp.where(ln < L_EKD, jnp.exp(gc),
                  jnp.where(ln < L_EGC, jnp.exp(g_end - gc),
                  jnp.where(ln < L_EGC + 8, jnp.broadcast_to(jnp.exp(g_end), gc.shape), 0.0)))))
            o_ref[0, rows, :] = out

    return pl.pallas_call(
        body, name=name, grid=(2, s // ts),
        in_specs=[pl.BlockSpec((ts, BA_W), lambda d, i: (i, d)),
                  pl.BlockSpec((1, 1, BA_W), lambda d, i: (d, 0, 0)),
                  pl.BlockSpec((1, 1, BA_W), lambda d, i: (d, 0, 0))],
        out_specs=pl.BlockSpec((1, ts, BA_W), lambda d, i: (d, i, 0)),
        out_shape=jax.ShapeDtypeStruct((2, s, BA_W), F32),
        compiler_params=_params(("parallel", "parallel")),
    )(ba, alog, dtb)


def _scal_bwd(dscal, drow, ba, alog, dtb, *, name, ts=512):
    s = ba.shape[0]

    def body(ds_ref, dr_ref, ba_ref, al_ref, dt_ref, dba_ref, dal_ref, ddt_ref):
        d = pl.program_id(0)
        sgn = 1 - 2 * d
        x = ba_ref[...]
        lane = lax.broadcasted_iota(jnp.int32, x.shape, 1)
        in_g = (lane >= L_G) & (lane < L_G + 8)
        beta = _sigmoid(x)
        z = x + dt_ref[0]
        neg_a = -jnp.exp(al_ref[0])
        g = neg_a * _softplus(z)
        dsv = ds_ref[0]
        dgc = jnp.where(in_g, dsv + dr_ref[0], 0.0)
        utri = jnp.where(_tri(CHUNK, -sgn), 1.0, 0.0).astype(F32)
        dal = jnp.zeros((1, BA_W), F32)
        ddt = jnp.zeros((1, BA_W), F32)
        for c in range(ts // CHUNK):
            rows = slice(c * CHUNK, (c + 1) * CHUNK)
            dg = _dot_exact(utri, dgc[rows])
            dz = dg * neg_a * _sigmoid(z[rows])
            dal = dal + jnp.sum(dg * g[rows], axis=0, keepdims=True)
            ddt = ddt + jnp.sum(dz, axis=0, keepdims=True)
            b = beta[rows]
            out = jnp.where(lane[rows] < L_G, dsv[rows] * b * (1.0 - b), jnp.where(in_g[rows], dz, 0.0))
            dba_ref[rows, :] = out.astype(BF16)

        @pl.when(pl.program_id(1) == 0)
        def _():
            dal_ref[...] = jnp.zeros_like(dal_ref)
            ddt_ref[...] = jnp.zeros_like(ddt_ref)

        dal_ref[0] += dal
        ddt_ref[0] += ddt

    row3 = pl.BlockSpec((1, 1, BA_W), lambda d, i: (d, 0, 0))
    tok3 = pl.BlockSpec((1, ts, BA_W), lambda d, i: (d, i, 0))
    return pl.pallas_call(
        body, name=name, grid=(2, s // ts),
        in_specs=[tok3, tok3, pl.BlockSpec((ts, BA_W), lambda d, i: (i, d)), row3, row3],
        out_specs=[pl.BlockSpec((ts, BA_W), lambda d, i: (i, d)), row3, row3],
        out_shape=[jax.ShapeDtypeStruct((s, 2 * BA_W), BF16), jax.ShapeDtypeStruct((2, 1, BA_W), F32),
                   jax.ShapeDtypeStruct((2, 1, BA_W), F32)],
        compiler_params=_params(("arbitrary", "arbitrary")),
    )(dscal, drow, ba, alog, dtb)


HALO = 16


def _halo_specs(ts, width, col_block, n_rows, rows=HALO):
    r = ts // rows
    last = n_rows // rows - 1
    return [pl.BlockSpec((rows, width), lambda i: (jnp.maximum(i * r - 1, 0), col_block)),
            pl.BlockSpec((ts, width), lambda i: (i, col_block)),
            pl.BlockSpec((rows, width), lambda i: (jnp.minimum((i + 1) * r, last), col_block))]


def _fill_halo(dst_ref, prev_ref, cur_ref, next_ref, first, last, fn=lambda r: r[...].astype(F32)):
    h = prev_ref.shape[0]
    ts = cur_ref.shape[0]
    p = fn(prev_ref)
    n = fn(next_ref)
    dst_ref[0:h, :] = jnp.where(first, 0.0, p)
    dst_ref[h:h + ts, :] = fn(cur_ref)
    dst_ref[h + ts:h + ts + h, :] = jnp.where(last, 0.0, n)


def _dwconv_rows(src_ref, w, start, n_rows, cols):
    acc = w[0:1, :] * src_ref[start:start + n_rows, cols]
    for i in range(1, w.shape[0]):
        acc = acc + w[i:i + 1, :] * src_ref[start + i:start + i + n_rows, cols]
    return acc


def _l2norm_heads(act, scale):
    outs = []
    for hd in range(HEADS):
        seg = act[:, hd * HEAD_DIM:(hd + 1) * HEAD_DIM]
        outs.append(seg * (lax.rsqrt(jnp.sum(seg * seg, axis=-1, keepdims=True) + EPS) * scale))
    return jnp.concatenate(outs, axis=-1)


Q_SCALE = HEAD_DIM ** -0.5


def _conv_fwd(proj, conv_dn, conv_a, *, name, ts=512):
    s = proj.shape[0]
    hd = HEADS * HEAD_DIM
    nt = s // ts

    def body(qp_ref, qc_ref, qn_ref, ap_ref, ac_ref, an_ref, wdn_ref, wa_ref,
             q_ref, k_ref, v_ref, ya_ref, xs_ref, xa_ref):
        i = pl.program_id(0)
        first, last = i == 0, i == nt - 1
        _fill_halo(xs_ref, qp_ref, qc_ref, qn_ref, first, last)
        wdn = wdn_ref[...]
        for part, o_ref in enumerate((q_ref, k_ref, v_ref)):
            cols = slice(part * hd, (part + 1) * hd)
            pre = _dwconv_rows(xs_ref, wdn[:, cols], HALO - 2, ts, cols)
            act = pre * _sigmoid(pre)
            if part == 0:
                act = _l2norm_heads(act, Q_SCALE)
            elif part == 1:
                act = _l2norm_heads(act, 1.0)
            o_ref[...] = act
        cv = lambda r: r[:, CONV_A:2 * CONV_A].astype(F32) * r[:, 2 * CONV_A:].astype(F32)
        _fill_halo(xa_ref, ap_ref, ac_ref, an_ref, first, last, fn=cv)
        conv = _dwconv_rows(xa_ref, wa_ref[...], HALO - 1, ts, slice(0, CONV_A))
        ya_ref[...] = (ac_ref[:, 0:CONV_A].astype(F32) * conv).astype(BF16)

    tile = lambda w: pl.BlockSpec((ts, w), lambda i: (i, 0))
    return pl.pallas_call(
        body, name=name, grid=(nt,),
        in_specs=(_halo_specs(ts, QKV_W, 0, s) + _halo_specs(ts, A_W, A_OFF // A_W, s)
                  + [_resident(conv_dn.shape), _resident(conv_a.shape)]),
        out_specs=[tile(hd), tile(hd), tile(hd), tile(CONV_A)],
        out_shape=[jax.ShapeDtypeStruct((s, hd), F32)] * 3 + [jax.ShapeDtypeStruct((s, CONV_A), BF16)],
        scratch_shapes=[pltpu.VMEM((ts + 2 * HALO, QKV_W), F32), pltpu.VMEM((ts + 2 * HALO, CONV_A), F32)],
        compiler_params=_params(("parallel",)),
    )(proj, proj, proj, proj, proj, proj, conv_dn, conv_a)


INV_BASE = 8


GROUP = 4
GROWS = GROUP * CHUNK
N_GROUPS = HEADS // GROUP


def _stack(parts):
    return jnp.concatenate(parts, axis=0)


M_INCL, M_STRICT, M_EYE, M_BASE, M_JOIN = 0, 1, 2, 3, 4
JOIN_SIZES = (16, 32, 64)
N_MASKS = M_JOIN + len(JOIN_SIZES)


def _write_group_masks(mask_ref, sgn, n_masks):
    i = lax.broadcasted_iota(jnp.int32, (GROWS, GROWS), 0)
    j = lax.broadcasted_iota(jnp.int32, (GROWS, GROWS), 1)
    same = lambda m: jnp.right_shift(i, int(math.log2(m))) == jnp.right_shift(j, int(math.log2(m)))
    dlt = (i - j) * sgn
    one = lambda cond: jnp.where(cond, 1.0, 0.0).astype(F32)
    mask_ref[M_INCL] = one(same(CHUNK) & (dlt >= 0))
    mask_ref[M_STRICT] = one(same(CHUNK) & (dlt > 0))
    if n_masks > M_EYE:
        mask_ref[M_EYE] = one(i == j)
        mask_ref[M_BASE] = one(same(INV_BASE))
        for lvl, m in enumerate(JOIN_SIZES):
            mask_ref[M_JOIN + lvl] = one(same(m) & jnp.logical_not(same(m // 2)))


def _group_decay(gcol, grow, mask_ref):
    return jnp.exp(jnp.minimum(gcol - grow, 0.0)) * mask_ref[M_INCL]


def _block_inverse_many(a_ms, mask_refs):
    xs = [-(a * m[M_BASE]) for a, m in zip(a_ms, mask_refs)]
    ts = [m[M_EYE] + x for x, m in zip(xs, mask_refs)]
    ps = xs
    for _ in range(int(math.log2(INV_BASE)) - 1):
        p_bs = [p.astype(BF16) for p in ps]
        ps = [_dot(p_b, p_b) for p_b in p_bs]
        ts = [t + _dot(t.astype(BF16), p.astype(BF16)) for t, p in zip(ts, ps)]
    for lvl in range(len(JOIN_SIZES)):
        t_bs = [t.astype(BF16) for t in ts]
        joins = [(a * m[M_JOIN + lvl]).astype(BF16) for a, m in zip(a_ms, mask_refs)]
        mids = [_dot(t_b, j).astype(BF16) for t_b, j in zip(t_bs, joins)]
        ts = [t - _dot(mid, t_b) for t, mid, t_b in zip(ts, mids, t_bs)]
    return ts


def _group_operands(q_ref, k_ref, v_ref, scv, grp):
    heads = [GROUP * grp + t for t in range(GROUP)]
    tiles = lambda ref: [ref[:, h * HEAD_DIM:(h + 1) * HEAD_DIM] for h in heads]
    col = lambda base: [scv[:, base + h:base + h + 1] for h in heads]
    egc = [scv[0:1, L_EGC + h:L_EGC + h + 1] for h in heads]
    return heads, tiles(q_ref), tiles(k_ref), tiles(v_ref), col(L_BETA), col(L_G), col(L_EG), col(L_EKD), egc


def _delta_fwd(q, k, v, scal, grow, *, name):
    s = q.shape[0]
    n = s // CHUNK
    hd_all = HEADS * HEAD_DIM

    def body(*refs):
        ins, outs, (state, mask_ref) = refs[:10], refs[10:18], refs[18:]

        @pl.when(pl.program_id(0) == 0)
        def _():
            state[...] = jnp.zeros_like(state)
            for d in range(2):
                _write_group_masks(mask_ref.at[d], 1 - 2 * d, N_MASKS)

        chains = []
        for d in range(2):
            q_ref, k_ref, v_ref, sc_ref, gr_ref = ins[5 * d:5 * d + 5]
            scv = sc_ref[0]
            for grp in range(N_GROUPS):
                chains.append(dict(
                    d=d, grp=grp, gr_ref=gr_ref, out=outs[4 * d:4 * d + 4], state=state.at[d], masks=mask_ref.at[d],
                    ops=_group_operands(q_ref, k_ref, v_ref, scv, grp)))
        for ch in chains:
            heads, qs, ks, vs, beta, gcol, eg, ekd, egc = ch["ops"]
            ch["dm"] = _group_decay(_stack(gcol), ch["gr_ref"][0, 0, ch["grp"]:ch["grp"] + 1, :], ch["masks"])
            ch["k_b"] = _stack(ks).astype(BF16)
            ch["kb_b"] = _stack([ks[t] * beta[t] for t in range(GROUP)]).astype(BF16)
        for ch in chains:
            ch["a_m"] = _dot_nt(ch["kb_b"], ch["k_b"]) * ch["dm"] * ch["masks"][M_STRICT]
        tinvs = _block_inverse_many([ch["a_m"] for ch in chains], [ch["masks"] for ch in chains])
        for ch, tinv in zip(chains, tinvs):
            heads, qs, ks, vs, beta, gcol, eg, ekd, egc = ch["ops"]
            o_ref, st_ref, t_ref, vn_ref = ch["out"]
            ch["tinv"] = tinv.astype(BF16)
            t_ref[0, ch["grp"]] = ch["tinv"]
            ch["p_b"] = (_dot_nt(_stack(qs).astype(BF16), ch["k_b"]) * ch["dm"]).astype(BF16)
            ch["sh"] = [ch["state"][h] for h in heads]
            ch["sh_b"] = [x.astype(BF16) for x in ch["sh"]]
            for t, h in enumerate(heads):
                st_ref[0, h] = ch["sh_b"][t]
        for ch in chains:
            heads, qs, ks, vs, beta, gcol, eg, ekd, egc = ch["ops"]
            ch["br"] = _stack([beta[t] * (vs[t] - _dot((ks[t] * eg[t]).astype(BF16), ch["sh_b"][t]))
                               for t in range(GROUP)]).astype(BF16)
        for ch in chains:
            ch["vn_b"] = _dot(ch["tinv"], ch["br"]).astype(BF16)
        for ch in chains:
            ch["o_intra"] = _dot(ch["p_b"], ch["vn_b"])
        for ch in chains:
            heads, qs, ks, vs, beta, gcol, eg, ekd, egc = ch["ops"]
            o_ref, st_ref, t_ref, vn_ref = ch["out"]
            for t, h in enumerate(heads):
                rows = slice(t * CHUNK, (t + 1) * CHUNK)
                cols = slice(h * HEAD_DIM, (h + 1) * HEAD_DIM)
                o_ref[:, cols] = _dot((qs[t] * eg[t]).astype(BF16), ch["sh_b"][t]) + ch["o_intra"][rows]
                ch["state"][h] = egc[t] * ch["sh"][t] + _dot_tn((ks[t] * ekd[t]).astype(BF16), ch["vn_b"][rows])
                vn_ref[:, cols] = ch["vn_b"][rows]

    at = [lambda c: c, lambda c: n - 1 - c]
    in_specs, out_specs = [], []
    for d in range(2):
        tok = pl.BlockSpec((CHUNK, hd_all), lambda c, d=d: (at[d](c), 0))
        in_specs += [tok] * 3 + [pl.BlockSpec((1, CHUNK, BA_W), lambda c, d=d: (d, at[d](c), 0)),
                                 pl.BlockSpec((1, 1, N_GROUPS, GROWS), lambda c, d=d: (d, at[d](c), 0, 0))]
        out_specs += [tok, pl.BlockSpec((1, HEADS, HEAD_DIM, HEAD_DIM), lambda c, d=d: (at[d](c), 0, 0, 0)),
                      pl.BlockSpec((1, N_GROUPS, GROWS, GROWS), lambda c, d=d: (at[d](c), 0, 0, 0)), tok]
    per_dir_shape = [jax.ShapeDtypeStruct((s, hd_all), F32),
                     jax.ShapeDtypeStruct((n, HEADS, HEAD_DIM, HEAD_DIM), BF16),
                     jax.ShapeDtypeStruct((n, N_GROUPS, GROWS, GROWS), BF16),
                     jax.ShapeDtypeStruct((s, hd_all), BF16)]
    outs = pl.pallas_call(
        body, name=name, grid=(n,), in_specs=in_specs, out_specs=out_specs, out_shape=per_dir_shape * 2,
        scratch_shapes=[pltpu.VMEM((2, HEADS, HEAD_DIM, HEAD_DIM), F32),
                        pltpu.VMEM((2, N_MASKS, GROWS, GROWS), F32)],
        compiler_params=_params(("arbitrary",)),
    )(*([q, k, v, scal, grow] * 2))
    return tuple((outs[i], outs[4 + i]) for i in range(4))


def _delta_bwd(q, k, v, scal, grow, states, tinv, vn, do, *, name, carry=None):
    s = q.shape[0]
    n = s // CHUNK
    hd_all = HEADS * HEAD_DIM

    grp_rows = [slice(t * CHUNK, (t + 1) * CHUNK) for t in range(GROUP)]
    per_head = lambda fn: _stack([fn(t) for t in range(GROUP)])

    def body(*refs):
        ins, outs, (dstate, mask_ref) = refs[:18], refs[18:28], refs[28:]

        @pl.when(pl.program_id(0) == 0)
        def _():
            dstate[...] = jnp.zeros_like(dstate)
            for d in range(2):
                _write_group_masks(mask_ref.at[d], 1 - 2 * d, M_EYE)

        chains = []
        for d in range(2):
            q_ref, k_ref, v_ref, sc_ref, gr_ref, st_ref, t_ref, vn_ref, do_ref = ins[9 * d:9 * d + 9]
            scv = sc_ref[0]
            for grp in range(N_GROUPS):
                c = types.SimpleNamespace(d=d, grp=grp, out=outs[5 * d:5 * d + 5], dstate=dstate.at[d],
                                          masks=mask_ref.at[d])
                (c.heads, qs, ks, c.vs, beta, gcol, eg, ekd, c.egc) = _group_operands(q_ref, k_ref, v_ref, scv, grp)
                c.cols = [slice(h * HEAD_DIM, (h + 1) * HEAD_DIM) for h in c.heads]
                c.dm = _group_decay(_stack(gcol), gr_ref[0, 0, grp:grp + 1, :], c.masks)
                c.dm_strict = c.dm * c.masks[M_STRICT]
                other = mask_ref.at[1 - d]
                c.dm_t = jnp.exp(jnp.minimum(gr_ref[0, 0, grp:grp + 1, :] - _stack(gcol), 0.0)) * other[M_INCL]
                c.dm_t_strict = c.dm_t * other[M_STRICT]
                c.beta, c.eg, c.ekd = _stack(beta), _stack(eg), _stack(ekd)
                c.q, c.k = _stack(qs), _stack(ks)
                c.q_b, c.k_b = c.q.astype(BF16), c.k.astype(BF16)
                c.kb_b = (c.k * c.beta).astype(BF16)
                c.kg, c.qg, c.kd = c.k * c.eg, c.q * c.eg, c.k * c.ekd
                c.kg_b, c.qg_b, c.kd_b = c.kg.astype(BF16), c.qg.astype(BF16), c.kd.astype(BF16)
                c.vn_b = _stack([vn_ref[:, cc] for cc in c.cols])
                c.do_b = _stack([do_ref[:, cc] for cc in c.cols]).astype(BF16)
                c.sh_b = [st_ref[0, h] for h in c.heads]
                c.dsp = [c.dstate[h] for h in c.heads]
                c.dsp_b = [x.astype(BF16) for x in c.dsp]
                c.t_b = t_ref[0, grp]
                chains.append(c)
        for c in chains:
            c.kk = _dot_nt(c.kb_b, c.k_b)
            c.qk = _dot_nt(c.q_b, c.k_b)
            c.pt_b = (_dot_nt(c.k_b, c.q_b) * c.dm_t).astype(BF16)
        for c in chains:
            c.r = per_head(lambda t: c.vs[t] - _dot(c.kg_b[grp_rows[t]], c.sh_b[t]))
            c.kd_ds = per_head(lambda t: _dot(c.kd_b[grp_rows[t]], c.dsp_b[t]))
        for c in chains:
            c.dvn_b = (_dot(c.pt_b, c.do_b) + c.kd_ds).astype(BF16)
        for c in chains:
            c.db = _dot_tn(c.t_b, c.dvn_b)
        for c in chains:
            c.dr = c.db * c.beta
            c.dbeta = jnp.sum(c.db * c.r, axis=-1, keepdims=True)
            c.dr_b, c.db_b = c.dr.astype(BF16), c.db.astype(BF16)
        for c in chains:
            c.dkg = -per_head(lambda t: _dot_nt(c.dr_b[grp_rows[t]], c.sh_b[t]))
            c.dqg = per_head(lambda t: _dot_nt(c.do_b[grp_rows[t]], c.sh_b[t]))
            c.dkd = per_head(lambda t: _dot_nt(c.vn_b[grp_rows[t]], c.dsp_b[t]))
        for c in chains:
            c.dpm = _dot_nt(c.do_b, c.vn_b) * c.dm
            c.dam = -_dot_nt(c.db_b, c.vn_b) * c.dm_strict
            c.dpm_b, c.dam_b = c.dpm.astype(BF16), c.dam.astype(BF16)
            c.dpm_t_b = (_dot_nt(c.vn_b, c.do_b) * c.dm_t).astype(BF16)
            c.dam_t_b = (-_dot_nt(c.vn_b, c.db_b) * c.dm_t_strict).astype(BF16)
        for c in chains:
            c.dkb = _dot(c.dam_b, c.k_b)
            c.dq = c.dqg * c.eg + _dot(c.dpm_b, c.k_b)
        for c in chains:
            c.dk = (c.dkg * c.eg + c.dkd * c.ekd + _dot(c.dpm_t_b, c.q_b) + _dot(c.dam_t_b, c.kb_b)
                    + c.dkb * c.beta)
        lane = lax.broadcasted_iota(jnp.int32, (CHUNK, BA_W), 1)
        row = lax.broadcasted_iota(jnp.int32, (CHUNK, 1), 0)
        dsc_acc = [jnp.zeros((CHUNK, BA_W), F32) for _ in range(2)]
        for c in chains:
            dq_ref, dk_ref, dv_ref, dsc_ref, dgr_ref = c.out
            end_row = CHUNK - 1 if c.d == 0 else 0
            dbeta = c.dbeta + jnp.sum(c.dkb * c.k, axis=-1, keepdims=True)
            m = c.dpm * c.qk + c.dam * c.kk
            kd_term = jnp.sum(c.dkd * c.kd, axis=-1, keepdims=True)
            dgcol = (jnp.sum(c.dqg * c.qg, axis=-1, keepdims=True) + jnp.sum(c.dkg * c.kg, axis=-1, keepdims=True)
                     - kd_term + jnp.sum(m, axis=-1, keepdims=True))
            dgr_ref[0, c.grp:c.grp + 1, :] = -jnp.sum(m, axis=0, keepdims=True)
            for t, h in enumerate(c.heads):
                rows, cols = grp_rows[t], c.cols[t]
                dq_ref[:, cols] = c.dq[rows]
                dk_ref[:, cols] = c.dk[rows]
                dv_ref[:, cols] = c.dr[rows]
                dg_end = jnp.sum(kd_term[rows]) + c.egc[t] * jnp.sum(c.dsp[t] * c.sh_b[t].astype(F32))
                dgcol_h = dgcol[rows] + jnp.where(row == end_row, dg_end, 0.0)
                dsc_acc[c.d] = jnp.where(lane == L_BETA + h, dbeta[rows], dsc_acc[c.d])
                dsc_acc[c.d] = jnp.where(lane == L_G + h, dgcol_h, dsc_acc[c.d])
                c.dstate[h] = (_dot_tn(c.qg_b[rows], c.do_b[rows]) + c.egc[t] * c.dsp[t]
                               - _dot_tn(c.kg_b[rows], c.dr_b[rows]))
        for d in range(2):
            outs[5 * d + 3][...] = dsc_acc[d]

    at = [lambda c: n - 1 - c, lambda c: c]
    in_specs, out_specs, args = [], [], []
    for d in range(2):
        tok = pl.BlockSpec((CHUNK, hd_all), lambda c, d=d: (at[d](c), 0))
        in_specs += [tok] * 3 + [pl.BlockSpec((1, CHUNK, BA_W), lambda c, d=d: (d, at[d](c), 0)),
                                 pl.BlockSpec((1, 1, N_GROUPS, GROWS), lambda c, d=d: (d, at[d](c), 0, 0)),
                                 pl.BlockSpec((1, HEADS, HEAD_DIM, HEAD_DIM), lambda c, d=d: (at[d](c), 0, 0, 0)),
                                 pl.BlockSpec((1, N_GROUPS, GROWS, GROWS), lambda c, d=d: (at[d](c), 0, 0, 0)),
                                 tok, tok]
        args += [q, k, v, scal, grow, states[d], tinv[d], vn[d], do]
        out_specs += [tok] * 3 + [pl.BlockSpec((CHUNK, BA_W), lambda c, d=d: (at[d](c), 0)),
                                  pl.BlockSpec((1, N_GROUPS, GROWS), lambda c, d=d: (at[d](c), 0, 0))]
    per_dir_shape = ([jax.ShapeDtypeStruct((s, hd_all), F32)] * 3
                     + [jax.ShapeDtypeStruct((s, BA_W), F32), jax.ShapeDtypeStruct((n, N_GROUPS, GROWS), F32)])
    res = _launch(
        body, carry, tuple(args), name=name, grid=(n,), in_specs=in_specs, out_specs=out_specs,
        out_shape=per_dir_shape * 2,
        scratch_shapes=[pltpu.VMEM((2, HEADS, HEAD_DIM, HEAD_DIM), F32), pltpu.VMEM((2, M_EYE, GROWS, GROWS), F32)],
        sem=("arbitrary",))
    outs, got = res if carry is not None else (res, None)
    paired = tuple((outs[i], outs[5 + i]) for i in range(5))
    return paired if carry is None else (paired, got)


def _gate_norm_fwd(o2, proj, dnw, *, name, ts=512):
    s = o2[0].shape[0]
    hd_all = HEADS * HEAD_DIM

    def body(of_ref, ob_ref, z_ref, w_ref, y_ref):
        w = w_ref[...]
        for hd in range(HEADS):
            cols = slice(hd * HEAD_DIM, (hd + 1) * HEAD_DIM)
            seg = of_ref[:, cols] + ob_ref[:, cols]
            r = lax.rsqrt(jnp.mean(seg * seg, axis=-1, keepdims=True) + EPS)
            z = z_ref[:, cols].astype(F32)
            y_ref[:, cols] = ((seg * r * w) * (z * _sigmoid(z))).astype(BF16)

    tile = pl.BlockSpec((ts, hd_all), lambda i: (i, 0))
    return pl.pallas_call(
        body, name=name, grid=(s // ts,),
        in_specs=[tile, tile, pl.BlockSpec((ts, Z_W), lambda i: (i, Z_OFF // Z_W)), _row(HEAD_DIM)],
        out_specs=tile,
        out_shape=jax.ShapeDtypeStruct((s, hd_all), BF16),
        compiler_params=_params(("parallel",)),
    )(o2[0], o2[1], proj, dnw)


def _gate_norm_bwd(dyb, o2, proj, dnw, *, name, ts=512):
    s = o2[0].shape[0]
    hd_all = HEADS * HEAD_DIM

    def body(dy_ref, of_ref, ob_ref, z_ref, w_ref, do_ref, dz_ref, dw_ref):
        w = w_ref[...]
        dw = jnp.zeros((1, HEAD_DIM), F32)
        for hd in range(HEADS):
            cols = slice(hd * HEAD_DIM, (hd + 1) * HEAD_DIM)
            seg = of_ref[:, cols] + ob_ref[:, cols]
            r = lax.rsqrt(jnp.mean(seg * seg, axis=-1, keepdims=True) + EPS)
            xhat = seg * r
            z = z_ref[:, cols].astype(F32)
            sg = _sigmoid(z)
            dy = dy_ref[:, cols]
            dnrm = dy * (z * sg)
            dz_ref[:, cols] = (dy * (xhat * w) * (sg * (1.0 + z * (1.0 - sg)))).astype(BF16)
            dw = dw + jnp.sum(dnrm * xhat, axis=0, keepdims=True)
            dxhat = dnrm * w
            do_ref[:, cols] = r * (dxhat - xhat * jnp.mean(dxhat * xhat, axis=-1, keepdims=True))

        @pl.when(pl.program_id(0) == 0)
        def _():
            dw_ref[...] = jnp.zeros_like(dw_ref)

        dw_ref[...] += dw

    tile = pl.BlockSpec((ts, hd_all), lambda i: (i, 0))
    return pl.pallas_call(
        body, name=name, grid=(s // ts,),
        in_specs=[tile, tile, tile, pl.BlockSpec((ts, Z_W), lambda i: (i, Z_OFF // Z_W)), _row(HEAD_DIM)],
        out_specs=[tile, tile, _row(HEAD_DIM)],
        out_shape=[jax.ShapeDtypeStruct((s, hd_all), F32), jax.ShapeDtypeStruct((s, hd_all), BF16),
                   jax.ShapeDtypeStruct((1, HEAD_DIM), F32)],
        compiler_params=_params(("arbitrary",)),
    )(dyb, o2[0], o2[1], proj, dnw)


def _merge_fwd(ya, yb, proj, wa, wb, wo, h, g, *, name, ts=512):
    s, d = h.shape

    def body(ya_ref, yb_ref, gt_ref, wa_ref, wb_ref, wo_ref, h_ref, g_ref, pa_ref, pb_ref, mix_ref, ho_ref):
        pa = _dot(ya_ref[...], wa_ref[...])
        pb = _dot(yb_ref[...], wb_ref[...])
        pa_ref[...] = pa.astype(BF16)
        pb_ref[...] = pb.astype(BF16)
        merged = (_sigmoid(gt_ref[:, :d].astype(F32)) * pa + _sigmoid(gt_ref[:, d:].astype(F32)) * pb)
        mix = _dot(merged.astype(BF16), wo_ref[...])
        mix_ref[...] = mix.astype(BF16)
        ho_ref[...] = h_ref[...] + g_ref[...] * mix

    tile = pl.BlockSpec((ts, d), lambda i: (i, 0))
    return pl.pallas_call(
        body, name=name, grid=(s // ts,),
        in_specs=[pl.BlockSpec((ts, CONV_A), lambda i: (i, 0)), tile,
                  pl.BlockSpec((ts, GATE_W), lambda i: (i, GATE_OFF // GATE_W)),
                  _resident(wa.shape), _resident(wb.shape), _resident(wo.shape), tile, _row(d)],
        out_specs=[tile, tile, tile, tile],
        out_shape=[jax.ShapeDtypeStruct((s, d), BF16)] * 3 + [jax.ShapeDtypeStruct((s, d), F32)],
        compiler_params=_params(("parallel",)),
    )(ya, yb, proj, wa, wb, wo, h, g)


def _merge_bwd(dh, g, mix, pa, pb, proj, wa, wb, wo, *, name, ts=256):
    s, d = dh.shape

    def body(dh_ref, g_ref, mix_ref, pa_ref, pb_ref, gt_ref, wa_ref, wb_ref, wo_ref,
             dmix_ref, mg_ref, dpa_ref, dpb_ref, dgt_ref, dya_ref, dyb_ref, dg_ref):
        dh_v = dh_ref[...]
        dmix = (g_ref[...] * dh_v).astype(BF16)
        dmix_ref[...] = dmix

        @pl.when(pl.program_id(0) == 0)
        def _():
            dg_ref[...] = jnp.zeros_like(dg_ref)

        dg_ref[...] += jnp.sum(dh_v * mix_ref[...].astype(F32), axis=0, keepdims=True)
        dmerged = _dot_nt(dmix, wo_ref[...])
        pa = pa_ref[...].astype(F32)
        pb = pb_ref[...].astype(F32)
        sa = _sigmoid(gt_ref[:, :d].astype(F32))
        sb = _sigmoid(gt_ref[:, d:].astype(F32))
        mg_ref[...] = (sa * pa + sb * pb).astype(BF16)
        dpa = (dmerged * sa).astype(BF16)
        dpb = (dmerged * sb).astype(BF16)
        dpa_ref[...] = dpa
        dpb_ref[...] = dpb
        dgt_ref[:, :d] = (dmerged * pa * sa * (1.0 - sa)).astype(BF16)
        dgt_ref[:, d:] = (dmerged * pb * sb * (1.0 - sb)).astype(BF16)
        dya_ref[...] = _dot_nt(dpa, wa_ref[...])
        dyb_ref[...] = _dot_nt(dpb, wb_ref[...])

    tile = pl.BlockSpec((ts, d), lambda i: (i, 0))
    return pl.pallas_call(
        body, name=name, grid=(s // ts,),
        in_specs=[tile, _row(d), tile, tile, tile,
                  pl.BlockSpec((ts, GATE_W), lambda i: (i, GATE_OFF // GATE_W)),
                  _resident(wa.shape), _resident(wb.shape), _resident(wo.shape)],
        out_specs=[tile, tile, tile, tile, pl.BlockSpec((ts, GATE_W), lambda i: (i, 0)),
                   pl.BlockSpec((ts, CONV_A), lambda i: (i, 0)), tile, _row(d)],
        out_shape=[jax.ShapeDtypeStruct((s, d), BF16)] * 4
                  + [jax.ShapeDtypeStruct((s, GATE_W), BF16), jax.ShapeDtypeStruct((s, CONV_A), F32),
                     jax.ShapeDtypeStruct((s, d), F32), jax.ShapeDtypeStruct((1, d), F32)],
        compiler_params=_params(("arbitrary",)),
    )(dh, g, mix, pa, pb, proj, wa, wb, wo)


def _final_fwd_bwd(h, nw, target, *, name, ts=512):
    s, d = h.shape

    def body(h_ref, nw_ref, t_ref, loss_ref, dh_ref, dnw_ref):
        x = h_ref[...]
        w = nw_ref[...]
        r = lax.rsqrt(jnp.mean(x * x, axis=-1, keepdims=True) + EPS)
        xhat = x * r
        e = xhat * w - t_ref[...]
        part = 0.5 * jnp.sum(jnp.mean(e * e, axis=-1, keepdims=True))
        dy = e * (1.0 / d)
        dxhat = dy * w
        dh_ref[...] = r * (dxhat - xhat * jnp.mean(dxhat * xhat, axis=-1, keepdims=True))

        @pl.when(pl.program_id(0) == 0)
        def _():
            loss_ref[...] = jnp.zeros_like(loss_ref)
            dnw_ref[...] = jnp.zeros_like(dnw_ref)

        loss_ref[...] += jnp.broadcast_to(part, loss_ref.shape)
        dnw_ref[...] += jnp.sum(dy * xhat, axis=0, keepdims=True)

    tile = pl.BlockSpec((ts, d), lambda i: (i, 0))
    return pl.pallas_call(
        body, name=name, grid=(s // ts,),
        in_specs=[tile, _row(d), tile],
        out_specs=[_row(128), tile, _row(d)],
        out_shape=[jax.ShapeDtypeStruct((1, 128), F32), jax.ShapeDtypeStruct((s, d), F32),
                   jax.ShapeDtypeStruct((1, d), F32)],
        compiler_params=_params(("arbitrary",)),
    )(h, nw, target)


EXT = 8


def _l2norm_heads_bwd(act, dout, scale):
    outs = []
    for hd in range(HEADS):
        cols = slice(hd * HEAD_DIM, (hd + 1) * HEAD_DIM)
        seg = act[:, cols]
        nrm = lax.rsqrt(jnp.sum(seg * seg, axis=-1, keepdims=True) + EPS)
        yhat = seg * nrm
        dsg = dout[:, cols]
        outs.append((scale * nrm) * (dsg - yhat * jnp.sum(yhat * dsg, axis=-1, keepdims=True)))
    return jnp.concatenate(outs, axis=-1)


def _conv_bwd(dq2, dk2, dv2, dya, proj, conv_dn, conv_a, *, name, ts=256, carry=None):
    s = proj.shape[0]
    hd = HEADS * HEAD_DIM
    nt = s // ts
    te = ts + 2 * EXT
    kdn, ka = conv_dn.shape[0], conv_a.shape[0]

    def body(*refs):
        (qp_ref, qc_ref, qn_ref, ap_ref, ac_ref, an_ref) = refs[0:6]
        d3 = refs[6:24]
        (yp_ref, yc_ref, yn_ref, wdn_ref, wa_ref) = refs[24:29]
        (dqkv_ref, da_ref, dwdn_ref, dwa_ref) = refs[29:33]
        xs_ref, dps_ref, xa_ref, dca_ref = refs[33:37]
        i = pl.program_id(0)
        first, last = i == 0, i == nt - 1

        @pl.when(first)
        def _():
            dwdn_ref[...] = jnp.zeros_like(dwdn_ref)
            dwa_ref[...] = jnp.zeros_like(dwa_ref)

        rowe = lax.broadcasted_iota(jnp.int32, (te, 1), 0)
        inside = ~((first & (rowe < EXT)) | (last & (rowe >= EXT + ts)))
        _fill_halo(xs_ref, qp_ref, qc_ref, qn_ref, first, last)
        wdn = wdn_ref[...]
        for part in range(3):
            cols = slice(part * hd, (part + 1) * hd)
            pre = _dwconv_rows(xs_ref, wdn[:, cols], HALO - EXT - 2, te, cols)
            sg = _sigmoid(pre)
            act = pre * sg
            pf, cf, nf, pb, cb, nb = d3[6 * part:6 * part + 6]
            dout = jnp.concatenate([pf[...] + pb[...], cf[...] + cb[...], nf[...] + nb[...]], axis=0)
            if part == 0:
                dact = _l2norm_heads_bwd(act, dout, Q_SCALE)
            elif part == 1:
                dact = _l2norm_heads_bwd(act, dout, 1.0)
            else:
                dact = dout
            dpre = jnp.where(inside, dact * (sg * (1.0 + pre * (1.0 - sg))), 0.0)
            dps_ref[:, cols] = dpre
            acc = wdn[0:1, cols] * dps_ref[EXT + 2:EXT + 2 + ts, cols]
            for tap in range(1, kdn):
                acc = acc + wdn[tap:tap + 1, cols] * dps_ref[EXT + 2 - tap:EXT + 2 - tap + ts, cols]
            dqkv_ref[:, cols] = acc.astype(BF16)
            dcur = dps_ref[EXT:EXT + ts, cols]
            for tap in range(kdn):
                dwdn_ref[tap:tap + 1, cols] += jnp.sum(
                    dcur * xs_ref[HALO - 2 + tap:HALO - 2 + tap + ts, cols], axis=0, keepdims=True)

        cv = lambda r: r[:, CONV_A:2 * CONV_A].astype(F32) * r[:, 2 * CONV_A:].astype(F32)
        _fill_halo(xa_ref, ap_ref, ac_ref, an_ref, first, last, fn=cv)
        wa = wa_ref[...]
        gate_b = jnp.concatenate([ap_ref[HALO - EXT:, 0:CONV_A], ac_ref[:, 0:CONV_A], an_ref[0:EXT, 0:CONV_A]],
                                 axis=0).astype(F32)
        dya_e = jnp.concatenate([yp_ref[...], yc_ref[...], yn_ref[...]], axis=0)
        dca_ref[...] = jnp.where(inside, dya_e * gate_b, 0.0)
        conv = _dwconv_rows(xa_ref, wa, HALO - 1, ts, slice(0, CONV_A))
        acc = wa[0:1, :] * dca_ref[EXT + 1:EXT + 1 + ts, :]
        for tap in range(1, ka):
            acc = acc + wa[tap:tap + 1, :] * dca_ref[EXT + 1 - tap:EXT + 1 - tap + ts, :]
        gc = ac_ref[:, CONV_A:2 * CONV_A].astype(F32)
        val = ac_ref[:, 2 * CONV_A:].astype(F32)
        da_ref[:, 0:CONV_A] = (yc_ref[...] * conv).astype(BF16)
        da_ref[:, CONV_A:2 * CONV_A] = (acc * val).astype(BF16)
        da_ref[:, 2 * CONV_A:] = (acc * gc).astype(BF16)
        dcur = dca_ref[EXT:EXT + ts, :]
        for tap in range(ka):
            dwa_ref[tap:tap + 1, :] += jnp.sum(
                dcur * xa_ref[HALO - 1 + tap:HALO - 1 + tap + ts, :], axis=0, keepdims=True)

    cot = [arr for pair in (dq2, dk2, dv2) for arr in pair for _ in range(3)]
    return _launch(
        body, carry, (proj, proj, proj, proj, proj, proj, *cot, dya, dya, dya, conv_dn, conv_a),
        name=name, grid=(nt,),
        in_specs=(_halo_specs(ts, QKV_W, 0, s) + _halo_specs(ts, A_W, A_OFF // A_W, s)
                  + _halo_specs(ts, hd, 0, s, rows=EXT) * 6 + _halo_specs(ts, CONV_A, 0, s, rows=EXT)
                  + [_resident(conv_dn.shape), _resident(conv_a.shape)]),
        out_specs=[pl.BlockSpec((ts, QKV_W), lambda i: (i, 0)), pl.BlockSpec((ts, A_W), lambda i: (i, 0)),
                   pl.BlockSpec((8, QKV_W), lambda i: (0, 0)), pl.BlockSpec((8, CONV_A), lambda i: (0, 0))],
        out_shape=[jax.ShapeDtypeStruct((s, QKV_W), BF16), jax.ShapeDtypeStruct((s, A_W), BF16),
                   jax.ShapeDtypeStruct((8, QKV_W), F32), jax.ShapeDtypeStruct((8, CONV_A), F32)],
        scratch_shapes=[pltpu.VMEM((ts + 2 * HALO, QKV_W), F32), pltpu.VMEM((te, QKV_W), F32),
                        pltpu.VMEM((ts + 2 * HALO, CONV_A), F32), pltpu.VMEM((te, CONV_A), F32)],
        sem=("arbitrary",))


IN_A = (0, 1536)
IN_QKV = (1536, 4608)
IN_Z = (4608, 5632)
IN_BA = 5632
IN_GATE = (5664, 7712)
IN_COLS = 7712
G_REPL = 4


def _split_w_in(w_in):
    sl = lambda ab: w_in[:, ab[0]:ab[1]]
    w_main = jnp.concatenate([sl(IN_QKV), sl(IN_Z), sl(IN_GATE), sl(IN_A)], axis=1)
    blocks = []
    for d in range(2):
        beta = w_in[:, IN_BA + 8 * d:IN_BA + 8 * d + 8]
        alpha = w_in[:, IN_BA + 16 + 8 * d:IN_BA + 24 + 8 * d]
        pad = jnp.zeros((w_in.shape[0], BA_W - 8 - 8 * G_REPL), w_in.dtype)
        blocks += [beta] + [alpha] * G_REPL + [pad]
    return w_main, jnp.concatenate(blocks, axis=1)


def _merge_dw_in(dw_qkv, dw_z, dw_gate, dw_a, dw_ba):
    ba = [dw_ba[:, 0:8], dw_ba[:, BA_W:BA_W + 8], dw_ba[:, 8:16], dw_ba[:, BA_W + 8:BA_W + 16]]
    return jnp.concatenate([dw_a, dw_qkv, dw_z] + ba + [dw_gate], axis=1)


def _decay_rows(a_log_fwd, dt_bias_fwd, a_log_bwd, dt_bias_bwd):
    def rows(f, b):
        out = []
        for vec in (f, b):
            vec = vec.reshape(HEADS)
            out.append(jnp.concatenate([jnp.zeros((8,), F32)] + [vec] * G_REPL
                                       + [jnp.zeros((BA_W - 8 - 8 * G_REPL,), F32)])[None])
        return jnp.stack(out)
    return rows(a_log_fwd, a_log_bwd), rows(dt_bias_fwd, dt_bias_bwd)


def _local_step(x, target, mod9, wt, comm):
    s, d = x.shape
    n = s // CHUNK
    wt = dict(wt)
    sh1, sc1, g1, sh2, sc2, g2, sh3, sc3, g3 = [mod9[i:i + 1] for i in range(9)]
    alog, dtb = _decay_rows(wt["a_log_fwd"], wt["dt_bias_fwd"], wt["a_log_bwd"], wt["dt_bias_bwd"])

    (u1, a1, b1, f1), got = comm.gather(
        ["w_ffn1_down", "w_in"],
        lambda c: _ffn_up_fwd(x, wt["norm_ffn1"], sc1, sh1, wt["w_ffn1_up"], name="ffn1_up", carry=c))
    wt.update(got)
    w_main, w_ba = _split_w_in(wt["w_in"])
    y1, h1 = _ffn_down_fwd(f1, wt["w_ffn1_down"], x, g1, name="ffn1_down")
    (u2, proj, ba), got = comm.gather(
        ["w_a_out", "w_b_out", "w_out", "w_ffn2_up", "w_ffn2_down"],
        lambda c: _in_proj_fwd(h1, wt["norm_mix"], sc2, sh2, w_main, w_ba, name="in_proj", carry=c))
    wt.update(got)
    scal = _scal_fwd(ba, alog, dtb, name="scal_fwd")
    grow = scal[:, :, L_G:L_G + 8].reshape(2, n, CHUNK, HEADS).transpose(0, 1, 3, 2).reshape(
        2, n, N_GROUPS, GROWS)
    q, k, v, ya = _conv_fwd(proj, wt["conv_dn"], wt["conv_a"], name="conv_fwd")
    o2, states, tinv, vn = _delta_fwd(q, k, v, scal, grow, name="delta_fwd")
    yb = _gate_norm_fwd(o2, proj, wt["dn_norm"], name="gate_norm_fwd")
    pa, pb, mix, h2 = _merge_fwd(ya, yb, proj, wt["w_a_out"], wt["w_b_out"], wt["w_out"], h1, g2,
                                 name="merge_fwd")
    u3, a3, b3, f3 = _ffn_up_fwd(h2, wt["norm_ffn2"], sc3, sh3, wt["w_ffn2_up"], name="ffn2_up")
    y3, h3 = _ffn_down_fwd(f3, wt["w_ffn2_down"], h2, g3, name="ffn2_down")
    loss, dh3, dnorm_final = _final_fwd_bwd(h3, wt["norm_final"], target, name="final")

    dy3, dab3, dg3 = _ffn_bwd_act(dh3, g3, y3, a3, b3, wt["w_ffn2_down"], name="ffn2_bwd_act")
    dh2, dn3, dsc3, dsh3 = _norm_mod_matmul_bwd([(dab3, wt["w_ffn2_up"])], h2, wt["norm_ffn2"], sc3, dh3,
                                                name="ffn2_bwd_up")
    gw = {}
    gw["w_ffn2_up"] = _matmul_tn(u3, dab3, name="dw_ffn2_up", tm=1024, tn=2816)
    gw["w_ffn2_down"] = _matmul_tn(f3, dy3, name="dw_ffn2_down", tm=1408, tn=1024)

    dmix, merged, dpa, dpb, dgates, dya, dyb, dg2 = _merge_bwd(
        dh2, g2, mix, pa, pb, proj, wt["w_a_out"], wt["w_b_out"], wt["w_out"], name="merge_bwd")
    gw["w_out"] = _matmul_tn(merged, dmix, name="dw_out", tm=1024, tn=1024)
    gw["w_a_out"] = _matmul_tn(ya, dpa, name="dw_a_out", tm=512, tn=1024)
    gw["w_b_out"] = _matmul_tn(yb, dpb, name="dw_b_out", tm=1024, tn=1024)
    do, dz, ddn = _gate_norm_bwd(dyb, o2, proj, wt["dn_norm"], name="gate_norm_bwd")
    recv = {}
    (dq2, dk2, dv2, dscal, drow), got = comm.scatter(
        {nm: gw.pop(nm) for nm in ("w_ffn2_up", "w_ffn2_down")},
        lambda c: _delta_bwd(q, k, v, scal, grow, states, tinv, vn, do, name="delta_bwd", carry=c))
    recv.update(got)
    drow_p = jnp.pad(jnp.stack(drow).reshape(2, n, HEADS, CHUNK).transpose(0, 1, 3, 2).reshape(2, s, HEADS),
                     ((0, 0), (0, 0), (L_G, BA_W - L_G - HEADS)))
    dba, dalog, ddtb = _scal_bwd(jnp.stack(dscal), drow_p, ba, alog, dtb, name="scal_bwd")
    (dqkv, dbr_a, dconv_dn, dconv_a), got = comm.scatter(
        {nm: gw.pop(nm) for nm in ("w_out", "w_a_out", "w_b_out")},
        lambda c: _conv_bwd(dq2, dk2, dv2, dya, proj, wt["conv_dn"], wt["conv_a"], name="conv_bwd", carry=c))
    recv.update(got)
    dw_in = _merge_dw_in(
        _matmul_tn(u2, dqkv, name="dw_in_qkv", tm=1024, tn=3072),
        _matmul_tn(u2, dz, name="dw_in_z", tm=1024, tn=1024),
        _matmul_tn(u2, dgates, name="dw_in_gate", tm=1024, tn=2048),
        _matmul_tn(u2, dbr_a, name="dw_in_a", tm=1024, tn=1536),
        _matmul_tn(u2, dba, name="dw_in_ba", tm=1024, tn=2 * BA_W))
    (dh1, dn2, dsc2, dsh2), got = comm.scatter(
        {"w_in": dw_in},
        lambda c: _norm_mod_matmul_bwd(
            [(dqkv, w_main, 0), (dz, w_main, Z_OFF // Z_W), (dgates, w_main, GATE_OFF // GATE_W),
             (dbr_a, w_main, A_OFF // A_W), (dba, w_ba)],
            h1, wt["norm_mix"], sc2, dh2, name="in_proj_bwd", carry=c))
    recv.update(got)

    dy1, dab1, dg1 = _ffn_bwd_act(dh1, g1, y1, a1, b1, wt["w_ffn1_down"], name="ffn1_bwd_act")
    dw_down1 = _matmul_tn(f1, dy1, name="dw_ffn1_down", tm=1408, tn=1024)
    dw_up1, got = comm.scatter(
        {"w_ffn1_down": dw_down1},
        lambda c: _matmul_tn(u1, dab1, name="dw_ffn1_up", tm=1024, tn=2816, carry=c))
    recv.update(got)
    (dx, dn1, dsc1, dsh1), got = comm.scatter(
        {"w_ffn1_up": dw_up1},
        lambda c: _norm_mod_matmul_bwd([(dab1, wt["w_ffn1_up"])], x, wt["norm_ffn1"], sc1, dh1,
                                       name="ffn1_bwd_up", carry=c))
    recv.update(got)

    small = {
        "mod": jnp.concatenate([dsh1, dsc1, dg1, dsh2, dsc2, dg2, dsh3, dsc3, dg3], axis=1),
        "norm_ffn1": dn1, "norm_mix": dn2, "norm_ffn2": dn3, "norm_final": dnorm_final,
        "a_log_fwd": dalog[0, :, L_G:L_G + 8], "dt_bias_fwd": ddtb[0, :, L_G:L_G + 8],
        "a_log_bwd": dalog[1, :, L_G:L_G + 8], "dt_bias_bwd": ddtb[1, :, L_G:L_G + 8],
        "dn_norm": ddn,
        "conv_a": dconv_a[0:3].reshape(1, -1), "conv_dn": dconv_dn[0:5].reshape(1, -1),
    }
    return loss, dx, recv, small


def _full_weight(name, g):
    if name in COL_SHARDED + CONV_SHARDED:
        return g.transpose(1, 0, 2).reshape(g.shape[1], -1)
    return g.reshape(-1, g.shape[-1])


def _grad_pieces(name, g):
    g = g.astype(BF16)
    if name in COL_SHARDED:
        return g.reshape(g.shape[0], N_DEV, -1).transpose(1, 0, 2)
    return g.reshape(N_DEV, -1, g.shape[-1])


class _MeshComm:
    def __init__(self, shards):
        self.shards = shards

    def _run(self, xs, carrier, name, gather):
        if carrier is None:
            return None, _exchange(xs, name=name, gather=gather)
        return carrier((xs, gather))

    def gather(self, names, carrier=None, name=None):
        outs, got = self._run([self.shards[nm] for nm in names], carrier, name, True)
        return outs, {nm: _full_weight(nm, g) for nm, g in zip(names, got)}

    def scatter(self, grads, carrier=None, name=None):
        names = list(grads)
        outs, got = self._run([_grad_pieces(nm, grads[nm]) for nm in names], carrier, name, False)
        return outs, dict(zip(names, got))


def _mod_fwd(c_all, w_ada, *, name):
    def body(c_ref, w_ref, o_ref):
        cv = c_ref[...]
        o_ref[...] = _dot3(cv * _sigmoid(cv), w_ref[...])

    return pl.pallas_call(
        body, name=name, out_shape=jax.ShapeDtypeStruct((c_all.shape[0], w_ada.shape[1]), F32),
        compiler_params=_params(),
    )(c_all, w_ada)


def _adamw_math(w, g, m, v):
    m_new = ADAM_B1 * m + (1.0 - ADAM_B1) * g
    v_new = ADAM_B2 * v + (1.0 - ADAM_B2) * (g * g)
    m_hat = m_new / (1.0 - ADAM_B1 ** ADAM_STEP)
    v_hat = v_new / (1.0 - ADAM_B2 ** ADAM_STEP)
    delta = -ADAM_LR * (m_hat / (jnp.sqrt(v_hat) + ADAM_EPS) + ADAM_WD * w)
    return delta, m_new, v_new


def _reduce_adamw(pieces, w, m, v, *, name, tr):
    r, c = w.shape

    def body(p_ref, w_ref, m_ref, v_ref, g_ref, d_ref, mo_ref, vo_ref):
        g = p_ref[0].astype(F32)
        for src in range(1, N_DEV):
            g = g + p_ref[src].astype(F32)
        g_ref[...] = g
        d_ref[...], mo_ref[...], vo_ref[...] = _adamw_math(w_ref[...], g, m_ref[...], v_ref[...])

    tile = pl.BlockSpec((tr, c), lambda i: (i, 0))
    return pl.pallas_call(
        body, name=name, grid=(r // tr,),
        in_specs=[pl.BlockSpec((N_DEV, tr, c), lambda i: (0, i, 0)), tile, tile, tile],
        out_specs=[tile] * 4, out_shape=[jax.ShapeDtypeStruct((r, c), F32)] * 4,
        compiler_params=_params(("parallel",)),
    )(pieces, w, m, v)


def _ada_grad_adamw(c_all_t, dmod_cols, w, m, v, *, name, tr=256):
    r, c = w.shape

    def body(c_ref, dm_ref, w_ref, m_ref, v_ref, g_ref, d_ref, mo_ref, vo_ref):
        cv = c_ref[...]
        act = cv * _sigmoid(cv)
        dm = dm_ref[...]
        g = act[:, 0:1] * dm[0:1, :]
        for b in range(1, N_DEV):
            g = g + act[:, b:b + 1] * dm[b:b + 1, :]
        g_ref[...] = g
        d_ref[...], mo_ref[...], vo_ref[...] = _adamw_math(w_ref[...], g, m_ref[...], v_ref[...])

    tile = pl.BlockSpec((tr, c), lambda i: (i, 0))
    return pl.pallas_call(
        body, name=name, grid=(r // tr,),
        in_specs=[pl.BlockSpec((tr, N_DEV), lambda i: (i, 0)), pl.BlockSpec((N_DEV, c), lambda i: (0, 0)),
                  tile, tile, tile],
        out_specs=[tile] * 4, out_shape=[jax.ShapeDtypeStruct((r, c), F32)] * 4,
        compiler_params=_params(("parallel",)),
    )(c_all_t, dmod_cols, w, m, v)


def _sum_rows(parts, *, name):
    def body(p_ref, o_ref):
        acc = p_ref[0:1, :]
        for src in range(1, N_DEV):
            acc = acc + p_ref[src:src + 1, :]
        o_ref[...] = acc

    return pl.pallas_call(
        body, name=name, out_shape=jax.ShapeDtypeStruct((1, parts.shape[1]), F32), compiler_params=_params(),
    )(parts)


def _adamw_rows(g, w, m, v, *, name):
    def body(g_ref, w_ref, m_ref, v_ref, d_ref, mo_ref, vo_ref):
        d_ref[...], mo_ref[...], vo_ref[...] = _adamw_math(w_ref[...], g_ref[...], m_ref[...], v_ref[...])

    return pl.pallas_call(
        body, name=name, out_shape=[jax.ShapeDtypeStruct(g.shape, F32)] * 3, compiler_params=_params(),
    )(g, w, m, v)


WEIGHTS = ["w_ada", "b_ada", "norm_ffn1", "w_ffn1_up", "w_ffn1_down", "norm_mix", "w_in", "conv_a", "conv_dn",
           "a_log_fwd", "dt_bias_fwd", "a_log_bwd", "dt_bias_bwd", "dn_norm", "w_a_out", "w_b_out", "w_out",
           "norm_ffn2", "w_ffn2_up", "w_ffn2_down", "norm_final"]
COL_SHARDED = ["w_ffn1_up", "w_in", "w_a_out", "w_ffn2_up"]
ROW_SHARDED = ["w_ffn1_down", "w_b_out", "w_out", "w_ffn2_down"]
CONV_SHARDED = ["conv_a", "conv_dn"]
REPLICATED = ["b_ada", "norm_ffn1", "norm_mix", "a_log_fwd", "dt_bias_fwd", "a_log_bwd", "dt_bias_bwd",
              "dn_norm", "norm_ffn2", "norm_final"]
SMALL_ORDER = ["mod", "norm_ffn1", "norm_mix", "norm_ffn2", "norm_final", "a_log_fwd", "dt_bias_fwd",
               "a_log_bwd", "dt_bias_bwd", "dn_norm", "conv_a", "conv_dn"]
REDUCE_ROWS = {"w_ffn1_up": 256, "w_in": 256, "w_a_out": 256, "w_ffn2_up": 256,
               "w_ffn1_down": 176, "w_b_out": 128, "w_out": 128, "w_ffn2_down": 176}


def _pad_lanes(row):
    pad = (-row.shape[1]) % 128
    return jnp.pad(row, ((0, 0), (0, pad)))


def kernel(x, c, w_ada, b_ada, norm_ffn1, w_ffn1_up, w_ffn1_down, norm_mix, w_in, conv_a, conv_dn, a_log_fwd, dt_bias_fwd, a_log_bwd, dt_bias_bwd, dn_norm, w_a_out, w_b_out, w_out, norm_ffn2, w_ffn2_up, w_ffn2_down, norm_final, loss_target, m_w_ada, m_b_ada, m_norm_ffn1, m_w_ffn1_up, m_w_ffn1_down, m_norm_mix, m_w_in, m_conv_a, m_conv_dn, m_a_log_fwd, m_dt_bias_fwd, m_a_log_bwd, m_dt_bias_bwd, m_dn_norm, m_w_a_out, m_w_b_out, m_w_out, m_norm_ffn2, m_w_ffn2_up, m_w_ffn2_down, m_norm_final, v_w_ada, v_b_ada, v_norm_ffn1, v_w_ffn1_up, v_w_ffn1_down, v_norm_mix, v_w_in, v_conv_a, v_conv_dn, v_a_log_fwd, v_dt_bias_fwd, v_a_log_bwd, v_dt_bias_bwd, v_dn_norm, v_w_a_out, v_w_b_out, v_w_out, v_norm_ffn2, v_w_ffn2_up, v_w_ffn2_down, v_norm_final):
    args = dict(locals())
    w_loc = {n: args[n] for n in WEIGHTS}
    m_loc = {n: args["m_" + n] for n in WEIGHTS}
    v_loc = {n: args["v_" + n] for n in WEIGHTS}
    me = _flat_index(_my_position())
    d_model = x.shape[-1]

    big = COL_SHARDED + ROW_SHARDED
    shards = {n: w_loc[n][0].astype(BF16) for n in big}
    shards.update({n: w_loc[n][0] for n in CONV_SHARDED})
    shards["c"] = c
    comm = _MeshComm(shards)
    wt = comm.gather(["c", "conv_a", "conv_dn", "w_ffn1_up"], name="gather_first")[1]
    c_all = wt.pop("c")
    for n in REPLICATED[1:]:
        wt[n] = w_loc[n].reshape(1, -1)

    mod_cols = _mod_fwd(c_all, w_ada[0], name="mod_fwd")
    mod_all = _exchange([mod_cols], name="gather_mod", gather=True)[0]
    mod_mine = lax.dynamic_index_in_dim(mod_all, me, axis=1, keepdims=False).reshape(1, -1) + b_ada
    mod9 = mod_mine.reshape(9, d_model)

    loss_loc, dx, recv, small = _local_step(x[0], loss_target[0], mod9, wt, comm)
    loss = lax.psum(loss_loc[0, 0], MESH_AXES)

    res = {}
    for n in big:
        res[n] = _reduce_adamw(recv[n], w_loc[n][0], m_loc[n][0], v_loc[n][0], name="adamw_" + n,
                               tr=REDUCE_ROWS[n])

    packed = _pad_lanes(jnp.concatenate([small[n].reshape(1, -1) for n in SMALL_ORDER], axis=1))
    parts = _exchange([packed], name="gather_small", gather=True)[0].reshape(N_DEV, -1)
    total = _sum_rows(parts, name="sum_small")
    off = 0
    gsmall = {}
    for n in SMALL_ORDER:
        size = small[n].size
        gsmall[n] = total[:, off:off + size]
        off += size
    dmod_all = parts[:, 0:9 * d_model]
    ada_cols = w_ada.shape[-1]
    dmod_cols = lax.dynamic_slice_in_dim(dmod_all, me * ada_cols, ada_cols, axis=1)
    res["w_ada"] = _ada_grad_adamw(c_all.T, dmod_cols, w_ada[0], m_w_ada[0], v_w_ada[0], name="adamw_w_ada")
    g_rows = {"b_ada": gsmall["mod"]}
    for n in REPLICATED[1:]:
        g_rows[n] = gsmall[n]
    for n in CONV_SHARDED:
        taps, width = w_loc[n].shape[1], w_loc[n].shape[2]
        full = gsmall[n].reshape(taps, -1)
        g_rows[n] = lax.dynamic_slice_in_dim(full, me * width, width, axis=1).reshape(1, -1)
    row_names = REPLICATED + CONV_SHARDED
    cat = lambda src: _pad_lanes(jnp.concatenate([src[n].reshape(1, -1) for n in row_names], axis=1))
    g_cat = cat(g_rows)
    d_cat, m_cat, v_cat = _adamw_rows(g_cat, cat(w_loc), cat(m_loc), cat(v_loc), name="adamw_small")
    off = 0
    for n in row_names:
        size = w_loc[n].size
        res[n] = tuple(t[:, off:off + size] for t in (g_cat, d_cat, m_cat, v_cat))
        off += size

    outs = [loss, dx[None]]
    for kind in range(4):
        for n in WEIGHTS:
            outs.append(res[n][kind].reshape(w_loc[n].shape))
    return tuple(outs)
```
